```python
import math
import jax, jax.numpy as jnp
from jax import lax
import numpy as np

D_MODEL = 2048
BATCH = 8
SEQ = 8192
DEPTH = 4

N_MIXERS = 3
N_LAYERS_A = (DEPTH + 2) // 3
N_LAYERS_B = (DEPTH + 1) // 3
N_LAYERS_C = DEPTH // 3
RMS_EPS = 1e-6

ATTN_HEAD_DIM = 64
ATTN_HEADS = D_MODEL // ATTN_HEAD_DIM
ATTN_KV_HEADS = ATTN_HEADS // 8
ATTN_GROUP = ATTN_HEADS // ATTN_KV_HEADS
WINDOW = 128
ATTN_BLOCK = 128
N_HALO = WINDOW // ATTN_BLOCK
ATTN_KEYS = (2 * N_HALO + 1) * ATTN_BLOCK

SSM_GROUP_CH = 16
SSM_GROUPS = D_MODEL // SSM_GROUP_CH
SSM_STATE = 64
SSM_DIRS = 2
DT_MIN = 1e-3
DT_MAX = 1e-1

MLA_HEADS = 16
MLA_Q_LORA = 512
MLA_KV_LORA = 512
MLA_NOPE = 128
MLA_ROPE = 64
MLA_V = 128
MLA_BLOCK = 128
ROPE_THETA = 10000.0

D_FF = 5632
CONV_WIDTH = 3

kernel_name = "hybrid_swa_s5_mla_convffn_encoder"


def _rmsnorm(x, g):
    xf = x.astype(jnp.float32)
    y = xf * lax.rsqrt(jnp.mean(xf * xf, axis=-1, keepdims=True) + RMS_EPS)
    return (y * g.astype(jnp.float32)).astype(x.dtype)


def _alibi_slopes(n_heads):
    return (2.0 ** (-8.0 * np.arange(1, n_heads + 1) / n_heads)).astype(np.float32)


def _windowed_gqa_alibi_sink(h, w_qkv, w_o, sink):
    B, S, _ = h.shape
    nb = S // ATTN_BLOCK
    H, KV, G, dh = ATTN_HEADS, ATTN_KV_HEADS, ATTN_GROUP, ATTN_HEAD_DIM
    qkv = h @ w_qkv
    q, k, v = jnp.split(qkv, [H * dh, (H + KV) * dh], axis=-1)
    q = q.reshape(B, nb, ATTN_BLOCK, KV, G, dh)
    pad = N_HALO * ATTN_BLOCK
    padw = ((0, 0), (pad, pad), (0, 0), (0, 0))
    kp = jnp.pad(k.reshape(B, S, KV, dh), padw).reshape(B, nb + 2 * N_HALO, ATTN_BLOCK, KV, dh)
    vp = jnp.pad(v.reshape(B, S, KV, dh), padw).reshape(B, nb + 2 * N_HALO, ATTN_BLOCK, KV, dh)
    kb = jnp.concatenate([kp[:, o:o + nb] for o in range(2 * N_HALO + 1)], axis=2)
    vb = jnp.concatenate([vp[:, o:o + nb] for o in range(2 * N_HALO + 1)], axis=2)
    s = jnp.einsum('bnqkgd,bnskd->bnkgqs', q, kb,
                   preferred_element_type=jnp.float32) * (dh ** -0.5)
    rel = jnp.arange(ATTN_KEYS)[None, :] - pad - jnp.arange(ATTN_BLOCK)[:, None]
    key_idx = jnp.arange(nb)[:, None] * ATTN_BLOCK - pad + jnp.arange(ATTN_KEYS)[None, :]
    valid = (jnp.abs(rel) <= WINDOW)[None] & ((key_idx >= 0) & (key_idx < S))[:, None, :]
    slopes = jnp.asarray(_alibi_slopes(H)).reshape(KV, G)
    bias = -slopes[:, :, None, None] * jnp.abs(rel).astype(jnp.float32)
    s = jnp.where(valid[None, :, None, None], s + bias, -jnp.inf)
    sink_l = sink.astype(jnp.float32).reshape(KV, G)[:, :, None, None]
    m = jnp.maximum(s.max(axis=-1, keepdims=True), sink_l)
    p = jnp.exp(s - m)
    p = p / (p.sum(axis=-1, keepdims=True) + jnp.exp(sink_l - m))
    o = jnp.einsum('bnkgqs,bnskd->bnqkgd', p.astype(vb.dtype), vb)
    return o.reshape(B, S, H * dh) @ w_o


def _complex_recurrence_combine(e1, e2):
    a1r, a1i, b1r, b1i = e1
    a2r, a2i, b2r, b2i = e2
    return (a2r * a1r - a2i * a1i,
            a2r * a1i + a2i * a1r,
            a2r * b1r - a2i * b1i + b2r,
            a2r * b1i + a2i * b1r + b2i)


def _s5_direction(ug, a_re, a_im, log_step, b_re, b_im, c_re, c_im, reverse):
    S = ug.shape[1]
    a_re = a_re.astype(jnp.float32)
    a_im = a_im.astype(jnp.float32)
    step = jnp.exp(log_step.astype(jnp.float32))[:, None]
    mag = jnp.exp(step * a_re)
    lb_re = mag * jnp.cos(step * a_im)
    lb_im = mag * jnp.sin(step * a_im)
    n_re, n_im = lb_re - 1.0, lb_im
    den = a_re * a_re + a_im * a_im
    coef_re = (n_re * a_re + n_im * a_im) / den
    coef_im = (n_im * a_re - n_re * a_im) / den
    b_re = b_re.astype(jnp.float32)
    b_im = b_im.astype(jnp.float32)
    bb_re = coef_re[..., None] * b_re - coef_im[..., None] * b_im
    bb_im = coef_re[..., None] * b_im + coef_im[..., None] * b_re
    bu_re = jnp.einsum('bsgc,gnc->bsgn', ug, bb_re)
    bu_im = jnp.einsum('bsgc,gnc->bsgn', ug, bb_im)
    lam_re = jnp.broadcast_to(lb_re, (1, S) + lb_re.shape)
    lam_im = jnp.broadcast_to(lb_im, (1, S) + lb_im.shape)
    _, _, x_re, x_im = lax.associative_scan(
        _complex_recurrence_combine, (lam_re, lam_im, bu_re, bu_im), reverse=reverse, axis=1)
    return (jnp.einsum('bsgn,gcn->bsgc', x_re, c_re.astype(jnp.float32))
            - jnp.einsum('bsgn,gcn->bsgc', x_im, c_im.astype(jnp.float32)))


def _s5_bidirectional_glu(h, a_re, a_im, log_step, b_re, b_im, c_re, c_im, d_skip, w_glu, b_glu):
    B, S, D = h.shape
    u = h.astype(jnp.float32)
    ug = u.reshape(B, S, SSM_GROUPS, SSM_GROUP_CH)
    y = d_skip.astype(jnp.float32) * u
    for dirn in range(SSM_DIRS):
        y = y + _s5_direction(ug, a_re[dirn], a_im[dirn], log_step[dirn], b_re[dirn], b_im[dirn],
                              c_re[dirn], c_im[dirn], reverse=(dirn == 1)).reshape(B, S, D)
    z = jax.nn.gelu(y)
    out = z * jax.nn.sigmoid(z @ w_glu.astype(jnp.float32) + b_glu.astype(jnp.float32))
    return out.astype(h.dtype)


def _rope(x, cos, sin):
    half = x.shape[-1] // 2
    x1, x2 = x[..., :half], x[..., half:]
    return jnp.concatenate([x1 * cos - x2 * sin, x2 * cos + x1 * sin], axis=-1).astype(x.dtype)


def _mla(h, w_dqkv, q_norm, kv_norm, w_uq, w_ukv, w_o):
    B, S, _ = h.shape
    H = MLA_HEADS
    nb = S // MLA_BLOCK
    d = h @ w_dqkv
    c_q, c_kv, k_rope = jnp.split(d, [MLA_Q_LORA, MLA_Q_LORA + MLA_KV_LORA], axis=-1)
    c_q = _rmsnorm(c_q, q_norm)
    c_kv = _rmsnorm(c_kv, kv_norm)
    q = (c_q @ w_uq).reshape(B, S, H, MLA_NOPE + MLA_ROPE)
    q_nope, q_rope = jnp.split(q, [MLA_NOPE], axis=-1)
    kv = (c_kv @ w_ukv).reshape(B, S, H, MLA_NOPE + MLA_V)
    k_nope, v = jnp.split(kv, [MLA_NOPE], axis=-1)
    half = MLA_ROPE // 2
    pos = jnp.arange(S, dtype=jnp.float32)
    inv = ROPE_THETA ** (-jnp.arange(half, dtype=jnp.float32) / half)
    ang = pos[:, None] * inv[None, :]
    cos, sin = jnp.cos(ang), jnp.sin(ang)
    q_rope = _rope(q_rope, cos[:, None, :], sin[:, None, :])
    k_rope = _rope(k_rope, cos, sin)
    scale = (MLA_NOPE + MLA_ROPE) ** -0.5

    def q_block(args):
        qn, qr = args
        s = (jnp.einsum('bqhd,bshd->bhqs', qn, k_nope, preferred_element_type=jnp.float32)
             + jnp.einsum('bqhd,bsd->bhqs', qr, k_rope, preferred_element_type=jnp.float32)) * scale
        p = jax.nn.softmax(s, axis=-1)
        return jnp.einsum('bhqs,bshd->bqhd', p.astype(v.dtype), v)

    qn_b = q_nope.reshape(B, nb, MLA_BLOCK, H, MLA_NOPE).swapaxes(0, 1)
    qr_b = q_rope.reshape(B, nb, MLA_BLOCK, H, MLA_ROPE).swapaxes(0, 1)
    o = lax.map(q_block, (qn_b, qr_b))
    o = o.swapaxes(0, 1).reshape(B, S, H * MLA_V)
    return o @ w_o


def _conv_ffn(h, w_up, conv_w, conv_b, w_down):
    S = h.shape[1]
    u = h @ w_up
    r = CONV_WIDTH // 2
    up = jnp.pad(u, ((0, 0), (r, r), (0, 0)))
    c = conv_b + sum(conv_w[t] * up[:, t:t + S] for t in range(CONV_WIDTH))
    gate, val = jnp.split(c, 2, axis=-1)
    return (jax.nn.silu(gate) * val) @ w_down


def _fwd_setup_inputs(seed: int = 0) -> dict:
    key = jax.random.key(seed)
    ks = iter(jax.random.split(key, 32))
    f32 = jnp.float32

    def nrm(shape, scale):
        return jax.random.normal(next(ks), shape, f32) * scale

    D = D_MODEL
    attn_qkv_w = (ATTN_HEADS + 2 * ATTN_KV_HEADS) * ATTN_HEAD_DIM
    ssm_shape = (N_LAYERS_B, SSM_DIRS, SSM_GROUPS, SSM_STATE)
    n_idx = jnp.arange(SSM_STATE, dtype=f32)
    return {
        "x": nrm((BATCH, SEQ, D), 1.0),
        "mix_norm": 1.0 + nrm((DEPTH, D), 0.02),
        "ffn_norm": 1.0 + nrm((DEPTH, D), 0.02),
        "final_norm": 1.0 + nrm((D,), 0.02),
        "attn_w_qkv": nrm((N_LAYERS_A, D, attn_qkv_w), D ** -0.5),
        "attn_w_o": nrm((N_LAYERS_A, ATTN_HEADS * ATTN_HEAD_DIM, D), (ATTN_HEADS * ATTN_HEAD_DIM) ** -0.5),
        "attn_sink": nrm((N_LAYERS_A, ATTN_HEADS), 0.5),
        "ssm_a_re": -0.5 + nrm(ssm_shape, 0.01),
        "ssm_a_im": jnp.pi * n_idx + nrm(ssm_shape, 0.01),
        "ssm_log_step": jax.random.uniform(next(ks), (N_LAYERS_B, SSM_DIRS, SSM_GROUPS), f32,
                                           math.log(DT_MIN), math.log(DT_MAX)),
        "ssm_b_re": nrm(ssm_shape + (SSM_GROUP_CH,), (2 * SSM_GROUP_CH) ** -0.5),
        "ssm_b_im": nrm(ssm_shape + (SSM_GROUP_CH,), (2 * SSM_GROUP_CH) ** -0.5),
        "ssm_c_re": nrm((N_LAYERS_B, SSM_DIRS, SSM_GROUPS, SSM_GROUP_CH, SSM_STATE), SSM_STATE ** -0.5),
        "ssm_c_im": nrm((N_LAYERS_B, SSM_DIRS, SSM_GROUPS, SSM_GROUP_CH, SSM_STATE), SSM_STATE ** -0.5),
        "ssm_d": nrm((N_LAYERS_B, D), 1.0),
        "ssm_w_glu": nrm((N_LAYERS_B, D, D), D ** -0.5),
        "ssm_b_glu": nrm((N_LAYERS_B, D), 0.01),
        "mla_w_dqkv": nrm((N_LAYERS_C, D, MLA_Q_LORA + MLA_KV_LORA + MLA_ROPE), D ** -0.5),
        "mla_q_norm": 1.0 + nrm((N_LAYERS_C, MLA_Q_LORA), 0.02),
        "mla_kv_norm": 1.0 + nrm((N_LAYERS_C, MLA_KV_LORA), 0.02),
        "mla_w_uq": nrm((N_LAYERS_C, MLA_Q_LORA, MLA_HEADS * (MLA_NOPE + MLA_ROPE)), MLA_Q_LORA ** -0.5),
        "mla_w_ukv": nrm((N_LAYERS_C, MLA_KV_LORA, MLA_HEADS * (MLA_NOPE + MLA_V)), MLA_KV_LORA ** -0.5),
        "mla_w_o": nrm((N_LAYERS_C, MLA_HEADS * MLA_V, D), (MLA_HEADS * MLA_V) ** -0.5),
        "ffn_w_up": nrm((DEPTH, D, 2 * D_FF), D ** -0.5),
        "ffn_conv_w": nrm((DEPTH, CONV_WIDTH, 2 * D_FF), CONV_WIDTH ** -0.5),
        "ffn_conv_b": nrm((DEPTH, 2 * D_FF), 0.01),
        "ffn_w_down": nrm((DEPTH, D_FF, D), D_FF ** -0.5),
    }


def _fwd_reference(x, mix_norm, ffn_norm, final_norm, attn_w_qkv, attn_w_o, attn_sink,
              ssm_a_re, ssm_a_im, ssm_log_step, ssm_b_re, ssm_b_im, ssm_c_re, ssm_c_im,
              ssm_d, ssm_w_glu, ssm_b_glu, mla_w_dqkv, mla_q_norm, mla_kv_norm, mla_w_uq,
              mla_w_ukv, mla_w_o, ffn_w_up, ffn_conv_w, ffn_conv_b, ffn_w_down):
    for i in range(DEPTH):
        kind = i % N_MIXERS
        j = i // N_MIXERS
        h = _rmsnorm(x, mix_norm[i])
        if kind == 0:
            h = _windowed_gqa_alibi_sink(h, attn_w_qkv[j], attn_w_o[j], attn_sink[j])
        elif kind == 1:
            h = _s5_bidirectional_glu(h, ssm_a_re[j], ssm_a_im[j], ssm_log_step[j], ssm_b_re[j],
                                      ssm_b_im[j], ssm_c_re[j], ssm_c_im[j], ssm_d[j],
                                      ssm_w_glu[j], ssm_b_glu[j])
        else:
            h = _mla(h, mla_w_dqkv[j], mla_q_norm[j], mla_kv_norm[j], mla_w_uq[j],
                     mla_w_ukv[j], mla_w_o[j])
        x = x + h
        x = x + _conv_ffn(_rmsnorm(x, ffn_norm[i]), ffn_w_up[i], ffn_conv_w[i],
                          ffn_conv_b[i], ffn_w_down[i])
    return _rmsnorm(x, final_norm)


import jax as _jax
import jax.numpy as _jnp

TWIN_FORMAT = 'train_step'
FWD_PARAMS = ['x', 'mix_norm', 'ffn_norm', 'final_norm', 'attn_w_qkv', 'attn_w_o', 'attn_sink', 'ssm_a_re', 'ssm_a_im', 'ssm_log_step', 'ssm_b_re', 'ssm_b_im', 'ssm_c_re', 'ssm_c_im', 'ssm_d', 'ssm_w_glu', 'ssm_b_glu', 'mla_w_dqkv', 'mla_q_norm', 'mla_kv_norm', 'mla_w_uq', 'mla_w_ukv', 'mla_w_o', 'ffn_w_up', 'ffn_conv_w', 'ffn_conv_b', 'ffn_w_down']
TWIN_WEIGHTS = ['mix_norm', 'ffn_norm', 'final_norm', 'attn_w_qkv', 'attn_w_o', 'attn_sink', 'ssm_a_re', 'ssm_a_im', 'ssm_log_step', 'ssm_b_re', 'ssm_b_im', 'ssm_c_re', 'ssm_c_im', 'ssm_d', 'ssm_w_glu', 'ssm_b_glu', 'mla_w_dqkv', 'mla_q_norm', 'mla_kv_norm', 'mla_w_uq', 'mla_w_ukv', 'mla_w_o', 'ffn_w_up', 'ffn_conv_w', 'ffn_conv_b', 'ffn_w_down']
TWIN_DIFF_INPUT = 'x'
TWIN_INPUTS = ['x', 'mix_norm', 'ffn_norm', 'final_norm', 'attn_w_qkv', 'attn_w_o', 'attn_sink', 'ssm_a_re', 'ssm_a_im', 'ssm_log_step', 'ssm_b_re', 'ssm_b_im', 'ssm_c_re', 'ssm_c_im', 'ssm_d', 'ssm_w_glu', 'ssm_b_glu', 'mla_w_dqkv', 'mla_q_norm', 'mla_kv_norm', 'mla_w_uq', 'mla_w_ukv', 'mla_w_o', 'ffn_w_up', 'ffn_conv_w', 'ffn_conv_b', 'ffn_w_down', 'loss_target', 'm_mix_norm', 'm_ffn_norm', 'm_final_norm', 'm_attn_w_qkv', 'm_attn_w_o', 'm_attn_sink', 'm_ssm_a_re', 'm_ssm_a_im', 'm_ssm_log_step', 'm_ssm_b_re', 'm_ssm_b_im', 'm_ssm_c_re', 'm_ssm_c_im', 'm_ssm_d', 'm_ssm_w_glu', 'm_ssm_b_glu', 'm_mla_w_dqkv', 'm_mla_q_norm', 'm_mla_kv_norm', 'm_mla_w_uq', 'm_mla_w_ukv', 'm_mla_w_o', 'm_ffn_w_up', 'm_ffn_conv_w', 'm_ffn_conv_b', 'm_ffn_w_down', 'v_mix_norm', 'v_ffn_norm', 'v_final_norm', 'v_attn_w_qkv', 'v_attn_w_o', 'v_attn_sink', 'v_ssm_a_re', 'v_ssm_a_im', 'v_ssm_log_step', 'v_ssm_b_re', 'v_ssm_b_im', 'v_ssm_c_re', 'v_ssm_c_im', 'v_ssm_d', 'v_ssm_w_glu', 'v_ssm_b_glu', 'v_mla_w_dqkv', 'v_mla_q_norm', 'v_mla_kv_norm', 'v_mla_w_uq', 'v_mla_w_ukv', 'v_mla_w_o', 'v_ffn_w_up', 'v_ffn_conv_w', 'v_ffn_conv_b', 'v_ffn_w_down']
TWIN_OUTPUTS = ['loss', 'grad_x', 'grad_mix_norm', 'grad_ffn_norm', 'grad_final_norm', 'grad_attn_w_qkv', 'grad_attn_w_o', 'grad_attn_sink', 'grad_ssm_a_re', 'grad_ssm_a_im', 'grad_ssm_log_step', 'grad_ssm_b_re', 'grad_ssm_b_im', 'grad_ssm_c_re', 'grad_ssm_c_im', 'grad_ssm_d', 'grad_ssm_w_glu', 'grad_ssm_b_glu', 'grad_mla_w_dqkv', 'grad_mla_q_norm', 'grad_mla_kv_norm', 'grad_mla_w_uq', 'grad_mla_w_ukv', 'grad_mla_w_o', 'grad_ffn_w_up', 'grad_ffn_conv_w', 'grad_ffn_conv_b', 'grad_ffn_w_down', 'delta_mix_norm', 'delta_ffn_norm', 'delta_final_norm', 'delta_attn_w_qkv', 'delta_attn_w_o', 'delta_attn_sink', 'delta_ssm_a_re', 'delta_ssm_a_im', 'delta_ssm_log_step', 'delta_ssm_b_re', 'delta_ssm_b_im', 'delta_ssm_c_re', 'delta_ssm_c_im', 'delta_ssm_d', 'delta_ssm_w_glu', 'delta_ssm_b_glu', 'delta_mla_w_dqkv', 'delta_mla_q_norm', 'delta_mla_kv_norm', 'delta_mla_w_uq', 'delta_mla_w_ukv', 'delta_mla_w_o', 'delta_ffn_w_up', 'delta_ffn_conv_w', 'delta_ffn_conv_b', 'delta_ffn_w_down', 'new_m_mix_norm', 'new_m_ffn_norm', 'new_m_final_norm', 'new_m_attn_w_qkv', 'new_m_attn_w_o', 'new_m_attn_sink', 'new_m_ssm_a_re', 'new_m_ssm_a_im', 'new_m_ssm_log_step', 'new_m_ssm_b_re', 'new_m_ssm_b_im', 'new_m_ssm_c_re', 'new_m_ssm_c_im', 'new_m_ssm_d', 'new_m_ssm_w_glu', 'new_m_ssm_b_glu', 'new_m_mla_w_dqkv', 'new_m_mla_q_norm', 'new_m_mla_kv_norm', 'new_m_mla_w_uq', 'new_m_mla_w_ukv', 'new_m_mla_w_o', 'new_m_ffn_w_up', 'new_m_ffn_conv_w', 'new_m_ffn_conv_b', 'new_m_ffn_w_down', 'new_v_mix_norm', 'new_v_ffn_norm', 'new_v_final_norm', 'new_v_attn_w_qkv', 'new_v_attn_w_o', 'new_v_attn_sink', 'new_v_ssm_a_re', 'new_v_ssm_a_im', 'new_v_ssm_log_step', 'new_v_ssm_b_re', 'new_v_ssm_b_im', 'new_v_ssm_c_re', 'new_v_ssm_c_im', 'new_v_ssm_d', 'new_v_ssm_w_glu', 'new_v_ssm_b_glu', 'new_v_mla_w_dqkv', 'new_v_mla_q_norm', 'new_v_mla_kv_norm', 'new_v_mla_w_uq', 'new_v_mla_w_ukv', 'new_v_mla_w_o', 'new_v_ffn_w_up', 'new_v_ffn_conv_w', 'new_v_ffn_conv_b', 'new_v_ffn_w_down']
TWIN_LEAF_KINDS = {'loss': 'loss', 'grad_x': 'grad_x', 'grad_mix_norm': 'grad_w', 'grad_ffn_norm': 'grad_w', 'grad_final_norm': 'grad_w', 'grad_attn_w_qkv': 'grad_w', 'grad_attn_w_o': 'grad_w', 'grad_attn_sink': 'grad_w', 'grad_ssm_a_re': 'grad_w', 'grad_ssm_a_im': 'grad_w', 'grad_ssm_log_step': 'grad_w', 'grad_ssm_b_re': 'grad_w', 'grad_ssm_b_im': 'grad_w', 'grad_ssm_c_re': 'grad_w', 'grad_ssm_c_im': 'grad_w', 'grad_ssm_d': 'grad_w', 'grad_ssm_w_glu': 'grad_w', 'grad_ssm_b_glu': 'grad_w', 'grad_mla_w_dqkv': 'grad_w', 'grad_mla_q_norm': 'grad_w', 'grad_mla_kv_norm': 'grad_w', 'grad_mla_w_uq': 'grad_w', 'grad_mla_w_ukv': 'grad_w', 'grad_mla_w_o': 'grad_w', 'grad_ffn_w_up': 'grad_w', 'grad_ffn_conv_w': 'grad_w', 'grad_ffn_conv_b': 'grad_w', 'grad_ffn_w_down': 'grad_w', 'delta_mix_norm': 'delta_w', 'delta_ffn_norm': 'delta_w', 'delta_final_norm': 'delta_w', 'delta_attn_w_qkv': 'delta_w', 'delta_attn_w_o': 'delta_w', 'delta_attn_sink': 'delta_w', 'delta_ssm_a_re': 'delta_w', 'delta_ssm_a_im': 'delta_w', 'delta_ssm_log_step': 'delta_w', 'delta_ssm_b_re': 'delta_w', 'delta_ssm_b_im': 'delta_w', 'delta_ssm_c_re': 'delta_w', 'delta_ssm_c_im': 'delta_w', 'delta_ssm_d': 'delta_w', 'delta_ssm_w_glu': 'delta_w', 'delta_ssm_b_glu': 'delta_w', 'delta_mla_w_dqkv': 'delta_w', 'delta_mla_q_norm': 'delta_w', 'delta_mla_kv_norm': 'delta_w', 'delta_mla_w_uq': 'delta_w', 'delta_mla_w_ukv': 'delta_w', 'delta_mla_w_o': 'delta_w', 'delta_ffn_w_up': 'delta_w', 'delta_ffn_conv_w': 'delta_w', 'delta_ffn_conv_b': 'delta_w', 'delta_ffn_w_down': 'delta_w', 'new_m_mix_norm': 'new_m', 'new_m_ffn_norm': 'new_m', 'new_m_final_norm': 'new_m', 'new_m_attn_w_qkv': 'new_m', 'new_m_attn_w_o': 'new_m', 'new_m_attn_sink': 'new_m', 'new_m_ssm_a_re': 'new_m', 'new_m_ssm_a_im': 'new_m', 'new_m_ssm_log_step': 'new_m', 'new_m_ssm_b_re': 'new_m', 'new_m_ssm_b_im': 'new_m', 'new_m_ssm_c_re': 'new_m', 'new_m_ssm_c_im': 'new_m', 'new_m_ssm_d': 'new_m', 'new_m_ssm_w_glu': 'new_m', 'new_m_ssm_b_glu': 'new_m', 'new_m_mla_w_dqkv': 'new_m', 'new_m_mla_q_norm': 'new_m', 'new_m_mla_kv_norm': 'new_m', 'new_m_mla_w_uq': 'new_m', 'new_m_mla_w_ukv': 'new_m', 'new_m_mla_w_o': 'new_m', 'new_m_ffn_w_up': 'new_m', 'new_m_ffn_conv_w': 'new_m', 'new_m_ffn_conv_b': 'new_m', 'new_m_ffn_w_down': 'new_m', 'new_v_mix_norm': 'new_v', 'new_v_ffn_norm': 'new_v', 'new_v_final_norm': 'new_v', 'new_v_attn_w_qkv': 'new_v', 'new_v_attn_w_o': 'new_v', 'new_v_attn_sink': 'new_v', 'new_v_ssm_a_re': 'new_v', 'new_v_ssm_a_im': 'new_v', 'new_v_ssm_log_step': 'new_v', 'new_v_ssm_b_re': 'new_v', 'new_v_ssm_b_im': 'new_v', 'new_v_ssm_c_re': 'new_v', 'new_v_ssm_c_im': 'new_v', 'new_v_ssm_d': 'new_v', 'new_v_ssm_w_glu': 'new_v', 'new_v_ssm_b_glu': 'new_v', 'new_v_mla_w_dqkv': 'new_v', 'new_v_mla_q_norm': 'new_v', 'new_v_mla_kv_norm': 'new_v', 'new_v_mla_w_uq': 'new_v', 'new_v_mla_w_ukv': 'new_v', 'new_v_mla_w_o': 'new_v', 'new_v_ffn_w_up': 'new_v', 'new_v_ffn_conv_w': 'new_v', 'new_v_ffn_conv_b': 'new_v', 'new_v_ffn_w_down': 'new_v'}


def _forward(args):
    return _fwd_reference(*[args[k] for k in FWD_PARAMS])


def _output_shape():
    def fwd():
        inp = _fwd_setup_inputs(0)
        return _fwd_reference(*[inp[k] for k in FWD_PARAMS])
    out = _jax.eval_shape(fwd)
    return out.shape, out.dtype

N_MICROBATCH = 1
ADAM_LR = 0.001
ADAM_B1 = 0.9
ADAM_B2 = 0.999
ADAM_EPS = 1e-08
ADAM_WD = 0.01
ADAM_STEP = 10
PER_EXAMPLE_BATCH_AXIS = {'x': 0, 'loss_target': 0}
SHARED_INPUTS = []
_WEIGHT_DTYPES = {'mix_norm': _jnp.float32, 'ffn_norm': _jnp.float32, 'final_norm': _jnp.float32, 'attn_w_qkv': _jnp.float32, 'attn_w_o': _jnp.float32, 'attn_sink': _jnp.float32, 'ssm_a_re': _jnp.float32, 'ssm_a_im': _jnp.float32, 'ssm_log_step': _jnp.float32, 'ssm_b_re': _jnp.float32, 'ssm_b_im': _jnp.float32, 'ssm_c_re': _jnp.float32, 'ssm_c_im': _jnp.float32, 'ssm_d': _jnp.float32, 'ssm_w_glu': _jnp.float32, 'ssm_b_glu': _jnp.float32, 'mla_w_dqkv': _jnp.float32, 'mla_q_norm': _jnp.float32, 'mla_kv_norm': _jnp.float32, 'mla_w_uq': _jnp.float32, 'mla_w_ukv': _jnp.float32, 'mla_w_o': _jnp.float32, 'ffn_w_up': _jnp.float32, 'ffn_conv_w': _jnp.float32, 'ffn_conv_b': _jnp.float32, 'ffn_w_down': _jnp.float32}
MOMENT_SCALE = {'mix_norm': 6.963724e-02, 'ffn_norm': 9.271798e-02, 'final_norm': 3.284498e+01, 'attn_w_qkv': 4.839698e-02, 'attn_w_o': 3.648580e-02, 'attn_sink': 4.688782e-02, 'ssm_a_re': 3.760298e-03, 'ssm_a_im': 3.703195e-03, 'ssm_log_step': 3.526045e+00, 'ssm_b_re': 2.319837e-03, 'ssm_b_im': 2.267142e-03, 'ssm_c_re': 3.242739e-03, 'ssm_c_im': 3.209745e-03, 'ssm_d': 1.054782e-01, 'ssm_w_glu': 1.937526e-02, 'ssm_b_glu': 5.216446e-02, 'mla_w_dqkv': 3.311504e-02, 'mla_q_norm': 2.090473e-02, 'mla_kv_norm': 4.317148e-02, 'mla_w_uq': 8.990760e-03, 'mla_w_ukv': 1.491995e-02, 'mla_w_o': 1.944056e-02, 'ffn_w_up': 3.892329e-02, 'ffn_conv_w': 3.864175e-02, 'ffn_conv_b': 3.891793e-02, 'ffn_w_down': 6.358574e-02}


def _to_microbatches(a, axis):
    t = _jnp.moveaxis(a, axis, 0)
    t = t.reshape((N_MICROBATCH, t.shape[0] // N_MICROBATCH) + t.shape[1:])
    return _jnp.moveaxis(t, 1, axis + 1)


def setup_inputs(seed: int = 0) -> dict:
    inp = _fwd_setup_inputs(seed)
    key = _jax.random.fold_in(_jax.random.key(seed), 7919)
    shape, _ = _output_shape()
    out = dict(inp)
    out["loss_target"] = _jax.random.normal(_jax.random.fold_in(key, 0), shape, _jnp.float32)
    for i, name in enumerate(TWIN_WEIGHTS):
        w = inp[name].astype(_jnp.float32)
        if MOMENT_SCALE is None:
            s = _jnp.sqrt(_jnp.mean(_jnp.square(w)) + 1e-30)
        else:
            s = MOMENT_SCALE[name]
        km, kv = _jax.random.split(_jax.random.fold_in(key, i + 1))
        out[name] = w
        out["m_" + name] = s * _jax.random.normal(km, w.shape, _jnp.float32)
        out["v_" + name] = (s * s) * _jax.random.uniform(kv, w.shape, _jnp.float32, 0.5, 1.5)
    if N_MICROBATCH > 1:
        for name, axis in PER_EXAMPLE_BATCH_AXIS.items():
            out[name] = _to_microbatches(out[name], axis)
    return {'x': out['x'], 'mix_norm': out['mix_norm'], 'ffn_norm': out['ffn_norm'], 'final_norm': out['final_norm'], 'attn_w_qkv': out['attn_w_qkv'], 'attn_w_o': out['attn_w_o'], 'attn_sink': out['attn_sink'], 'ssm_a_re': out['ssm_a_re'], 'ssm_a_im': out['ssm_a_im'], 'ssm_log_step': out['ssm_log_step'], 'ssm_b_re': out['ssm_b_re'], 'ssm_b_im': out['ssm_b_im'], 'ssm_c_re': out['ssm_c_re'], 'ssm_c_im': out['ssm_c_im'], 'ssm_d': out['ssm_d'], 'ssm_w_glu': out['ssm_w_glu'], 'ssm_b_glu': out['ssm_b_glu'], 'mla_w_dqkv': out['mla_w_dqkv'], 'mla_q_norm': out['mla_q_norm'], 'mla_kv_norm': out['mla_kv_norm'], 'mla_w_uq': out['mla_w_uq'], 'mla_w_ukv': out['mla_w_ukv'], 'mla_w_o': out['mla_w_o'], 'ffn_w_up': out['ffn_w_up'], 'ffn_conv_w': out['ffn_conv_w'], 'ffn_conv_b': out['ffn_conv_b'], 'ffn_w_down': out['ffn_w_down'], 'loss_target': out['loss_target'], 'm_mix_norm': out['m_mix_norm'], 'm_ffn_norm': out['m_ffn_norm'], 'm_final_norm': out['m_final_norm'], 'm_attn_w_qkv': out['m_attn_w_qkv'], 'm_attn_w_o': out['m_attn_w_o'], 'm_attn_sink': out['m_attn_sink'], 'm_ssm_a_re': out['m_ssm_a_re'], 'm_ssm_a_im': out['m_ssm_a_im'], 'm_ssm_log_step': out['m_ssm_log_step'], 'm_ssm_b_re': out['m_ssm_b_re'], 'm_ssm_b_im': out['m_ssm_b_im'], 'm_ssm_c_re': out['m_ssm_c_re'], 'm_ssm_c_im': out['m_ssm_c_im'], 'm_ssm_d': out['m_ssm_d'], 'm_ssm_w_glu': out['m_ssm_w_glu'], 'm_ssm_b_glu': out['m_ssm_b_glu'], 'm_mla_w_dqkv': out['m_mla_w_dqkv'], 'm_mla_q_norm': out['m_mla_q_norm'], 'm_mla_kv_norm': out['m_mla_kv_norm'], 'm_mla_w_uq': out['m_mla_w_uq'], 'm_mla_w_ukv': out['m_mla_w_ukv'], 'm_mla_w_o': out['m_mla_w_o'], 'm_ffn_w_up': out['m_ffn_w_up'], 'm_ffn_conv_w': out['m_ffn_conv_w'], 'm_ffn_conv_b': out['m_ffn_conv_b'], 'm_ffn_w_down': out['m_ffn_w_down'], 'v_mix_norm': out['v_mix_norm'], 'v_ffn_norm': out['v_ffn_norm'], 'v_final_norm': out['v_final_norm'], 'v_attn_w_qkv': out['v_attn_w_qkv'], 'v_attn_w_o': out['v_attn_w_o'], 'v_attn_sink': out['v_attn_sink'], 'v_ssm_a_re': out['v_ssm_a_re'], 'v_ssm_a_im': out['v_ssm_a_im'], 'v_ssm_log_step': out['v_ssm_log_step'], 'v_ssm_b_re': out['v_ssm_b_re'], 'v_ssm_b_im': out['v_ssm_b_im'], 'v_ssm_c_re': out['v_ssm_c_re'], 'v_ssm_c_im': out['v_ssm_c_im'], 'v_ssm_d': out['v_ssm_d'], 'v_ssm_w_glu': out['v_ssm_w_glu'], 'v_ssm_b_glu': out['v_ssm_b_glu'], 'v_mla_w_dqkv': out['v_mla_w_dqkv'], 'v_mla_q_norm': out['v_mla_q_norm'], 'v_mla_kv_norm': out['v_mla_kv_norm'], 'v_mla_w_uq': out['v_mla_w_uq'], 'v_mla_w_ukv': out['v_mla_w_ukv'], 'v_mla_w_o': out['v_mla_w_o'], 'v_ffn_w_up': out['v_ffn_w_up'], 'v_ffn_conv_w': out['v_ffn_conv_w'], 'v_ffn_conv_b': out['v_ffn_conv_b'], 'v_ffn_w_down': out['v_ffn_w_down']}


def _loss(weights, diff, rest, loss_target):
    with _jax.named_scope("forward"):
        args = {**rest, TWIN_DIFF_INPUT: diff, **{k: w.astype(_WEIGHT_DTYPES[k]) for k, w in weights.items()}}
        y = _forward(args)
    with _jax.named_scope("loss_head"):
        err = _jnp.square(y.astype(_jnp.float32) - loss_target)
        return 0.5 * _jnp.sum(_jnp.mean(err, axis=-1)) if err.ndim else 0.5 * err


def _adamw(w, g, m, v):
    m = ADAM_B1 * m + (1.0 - ADAM_B1) * g
    v = ADAM_B2 * v + (1.0 - ADAM_B2) * _jnp.square(g)
    m_hat = m / (1.0 - ADAM_B1 ** ADAM_STEP)
    v_hat = v / (1.0 - ADAM_B2 ** ADAM_STEP)
    delta = -ADAM_LR * (m_hat / (_jnp.sqrt(v_hat) + ADAM_EPS) + ADAM_WD * w)
    return delta, m, v


def reference(x, mix_norm, ffn_norm, final_norm, attn_w_qkv, attn_w_o, attn_sink, ssm_a_re, ssm_a_im, ssm_log_step, ssm_b_re, ssm_b_im, ssm_c_re, ssm_c_im, ssm_d, ssm_w_glu, ssm_b_glu, mla_w_dqkv, mla_q_norm, mla_kv_norm, mla_w_uq, mla_w_ukv, mla_w_o, ffn_w_up, ffn_conv_w, ffn_conv_b, ffn_w_down, loss_target, m_mix_norm, m_ffn_norm, m_final_norm, m_attn_w_qkv, m_attn_w_o, m_attn_sink, m_ssm_a_re, m_ssm_a_im, m_ssm_log_step, m_ssm_b_re, m_ssm_b_im, m_ssm_c_re, m_ssm_c_im, m_ssm_d, m_ssm_w_glu, m_ssm_b_glu, m_mla_w_dqkv, m_mla_q_norm, m_mla_kv_norm, m_mla_w_uq, m_mla_w_ukv, m_mla_w_o, m_ffn_w_up, m_ffn_conv_w, m_ffn_conv_b, m_ffn_w_down, v_mix_norm, v_ffn_norm, v_final_norm, v_attn_w_qkv, v_attn_w_o, v_attn_sink, v_ssm_a_re, v_ssm_a_im, v_ssm_log_step, v_ssm_b_re, v_ssm_b_im, v_ssm_c_re, v_ssm_c_im, v_ssm_d, v_ssm_w_glu, v_ssm_b_glu, v_mla_w_dqkv, v_mla_q_norm, v_mla_kv_norm, v_mla_w_uq, v_mla_w_ukv, v_mla_w_o, v_ffn_w_up, v_ffn_conv_w, v_ffn_conv_b, v_ffn_w_down):
    given = dict(x=x, mix_norm=mix_norm, ffn_norm=ffn_norm, final_norm=final_norm, attn_w_qkv=attn_w_qkv, attn_w_o=attn_w_o, attn_sink=attn_sink, ssm_a_re=ssm_a_re, ssm_a_im=ssm_a_im, ssm_log_step=ssm_log_step, ssm_b_re=ssm_b_re, ssm_b_im=ssm_b_im, ssm_c_re=ssm_c_re, ssm_c_im=ssm_c_im, ssm_d=ssm_d, ssm_w_glu=ssm_w_glu, ssm_b_glu=ssm_b_glu, mla_w_dqkv=mla_w_dqkv, mla_q_norm=mla_q_norm, mla_kv_norm=mla_kv_norm, mla_w_uq=mla_w_uq, mla_w_ukv=mla_w_ukv, mla_w_o=mla_w_o, ffn_w_up=ffn_w_up, ffn_conv_w=ffn_conv_w, ffn_conv_b=ffn_conv_b, ffn_w_down=ffn_w_down, loss_target=loss_target, m_mix_norm=m_mix_norm, m_ffn_norm=m_ffn_norm, m_final_norm=m_final_norm, m_attn_w_qkv=m_attn_w_qkv, m_attn_w_o=m_attn_w_o, m_attn_sink=m_attn_sink, m_ssm_a_re=m_ssm_a_re, m_ssm_a_im=m_ssm_a_im, m_ssm_log_step=m_ssm_log_step, m_ssm_b_re=m_ssm_b_re, m_ssm_b_im=m_ssm_b_im, m_ssm_c_re=m_ssm_c_re, m_ssm_c_im=m_ssm_c_im, m_ssm_d=m_ssm_d, m_ssm_w_glu=m_ssm_w_glu, m_ssm_b_glu=m_ssm_b_glu, m_mla_w_dqkv=m_mla_w_dqkv, m_mla_q_norm=m_mla_q_norm, m_mla_kv_norm=m_mla_kv_norm, m_mla_w_uq=m_mla_w_uq, m_mla_w_ukv=m_mla_w_ukv, m_mla_w_o=m_mla_w_o, m_ffn_w_up=m_ffn_w_up, m_ffn_conv_w=m_ffn_conv_w, m_ffn_conv_b=m_ffn_conv_b, m_ffn_w_down=m_ffn_w_down, v_mix_norm=v_mix_norm, v_ffn_norm=v_ffn_norm, v_final_norm=v_final_norm, v_attn_w_qkv=v_attn_w_qkv, v_attn_w_o=v_attn_w_o, v_attn_sink=v_attn_sink, v_ssm_a_re=v_ssm_a_re, v_ssm_a_im=v_ssm_a_im, v_ssm_log_step=v_ssm_log_step, v_ssm_b_re=v_ssm_b_re, v_ssm_b_im=v_ssm_b_im, v_ssm_c_re=v_ssm_c_re, v_ssm_c_im=v_ssm_c_im, v_ssm_d=v_ssm_d, v_ssm_w_glu=v_ssm_w_glu, v_ssm_b_glu=v_ssm_b_glu, v_mla_w_dqkv=v_mla_w_dqkv, v_mla_q_norm=v_mla_q_norm, v_mla_kv_norm=v_mla_kv_norm, v_mla_w_uq=v_mla_w_uq, v_mla_w_ukv=v_mla_w_ukv, v_mla_w_o=v_mla_w_o, v_ffn_w_up=v_ffn_w_up, v_ffn_conv_w=v_ffn_conv_w, v_ffn_conv_b=v_ffn_conv_b, v_ffn_w_down=v_ffn_w_down)
    weights = {n: given[n] for n in TWIN_WEIGHTS}
    shared = {n: given[n] for n in SHARED_INPUTS}
    per_example = {n: given[n] for n in ['x']}
    grad_fn = _jax.value_and_grad(_loss, argnums=(0, 1))

    def one_microbatch(ex, loss_target):
        ex = dict(ex)
        diff = ex.pop(TWIN_DIFF_INPUT)
        return grad_fn(weights, diff, {**shared, **ex}, loss_target)

    if N_MICROBATCH == 1:
        loss, (grad_w, grad_x) = one_microbatch(per_example, given["loss_target"])
    else:
        def body(carry, xs):
            loss_sum, grad_sum = carry
            l_k, (gw_k, gx_k) = one_microbatch(xs[0], xs[1])
            with _jax.named_scope("update"):
                return (loss_sum + l_k, _jax.tree.map(_jnp.add, grad_sum, gw_k)), gx_k

        init = (_jnp.zeros((), _jnp.float32), _jax.tree.map(_jnp.zeros_like, weights))
        (loss, grad_w), grad_x = _jax.lax.scan(body, init, (per_example, given["loss_target"]))
    with _jax.named_scope("update"):
        delta_w, new_m, new_v = {}, {}, {}
        for n in TWIN_WEIGHTS:
            delta_w[n], new_m[n], new_v[n] = _adamw(weights[n], grad_w[n], given["m_" + n], given["v_" + n])
    return (loss, grad_x, *[grad_w[n] for n in TWIN_WEIGHTS], *[delta_w[n] for n in TWIN_WEIGHTS],
            *[new_m[n] for n in TWIN_WEIGHTS], *[new_v[n] for n in TWIN_WEIGHTS])
```

```python
import functools
import math

import numpy as np
import jax
import jax.numpy as jnp
from jax import lax
from jax.experimental import pallas as pl
from jax.experimental.pallas import tpu as pltpu

F32 = jnp.float32
BF16 = jnp.bfloat16
MESH = pl.DeviceIdType.MESH

RMS_EPS = 1e-6
ATTN_DH = 64
ATTN_GROUP = 8
ATTN_BLOCK = 128
SSM_GROUP_CH = 16
SSM_STATE = 64
SSM_SEGMENTS = 8
MLA_HEADS = 16
MLA_LORA = 512
MLA_NOPE = 128
MLA_ROPE = 64
MLA_V = 128
ROPE_THETA = 10000.0
LANES = 128
VMEM_LIMIT = 56 * 1024 * 1024

ADAM_LR = 0.001
ADAM_B1 = 0.9
ADAM_B2 = 0.999
ADAM_EPS = 1e-08
ADAM_WD = 0.01
ADAM_STEP = 10


def _cp(sem=None):
    kw = dict(vmem_limit_bytes=VMEM_LIMIT)
    if sem is not None:
        kw["dimension_semantics"] = sem
    return pltpu.CompilerParams(**kw)


def _pick(n, cands):
    for c in cands:
        if n % c == 0:
            return c
    return n


def _gmm(a, b, *, grid, a_spec, b_spec, o_spec, out_shape, dims, kax, acc_shape, name,
         epi=None, epi_in=(), epi_specs=()):
    nk = grid[kax]
    n_epi = len(epi_in)

    def body(a_ref, b_ref, *rest):
        e_refs = rest[:n_epi]
        o_ref = rest[n_epi]
        acc_ref = rest[n_epi + 1]
        k = pl.program_id(kax)

        @pl.when(k == 0)
        def _():
            acc_ref[...] = jnp.zeros_like(acc_ref)

        acc_ref[...] += lax.dot_general(a_ref[...].astype(BF16), b_ref[...].astype(BF16), dims,
                                        preferred_element_type=F32)

        @pl.when(k == nk - 1)
        def _():
            r = acc_ref[...]
            if epi is not None:
                r = epi(r, *[e[...] for e in e_refs])
            o_ref[...] = r.astype(o_ref.dtype)

    sem = tuple("arbitrary" if i == kax else "parallel" for i in range(len(grid)))
    return pl.pallas_call(
        body, name=name, grid=grid,
        in_specs=[a_spec, b_spec, *epi_specs], out_specs=o_spec, out_shape=out_shape,
        scratch_shapes=[pltpu.VMEM(acc_shape, F32)], compiler_params=_cp(sem),
    )(a, b, *epi_in)


def _mm(a, b, *, ta=False, tb=False, out_dtype=F32, name, epi=None, epi_in=(), tm=None, tn=None, tk=None):
    (K, M) = a.shape if ta else a.shape[::-1]
    (N, K2) = b.shape if tb else b.shape[::-1]
    assert K == K2, (a.shape, b.shape, ta, tb)
    tm = tm or _pick(M, (1024, 512, 256, 128))
    tn = tn or _pick(N, (1024, 512, 256, 128))
    tk = tk or _pick(K, (512, 256, 128))
    grid = (M // tm, N // tn, K // tk)
    a_spec = pl.BlockSpec((tk, tm), lambda i, j, k: (k, i)) if ta else pl.BlockSpec((tm, tk), lambda i, j, k: (i, k))
    b_spec = pl.BlockSpec((tn, tk), lambda i, j, k: (j, k)) if tb else pl.BlockSpec((tk, tn), lambda i, j, k: (k, j))
    dims = (((0 if ta else 1,), (1 if tb else 0,)), ((), ()))
    epi_specs = [pl.BlockSpec((1, tn), lambda i, j, k: (0, j)) if e.shape[0] == 1
                 else pl.BlockSpec((tm, tn), lambda i, j, k: (i, j)) for e in epi_in]
    return _gmm(a, b, grid=grid, a_spec=a_spec, b_spec=b_spec,
                o_spec=pl.BlockSpec((tm, tn), lambda i, j, k: (i, j)),
                out_shape=jax.ShapeDtypeStruct((M, N), out_dtype), dims=dims, kax=2,
                acc_shape=(tm, tn), name=name, epi=epi, epi_in=epi_in, epi_specs=epi_specs)


def _add_epi(r, res):
    return res + r


def _rmsnorm_fwd(x, g, out_dtype, name):
    S, D = x.shape
    tr = _pick(S, (256, 128, 8))

    def body(x_ref, g_ref, o_ref):
        xf = x_ref[...]
        r = lax.rsqrt(jnp.mean(xf * xf, axis=-1, keepdims=True) + RMS_EPS)
        o_ref[...] = ((xf * r) * g_ref[...]).astype(o_ref.dtype)

    return pl.pallas_call(
        body, name=name, grid=(S // tr,),
        in_specs=[pl.BlockSpec((tr, D), lambda i: (i, 0)), pl.BlockSpec((1, D), lambda i: (0, 0))],
        out_specs=pl.BlockSpec((tr, D), lambda i: (i, 0)),
        out_shape=jax.ShapeDtypeStruct((S, D), out_dtype), compiler_params=_cp(("parallel",)),
    )(x, g)


def _rms_bwd_math(xf, g, dy):
    r = lax.rsqrt(jnp.mean(xf * xf, axis=-1, keepdims=True) + RMS_EPS)
    xh = xf * r
    dxh = dy * g
    dx = r * (dxh - xh * jnp.mean(dxh * xh, axis=-1, keepdims=True))
    return dx, jnp.sum(dy * xh, axis=0, keepdims=True)


def _rmsnorm_bwd(x, g, dy, dres, name):
    S, D = x.shape
    tr = _pick(S, (256, 128, 8))

    def body(x_ref, g_ref, dy_ref, dres_ref, dx_ref, dg_ref):
        @pl.when(pl.program_id(0) == 0)
        def _():
            dg_ref[...] = jnp.zeros_like(dg_ref)

        dx, dg = _rms_bwd_math(x_ref[...], g_ref[...], dy_ref[...].astype(F32))
        dx_ref[...] = dres_ref[...] + dx
        dg_ref[...] += dg

    row = pl.BlockSpec((tr, D), lambda i: (i, 0))
    vec = pl.BlockSpec((1, D), lambda i: (0, 0))
    return pl.pallas_call(
        body, name=name, grid=(S // tr,), in_specs=[row, vec, row, row], out_specs=[row, vec],
        out_shape=[jax.ShapeDtypeStruct((S, D), F32), jax.ShapeDtypeStruct((1, D), F32)],
        compiler_params=_cp(("arbitrary",)),
    )(x, g, dy, dres)


def _loss_head(x, g, target, name):
    S, D = x.shape
    tr = _pick(S, (256, 128, 8))

    def body(x_ref, g_ref, t_ref, loss_ref, dx_ref, dg_ref):
        @pl.when(pl.program_id(0) == 0)
        def _():
            dg_ref[...] = jnp.zeros_like(dg_ref)
            loss_ref[...] = jnp.zeros_like(loss_ref)

        xf = x_ref[...]
        gg = g_ref[...]
        r = lax.rsqrt(jnp.mean(xf * xf, axis=-1, keepdims=True) + RMS_EPS)
        e = (xf * r) * gg - t_ref[...]
        loss_ref[...] += 0.5 * jnp.sum(jnp.mean(e * e, axis=-1, keepdims=True), axis=0, keepdims=True)
        dx, dg = _rms_bwd_math(xf, gg, e * (1.0 / D))
        dx_ref[...] = dx
        dg_ref[...] += dg

    row = pl.BlockSpec((tr, D), lambda i: (i, 0))
    vec = pl.BlockSpec((1, D), lambda i: (0, 0))
    one = pl.BlockSpec((1, 1), lambda i: (0, 0))
    return pl.pallas_call(
        body, name=name, grid=(S // tr,), in_specs=[row, vec, row], out_specs=[one, row, vec],
        out_shape=[jax.ShapeDtypeStruct((1, 1), F32), jax.ShapeDtypeStruct((S, D), F32),
                   jax.ShapeDtypeStruct((1, D), F32)],
        compiler_params=_cp(("arbitrary",)),
    )(x, g, target)


HALO = 16


def _shift_rows(main, prev_row, next_row):
    tr = main.shape[0]
    row = lax.broadcasted_iota(jnp.int32, main.shape, 0)
    up = jnp.where(row == 0, prev_row, pltpu.roll(main, 1, 0))
    dn = jnp.where(row == tr - 1, next_row, pltpu.roll(main, tr - 1, 0))
    return up, dn


def _halo_specs(tr, tn, S, col_of):
    hb = tr // HALO
    last = S // HALO - 1
    return [pl.BlockSpec((tr, tn), lambda j, i: (i, col_of(j))),
            pl.BlockSpec((HALO, tn), lambda j, i: (jnp.maximum(i * hb - 1, 0), col_of(j))),
            pl.BlockSpec((HALO, tn), lambda j, i: (jnp.minimum((i + 1) * hb, last), col_of(j)))]


def _halo_rows(main_ref, prev_ref, next_ref, i, n_i):
    main = main_ref[...].astype(F32)
    prev_row = prev_ref[HALO - 1:HALO, :].astype(F32) * (i > 0).astype(F32)
    next_row = next_ref[0:1, :].astype(F32) * (i < n_i - 1).astype(F32)
    up, dn = _shift_rows(main, prev_row, next_row)
    return up, main, dn


def _conv3(w_ref, b_ref, up, mid, dn):
    return b_ref[...] + w_ref[0:1, :] * up + w_ref[1:2, :] * mid + w_ref[2:3, :] * dn


def _ffn_tiles(S, F):
    return _pick(S, (512, 256, 128, 16)), _pick(F, (512, 256, 128))


def _conv_gate_fwd(u, conv_w, conv_b, name):
    S, F2 = u.shape
    F = F2 // 2
    tr, tn = _ffn_tiles(S, F)
    nj, ni = F // tn, S // tr

    def body(gm, gp, gn, vm, vp, vn, wg, wv, bg, bv, o_ref):
        i = pl.program_id(1)
        cg = _conv3(wg, bg, *_halo_rows(gm, gp, gn, i, ni))
        cv = _conv3(wv, bv, *_halo_rows(vm, vp, vn, i, ni))
        o_ref[...] = (cg * jax.nn.sigmoid(cg) * cv).astype(o_ref.dtype)

    wspec = lambda off: pl.BlockSpec((3, tn), lambda j, i: (0, j + off))
    bspec = lambda off: pl.BlockSpec((1, tn), lambda j, i: (0, j + off))
    return pl.pallas_call(
        body, name=name, grid=(nj, ni),
        in_specs=[*_halo_specs(tr, tn, S, lambda j: j), *_halo_specs(tr, tn, S, lambda j: j + nj),
                  wspec(0), wspec(nj), bspec(0), bspec(nj)],
        out_specs=pl.BlockSpec((tr, tn), lambda j, i: (i, j)),
        out_shape=jax.ShapeDtypeStruct((S, F), BF16), compiler_params=_cp(("parallel", "parallel")),
    )(u, u, u, u, u, u, conv_w, conv_w, conv_b, conv_b)


def _conv_gate_bwd(u, da, conv_w, conv_b, name):
    S, F2 = u.shape
    F = F2 // 2
    tr, tn = _ffn_tiles(S, F)
    nj, ni = F // tn, S // tr

    def body(gm, gp, gn, vm, vp, vn, wg, wv, bg, bv, da_ref, dcg_ref, dcv_ref, dwg_ref, dwv_ref, dbg_ref, dbv_ref):
        i = pl.program_id(1)

        @pl.when(i == 0)
        def _():
            for r in (dwg_ref, dwv_ref, dbg_ref, dbv_ref):
                r[...] = jnp.zeros_like(r)

        g_rows = _halo_rows(gm, gp, gn, i, ni)
        v_rows = _halo_rows(vm, vp, vn, i, ni)
        cg = _conv3(wg, bg, *g_rows)
        cv = _conv3(wv, bv, *v_rows)
        sg = jax.nn.sigmoid(cg)
        d = da_ref[...].astype(F32)
        dcv = d * (cg * sg)
        dcg = d * cv * (sg * (1.0 + cg * (1.0 - sg)))
        dcg_ref[...] = dcg.astype(dcg_ref.dtype)
        dcv_ref[...] = dcv.astype(dcv_ref.dtype)
        for t in range(3):
            dwg_ref[t:t + 1, :] += jnp.sum(dcg * g_rows[t], axis=0, keepdims=True)
            dwv_ref[t:t + 1, :] += jnp.sum(dcv * v_rows[t], axis=0, keepdims=True)
        dbg_ref[...] += jnp.sum(dcg, axis=0, keepdims=True)
        dbv_ref[...] += jnp.sum(dcv, axis=0, keepdims=True)

    wspec = lambda off: pl.BlockSpec((3, tn), lambda j, i: (0, j + off))
    bspec = lambda off: pl.BlockSpec((1, tn), lambda j, i: (0, j + off))
    tile = pl.BlockSpec((tr, tn), lambda j, i: (i, j))
    outs = pl.pallas_call(
        body, name=name, grid=(nj, ni),
        in_specs=[*_halo_specs(tr, tn, S, lambda j: j), *_halo_specs(tr, tn, S, lambda j: j + nj),
                  wspec(0), wspec(nj), bspec(0), bspec(nj), tile],
        out_specs=[tile, tile, wspec(0), wspec(0), bspec(0), bspec(0)],
        out_shape=[jax.ShapeDtypeStruct((S, F), BF16), jax.ShapeDtypeStruct((S, F), BF16),
                   jax.ShapeDtypeStruct((3, F), F32), jax.ShapeDtypeStruct((3, F), F32),
                   jax.ShapeDtypeStruct((1, F), F32), jax.ShapeDtypeStruct((1, F), F32)],
        compiler_params=_cp(("parallel", "arbitrary")),
    )(u, u, u, u, u, u, conv_w, conv_w, conv_b, conv_b, da)
    dcg, dcv, dwg, dwv, dbg, dbv = outs
    return dcg, dcv, jnp.concatenate([dwg, dwv], axis=1), jnp.concatenate([dbg, dbv], axis=1)


def _conv_transpose(dc, w, name):
    S, Fx = dc.shape
    tr, tn = _ffn_tiles(S, Fx)
    nj, ni = Fx // tn, S // tr

    def body(m, p, n, w_ref, o_ref):
        up, mid, dn = _halo_rows(m, p, n, pl.program_id(1), ni)
        o_ref[...] = (w_ref[0:1, :] * dn + w_ref[1:2, :] * mid + w_ref[2:3, :] * up).astype(o_ref.dtype)

    return pl.pallas_call(
        body, name=name, grid=(nj, ni),
        in_specs=[*_halo_specs(tr, tn, S, lambda j: j), pl.BlockSpec((3, tn), lambda j, i: (0, j))],
        out_specs=pl.BlockSpec((tr, tn), lambda j, i: (i, j)),
        out_shape=jax.ShapeDtypeStruct((S, Fx), BF16), compiler_params=_cp(("parallel", "parallel")),
    )(dc, dc, dc, w)


def _ffn_fwd(x, norm_g, w_up, conv_w, conv_b, w_down, tag):
    hn = _rmsnorm_fwd(x, norm_g, BF16, f"ffn_norm_{tag}")
    u = _mm(hn, w_up, out_dtype=BF16, name=f"ffn_up_{tag}")
    a = _conv_gate_fwd(u, conv_w, conv_b, f"ffn_gate_{tag}")
    x_new = _mm(a, w_down, name=f"ffn_down_{tag}", epi=_add_epi, epi_in=(x,))
    return x_new, (x, hn, u, a)


def _ffn_bwd(dres, saved, norm_g, w_up, conv_w, conv_b, w_down, tag):
    x, hn, u, a = saved
    da = _mm(dres, w_down, tb=True, out_dtype=BF16, name=f"ffn_da_{tag}")
    dw_down = _mm(a, dres, ta=True, name=f"ffn_dwdown_{tag}")
    dcg, dcv, dconv_w, dconv_b = _conv_gate_bwd(u, da, conv_w, conv_b, f"ffn_gate_bwd_{tag}")
    dc = jnp.concatenate([dcg, dcv], axis=1)
    du = _conv_transpose(dc, conv_w, f"ffn_convt_{tag}")
    dhn = _mm(du, w_up, tb=True, name=f"ffn_dhn_{tag}")
    dw_up = _mm(hn, du, ta=True, name=f"ffn_dwup_{tag}")
    dres, dg = _rmsnorm_bwd(x, norm_g, dhn, dres, f"ffn_norm_bwd_{tag}")
    return dres, (dg, dw_up, dconv_w, dconv_b, dw_down)


ATTN_KEYS = 3 * ATTN_BLOCK


def _attn_window(i, S):
    ks = pl.multiple_of(jnp.clip((i - 1) * ATTN_BLOCK, 0, S - ATTN_KEYS), ATTN_BLOCK)
    qpos = i * ATTN_BLOCK + lax.broadcasted_iota(jnp.int32, (ATTN_BLOCK, ATTN_KEYS), 0)
    kpos = ks + lax.broadcasted_iota(jnp.int32, (ATTN_BLOCK, ATTN_KEYS), 1)
    arel = jnp.abs(kpos - qpos)
    return ks, arel.astype(F32), arel <= ATTN_BLOCK


def _attn_probs(q, k, slope, sink, arel, valid):
    s = lax.dot_general(q, k, (((1,), (1,)), ((), ())), preferred_element_type=F32) * (ATTN_DH ** -0.5)
    s = jnp.where(valid, s - slope * arel, -jnp.inf)
    m = jnp.maximum(jnp.max(s, axis=-1, keepdims=True), sink)
    p = jnp.exp(s - m)
    es = jnp.exp(sink - m)
    inv = 1.0 / (jnp.sum(p, axis=-1, keepdims=True) + es)
    return p * inv, es * inv


def _attn_specs(S, D):
    H = D // ATTN_DH
    KVW = (H // ATTN_GROUP) * ATTN_DH
    q_spec = pl.BlockSpec((ATTN_BLOCK, D), lambda i: (i, 0))
    k_spec = pl.BlockSpec((S, KVW), lambda i: (0, D // KVW))
    v_spec = pl.BlockSpec((S, KVW), lambda i: (0, D // KVW + 1))
    return H, KVW, q_spec, k_spec, v_spec


def _attn_fwd(qkv, sink, name):
    S = qkv.shape[0]
    D = qkv.shape[1] * ATTN_GROUP // (ATTN_GROUP + 2)
    H, KVW, q_spec, k_spec, v_spec = _attn_specs(S, D)

    def body(q_ref, k_ref, v_ref, sink_ref, o_ref):
        ks, arel, valid = _attn_window(pl.program_id(0), S)
        for kvh in range(H // ATTN_GROUP):
            cols = slice(kvh * ATTN_DH, (kvh + 1) * ATTN_DH)
            k = k_ref[pl.ds(ks, ATTN_KEYS), cols]
            v = v_ref[pl.ds(ks, ATTN_KEYS), cols]
            for g in range(ATTN_GROUP):
                h = kvh * ATTN_GROUP + g
                hc = slice(h * ATTN_DH, (h + 1) * ATTN_DH)
                p, _ = _attn_probs(q_ref[:, hc], k, 2.0 ** (-8.0 * (h + 1) / H), sink_ref[h], arel, valid)
                o_ref[:, hc] = jnp.dot(p.astype(BF16), v, preferred_element_type=F32).astype(o_ref.dtype)

    return pl.pallas_call(
        body, name=name, grid=(S // ATTN_BLOCK,),
        in_specs=[q_spec, k_spec, v_spec, pl.BlockSpec(memory_space=pltpu.SMEM)],
        out_specs=q_spec, out_shape=jax.ShapeDtypeStruct((S, D), BF16),
        compiler_params=_cp(("parallel",)),
    )(qkv, qkv, qkv, sink)


def _attn_bwd(qkv, sink, do, name):
    S = qkv.shape[0]
    D = qkv.shape[1] * ATTN_GROUP // (ATTN_GROUP + 2)
    H, KVW, q_spec, k_spec, v_spec = _attn_specs(S, D)
    scale = ATTN_DH ** -0.5

    def body(q_ref, k_ref, v_ref, sink_ref, do_ref, dq_ref, dk_ref, dv_ref, ds_ref):
        @pl.when(pl.program_id(0) == 0)
        def _():
            dk_ref[...] = jnp.zeros_like(dk_ref)
            dv_ref[...] = jnp.zeros_like(dv_ref)
            ds_ref[...] = jnp.zeros_like(ds_ref)

        ks, arel, valid = _attn_window(pl.program_id(0), S)
        rows = pl.ds(ks, ATTN_KEYS)
        for kvh in range(H // ATTN_GROUP):
            cols = slice(kvh * ATTN_DH, (kvh + 1) * ATTN_DH)
            k = k_ref[rows, cols]
            v = v_ref[rows, cols]
            dk = jnp.zeros((ATTN_KEYS, ATTN_DH), F32)
            dv = jnp.zeros((ATTN_KEYS, ATTN_DH), F32)
            for g in range(ATTN_GROUP):
                h = kvh * ATTN_GROUP + g
                hc = slice(h * ATTN_DH, (h + 1) * ATTN_DH)
                q = q_ref[:, hc]
                d_o = do_ref[:, hc]
                p, p_sink = _attn_probs(q, k, 2.0 ** (-8.0 * (h + 1) / H), sink_ref[h], arel, valid)
                dp = lax.dot_general(d_o, v, (((1,), (1,)), ((), ())), preferred_element_type=F32)
                delta = jnp.sum(p * dp, axis=-1, keepdims=True)
                dsc = (p * (dp - delta)).astype(BF16)
                ds_ref[:, h:h + 1] += -p_sink * delta
                dq_ref[:, hc] = (jnp.dot(dsc, k, preferred_element_type=F32) * scale).astype(dq_ref.dtype)
                dk += lax.dot_general(dsc, q, (((0,), (0,)), ((), ())), preferred_element_type=F32)
                dv += lax.dot_general(p.astype(BF16), d_o, (((0,), (0,)), ((), ())), preferred_element_type=F32)
            dk_ref[rows, cols] += dk * scale
            dv_ref[rows, cols] += dv

    kv_out = pl.BlockSpec((S, KVW), lambda i: (0, 0))
    return pl.pallas_call(
        body, name=name, grid=(S // ATTN_BLOCK,),
        in_specs=[q_spec, k_spec, v_spec, pl.BlockSpec(memory_space=pltpu.SMEM), q_spec],
        out_specs=[q_spec, kv_out, kv_out, pl.BlockSpec((ATTN_BLOCK, H), lambda i: (0, 0))],
        out_shape=[jax.ShapeDtypeStruct((S, D), BF16), jax.ShapeDtypeStruct((S, KVW), F32),
                   jax.ShapeDtypeStruct((S, KVW), F32), jax.ShapeDtypeStruct((ATTN_BLOCK, H), F32)],
        compiler_params=_cp(("arbitrary",)),
    )(qkv, qkv, qkv, sink, do)


def _attn_layer_fwd(x, norm_g, w_qkv, w_o, sink, tag):
    hn = _rmsnorm_fwd(x, norm_g, BF16, f"attn_norm_{tag}")
    qkv = _mm(hn, w_qkv, out_dtype=BF16, name=f"attn_qkv_{tag}")
    o = _attn_fwd(qkv, sink, f"attn_core_{tag}")
    x_new = _mm(o, w_o, name=f"attn_out_{tag}", epi=_add_epi, epi_in=(x,))
    return x_new, (x, hn, qkv, o)


def _attn_layer_bwd(dres, saved, norm_g, w_qkv, w_o, sink, tag):
    x, hn, qkv, o = saved
    do = _mm(dres, w_o, tb=True, out_dtype=BF16, name=f"attn_do_{tag}")
    dw_o = _mm(o, dres, ta=True, name=f"attn_dwo_{tag}")
    dq, dk, dv, dsink = _attn_bwd(qkv, sink, do, f"attn_core_bwd_{tag}")
    dqkv = jnp.concatenate([dq, dk.astype(BF16), dv.astype(BF16)], axis=1)
    dhn = _mm(dqkv, w_qkv, tb=True, name=f"attn_dhn_{tag}")
    dw_qkv = _mm(hn, dqkv, ta=True, name=f"attn_dwqkv_{tag}")
    dres, dg = _rmsnorm_bwd(x, norm_g, dhn, dres, f"attn_norm_bwd_{tag}")
    return dres, (dg, dw_qkv, dw_o, jnp.sum(dsink, axis=0))


MLA_W = 2 * LANES
MLA_DPAD = 2 * MLA_LORA + LANES
MLA_SCALE = (MLA_NOPE + MLA_ROPE) ** -0.5


def _rope_tables(S):
    half = MLA_ROPE // 2
    pos = jnp.arange(S, dtype=F32)
    inv = ROPE_THETA ** (-jnp.arange(half, dtype=F32) / half)
    ang = pos[:, None] * inv[None, :]
    cos, sin = jnp.cos(ang), jnp.sin(ang)
    z = jnp.zeros((S, LANES - 2 * half), F32)
    zh = jnp.zeros((S, half), F32)
    return (jnp.concatenate([cos, cos, z], axis=1), jnp.concatenate([-sin, zh, z], axis=1),
            jnp.concatenate([zh, sin, z], axis=1))


def _rope(t, ca, sb, sc):
    return t * ca + pltpu.roll(t, 96, 1) * sb + pltpu.roll(t, 32, 1) * sc


def _rope_t(d, ca, sb, sc):
    return d * ca + pltpu.roll(d * sb, 32, 1) + pltpu.roll(d * sc, 96, 1)


def _rms(xf, g):
    return (xf * lax.rsqrt(jnp.mean(xf * xf, axis=-1, keepdims=True) + RMS_EPS)) * g


def _mla_prep(d, qn, kvn, tabs, name):
    S = d.shape[0]
    tr = _pick(S, (256, 128, 8))
    L = MLA_LORA

    def body(d_ref, qn_ref, kvn_ref, ca, sb, sc, cq_ref, ckv_ref, kr_ref):
        cq_ref[...] = _rms(d_ref[:, :L], qn_ref[...]).astype(BF16)
        ckv_ref[...] = _rms(d_ref[:, L:2 * L], kvn_ref[...]).astype(BF16)
        kr_ref[...] = _rope(d_ref[:, 2 * L:], ca[...], sb[...], sc[...]).astype(BF16)

    row = lambda w: pl.BlockSpec((tr, w), lambda i: (i, 0))
    vec = pl.BlockSpec((1, L), lambda i: (0, 0))
    return pl.pallas_call(
        body, name=name, grid=(S // tr,),
        in_specs=[row(MLA_DPAD), vec, vec, row(LANES), row(LANES), row(LANES)],
        out_specs=[row(L), row(L), row(LANES)],
        out_shape=[jax.ShapeDtypeStruct((S, L), BF16), jax.ShapeDtypeStruct((S, L), BF16),
                   jax.ShapeDtypeStruct((S, LANES), BF16)],
        compiler_params=_cp(("parallel",)),
    )(d, qn, kvn, *tabs)


def _mla_prep_bwd(d, qn, kvn, tabs, dcq, dckv, dkr_h, name):
    S = d.shape[0]
    H = dkr_h.shape[0]
    tr = _pick(S, (256, 128, 8))
    L = MLA_LORA

    def body(d_ref, qn_ref, kvn_ref, ca, sb, sc, dcq_ref, dckv_ref, dkr_ref, dd_ref, dqn_ref, dkvn_ref):
        @pl.when(pl.program_id(0) == 0)
        def _():
            dqn_ref[...] = jnp.zeros_like(dqn_ref)
            dkvn_ref[...] = jnp.zeros_like(dkvn_ref)

        dx, dg = _rms_bwd_math(d_ref[:, :L], qn_ref[...], dcq_ref[...])
        dd_ref[:, :L] = dx.astype(BF16)
        dqn_ref[...] += dg
        dx, dg = _rms_bwd_math(d_ref[:, L:2 * L], kvn_ref[...], dckv_ref[...])
        dd_ref[:, L:2 * L] = dx.astype(BF16)
        dkvn_ref[...] += dg
        dkr = dkr_ref[0]
        for h in range(1, H):
            dkr = dkr + dkr_ref[h]
        dd_ref[:, 2 * L:] = _rope_t(dkr, ca[...], sb[...], sc[...]).astype(BF16)

    row = lambda w: pl.BlockSpec((tr, w), lambda i: (i, 0))
    vec = pl.BlockSpec((1, L), lambda i: (0, 0))
    return pl.pallas_call(
        body, name=name, grid=(S // tr,),
        in_specs=[row(MLA_DPAD), vec, vec, row(LANES), row(LANES), row(LANES), row(L), row(L),
                  pl.BlockSpec((H, tr, LANES), lambda i: (0, i, 0))],
        out_specs=[row(MLA_DPAD), vec, vec],
        out_shape=[jax.ShapeDtypeStruct((S, MLA_DPAD), BF16), jax.ShapeDtypeStruct((1, L), F32),
                   jax.ShapeDtypeStruct((1, L), F32)],
        compiler_params=_cp(("arbitrary",)),
    )(d, qn, kvn, *tabs, dcq, dckv, dkr_h)


def _heads_proj(a, w, out_dtype, name):
    S, K = a.shape
    H, _, n = w.shape
    tm = _pick(S, (1024, 512, 256, 128))
    return _gmm(a, w, grid=(S // tm, H, 1),
                a_spec=pl.BlockSpec((tm, K), lambda m, h, k: (m, 0)),
                b_spec=pl.BlockSpec((None, K, n), lambda m, h, k: (h, 0, 0)),
                o_spec=pl.BlockSpec((None, tm, n), lambda m, h, k: (h, m, 0)),
                out_shape=jax.ShapeDtypeStruct((H, S, n), out_dtype),
                dims=(((1,), (0,)), ((), ())), kax=2, acc_shape=(tm, n), name=name)


def _heads_proj_dx(dy, w, name):
    H, S, n = dy.shape
    K = w.shape[1]
    tm = _pick(S, (1024, 512, 256, 128))
    return _gmm(dy, w, grid=(S // tm, 1, H),
                a_spec=pl.BlockSpec((None, tm, n), lambda m, j, h: (h, m, 0)),
                b_spec=pl.BlockSpec((None, K, n), lambda m, j, h: (h, 0, 0)),
                o_spec=pl.BlockSpec((tm, K), lambda m, j, h: (m, 0)),
                out_shape=jax.ShapeDtypeStruct((S, K), F32),
                dims=(((1,), (1,)), ((), ())), kax=2, acc_shape=(tm, K), name=name)


def _heads_proj_dw(a, dy, name):
    S, K = a.shape
    H, _, n = dy.shape
    tk = _pick(S, (512, 256, 128))
    return _gmm(a, dy, grid=(H, 1, S // tk),
                a_spec=pl.BlockSpec((tk, K), lambda h, j, k: (k, 0)),
                b_spec=pl.BlockSpec((None, tk, n), lambda h, j, k: (h, k, 0)),
                o_spec=pl.BlockSpec((None, K, n), lambda h, j, k: (h, 0, 0)),
                out_shape=jax.ShapeDtypeStruct((H, K, n), F32),
                dims=(((0,), (0,)), ((), ())), kax=2, acc_shape=(K, n), name=name)


def _mla_rope_q(q_ext, tabs, bwd, name):
    H, S, _ = q_ext.shape
    tr = _pick(S, (512, 256, 128, 8))

    def body(q_ref, ca, sb, sc, o_ref):
        o_ref[:, :LANES] = q_ref[:, :LANES].astype(BF16)
        fn = _rope_t if bwd else _rope
        o_ref[:, LANES:] = fn(q_ref[:, LANES:].astype(F32), ca[...], sb[...], sc[...]).astype(BF16)

    blk = pl.BlockSpec((None, tr, MLA_W), lambda i, h: (h, i, 0))
    tab = pl.BlockSpec((tr, LANES), lambda i, h: (i, 0))
    return pl.pallas_call(
        body, name=name, grid=(S // tr, H), in_specs=[blk, tab, tab, tab], out_specs=blk,
        out_shape=jax.ShapeDtypeStruct((H, S, MLA_W), BF16), compiler_params=_cp(("parallel", "parallel")),
    )(q_ext, *tabs)


def _col_to_row(col):
    n = col.shape[0]
    eye = lax.broadcasted_iota(jnp.int32, (n, n), 0) == lax.broadcasted_iota(jnp.int32, (n, n), 1)
    return jnp.sum(jnp.where(eye, col, 0.0), axis=0, keepdims=True)


def _mla_flash_fwd(q, kv, kr, name):
    H, S, _ = q.shape
    tq = _pick(S, (512, 256, 128))
    tk = _pick(S, (512, 256, 128))

    def body(q_ref, kv_ref, kr_ref, o_ref, lse_ref, kbuf):
        @pl.when(pl.program_id(1) == 0)
        def _():
            kbuf[:, :LANES] = kv_ref[:, :LANES]
            kbuf[:, LANES:] = kr_ref[...]

        qv = q_ref[...]

        def step(c, carry):
            m, l, acc = carry
            rows = pl.ds(pl.multiple_of(c * tk, tk), tk)
            s = lax.dot_general(qv, kbuf[rows, :], (((1,), (1,)), ((), ())), preferred_element_type=F32) * MLA_SCALE
            m_new = jnp.maximum(m, jnp.max(s, axis=-1, keepdims=True))
            p = jnp.exp(s - m_new)
            alpha = jnp.exp(m - m_new)
            l = alpha * l + jnp.sum(p, axis=-1, keepdims=True)
            acc = alpha * acc + jnp.dot(p.astype(BF16), kv_ref[rows, LANES:], preferred_element_type=F32)
            return m_new, l, acc

        init = (jnp.full((tq, 1), -jnp.inf, F32), jnp.zeros((tq, 1), F32), jnp.zeros((tq, MLA_V), F32))
        m, l, acc = lax.fori_loop(0, S // tk, step, init)
        o_ref[...] = (acc / l).astype(o_ref.dtype)
        lse_ref[...] = _col_to_row(m + jnp.log(l))

    return pl.pallas_call(
        body, name=name, grid=(H, S // tq),
        in_specs=[pl.BlockSpec((None, tq, MLA_W), lambda h, i: (h, i, 0)),
                  pl.BlockSpec((None, S, MLA_W), lambda h, i: (h, 0, 0)),
                  pl.BlockSpec((S, LANES), lambda h, i: (0, 0))],
        out_specs=[pl.BlockSpec((tq, MLA_V), lambda h, i: (i, h)),
                   pl.BlockSpec((None, 1, tq), lambda h, i: (h, 0, i))],
        out_shape=[jax.ShapeDtypeStruct((S, H * MLA_V), BF16), jax.ShapeDtypeStruct((H, 1, S), F32)],
        scratch_shapes=[pltpu.VMEM((S, MLA_W), BF16)],
        compiler_params=_cp(("parallel", "arbitrary")),
    )(q, kv, kr)


def _mla_delta(o, do, H, name):
    S = o.shape[0]
    tq = _pick(S, (512, 256, 128))

    def body(o_ref, do_ref, d_ref):
        prod = o_ref[...].astype(F32) * do_ref[...].astype(F32)
        d_ref[...] = _col_to_row(jnp.sum(prod, axis=-1, keepdims=True))

    blk = pl.BlockSpec((tq, MLA_V), lambda i, h: (i, h))
    return pl.pallas_call(
        body, name=name, grid=(S // tq, H), in_specs=[blk, blk],
        out_specs=pl.BlockSpec((None, 1, tq), lambda i, h: (h, 0, i)),
        out_shape=jax.ShapeDtypeStruct((H, 1, S), F32), compiler_params=_cp(("parallel", "parallel")),
    )(o, do)


def _mla_flash_bwd(q, kv, kr, do, lse, delta, name):
    H, S, _ = q.shape
    tq = _pick(S, (512, 256, 128))
    tkv = _pick(S, (512, 256, 128))

    def body(q_ref, kv_ref, kr_ref, do_ref, lse_ref, dl_ref, dq_ref, dkv_ref, dkr_ref):
        @pl.when(pl.program_id(1) == 0)
        def _():
            dq_ref[...] = jnp.zeros_like(dq_ref)

        v = kv_ref[:, LANES:]
        k = jnp.concatenate([kv_ref[:, :LANES], kr_ref[...]], axis=1)

        def step(c, carry):
            dk, dv = carry
            start = pl.multiple_of(c * tq, tq)
            rows = pl.ds(start, tq)
            qv = q_ref[rows, :]
            d_o = do_ref[rows, :]
            s_t = lax.dot_general(k, qv, (((1,), (1,)), ((), ())), preferred_element_type=F32) * MLA_SCALE
            p_t = jnp.exp(s_t - lse_ref[:, rows])
            dv = dv + jnp.dot(p_t.astype(BF16), d_o, preferred_element_type=F32)
            dp_t = lax.dot_general(v, d_o, (((1,), (1,)), ((), ())), preferred_element_type=F32)
            ds_t = (p_t * (dp_t - dl_ref[:, rows])).astype(BF16)
            dk = dk + jnp.dot(ds_t, qv, preferred_element_type=F32)
            dq_ref[rows, :] += lax.dot_general(ds_t, k, (((0,), (0,)), ((), ())),
                                               preferred_element_type=F32) * MLA_SCALE
            return dk, dv

        dk, dv = lax.fori_loop(0, S // tq, step, (jnp.zeros((tkv, MLA_W), F32), jnp.zeros((tkv, MLA_V), F32)))
        dkv_ref[:, :LANES] = (dk[:, :LANES] * MLA_SCALE).astype(BF16)
        dkv_ref[:, LANES:] = dv.astype(BF16)
        dkr_ref[...] = dk[:, LANES:] * MLA_SCALE

    stat = pl.BlockSpec((None, 1, S), lambda h, j: (h, 0, 0))
    return pl.pallas_call(
        body, name=name, grid=(H, S // tkv),
        in_specs=[pl.BlockSpec((None, S, MLA_W), lambda h, j: (h, 0, 0)),
                  pl.BlockSpec((None, tkv, MLA_W), lambda h, j: (h, j, 0)),
                  pl.BlockSpec((tkv, LANES), lambda h, j: (j, 0)),
                  pl.BlockSpec((S, MLA_V), lambda h, j: (0, h)), stat, stat],
        out_specs=[pl.BlockSpec((None, S, MLA_W), lambda h, j: (h, 0, 0)),
                   pl.BlockSpec((None, tkv, MLA_W), lambda h, j: (h, j, 0)),
                   pl.BlockSpec((None, tkv, LANES), lambda h, j: (h, j, 0))],
        out_shape=[jax.ShapeDtypeStruct((H, S, MLA_W), F32), jax.ShapeDtypeStruct((H, S, MLA_W), BF16),
                   jax.ShapeDtypeStruct((H, S, LANES), F32)],
        compiler_params=_cp(("parallel", "arbitrary")),
    )(q, kv, kr, do, lse, delta)


def _mla_weights(w_dqkv, w_uq, w_ukv):
    H = MLA_HEADS
    wd = jnp.pad(w_dqkv, ((0, 0), (0, MLA_DPAD - w_dqkv.shape[1])))
    wq = w_uq.reshape(MLA_LORA, H, MLA_NOPE + MLA_ROPE)
    wq = jnp.pad(wq, ((0, 0), (0, 0), (0, MLA_W - wq.shape[2]))).transpose(1, 0, 2)
    wkv = w_ukv.reshape(MLA_LORA, H, MLA_NOPE + MLA_V).transpose(1, 0, 2)
    return wd, wq, wkv


def _mla_layer_fwd(x, norm_g, wd, wq, wkv, w_o, qn, kvn, tag):
    S = x.shape[0]
    tabs = _rope_tables(S)
    hn = _rmsnorm_fwd(x, norm_g, BF16, f"mla_norm_{tag}")
    d = _mm(hn, wd, name=f"mla_down_{tag}")
    cq, ckv, kr = _mla_prep(d, qn, kvn, tabs, f"mla_prep_{tag}")
    q = _mla_rope_q(_heads_proj(cq, wq, F32, f"mla_uq_{tag}"), tabs, False, f"mla_ropeq_{tag}")
    kv = _heads_proj(ckv, wkv, BF16, f"mla_ukv_{tag}")
    o, lse = _mla_flash_fwd(q, kv, kr, f"mla_flash_{tag}")
    x_new = _mm(o, w_o, name=f"mla_out_{tag}", epi=_add_epi, epi_in=(x,))
    return x_new, (x, hn, d, cq, ckv, kr, q, kv, o, lse)


def _mla_layer_bwd(dres, saved, norm_g, wd, wq, wkv, w_o, qn, kvn, tag):
    x, hn, d, cq, ckv, kr, q, kv, o, lse = saved
    S = x.shape[0]
    H = MLA_HEADS
    tabs = _rope_tables(S)
    do = _mm(dres, w_o, tb=True, out_dtype=BF16, name=f"mla_do_{tag}")
    dw_o = _mm(o, dres, ta=True, name=f"mla_dwo_{tag}")
    delta = _mla_delta(o, do, H, f"mla_delta_{tag}")
    dq, dkv, dkr_h = _mla_flash_bwd(q, kv, kr, do, lse, delta, f"mla_flash_bwd_{tag}")
    dq_ext = _mla_rope_q(dq, tabs, True, f"mla_ropeq_bwd_{tag}")
    dwq = _heads_proj_dw(cq, dq_ext, f"mla_dwuq_{tag}")
    dcq = _heads_proj_dx(dq_ext, wq, f"mla_dcq_{tag}")
    dwkv = _heads_proj_dw(ckv, dkv, f"mla_dwukv_{tag}")
    dckv = _heads_proj_dx(dkv, wkv, f"mla_dckv_{tag}")
    dd, dqn, dkvn = _mla_prep_bwd(d, qn, kvn, tabs, dcq, dckv, dkr_h, f"mla_prep_bwd_{tag}")
    dhn = _mm(dd, wd, tb=True, name=f"mla_dhn_{tag}")
    dwd = _mm(hn, dd, ta=True, name=f"mla_dwd_{tag}")
    dres, dg = _rmsnorm_bwd(x, norm_g, dhn, dres, f"mla_norm_bwd_{tag}")
    dw_dqkv = dwd[:, :2 * MLA_LORA + MLA_ROPE]
    dw_uq = dwq.transpose(1, 0, 2)[:, :, :MLA_NOPE + MLA_ROPE].reshape(MLA_LORA, -1)
    dw_ukv = dwkv.transpose(1, 0, 2).reshape(MLA_LORA, -1)
    return dres, (dg, dw_dqkv, dqn, dkvn, dw_uq, dw_ukv, dw_o)


S5_CB = LANES
S5_SB = (S5_CB // SSM_GROUP_CH) * SSM_STATE
S5_ROWS = 1024


def _s5_disc(a_re, a_im, ls, b_re, b_im):
    step = jnp.exp(ls)
    mag = jnp.exp(step * a_re)
    lb_re = mag * jnp.cos(step * a_im)
    lb_im = mag * jnp.sin(step * a_im)
    n_re, n_im = lb_re - 1.0, lb_im
    den = a_re * a_re + a_im * a_im
    coef_re = (n_re * a_re + n_im * a_im) / den
    coef_im = (n_im * a_re - n_re * a_im) / den
    return lb_re, lb_im, coef_re * b_re - coef_im * b_im, coef_re * b_im + coef_im * b_re


def _s5_disc_fwd(a_re, a_im, ls, b_re, b_im, name):
    GN = a_re.shape[-1]

    def body(ar, ai, l, br, bi, o_lr, o_li, o_br, o_bi):
        for o, v in zip((o_lr, o_li, o_br, o_bi), _s5_disc(ar[...], ai[...], l[...], br[...], bi[...])):
            o[...] = v

    vec = pl.BlockSpec((None, 1, GN), lambda d: (d, 0, 0))
    mat = pl.BlockSpec((None, SSM_GROUP_CH, GN), lambda d: (d, 0, 0))
    sv = jax.ShapeDtypeStruct(a_re.shape, F32)
    sm = jax.ShapeDtypeStruct(b_re.shape, F32)
    return pl.pallas_call(body, name=name, grid=(2,), in_specs=[vec, vec, vec, mat, mat],
                          out_specs=[vec, vec, mat, mat], out_shape=[sv, sv, sm, sm],
                          compiler_params=_cp(("parallel",)))(a_re, a_im, ls, b_re, b_im)


def _s5_disc_bwd(a_re, a_im, ls, b_re, b_im, d_lr, d_li, d_br, d_bi, name):
    GN = a_re.shape[-1]

    def body(ar, ai, l, br, bi, g_lr, g_li, g_br, g_bi, o_ar, o_ai, o_l, o_br, o_bi):
        _, vjp = jax.vjp(_s5_disc, ar[...], ai[...], l[...], br[...], bi[...])
        for o, v in zip((o_ar, o_ai, o_l, o_br, o_bi), vjp((g_lr[...], g_li[...], g_br[...], g_bi[...]))):
            o[...] = v

    vec = pl.BlockSpec((None, 1, GN), lambda d: (d, 0, 0))
    mat = pl.BlockSpec((None, SSM_GROUP_CH, GN), lambda d: (d, 0, 0))
    sv = jax.ShapeDtypeStruct(a_re.shape, F32)
    sm = jax.ShapeDtypeStruct(b_re.shape, F32)
    return pl.pallas_call(body, name=name, grid=(2,), in_specs=[vec, vec, vec, mat, mat, vec, vec, mat, mat],
                          out_specs=[vec, vec, vec, mat, mat], out_shape=[sv, sv, sv, sm, sm],
                          compiler_params=_cp(("parallel",)))(a_re, a_im, ls, b_re, b_im, d_lr, d_li, d_br, d_bi)


def _cmul(ar, ai, br, bi):
    return ar * br - ai * bi, ar * bi + ai * br


def _segment_carries(lr, li, er, ei, n_steps, reverse):
    pr, pi = lr, li
    for _ in range(int(math.log2(n_steps))):
        pr, pi = _cmul(pr, pi, pr, pi)
    row = lax.broadcasted_iota(jnp.int32, er.shape, 0)
    edge = (SSM_SEGMENTS - 1) if reverse else 0
    shift = (SSM_SEGMENTS - 1) if reverse else 1
    cr = jnp.zeros_like(er)
    ci = jnp.zeros_like(ei)
    for _ in range(SSM_SEGMENTS - 1):
        tr_, ti_ = _cmul(pr, pi, cr, ci)
        cr = jnp.where(row == edge, 0.0, pltpu.roll(tr_ + er, shift, 0))
        ci = jnp.where(row == edge, 0.0, pltpu.roll(ti_ + ei, shift, 0))
    return cr, ci


def _s5_geometry(S, D):
    assert S % SSM_SEGMENTS == 0 and D % S5_CB == 0
    n_steps = S // SSM_SEGMENTS
    assert n_steps & (n_steps - 1) == 0, "segment length must be a power of two"
    rows = min(S5_ROWS, S)
    return n_steps, rows, S // rows, D // S5_CB


def _s5_scan(u, b_re, b_im, c_re, c_im, lam_re, lam_im, ends, name):
    S, D = u.shape
    n_steps, rows, nch, ncb = _s5_geometry(S, D)
    full = ends is not None
    GN = ncb * S5_SB

    def body(*refs):
        if full:
            (u_ref, br_ref, bi_ref, cr_ref, ci_ref, lr_ref, li_ref, er_ref, ei_ref,
             xr_ref, xi_ref, y_ref, st_r, st_i, buf_r, buf_i) = refs
        else:
            u_ref, br_ref, bi_ref, lr_ref, li_ref, er_ref, ei_ref, st_r, st_i, buf_r, buf_i = refs
        lr = jnp.broadcast_to(lr_ref[...], (SSM_SEGMENTS, S5_SB))
        li = jnp.broadcast_to(li_ref[...], (SSM_SEGMENTS, S5_SB))

        @pl.when(pl.program_id(1) == 0)
        def _():
            if full:
                st_r[...], st_i[...] = _segment_carries(lr, li, er_ref[...], ei_ref[...], n_steps, False)
            else:
                st_r[...] = jnp.zeros_like(st_r)
                st_i[...] = jnp.zeros_like(st_i)

        ub = u_ref[...].astype(BF16)
        buf_r[...] = jnp.dot(ub, br_ref[...], preferred_element_type=F32)
        buf_i[...] = jnp.dot(ub, bi_ref[...], preferred_element_type=F32)

        def step(i, carry):
            sr, si = carry
            r = pl.ds(pl.multiple_of(i * SSM_SEGMENTS, SSM_SEGMENTS), SSM_SEGMENTS)
            if full:
                xr_ref[r, :] = sr
                xi_ref[r, :] = si
            nr = lr * sr - li * si + buf_r[r, :]
            ni = lr * si + li * sr + buf_i[r, :]
            if full:
                buf_r[r, :] = nr
                buf_i[r, :] = ni
            return nr, ni

        sr, si = lax.fori_loop(0, rows // SSM_SEGMENTS, step, (st_r[...], st_i[...]))
        st_r[...] = sr
        st_i[...] = si
        if full:
            y_ref[...] = (jnp.dot(buf_r[...].astype(BF16), cr_ref[...], preferred_element_type=F32)
                          - jnp.dot(buf_i[...].astype(BF16), ci_ref[...], preferred_element_type=F32))
        else:
            er_ref[...] = sr
            ei_ref[...] = si

    u_spec = pl.BlockSpec((rows, S5_CB), lambda b, c: (c, b))
    bmat = pl.BlockSpec((None, S5_CB, S5_SB), lambda b, c: (b, 0, 0))
    cmat = pl.BlockSpec((None, S5_SB, S5_CB), lambda b, c: (b, 0, 0))
    lvec = pl.BlockSpec((1, S5_SB), lambda b, c: (0, b))
    evec = pl.BlockSpec((SSM_SEGMENTS, S5_SB), lambda b, c: (0, b))
    xblk = pl.BlockSpec((rows, S5_SB), lambda b, c: (c, b))
    scratch = [pltpu.VMEM((SSM_SEGMENTS, S5_SB), F32)] * 2 + [pltpu.VMEM((rows, S5_SB), F32)] * 2
    e_shape = jax.ShapeDtypeStruct((SSM_SEGMENTS, GN), F32)
    if full:
        x_shape = jax.ShapeDtypeStruct((S, GN), F32)
        return pl.pallas_call(
            body, name=name, grid=(ncb, nch),
            in_specs=[u_spec, bmat, bmat, cmat, cmat, lvec, lvec, evec, evec],
            out_specs=[xblk, xblk, u_spec], out_shape=[x_shape, x_shape, jax.ShapeDtypeStruct((S, D), F32)],
            scratch_shapes=scratch, compiler_params=_cp(("parallel", "arbitrary")),
        )(u, b_re, b_im, c_re, c_im, lam_re, lam_im, *ends)
    return pl.pallas_call(
        body, name=name, grid=(ncb, nch), in_specs=[u_spec, bmat, bmat, lvec, lvec],
        out_specs=[evec, evec], out_shape=[e_shape, e_shape],
        scratch_shapes=scratch, compiler_params=_cp(("parallel", "arbitrary")),
    )(u, b_re, b_im, lam_re, lam_im)


def _s5_scan_bwd(dy, u, xp, b_re, b_im, c_re, c_im, lam_re, lam_im, starts, name):
    S, D = dy.shape
    n_steps, rows, nch, ncb = _s5_geometry(S, D)
    full = starts is not None
    GN = ncb * S5_SB
    nt = (((1,), (1,)), ((), ()))
    tn = (((0,), (0,)), ((), ()))

    def body(*refs):
        if full:
            (dy_ref, u_ref, xr_ref, xi_ref, br_ref, bi_ref, cr_ref, ci_ref, lr_ref, li_ref, gr_ref, gi_ref,
             du_ref, dbr_ref, dbi_ref, dcr_ref, dci_ref, dlr_ref, dli_ref, st_r, st_i, buf_r, buf_i) = refs
        else:
            dy_ref, cr_ref, ci_ref, lr_ref, li_ref, gr_ref, gi_ref, st_r, st_i, buf_r, buf_i = refs
        lr = jnp.broadcast_to(lr_ref[...], (SSM_SEGMENTS, S5_SB))
        li = jnp.broadcast_to(li_ref[...], (SSM_SEGMENTS, S5_SB))

        @pl.when(pl.program_id(1) == 0)
        def _():
            if full:
                st_r[...], st_i[...] = _segment_carries(lr, -li, gr_ref[...], gi_ref[...], n_steps, True)
                for r in (dbr_ref, dbi_ref, dcr_ref, dci_ref, dlr_ref, dli_ref):
                    r[...] = jnp.zeros_like(r)
            else:
                st_r[...] = jnp.zeros_like(st_r)
                st_i[...] = jnp.zeros_like(st_i)

        dyb = dy_ref[...].astype(BF16)
        buf_r[...] = lax.dot_general(dyb, cr_ref[...], nt, preferred_element_type=F32)
        buf_i[...] = -lax.dot_general(dyb, ci_ref[...], nt, preferred_element_type=F32)
        n_it = rows // SSM_SEGMENTS

        def step(j, carry):
            gr, gi = carry
            r = pl.ds(pl.multiple_of((n_it - 1 - j) * SSM_SEGMENTS, SSM_SEGMENTS), SSM_SEGMENTS)
            nr = lr * gr + li * gi + buf_r[r, :]
            ni = lr * gi - li * gr + buf_i[r, :]
            if full:
                buf_r[r, :] = nr
                buf_i[r, :] = ni
            return nr, ni

        gr, gi = lax.fori_loop(0, n_it, step, (st_r[...], st_i[...]))
        st_r[...] = gr
        st_i[...] = gi
        if not full:
            gr_ref[...] = gr
            gi_ref[...] = gi
            return
        g_r, g_i = buf_r[...], buf_i[...]
        xr, xi = xr_ref[...], xi_ref[...]
        dlr_ref[...] += jnp.sum(g_r * xr + g_i * xi, axis=0, keepdims=True)
        dli_ref[...] += jnp.sum(g_i * xr - g_r * xi, axis=0, keepdims=True)
        ub = u_ref[...].astype(BF16)
        gb_r, gb_i = g_r.astype(BF16), g_i.astype(BF16)
        du_ref[...] = (lax.dot_general(gb_r, br_ref[...], nt, preferred_element_type=F32)
                       + lax.dot_general(gb_i, bi_ref[...], nt, preferred_element_type=F32))
        dbr_ref[...] += lax.dot_general(ub, gb_r, tn, preferred_element_type=F32)
        dbi_ref[...] += lax.dot_general(ub, gb_i, tn, preferred_element_type=F32)
        lr_, li_ = lr_ref[...], li_ref[...]
        x_r = lr_ * xr - li_ * xi + jnp.dot(ub, br_ref[...], preferred_element_type=F32)
        x_i = lr_ * xi + li_ * xr + jnp.dot(ub, bi_ref[...], preferred_element_type=F32)
        dcr_ref[...] += lax.dot_general(x_r.astype(BF16), dyb, tn, preferred_element_type=F32)
        dci_ref[...] -= lax.dot_general(x_i.astype(BF16), dyb, tn, preferred_element_type=F32)

    rev = lambda c: nch - 1 - c
    u_spec = pl.BlockSpec((rows, S5_CB), lambda b, c: (rev(c), b))
    bmat = pl.BlockSpec((None, S5_CB, S5_SB), lambda b, c: (b, 0, 0))
    cmat = pl.BlockSpec((None, S5_SB, S5_CB), lambda b, c: (b, 0, 0))
    lvec = pl.BlockSpec((1, S5_SB), lambda b, c: (0, b))
    evec = pl.BlockSpec((SSM_SEGMENTS, S5_SB), lambda b, c: (0, b))
    xblk = pl.BlockSpec((rows, S5_SB), lambda b, c: (rev(c), b))
    scratch = [pltpu.VMEM((SSM_SEGMENTS, S5_SB), F32)] * 2 + [pltpu.VMEM((rows, S5_SB), F32)] * 2
    e_shape = jax.ShapeDtypeStruct((SSM_SEGMENTS, GN), F32)
    if full:
        return pl.pallas_call(
            body, name=name, grid=(ncb, nch),
            in_specs=[u_spec, u_spec, xblk, xblk, bmat, bmat, cmat, cmat, lvec, lvec, evec, evec],
            out_specs=[u_spec, bmat, bmat, cmat, cmat, lvec, lvec],
            out_shape=[jax.ShapeDtypeStruct((S, D), F32), jax.ShapeDtypeStruct(b_re.shape, F32),
                       jax.ShapeDtypeStruct(b_re.shape, F32), jax.ShapeDtypeStruct(c_re.shape, F32),
                       jax.ShapeDtypeStruct(c_re.shape, F32), jax.ShapeDtypeStruct((1, GN), F32),
                       jax.ShapeDtypeStruct((1, GN), F32)],
            scratch_shapes=scratch, compiler_params=_cp(("parallel", "arbitrary")),
        )(dy, u, *xp, b_re, b_im, c_re, c_im, lam_re, lam_im, *starts)
    return pl.pallas_call(
        body, name=name, grid=(ncb, nch), in_specs=[u_spec, cmat, cmat, lvec, lvec],
        out_specs=[evec, evec], out_shape=[e_shape, e_shape],
        scratch_shapes=scratch, compiler_params=_cp(("parallel", "arbitrary")),
    )(dy, c_re, c_im, lam_re, lam_im)


def _s5_perm(t, reverse):
    S, D = t.shape
    t = t[::-1] if reverse else t
    return t.reshape(SSM_SEGMENTS, S // SSM_SEGMENTS, D).transpose(1, 0, 2).reshape(S, D)


def _s5_unperm(t, reverse):
    S, D = t.shape
    t = t.reshape(S // SSM_SEGMENTS, SSM_SEGMENTS, D).transpose(1, 0, 2).reshape(S, D)
    return t[::-1] if reverse else t


def _s5_blockdiag_b(bb, ncb):
    gpb = S5_CB // SSM_GROUP_CH
    t = bb.reshape(SSM_GROUP_CH, ncb, gpb, SSM_STATE)
    return jnp.einsum('cbgn,gh->bgchn', t, jnp.eye(gpb, dtype=bb.dtype)).reshape(ncb, S5_CB, S5_SB)


def _s5_blockdiag_b_t(dblk):
    ncb = dblk.shape[0]
    gpb = S5_CB // SSM_GROUP_CH
    t = dblk.reshape(ncb, gpb, SSM_GROUP_CH, gpb, SSM_STATE)
    return jnp.einsum('bgchn,gh->cbgn', t, jnp.eye(gpb, dtype=dblk.dtype)).reshape(SSM_GROUP_CH, -1)


def _s5_blockdiag_c(c, ncb):
    gpb = S5_CB // SSM_GROUP_CH
    t = c.reshape(ncb, gpb, SSM_GROUP_CH, SSM_STATE)
    return jnp.einsum('bgcn,gh->bgnhc', t, jnp.eye(gpb, dtype=c.dtype)).reshape(ncb, S5_SB, S5_CB)


def _s5_blockdiag_c_t(dblk):
    ncb = dblk.shape[0]
    gpb = S5_CB // SSM_GROUP_CH
    t = dblk.reshape(ncb, gpb, SSM_STATE, gpb, SSM_GROUP_CH)
    return jnp.einsum('bgnhc,gh->bgcn', t, jnp.eye(gpb, dtype=dblk.dtype)).reshape(-1, SSM_GROUP_CH, SSM_STATE)


_GELU_C = math.sqrt(2.0 / math.pi)


def _gelu(y):
    return y * (0.5 * (1.0 + jnp.tanh(_GELU_C * (y + 0.044715 * (y * y * y)))))


def _gelu_grad(y):
    t = jnp.tanh(_GELU_C * (y + 0.044715 * (y * y * y)))
    return 0.5 * (1.0 + t) + 0.5 * y * (1.0 - t * t) * (_GELU_C * (1.0 + 3.0 * 0.044715 * y * y))


def _rowwise(fn, ins, outs, name, acc=()):
    S, D = next(a.shape for a in ins if a.shape[0] != 1)
    tr = _pick(S, (256, 128, 8))
    row = pl.BlockSpec((tr, D), lambda i: (i, 0))
    vec = pl.BlockSpec((1, D), lambda i: (0, 0))
    n_in = len(ins)

    def body(*refs):
        res = fn(*[r[...] for r in refs[:n_in]])
        for k, (o, v) in enumerate(zip(refs[n_in:], res)):
            if k in acc:
                @pl.when(pl.program_id(0) == 0)
                def _():
                    o[...] = jnp.zeros_like(o)
                o[...] += jnp.sum(v, axis=0, keepdims=True)
            else:
                o[...] = v.astype(o.dtype)

    return pl.pallas_call(
        body, name=name, grid=(S // tr,), in_specs=[vec if a.shape[0] == 1 else row for a in ins],
        out_specs=[vec if k in acc else row for k in range(len(outs))],
        out_shape=[jax.ShapeDtypeStruct((1, D) if k in acc else (S, D), dt) for k, dt in enumerate(outs)],
        compiler_params=_cp(("arbitrary",) if acc else ("parallel",)),
    )(*ins)


def _s5_params(p):
    G, N = p["a_re"].shape[1:]
    vec = lambda a: a.reshape(2, 1, G * N)
    ls = jnp.broadcast_to(p["log_step"][:, :, None], (2, G, N))
    bt = lambda b: b.transpose(0, 3, 1, 2).reshape(2, SSM_GROUP_CH, G * N)
    return vec(p["a_re"]), vec(p["a_im"]), vec(ls), bt(p["b_re"]), bt(p["b_im"])


def _s5_layer_fwd(x, norm_g, p, w_glu, tag):
    S, D = x.shape
    ncb = D // S5_CB
    hn = _rmsnorm_fwd(x, norm_g, F32, f"s5_norm_{tag}")
    raw = _s5_params(p)
    lam_r, lam_i, bb_r, bb_i = _s5_disc_fwd(*raw, f"s5_disc_{tag}")
    dirs = []
    ys = []
    for dirn in range(2):
        up = _s5_perm(hn, dirn == 1)
        mats = (_s5_blockdiag_b(bb_r[dirn], ncb).astype(BF16), _s5_blockdiag_b(bb_i[dirn], ncb).astype(BF16),
                _s5_blockdiag_c(p["c_re"][dirn], ncb).astype(BF16), _s5_blockdiag_c(p["c_im"][dirn], ncb).astype(BF16))
        lam = (lam_r[dirn], lam_i[dirn])
        ends = _s5_scan(up, mats[0], mats[1], None, None, *lam, None, f"s5_ends_{tag}_{dirn}")
        xr, xi, y = _s5_scan(up, *mats, *lam, ends, f"s5_scan_{tag}_{dirn}")
        dirs.append((up, (xr, xi), mats, lam))
        ys.append(_s5_unperm(y, dirn == 1))
    ytot, z = _rowwise(lambda u, d, a, b: ((lambda y: (y, _gelu(y)))(d * u + a + b)),
                       [hn, p["d"], ys[0], ys[1]], [F32, BF16], f"s5_gelu_{tag}")
    t = _mm(z, w_glu, name=f"s5_glu_{tag}", epi=lambda r, b: r + b, epi_in=(p["b_glu"],))
    (x_new,) = _rowwise(lambda xx, zz, tt: (xx + zz.astype(F32) * jax.nn.sigmoid(tt),),
                        [x, z, t], [F32], f"s5_out_{tag}")
    return x_new, (x, hn, raw, dirs, ytot, z, t)


def _s5_layer_bwd(dres, saved, norm_g, p, w_glu, tag):
    x, hn, raw, dirs, ytot, z, t = saved
    S, D = x.shape
    G, N = p["a_re"].shape[1:]

    def glu_bwd(do, zz, tt):
        sg = jax.nn.sigmoid(tt)
        dt = do * zz.astype(F32) * (sg * (1.0 - sg))
        return dt, do * sg, dt

    dt, dzd, db_glu = _rowwise(glu_bwd, [dres, z, t], [BF16, F32, F32], f"s5_out_bwd_{tag}", acc=(2,))
    dz = _mm(dt, w_glu, tb=True, name=f"s5_dz_{tag}", epi=_add_epi, epi_in=(dzd,))
    dw_glu = _mm(z, dt, ta=True, name=f"s5_dwglu_{tag}")

    def gelu_bwd(dzz, y, u, d):
        dy = dzz * _gelu_grad(y)
        return dy, dy * d, dy * u

    dy, du, dd = _rowwise(gelu_bwd, [dz, ytot, hn, p["d"]], [F32, F32, F32], f"s5_gelu_bwd_{tag}", acc=(2,))
    d_lr, d_li, d_bbr, d_bbi, d_cr, d_ci = [], [], [], [], [], []
    for dirn in range(2):
        up, xp, mats, lam = dirs[dirn]
        dyp = _s5_perm(dy, dirn == 1)
        starts = _s5_scan_bwd(dyp, None, None, None, None, mats[2], mats[3], *lam, None, f"s5_starts_{tag}_{dirn}")
        dup, dbr, dbi, dcr, dci, dlr, dli = _s5_scan_bwd(dyp, up, xp, *mats, *lam, starts, f"s5_scan_bwd_{tag}_{dirn}")
        du = du + _s5_unperm(dup, dirn == 1)
        d_lr.append(dlr)
        d_li.append(dli)
        d_bbr.append(_s5_blockdiag_b_t(dbr))
        d_bbi.append(_s5_blockdiag_b_t(dbi))
        d_cr.append(_s5_blockdiag_c_t(dcr))
        d_ci.append(_s5_blockdiag_c_t(dci))
    da_re, da_im, dls, db_re, db_im = _s5_disc_bwd(*raw, jnp.stack(d_lr), jnp.stack(d_li), jnp.stack(d_bbr),
                                                   jnp.stack(d_bbi), f"s5_disc_bwd_{tag}")
    dres, dg = _rmsnorm_bwd(x, norm_g, du, dres, f"s5_norm_bwd_{tag}")
    unb = lambda b: b.reshape(2, SSM_GROUP_CH, G, N).transpose(0, 2, 3, 1)
    grads = dict(a_re=da_re.reshape(2, G, N), a_im=da_im.reshape(2, G, N), log_step=dls.reshape(2, G, N).sum(-1),
                 b_re=unb(db_re), b_im=unb(db_im), c_re=jnp.stack(d_cr), c_im=jnp.stack(d_ci),
                 d=dd, w_glu=dw_glu, b_glu=db_glu, norm=dg)
    return dres, grads


def _adamw(w, g, m, v, name):
    R, C = w.shape
    tr = _pick(R, (512, 256, 128, 64, 32, 16, 8))
    tn = _pick(C, (512, 256, 128))

    def body(w_ref, g_ref, m_ref, v_ref, d_ref, nm_ref, nv_ref):
        gg = g_ref[...]
        m2 = ADAM_B1 * m_ref[...] + (1.0 - ADAM_B1) * gg
        v2 = ADAM_B2 * v_ref[...] + (1.0 - ADAM_B2) * (gg * gg)
        m_hat = m2 / (1.0 - ADAM_B1 ** ADAM_STEP)
        v_hat = v2 / (1.0 - ADAM_B2 ** ADAM_STEP)
        d_ref[...] = -ADAM_LR * (m_hat / (jnp.sqrt(v_hat) + ADAM_EPS) + ADAM_WD * w_ref[...])
        nm_ref[...] = m2
        nv_ref[...] = v2

    blk = pl.BlockSpec((tr, tn), lambda i, j: (i, j))
    shp = jax.ShapeDtypeStruct((R, C), F32)
    return pl.pallas_call(body, name=name, grid=(R // tr, C // tn), in_specs=[blk] * 4, out_specs=[blk] * 3,
                          out_shape=[shp] * 3, compiler_params=_cp(("parallel", "parallel")))(w, g, m, v)


def _add_selected(p, sel, others, name):
    K, _, M, C = p.shape
    tr = _pick(M, [t for t in (512, 256, 128, 64, 32, 16, 8) if t * C * 4 <= 2 ** 21])
    n_o = len(others)

    def body(sel_ref, p_ref, *refs):
        acc = p_ref[...]
        for r in refs[:n_o]:
            acc = acc + r[...]
        refs[n_o][...] = acc

    grid_spec = pltpu.PrefetchScalarGridSpec(
        num_scalar_prefetch=1, grid=(K, M // tr),
        in_specs=[pl.BlockSpec((None, None, tr, C), lambda k, i, s: (k, s[0], i, 0))]
        + [pl.BlockSpec((None, tr, C), lambda k, i, s: (k, i, 0))] * n_o,
        out_specs=pl.BlockSpec((None, tr, C), lambda k, i, s: (k, i, 0)))
    return pl.pallas_call(body, name=name, grid_spec=grid_spec, out_shape=jax.ShapeDtypeStruct((K, M, C), F32),
                          compiler_params=_cp(("parallel", "parallel")))(sel, p, *others)


def _sum_slots(a, name):
    n, R, C = a.shape
    tr = _pick(R, (512, 256, 128, 64, 32, 16, 8))

    def body(a_ref, o_ref):
        acc = a_ref[0]
        for k in range(1, n):
            acc = acc + a_ref[k]
        o_ref[...] = acc

    return pl.pallas_call(body, name=name, grid=(R // tr,),
                          in_specs=[pl.BlockSpec((n, tr, C), lambda i: (0, i, 0))],
                          out_specs=pl.BlockSpec((tr, C), lambda i: (i, 0)),
                          out_shape=jax.ShapeDtypeStruct((R, C), F32), compiler_params=_cp(("parallel",)))(a)


def _position():
    return lax.axis_index("x"), lax.axis_index("y"), lax.axis_index("c")


def _other_chips(x, y):
    return [(1 - x, y), (x, 1 - y), (1 - x, 1 - y)]


def _run_copies(sends, recvs, local, send_sems, recv_sems, local_sems):
    started = []
    for k, (src, dst, dev) in enumerate(sends):
        cp = pltpu.make_async_remote_copy(src_ref=src, dst_ref=dst, send_sem=send_sems.at[k],
                                          recv_sem=recv_sems.at[k], device_id=dev, device_id_type=MESH)
        cp.start()
        started.append(cp)
    locals_ = []
    for k, (src, dst) in enumerate(local):
        cp = pltpu.make_async_copy(src, dst, local_sems.at[k])
        cp.start()
        locals_.append(cp)
    for k, (dst, dev) in enumerate(recvs):
        pltpu.make_async_remote_copy(src_ref=dst, dst_ref=dst, send_sem=send_sems.at[k], recv_sem=recv_sems.at[k],
                                     device_id=dev, device_id_type=MESH).wait_recv()
    for cp in started:
        cp.wait_send()
    for cp in locals_:
        cp.wait()


def _comm_call(plan, ins, out_shapes, n_remote, n_local, name):
    n_in = len(ins)
    n_out = len(out_shapes)

    def body(*refs):
        in_refs, out_refs = refs[:n_in], refs[n_in:n_in + n_out]
        send_sems, recv_sems, local_sems = refs[n_in + n_out:]
        sends, recvs, local = plan(in_refs, out_refs)
        assert len(sends) == len(recvs) == n_remote and len(local) == n_local
        _run_copies(sends, recvs, local, send_sems, recv_sems, local_sems)

    hbm = pl.BlockSpec(memory_space=pltpu.HBM)
    return pl.pallas_call(
        body, name=name, in_specs=[hbm] * n_in, out_specs=[hbm] * n_out, out_shape=out_shapes,
        scratch_shapes=[pltpu.SemaphoreType.DMA((n_remote,)), pltpu.SemaphoreType.DMA((n_remote,)),
                        pltpu.SemaphoreType.DMA((max(n_local, 1),))],
    )(*ins)


def _gather_chips(arrs, name):
    n = len(arrs)

    def plan(ins, outs):
        x, y, c = _position()
        me = 2 * x + y
        sends, recvs = [], []
        for px, py in _other_chips(x, y):
            for i in range(n):
                sends.append((ins[i], outs[i].at[me], (px, py, c)))
                recvs.append((outs[i].at[2 * px + py], (px, py, c)))
        return sends, recvs, [(ins[i], outs[i].at[me]) for i in range(n)]

    shapes = [jax.ShapeDtypeStruct((4,) + a.shape, a.dtype) for a in arrs]
    return _comm_call(plan, arrs, shapes, 3 * n, n, name)


def _sibling_halves(pieces, name):
    n = len(pieces)

    def plan(ins, outs):
        x, y, c = _position()
        sib = (x, y, 1 - c)
        return ([(ins[i].at[:, 1 - c], outs[i], sib) for i in range(n)],
                [(outs[i], sib) for i in range(n)], [])

    shapes = [jax.ShapeDtypeStruct((p.shape[0],) + p.shape[2:], p.dtype) for p in pieces]
    return _comm_call(plan, pieces, shapes, n, 0, name)


def _scatter_chips(sums, name):
    n = len(sums)

    def plan(ins, outs):
        x, y, c = _position()
        sends, recvs = [], []
        for j, (px, py) in enumerate(_other_chips(x, y)):
            for i in range(n):
                sends.append((ins[i].at[2 * px + py], outs[i].at[j], (px, py, c)))
                recvs.append((outs[i].at[j], (px, py, c)))
        return sends, recvs, []

    shapes = [jax.ShapeDtypeStruct((3,) + s.shape[1:], s.dtype) for s in sums]
    return _comm_call(plan, sums, shapes, 3 * n, 0, name)


def _sibling_join(halves, name):
    n = len(halves)

    def plan(ins, outs):
        x, y, c = _position()
        sib = (x, y, 1 - c)
        return ([(ins[i], outs[i].at[c], sib) for i in range(n)],
                [(outs[i].at[1 - c], sib) for i in range(n)],
                [(ins[i], outs[i].at[c]) for i in range(n)])

    shapes = [jax.ShapeDtypeStruct((2,) + h.shape, h.dtype) for h in halves]
    return _comm_call(plan, halves, shapes, n, n, name)


def _gather_all(v, name):
    rels = [(dx, dy, dc) for dx in (0, 1) for dy in (0, 1) for dc in (0, 1)][1:]

    def plan(ins, outs):
        x, y, c = _position()
        me = 4 * x + 2 * y + c
        flip = lambda a, d: 1 - a if d else a
        sends, recvs = [], []
        for dx, dy, dc in rels:
            px, py, pc = flip(x, dx), flip(y, dy), flip(c, dc)
            sends.append((ins[0], outs[0].at[me], (px, py, pc)))
            recvs.append((outs[0].at[4 * px + 2 * py + pc], (px, py, pc)))
        return sends, recvs, [(ins[0], outs[0].at[me])]

    return _comm_call(plan, [v], [jax.ShapeDtypeStruct((8,) + v.shape, v.dtype)], len(rels), 1, name)[0]


WEIGHTS = ['mix_norm', 'ffn_norm', 'final_norm', 'attn_w_qkv', 'attn_w_o', 'attn_sink', 'ssm_a_re', 'ssm_a_im',
           'ssm_log_step', 'ssm_b_re', 'ssm_b_im', 'ssm_c_re', 'ssm_c_im', 'ssm_d', 'ssm_w_glu', 'ssm_b_glu',
           'mla_w_dqkv', 'mla_q_norm', 'mla_kv_norm', 'mla_w_uq', 'mla_w_ukv', 'mla_w_o', 'ffn_w_up',
           'ffn_conv_w', 'ffn_conv_b', 'ffn_w_down']
BIG = dict(attn_w_qkv='col', attn_w_o='row', ssm_w_glu='row', mla_w_dqkv='row', mla_w_uq='col',
           mla_w_ukv='col', mla_w_o='row', ffn_w_up='col', ffn_w_down='row')
SMALL_SHARDED = ('mla_q_norm', 'mla_kv_norm', 'ffn_conv_w')
N_CHIPS = 4


def _assemble(g, kind):
    if kind == 'row':
        t = g.transpose(1, 0, 2, 3)
        return t.reshape(t.shape[0], -1, t.shape[3])
    t = g.transpose(1, 2, 0, 3)
    return t.reshape(t.shape[0], t.shape[1], -1)


def _to_pieces(w, kind):
    L, R, C = w.shape
    if kind == 'col':
        t = w.reshape(L, 2, R // 2, N_CHIPS, C // N_CHIPS).transpose(3, 1, 0, 2, 4)
    else:
        t = w.reshape(L, N_CHIPS, R // N_CHIPS, 2, C // 2).transpose(1, 3, 0, 2, 4)
    return t.reshape(N_CHIPS, 2, L * t.shape[3], t.shape[4])


def _from_halves(h, kind, shard_shape):
    L = shard_shape[0]
    t = h.reshape(2, L, -1, h.shape[2])
    t = t.transpose(1, 0, 2, 3) if kind == 'col' else t.transpose(1, 2, 0, 3)
    return t.reshape(shard_shape)


def _pack(arrs):
    flat = jnp.concatenate([a.reshape(-1) for a in arrs])
    pad = (-flat.shape[0]) % (8 * LANES)
    return jnp.pad(flat, (0, pad)).reshape(-1, LANES)


def _unpack(packed, like):
    flat = packed.reshape(-1)
    out, off = [], 0
    for a in like:
        out.append(flat[off:off + a.size].reshape(a.shape))
        off += a.size
    return out


def kernel(x, mix_norm, ffn_norm, final_norm, attn_w_qkv, attn_w_o, attn_sink, ssm_a_re, ssm_a_im, ssm_log_step, ssm_b_re, ssm_b_im, ssm_c_re, ssm_c_im, ssm_d, ssm_w_glu, ssm_b_glu, mla_w_dqkv, mla_q_norm, mla_kv_norm, mla_w_uq, mla_w_ukv, mla_w_o, ffn_w_up, ffn_conv_w, ffn_conv_b, ffn_w_down, loss_target, m_mix_norm, m_ffn_norm, m_final_norm, m_attn_w_qkv, m_attn_w_o, m_attn_sink, m_ssm_a_re, m_ssm_a_im, m_ssm_log_step, m_ssm_b_re, m_ssm_b_im, m_ssm_c_re, m_ssm_c_im, m_ssm_d, m_ssm_w_glu, m_ssm_b_glu, m_mla_w_dqkv, m_mla_q_norm, m_mla_kv_norm, m_mla_w_uq, m_mla_w_ukv, m_mla_w_o, m_ffn_w_up, m_ffn_conv_w, m_ffn_conv_b, m_ffn_w_down, v_mix_norm, v_ffn_norm, v_final_norm, v_attn_w_qkv, v_attn_w_o, v_attn_sink, v_ssm_a_re, v_ssm_a_im, v_ssm_log_step, v_ssm_b_re, v_ssm_b_im, v_ssm_c_re, v_ssm_c_im, v_ssm_d, v_ssm_w_glu, v_ssm_b_glu, v_mla_w_dqkv, v_mla_q_norm, v_mla_kv_norm, v_mla_w_uq, v_mla_w_ukv, v_mla_w_o, v_ffn_w_up, v_ffn_conv_w, v_ffn_conv_b, v_ffn_w_down):
    args = locals()
    w = {n: args[n] for n in WEIGHTS}
    mom = {n: args["m_" + n] for n in WEIGHTS}
    var = {n: args["v_" + n] for n in WEIGHTS}
    depth = mix_norm.shape[0]
    xs = x[0]
    target = loss_target[0]
    chip = 2 * lax.axis_index("x") + lax.axis_index("y")
    core = lax.axis_index("c")

    big_names = list(BIG)
    gathered = _gather_chips([w[n].astype(BF16) for n in big_names] + [w[n] for n in SMALL_SHARDED], "gather_weights")
    full = {n: _assemble(g, BIG[n]) for n, g in zip(big_names, gathered)}
    for n, g in zip(SMALL_SHARDED, gathered[len(big_names):]):
        full[n] = jnp.moveaxis(g, 0, -2).reshape(g.shape[1:-1] + (-1,))

    def ffn_args(i):
        return (ffn_norm[i:i + 1], full["ffn_w_up"][i], full["ffn_conv_w"][i], ffn_conv_b[i:i + 1], full["ffn_w_down"][i])

    def s5_params(j):
        return dict(a_re=ssm_a_re[j], a_im=ssm_a_im[j], log_step=ssm_log_step[j], b_re=ssm_b_re[j], b_im=ssm_b_im[j],
                    c_re=ssm_c_re[j], c_im=ssm_c_im[j], d=ssm_d[j:j + 1], b_glu=ssm_b_glu[j:j + 1])

    def mla_args(j):
        wd, wq, wkv = _mla_weights(full["mla_w_dqkv"][j], full["mla_w_uq"][j], full["mla_w_ukv"][j])
        return (wd, wq, wkv, full["mla_w_o"][j], full["mla_q_norm"][j:j + 1], full["mla_kv_norm"][j:j + 1])

    h = xs
    saved = []
    for i in range(depth):
        kind, j = i % 3, i // 3
        g = mix_norm[i:i + 1]
        if kind == 0:
            h, sm = _attn_layer_fwd(h, g, full["attn_w_qkv"][j], full["attn_w_o"][j], attn_sink[j], f"{i}")
        elif kind == 1:
            h, sm = _s5_layer_fwd(h, g, s5_params(j), full["ssm_w_glu"][j], f"{i}")
        else:
            h, sm = _mla_layer_fwd(h, g, *mla_args(j), f"{i}")
        h, sf = _ffn_fwd(h, *ffn_args(i), f"{i}")
        saved.append((sm, sf))
    loss_part, dres, d_final = _loss_head(h, final_norm.reshape(1, -1), target, "loss_head")

    gl = {n: [None] * w[n].shape[0] for n in WEIGHTS if n != 'final_norm'}
    for i in reversed(range(depth)):
        kind, j = i % 3, i // 3
        sm, sf = saved[i]
        dres, (dg, dwu, dcw, dcb, dwd) = _ffn_bwd(dres, sf, *ffn_args(i), f"{i}")
        gl['ffn_norm'][i], gl['ffn_w_up'][i], gl['ffn_conv_w'][i] = dg[0], dwu, dcw
        gl['ffn_conv_b'][i], gl['ffn_w_down'][i] = dcb[0], dwd
        g = mix_norm[i:i + 1]
        if kind == 0:
            dres, (dg, dwq, dwo, dsk) = _attn_layer_bwd(dres, sm, g, full["attn_w_qkv"][j], full["attn_w_o"][j],
                                                        attn_sink[j], f"{i}")
            gl['attn_w_qkv'][j], gl['attn_w_o'][j], gl['attn_sink'][j] = dwq, dwo, dsk
        elif kind == 1:
            dres, gs = _s5_layer_bwd(dres, sm, g, s5_params(j), full["ssm_w_glu"][j], f"{i}")
            dg = gs['norm']
            for k in ('a_re', 'a_im', 'log_step', 'b_re', 'b_im', 'c_re', 'c_im'):
                gl['ssm_' + k][j] = gs[k]
            gl['ssm_d'][j], gl['ssm_b_glu'][j], gl['ssm_w_glu'][j] = gs['d'][0], gs['b_glu'][0], gs['w_glu']
        else:
            dres, (dg, dwd_, dqn, dkvn, dwuq, dwukv, dwo) = _mla_layer_bwd(dres, sm, g, *mla_args(j), f"{i}")
            gl['mla_w_dqkv'][j], gl['mla_q_norm'][j], gl['mla_kv_norm'][j] = dwd_, dqn[0], dkvn[0]
            gl['mla_w_uq'][j], gl['mla_w_ukv'][j], gl['mla_w_o'][j] = dwuq, dwukv, dwo
        gl['mix_norm'][i] = dg[0]
    local = {n: jnp.stack(v) for n, v in gl.items()}
    local['final_norm'] = d_final[0]

    pieces = [_to_pieces(local[n], BIG[n]) for n in big_names]
    from_sib = _sibling_halves(pieces, "reduce_sibling")
    sel_core = jnp.reshape(core, (1,)).astype(jnp.int32)
    pair = [_add_selected(p, sel_core, [r], f"reduce_add_pair_{n}") for n, p, r in zip(big_names, pieces, from_sib)]
    from_chips = _scatter_chips(pair, "reduce_chips")
    sel_chip = jnp.reshape(chip, (1,)).astype(jnp.int32)
    mine = [_add_selected(p.reshape((1,) + p.shape), sel_chip, [r[k:k + 1] for k in range(3)], f"reduce_add_chips_{n}")[0]
            for n, p, r in zip(big_names, pair, from_chips)]
    joined = _sibling_join(mine, "reduce_join")
    grads = {n: _from_halves(hv, BIG[n], w[n].shape) for n, hv in zip(big_names, joined)}

    small_names = [n for n in WEIGHTS if n not in BIG]
    summed = _sum_slots(_gather_all(_pack([local[n] for n in small_names]), "reduce_small"), "reduce_small_sum")
    for n, gsum in zip(small_names, _unpack(summed, [local[n] for n in small_names])):
        if n in SMALL_SHARDED:
            width = w[n].shape[-1]
            gsum = lax.dynamic_slice_in_dim(gsum, chip * width, width, axis=gsum.ndim - 1)
        grads[n] = gsum

    delta, new_m, new_v = {}, {}, {}
    for n in big_names:
        two_d = lambda a: a.reshape(-1, a.shape[-1])
        d_, m_, v_ = _adamw(two_d(w[n]), two_d(grads[n]), two_d(mom[n]), two_d(var[n]), f"adamw_{n}")
        delta[n], new_m[n], new_v[n] = (t.reshape(w[n].shape) for t in (d_, m_, v_))
    outs = _adamw(*[_pack([src[n] for n in small_names]) for src in (w, grads, mom, var)], "adamw_small")
    for dst, packed in zip((delta, new_m, new_v), outs):
        for n, t in zip(small_names, _unpack(packed, [w[n] for n in small_names])):
            dst[n] = t

    loss = lax.psum(loss_part[0, 0], ("x", "y", "c"))
    return (loss, dres[None], *[grads[n] for n in WEIGHTS], *[delta[n] for n in WEIGHTS],
            *[new_m[n] for n in WEIGHTS], *[new_v[n] for n in WEIGHTS])
```

```python
import functools
import math

import numpy as np
import jax
import jax.numpy as jnp
from jax import lax
from jax.experimental import pallas as pl
from jax.experimental.pallas import tpu as pltpu

F32 = jnp.float32
BF16 = jnp.bfloat16
MESH = pl.DeviceIdType.MESH

RMS_EPS = 1e-6
ATTN_DH = 64
ATTN_GROUP = 8
ATTN_BLOCK = 128
SSM_GROUP_CH = 16
SSM_STATE = 64
SSM_SEGMENTS = 8
MLA_HEADS = 16
MLA_LORA = 512
MLA_NOPE = 128
MLA_ROPE = 64
MLA_V = 128
ROPE_THETA = 10000.0
LANES = 128
VMEM_LIMIT = 56 * 1024 * 1024

ADAM_LR = 0.001
ADAM_B1 = 0.9
ADAM_B2 = 0.999
ADAM_EPS = 1e-08
ADAM_WD = 0.01
ADAM_STEP = 10


def _cp(sem=None):
    kw = dict(vmem_limit_bytes=VMEM_LIMIT)
    if sem is not None:
        kw["dimension_semantics"] = sem
    return pltpu.CompilerParams(**kw)


MM_FULL_K = 2048
MM_FULL_N = 1536


def _pick(n, cands):
    for c in cands:
        if n % c == 0:
            return c
    return n


def _gmm(a, b, *, grid, a_spec, b_spec, o_spec, out_shape, dims, kax, acc_shape, name,
         epi=None, epi_in=(), epi_specs=()):
    nk = grid[kax]
    n_epi = len(epi_in)

    def finish(r, e_refs, o_ref):
        if epi is not None:
            r = epi(r, *[e[...] for e in e_refs])
        o_ref[...] = r.astype(o_ref.dtype)

    def product(a_ref, b_ref):
        return lax.dot_general(a_ref[...].astype(BF16), b_ref[...].astype(BF16), dims, preferred_element_type=F32)

    def body_one(a_ref, b_ref, *rest):
        finish(product(a_ref, b_ref), rest[:n_epi], rest[n_epi])

    def body_acc(a_ref, b_ref, *rest):
        acc_ref = rest[n_epi + 1]
        k = pl.program_id(kax)

        @pl.when(k == 0)
        def _():
            acc_ref[...] = jnp.zeros_like(acc_ref)

        acc_ref[...] += product(a_ref, b_ref)

        @pl.when(k == nk - 1)
        def _():
            finish(acc_ref[...], rest[:n_epi], rest[n_epi])

    sem = tuple("arbitrary" if i == kax else "parallel" for i in range(len(grid)))
    return pl.pallas_call(
        body_one if nk == 1 else body_acc, name=name, grid=grid,
        in_specs=[a_spec, b_spec, *epi_specs], out_specs=o_spec, out_shape=out_shape,
        scratch_shapes=[] if nk == 1 else [pltpu.VMEM(acc_shape, F32)], compiler_params=_cp(sem),
    )(a, b, *epi_in)


def _mm(a, b, *, ta=False, tb=False, out_dtype=F32, name, epi=None, epi_in=(), tm=None, tn=None, tk=None):
    (K, M) = a.shape if ta else a.shape[::-1]
    (N, K2) = b.shape if tb else b.shape[::-1]
    assert K == K2, (a.shape, b.shape, ta, tb)
    tm = tm or _pick(M, (1024, 512, 256, 128))
    tn = tn or (N if N <= MM_FULL_N else _pick(N, (1024, 512, 256, 128)))
    tk = tk or (K if K <= MM_FULL_K else _pick(K, (1024, 512, 256, 128)))
    grid = (M // tm, N // tn, K // tk)
    a_spec = pl.BlockSpec((tk, tm), lambda i, j, k: (k, i)) if ta else pl.BlockSpec((tm, tk), lambda i, j, k: (i, k))
    b_spec = pl.BlockSpec((tn, tk), lambda i, j, k: (j, k)) if tb else pl.BlockSpec((tk, tn), lambda i, j, k: (k, j))
    dims = (((0 if ta else 1,), (1 if tb else 0,)), ((), ()))
    epi_specs = [pl.BlockSpec((1, tn), lambda i, j, k: (0, j)) if e.shape[0] == 1
                 else pl.BlockSpec((tm, tn), lambda i, j, k: (i, j)) for e in epi_in]
    return _gmm(a, b, grid=grid, a_spec=a_spec, b_spec=b_spec,
                o_spec=pl.BlockSpec((tm, tn), lambda i, j, k: (i, j)),
                out_shape=jax.ShapeDtypeStruct((M, N), out_dtype), dims=dims, kax=2,
                acc_shape=(tm, tn), name=name, epi=epi, epi_in=epi_in, epi_specs=epi_specs)


def _add_epi(r, res):
    return res + r


def _rmsnorm_fwd(x, g, out_dtype, name):
    S, D = x.shape
    tr = _pick(S, (256, 128, 8))

    def body(x_ref, g_ref, o_ref):
        xf = x_ref[...]
        r = lax.rsqrt(jnp.mean(xf * xf, axis=-1, keepdims=True) + RMS_EPS)
        o_ref[...] = ((xf * r) * g_ref[...]).astype(o_ref.dtype)

    return pl.pallas_call(
        body, name=name, grid=(S // tr,),
        in_specs=[pl.BlockSpec((tr, D), lambda i: (i, 0)), pl.BlockSpec((1, D), lambda i: (0, 0))],
        out_specs=pl.BlockSpec((tr, D), lambda i: (i, 0)),
        out_shape=jax.ShapeDtypeStruct((S, D), out_dtype), compiler_params=_cp(("parallel",)),
    )(x, g)


def _rms_bwd_math(xf, g, dy):
    r = lax.rsqrt(jnp.mean(xf * xf, axis=-1, keepdims=True) + RMS_EPS)
    xh = xf * r
    dxh = dy * g
    dx = r * (dxh - xh * jnp.mean(dxh * xh, axis=-1, keepdims=True))
    return dx, jnp.sum(dy * xh, axis=0, keepdims=True)


def _rmsnorm_bwd(x, g, dy, dres, name):
    S, D = x.shape
    tr = _pick(S, (256, 128, 8))
    dys = list(dy) if isinstance(dy, (list, tuple)) else [dy]
    n_dy = len(dys)

    def body(x_ref, g_ref, *refs):
        dy_refs = refs[:n_dy]
        dres_ref, dx_ref, dxb_ref, dg_ref = refs[n_dy:]

        @pl.when(pl.program_id(0) == 0)
        def _():
            dg_ref[...] = jnp.zeros_like(dg_ref)

        dy_sum = dy_refs[0][...].astype(F32)
        for r in dy_refs[1:]:
            dy_sum = dy_sum + r[...].astype(F32)
        dx, dg = _rms_bwd_math(x_ref[...], g_ref[...], dy_sum)
        tot = dres_ref[...] + dx
        dx_ref[...] = tot
        dxb_ref[...] = tot.astype(BF16)
        dg_ref[...] += dg

    row = pl.BlockSpec((tr, D), lambda i: (i, 0))
    vec = pl.BlockSpec((1, D), lambda i: (0, 0))
    dx, dxb, dg = pl.pallas_call(
        body, name=name, grid=(S // tr,), in_specs=[row, vec] + [row] * (n_dy + 1), out_specs=[row, row, vec],
        out_shape=[jax.ShapeDtypeStruct((S, D), F32), jax.ShapeDtypeStruct((S, D), BF16),
                   jax.ShapeDtypeStruct((1, D), F32)],
        compiler_params=_cp(("arbitrary",)),
    )(x, g, *dys, dres)
    return (dx, dxb), dg


def _loss_head(x, g, target, name):
    S, D = x.shape
    tr = _pick(S, (256, 128, 8))

    def body(x_ref, g_ref, t_ref, loss_ref, dx_ref, dxb_ref, dg_ref):
        @pl.when(pl.program_id(0) == 0)
        def _():
            dg_ref[...] = jnp.zeros_like(dg_ref)
            loss_ref[...] = jnp.zeros_like(loss_ref)

        xf = x_ref[...]
        gg = g_ref[...]
        r = lax.rsqrt(jnp.mean(xf * xf, axis=-1, keepdims=True) + RMS_EPS)
        e = (xf * r) * gg - t_ref[...]
        loss_ref[...] += 0.5 * jnp.sum(jnp.mean(e * e, axis=-1, keepdims=True), axis=0, keepdims=True)
        dx, dg = _rms_bwd_math(xf, gg, e * (1.0 / D))
        dx_ref[...] = dx
        dxb_ref[...] = dx.astype(BF16)
        dg_ref[...] += dg

    row = pl.BlockSpec((tr, D), lambda i: (i, 0))
    vec = pl.BlockSpec((1, D), lambda i: (0, 0))
    one = pl.BlockSpec((1, 1), lambda i: (0, 0))
    loss, dx, dxb, dg = pl.pallas_call(
        body, name=name, grid=(S // tr,), in_specs=[row, vec, row], out_specs=[one, row, row, vec],
        out_shape=[jax.ShapeDtypeStruct((1, 1), F32), jax.ShapeDtypeStruct((S, D), F32),
                   jax.ShapeDtypeStruct((S, D), BF16), jax.ShapeDtypeStruct((1, D), F32)],
        compiler_params=_cp(("arbitrary",)),
    )(x, g, target)
    return loss, (dx, dxb), dg


HALO = 16


def _shift_rows(main, prev_row, next_row):
    tr = main.shape[0]
    row = lax.broadcasted_iota(jnp.int32, main.shape, 0)
    up = jnp.where(row == 0, prev_row, pltpu.roll(main, 1, 0))
    dn = jnp.where(row == tr - 1, next_row, pltpu.roll(main, tr - 1, 0))
    return up, dn


def _halo_specs(tr, tn, S, col_of):
    hb = tr // HALO
    last = S // HALO - 1
    return [pl.BlockSpec((tr, tn), lambda j, i: (i, col_of(j))),
            pl.BlockSpec((HALO, tn), lambda j, i: (jnp.maximum(i * hb - 1, 0), col_of(j))),
            pl.BlockSpec((HALO, tn), lambda j, i: (jnp.minimum((i + 1) * hb, last), col_of(j)))]


def _halo_rows(main_ref, prev_ref, next_ref, i, n_i):
    main = main_ref[...].astype(F32)
    prev_row = prev_ref[HALO - 1:HALO, :].astype(F32) * (i > 0).astype(F32)
    next_row = next_ref[0:1, :].astype(F32) * (i < n_i - 1).astype(F32)
    up, dn = _shift_rows(main, prev_row, next_row)
    return up, main, dn


def _conv3(w_ref, b_ref, up, mid, dn):
    return b_ref[...] + w_ref[0:1, :] * up + w_ref[1:2, :] * mid + w_ref[2:3, :] * dn


def _ffn_tiles(S, F):
    return _pick(S, (512, 256, 128, 16)), _pick(F, (512, 256, 128))


def _conv_gate_fwd(u, conv_w, conv_b, name):
    S, F2 = u.shape
    F = F2 // 2
    tr, tn = _ffn_tiles(S, F)
    nj, ni = F // tn, S // tr

    def body(gm, gp, gn, vm, vp, vn, wg, wv, bg, bv, o_ref):
        i = pl.program_id(1)
        cg = _conv3(wg, bg, *_halo_rows(gm, gp, gn, i, ni))
        cv = _conv3(wv, bv, *_halo_rows(vm, vp, vn, i, ni))
        o_ref[...] = (cg * jax.nn.sigmoid(cg) * cv).astype(o_ref.dtype)

    wspec = lambda off: pl.BlockSpec((3, tn), lambda j, i: (0, j + off))
    bspec = lambda off: pl.BlockSpec((1, tn), lambda j, i: (0, j + off))
    return pl.pallas_call(
        body, name=name, grid=(nj, ni),
        in_specs=[*_halo_specs(tr, tn, S, lambda j: j), *_halo_specs(tr, tn, S, lambda j: j + nj),
                  wspec(0), wspec(nj), bspec(0), bspec(nj)],
        out_specs=pl.BlockSpec((tr, tn), lambda j, i: (i, j)),
        out_shape=jax.ShapeDtypeStruct((S, F), BF16), compiler_params=_cp(("parallel", "parallel")),
    )(u, u, u, u, u, u, conv_w, conv_w, conv_b, conv_b)


def _conv_gate_bwd(u, da, conv_w, conv_b, name):
    S, F2 = u.shape
    F = F2 // 2
    tr, tn = _ffn_tiles(S, F)
    nj, ni = F // tn, S // tr

    def body(gm, gp, gn, vm, vp, vn, wg, wv, bg, bv, da_ref, dcg_ref, dcv_ref, dwg_ref, dwv_ref, dbg_ref, dbv_ref):
        i = pl.program_id(1)

        @pl.when(i == 0)
        def _():
            for r in (dwg_ref, dwv_ref, dbg_ref, dbv_ref):
                r[...] = jnp.zeros_like(r)

        g_rows = _halo_rows(gm, gp, gn, i, ni)
        v_rows = _halo_rows(vm, vp, vn, i, ni)
        cg = _conv3(wg, bg, *g_rows)
        cv = _conv3(wv, bv, *v_rows)
        sg = jax.nn.sigmoid(cg)
        d = da_ref[...].astype(F32)
        dcv = d * (cg * sg)
        dcg = d * cv * (sg * (1.0 + cg * (1.0 - sg)))
        dcg_ref[...] = dcg.astype(dcg_ref.dtype)
        dcv_ref[...] = dcv.astype(dcv_ref.dtype)
        for t in range(3):
            dwg_ref[t:t + 1, :] += jnp.sum(dcg * g_rows[t], axis=0, keepdims=True)
            dwv_ref[t:t + 1, :] += jnp.sum(dcv * v_rows[t], axis=0, keepdims=True)
        dbg_ref[...] += jnp.sum(dcg, axis=0, keepdims=True)
        dbv_ref[...] += jnp.sum(dcv, axis=0, keepdims=True)

    wspec = lambda off: pl.BlockSpec((3, tn), lambda j, i: (0, j + off))
    bspec = lambda off: pl.BlockSpec((1, tn), lambda j, i: (0, j + off))
    tile = pl.BlockSpec((tr, tn), lambda j, i: (i, j))
    outs = pl.pallas_call(
        body, name=name, grid=(nj, ni),
        in_specs=[*_halo_specs(tr, tn, S, lambda j: j), *_halo_specs(tr, tn, S, lambda j: j + nj),
                  wspec(0), wspec(nj), bspec(0), bspec(nj), tile],
        out_specs=[tile, tile, wspec(0), wspec(0), bspec(0), bspec(0)],
        out_shape=[jax.ShapeDtypeStruct((S, F), BF16), jax.ShapeDtypeStruct((S, F), BF16),
                   jax.ShapeDtypeStruct((3, F), F32), jax.ShapeDtypeStruct((3, F), F32),
                   jax.ShapeDtypeStruct((1, F), F32), jax.ShapeDtypeStruct((1, F), F32)],
        compiler_params=_cp(("parallel", "arbitrary")),
    )(u, u, u, u, u, u, conv_w, conv_w, conv_b, conv_b, da)
    dcg, dcv, dwg, dwv, dbg, dbv = outs
    return dcg, dcv, jnp.concatenate([dwg, dwv], axis=1), jnp.concatenate([dbg, dbv], axis=1)


def _conv_transpose(dc, w, name):
    S, Fx = dc.shape
    tr, tn = _ffn_tiles(S, Fx)
    nj, ni = Fx // tn, S // tr

    def body(m, p, n, w_ref, o_ref):
        up, mid, dn = _halo_rows(m, p, n, pl.program_id(1), ni)
        o_ref[...] = (w_ref[0:1, :] * dn + w_ref[1:2, :] * mid + w_ref[2:3, :] * up).astype(o_ref.dtype)

    return pl.pallas_call(
        body, name=name, grid=(nj, ni),
        in_specs=[*_halo_specs(tr, tn, S, lambda j: j), pl.BlockSpec((3, tn), lambda j, i: (0, j))],
        out_specs=pl.BlockSpec((tr, tn), lambda j, i: (i, j)),
        out_shape=jax.ShapeDtypeStruct((S, Fx), BF16), compiler_params=_cp(("parallel", "parallel")),
    )(dc, dc, dc, w)


def _ffn_fwd(x, norm_g, w_up, conv_w, conv_b, w_down, tag):
    hn = _rmsnorm_fwd(x, norm_g, BF16, f"ffn_norm_{tag}")
    u = _mm(hn, w_up, out_dtype=BF16, name=f"ffn_up_{tag}")
    a = _conv_gate_fwd(u, conv_w, conv_b, f"ffn_gate_{tag}")
    x_new = _mm(a, w_down, name=f"ffn_down_{tag}", epi=_add_epi, epi_in=(x,))
    return x_new, (x, hn, u, a)


def _ffn_bwd(dres, saved, norm_g, w_up, conv_w, conv_b, w_down, tag):
    x, hn, u, a = saved
    dres, dres_b = dres
    da = _mm(dres_b, w_down, tb=True, out_dtype=BF16, name=f"ffn_da_{tag}")
    dw_down = _mm(a, dres_b, ta=True, name=f"ffn_dwdown_{tag}")
    dcg, dcv, dconv_w, dconv_b = _conv_gate_bwd(u, da, conv_w, conv_b, f"ffn_gate_bwd_{tag}")
    dc = jnp.concatenate([dcg, dcv], axis=1)
    du = _conv_transpose(dc, conv_w, f"ffn_convt_{tag}")
    dhn = _mm(du, w_up, tb=True, name=f"ffn_dhn_{tag}")
    dw_up = _mm(hn, du, ta=True, name=f"ffn_dwup_{tag}")
    dres, dg = _rmsnorm_bwd(x, norm_g, dhn, dres, f"ffn_norm_bwd_{tag}")
    return dres, (dg, dw_up, dconv_w, dconv_b, dw_down)


ATTN_KEYS = 3 * ATTN_BLOCK


def _attn_window(i, S):
    ks = pl.multiple_of(jnp.clip((i - 1) * ATTN_BLOCK, 0, S - ATTN_KEYS), ATTN_BLOCK)
    qpos = i * ATTN_BLOCK + lax.broadcasted_iota(jnp.int32, (ATTN_BLOCK, ATTN_KEYS), 0)
    kpos = ks + lax.broadcasted_iota(jnp.int32, (ATTN_BLOCK, ATTN_KEYS), 1)
    arel = jnp.abs(kpos - qpos)
    return ks, arel.astype(F32), arel <= ATTN_BLOCK


def _attn_probs(q, k, slope, sink, arel, valid):
    s = lax.dot_general(q, k, (((1,), (1,)), ((), ())), preferred_element_type=F32) * (ATTN_DH ** -0.5)
    s = jnp.where(valid, s - slope * arel, -jnp.inf)
    m = jnp.maximum(jnp.max(s, axis=-1, keepdims=True), sink)
    p = jnp.exp(s - m)
    es = jnp.exp(sink - m)
    inv = 1.0 / (jnp.sum(p, axis=-1, keepdims=True) + es)
    return p * inv, es * inv


def _attn_specs(S, D):
    H = D // ATTN_DH
    KVW = (H // ATTN_GROUP) * ATTN_DH
    q_spec = pl.BlockSpec((ATTN_BLOCK, D), lambda i: (i, 0))
    k_spec = pl.BlockSpec((S, KVW), lambda i: (0, D // KVW))
    v_spec = pl.BlockSpec((S, KVW), lambda i: (0, D // KVW + 1))
    return H, KVW, q_spec, k_spec, v_spec


def _attn_fwd(qkv, sink, name):
    S = qkv.shape[0]
    D = qkv.shape[1] * ATTN_GROUP // (ATTN_GROUP + 2)
    H, KVW, q_spec, k_spec, v_spec = _attn_specs(S, D)

    def body(q_ref, k_ref, v_ref, sink_ref, o_ref):
        ks, arel, valid = _attn_window(pl.program_id(0), S)
        for kvh in range(H // ATTN_GROUP):
            cols = slice(kvh * ATTN_DH, (kvh + 1) * ATTN_DH)
            k = k_ref[pl.ds(ks, ATTN_KEYS), cols]
            v = v_ref[pl.ds(ks, ATTN_KEYS), cols]
            for g in range(ATTN_GROUP):
                h = kvh * ATTN_GROUP + g
                hc = slice(h * ATTN_DH, (h + 1) * ATTN_DH)
                p, _ = _attn_probs(q_ref[:, hc], k, 2.0 ** (-8.0 * (h + 1) / H), sink_ref[h], arel, valid)
                o_ref[:, hc] = jnp.dot(p.astype(BF16), v, preferred_element_type=F32).astype(o_ref.dtype)

    return pl.pallas_call(
        body, name=name, grid=(S // ATTN_BLOCK,),
        in_specs=[q_spec, k_spec, v_spec, pl.BlockSpec(memory_space=pltpu.SMEM)],
        out_specs=q_spec, out_shape=jax.ShapeDtypeStruct((S, D), BF16),
        compiler_params=_cp(("parallel",)),
    )(qkv, qkv, qkv, sink)


def _attn_bwd(qkv, sink, do, name):
    S = qkv.shape[0]
    D = qkv.shape[1] * ATTN_GROUP // (ATTN_GROUP + 2)
    H, KVW, q_spec, k_spec, v_spec = _attn_specs(S, D)
    scale = ATTN_DH ** -0.5

    def body(q_ref, k_ref, v_ref, sink_ref, do_ref, dq_ref, dk_ref, dv_ref, ds_ref):
        @pl.when(pl.program_id(0) == 0)
        def _():
            dk_ref[...] = jnp.zeros_like(dk_ref)
            dv_ref[...] = jnp.zeros_like(dv_ref)
            ds_ref[...] = jnp.zeros_like(ds_ref)

        ks, arel, valid = _attn_window(pl.program_id(0), S)
        rows = pl.ds(ks, ATTN_KEYS)
        for kvh in range(H // ATTN_GROUP):
            cols = slice(kvh * ATTN_DH, (kvh + 1) * ATTN_DH)
            k = k_ref[rows, cols]
            v = v_ref[rows, cols]
            dk = jnp.zeros((ATTN_KEYS, ATTN_DH), F32)
            dv = jnp.zeros((ATTN_KEYS, ATTN_DH), F32)
            for g in range(ATTN_GROUP):
                h = kvh * ATTN_GROUP + g
                hc = slice(h * ATTN_DH, (h + 1) * ATTN_DH)
                q = q_ref[:, hc]
                d_o = do_ref[:, hc]
                p, p_sink = _attn_probs(q, k, 2.0 ** (-8.0 * (h + 1) / H), sink_ref[h], arel, valid)
                dp = lax.dot_general(d_o, v, (((1,), (1,)), ((), ())), preferred_element_type=F32)
                delta = jnp.sum(p * dp, axis=-1, keepdims=True)
                dsc = (p * (dp - delta)).astype(BF16)
                ds_ref[:, h:h + 1] += -p_sink * delta
                dq_ref[:, hc] = (jnp.dot(dsc, k, preferred_element_type=F32) * scale).astype(dq_ref.dtype)
                dk += lax.dot_general(dsc, q, (((0,), (0,)), ((), ())), preferred_element_type=F32)
                dv += lax.dot_general(p.astype(BF16), d_o, (((0,), (0,)), ((), ())), preferred_element_type=F32)
            dk_ref[rows, cols] += dk * scale
            dv_ref[rows, cols] += dv

    kv_out = pl.BlockSpec((S, KVW), lambda i: (0, 0))
    return pl.pallas_call(
        body, name=name, grid=(S // ATTN_BLOCK,),
        in_specs=[q_spec, k_spec, v_spec, pl.BlockSpec(memory_space=pltpu.SMEM), q_spec],
        out_specs=[q_spec, kv_out, kv_out, pl.BlockSpec((ATTN_BLOCK, H), lambda i: (0, 0))],
        out_shape=[jax.ShapeDtypeStruct((S, D), BF16), jax.ShapeDtypeStruct((S, KVW), F32),
                   jax.ShapeDtypeStruct((S, KVW), F32), jax.ShapeDtypeStruct((ATTN_BLOCK, H), F32)],
        compiler_params=_cp(("arbitrary",)),
    )(qkv, qkv, qkv, sink, do)


def _attn_layer_fwd(x, norm_g, w_qkv, w_o, sink, tag):
    hn = _rmsnorm_fwd(x, norm_g, BF16, f"attn_norm_{tag}")
    qkv = _mm(hn, w_qkv, out_dtype=BF16, name=f"attn_qkv_{tag}")
    o = _attn_fwd(qkv, sink, f"attn_core_{tag}")
    x_new = _mm(o, w_o, name=f"attn_out_{tag}", epi=_add_epi, epi_in=(x,))
    return x_new, (x, hn, qkv, o)


def _attn_layer_bwd(dres, saved, norm_g, w_qkv, w_o, sink, tag):
    x, hn, qkv, o = saved
    dres, dres_b = dres
    do = _mm(dres_b, w_o, tb=True, out_dtype=BF16, name=f"attn_do_{tag}")
    dw_o = _mm(o, dres_b, ta=True, name=f"attn_dwo_{tag}")
    dq, dk, dv, dsink = _attn_bwd(qkv, sink, do, f"attn_core_bwd_{tag}")
    dqkv = jnp.concatenate([dq, dk.astype(BF16), dv.astype(BF16)], axis=1)
    dhn = _mm(dqkv, w_qkv, tb=True, name=f"attn_dhn_{tag}")
    dw_qkv = _mm(hn, dqkv, ta=True, name=f"attn_dwqkv_{tag}")
    dres, dg = _rmsnorm_bwd(x, norm_g, dhn, dres, f"attn_norm_bwd_{tag}")
    return dres, (dg, dw_qkv, dw_o, jnp.sum(dsink, axis=0))


MLA_W = 2 * LANES
MLA_DPAD = 2 * MLA_LORA + LANES
MLA_SCALE = (MLA_NOPE + MLA_ROPE) ** -0.5
LOG2E = math.log2(math.e)
LN2 = math.log(2.0)


def _rope_tables(S):
    half = MLA_ROPE // 2
    pos = jnp.arange(S, dtype=F32)
    inv = ROPE_THETA ** (-jnp.arange(half, dtype=F32) / half)
    ang = pos[:, None] * inv[None, :]
    cos, sin = jnp.cos(ang), jnp.sin(ang)
    z = jnp.zeros((S, LANES - 2 * half), F32)
    zh = jnp.zeros((S, half), F32)
    return (jnp.concatenate([cos, cos, z], axis=1), jnp.concatenate([-sin, zh, z], axis=1),
            jnp.concatenate([zh, sin, z], axis=1))


def _rope(t, ca, sb, sc):
    return t * ca + pltpu.roll(t, 96, 1) * sb + pltpu.roll(t, 32, 1) * sc


def _rope_t(d, ca, sb, sc):
    return d * ca + pltpu.roll(d * sb, 32, 1) + pltpu.roll(d * sc, 96, 1)


def _rms(xf, g):
    return (xf * lax.rsqrt(jnp.mean(xf * xf, axis=-1, keepdims=True) + RMS_EPS)) * g


def _mla_prep(d, qn, kvn, tabs, name):
    S = d.shape[0]
    tr = _pick(S, (256, 128, 8))
    L = MLA_LORA

    def body(d_ref, qn_ref, kvn_ref, ca, sb, sc, cq_ref, ckv_ref, kr_ref):
        cq_ref[...] = _rms(d_ref[:, :L], qn_ref[...]).astype(BF16)
        ckv_ref[...] = _rms(d_ref[:, L:2 * L], kvn_ref[...]).astype(BF16)
        kr_ref[...] = _rope(d_ref[:, 2 * L:], ca[...], sb[...], sc[...]).astype(BF16)

    row = lambda w: pl.BlockSpec((tr, w), lambda i: (i, 0))
    vec = pl.BlockSpec((1, L), lambda i: (0, 0))
    return pl.pallas_call(
        body, name=name, grid=(S // tr,),
        in_specs=[row(MLA_DPAD), vec, vec, row(LANES), row(LANES), row(LANES)],
        out_specs=[row(L), row(L), row(LANES)],
        out_shape=[jax.ShapeDtypeStruct((S, L), BF16), jax.ShapeDtypeStruct((S, L), BF16),
                   jax.ShapeDtypeStruct((S, LANES), BF16)],
        compiler_params=_cp(("parallel",)),
    )(d, qn, kvn, *tabs)


def _mla_prep_bwd(d, qn, kvn, tabs, dcq, dckv, dkr_h, name):
    S = d.shape[0]
    H = dkr_h.shape[0]
    tr = _pick(S, (256, 128, 8))
    L = MLA_LORA

    def body(d_ref, qn_ref, kvn_ref, ca, sb, sc, dcq_ref, dckv_ref, dkr_ref, dd_ref, dqn_ref, dkvn_ref):
        @pl.when(pl.program_id(0) == 0)
        def _():
            dqn_ref[...] = jnp.zeros_like(dqn_ref)
            dkvn_ref[...] = jnp.zeros_like(dkvn_ref)

        dx, dg = _rms_bwd_math(d_ref[:, :L], qn_ref[...], dcq_ref[...])
        dd_ref[:, :L] = dx.astype(BF16)
        dqn_ref[...] += dg
        dx, dg = _rms_bwd_math(d_ref[:, L:2 * L], kvn_ref[...], dckv_ref[...])
        dd_ref[:, L:2 * L] = dx.astype(BF16)
        dkvn_ref[...] += dg
        dkr = dkr_ref[0]
        for h in range(1, H):
            dkr = dkr + dkr_ref[h]
        dd_ref[:, 2 * L:] = _rope_t(dkr, ca[...], sb[...], sc[...]).astype(BF16)

    row = lambda w: pl.BlockSpec((tr, w), lambda i: (i, 0))
    vec = pl.BlockSpec((1, L), lambda i: (0, 0))
    return pl.pallas_call(
        body, name=name, grid=(S // tr,),
        in_specs=[row(MLA_DPAD), vec, vec, row(LANES), row(LANES), row(LANES), row(L), row(L),
                  pl.BlockSpec((H, tr, LANES), lambda i: (0, i, 0))],
        out_specs=[row(MLA_DPAD), vec, vec],
        out_shape=[jax.ShapeDtypeStruct((S, MLA_DPAD), BF16), jax.ShapeDtypeStruct((1, L), F32),
                   jax.ShapeDtypeStruct((1, L), F32)],
        compiler_params=_cp(("arbitrary",)),
    )(d, qn, kvn, *tabs, dcq, dckv, dkr_h)


def _heads_proj(a, w, out_dtype, name):
    S, K = a.shape
    H, _, n = w.shape
    tm = _pick(S, (1024, 512, 256, 128))
    return _gmm(a, w, grid=(S // tm, H, 1),
                a_spec=pl.BlockSpec((tm, K), lambda m, h, k: (m, 0)),
                b_spec=pl.BlockSpec((None, K, n), lambda m, h, k: (h, 0, 0)),
                o_spec=pl.BlockSpec((None, tm, n), lambda m, h, k: (h, m, 0)),
                out_shape=jax.ShapeDtypeStruct((H, S, n), out_dtype),
                dims=(((1,), (0,)), ((), ())), kax=2, acc_shape=(tm, n), name=name)


def _heads_proj_dx(dy, w, name):
    H, S, n = dy.shape
    K = w.shape[1]
    tm = _pick(S, (1024, 512, 256, 128))
    return _gmm(dy, w, grid=(S // tm, 1, H),
                a_spec=pl.BlockSpec((None, tm, n), lambda m, j, h: (h, m, 0)),
                b_spec=pl.BlockSpec((None, K, n), lambda m, j, h: (h, 0, 0)),
                o_spec=pl.BlockSpec((tm, K), lambda m, j, h: (m, 0)),
                out_shape=jax.ShapeDtypeStruct((S, K), F32),
                dims=(((1,), (1,)), ((), ())), kax=2, acc_shape=(tm, K), name=name)


def _heads_proj_dw(a, dy, name):
    S, K = a.shape
    H, _, n = dy.shape
    tk = _pick(S, (512, 256, 128))
    return _gmm(a, dy, grid=(H, 1, S // tk),
                a_spec=pl.BlockSpec((tk, K), lambda h, j, k: (k, 0)),
                b_spec=pl.BlockSpec((None, tk, n), lambda h, j, k: (h, k, 0)),
                o_spec=pl.BlockSpec((None, K, n), lambda h, j, k: (h, 0, 0)),
                out_shape=jax.ShapeDtypeStruct((H, K, n), F32),
                dims=(((0,), (0,)), ((), ())), kax=2, acc_shape=(K, n), name=name)


def _mla_rope_q(q_ext, tabs, bwd, name):
    H, S, _ = q_ext.shape
    tr = _pick(S, (512, 256, 128, 8))
    mult = 1.0 if bwd else MLA_SCALE * LOG2E

    def body(q_ref, ca, sb, sc, o_ref):
        o_ref[:, :LANES] = (q_ref[:, :LANES].astype(F32) * mult).astype(BF16)
        fn = _rope_t if bwd else _rope
        o_ref[:, LANES:] = (fn(q_ref[:, LANES:].astype(F32), ca[...], sb[...], sc[...]) * mult).astype(BF16)

    blk = pl.BlockSpec((None, tr, MLA_W), lambda i, h: (h, i, 0))
    tab = pl.BlockSpec((tr, LANES), lambda i, h: (i, 0))
    return pl.pallas_call(
        body, name=name, grid=(S // tr, H), in_specs=[blk, tab, tab, tab], out_specs=blk,
        out_shape=jax.ShapeDtypeStruct((H, S, MLA_W), BF16), compiler_params=_cp(("parallel", "parallel")),
    )(q_ext, *tabs)


def _col_to_row(col):
    n = col.shape[0]
    eye = lax.broadcasted_iota(jnp.int32, (n, n), 0) == lax.broadcasted_iota(jnp.int32, (n, n), 1)
    return jnp.sum(jnp.where(eye, col, 0.0), axis=0, keepdims=True)


def _mla_flash_fwd(q, kv, kr, name):
    H, S, _ = q.shape
    tq = _pick(S, (512, 256, 128))
    tk = _pick(S, (512, 256, 128))

    def body(q_ref, kv_ref, kr_ref, o_ref, lse_ref, kbuf, vbuf):
        @pl.when(pl.program_id(1) == 0)
        def _():
            kbuf[:, :LANES] = kv_ref[:, :LANES]
            kbuf[:, LANES:] = kr_ref[...]
            vbuf[:, :LANES] = kv_ref[:, LANES:]
            vbuf[:, LANES:] = jnp.ones((S, LANES), BF16)

        qv = q_ref[...]

        def step(c, carry):
            m, acc = carry
            rows = pl.ds(pl.multiple_of(c * tk, tk), tk)
            s = lax.dot_general(qv, kbuf[rows, :], (((1,), (1,)), ((), ())), preferred_element_type=F32)
            m_new = jnp.maximum(m, jnp.max(s, axis=-1, keepdims=True))
            p = jnp.exp2(s - m_new).astype(BF16)
            acc = jnp.exp2(m - m_new) * acc + jnp.dot(p, vbuf[rows, :], preferred_element_type=F32)
            return m_new, acc

        init = (jnp.full((tq, 1), -jnp.inf, F32), jnp.zeros((tq, MLA_W), F32))
        m, acc = lax.fori_loop(0, S // tk, step, init)
        l = acc[:, LANES:LANES + 1]
        o_ref[...] = (acc[:, :LANES] / l).astype(o_ref.dtype)
        lse_ref[...] = _col_to_row(m + jnp.log2(l))

    return pl.pallas_call(
        body, name=name, grid=(H, S // tq),
        in_specs=[pl.BlockSpec((None, tq, MLA_W), lambda h, i: (h, i, 0)),
                  pl.BlockSpec((None, S, MLA_W), lambda h, i: (h, 0, 0)),
                  pl.BlockSpec((S, LANES), lambda h, i: (0, 0))],
        out_specs=[pl.BlockSpec((tq, MLA_V), lambda h, i: (i, h)),
                   pl.BlockSpec((None, 1, tq), lambda h, i: (h, 0, i))],
        out_shape=[jax.ShapeDtypeStruct((S, H * MLA_V), BF16), jax.ShapeDtypeStruct((H, 1, S), F32)],
        scratch_shapes=[pltpu.VMEM((S, MLA_W), BF16), pltpu.VMEM((S, MLA_W), BF16)],
        compiler_params=_cp(("parallel", "arbitrary")),
    )(q, kv, kr)


def _mla_delta(o, do, H, name):
    S = o.shape[0]
    tq = _pick(S, (512, 256, 128))

    def body(o_ref, do_ref, d_ref):
        prod = o_ref[...].astype(F32) * do_ref[...].astype(F32)
        d_ref[...] = _col_to_row(jnp.sum(prod, axis=-1, keepdims=True))

    blk = pl.BlockSpec((tq, MLA_V), lambda i, h: (i, h))
    return pl.pallas_call(
        body, name=name, grid=(S // tq, H), in_specs=[blk, blk],
        out_specs=pl.BlockSpec((None, 1, tq), lambda i, h: (h, 0, i)),
        out_shape=jax.ShapeDtypeStruct((H, 1, S), F32), compiler_params=_cp(("parallel", "parallel")),
    )(o, do)


def _mla_flash_bwd(q, kv, kr, do, lse, delta, name):
    H, S, _ = q.shape
    tq = _pick(S, (512, 256, 128))
    tkv = _pick(S, (512, 256, 128))

    def body(q_ref, kv_ref, kr_ref, do_ref, lse_ref, dl_ref, dq_ref, dkv_ref, dkr_ref):
        @pl.when(pl.program_id(1) == 0)
        def _():
            dq_ref[...] = jnp.zeros_like(dq_ref)

        v = kv_ref[:, LANES:]
        k = jnp.concatenate([kv_ref[:, :LANES], kr_ref[...]], axis=1)

        def step(c, carry):
            dk, dv = carry
            start = pl.multiple_of(c * tq, tq)
            rows = pl.ds(start, tq)
            qv = q_ref[rows, :]
            d_o = do_ref[rows, :]
            s_t = lax.dot_general(k, qv, (((1,), (1,)), ((), ())), preferred_element_type=F32)
            p_t = jnp.exp2(s_t - lse_ref[:, rows])
            dv = dv + jnp.dot(p_t.astype(BF16), d_o, preferred_element_type=F32)
            dp_t = lax.dot_general(v, d_o, (((1,), (1,)), ((), ())), preferred_element_type=F32)
            ds_t = (p_t * (dp_t - dl_ref[:, rows])).astype(BF16)
            dk = dk + jnp.dot(ds_t, qv, preferred_element_type=F32)
            dq_ref[rows, :] += lax.dot_general(ds_t, k, (((0,), (0,)), ((), ())),
                                               preferred_element_type=F32) * MLA_SCALE
            return dk, dv

        dk, dv = lax.fori_loop(0, S // tq, step, (jnp.zeros((tkv, MLA_W), F32), jnp.zeros((tkv, MLA_V), F32)))
        dkv_ref[:, :LANES] = (dk[:, :LANES] * LN2).astype(BF16)
        dkv_ref[:, LANES:] = dv.astype(BF16)
        dkr_ref[...] = dk[:, LANES:] * LN2

    stat = pl.BlockSpec((None, 1, S), lambda h, j: (h, 0, 0))
    return pl.pallas_call(
        body, name=name, grid=(H, S // tkv),
        in_specs=[pl.BlockSpec((None, S, MLA_W), lambda h, j: (h, 0, 0)),
                  pl.BlockSpec((None, tkv, MLA_W), lambda h, j: (h, j, 0)),
                  pl.BlockSpec((tkv, LANES), lambda h, j: (j, 0)),
                  pl.BlockSpec((S, MLA_V), lambda h, j: (0, h)), stat, stat],
        out_specs=[pl.BlockSpec((None, S, MLA_W), lambda h, j: (h, 0, 0)),
                   pl.BlockSpec((None, tkv, MLA_W), lambda h, j: (h, j, 0)),
                   pl.BlockSpec((None, tkv, LANES), lambda h, j: (h, j, 0))],
        out_shape=[jax.ShapeDtypeStruct((H, S, MLA_W), F32), jax.ShapeDtypeStruct((H, S, MLA_W), BF16),
                   jax.ShapeDtypeStruct((H, S, LANES), F32)],
        compiler_params=_cp(("parallel", "arbitrary")),
    )(q, kv, kr, do, lse, delta)


def _mla_weights(w_dqkv, w_uq, w_ukv):
    H = MLA_HEADS
    wd = jnp.pad(w_dqkv, ((0, 0), (0, MLA_DPAD - w_dqkv.shape[1])))
    wq = w_uq.reshape(MLA_LORA, H, MLA_NOPE + MLA_ROPE)
    wq = jnp.pad(wq, ((0, 0), (0, 0), (0, MLA_W - wq.shape[2]))).transpose(1, 0, 2)
    wkv = w_ukv.reshape(MLA_LORA, H, MLA_NOPE + MLA_V).transpose(1, 0, 2)
    return wd, wq, wkv


def _mla_layer_fwd(x, norm_g, wd, wq, wkv, w_o, qn, kvn, tag):
    S = x.shape[0]
    tabs = _rope_tables(S)
    hn = _rmsnorm_fwd(x, norm_g, BF16, f"mla_norm_{tag}")
    d = _mm(hn, wd, name=f"mla_down_{tag}")
    cq, ckv, kr = _mla_prep(d, qn, kvn, tabs, f"mla_prep_{tag}")
    q = _mla_rope_q(_heads_proj(cq, wq, F32, f"mla_uq_{tag}"), tabs, False, f"mla_ropeq_{tag}")
    kv = _heads_proj(ckv, wkv, BF16, f"mla_ukv_{tag}")
    o, lse = _mla_flash_fwd(q, kv, kr, f"mla_flash_{tag}")
    x_new = _mm(o, w_o, name=f"mla_out_{tag}", epi=_add_epi, epi_in=(x,))
    return x_new, (x, hn, d, cq, ckv, kr, q, kv, o, lse)


def _mla_layer_bwd(dres, saved, norm_g, wd, wq, wkv, w_o, qn, kvn, tag):
    x, hn, d, cq, ckv, kr, q, kv, o, lse = saved
    S = x.shape[0]
    H = MLA_HEADS
    tabs = _rope_tables(S)
    dres, dres_b = dres
    do = _mm(dres_b, w_o, tb=True, out_dtype=BF16, name=f"mla_do_{tag}")
    dw_o = _mm(o, dres_b, ta=True, name=f"mla_dwo_{tag}")
    delta = _mla_delta(o, do, H, f"mla_delta_{tag}")
    dq, dkv, dkr_h = _mla_flash_bwd(q, kv, kr, do, lse, delta, f"mla_flash_bwd_{tag}")
    dq_ext = _mla_rope_q(dq, tabs, True, f"mla_ropeq_bwd_{tag}")
    dwq = _heads_proj_dw(cq, dq_ext, f"mla_dwuq_{tag}")
    dcq = _heads_proj_dx(dq_ext, wq, f"mla_dcq_{tag}")
    dwkv = _heads_proj_dw(ckv, dkv, f"mla_dwukv_{tag}")
    dckv = _heads_proj_dx(dkv, wkv, f"mla_dckv_{tag}")
    dd, dqn, dkvn = _mla_prep_bwd(d, qn, kvn, tabs, dcq, dckv, dkr_h, f"mla_prep_bwd_{tag}")
    dhn = _mm(dd, wd, tb=True, name=f"mla_dhn_{tag}")
    dwd = _mm(hn, dd, ta=True, name=f"mla_dwd_{tag}")
    dres, dg = _rmsnorm_bwd(x, norm_g, dhn, dres, f"mla_norm_bwd_{tag}")
    dw_dqkv = dwd[:, :2 * MLA_LORA + MLA_ROPE]
    dw_uq = dwq.transpose(1, 0, 2)[:, :, :MLA_NOPE + MLA_ROPE].reshape(MLA_LORA, -1)
    dw_ukv = dwkv.transpose(1, 0, 2).reshape(MLA_LORA, -1)
    return dres, (dg, dw_dqkv, dqn, dkvn, dw_uq, dw_ukv, dw_o)


S5_CB = LANES
S5_SB = (S5_CB // SSM_GROUP_CH) * SSM_STATE
S5_ROWS = 1024


def _s5_disc(a_re, a_im, ls, b_re, b_im):
    step = jnp.exp(ls)
    mag = jnp.exp(step * a_re)
    lb_re = mag * jnp.cos(step * a_im)
    lb_im = mag * jnp.sin(step * a_im)
    n_re, n_im = lb_re - 1.0, lb_im
    den = a_re * a_re + a_im * a_im
    coef_re = (n_re * a_re + n_im * a_im) / den
    coef_im = (n_im * a_re - n_re * a_im) / den
    return lb_re, lb_im, coef_re * b_re - coef_im * b_im, coef_re * b_im + coef_im * b_re


def _s5_disc_fwd(a_re, a_im, ls, b_re, b_im, name):
    GN = a_re.shape[-1]

    def body(ar, ai, l, br, bi, o_lr, o_li, o_br, o_bi):
        for o, v in zip((o_lr, o_li, o_br, o_bi), _s5_disc(ar[...], ai[...], l[...], br[...], bi[...])):
            o[...] = v

    vec = pl.BlockSpec((None, 1, GN), lambda d: (d, 0, 0))
    mat = pl.BlockSpec((None, SSM_GROUP_CH, GN), lambda d: (d, 0, 0))
    sv = jax.ShapeDtypeStruct(a_re.shape, F32)
    sm = jax.ShapeDtypeStruct(b_re.shape, F32)
    return pl.pallas_call(body, name=name, grid=(2,), in_specs=[vec, vec, vec, mat, mat],
                          out_specs=[vec, vec, mat, mat], out_shape=[sv, sv, sm, sm],
                          compiler_params=_cp(("parallel",)))(a_re, a_im, ls, b_re, b_im)


def _s5_disc_bwd(a_re, a_im, ls, b_re, b_im, d_lr, d_li, d_br, d_bi, name):
    GN = a_re.shape[-1]

    def body(ar, ai, l, br, bi, g_lr, g_li, g_br, g_bi, o_ar, o_ai, o_l, o_br, o_bi):
        _, vjp = jax.vjp(_s5_disc, ar[...], ai[...], l[...], br[...], bi[...])
        for o, v in zip((o_ar, o_ai, o_l, o_br, o_bi), vjp((g_lr[...], g_li[...], g_br[...], g_bi[...]))):
            o[...] = v

    vec = pl.BlockSpec((None, 1, GN), lambda d: (d, 0, 0))
    mat = pl.BlockSpec((None, SSM_GROUP_CH, GN), lambda d: (d, 0, 0))
    sv = jax.ShapeDtypeStruct(a_re.shape, F32)
    sm = jax.ShapeDtypeStruct(b_re.shape, F32)
    return pl.pallas_call(body, name=name, grid=(2,), in_specs=[vec, vec, vec, mat, mat, vec, vec, mat, mat],
                          out_specs=[vec, vec, vec, mat, mat], out_shape=[sv, sv, sv, sm, sm],
                          compiler_params=_cp(("parallel",)))(a_re, a_im, ls, b_re, b_im, d_lr, d_li, d_br, d_bi)


def _cmul(ar, ai, br, bi):
    return ar * br - ai * bi, ar * bi + ai * br


def _segment_carries(lr, li, er, ei, n_steps, reverse):
    pr, pi = lr, li
    for _ in range(int(math.log2(n_steps))):
        pr, pi = _cmul(pr, pi, pr, pi)
    row = lax.broadcasted_iota(jnp.int32, er.shape, 0)
    edge = (SSM_SEGMENTS - 1) if reverse else 0
    shift = (SSM_SEGMENTS - 1) if reverse else 1
    cr = jnp.zeros_like(er)
    ci = jnp.zeros_like(ei)
    for _ in range(SSM_SEGMENTS - 1):
        tr_, ti_ = _cmul(pr, pi, cr, ci)
        cr = jnp.where(row == edge, 0.0, pltpu.roll(tr_ + er, shift, 0))
        ci = jnp.where(row == edge, 0.0, pltpu.roll(ti_ + ei, shift, 0))
    return cr, ci


def _s5_geometry(S, D):
    assert S % SSM_SEGMENTS == 0 and D % S5_CB == 0
    n_steps = S // SSM_SEGMENTS
    assert n_steps & (n_steps - 1) == 0, "segment length must be a power of two"
    rows = min(S5_ROWS, S)
    return n_steps, rows, S // rows, D // S5_CB


def _s5_scan(u, b_re, b_im, c_re, c_im, lam_re, lam_im, ends, descending, name):
    S, D = u.shape
    n_steps, rows, nch, ncb = _s5_geometry(S, D)
    full = ends is not None
    GN = ncb * S5_SB

    def body(*refs):
        if full:
            (u_ref, br_ref, bi_ref, cr_ref, ci_ref, lr_ref, li_ref, er_ref, ei_ref,
             xr_ref, xi_ref, y_ref, st_r, st_i, buf_r, buf_i) = refs
        else:
            u_ref, br_ref, bi_ref, lr_ref, li_ref, er_ref, ei_ref, st_r, st_i, buf_r, buf_i = refs
        lr = jnp.broadcast_to(lr_ref[...], (SSM_SEGMENTS, S5_SB))
        li = jnp.broadcast_to(li_ref[...], (SSM_SEGMENTS, S5_SB))

        @pl.when(pl.program_id(1) == 0)
        def _():
            if full:
                st_r[...], st_i[...] = _segment_carries(lr, li, er_ref[...], ei_ref[...], n_steps, descending)
            else:
                st_r[...] = jnp.zeros_like(st_r)
                st_i[...] = jnp.zeros_like(st_i)

        ub = u_ref[...].astype(BF16)
        buf_r[...] = jnp.dot(ub, br_ref[...], preferred_element_type=F32)
        buf_i[...] = jnp.dot(ub, bi_ref[...], preferred_element_type=F32)

        n_it = rows // SSM_SEGMENTS

        def step(i, carry):
            sr, si = carry
            i = n_it - 1 - i if descending else i
            r = pl.ds(pl.multiple_of(i * SSM_SEGMENTS, SSM_SEGMENTS), SSM_SEGMENTS)
            if full:
                xr_ref[r, :] = sr
                xi_ref[r, :] = si
            nr = lr * sr - li * si + buf_r[r, :]
            ni = lr * si + li * sr + buf_i[r, :]
            if full:
                buf_r[r, :] = nr
                buf_i[r, :] = ni
            return nr, ni

        sr, si = lax.fori_loop(0, n_it, step, (st_r[...], st_i[...]))
        st_r[...] = sr
        st_i[...] = si
        if full:
            y_ref[...] = (jnp.dot(buf_r[...].astype(BF16), cr_ref[...], preferred_element_type=F32)
                          - jnp.dot(buf_i[...].astype(BF16), ci_ref[...], preferred_element_type=F32))
        else:
            er_ref[...] = sr
            ei_ref[...] = si

    chunk = (lambda c: nch - 1 - c) if descending else (lambda c: c)
    u_spec = pl.BlockSpec((rows, S5_CB), lambda b, c: (chunk(c), b))
    bmat = pl.BlockSpec((None, S5_CB, S5_SB), lambda b, c: (b, 0, 0))
    cmat = pl.BlockSpec((None, S5_SB, S5_CB), lambda b, c: (b, 0, 0))
    lvec = pl.BlockSpec((1, S5_SB), lambda b, c: (0, b))
    evec = pl.BlockSpec((SSM_SEGMENTS, S5_SB), lambda b, c: (0, b))
    xblk = pl.BlockSpec((rows, S5_SB), lambda b, c: (chunk(c), b))
    scratch = [pltpu.VMEM((SSM_SEGMENTS, S5_SB), F32)] * 2 + [pltpu.VMEM((rows, S5_SB), F32)] * 2
    e_shape = jax.ShapeDtypeStruct((SSM_SEGMENTS, GN), F32)
    if full:
        x_shape = jax.ShapeDtypeStruct((S, GN), F32)
        return pl.pallas_call(
            body, name=name, grid=(ncb, nch),
            in_specs=[u_spec, bmat, bmat, cmat, cmat, lvec, lvec, evec, evec],
            out_specs=[xblk, xblk, u_spec], out_shape=[x_shape, x_shape, jax.ShapeDtypeStruct((S, D), F32)],
            scratch_shapes=scratch, compiler_params=_cp(("parallel", "arbitrary")),
        )(u, b_re, b_im, c_re, c_im, lam_re, lam_im, *ends)
    return pl.pallas_call(
        body, name=name, grid=(ncb, nch), in_specs=[u_spec, bmat, bmat, lvec, lvec],
        out_specs=[evec, evec], out_shape=[e_shape, e_shape],
        scratch_shapes=scratch, compiler_params=_cp(("parallel", "arbitrary")),
    )(u, b_re, b_im, lam_re, lam_im)


def _s5_scan_bwd(dy, u, xp, b_re, b_im, c_re, c_im, lam_re, lam_im, starts, descending, name):
    S, D = dy.shape
    n_steps, rows, nch, ncb = _s5_geometry(S, D)
    full = starts is not None
    GN = ncb * S5_SB
    nt = (((1,), (1,)), ((), ()))
    tn = (((0,), (0,)), ((), ()))

    def body(*refs):
        if full:
            (dy_ref, u_ref, xr_ref, xi_ref, br_ref, bi_ref, cr_ref, ci_ref, lr_ref, li_ref, gr_ref, gi_ref,
             du_ref, dbr_ref, dbi_ref, dcr_ref, dci_ref, dlr_ref, dli_ref, st_r, st_i, buf_r, buf_i) = refs
        else:
            dy_ref, cr_ref, ci_ref, lr_ref, li_ref, gr_ref, gi_ref, st_r, st_i, buf_r, buf_i = refs
        lr = jnp.broadcast_to(lr_ref[...], (SSM_SEGMENTS, S5_SB))
        li = jnp.broadcast_to(li_ref[...], (SSM_SEGMENTS, S5_SB))

        @pl.when(pl.program_id(1) == 0)
        def _():
            if full:
                st_r[...], st_i[...] = _segment_carries(lr, -li, gr_ref[...], gi_ref[...], n_steps, descending)
                for r in (dbr_ref, dbi_ref, dcr_ref, dci_ref, dlr_ref, dli_ref):
                    r[...] = jnp.zeros_like(r)
            else:
                st_r[...] = jnp.zeros_like(st_r)
                st_i[...] = jnp.zeros_like(st_i)

        dyb = dy_ref[...].astype(BF16)
        buf_r[...] = lax.dot_general(dyb, cr_ref[...], nt, preferred_element_type=F32)
        buf_i[...] = -lax.dot_general(dyb, ci_ref[...], nt, preferred_element_type=F32)
        n_it = rows // SSM_SEGMENTS

        def step(j, carry):
            gr, gi = carry
            j = n_it - 1 - j if descending else j
            r = pl.ds(pl.multiple_of(j * SSM_SEGMENTS, SSM_SEGMENTS), SSM_SEGMENTS)
            nr = lr * gr + li * gi + buf_r[r, :]
            ni = lr * gi - li * gr + buf_i[r, :]
            if full:
                buf_r[r, :] = nr
                buf_i[r, :] = ni
            return nr, ni

        gr, gi = lax.fori_loop(0, n_it, step, (st_r[...], st_i[...]))
        st_r[...] = gr
        st_i[...] = gi
        if not full:
            gr_ref[...] = gr
            gi_ref[...] = gi
            return
        g_r, g_i = buf_r[...], buf_i[...]
        xr, xi = xr_ref[...], xi_ref[...]
        dlr_ref[...] += jnp.sum(g_r * xr + g_i * xi, axis=0, keepdims=True)
        dli_ref[...] += jnp.sum(g_i * xr - g_r * xi, axis=0, keepdims=True)
        ub = u_ref[...].astype(BF16)
        gb_r, gb_i = g_r.astype(BF16), g_i.astype(BF16)
        du_ref[...] = (lax.dot_general(gb_r, br_ref[...], nt, preferred_element_type=F32)
                       + lax.dot_general(gb_i, bi_ref[...], nt, preferred_element_type=F32))
        dbr_ref[...] += lax.dot_general(ub, gb_r, tn, preferred_element_type=F32)
        dbi_ref[...] += lax.dot_general(ub, gb_i, tn, preferred_element_type=F32)
        lr_, li_ = lr_ref[...], li_ref[...]
        x_r = lr_ * xr - li_ * xi + jnp.dot(ub, br_ref[...], preferred_element_type=F32)
        x_i = lr_ * xi + li_ * xr + jnp.dot(ub, bi_ref[...], preferred_element_type=F32)
        dcr_ref[...] += lax.dot_general(x_r.astype(BF16), dyb, tn, preferred_element_type=F32)
        dci_ref[...] -= lax.dot_general(x_i.astype(BF16), dyb, tn, preferred_element_type=F32)

    rev = (lambda c: nch - 1 - c) if descending else (lambda c: c)
    u_spec = pl.BlockSpec((rows, S5_CB), lambda b, c: (rev(c), b))
    bmat = pl.BlockSpec((None, S5_CB, S5_SB), lambda b, c: (b, 0, 0))
    cmat = pl.BlockSpec((None, S5_SB, S5_CB), lambda b, c: (b, 0, 0))
    lvec = pl.BlockSpec((1, S5_SB), lambda b, c: (0, b))
    evec = pl.BlockSpec((SSM_SEGMENTS, S5_SB), lambda b, c: (0, b))
    xblk = pl.BlockSpec((rows, S5_SB), lambda b, c: (rev(c), b))
    scratch = [pltpu.VMEM((SSM_SEGMENTS, S5_SB), F32)] * 2 + [pltpu.VMEM((rows, S5_SB), F32)] * 2
    e_shape = jax.ShapeDtypeStruct((SSM_SEGMENTS, GN), F32)
    if full:
        return pl.pallas_call(
            body, name=name, grid=(ncb, nch),
            in_specs=[u_spec, u_spec, xblk, xblk, bmat, bmat, cmat, cmat, lvec, lvec, evec, evec],
            out_specs=[u_spec, bmat, bmat, cmat, cmat, lvec, lvec],
            out_shape=[jax.ShapeDtypeStruct((S, D), F32), jax.ShapeDtypeStruct(b_re.shape, F32),
                       jax.ShapeDtypeStruct(b_re.shape, F32), jax.ShapeDtypeStruct(c_re.shape, F32),
                       jax.ShapeDtypeStruct(c_re.shape, F32), jax.ShapeDtypeStruct((1, GN), F32),
                       jax.ShapeDtypeStruct((1, GN), F32)],
            scratch_shapes=scratch, compiler_params=_cp(("parallel", "arbitrary")),
        )(dy, u, *xp, b_re, b_im, c_re, c_im, lam_re, lam_im, *starts)
    return pl.pallas_call(
        body, name=name, grid=(ncb, nch), in_specs=[u_spec, cmat, cmat, lvec, lvec],
        out_specs=[evec, evec], out_shape=[e_shape, e_shape],
        scratch_shapes=scratch, compiler_params=_cp(("parallel", "arbitrary")),
    )(dy, c_re, c_im, lam_re, lam_im)


def _s5_perm(t):
    S, D = t.shape
    return t.reshape(SSM_SEGMENTS, S // SSM_SEGMENTS, D).transpose(1, 0, 2).reshape(S, D)


def _s5_unperm(t):
    S, D = t.shape
    return t.reshape(S // SSM_SEGMENTS, SSM_SEGMENTS, D).transpose(1, 0, 2).reshape(S, D)


def _s5_blockdiag_b(bb, ncb):
    gpb = S5_CB // SSM_GROUP_CH
    t = bb.reshape(SSM_GROUP_CH, ncb, gpb, SSM_STATE)
    return jnp.einsum('cbgn,gh->bgchn', t, jnp.eye(gpb, dtype=bb.dtype)).reshape(ncb, S5_CB, S5_SB)


def _s5_blockdiag_b_t(dblk):
    ncb = dblk.shape[0]
    gpb = S5_CB // SSM_GROUP_CH
    t = dblk.reshape(ncb, gpb, SSM_GROUP_CH, gpb, SSM_STATE)
    return jnp.einsum('bgchn,gh->cbgn', t, jnp.eye(gpb, dtype=dblk.dtype)).reshape(SSM_GROUP_CH, -1)


def _s5_blockdiag_c(c, ncb):
    gpb = S5_CB // SSM_GROUP_CH
    t = c.reshape(ncb, gpb, SSM_GROUP_CH, SSM_STATE)
    return jnp.einsum('bgcn,gh->bgnhc', t, jnp.eye(gpb, dtype=c.dtype)).reshape(ncb, S5_SB, S5_CB)


def _s5_blockdiag_c_t(dblk):
    ncb = dblk.shape[0]
    gpb = S5_CB // SSM_GROUP_CH
    t = dblk.reshape(ncb, gpb, SSM_STATE, gpb, SSM_GROUP_CH)
    return jnp.einsum('bgnhc,gh->bgcn', t, jnp.eye(gpb, dtype=dblk.dtype)).reshape(-1, SSM_GROUP_CH, SSM_STATE)


_GELU_C = math.sqrt(2.0 / math.pi)


def _gelu(y):
    return y * (0.5 * (1.0 + jnp.tanh(_GELU_C * (y + 0.044715 * (y * y * y)))))


def _gelu_grad(y):
    t = jnp.tanh(_GELU_C * (y + 0.044715 * (y * y * y)))
    return 0.5 * (1.0 + t) + 0.5 * y * (1.0 - t * t) * (_GELU_C * (1.0 + 3.0 * 0.044715 * y * y))


def _rowwise(fn, ins, outs, name, acc=()):
    S, D = next(a.shape for a in ins if a.shape[0] != 1)
    tr = _pick(S, (256, 128, 8))
    row = pl.BlockSpec((tr, D), lambda i: (i, 0))
    vec = pl.BlockSpec((1, D), lambda i: (0, 0))
    n_in = len(ins)

    def body(*refs):
        res = fn(*[r[...] for r in refs[:n_in]])
        for k, (o, v) in enumerate(zip(refs[n_in:], res)):
            if k in acc:
                @pl.when(pl.program_id(0) == 0)
                def _():
                    o[...] = jnp.zeros_like(o)
                o[...] += jnp.sum(v, axis=0, keepdims=True)
            else:
                o[...] = v.astype(o.dtype)

    return pl.pallas_call(
        body, name=name, grid=(S // tr,), in_specs=[vec if a.shape[0] == 1 else row for a in ins],
        out_specs=[vec if k in acc else row for k in range(len(outs))],
        out_shape=[jax.ShapeDtypeStruct((1, D) if k in acc else (S, D), dt) for k, dt in enumerate(outs)],
        compiler_params=_cp(("arbitrary",) if acc else ("parallel",)),
    )(*ins)


def _s5_params(p):
    G, N = p["a_re"].shape[1:]
    vec = lambda a: a.reshape(2, 1, G * N)
    ls = jnp.broadcast_to(p["log_step"][:, :, None], (2, G, N))
    bt = lambda b: b.transpose(0, 3, 1, 2).reshape(2, SSM_GROUP_CH, G * N)
    return vec(p["a_re"]), vec(p["a_im"]), vec(ls), bt(p["b_re"]), bt(p["b_im"])


def _s5_layer_fwd(x, norm_g, p, w_glu, tag):
    S, D = x.shape
    ncb = D // S5_CB
    xp = _s5_perm(x)
    hn = _rmsnorm_fwd(xp, norm_g, F32, f"s5_norm_{tag}")
    raw = _s5_params(p)
    lam_r, lam_i, bb_r, bb_i = _s5_disc_fwd(*raw, f"s5_disc_{tag}")
    dirs = []
    ys = []
    for dirn in range(2):
        mats = (_s5_blockdiag_b(bb_r[dirn], ncb).astype(BF16), _s5_blockdiag_b(bb_i[dirn], ncb).astype(BF16),
                _s5_blockdiag_c(p["c_re"][dirn], ncb).astype(BF16), _s5_blockdiag_c(p["c_im"][dirn], ncb).astype(BF16))
        lam = (lam_r[dirn], lam_i[dirn])
        ends = _s5_scan(hn, mats[0], mats[1], None, None, *lam, None, dirn == 1, f"s5_ends_{tag}_{dirn}")
        xr, xi, y = _s5_scan(hn, *mats, *lam, ends, dirn == 1, f"s5_scan_{tag}_{dirn}")
        dirs.append(((xr, xi), mats, lam))
        ys.append(y)
    ytot, z = _rowwise(lambda u, d, a, b: ((lambda y: (y, _gelu(y)))(d * u + a + b)),
                       [hn, p["d"], ys[0], ys[1]], [F32, BF16], f"s5_gelu_{tag}")
    t = _mm(z, w_glu, name=f"s5_glu_{tag}", epi=lambda r, b: r + b, epi_in=(p["b_glu"],))
    (x_new,) = _rowwise(lambda xx, zz, tt: (xx + zz.astype(F32) * jax.nn.sigmoid(tt),),
                        [xp, z, t], [F32], f"s5_out_{tag}")
    return _s5_unperm(x_new), (xp, hn, raw, dirs, ytot, z, t)


def _s5_layer_bwd(dres, saved, norm_g, p, w_glu, tag):
    x, hn, raw, dirs, ytot, z, t = saved
    S, D = x.shape
    G, N = p["a_re"].shape[1:]
    dres = _s5_perm(dres[0])

    def glu_bwd(do, zz, tt):
        sg = jax.nn.sigmoid(tt)
        dt = do * zz.astype(F32) * (sg * (1.0 - sg))
        return dt, do * sg, dt

    dt, dzd, db_glu = _rowwise(glu_bwd, [dres, z, t], [BF16, F32, F32], f"s5_out_bwd_{tag}", acc=(2,))
    dz = _mm(dt, w_glu, tb=True, name=f"s5_dz_{tag}", epi=_add_epi, epi_in=(dzd,))
    dw_glu = _mm(z, dt, ta=True, name=f"s5_dwglu_{tag}")

    def gelu_bwd(dzz, y, u, d):
        dy = dzz * _gelu_grad(y)
        return dy, dy * d, dy * u

    dy, du, dd = _rowwise(gelu_bwd, [dz, ytot, hn, p["d"]], [F32, F32, F32], f"s5_gelu_bwd_{tag}", acc=(2,))
    d_lr, d_li, d_bbr, d_bbi, d_cr, d_ci = [], [], [], [], [], []
    du = [du]
    for dirn in range(2):
        xp, mats, lam = dirs[dirn]
        starts = _s5_scan_bwd(dy, None, None, None, None, mats[2], mats[3], *lam, None, dirn == 0,
                              f"s5_starts_{tag}_{dirn}")
        dup, dbr, dbi, dcr, dci, dlr, dli = _s5_scan_bwd(dy, hn, xp, *mats, *lam, starts, dirn == 0,
                                                         f"s5_scan_bwd_{tag}_{dirn}")
        du.append(dup)
        d_lr.append(dlr)
        d_li.append(dli)
        d_bbr.append(_s5_blockdiag_b_t(dbr))
        d_bbi.append(_s5_blockdiag_b_t(dbi))
        d_cr.append(_s5_blockdiag_c_t(dcr))
        d_ci.append(_s5_blockdiag_c_t(dci))
    da_re, da_im, dls, db_re, db_im = _s5_disc_bwd(*raw, jnp.stack(d_lr), jnp.stack(d_li), jnp.stack(d_bbr),
                                                   jnp.stack(d_bbi), f"s5_disc_bwd_{tag}")
    dres, dg = _rmsnorm_bwd(x, norm_g, du, dres, f"s5_norm_bwd_{tag}")
    dres = tuple(_s5_unperm(t_) for t_ in dres)
    unb = lambda b: b.reshape(2, SSM_GROUP_CH, G, N).transpose(0, 2, 3, 1)
    grads = dict(a_re=da_re.reshape(2, G, N), a_im=da_im.reshape(2, G, N), log_step=dls.reshape(2, G, N).sum(-1),
                 b_re=unb(db_re), b_im=unb(db_im), c_re=jnp.stack(d_cr), c_im=jnp.stack(d_ci),
                 d=dd, w_glu=dw_glu, b_glu=db_glu, norm=dg)
    return dres, grads


def _adamw(w, g, m, v, name):
    R, C = w.shape
    tr = _pick(R, (512, 256, 128, 64, 32, 16, 8))
    tn = _pick(C, (512, 256, 128))

    def body(w_ref, g_ref, m_ref, v_ref, d_ref, nm_ref, nv_ref):
        gg = g_ref[...]
        m2 = ADAM_B1 * m_ref[...] + (1.0 - ADAM_B1) * gg
        v2 = ADAM_B2 * v_ref[...] + (1.0 - ADAM_B2) * (gg * gg)
        m_hat = m2 / (1.0 - ADAM_B1 ** ADAM_STEP)
        v_hat = v2 / (1.0 - ADAM_B2 ** ADAM_STEP)
        d_ref[...] = -ADAM_LR * (m_hat / (jnp.sqrt(v_hat) + ADAM_EPS) + ADAM_WD * w_ref[...])
        nm_ref[...] = m2
        nv_ref[...] = v2

    blk = pl.BlockSpec((tr, tn), lambda i, j: (i, j))
    shp = jax.ShapeDtypeStruct((R, C), F32)
    return pl.pallas_call(body, name=name, grid=(R // tr, C // tn), in_specs=[blk] * 4, out_specs=[blk] * 3,
                          out_shape=[shp] * 3, compiler_params=_cp(("parallel", "parallel")))(w, g, m, v)


def _add_selected(p, sel, others, name, also_bf16=False):
    K, _, M, C = p.shape
    tr = _pick(M, [t for t in (512, 256, 128, 64, 32, 16) if t * C * 4 <= 2 ** 21])
    n_o = len(others)

    def body(sel_ref, p_ref, *refs):
        acc = p_ref[...]
        for r in refs[:n_o]:
            acc = acc + r[...].astype(F32)
        refs[n_o][...] = acc
        if also_bf16:
            refs[n_o + 1][...] = acc.astype(BF16)

    blk = pl.BlockSpec((None, tr, C), lambda k, i, s: (k, i, 0))
    grid_spec = pltpu.PrefetchScalarGridSpec(
        num_scalar_prefetch=1, grid=(K, M // tr),
        in_specs=[pl.BlockSpec((None, None, tr, C), lambda k, i, s: (k, s[0], i, 0))] + [blk] * n_o,
        out_specs=[blk, blk] if also_bf16 else blk)
    shp = jax.ShapeDtypeStruct((K, M, C), F32)
    return pl.pallas_call(body, name=name, grid_spec=grid_spec,
                          out_shape=[shp, jax.ShapeDtypeStruct((K, M, C), BF16)] if also_bf16 else shp,
                          compiler_params=_cp(("parallel", "parallel")))(sel, p, *others)


def _sum_slots(a, name):
    n, R, C = a.shape
    tr = _pick(R, (512, 256, 128, 64, 32, 16, 8))

    def body(a_ref, o_ref):
        acc = a_ref[0]
        for k in range(1, n):
            acc = acc + a_ref[k]
        o_ref[...] = acc

    return pl.pallas_call(body, name=name, grid=(R // tr,),
                          in_specs=[pl.BlockSpec((n, tr, C), lambda i: (0, i, 0))],
                          out_specs=pl.BlockSpec((tr, C), lambda i: (i, 0)),
                          out_shape=jax.ShapeDtypeStruct((R, C), F32), compiler_params=_cp(("parallel",)))(a)


def _position():
    return lax.axis_index("x"), lax.axis_index("y"), lax.axis_index("c")


def _other_chips(x, y):
    return [(1 - x, y), (x, 1 - y), (1 - x, 1 - y)]


def _run_copies(sends, recvs, local, send_sems, recv_sems, local_sems):
    started = []
    for k, (src, dst, dev) in enumerate(sends):
        cp = pltpu.make_async_remote_copy(src_ref=src, dst_ref=dst, send_sem=send_sems.at[k],
                                          recv_sem=recv_sems.at[k], device_id=dev, device_id_type=MESH)
        cp.start()
        started.append(cp)
    locals_ = []
    for k, (src, dst) in enumerate(local):
        cp = pltpu.make_async_copy(src, dst, local_sems.at[k])
        cp.start()
        locals_.append(cp)
    for k, (dst, dev) in enumerate(recvs):
        pltpu.make_async_remote_copy(src_ref=dst, dst_ref=dst, send_sem=send_sems.at[k], recv_sem=recv_sems.at[k],
                                     device_id=dev, device_id_type=MESH).wait_recv()
    for cp in started:
        cp.wait_send()
    for cp in locals_:
        cp.wait()


def _comm_call(plan, ins, out_shapes, n_remote, n_local, name):
    n_in = len(ins)
    n_out = len(out_shapes)

    def body(*refs):
        in_refs, out_refs = refs[:n_in], refs[n_in:n_in + n_out]
        send_sems, recv_sems, local_sems = refs[n_in + n_out:]
        sends, recvs, local = plan(in_refs, out_refs)
        assert len(sends) == len(recvs) == n_remote and len(local) == n_local
        _run_copies(sends, recvs, local, send_sems, recv_sems, local_sems)

    hbm = pl.BlockSpec(memory_space=pltpu.HBM)
    return pl.pallas_call(
        body, name=name, in_specs=[hbm] * n_in, out_specs=[hbm] * n_out, out_shape=out_shapes,
        scratch_shapes=[pltpu.SemaphoreType.DMA((n_remote,)), pltpu.SemaphoreType.DMA((n_remote,)),
                        pltpu.SemaphoreType.DMA((max(n_local, 1),))],
    )(*ins)


def _gather_chips(arrs, name):
    n = len(arrs)

    def plan(ins, outs):
        x, y, c = _position()
        me = 2 * x + y
        sends, recvs = [], []
        for px, py in _other_chips(x, y):
            for i in range(n):
                sends.append((ins[i], outs[i].at[me], (px, py, c)))
                recvs.append((outs[i].at[2 * px + py], (px, py, c)))
        return sends, recvs, [(ins[i], outs[i].at[me]) for i in range(n)]

    shapes = [jax.ShapeDtypeStruct((4,) + a.shape, a.dtype) for a in arrs]
    return _comm_call(plan, arrs, shapes, 3 * n, n, name)


def _gather_chips_two_level(arrs, name):
    n = len(arrs)
    hr = [a.shape[1] // 2 for a in arrs]

    def body(*refs):
        ins, outs = refs[:n], refs[n:2 * n]
        send_sems, recv_sems, local_sems = refs[2 * n:]
        x, y, c = _position()
        me = 2 * x + y
        sib = (x, y, 1 - c)
        chips = _other_chips(x, y)

        def remote(k, src, dst, dev):
            return pltpu.make_async_remote_copy(src_ref=src, dst_ref=dst, send_sem=send_sems.at[k],
                                                recv_sem=recv_sems.at[k], device_id=dev, device_id_type=MESH)

        local = []
        for i in range(n):
            for h in range(2):
                cp = pltpu.make_async_copy(ins[i].at[:, pl.ds(h * hr[i], hr[i])], outs[i].at[me, h],
                                           local_sems.at[2 * i + h])
                cp.start()
                local.append(cp)
        first = []
        for j, (px, py) in enumerate(chips):
            for i in range(n):
                cp = remote(j * n + i, ins[i].at[:, pl.ds(c * hr[i], hr[i])], outs[i].at[me, c], (px, py, c))
                cp.start()
                first.append(cp)
        passed = []
        for j, (px, py) in enumerate(chips):
            for i in range(n):
                landed = outs[i].at[2 * px + py, c]
                remote(j * n + i, landed, landed, (px, py, c)).wait_recv()
                cp = remote(3 * n + j * n + i, landed, landed, sib)
                cp.start()
                passed.append(cp)
        for j, (px, py) in enumerate(chips):
            for i in range(n):
                slot = outs[i].at[2 * px + py, 1 - c]
                remote(3 * n + j * n + i, slot, slot, sib).wait_recv()
        for cp in first + passed:
            cp.wait_send()
        for cp in local:
            cp.wait()

    hbm = pl.BlockSpec(memory_space=pltpu.HBM)
    shapes = [jax.ShapeDtypeStruct((N_CHIPS, 2, a.shape[0], a.shape[1] // 2, a.shape[2]), a.dtype) for a in arrs]
    return pl.pallas_call(
        body, name=name, in_specs=[hbm] * n, out_specs=[hbm] * n, out_shape=shapes,
        scratch_shapes=[pltpu.SemaphoreType.DMA((6 * n,)), pltpu.SemaphoreType.DMA((6 * n,)),
                        pltpu.SemaphoreType.DMA((2 * n,))],
    )(*arrs)


def _sibling_halves(pieces, name):
    n = len(pieces)

    def plan(ins, outs):
        x, y, c = _position()
        sib = (x, y, 1 - c)
        return ([(ins[i].at[:, 1 - c], outs[i], sib) for i in range(n)],
                [(outs[i], sib) for i in range(n)], [])

    shapes = [jax.ShapeDtypeStruct((p.shape[0],) + p.shape[2:], p.dtype) for p in pieces]
    return _comm_call(plan, pieces, shapes, n, 0, name)


def _scatter_chips(sums, name):
    n = len(sums)

    def plan(ins, outs):
        x, y, c = _position()
        sends, recvs = [], []
        for j, (px, py) in enumerate(_other_chips(x, y)):
            for i in range(n):
                sends.append((ins[i].at[2 * px + py], outs[i].at[j], (px, py, c)))
                recvs.append((outs[i].at[j], (px, py, c)))
        return sends, recvs, []

    shapes = [jax.ShapeDtypeStruct((3,) + s.shape[1:], s.dtype) for s in sums]
    return _comm_call(plan, sums, shapes, 3 * n, 0, name)


def _sibling_join(halves, name):
    n = len(halves)

    def plan(ins, outs):
        x, y, c = _position()
        sib = (x, y, 1 - c)
        return ([(ins[i], outs[i].at[c], sib) for i in range(n)],
                [(outs[i].at[1 - c], sib) for i in range(n)],
                [(ins[i], outs[i].at[c]) for i in range(n)])

    shapes = [jax.ShapeDtypeStruct((2,) + h.shape, h.dtype) for h in halves]
    return _comm_call(plan, halves, shapes, n, n, name)


def _gather_all(v, name):
    rels = [(dx, dy, dc) for dx in (0, 1) for dy in (0, 1) for dc in (0, 1)][1:]

    def plan(ins, outs):
        x, y, c = _position()
        me = 4 * x + 2 * y + c
        flip = lambda a, d: 1 - a if d else a
        sends, recvs = [], []
        for dx, dy, dc in rels:
            px, py, pc = flip(x, dx), flip(y, dy), flip(c, dc)
            sends.append((ins[0], outs[0].at[me], (px, py, pc)))
            recvs.append((outs[0].at[4 * px + 2 * py + pc], (px, py, pc)))
        return sends, recvs, [(ins[0], outs[0].at[me])]

    return _comm_call(plan, [v], [jax.ShapeDtypeStruct((8,) + v.shape, v.dtype)], len(rels), 1, name)[0]


WEIGHTS = ['mix_norm', 'ffn_norm', 'final_norm', 'attn_w_qkv', 'attn_w_o', 'attn_sink', 'ssm_a_re', 'ssm_a_im',
           'ssm_log_step', 'ssm_b_re', 'ssm_b_im', 'ssm_c_re', 'ssm_c_im', 'ssm_d', 'ssm_w_glu', 'ssm_b_glu',
           'mla_w_dqkv', 'mla_q_norm', 'mla_kv_norm', 'mla_w_uq', 'mla_w_ukv', 'mla_w_o', 'ffn_w_up',
           'ffn_conv_w', 'ffn_conv_b', 'ffn_w_down']
BIG = dict(attn_w_qkv='col', attn_w_o='row', ssm_w_glu='row', mla_w_dqkv='row', mla_w_uq='col',
           mla_w_ukv='col', mla_w_o='row', ffn_w_up='col', ffn_w_down='row')
SMALL_SHARDED = ('mla_q_norm', 'mla_kv_norm', 'ffn_conv_w')
N_CHIPS = 4


def _assemble(g, kind):
    L = g.shape[2]
    if kind == 'row':
        return g.transpose(2, 0, 1, 3, 4).reshape(L, -1, g.shape[4])
    return g.transpose(2, 1, 3, 0, 4).reshape(L, 2 * g.shape[3], -1)


def _to_pieces(w, kind):
    L, R, C = w.shape
    if kind == 'col':
        t = w.reshape(L, 2, R // 2, N_CHIPS, C // N_CHIPS).transpose(3, 1, 0, 2, 4)
    else:
        t = w.reshape(L, N_CHIPS, R // N_CHIPS, 2, C // 2).transpose(1, 3, 0, 2, 4)
    return t.reshape(N_CHIPS, 2, L * t.shape[3], t.shape[4])


def _from_halves(h, kind, shard_shape):
    L = shard_shape[0]
    t = h.reshape(2, L, -1, h.shape[2])
    t = t.transpose(1, 0, 2, 3) if kind == 'col' else t.transpose(1, 2, 0, 3)
    return t.reshape(shard_shape)


def _pack(arrs):
    flat = jnp.concatenate([a.reshape(-1) for a in arrs])
    pad = (-flat.shape[0]) % (8 * LANES)
    return jnp.pad(flat, (0, pad)).reshape(-1, LANES)


def _unpack(packed, like):
    flat = packed.reshape(-1)
    out, off = [], 0
    for a in like:
        out.append(flat[off:off + a.size].reshape(a.shape))
        off += a.size
    return out


def kernel(x, mix_norm, ffn_norm, final_norm, attn_w_qkv, attn_w_o, attn_sink, ssm_a_re, ssm_a_im, ssm_log_step, ssm_b_re, ssm_b_im, ssm_c_re, ssm_c_im, ssm_d, ssm_w_glu, ssm_b_glu, mla_w_dqkv, mla_q_norm, mla_kv_norm, mla_w_uq, mla_w_ukv, mla_w_o, ffn_w_up, ffn_conv_w, ffn_conv_b, ffn_w_down, loss_target, m_mix_norm, m_ffn_norm, m_final_norm, m_attn_w_qkv, m_attn_w_o, m_attn_sink, m_ssm_a_re, m_ssm_a_im, m_ssm_log_step, m_ssm_b_re, m_ssm_b_im, m_ssm_c_re, m_ssm_c_im, m_ssm_d, m_ssm_w_glu, m_ssm_b_glu, m_mla_w_dqkv, m_mla_q_norm, m_mla_kv_norm, m_mla_w_uq, m_mla_w_ukv, m_mla_w_o, m_ffn_w_up, m_ffn_conv_w, m_ffn_conv_b, m_ffn_w_down, v_mix_norm, v_ffn_norm, v_final_norm, v_attn_w_qkv, v_attn_w_o, v_attn_sink, v_ssm_a_re, v_ssm_a_im, v_ssm_log_step, v_ssm_b_re, v_ssm_b_im, v_ssm_c_re, v_ssm_c_im, v_ssm_d, v_ssm_w_glu, v_ssm_b_glu, v_mla_w_dqkv, v_mla_q_norm, v_mla_kv_norm, v_mla_w_uq, v_mla_w_ukv, v_mla_w_o, v_ffn_w_up, v_ffn_conv_w, v_ffn_conv_b, v_ffn_w_down):
    args = locals()
    w = {n: args[n] for n in WEIGHTS}
    mom = {n: args["m_" + n] for n in WEIGHTS}
    var = {n: args["v_" + n] for n in WEIGHTS}
    depth = mix_norm.shape[0]
    xs = x[0]
    target = loss_target[0]
    chip = 2 * lax.axis_index("x") + lax.axis_index("y")
    core = lax.axis_index("c")

    big_names = list(BIG)
    gathered = _gather_chips_two_level([w[n].astype(BF16) for n in big_names], "gather_weights")
    full = {n: _assemble(g, BIG[n]) for n, g in zip(big_names, gathered)}
    for n, g in zip(SMALL_SHARDED, _gather_chips([w[n] for n in SMALL_SHARDED], "gather_small_weights")):
        full[n] = jnp.moveaxis(g, 0, -2).reshape(g.shape[1:-1] + (-1,))

    def ffn_args(i):
        return (ffn_norm[i:i + 1], full["ffn_w_up"][i], full["ffn_conv_w"][i], ffn_conv_b[i:i + 1], full["ffn_w_down"][i])

    def s5_params(j):
        return dict(a_re=ssm_a_re[j], a_im=ssm_a_im[j], log_step=ssm_log_step[j], b_re=ssm_b_re[j], b_im=ssm_b_im[j],
                    c_re=ssm_c_re[j], c_im=ssm_c_im[j], d=ssm_d[j:j + 1], b_glu=ssm_b_glu[j:j + 1])

    def mla_args(j):
        wd, wq, wkv = _mla_weights(full["mla_w_dqkv"][j], full["mla_w_uq"][j], full["mla_w_ukv"][j])
        return (wd, wq, wkv, full["mla_w_o"][j], full["mla_q_norm"][j:j + 1], full["mla_kv_norm"][j:j + 1])

    h = xs
    saved = []
    for i in range(depth):
        kind, j = i % 3, i // 3
        g = mix_norm[i:i + 1]
        if kind == 0:
            h, sm = _attn_layer_fwd(h, g, full["attn_w_qkv"][j], full["attn_w_o"][j], attn_sink[j], f"{i}")
        elif kind == 1:
            h, sm = _s5_layer_fwd(h, g, s5_params(j), full["ssm_w_glu"][j], f"{i}")
        else:
            h, sm = _mla_layer_fwd(h, g, *mla_args(j), f"{i}")
        h, sf = _ffn_fwd(h, *ffn_args(i), f"{i}")
        saved.append((sm, sf))
    loss_part, dres, d_final = _loss_head(h, final_norm.reshape(1, -1), target, "loss_head")

    gl = {n: [None] * w[n].shape[0] for n in WEIGHTS if n != 'final_norm'}
    for i in reversed(range(depth)):
        kind, j = i % 3, i // 3
        sm, sf = saved[i]
        dres, (dg, dwu, dcw, dcb, dwd) = _ffn_bwd(dres, sf, *ffn_args(i), f"{i}")
        gl['ffn_norm'][i], gl['ffn_w_up'][i], gl['ffn_conv_w'][i] = dg[0], dwu, dcw
        gl['ffn_conv_b'][i], gl['ffn_w_down'][i] = dcb[0], dwd
        g = mix_norm[i:i + 1]
        if kind == 0:
            dres, (dg, dwq, dwo, dsk) = _attn_layer_bwd(dres, sm, g, full["attn_w_qkv"][j], full["attn_w_o"][j],
                                                        attn_sink[j], f"{i}")
            gl['attn_w_qkv'][j], gl['attn_w_o'][j], gl['attn_sink'][j] = dwq, dwo, dsk
        elif kind == 1:
            dres, gs = _s5_layer_bwd(dres, sm, g, s5_params(j), full["ssm_w_glu"][j], f"{i}")
            dg = gs['norm']
            for k in ('a_re', 'a_im', 'log_step', 'b_re', 'b_im', 'c_re', 'c_im'):
                gl['ssm_' + k][j] = gs[k]
            gl['ssm_d'][j], gl['ssm_b_glu'][j], gl['ssm_w_glu'][j] = gs['d'][0], gs['b_glu'][0], gs['w_glu']
        else:
            dres, (dg, dwd_, dqn, dkvn, dwuq, dwukv, dwo) = _mla_layer_bwd(dres, sm, g, *mla_args(j), f"{i}")
            gl['mla_w_dqkv'][j], gl['mla_q_norm'][j], gl['mla_kv_norm'][j] = dwd_, dqn[0], dkvn[0]
            gl['mla_w_uq'][j], gl['mla_w_ukv'][j], gl['mla_w_o'][j] = dwuq, dwukv, dwo
        gl['mix_norm'][i] = dg[0]
    local = {n: jnp.stack(v) for n, v in gl.items()}
    local['final_norm'] = d_final[0]

    pieces = [_to_pieces(local[n], BIG[n]) for n in big_names]
    from_sib = _sibling_halves(pieces, "reduce_sibling")
    sel_core = jnp.reshape(core, (1,)).astype(jnp.int32)
    pair = [_add_selected(p, sel_core, [r], f"reduce_add_pair_{n}", also_bf16=True)
            for n, p, r in zip(big_names, pieces, from_sib)]
    from_chips = _scatter_chips([pb for _, pb in pair], "reduce_chips")
    sel_chip = jnp.reshape(chip, (1,)).astype(jnp.int32)
    mine = [_add_selected(p.reshape((1,) + p.shape), sel_chip, [r[k:k + 1] for k in range(3)], f"reduce_add_chips_{n}")[0]
            for n, (p, _), r in zip(big_names, pair, from_chips)]
    joined = _sibling_join(mine, "reduce_join")
    grads = {n: _from_halves(hv, BIG[n], w[n].shape) for n, hv in zip(big_names, joined)}

    small_names = [n for n in WEIGHTS if n not in BIG]
    summed = _sum_slots(_gather_all(_pack([local[n] for n in small_names]), "reduce_small"), "reduce_small_sum")
    for n, gsum in zip(small_names, _unpack(summed, [local[n] for n in small_names])):
        if n in SMALL_SHARDED:
            width = w[n].shape[-1]
            gsum = lax.dynamic_slice_in_dim(gsum, chip * width, width, axis=gsum.ndim - 1)
        grads[n] = gsum

    delta, new_m, new_v = {}, {}, {}
    for n in big_names:
        two_d = lambda a: a.reshape(-1, a.shape[-1])
        d_, m_, v_ = _adamw(two_d(w[n]), two_d(grads[n]), two_d(mom[n]), two_d(var[n]), f"adamw_{n}")
        delta[n], new_m[n], new_v[n] = (t.reshape(w[n].shape) for t in (d_, m_, v_))
    outs = _adamw(*[_pack([src[n] for n in small_names]) for src in (w, grads, mom, var)], "adamw_small")
    for dst, packed in zip((delta, new_m, new_v), outs):
        for n, t in zip(small_names, _unpack(packed, [w[n] for n in small_names])):
            dst[n] = t

    loss = lax.psum(loss_part[0, 0], ("x", "y", "c"))
    return (loss, dres[0][None], *[grads[n] for n in WEIGHTS], *[delta[n] for n in WEIGHTS],
            *[new_m[n] for n in WEIGHTS], *[new_v[n] for n in WEIGHTS])
```

```python
import functools
import math

import numpy as np
import jax
import jax.numpy as jnp
from jax import lax
from jax.experimental import pallas as pl
from jax.experimental.pallas import tpu as pltpu

F32 = jnp.float32
BF16 = jnp.bfloat16
MESH = pl.DeviceIdType.MESH

RMS_EPS = 1e-6
ATTN_DH = 64
ATTN_GROUP = 8
ATTN_BLOCK = 128
SSM_GROUP_CH = 16
SSM_STATE = 64
SSM_SEGMENTS = 8
MLA_HEADS = 16
MLA_LORA = 512
MLA_NOPE = 128
MLA_ROPE = 64
MLA_V = 128
ROPE_THETA = 10000.0
LANES = 128
VMEM_LIMIT = 56 * 1024 * 1024

ADAM_LR = 0.001
ADAM_B1 = 0.9
ADAM_B2 = 0.999
ADAM_EPS = 1e-08
ADAM_WD = 0.01
ADAM_STEP = 10


def _cp(sem=None):
    kw = dict(vmem_limit_bytes=VMEM_LIMIT)
    if sem is not None:
        kw["dimension_semantics"] = sem
    return pltpu.CompilerParams(**kw)


MM_FULL_K = 2048
MM_FULL_N = 1536


def _pick(n, cands):
    for c in cands:
        if n % c == 0:
            return c
    return n


def _gmm(a, b, *, grid, a_spec, b_spec, o_spec, out_shape, dims, kax, acc_shape, name,
         epi=None, epi_in=(), epi_specs=()):
    nk = grid[kax]
    n_epi = len(epi_in)

    def finish(r, e_refs, o_ref):
        if epi is not None:
            r = epi(r, *[e[...] for e in e_refs])
        o_ref[...] = r.astype(o_ref.dtype)

    def product(a_ref, b_ref):
        return lax.dot_general(a_ref[...].astype(BF16), b_ref[...].astype(BF16), dims, preferred_element_type=F32)

    def body_one(a_ref, b_ref, *rest):
        finish(product(a_ref, b_ref), rest[:n_epi], rest[n_epi])

    def body_acc(a_ref, b_ref, *rest):
        acc_ref = rest[n_epi + 1]
        k = pl.program_id(kax)

        @pl.when(k == 0)
        def _():
            acc_ref[...] = jnp.zeros_like(acc_ref)

        acc_ref[...] += product(a_ref, b_ref)

        @pl.when(k == nk - 1)
        def _():
            finish(acc_ref[...], rest[:n_epi], rest[n_epi])

    sem = tuple("arbitrary" if i == kax else "parallel" for i in range(len(grid)))
    return pl.pallas_call(
        body_one if nk == 1 else body_acc, name=name, grid=grid,
        in_specs=[a_spec, b_spec, *epi_specs], out_specs=o_spec, out_shape=out_shape,
        scratch_shapes=[] if nk == 1 else [pltpu.VMEM(acc_shape, F32)], compiler_params=_cp(sem),
    )(a, b, *epi_in)


def _mm(a, b, *, ta=False, tb=False, out_dtype=F32, name, epi=None, epi_in=(), tm=None, tn=None, tk=None):
    (K, M) = a.shape if ta else a.shape[::-1]
    (N, K2) = b.shape if tb else b.shape[::-1]
    assert K == K2, (a.shape, b.shape, ta, tb)
    tm = tm or _pick(M, (1024, 512, 256, 128))
    tn = tn or (N if N <= MM_FULL_N else _pick(N, (1024, 512, 256, 128)))
    tk = tk or (K if K <= MM_FULL_K else _pick(K, (1024, 512, 256, 128)))
    grid = (M // tm, N // tn, K // tk)
    a_spec = pl.BlockSpec((tk, tm), lambda i, j, k: (k, i)) if ta else pl.BlockSpec((tm, tk), lambda i, j, k: (i, k))
    b_spec = pl.BlockSpec((tn, tk), lambda i, j, k: (j, k)) if tb else pl.BlockSpec((tk, tn), lambda i, j, k: (k, j))
    dims = (((0 if ta else 1,), (1 if tb else 0,)), ((), ()))
    epi_specs = [pl.BlockSpec((1, tn), lambda i, j, k: (0, j)) if e.shape[0] == 1
                 else pl.BlockSpec((tm, tn), lambda i, j, k: (i, j)) for e in epi_in]
    return _gmm(a, b, grid=grid, a_spec=a_spec, b_spec=b_spec,
                o_spec=pl.BlockSpec((tm, tn), lambda i, j, k: (i, j)),
                out_shape=jax.ShapeDtypeStruct((M, N), out_dtype), dims=dims, kax=2,
                acc_shape=(tm, tn), name=name, epi=epi, epi_in=epi_in, epi_specs=epi_specs)


def _add_epi(r, res):
    return res + r


def _rmsnorm_fwd(x, g, out_dtype, name):
    S, D = x.shape
    tr = _pick(S, (256, 128, 8))

    def body(x_ref, g_ref, o_ref):
        xf = x_ref[...]
        r = lax.rsqrt(jnp.mean(xf * xf, axis=-1, keepdims=True) + RMS_EPS)
        o_ref[...] = ((xf * r) * g_ref[...]).astype(o_ref.dtype)

    return pl.pallas_call(
        body, name=name, grid=(S // tr,),
        in_specs=[pl.BlockSpec((tr, D), lambda i: (i, 0)), pl.BlockSpec((1, D), lambda i: (0, 0))],
        out_specs=pl.BlockSpec((tr, D), lambda i: (i, 0)),
        out_shape=jax.ShapeDtypeStruct((S, D), out_dtype), compiler_params=_cp(("parallel",)),
    )(x, g)


def _rms_bwd_math(xf, g, dy):
    r = lax.rsqrt(jnp.mean(xf * xf, axis=-1, keepdims=True) + RMS_EPS)
    xh = xf * r
    dxh = dy * g
    dx = r * (dxh - xh * jnp.mean(dxh * xh, axis=-1, keepdims=True))
    return dx, jnp.sum(dy * xh, axis=0, keepdims=True)


def _rmsnorm_bwd(x, g, dy, dres, name):
    S, D = x.shape
    tr = _pick(S, (256, 128, 8))
    dys = list(dy) if isinstance(dy, (list, tuple)) else [dy]
    n_dy = len(dys)

    def body(x_ref, g_ref, *refs):
        dy_refs = refs[:n_dy]
        dres_ref, dx_ref, dxb_ref, dg_ref = refs[n_dy:]

        @pl.when(pl.program_id(0) == 0)
        def _():
            dg_ref[...] = jnp.zeros_like(dg_ref)

        dy_sum = dy_refs[0][...].astype(F32)
        for r in dy_refs[1:]:
            dy_sum = dy_sum + r[...].astype(F32)
        dx, dg = _rms_bwd_math(x_ref[...], g_ref[...], dy_sum)
        tot = dres_ref[...] + dx
        dx_ref[...] = tot
        dxb_ref[...] = tot.astype(BF16)
        dg_ref[...] += dg

    row = pl.BlockSpec((tr, D), lambda i: (i, 0))
    vec = pl.BlockSpec((1, D), lambda i: (0, 0))
    dx, dxb, dg = pl.pallas_call(
        body, name=name, grid=(S // tr,), in_specs=[row, vec] + [row] * (n_dy + 1), out_specs=[row, row, vec],
        out_shape=[jax.ShapeDtypeStruct((S, D), F32), jax.ShapeDtypeStruct((S, D), BF16),
                   jax.ShapeDtypeStruct((1, D), F32)],
        compiler_params=_cp(("arbitrary",)),
    )(x, g, *dys, dres)
    return (dx, dxb), dg


def _loss_head(x, g, target, name):
    S, D = x.shape
    tr = _pick(S, (256, 128, 8))

    def body(x_ref, g_ref, t_ref, loss_ref, dx_ref, dxb_ref, dg_ref):
        @pl.when(pl.program_id(0) == 0)
        def _():
            dg_ref[...] = jnp.zeros_like(dg_ref)
            loss_ref[...] = jnp.zeros_like(loss_ref)

        xf = x_ref[...]
        gg = g_ref[...]
        r = lax.rsqrt(jnp.mean(xf * xf, axis=-1, keepdims=True) + RMS_EPS)
        e = (xf * r) * gg - t_ref[...]
        loss_ref[...] += 0.5 * jnp.sum(jnp.mean(e * e, axis=-1, keepdims=True), axis=0, keepdims=True)
        dx, dg = _rms_bwd_math(xf, gg, e * (1.0 / D))
        dx_ref[...] = dx
        dxb_ref[...] = dx.astype(BF16)
        dg_ref[...] += dg

    row = pl.BlockSpec((tr, D), lambda i: (i, 0))
    vec = pl.BlockSpec((1, D), lambda i: (0, 0))
    one = pl.BlockSpec((1, 1), lambda i: (0, 0))
    loss, dx, dxb, dg = pl.pallas_call(
        body, name=name, grid=(S // tr,), in_specs=[row, vec, row], out_specs=[one, row, row, vec],
        out_shape=[jax.ShapeDtypeStruct((1, 1), F32), jax.ShapeDtypeStruct((S, D), F32),
                   jax.ShapeDtypeStruct((S, D), BF16), jax.ShapeDtypeStruct((1, D), F32)],
        compiler_params=_cp(("arbitrary",)),
    )(x, g, target)
    return loss, (dx, dxb), dg


HALO = 16


def _shift_rows(main, prev_row, next_row):
    tr = main.shape[0]
    row = lax.broadcasted_iota(jnp.int32, main.shape, 0)
    up = jnp.where(row == 0, prev_row, pltpu.roll(main, 1, 0))
    dn = jnp.where(row == tr - 1, next_row, pltpu.roll(main, tr - 1, 0))
    return up, dn


def _halo_specs(tr, tn, S, col_of):
    hb = tr // HALO
    last = S // HALO - 1
    return [pl.BlockSpec((tr, tn), lambda j, i: (i, col_of(j))),
            pl.BlockSpec((HALO, tn), lambda j, i: (jnp.maximum(i * hb - 1, 0), col_of(j))),
            pl.BlockSpec((HALO, tn), lambda j, i: (jnp.minimum((i + 1) * hb, last), col_of(j)))]


def _halo_rows(main_ref, prev_ref, next_ref, i, n_i):
    main = main_ref[...].astype(F32)
    prev_row = prev_ref[HALO - 1:HALO, :].astype(F32) * (i > 0).astype(F32)
    next_row = next_ref[0:1, :].astype(F32) * (i < n_i - 1).astype(F32)
    up, dn = _shift_rows(main, prev_row, next_row)
    return up, main, dn


def _conv3(w_ref, b_ref, up, mid, dn):
    return b_ref[...] + w_ref[0:1, :] * up + w_ref[1:2, :] * mid + w_ref[2:3, :] * dn


def _ffn_tiles(S, F):
    return _pick(S, (512, 256, 128, 16)), _pick(F, (512, 256, 128))


def _conv_gate_fwd(u, conv_w, conv_b, name):
    S, F2 = u.shape
    F = F2 // 2
    tr, tn = _ffn_tiles(S, F)
    nj, ni = F // tn, S // tr

    def body(gm, gp, gn, vm, vp, vn, wg, wv, bg, bv, o_ref):
        i = pl.program_id(1)
        cg = _conv3(wg, bg, *_halo_rows(gm, gp, gn, i, ni))
        cv = _conv3(wv, bv, *_halo_rows(vm, vp, vn, i, ni))
        o_ref[...] = (cg * jax.nn.sigmoid(cg) * cv).astype(o_ref.dtype)

    wspec = lambda off: pl.BlockSpec((3, tn), lambda j, i: (0, j + off))
    bspec = lambda off: pl.BlockSpec((1, tn), lambda j, i: (0, j + off))
    return pl.pallas_call(
        body, name=name, grid=(nj, ni),
        in_specs=[*_halo_specs(tr, tn, S, lambda j: j), *_halo_specs(tr, tn, S, lambda j: j + nj),
                  wspec(0), wspec(nj), bspec(0), bspec(nj)],
        out_specs=pl.BlockSpec((tr, tn), lambda j, i: (i, j)),
        out_shape=jax.ShapeDtypeStruct((S, F), BF16), compiler_params=_cp(("parallel", "parallel")),
    )(u, u, u, u, u, u, conv_w, conv_w, conv_b, conv_b)


def _conv_gate_bwd(u, da, conv_w, conv_b, name):
    S, F2 = u.shape
    F = F2 // 2
    tr, tn = _ffn_tiles(S, F)
    nj, ni = F // tn, S // tr

    def body(gm, gp, gn, vm, vp, vn, wg, wv, bg, bv, da_ref, dcg_ref, dcv_ref, dwg_ref, dwv_ref, dbg_ref, dbv_ref):
        i = pl.program_id(1)

        @pl.when(i == 0)
        def _():
            for r in (dwg_ref, dwv_ref, dbg_ref, dbv_ref):
                r[...] = jnp.zeros_like(r)

        g_rows = _halo_rows(gm, gp, gn, i, ni)
        v_rows = _halo_rows(vm, vp, vn, i, ni)
        cg = _conv3(wg, bg, *g_rows)
        cv = _conv3(wv, bv, *v_rows)
        sg = jax.nn.sigmoid(cg)
        d = da_ref[...].astype(F32)
        dcv = d * (cg * sg)
        dcg = d * cv * (sg * (1.0 + cg * (1.0 - sg)))
        dcg_ref[...] = dcg.astype(dcg_ref.dtype)
        dcv_ref[...] = dcv.astype(dcv_ref.dtype)
        for t in range(3):
            dwg_ref[t:t + 1, :] += jnp.sum(dcg * g_rows[t], axis=0, keepdims=True)
            dwv_ref[t:t + 1, :] += jnp.sum(dcv * v_rows[t], axis=0, keepdims=True)
        dbg_ref[...] += jnp.sum(dcg, axis=0, keepdims=True)
        dbv_ref[...] += jnp.sum(dcv, axis=0, keepdims=True)

    wspec = lambda off: pl.BlockSpec((3, tn), lambda j, i: (0, j + off))
    bspec = lambda off: pl.BlockSpec((1, tn), lambda j, i: (0, j + off))
    tile = pl.BlockSpec((tr, tn), lambda j, i: (i, j))
    outs = pl.pallas_call(
        body, name=name, grid=(nj, ni),
        in_specs=[*_halo_specs(tr, tn, S, lambda j: j), *_halo_specs(tr, tn, S, lambda j: j + nj),
                  wspec(0), wspec(nj), bspec(0), bspec(nj), tile],
        out_specs=[tile, tile, wspec(0), wspec(0), bspec(0), bspec(0)],
        out_shape=[jax.ShapeDtypeStruct((S, F), BF16), jax.ShapeDtypeStruct((S, F), BF16),
                   jax.ShapeDtypeStruct((3, F), F32), jax.ShapeDtypeStruct((3, F), F32),
                   jax.ShapeDtypeStruct((1, F), F32), jax.ShapeDtypeStruct((1, F), F32)],
        compiler_params=_cp(("parallel", "arbitrary")),
    )(u, u, u, u, u, u, conv_w, conv_w, conv_b, conv_b, da)
    dcg, dcv, dwg, dwv, dbg, dbv = outs
    return dcg, dcv, jnp.concatenate([dwg, dwv], axis=1), jnp.concatenate([dbg, dbv], axis=1)


def _conv_transpose(dc, w, name):
    S, Fx = dc.shape
    tr, tn = _ffn_tiles(S, Fx)
    nj, ni = Fx // tn, S // tr

    def body(m, p, n, w_ref, o_ref):
        up, mid, dn = _halo_rows(m, p, n, pl.program_id(1), ni)
        o_ref[...] = (w_ref[0:1, :] * dn + w_ref[1:2, :] * mid + w_ref[2:3, :] * up).astype(o_ref.dtype)

    return pl.pallas_call(
        body, name=name, grid=(nj, ni),
        in_specs=[*_halo_specs(tr, tn, S, lambda j: j), pl.BlockSpec((3, tn), lambda j, i: (0, j))],
        out_specs=pl.BlockSpec((tr, tn), lambda j, i: (i, j)),
        out_shape=jax.ShapeDtypeStruct((S, Fx), BF16), compiler_params=_cp(("parallel", "parallel")),
    )(dc, dc, dc, w)


def _ffn_fwd(x, norm_g, w_up, conv_w, conv_b, w_down, tag):
    hn = _rmsnorm_fwd(x, norm_g, BF16, f"ffn_norm_{tag}")
    u = _mm(hn, w_up, out_dtype=BF16, name=f"ffn_up_{tag}")
    a = _conv_gate_fwd(u, conv_w, conv_b, f"ffn_gate_{tag}")
    x_new = _mm(a, w_down, name=f"ffn_down_{tag}", epi=_add_epi, epi_in=(x,))
    return x_new, (x, hn, u, a)


def _ffn_bwd(dres, saved, norm_g, w_up, conv_w, conv_b, w_down, tag):
    x, hn, u, a = saved
    dres, dres_b = dres
    da = _mm(dres_b, w_down, tb=True, out_dtype=BF16, name=f"ffn_da_{tag}")
    dw_down = _mm(a, dres_b, ta=True, name=f"ffn_dwdown_{tag}")
    dcg, dcv, dconv_w, dconv_b = _conv_gate_bwd(u, da, conv_w, conv_b, f"ffn_gate_bwd_{tag}")
    dc = jnp.concatenate([dcg, dcv], axis=1)
    du = _conv_transpose(dc, conv_w, f"ffn_convt_{tag}")
    dhn = _mm(du, w_up, tb=True, name=f"ffn_dhn_{tag}")
    dw_up = _mm(hn, du, ta=True, name=f"ffn_dwup_{tag}")
    dres, dg = _rmsnorm_bwd(x, norm_g, dhn, dres, f"ffn_norm_bwd_{tag}")
    return dres, (dg, dw_up, dconv_w, dconv_b, dw_down)


ATTN_KEYS = 3 * ATTN_BLOCK


def _attn_window(i, S):
    ks = pl.multiple_of(jnp.clip((i - 1) * ATTN_BLOCK, 0, S - ATTN_KEYS), ATTN_BLOCK)
    qpos = i * ATTN_BLOCK + lax.broadcasted_iota(jnp.int32, (ATTN_BLOCK, ATTN_KEYS), 0)
    kpos = ks + lax.broadcasted_iota(jnp.int32, (ATTN_BLOCK, ATTN_KEYS), 1)
    arel = jnp.abs(kpos - qpos)
    return ks, arel.astype(F32), arel <= ATTN_BLOCK


def _attn_probs(q, k, slope, sink, arel, valid):
    s = lax.dot_general(q, k, (((1,), (1,)), ((), ())), preferred_element_type=F32) * (ATTN_DH ** -0.5)
    s = jnp.where(valid, s - slope * arel, -jnp.inf)
    m = jnp.maximum(jnp.max(s, axis=-1, keepdims=True), sink)
    p = jnp.exp(s - m)
    es = jnp.exp(sink - m)
    inv = 1.0 / (jnp.sum(p, axis=-1, keepdims=True) + es)
    return p * inv, es * inv


def _attn_specs(S, D):
    H = D // ATTN_DH
    KVW = (H // ATTN_GROUP) * ATTN_DH
    q_spec = pl.BlockSpec((ATTN_BLOCK, D), lambda i: (i, 0))
    k_spec = pl.BlockSpec((S, KVW), lambda i: (0, D // KVW))
    v_spec = pl.BlockSpec((S, KVW), lambda i: (0, D // KVW + 1))
    return H, KVW, q_spec, k_spec, v_spec


def _attn_fwd(qkv, sink, name):
    S = qkv.shape[0]
    D = qkv.shape[1] * ATTN_GROUP // (ATTN_GROUP + 2)
    H, KVW, q_spec, k_spec, v_spec = _attn_specs(S, D)

    def body(q_ref, k_ref, v_ref, sink_ref, o_ref):
        ks, arel, valid = _attn_window(pl.program_id(0), S)
        for kvh in range(H // ATTN_GROUP):
            cols = slice(kvh * ATTN_DH, (kvh + 1) * ATTN_DH)
            k = k_ref[pl.ds(ks, ATTN_KEYS), cols]
            v = v_ref[pl.ds(ks, ATTN_KEYS), cols]
            for g in range(ATTN_GROUP):
                h = kvh * ATTN_GROUP + g
                hc = slice(h * ATTN_DH, (h + 1) * ATTN_DH)
                p, _ = _attn_probs(q_ref[:, hc], k, 2.0 ** (-8.0 * (h + 1) / H), sink_ref[h], arel, valid)
                o_ref[:, hc] = jnp.dot(p.astype(BF16), v, preferred_element_type=F32).astype(o_ref.dtype)

    return pl.pallas_call(
        body, name=name, grid=(S // ATTN_BLOCK,),
        in_specs=[q_spec, k_spec, v_spec, pl.BlockSpec(memory_space=pltpu.SMEM)],
        out_specs=q_spec, out_shape=jax.ShapeDtypeStruct((S, D), BF16),
        compiler_params=_cp(("parallel",)),
    )(qkv, qkv, qkv, sink)


def _attn_bwd(qkv, sink, do, name):
    S = qkv.shape[0]
    D = qkv.shape[1] * ATTN_GROUP // (ATTN_GROUP + 2)
    H, KVW, q_spec, k_spec, v_spec = _attn_specs(S, D)
    scale = ATTN_DH ** -0.5

    def body(q_ref, k_ref, v_ref, sink_ref, do_ref, dq_ref, dk_ref, dv_ref, ds_ref):
        @pl.when(pl.program_id(0) == 0)
        def _():
            dk_ref[...] = jnp.zeros_like(dk_ref)
            dv_ref[...] = jnp.zeros_like(dv_ref)
            ds_ref[...] = jnp.zeros_like(ds_ref)

        ks, arel, valid = _attn_window(pl.program_id(0), S)
        rows = pl.ds(ks, ATTN_KEYS)
        for kvh in range(H // ATTN_GROUP):
            cols = slice(kvh * ATTN_DH, (kvh + 1) * ATTN_DH)
            k = k_ref[rows, cols]
            v = v_ref[rows, cols]
            dk = jnp.zeros((ATTN_KEYS, ATTN_DH), F32)
            dv = jnp.zeros((ATTN_KEYS, ATTN_DH), F32)
            for g in range(ATTN_GROUP):
                h = kvh * ATTN_GROUP + g
                hc = slice(h * ATTN_DH, (h + 1) * ATTN_DH)
                q = q_ref[:, hc]
                d_o = do_ref[:, hc]
                p, p_sink = _attn_probs(q, k, 2.0 ** (-8.0 * (h + 1) / H), sink_ref[h], arel, valid)
                dp = lax.dot_general(d_o, v, (((1,), (1,)), ((), ())), preferred_element_type=F32)
                delta = jnp.sum(p * dp, axis=-1, keepdims=True)
                dsc = (p * (dp - delta)).astype(BF16)
                ds_ref[:, h:h + 1] += -p_sink * delta
                dq_ref[:, hc] = (jnp.dot(dsc, k, preferred_element_type=F32) * scale).astype(dq_ref.dtype)
                dk += lax.dot_general(dsc, q, (((0,), (0,)), ((), ())), preferred_element_type=F32)
                dv += lax.dot_general(p.astype(BF16), d_o, (((0,), (0,)), ((), ())), preferred_element_type=F32)
            dk_ref[rows, cols] += dk * scale
            dv_ref[rows, cols] += dv

    kv_out = pl.BlockSpec((S, KVW), lambda i: (0, 0))
    return pl.pallas_call(
        body, name=name, grid=(S // ATTN_BLOCK,),
        in_specs=[q_spec, k_spec, v_spec, pl.BlockSpec(memory_space=pltpu.SMEM), q_spec],
        out_specs=[q_spec, kv_out, kv_out, pl.BlockSpec((ATTN_BLOCK, H), lambda i: (0, 0))],
        out_shape=[jax.ShapeDtypeStruct((S, D), BF16), jax.ShapeDtypeStruct((S, KVW), F32),
                   jax.ShapeDtypeStruct((S, KVW), F32), jax.ShapeDtypeStruct((ATTN_BLOCK, H), F32)],
        compiler_params=_cp(("arbitrary",)),
    )(qkv, qkv, qkv, sink, do)


def _attn_layer_fwd(x, norm_g, w_qkv, w_o, sink, tag):
    hn = _rmsnorm_fwd(x, norm_g, BF16, f"attn_norm_{tag}")
    qkv = _mm(hn, w_qkv, out_dtype=BF16, name=f"attn_qkv_{tag}")
    o = _attn_fwd(qkv, sink, f"attn_core_{tag}")
    x_new = _mm(o, w_o, name=f"attn_out_{tag}", epi=_add_epi, epi_in=(x,))
    return x_new, (x, hn, qkv, o)


def _attn_layer_bwd(dres, saved, norm_g, w_qkv, w_o, sink, tag):
    x, hn, qkv, o = saved
    dres, dres_b = dres
    do = _mm(dres_b, w_o, tb=True, out_dtype=BF16, name=f"attn_do_{tag}")
    dw_o = _mm(o, dres_b, ta=True, name=f"attn_dwo_{tag}")
    dq, dk, dv, dsink = _attn_bwd(qkv, sink, do, f"attn_core_bwd_{tag}")
    dqkv = jnp.concatenate([dq, dk.astype(BF16), dv.astype(BF16)], axis=1)
    dhn = _mm(dqkv, w_qkv, tb=True, name=f"attn_dhn_{tag}")
    dw_qkv = _mm(hn, dqkv, ta=True, name=f"attn_dwqkv_{tag}")
    dres, dg = _rmsnorm_bwd(x, norm_g, dhn, dres, f"attn_norm_bwd_{tag}")
    return dres, (dg, dw_qkv, dw_o, jnp.sum(dsink, axis=0))


MLA_W = 2 * LANES
MLA_DPAD = 2 * MLA_LORA + LANES
MLA_SCALE = (MLA_NOPE + MLA_ROPE) ** -0.5
MLA_TILES = (1024, 512, 256, 128)
LOG2E = math.log2(math.e)
LN2 = math.log(2.0)


def _rope_tables(S):
    half = MLA_ROPE // 2
    pos = jnp.arange(S, dtype=F32)
    inv = ROPE_THETA ** (-jnp.arange(half, dtype=F32) / half)
    ang = pos[:, None] * inv[None, :]
    cos, sin = jnp.cos(ang), jnp.sin(ang)
    z = jnp.zeros((S, LANES - 2 * half), F32)
    zh = jnp.zeros((S, half), F32)
    return (jnp.concatenate([cos, cos, z], axis=1), jnp.concatenate([-sin, zh, z], axis=1),
            jnp.concatenate([zh, sin, z], axis=1))


def _rope(t, ca, sb, sc):
    return t * ca + pltpu.roll(t, 96, 1) * sb + pltpu.roll(t, 32, 1) * sc


def _rope_t(d, ca, sb, sc):
    return d * ca + pltpu.roll(d * sb, 32, 1) + pltpu.roll(d * sc, 96, 1)


def _rms(xf, g):
    return (xf * lax.rsqrt(jnp.mean(xf * xf, axis=-1, keepdims=True) + RMS_EPS)) * g


def _mla_prep(d, qn, kvn, tabs, name):
    S = d.shape[0]
    tr = _pick(S, (256, 128, 8))
    L = MLA_LORA

    def body(d_ref, qn_ref, kvn_ref, ca, sb, sc, cq_ref, ckv_ref, kr_ref):
        cq_ref[...] = _rms(d_ref[:, :L], qn_ref[...]).astype(BF16)
        ckv_ref[...] = _rms(d_ref[:, L:2 * L], kvn_ref[...]).astype(BF16)
        kr_ref[...] = _rope(d_ref[:, 2 * L:], ca[...], sb[...], sc[...]).astype(BF16)

    row = lambda w: pl.BlockSpec((tr, w), lambda i: (i, 0))
    vec = pl.BlockSpec((1, L), lambda i: (0, 0))
    return pl.pallas_call(
        body, name=name, grid=(S // tr,),
        in_specs=[row(MLA_DPAD), vec, vec, row(LANES), row(LANES), row(LANES)],
        out_specs=[row(L), row(L), row(LANES)],
        out_shape=[jax.ShapeDtypeStruct((S, L), BF16), jax.ShapeDtypeStruct((S, L), BF16),
                   jax.ShapeDtypeStruct((S, LANES), BF16)],
        compiler_params=_cp(("parallel",)),
    )(d, qn, kvn, *tabs)


def _mla_prep_bwd(d, qn, kvn, tabs, dcq, dckv, dkr_h, name):
    S = d.shape[0]
    H = dkr_h.shape[0]
    tr = _pick(S, (256, 128, 8))
    L = MLA_LORA

    def body(d_ref, qn_ref, kvn_ref, ca, sb, sc, dcq_ref, dckv_ref, dkr_ref, dd_ref, dqn_ref, dkvn_ref):
        @pl.when(pl.program_id(0) == 0)
        def _():
            dqn_ref[...] = jnp.zeros_like(dqn_ref)
            dkvn_ref[...] = jnp.zeros_like(dkvn_ref)

        dx, dg = _rms_bwd_math(d_ref[:, :L], qn_ref[...], dcq_ref[...])
        dd_ref[:, :L] = dx.astype(BF16)
        dqn_ref[...] += dg
        dx, dg = _rms_bwd_math(d_ref[:, L:2 * L], kvn_ref[...], dckv_ref[...])
        dd_ref[:, L:2 * L] = dx.astype(BF16)
        dkvn_ref[...] += dg
        dkr = dkr_ref[0]
        for h in range(1, H):
            dkr = dkr + dkr_ref[h]
        dd_ref[:, 2 * L:] = _rope_t(dkr, ca[...], sb[...], sc[...]).astype(BF16)

    row = lambda w: pl.BlockSpec((tr, w), lambda i: (i, 0))
    vec = pl.BlockSpec((1, L), lambda i: (0, 0))
    return pl.pallas_call(
        body, name=name, grid=(S // tr,),
        in_specs=[row(MLA_DPAD), vec, vec, row(LANES), row(LANES), row(LANES), row(L), row(L),
                  pl.BlockSpec((H, tr, LANES), lambda i: (0, i, 0))],
        out_specs=[row(MLA_DPAD), vec, vec],
        out_shape=[jax.ShapeDtypeStruct((S, MLA_DPAD), BF16), jax.ShapeDtypeStruct((1, L), F32),
                   jax.ShapeDtypeStruct((1, L), F32)],
        compiler_params=_cp(("arbitrary",)),
    )(d, qn, kvn, *tabs, dcq, dckv, dkr_h)


def _heads_proj(a, w, out_dtype, name):
    S, K = a.shape
    H, _, n = w.shape
    tm = _pick(S, (1024, 512, 256, 128))
    return _gmm(a, w, grid=(S // tm, H, 1),
                a_spec=pl.BlockSpec((tm, K), lambda m, h, k: (m, 0)),
                b_spec=pl.BlockSpec((None, K, n), lambda m, h, k: (h, 0, 0)),
                o_spec=pl.BlockSpec((None, tm, n), lambda m, h, k: (h, m, 0)),
                out_shape=jax.ShapeDtypeStruct((H, S, n), out_dtype),
                dims=(((1,), (0,)), ((), ())), kax=2, acc_shape=(tm, n), name=name)


def _heads_proj_dx(dy, w, name):
    H, S, n = dy.shape
    K = w.shape[1]
    tm = _pick(S, (1024, 512, 256, 128))
    return _gmm(dy, w, grid=(S // tm, 1, H),
                a_spec=pl.BlockSpec((None, tm, n), lambda m, j, h: (h, m, 0)),
                b_spec=pl.BlockSpec((None, K, n), lambda m, j, h: (h, 0, 0)),
                o_spec=pl.BlockSpec((tm, K), lambda m, j, h: (m, 0)),
                out_shape=jax.ShapeDtypeStruct((S, K), F32),
                dims=(((1,), (1,)), ((), ())), kax=2, acc_shape=(tm, K), name=name)


def _heads_proj_dw(a, dy, name):
    S, K = a.shape
    H, _, n = dy.shape
    tk = _pick(S, (512, 256, 128))
    return _gmm(a, dy, grid=(H, 1, S // tk),
                a_spec=pl.BlockSpec((tk, K), lambda h, j, k: (k, 0)),
                b_spec=pl.BlockSpec((None, tk, n), lambda h, j, k: (h, k, 0)),
                o_spec=pl.BlockSpec((None, K, n), lambda h, j, k: (h, 0, 0)),
                out_shape=jax.ShapeDtypeStruct((H, K, n), F32),
                dims=(((0,), (0,)), ((), ())), kax=2, acc_shape=(K, n), name=name)


def _mla_rope_q(q_ext, tabs, bwd, name):
    H, S, _ = q_ext.shape
    tr = _pick(S, (512, 256, 128, 8))
    mult = 1.0 if bwd else MLA_SCALE * LOG2E

    def body(q_ref, ca, sb, sc, o_ref):
        o_ref[:, :LANES] = (q_ref[:, :LANES].astype(F32) * mult).astype(BF16)
        fn = _rope_t if bwd else _rope
        o_ref[:, LANES:] = (fn(q_ref[:, LANES:].astype(F32), ca[...], sb[...], sc[...]) * mult).astype(BF16)

    blk = pl.BlockSpec((None, tr, MLA_W), lambda i, h: (h, i, 0))
    tab = pl.BlockSpec((tr, LANES), lambda i, h: (i, 0))
    return pl.pallas_call(
        body, name=name, grid=(S // tr, H), in_specs=[blk, tab, tab, tab], out_specs=blk,
        out_shape=jax.ShapeDtypeStruct((H, S, MLA_W), BF16), compiler_params=_cp(("parallel", "parallel")),
    )(q_ext, *tabs)


def _col_to_row(col):
    n = col.shape[0]
    eye = lax.broadcasted_iota(jnp.int32, (n, n), 0) == lax.broadcasted_iota(jnp.int32, (n, n), 1)
    return jnp.sum(jnp.where(eye, col, 0.0), axis=0, keepdims=True)


def _mla_flash_fwd(q, kv, kr, name, tq=None, tk=None, unroll=1):
    H, S, _ = q.shape
    tq = tq or _pick(S, MLA_TILES)
    tk = tk or _pick(S, MLA_TILES)

    def body(q_ref, kv_ref, kr_ref, o_ref, lse_ref, kbuf, vbuf):
        @pl.when(pl.program_id(1) == 0)
        def _():
            kbuf[:, :LANES] = kv_ref[:, :LANES]
            kbuf[:, LANES:] = kr_ref[...]
            vbuf[:, :LANES] = kv_ref[:, LANES:]
            vbuf[:, LANES:] = jnp.ones((S, LANES), BF16)

        qv = q_ref[...]

        def step(c, carry):
            m, acc = carry
            rows = pl.ds(pl.multiple_of(c * tk, tk), tk)
            s = lax.dot_general(qv, kbuf[rows, :], (((1,), (1,)), ((), ())), preferred_element_type=F32)
            m_new = jnp.maximum(m, jnp.max(s, axis=-1, keepdims=True))
            p = jnp.exp2(s - m_new).astype(BF16)
            acc = jnp.exp2(m - m_new) * acc + jnp.dot(p, vbuf[rows, :], preferred_element_type=F32)
            return m_new, acc

        init = (jnp.full((tq, 1), -jnp.inf, F32), jnp.zeros((tq, MLA_W), F32))
        m, acc = lax.fori_loop(0, S // tk, step, init, unroll=unroll)
        l = acc[:, LANES:LANES + 1]
        o_ref[...] = (acc[:, :LANES] / l).astype(o_ref.dtype)
        lse_ref[...] = _col_to_row(m + jnp.log2(l))

    return pl.pallas_call(
        body, name=name, grid=(H, S // tq),
        in_specs=[pl.BlockSpec((None, tq, MLA_W), lambda h, i: (h, i, 0)),
                  pl.BlockSpec((None, S, MLA_W), lambda h, i: (h, 0, 0)),
                  pl.BlockSpec((S, LANES), lambda h, i: (0, 0))],
        out_specs=[pl.BlockSpec((tq, MLA_V), lambda h, i: (i, h)),
                   pl.BlockSpec((None, 1, tq), lambda h, i: (h, 0, i))],
        out_shape=[jax.ShapeDtypeStruct((S, H * MLA_V), BF16), jax.ShapeDtypeStruct((H, 1, S), F32)],
        scratch_shapes=[pltpu.VMEM((S, MLA_W), BF16), pltpu.VMEM((S, MLA_W), BF16)],
        compiler_params=_cp(("parallel", "arbitrary")),
    )(q, kv, kr)


def _mla_delta(o, do, H, name):
    S = o.shape[0]
    tq = _pick(S, (512, 256, 128))

    def body(o_ref, do_ref, d_ref):
        prod = o_ref[...].astype(F32) * do_ref[...].astype(F32)
        d_ref[...] = _col_to_row(jnp.sum(prod, axis=-1, keepdims=True))

    blk = pl.BlockSpec((tq, MLA_V), lambda i, h: (i, h))
    return pl.pallas_call(
        body, name=name, grid=(S // tq, H), in_specs=[blk, blk],
        out_specs=pl.BlockSpec((None, 1, tq), lambda i, h: (h, 0, i)),
        out_shape=jax.ShapeDtypeStruct((H, 1, S), F32), compiler_params=_cp(("parallel", "parallel")),
    )(o, do)


def _mla_flash_bwd(q, kv, kr, do, lse, delta, name, tq=None, tkv=None, unroll=1):
    H, S, _ = q.shape
    tq = tq or _pick(S, MLA_TILES)
    tkv = tkv or _pick(S, MLA_TILES)

    def body(q_ref, kv_ref, kr_ref, do_ref, lse_ref, dl_ref, dq_ref, dkv_ref, dkr_ref):
        @pl.when(pl.program_id(1) == 0)
        def _():
            dq_ref[...] = jnp.zeros_like(dq_ref)

        v = kv_ref[:, LANES:]
        k = jnp.concatenate([kv_ref[:, :LANES], kr_ref[...]], axis=1)

        def step(c, carry):
            dk, dv = carry
            start = pl.multiple_of(c * tq, tq)
            rows = pl.ds(start, tq)
            qv = q_ref[rows, :]
            d_o = do_ref[rows, :]
            s_t = lax.dot_general(k, qv, (((1,), (1,)), ((), ())), preferred_element_type=F32)
            p_t = jnp.exp2(s_t - lse_ref[:, rows])
            dv = dv + jnp.dot(p_t.astype(BF16), d_o, preferred_element_type=F32)
            dp_t = lax.dot_general(v, d_o, (((1,), (1,)), ((), ())), preferred_element_type=F32)
            ds_t = (p_t * (dp_t - dl_ref[:, rows])).astype(BF16)
            dk = dk + jnp.dot(ds_t, qv, preferred_element_type=F32)
            dq_ref[rows, :] += lax.dot_general(ds_t, k, (((0,), (0,)), ((), ())),
                                               preferred_element_type=F32) * MLA_SCALE
            return dk, dv

        dk, dv = lax.fori_loop(0, S // tq, step, (jnp.zeros((tkv, MLA_W), F32), jnp.zeros((tkv, MLA_V), F32)),
                               unroll=unroll)
        dkv_ref[:, :LANES] = (dk[:, :LANES] * LN2).astype(BF16)
        dkv_ref[:, LANES:] = dv.astype(BF16)
        dkr_ref[...] = dk[:, LANES:] * LN2

    stat = pl.BlockSpec((None, 1, S), lambda h, j: (h, 0, 0))
    return pl.pallas_call(
        body, name=name, grid=(H, S // tkv),
        in_specs=[pl.BlockSpec((None, S, MLA_W), lambda h, j: (h, 0, 0)),
                  pl.BlockSpec((None, tkv, MLA_W), lambda h, j: (h, j, 0)),
                  pl.BlockSpec((tkv, LANES), lambda h, j: (j, 0)),
                  pl.BlockSpec((S, MLA_V), lambda h, j: (0, h)), stat, stat],
        out_specs=[pl.BlockSpec((None, S, MLA_W), lambda h, j: (h, 0, 0)),
                   pl.BlockSpec((None, tkv, MLA_W), lambda h, j: (h, j, 0)),
                   pl.BlockSpec((None, tkv, LANES), lambda h, j: (h, j, 0))],
        out_shape=[jax.ShapeDtypeStruct((H, S, MLA_W), F32), jax.ShapeDtypeStruct((H, S, MLA_W), BF16),
                   jax.ShapeDtypeStruct((H, S, LANES), F32)],
        compiler_params=_cp(("parallel", "arbitrary")),
    )(q, kv, kr, do, lse, delta)


def _mla_weights(w_dqkv, w_uq, w_ukv):
    H = MLA_HEADS
    wd = jnp.pad(w_dqkv, ((0, 0), (0, MLA_DPAD - w_dqkv.shape[1])))
    wq = w_uq.reshape(MLA_LORA, H, MLA_NOPE + MLA_ROPE)
    wq = jnp.pad(wq, ((0, 0), (0, 0), (0, MLA_W - wq.shape[2]))).transpose(1, 0, 2)
    wkv = w_ukv.reshape(MLA_LORA, H, MLA_NOPE + MLA_V).transpose(1, 0, 2)
    return wd, wq, wkv


def _mla_layer_fwd(x, norm_g, wd, wq, wkv, w_o, qn, kvn, tag):
    S = x.shape[0]
    tabs = _rope_tables(S)
    hn = _rmsnorm_fwd(x, norm_g, BF16, f"mla_norm_{tag}")
    d = _mm(hn, wd, name=f"mla_down_{tag}")
    cq, ckv, kr = _mla_prep(d, qn, kvn, tabs, f"mla_prep_{tag}")
    q = _mla_rope_q(_heads_proj(cq, wq, F32, f"mla_uq_{tag}"), tabs, False, f"mla_ropeq_{tag}")
    kv = _heads_proj(ckv, wkv, BF16, f"mla_ukv_{tag}")
    o, lse = _mla_flash_fwd(q, kv, kr, f"mla_flash_{tag}")
    x_new = _mm(o, w_o, name=f"mla_out_{tag}", epi=_add_epi, epi_in=(x,))
    return x_new, (x, hn, d, cq, ckv, kr, q, kv, o, lse)


def _mla_layer_bwd(dres, saved, norm_g, wd, wq, wkv, w_o, qn, kvn, tag):
    x, hn, d, cq, ckv, kr, q, kv, o, lse = saved
    S = x.shape[0]
    H = MLA_HEADS
    tabs = _rope_tables(S)
    dres, dres_b = dres
    do = _mm(dres_b, w_o, tb=True, out_dtype=BF16, name=f"mla_do_{tag}")
    dw_o = _mm(o, dres_b, ta=True, name=f"mla_dwo_{tag}")
    delta = _mla_delta(o, do, H, f"mla_delta_{tag}")
    dq, dkv, dkr_h = _mla_flash_bwd(q, kv, kr, do, lse, delta, f"mla_flash_bwd_{tag}")
    dq_ext = _mla_rope_q(dq, tabs, True, f"mla_ropeq_bwd_{tag}")
    dwq = _heads_proj_dw(cq, dq_ext, f"mla_dwuq_{tag}")
    dcq = _heads_proj_dx(dq_ext, wq, f"mla_dcq_{tag}")
    dwkv = _heads_proj_dw(ckv, dkv, f"mla_dwukv_{tag}")
    dckv = _heads_proj_dx(dkv, wkv, f"mla_dckv_{tag}")
    dd, dqn, dkvn = _mla_prep_bwd(d, qn, kvn, tabs, dcq, dckv, dkr_h, f"mla_prep_bwd_{tag}")
    dhn = _mm(dd, wd, tb=True, name=f"mla_dhn_{tag}")
    dwd = _mm(hn, dd, ta=True, name=f"mla_dwd_{tag}")
    dres, dg = _rmsnorm_bwd(x, norm_g, dhn, dres, f"mla_norm_bwd_{tag}")
    dw_dqkv = dwd[:, :2 * MLA_LORA + MLA_ROPE]
    dw_uq = dwq.transpose(1, 0, 2)[:, :, :MLA_NOPE + MLA_ROPE].reshape(MLA_LORA, -1)
    dw_ukv = dwkv.transpose(1, 0, 2).reshape(MLA_LORA, -1)
    return dres, (dg, dw_dqkv, dqn, dkvn, dw_uq, dw_ukv, dw_o)


S5_CB = LANES
S5_SB = (S5_CB // SSM_GROUP_CH) * SSM_STATE
S5_ROWS = 1024


def _s5_disc(a_re, a_im, ls, b_re, b_im):
    step = jnp.exp(ls)
    mag = jnp.exp(step * a_re)
    lb_re = mag * jnp.cos(step * a_im)
    lb_im = mag * jnp.sin(step * a_im)
    n_re, n_im = lb_re - 1.0, lb_im
    den = a_re * a_re + a_im * a_im
    coef_re = (n_re * a_re + n_im * a_im) / den
    coef_im = (n_im * a_re - n_re * a_im) / den
    return lb_re, lb_im, coef_re * b_re - coef_im * b_im, coef_re * b_im + coef_im * b_re


def _s5_disc_fwd(a_re, a_im, ls, b_re, b_im, name):
    GN = a_re.shape[-1]

    def body(ar, ai, l, br, bi, o_lr, o_li, o_br, o_bi):
        for o, v in zip((o_lr, o_li, o_br, o_bi), _s5_disc(ar[...], ai[...], l[...], br[...], bi[...])):
            o[...] = v

    vec = pl.BlockSpec((None, 1, GN), lambda d: (d, 0, 0))
    mat = pl.BlockSpec((None, SSM_GROUP_CH, GN), lambda d: (d, 0, 0))
    sv = jax.ShapeDtypeStruct(a_re.shape, F32)
    sm = jax.ShapeDtypeStruct(b_re.shape, F32)
    return pl.pallas_call(body, name=name, grid=(2,), in_specs=[vec, vec, vec, mat, mat],
                          out_specs=[vec, vec, mat, mat], out_shape=[sv, sv, sm, sm],
                          compiler_params=_cp(("parallel",)))(a_re, a_im, ls, b_re, b_im)


def _s5_disc_bwd(a_re, a_im, ls, b_re, b_im, d_lr, d_li, d_br, d_bi, name):
    GN = a_re.shape[-1]

    def body(ar, ai, l, br, bi, g_lr, g_li, g_br, g_bi, o_ar, o_ai, o_l, o_br, o_bi):
        _, vjp = jax.vjp(_s5_disc, ar[...], ai[...], l[...], br[...], bi[...])
        for o, v in zip((o_ar, o_ai, o_l, o_br, o_bi), vjp((g_lr[...], g_li[...], g_br[...], g_bi[...]))):
            o[...] = v

    vec = pl.BlockSpec((None, 1, GN), lambda d: (d, 0, 0))
    mat = pl.BlockSpec((None, SSM_GROUP_CH, GN), lambda d: (d, 0, 0))
    sv = jax.ShapeDtypeStruct(a_re.shape, F32)
    sm = jax.ShapeDtypeStruct(b_re.shape, F32)
    return pl.pallas_call(body, name=name, grid=(2,), in_specs=[vec, vec, vec, mat, mat, vec, vec, mat, mat],
                          out_specs=[vec, vec, vec, mat, mat], out_shape=[sv, sv, sv, sm, sm],
                          compiler_params=_cp(("parallel",)))(a_re, a_im, ls, b_re, b_im, d_lr, d_li, d_br, d_bi)


def _cmul(ar, ai, br, bi):
    return ar * br - ai * bi, ar * bi + ai * br


def _segment_carries(lr, li, er, ei, n_steps, reverse):
    pr, pi = lr, li
    for _ in range(int(math.log2(n_steps))):
        pr, pi = _cmul(pr, pi, pr, pi)
    row = lax.broadcasted_iota(jnp.int32, er.shape, 0)
    edge = (SSM_SEGMENTS - 1) if reverse else 0
    shift = (SSM_SEGMENTS - 1) if reverse else 1
    cr = jnp.zeros_like(er)
    ci = jnp.zeros_like(ei)
    for _ in range(SSM_SEGMENTS - 1):
        tr_, ti_ = _cmul(pr, pi, cr, ci)
        cr = jnp.where(row == edge, 0.0, pltpu.roll(tr_ + er, shift, 0))
        ci = jnp.where(row == edge, 0.0, pltpu.roll(ti_ + ei, shift, 0))
    return cr, ci


def _s5_geometry(S, D):
    assert S % SSM_SEGMENTS == 0 and D % S5_CB == 0
    n_steps = S // SSM_SEGMENTS
    assert n_steps & (n_steps - 1) == 0, "segment length must be a power of two"
    rows = min(S5_ROWS, S)
    return n_steps, rows, S // rows, D // S5_CB


def _s5_scan(u, b_re, b_im, c_re, c_im, lam_re, lam_im, ends, descending, name):
    S, D = u.shape
    n_steps, rows, nch, ncb = _s5_geometry(S, D)
    full = ends is not None
    GN = ncb * S5_SB

    def body(*refs):
        if full:
            (u_ref, br_ref, bi_ref, cr_ref, ci_ref, lr_ref, li_ref, er_ref, ei_ref,
             xr_ref, xi_ref, y_ref, st_r, st_i, buf_r, buf_i) = refs
        else:
            u_ref, br_ref, bi_ref, lr_ref, li_ref, er_ref, ei_ref, st_r, st_i, buf_r, buf_i = refs
        lr = jnp.broadcast_to(lr_ref[...], (SSM_SEGMENTS, S5_SB))
        li = jnp.broadcast_to(li_ref[...], (SSM_SEGMENTS, S5_SB))

        @pl.when(pl.program_id(1) == 0)
        def _():
            if full:
                st_r[...], st_i[...] = _segment_carries(lr, li, er_ref[...], ei_ref[...], n_steps, descending)
            else:
                st_r[...] = jnp.zeros_like(st_r)
                st_i[...] = jnp.zeros_like(st_i)

        ub = u_ref[...].astype(BF16)
        buf_r[...] = jnp.dot(ub, br_ref[...], preferred_element_type=F32)
        buf_i[...] = jnp.dot(ub, bi_ref[...], preferred_element_type=F32)

        n_it = rows // SSM_SEGMENTS

        def step(i, carry):
            sr, si = carry
            i = n_it - 1 - i if descending else i
            r = pl.ds(pl.multiple_of(i * SSM_SEGMENTS, SSM_SEGMENTS), SSM_SEGMENTS)
            if full:
                xr_ref[r, :] = sr
                xi_ref[r, :] = si
            nr = lr * sr - li * si + buf_r[r, :]
            ni = lr * si + li * sr + buf_i[r, :]
            if full:
                buf_r[r, :] = nr
                buf_i[r, :] = ni
            return nr, ni

        sr, si = lax.fori_loop(0, n_it, step, (st_r[...], st_i[...]))
        st_r[...] = sr
        st_i[...] = si
        if full:
            y_ref[...] = (jnp.dot(buf_r[...].astype(BF16), cr_ref[...], preferred_element_type=F32)
                          - jnp.dot(buf_i[...].astype(BF16), ci_ref[...], preferred_element_type=F32))
        else:
            er_ref[...] = sr
            ei_ref[...] = si

    chunk = (lambda c: nch - 1 - c) if descending else (lambda c: c)
    u_spec = pl.BlockSpec((rows, S5_CB), lambda b, c: (chunk(c), b))
    bmat = pl.BlockSpec((None, S5_CB, S5_SB), lambda b, c: (b, 0, 0))
    cmat = pl.BlockSpec((None, S5_SB, S5_CB), lambda b, c: (b, 0, 0))
    lvec = pl.BlockSpec((1, S5_SB), lambda b, c: (0, b))
    evec = pl.BlockSpec((SSM_SEGMENTS, S5_SB), lambda b, c: (0, b))
    xblk = pl.BlockSpec((rows, S5_SB), lambda b, c: (chunk(c), b))
    scratch = [pltpu.VMEM((SSM_SEGMENTS, S5_SB), F32)] * 2 + [pltpu.VMEM((rows, S5_SB), F32)] * 2
    e_shape = jax.ShapeDtypeStruct((SSM_SEGMENTS, GN), F32)
    if full:
        x_shape = jax.ShapeDtypeStruct((S, GN), F32)
        return pl.pallas_call(
            body, name=name, grid=(ncb, nch),
            in_specs=[u_spec, bmat, bmat, cmat, cmat, lvec, lvec, evec, evec],
            out_specs=[xblk, xblk, u_spec], out_shape=[x_shape, x_shape, jax.ShapeDtypeStruct((S, D), F32)],
            scratch_shapes=scratch, compiler_params=_cp(("parallel", "arbitrary")),
        )(u, b_re, b_im, c_re, c_im, lam_re, lam_im, *ends)
    return pl.pallas_call(
        body, name=name, grid=(ncb, nch), in_specs=[u_spec, bmat, bmat, lvec, lvec],
        out_specs=[evec, evec], out_shape=[e_shape, e_shape],
        scratch_shapes=scratch, compiler_params=_cp(("parallel", "arbitrary")),
    )(u, b_re, b_im, lam_re, lam_im)


def _s5_scan_bwd(dy, u, xp, b_re, b_im, c_re, c_im, lam_re, lam_im, starts, descending, name):
    S, D = dy.shape
    n_steps, rows, nch, ncb = _s5_geometry(S, D)
    full = starts is not None
    GN = ncb * S5_SB
    nt = (((1,), (1,)), ((), ()))
    tn = (((0,), (0,)), ((), ()))

    def body(*refs):
        if full:
            (dy_ref, u_ref, xr_ref, xi_ref, br_ref, bi_ref, cr_ref, ci_ref, lr_ref, li_ref, gr_ref, gi_ref,
             du_ref, dbr_ref, dbi_ref, dcr_ref, dci_ref, dlr_ref, dli_ref, st_r, st_i, buf_r, buf_i) = refs
        else:
            dy_ref, cr_ref, ci_ref, lr_ref, li_ref, gr_ref, gi_ref, st_r, st_i, buf_r, buf_i = refs
        lr = jnp.broadcast_to(lr_ref[...], (SSM_SEGMENTS, S5_SB))
        li = jnp.broadcast_to(li_ref[...], (SSM_SEGMENTS, S5_SB))

        @pl.when(pl.program_id(1) == 0)
        def _():
            if full:
                st_r[...], st_i[...] = _segment_carries(lr, -li, gr_ref[...], gi_ref[...], n_steps, descending)
                for r in (dbr_ref, dbi_ref, dcr_ref, dci_ref, dlr_ref, dli_ref):
                    r[...] = jnp.zeros_like(r)
            else:
                st_r[...] = jnp.zeros_like(st_r)
                st_i[...] = jnp.zeros_like(st_i)

        dyb = dy_ref[...].astype(BF16)
        buf_r[...] = lax.dot_general(dyb, cr_ref[...], nt, preferred_element_type=F32)
        buf_i[...] = -lax.dot_general(dyb, ci_ref[...], nt, preferred_element_type=F32)
        n_it = rows // SSM_SEGMENTS

        def step(j, carry):
            gr, gi = carry
            j = n_it - 1 - j if descending else j
            r = pl.ds(pl.multiple_of(j * SSM_SEGMENTS, SSM_SEGMENTS), SSM_SEGMENTS)
            nr = lr * gr + li * gi + buf_r[r, :]
            ni = lr * gi - li * gr + buf_i[r, :]
            if full:
                buf_r[r, :] = nr
                buf_i[r, :] = ni
            return nr, ni

        gr, gi = lax.fori_loop(0, n_it, step, (st_r[...], st_i[...]))
        st_r[...] = gr
        st_i[...] = gi
        if not full:
            gr_ref[...] = gr
            gi_ref[...] = gi
            return
        g_r, g_i = buf_r[...], buf_i[...]
        xr, xi = xr_ref[...], xi_ref[...]
        dlr_ref[...] += jnp.sum(g_r * xr + g_i * xi, axis=0, keepdims=True)
        dli_ref[...] += jnp.sum(g_i * xr - g_r * xi, axis=0, keepdims=True)
        ub = u_ref[...].astype(BF16)
        gb_r, gb_i = g_r.astype(BF16), g_i.astype(BF16)
        du_ref[...] = (lax.dot_general(gb_r, br_ref[...], nt, preferred_element_type=F32)
                       + lax.dot_general(gb_i, bi_ref[...], nt, preferred_element_type=F32))
        dbr_ref[...] += lax.dot_general(ub, gb_r, tn, preferred_element_type=F32)
        dbi_ref[...] += lax.dot_general(ub, gb_i, tn, preferred_element_type=F32)
        lr_, li_ = lr_ref[...], li_ref[...]
        x_r = lr_ * xr - li_ * xi + jnp.dot(ub, br_ref[...], preferred_element_type=F32)
        x_i = lr_ * xi + li_ * xr + jnp.dot(ub, bi_ref[...], preferred_element_type=F32)
        dcr_ref[...] += lax.dot_general(x_r.astype(BF16), dyb, tn, preferred_element_type=F32)
        dci_ref[...] -= lax.dot_general(x_i.astype(BF16), dyb, tn, preferred_element_type=F32)

    rev = (lambda c: nch - 1 - c) if descending else (lambda c: c)
    u_spec = pl.BlockSpec((rows, S5_CB), lambda b, c: (rev(c), b))
    bmat = pl.BlockSpec((None, S5_CB, S5_SB), lambda b, c: (b, 0, 0))
    cmat = pl.BlockSpec((None, S5_SB, S5_CB), lambda b, c: (b, 0, 0))
    lvec = pl.BlockSpec((1, S5_SB), lambda b, c: (0, b))
    evec = pl.BlockSpec((SSM_SEGMENTS, S5_SB), lambda b, c: (0, b))
    xblk = pl.BlockSpec((rows, S5_SB), lambda b, c: (rev(c), b))
    scratch = [pltpu.VMEM((SSM_SEGMENTS, S5_SB), F32)] * 2 + [pltpu.VMEM((rows, S5_SB), F32)] * 2
    e_shape = jax.ShapeDtypeStruct((SSM_SEGMENTS, GN), F32)
    if full:
        return pl.pallas_call(
            body, name=name, grid=(ncb, nch),
            in_specs=[u_spec, u_spec, xblk, xblk, bmat, bmat, cmat, cmat, lvec, lvec, evec, evec],
            out_specs=[u_spec, bmat, bmat, cmat, cmat, lvec, lvec],
            out_shape=[jax.ShapeDtypeStruct((S, D), F32), jax.ShapeDtypeStruct(b_re.shape, F32),
                       jax.ShapeDtypeStruct(b_re.shape, F32), jax.ShapeDtypeStruct(c_re.shape, F32),
                       jax.ShapeDtypeStruct(c_re.shape, F32), jax.ShapeDtypeStruct((1, GN), F32),
                       jax.ShapeDtypeStruct((1, GN), F32)],
            scratch_shapes=scratch, compiler_params=_cp(("parallel", "arbitrary")),
        )(dy, u, *xp, b_re, b_im, c_re, c_im, lam_re, lam_im, *starts)
    return pl.pallas_call(
        body, name=name, grid=(ncb, nch), in_specs=[u_spec, cmat, cmat, lvec, lvec],
        out_specs=[evec, evec], out_shape=[e_shape, e_shape],
        scratch_shapes=scratch, compiler_params=_cp(("parallel", "arbitrary")),
    )(dy, c_re, c_im, lam_re, lam_im)


def _s5_perm(t):
    S, D = t.shape
    return t.reshape(SSM_SEGMENTS, S // SSM_SEGMENTS, D).transpose(1, 0, 2).reshape(S, D)


def _s5_unperm(t):
    S, D = t.shape
    return t.reshape(S // SSM_SEGMENTS, SSM_SEGMENTS, D).transpose(1, 0, 2).reshape(S, D)


def _s5_blockdiag_b(bb, ncb):
    gpb = S5_CB // SSM_GROUP_CH
    t = bb.reshape(SSM_GROUP_CH, ncb, gpb, SSM_STATE)
    return jnp.einsum('cbgn,gh->bgchn', t, jnp.eye(gpb, dtype=bb.dtype)).reshape(ncb, S5_CB, S5_SB)


def _s5_blockdiag_b_t(dblk):
    ncb = dblk.shape[0]
    gpb = S5_CB // SSM_GROUP_CH
    t = dblk.reshape(ncb, gpb, SSM_GROUP_CH, gpb, SSM_STATE)
    return jnp.einsum('bgchn,gh->cbgn', t, jnp.eye(gpb, dtype=dblk.dtype)).reshape(SSM_GROUP_CH, -1)


def _s5_blockdiag_c(c, ncb):
    gpb = S5_CB // SSM_GROUP_CH
    t = c.reshape(ncb, gpb, SSM_GROUP_CH, SSM_STATE)
    return jnp.einsum('bgcn,gh->bgnhc', t, jnp.eye(gpb, dtype=c.dtype)).reshape(ncb, S5_SB, S5_CB)


def _s5_blockdiag_c_t(dblk):
    ncb = dblk.shape[0]
    gpb = S5_CB // SSM_GROUP_CH
    t = dblk.reshape(ncb, gpb, SSM_STATE, gpb, SSM_GROUP_CH)
    return jnp.einsum('bgnhc,gh->bgcn', t, jnp.eye(gpb, dtype=dblk.dtype)).reshape(-1, SSM_GROUP_CH, SSM_STATE)


_GELU_C = math.sqrt(2.0 / math.pi)


def _gelu(y):
    return y * (0.5 * (1.0 + jnp.tanh(_GELU_C * (y + 0.044715 * (y * y * y)))))


def _gelu_grad(y):
    t = jnp.tanh(_GELU_C * (y + 0.044715 * (y * y * y)))
    return 0.5 * (1.0 + t) + 0.5 * y * (1.0 - t * t) * (_GELU_C * (1.0 + 3.0 * 0.044715 * y * y))


def _rowwise(fn, ins, outs, name, acc=()):
    S, D = next(a.shape for a in ins if a.shape[0] != 1)
    tr = _pick(S, (256, 128, 8))
    row = pl.BlockSpec((tr, D), lambda i: (i, 0))
    vec = pl.BlockSpec((1, D), lambda i: (0, 0))
    n_in = len(ins)

    def body(*refs):
        res = fn(*[r[...] for r in refs[:n_in]])
        for k, (o, v) in enumerate(zip(refs[n_in:], res)):
            if k in acc:
                @pl.when(pl.program_id(0) == 0)
                def _():
                    o[...] = jnp.zeros_like(o)
                o[...] += jnp.sum(v, axis=0, keepdims=True)
            else:
                o[...] = v.astype(o.dtype)

    return pl.pallas_call(
        body, name=name, grid=(S // tr,), in_specs=[vec if a.shape[0] == 1 else row for a in ins],
        out_specs=[vec if k in acc else row for k in range(len(outs))],
        out_shape=[jax.ShapeDtypeStruct((1, D) if k in acc else (S, D), dt) for k, dt in enumerate(outs)],
        compiler_params=_cp(("arbitrary",) if acc else ("parallel",)),
    )(*ins)


def _s5_params(p):
    G, N = p["a_re"].shape[1:]
    vec = lambda a: a.reshape(2, 1, G * N)
    ls = jnp.broadcast_to(p["log_step"][:, :, None], (2, G, N))
    bt = lambda b: b.transpose(0, 3, 1, 2).reshape(2, SSM_GROUP_CH, G * N)
    return vec(p["a_re"]), vec(p["a_im"]), vec(ls), bt(p["b_re"]), bt(p["b_im"])


def _s5_layer_fwd(x, norm_g, p, w_glu, tag):
    S, D = x.shape
    ncb = D // S5_CB
    xp = _s5_perm(x)
    hn = _rmsnorm_fwd(xp, norm_g, F32, f"s5_norm_{tag}")
    raw = _s5_params(p)
    lam_r, lam_i, bb_r, bb_i = _s5_disc_fwd(*raw, f"s5_disc_{tag}")
    dirs = []
    ys = []
    for dirn in range(2):
        mats = (_s5_blockdiag_b(bb_r[dirn], ncb).astype(BF16), _s5_blockdiag_b(bb_i[dirn], ncb).astype(BF16),
                _s5_blockdiag_c(p["c_re"][dirn], ncb).astype(BF16), _s5_blockdiag_c(p["c_im"][dirn], ncb).astype(BF16))
        lam = (lam_r[dirn], lam_i[dirn])
        ends = _s5_scan(hn, mats[0], mats[1], None, None, *lam, None, dirn == 1, f"s5_ends_{tag}_{dirn}")
        xr, xi, y = _s5_scan(hn, *mats, *lam, ends, dirn == 1, f"s5_scan_{tag}_{dirn}")
        dirs.append(((xr, xi), mats, lam))
        ys.append(y)
    ytot, z = _rowwise(lambda u, d, a, b: ((lambda y: (y, _gelu(y)))(d * u + a + b)),
                       [hn, p["d"], ys[0], ys[1]], [F32, BF16], f"s5_gelu_{tag}")
    t = _mm(z, w_glu, name=f"s5_glu_{tag}", epi=lambda r, b: r + b, epi_in=(p["b_glu"],))
    (x_new,) = _rowwise(lambda xx, zz, tt: (xx + zz.astype(F32) * jax.nn.sigmoid(tt),),
                        [xp, z, t], [F32], f"s5_out_{tag}")
    return _s5_unperm(x_new), (xp, hn, raw, dirs, ytot, z, t)


def _s5_layer_bwd(dres, saved, norm_g, p, w_glu, tag):
    x, hn, raw, dirs, ytot, z, t = saved
    S, D = x.shape
    G, N = p["a_re"].shape[1:]
    dres = _s5_perm(dres[0])

    def glu_bwd(do, zz, tt):
        sg = jax.nn.sigmoid(tt)
        dt = do * zz.astype(F32) * (sg * (1.0 - sg))
        return dt, do * sg, dt

    dt, dzd, db_glu = _rowwise(glu_bwd, [dres, z, t], [BF16, F32, F32], f"s5_out_bwd_{tag}", acc=(2,))
    dz = _mm(dt, w_glu, tb=True, name=f"s5_dz_{tag}", epi=_add_epi, epi_in=(dzd,))
    dw_glu = _mm(z, dt, ta=True, name=f"s5_dwglu_{tag}")

    def gelu_bwd(dzz, y, u, d):
        dy = dzz * _gelu_grad(y)
        return dy, dy * d, dy * u

    dy, du, dd = _rowwise(gelu_bwd, [dz, ytot, hn, p["d"]], [F32, F32, F32], f"s5_gelu_bwd_{tag}", acc=(2,))
    d_lr, d_li, d_bbr, d_bbi, d_cr, d_ci = [], [], [], [], [], []
    du = [du]
    for dirn in range(2):
        xp, mats, lam = dirs[dirn]
        starts = _s5_scan_bwd(dy, None, None, None, None, mats[2], mats[3], *lam, None, dirn == 0,
                              f"s5_starts_{tag}_{dirn}")
        dup, dbr, dbi, dcr, dci, dlr, dli = _s5_scan_bwd(dy, hn, xp, *mats, *lam, starts, dirn == 0,
                                                         f"s5_scan_bwd_{tag}_{dirn}")
        du.append(dup)
        d_lr.append(dlr)
        d_li.append(dli)
        d_bbr.append(_s5_blockdiag_b_t(dbr))
        d_bbi.append(_s5_blockdiag_b_t(dbi))
        d_cr.append(_s5_blockdiag_c_t(dcr))
        d_ci.append(_s5_blockdiag_c_t(dci))
    da_re, da_im, dls, db_re, db_im = _s5_disc_bwd(*raw, jnp.stack(d_lr), jnp.stack(d_li), jnp.stack(d_bbr),
                                                   jnp.stack(d_bbi), f"s5_disc_bwd_{tag}")
    dres, dg = _rmsnorm_bwd(x, norm_g, du, dres, f"s5_norm_bwd_{tag}")
    dres = tuple(_s5_unperm(t_) for t_ in dres)
    unb = lambda b: b.reshape(2, SSM_GROUP_CH, G, N).transpose(0, 2, 3, 1)
    grads = dict(a_re=da_re.reshape(2, G, N), a_im=da_im.reshape(2, G, N), log_step=dls.reshape(2, G, N).sum(-1),
                 b_re=unb(db_re), b_im=unb(db_im), c_re=jnp.stack(d_cr), c_im=jnp.stack(d_ci),
                 d=dd, w_glu=dw_glu, b_glu=db_glu, norm=dg)
    return dres, grads


def _adamw(w, g, m, v, name):
    R, C = w.shape
    tr = _pick(R, (512, 256, 128, 64, 32, 16, 8))
    tn = _pick(C, (512, 256, 128))

    def body(w_ref, g_ref, m_ref, v_ref, d_ref, nm_ref, nv_ref):
        gg = g_ref[...]
        m2 = ADAM_B1 * m_ref[...] + (1.0 - ADAM_B1) * gg
        v2 = ADAM_B2 * v_ref[...] + (1.0 - ADAM_B2) * (gg * gg)
        m_hat = m2 / (1.0 - ADAM_B1 ** ADAM_STEP)
        v_hat = v2 / (1.0 - ADAM_B2 ** ADAM_STEP)
        d_ref[...] = -ADAM_LR * (m_hat / (jnp.sqrt(v_hat) + ADAM_EPS) + ADAM_WD * w_ref[...])
        nm_ref[...] = m2
        nv_ref[...] = v2

    blk = pl.BlockSpec((tr, tn), lambda i, j: (i, j))
    shp = jax.ShapeDtypeStruct((R, C), F32)
    return pl.pallas_call(body, name=name, grid=(R // tr, C // tn), in_specs=[blk] * 4, out_specs=[blk] * 3,
                          out_shape=[shp] * 3, compiler_params=_cp(("parallel", "parallel")))(w, g, m, v)


def _add_selected(p, sel, others, name, also_bf16=False):
    K, _, M, C = p.shape
    tr = _pick(M, [t for t in (512, 256, 128, 64, 32, 16) if t * C * 4 <= 2 ** 21])
    n_o = len(others)

    def body(sel_ref, p_ref, *refs):
        acc = p_ref[...]
        for r in refs[:n_o]:
            acc = acc + r[...].astype(F32)
        refs[n_o][...] = acc
        if also_bf16:
            refs[n_o + 1][...] = acc.astype(BF16)

    blk = pl.BlockSpec((None, tr, C), lambda k, i, s: (k, i, 0))
    grid_spec = pltpu.PrefetchScalarGridSpec(
        num_scalar_prefetch=1, grid=(K, M // tr),
        in_specs=[pl.BlockSpec((None, None, tr, C), lambda k, i, s: (k, s[0], i, 0))] + [blk] * n_o,
        out_specs=[blk, blk] if also_bf16 else blk)
    shp = jax.ShapeDtypeStruct((K, M, C), F32)
    return pl.pallas_call(body, name=name, grid_spec=grid_spec,
                          out_shape=[shp, jax.ShapeDtypeStruct((K, M, C), BF16)] if also_bf16 else shp,
                          compiler_params=_cp(("parallel", "parallel")))(sel, p, *others)


def _sum_slots(a, own, me, name):
    n, R, C = a.shape
    tr = _pick(R, (512, 256, 128, 64, 32, 16, 8))

    def body(me_ref, a_ref, own_ref, o_ref):
        term = lambda k: jnp.where(me_ref[0] == k, own_ref[...], a_ref[k])
        acc = term(0)
        for k in range(1, n):
            acc = acc + term(k)
        o_ref[...] = acc

    grid_spec = pltpu.PrefetchScalarGridSpec(
        num_scalar_prefetch=1, grid=(R // tr,),
        in_specs=[pl.BlockSpec((n, tr, C), lambda i, s: (0, i, 0)), pl.BlockSpec((tr, C), lambda i, s: (i, 0))],
        out_specs=pl.BlockSpec((tr, C), lambda i, s: (i, 0)))
    return pl.pallas_call(body, name=name, grid_spec=grid_spec, out_shape=jax.ShapeDtypeStruct((R, C), F32),
                          compiler_params=_cp(("parallel",)))(me, a, own)


def _position():
    return lax.axis_index("x"), lax.axis_index("y"), lax.axis_index("c")


def _other_chips(x, y):
    return [(1 - x, y), (x, 1 - y), (1 - x, 1 - y)]


def _run_copies(sends, recvs, local, send_sems, recv_sems, local_sems):
    started = []
    for k, (src, dst, dev) in enumerate(sends):
        cp = pltpu.make_async_remote_copy(src_ref=src, dst_ref=dst, send_sem=send_sems.at[k],
                                          recv_sem=recv_sems.at[k], device_id=dev, device_id_type=MESH)
        cp.start()
        started.append(cp)
    locals_ = []
    for k, (src, dst) in enumerate(local):
        cp = pltpu.make_async_copy(src, dst, local_sems.at[k])
        cp.start()
        locals_.append(cp)
    for k, (dst, dev) in enumerate(recvs):
        pltpu.make_async_remote_copy(src_ref=dst, dst_ref=dst, send_sem=send_sems.at[k], recv_sem=recv_sems.at[k],
                                     device_id=dev, device_id_type=MESH).wait_recv()
    for cp in started:
        cp.wait_send()
    for cp in locals_:
        cp.wait()


def _comm_call(plan, ins, out_shapes, n_remote, n_local, name):
    n_in = len(ins)
    n_out = len(out_shapes)

    def body(*refs):
        in_refs, out_refs = refs[:n_in], refs[n_in:n_in + n_out]
        send_sems, recv_sems, local_sems = refs[n_in + n_out:]
        sends, recvs, local = plan(in_refs, out_refs)
        assert len(sends) == len(recvs) == n_remote and len(local) == n_local
        _run_copies(sends, recvs, local, send_sems, recv_sems, local_sems)

    hbm = pl.BlockSpec(memory_space=pltpu.HBM)
    return pl.pallas_call(
        body, name=name, in_specs=[hbm] * n_in, out_specs=[hbm] * n_out, out_shape=out_shapes,
        scratch_shapes=[pltpu.SemaphoreType.DMA((n_remote,)), pltpu.SemaphoreType.DMA((n_remote,)),
                        pltpu.SemaphoreType.DMA((max(n_local, 1),))],
    )(*ins)


def _gather_chips(arrs, name):
    n = len(arrs)

    def plan(ins, outs):
        x, y, c = _position()
        me = 2 * x + y
        sends, recvs = [], []
        for px, py in _other_chips(x, y):
            for i in range(n):
                sends.append((ins[i], outs[i].at[me], (px, py, c)))
                recvs.append((outs[i].at[2 * px + py], (px, py, c)))
        return sends, recvs, [(ins[i], outs[i].at[me]) for i in range(n)]

    shapes = [jax.ShapeDtypeStruct((4,) + a.shape, a.dtype) for a in arrs]
    return _comm_call(plan, arrs, shapes, 3 * n, n, name)


def _gather_chips_two_level(arrs, name):
    n = len(arrs)
    hr = [a.shape[1] // 2 for a in arrs]

    def body(*refs):
        ins, outs = refs[:n], refs[n:2 * n]
        send_sems, recv_sems = refs[2 * n:]
        x, y, c = _position()
        me = 2 * x + y
        sib = (x, y, 1 - c)
        chips = _other_chips(x, y)

        def remote(k, src, dst, dev):
            return pltpu.make_async_remote_copy(src_ref=src, dst_ref=dst, send_sem=send_sems.at[k],
                                                recv_sem=recv_sems.at[k], device_id=dev, device_id_type=MESH)

        first = []
        for j, (px, py) in enumerate(chips):
            for i in range(n):
                cp = remote(j * n + i, ins[i].at[:, pl.ds(c * hr[i], hr[i])], outs[i].at[me, c], (px, py, c))
                cp.start()
                first.append(cp)
        passed = []
        for j, (px, py) in enumerate(chips):
            for i in range(n):
                landed = outs[i].at[2 * px + py, c]
                remote(j * n + i, landed, landed, (px, py, c)).wait_recv()
                cp = remote(3 * n + j * n + i, landed, landed, sib)
                cp.start()
                passed.append(cp)
        for j, (px, py) in enumerate(chips):
            for i in range(n):
                slot = outs[i].at[2 * px + py, 1 - c]
                remote(3 * n + j * n + i, slot, slot, sib).wait_recv()
        for cp in first + passed:
            cp.wait_send()

    hbm = pl.BlockSpec(memory_space=pltpu.HBM)
    shapes = [jax.ShapeDtypeStruct((N_CHIPS, 2, a.shape[0], a.shape[1] // 2, a.shape[2]), a.dtype) for a in arrs]
    return pl.pallas_call(
        body, name=name, in_specs=[hbm] * n, out_specs=[hbm] * n, out_shape=shapes,
        scratch_shapes=[pltpu.SemaphoreType.DMA((6 * n,)), pltpu.SemaphoreType.DMA((6 * n,))],
    )(*arrs)


def _sibling_halves(pieces, name):
    n = len(pieces)

    def plan(ins, outs):
        x, y, c = _position()
        sib = (x, y, 1 - c)
        return ([(ins[i].at[:, 1 - c], outs[i], sib) for i in range(n)],
                [(outs[i], sib) for i in range(n)], [])

    shapes = [jax.ShapeDtypeStruct((p.shape[0],) + p.shape[2:], p.dtype) for p in pieces]
    return _comm_call(plan, pieces, shapes, n, 0, name)


def _scatter_chips(sums, name):
    n = len(sums)

    def plan(ins, outs):
        x, y, c = _position()
        sends, recvs = [], []
        for j, (px, py) in enumerate(_other_chips(x, y)):
            for i in range(n):
                sends.append((ins[i].at[2 * px + py], outs[i].at[j], (px, py, c)))
                recvs.append((outs[i].at[j], (px, py, c)))
        return sends, recvs, []

    shapes = [jax.ShapeDtypeStruct((3,) + s.shape[1:], s.dtype) for s in sums]
    return _comm_call(plan, sums, shapes, 3 * n, 0, name)


def _sibling_swap(halves, name):
    n = len(halves)

    def plan(ins, outs):
        x, y, c = _position()
        sib = (x, y, 1 - c)
        return [(ins[i], outs[i], sib) for i in range(n)], [(outs[i], sib) for i in range(n)], []

    shapes = [jax.ShapeDtypeStruct(h.shape, h.dtype) for h in halves]
    return _comm_call(plan, halves, shapes, n, 0, name)


def _gather_all(v, name):
    rels = [(dx, dy, dc) for dx in (0, 1) for dy in (0, 1) for dc in (0, 1)][1:]

    def plan(ins, outs):
        x, y, c = _position()
        me = 4 * x + 2 * y + c
        flip = lambda a, d: 1 - a if d else a
        sends, recvs = [], []
        for dx, dy, dc in rels:
            px, py, pc = flip(x, dx), flip(y, dy), flip(c, dc)
            sends.append((ins[0], outs[0].at[me], (px, py, pc)))
            recvs.append((outs[0].at[4 * px + 2 * py + pc], (px, py, pc)))
        return sends, recvs, []

    return _comm_call(plan, [v], [jax.ShapeDtypeStruct((8,) + v.shape, v.dtype)], len(rels), 0, name)[0]


WEIGHTS = ['mix_norm', 'ffn_norm', 'final_norm', 'attn_w_qkv', 'attn_w_o', 'attn_sink', 'ssm_a_re', 'ssm_a_im',
           'ssm_log_step', 'ssm_b_re', 'ssm_b_im', 'ssm_c_re', 'ssm_c_im', 'ssm_d', 'ssm_w_glu', 'ssm_b_glu',
           'mla_w_dqkv', 'mla_q_norm', 'mla_kv_norm', 'mla_w_uq', 'mla_w_ukv', 'mla_w_o', 'ffn_w_up',
           'ffn_conv_w', 'ffn_conv_b', 'ffn_w_down']
BIG = dict(attn_w_qkv='col', attn_w_o='row', ssm_w_glu='row', mla_w_dqkv='row', mla_w_uq='col',
           mla_w_ukv='col', mla_w_o='row', ffn_w_up='col', ffn_w_down='row')
SMALL_SHARDED = ('mla_q_norm', 'mla_kv_norm', 'ffn_conv_w')
N_CHIPS = 4


def _assemble(g, own, chip, kind):
    L = g.shape[2]
    own_halves = own.reshape(L, 2, g.shape[3], g.shape[4]).transpose(1, 0, 2, 3)
    slot = lax.broadcasted_iota(jnp.int32, (N_CHIPS, 1, 1, 1, 1), 0)
    g = jnp.where(slot == chip, own_halves[None], g)
    if kind == 'row':
        return g.transpose(2, 0, 1, 3, 4).reshape(L, -1, g.shape[4])
    return g.transpose(2, 1, 3, 0, 4).reshape(L, 2 * g.shape[3], -1)


def _to_pieces(w, kind):
    L, R, C = w.shape
    if kind == 'col':
        t = w.reshape(L, 2, R // 2, N_CHIPS, C // N_CHIPS).transpose(3, 1, 0, 2, 4)
    else:
        t = w.reshape(L, N_CHIPS, R // N_CHIPS, 2, C // 2).transpose(1, 3, 0, 2, 4)
    return t.reshape(N_CHIPS, 2, L * t.shape[3], t.shape[4])


def _from_halves(h, kind, shard_shape):
    L = shard_shape[0]
    t = h.reshape(2, L, -1, h.shape[2])
    t = t.transpose(1, 0, 2, 3) if kind == 'col' else t.transpose(1, 2, 0, 3)
    return t.reshape(shard_shape)


def _pack(arrs):
    flat = jnp.concatenate([a.reshape(-1) for a in arrs])
    pad = (-flat.shape[0]) % (8 * LANES)
    return jnp.pad(flat, (0, pad)).reshape(-1, LANES)


def _unpack(packed, like):
    flat = packed.reshape(-1)
    out, off = [], 0
    for a in like:
        out.append(flat[off:off + a.size].reshape(a.shape))
        off += a.size
    return out


def kernel(x, mix_norm, ffn_norm, final_norm, attn_w_qkv, attn_w_o, attn_sink, ssm_a_re, ssm_a_im, ssm_log_step, ssm_b_re, ssm_b_im, ssm_c_re, ssm_c_im, ssm_d, ssm_w_glu, ssm_b_glu, mla_w_dqkv, mla_q_norm, mla_kv_norm, mla_w_uq, mla_w_ukv, mla_w_o, ffn_w_up, ffn_conv_w, ffn_conv_b, ffn_w_down, loss_target, m_mix_norm, m_ffn_norm, m_final_norm, m_attn_w_qkv, m_attn_w_o, m_attn_sink, m_ssm_a_re, m_ssm_a_im, m_ssm_log_step, m_ssm_b_re, m_ssm_b_im, m_ssm_c_re, m_ssm_c_im, m_ssm_d, m_ssm_w_glu, m_ssm_b_glu, m_mla_w_dqkv, m_mla_q_norm, m_mla_kv_norm, m_mla_w_uq, m_mla_w_ukv, m_mla_w_o, m_ffn_w_up, m_ffn_conv_w, m_ffn_conv_b, m_ffn_w_down, v_mix_norm, v_ffn_norm, v_final_norm, v_attn_w_qkv, v_attn_w_o, v_attn_sink, v_ssm_a_re, v_ssm_a_im, v_ssm_log_step, v_ssm_b_re, v_ssm_b_im, v_ssm_c_re, v_ssm_c_im, v_ssm_d, v_ssm_w_glu, v_ssm_b_glu, v_mla_w_dqkv, v_mla_q_norm, v_mla_kv_norm, v_mla_w_uq, v_mla_w_ukv, v_mla_w_o, v_ffn_w_up, v_ffn_conv_w, v_ffn_conv_b, v_ffn_w_down):
    args = locals()
    w = {n: args[n] for n in WEIGHTS}
    mom = {n: args["m_" + n] for n in WEIGHTS}
    var = {n: args["v_" + n] for n in WEIGHTS}
    depth = mix_norm.shape[0]
    xs = x[0]
    target = loss_target[0]
    chip = 2 * lax.axis_index("x") + lax.axis_index("y")
    core = lax.axis_index("c")

    big_names = list(BIG)
    w_bf = [w[n].astype(BF16) for n in big_names]
    gathered = _gather_chips_two_level(w_bf, "gather_weights")
    full = {n: _assemble(g, own, chip, BIG[n]) for n, g, own in zip(big_names, gathered, w_bf)}
    for n, g in zip(SMALL_SHARDED, _gather_chips([w[n] for n in SMALL_SHARDED], "gather_small_weights")):
        full[n] = jnp.moveaxis(g, 0, -2).reshape(g.shape[1:-1] + (-1,))

    def ffn_args(i):
        return (ffn_norm[i:i + 1], full["ffn_w_up"][i], full["ffn_conv_w"][i], ffn_conv_b[i:i + 1], full["ffn_w_down"][i])

    def s5_params(j):
        return dict(a_re=ssm_a_re[j], a_im=ssm_a_im[j], log_step=ssm_log_step[j], b_re=ssm_b_re[j], b_im=ssm_b_im[j],
                    c_re=ssm_c_re[j], c_im=ssm_c_im[j], d=ssm_d[j:j + 1], b_glu=ssm_b_glu[j:j + 1])

    def mla_args(j):
        wd, wq, wkv = _mla_weights(full["mla_w_dqkv"][j], full["mla_w_uq"][j], full["mla_w_ukv"][j])
        return (wd, wq, wkv, full["mla_w_o"][j], full["mla_q_norm"][j:j + 1], full["mla_kv_norm"][j:j + 1])

    h = xs
    saved = []
    for i in range(depth):
        kind, j = i % 3, i // 3
        g = mix_norm[i:i + 1]
        if kind == 0:
            h, sm = _attn_layer_fwd(h, g, full["attn_w_qkv"][j], full["attn_w_o"][j], attn_sink[j], f"{i}")
        elif kind == 1:
            h, sm = _s5_layer_fwd(h, g, s5_params(j), full["ssm_w_glu"][j], f"{i}")
        else:
            h, sm = _mla_layer_fwd(h, g, *mla_args(j), f"{i}")
        h, sf = _ffn_fwd(h, *ffn_args(i), f"{i}")
        saved.append((sm, sf))
    loss_part, dres, d_final = _loss_head(h, final_norm.reshape(1, -1), target, "loss_head")

    gl = {n: [None] * w[n].shape[0] for n in WEIGHTS if n != 'final_norm'}
    for i in reversed(range(depth)):
        kind, j = i % 3, i // 3
        sm, sf = saved[i]
        dres, (dg, dwu, dcw, dcb, dwd) = _ffn_bwd(dres, sf, *ffn_args(i), f"{i}")
        gl['ffn_norm'][i], gl['ffn_w_up'][i], gl['ffn_conv_w'][i] = dg[0], dwu, dcw
        gl['ffn_conv_b'][i], gl['ffn_w_down'][i] = dcb[0], dwd
        g = mix_norm[i:i + 1]
        if kind == 0:
            dres, (dg, dwq, dwo, dsk) = _attn_layer_bwd(dres, sm, g, full["attn_w_qkv"][j], full["attn_w_o"][j],
                                                        attn_sink[j], f"{i}")
            gl['attn_w_qkv'][j], gl['attn_w_o'][j], gl['attn_sink'][j] = dwq, dwo, dsk
        elif kind == 1:
            dres, gs = _s5_layer_bwd(dres, sm, g, s5_params(j), full["ssm_w_glu"][j], f"{i}")
            dg = gs['norm']
            for k in ('a_re', 'a_im', 'log_step', 'b_re', 'b_im', 'c_re', 'c_im'):
                gl['ssm_' + k][j] = gs[k]
            gl['ssm_d'][j], gl['ssm_b_glu'][j], gl['ssm_w_glu'][j] = gs['d'][0], gs['b_glu'][0], gs['w_glu']
        else:
            dres, (dg, dwd_, dqn, dkvn, dwuq, dwukv, dwo) = _mla_layer_bwd(dres, sm, g, *mla_args(j), f"{i}")
            gl['mla_w_dqkv'][j], gl['mla_q_norm'][j], gl['mla_kv_norm'][j] = dwd_, dqn[0], dkvn[0]
            gl['mla_w_uq'][j], gl['mla_w_ukv'][j], gl['mla_w_o'][j] = dwuq, dwukv, dwo
        gl['mix_norm'][i] = dg[0]
    local = {n: jnp.stack(v) for n, v in gl.items()}
    local['final_norm'] = d_final[0]

    pieces = [_to_pieces(local[n], BIG[n]) for n in big_names]
    from_sib = _sibling_halves(pieces, "reduce_sibling")
    sel_core = jnp.reshape(core, (1,)).astype(jnp.int32)
    pair = [_add_selected(p, sel_core, [r], f"reduce_add_pair_{n}", also_bf16=True)
            for n, p, r in zip(big_names, pieces, from_sib)]
    from_chips = _scatter_chips([pb for _, pb in pair], "reduce_chips")
    sel_chip = jnp.reshape(chip, (1,)).astype(jnp.int32)
    mine = [_add_selected(p.reshape((1,) + p.shape), sel_chip, [r[k:k + 1] for k in range(3)], f"reduce_add_chips_{n}")[0]
            for n, (p, _), r in zip(big_names, pair, from_chips)]
    theirs = _sibling_swap(mine, "reduce_join")
    grads = {n: _from_halves(jnp.where(core == 0, jnp.stack([a, b]), jnp.stack([b, a])), BIG[n], w[n].shape)
             for n, a, b in zip(big_names, mine, theirs)}

    small_names = [n for n in WEIGHTS if n not in BIG]
    packed = _pack([local[n] for n in small_names])
    device = jnp.reshape(2 * chip + core, (1,)).astype(jnp.int32)
    summed = _sum_slots(_gather_all(packed, "reduce_small"), packed, device, "reduce_small_sum")
    for n, gsum in zip(small_names, _unpack(summed, [local[n] for n in small_names])):
        if n in SMALL_SHARDED:
            width = w[n].shape[-1]
            gsum = lax.dynamic_slice_in_dim(gsum, chip * width, width, axis=gsum.ndim - 1)
        grads[n] = gsum

    delta, new_m, new_v = {}, {}, {}
    for n in big_names:
        two_d = lambda a: a.reshape(-1, a.shape[-1])
        d_, m_, v_ = _adamw(two_d(w[n]), two_d(grads[n]), two_d(mom[n]), two_d(var[n]), f"adamw_{n}")
        delta[n], new_m[n], new_v[n] = (t.reshape(w[n].shape) for t in (d_, m_, v_))
    outs = _adamw(*[_pack([src[n] for n in small_names]) for src in (w, grads, mom, var)], "adamw_small")
    for dst, packed in zip((delta, new_m, new_v), outs):
        for n, t in zip(small_names, _unpack(packed, [w[n] for n in small_names])):
            dst[n] = t

    loss = lax.psum(loss_part[0, 0], ("x", "y", "c"))
    return (loss, dres[0][None], *[grads[n] for n in WEIGHTS], *[delta[n] for n in WEIGHTS],
            *[new_m[n] for n in WEIGHTS], *[new_v[n] for n in WEIGHTS])
```

```python
import functools
import math

import numpy as np
import jax
import jax.numpy as jnp
from jax import lax
from jax.experimental import pallas as pl
from jax.experimental.pallas import tpu as pltpu

F32 = jnp.float32
BF16 = jnp.bfloat16
MESH = pl.DeviceIdType.MESH

RMS_EPS = 1e-6
ATTN_DH = 64
ATTN_GROUP = 8
ATTN_BLOCK = 128
SSM_GROUP_CH = 16
SSM_STATE = 64
SSM_SEGMENTS = 8
MLA_HEADS = 16
MLA_LORA = 512
MLA_NOPE = 128
MLA_ROPE = 64
MLA_V = 128
ROPE_THETA = 10000.0
LANES = 128
VMEM_LIMIT = 56 * 1024 * 1024

ADAM_LR = 0.001
ADAM_B1 = 0.9
ADAM_B2 = 0.999
ADAM_EPS = 1e-08
ADAM_WD = 0.01
ADAM_STEP = 10


def _cp(sem=None):
    kw = dict(vmem_limit_bytes=VMEM_LIMIT)
    if sem is not None:
        kw["dimension_semantics"] = sem
    return pltpu.CompilerParams(**kw)


MM_FULL_K = 2048
MM_FULL_N = 1536


def _pick(n, cands):
    for c in cands:
        if n % c == 0:
            return c
    return n


def _gmm(a, b, *, grid, a_spec, b_spec, o_spec, out_shape, dims, kax, acc_shape, name,
         epi=None, epi_in=(), epi_specs=()):
    nk = grid[kax]
    n_epi = len(epi_in)

    def finish(r, e_refs, o_ref):
        if epi is not None:
            r = epi(r, *[e[...] for e in e_refs])
        o_ref[...] = r.astype(o_ref.dtype)

    def product(a_ref, b_ref):
        return lax.dot_general(a_ref[...].astype(BF16), b_ref[...].astype(BF16), dims, preferred_element_type=F32)

    def body_one(a_ref, b_ref, *rest):
        finish(product(a_ref, b_ref), rest[:n_epi], rest[n_epi])

    def body_acc(a_ref, b_ref, *rest):
        acc_ref = rest[n_epi + 1]
        k = pl.program_id(kax)

        @pl.when(k == 0)
        def _():
            acc_ref[...] = jnp.zeros_like(acc_ref)

        acc_ref[...] += product(a_ref, b_ref)

        @pl.when(k == nk - 1)
        def _():
            finish(acc_ref[...], rest[:n_epi], rest[n_epi])

    sem = tuple("arbitrary" if i == kax else "parallel" for i in range(len(grid)))
    return pl.pallas_call(
        body_one if nk == 1 else body_acc, name=name, grid=grid,
        in_specs=[a_spec, b_spec, *epi_specs], out_specs=o_spec, out_shape=out_shape,
        scratch_shapes=[] if nk == 1 else [pltpu.VMEM(acc_shape, F32)], compiler_params=_cp(sem),
    )(a, b, *epi_in)


def _mm(a, b, *, ta=False, tb=False, out_dtype=F32, name, epi=None, epi_in=(), pieces=None):
    a_parts = a.shape[0] if a.ndim == 3 else 1
    b_parts = b.shape[0] if b.ndim == 3 else 1
    assert not (ta and a_parts > 1) and not (tb and b_parts > 1)
    (K, M) = a.shape if ta else (a.shape[-1] * a_parts, a.shape[-2])
    (N, K2) = b.shape if tb else (b.shape[-1] * b_parts, b.shape[-2])
    assert K == K2, (a.shape, b.shape, ta, tb)
    m_unit = {None: M, 'col': M // 2, 'row': M // N_CHIPS}[pieces]
    n_unit = {None: N // b_parts, 'col': N // N_CHIPS, 'row': N // 2}[pieces]
    k_unit = K // a_parts
    tm = _pick(m_unit, (1024, 1408, 512, 256, 128))
    tn = n_unit if n_unit <= MM_FULL_N else _pick(n_unit, (1024, 1408, 512, 256, 128))
    tk = k_unit if k_unit <= MM_FULL_K else _pick(k_unit, (2048, 1408, 1024, 512, 256, 128))
    grid = (M // tm, N // tn, K // tk)
    if ta:
        a_spec = pl.BlockSpec((tk, tm), lambda i, j, k: (k, i))
    elif a_parts > 1:
        nkp = k_unit // tk
        a_spec = pl.BlockSpec((None, tm, tk), lambda i, j, k: (k // nkp, i, k % nkp))
    else:
        a_spec = pl.BlockSpec((tm, tk), lambda i, j, k: (i, k))
    if tb:
        b_spec = pl.BlockSpec((tn, tk), lambda i, j, k: (j, k))
    elif b_parts > 1:
        njp = (N // b_parts) // tn
        b_spec = pl.BlockSpec((None, tk, tn), lambda i, j, k: (j // njp, k, j % njp))
    else:
        b_spec = pl.BlockSpec((tk, tn), lambda i, j, k: (k, j))
    dims = (((0 if ta else 1,), (1 if tb else 0,)), ((), ()))
    epi_specs = [pl.BlockSpec((1, tn), lambda i, j, k: (0, j)) if e.shape[0] == 1
                 else pl.BlockSpec((tm, tn), lambda i, j, k: (i, j)) for e in epi_in]
    if pieces is None:
        o_spec = pl.BlockSpec((tm, tn), lambda i, j, k: (i, j))
        out_shape = jax.ShapeDtypeStruct((M, N), out_dtype)
    else:
        mi, nj = m_unit // tm, n_unit // tn
        if pieces == 'col':
            o_spec = pl.BlockSpec((None, None, tm, tn), lambda i, j, k: (j // nj, i // mi, i % mi, j % nj))
        else:
            o_spec = pl.BlockSpec((None, None, tm, tn), lambda i, j, k: (i // mi, j // nj, i % mi, j % nj))
        out_shape = jax.ShapeDtypeStruct((N_CHIPS, 2, m_unit, n_unit), out_dtype)
    return _gmm(a, b, grid=grid, a_spec=a_spec, b_spec=b_spec, o_spec=o_spec, out_shape=out_shape, dims=dims, kax=2,
                acc_shape=(tm, tn), name=name, epi=epi, epi_in=epi_in, epi_specs=epi_specs)


def _add_epi(r, res):
    return res + r


def _rmsnorm_fwd(x, g, out_dtype, name):
    S, D = x.shape
    tr = _pick(S, (256, 128, 8))

    def body(x_ref, g_ref, o_ref):
        xf = x_ref[...]
        r = lax.rsqrt(jnp.mean(xf * xf, axis=-1, keepdims=True) + RMS_EPS)
        o_ref[...] = ((xf * r) * g_ref[...]).astype(o_ref.dtype)

    return pl.pallas_call(
        body, name=name, grid=(S // tr,),
        in_specs=[pl.BlockSpec((tr, D), lambda i: (i, 0)), pl.BlockSpec((1, D), lambda i: (0, 0))],
        out_specs=pl.BlockSpec((tr, D), lambda i: (i, 0)),
        out_shape=jax.ShapeDtypeStruct((S, D), out_dtype), compiler_params=_cp(("parallel",)),
    )(x, g)


def _rms_bwd_math(xf, g, dy):
    r = lax.rsqrt(jnp.mean(xf * xf, axis=-1, keepdims=True) + RMS_EPS)
    xh = xf * r
    dxh = dy * g
    dx = r * (dxh - xh * jnp.mean(dxh * xh, axis=-1, keepdims=True))
    return dx, jnp.sum(dy * xh, axis=0, keepdims=True)


def _rmsnorm_bwd(x, g, dy, dres, name):
    S, D = x.shape
    tr = _pick(S, (256, 128, 8))
    dys = list(dy) if isinstance(dy, (list, tuple)) else [dy]
    n_dy = len(dys)

    def body(x_ref, g_ref, *refs):
        dy_refs = refs[:n_dy]
        dres_ref, dx_ref, dxb_ref, dg_ref = refs[n_dy:]

        @pl.when(pl.program_id(0) == 0)
        def _():
            dg_ref[...] = jnp.zeros_like(dg_ref)

        dy_sum = dy_refs[0][...].astype(F32)
        for r in dy_refs[1:]:
            dy_sum = dy_sum + r[...].astype(F32)
        dx, dg = _rms_bwd_math(x_ref[...], g_ref[...], dy_sum)
        tot = dres_ref[...] + dx
        dx_ref[...] = tot
        dxb_ref[...] = tot.astype(BF16)
        dg_ref[...] += dg

    row = pl.BlockSpec((tr, D), lambda i: (i, 0))
    vec = pl.BlockSpec((1, D), lambda i: (0, 0))
    dx, dxb, dg = pl.pallas_call(
        body, name=name, grid=(S // tr,), in_specs=[row, vec] + [row] * (n_dy + 1), out_specs=[row, row, vec],
        out_shape=[jax.ShapeDtypeStruct((S, D), F32), jax.ShapeDtypeStruct((S, D), BF16),
                   jax.ShapeDtypeStruct((1, D), F32)],
        compiler_params=_cp(("arbitrary",)),
    )(x, g, *dys, dres)
    return (dx, dxb), dg


def _loss_head(x, g, target, name):
    S, D = x.shape
    tr = _pick(S, (256, 128, 8))

    def body(x_ref, g_ref, t_ref, loss_ref, dx_ref, dxb_ref, dg_ref):
        @pl.when(pl.program_id(0) == 0)
        def _():
            dg_ref[...] = jnp.zeros_like(dg_ref)
            loss_ref[...] = jnp.zeros_like(loss_ref)

        xf = x_ref[...]
        gg = g_ref[...]
        r = lax.rsqrt(jnp.mean(xf * xf, axis=-1, keepdims=True) + RMS_EPS)
        e = (xf * r) * gg - t_ref[...]
        loss_ref[...] += 0.5 * jnp.sum(jnp.mean(e * e, axis=-1, keepdims=True), axis=0, keepdims=True)
        dx, dg = _rms_bwd_math(xf, gg, e * (1.0 / D))
        dx_ref[...] = dx
        dxb_ref[...] = dx.astype(BF16)
        dg_ref[...] += dg

    row = pl.BlockSpec((tr, D), lambda i: (i, 0))
    vec = pl.BlockSpec((1, D), lambda i: (0, 0))
    one = pl.BlockSpec((1, 1), lambda i: (0, 0))
    loss, dx, dxb, dg = pl.pallas_call(
        body, name=name, grid=(S // tr,), in_specs=[row, vec, row], out_specs=[one, row, row, vec],
        out_shape=[jax.ShapeDtypeStruct((1, 1), F32), jax.ShapeDtypeStruct((S, D), F32),
                   jax.ShapeDtypeStruct((S, D), BF16), jax.ShapeDtypeStruct((1, D), F32)],
        compiler_params=_cp(("arbitrary",)),
    )(x, g, target)
    return loss, (dx, dxb), dg


HALO = 16


def _shift_rows(main, prev_row, next_row):
    tr = main.shape[0]
    row = lax.broadcasted_iota(jnp.int32, main.shape, 0)
    up = jnp.where(row == 0, prev_row, pltpu.roll(main, 1, 0))
    dn = jnp.where(row == tr - 1, next_row, pltpu.roll(main, tr - 1, 0))
    return up, dn


def _halo_specs(tr, tn, S, col_of):
    hb = tr // HALO
    last = S // HALO - 1
    return [pl.BlockSpec((tr, tn), lambda j, i: (i, col_of(j))),
            pl.BlockSpec((HALO, tn), lambda j, i: (jnp.maximum(i * hb - 1, 0), col_of(j))),
            pl.BlockSpec((HALO, tn), lambda j, i: (jnp.minimum((i + 1) * hb, last), col_of(j)))]


def _halo_rows(main_ref, prev_ref, next_ref, i, n_i):
    main = main_ref[...].astype(F32)
    prev_row = prev_ref[HALO - 1:HALO, :].astype(F32) * (i > 0).astype(F32)
    next_row = next_ref[0:1, :].astype(F32) * (i < n_i - 1).astype(F32)
    up, dn = _shift_rows(main, prev_row, next_row)
    return up, main, dn


def _conv3(w_ref, b_ref, up, mid, dn):
    return b_ref[...] + w_ref[0:1, :] * up + w_ref[1:2, :] * mid + w_ref[2:3, :] * dn


def _ffn_tiles(S, F):
    return _pick(S, (512, 256, 128, 16)), _pick(F, (512, 256, 128))


def _conv_gate_fwd(u, conv_w, conv_b, name):
    S, F2 = u.shape
    F = F2 // 2
    tr, tn = _ffn_tiles(S, F)
    nj, ni = F // tn, S // tr

    def body(gm, gp, gn, vm, vp, vn, wg, wv, bg, bv, o_ref):
        i = pl.program_id(1)
        cg = _conv3(wg, bg, *_halo_rows(gm, gp, gn, i, ni))
        cv = _conv3(wv, bv, *_halo_rows(vm, vp, vn, i, ni))
        o_ref[...] = (cg * jax.nn.sigmoid(cg) * cv).astype(o_ref.dtype)

    wspec = lambda off: pl.BlockSpec((3, tn), lambda j, i: (0, j + off))
    bspec = lambda off: pl.BlockSpec((1, tn), lambda j, i: (0, j + off))
    return pl.pallas_call(
        body, name=name, grid=(nj, ni),
        in_specs=[*_halo_specs(tr, tn, S, lambda j: j), *_halo_specs(tr, tn, S, lambda j: j + nj),
                  wspec(0), wspec(nj), bspec(0), bspec(nj)],
        out_specs=pl.BlockSpec((tr, tn), lambda j, i: (i, j)),
        out_shape=jax.ShapeDtypeStruct((S, F), BF16), compiler_params=_cp(("parallel", "parallel")),
    )(u, u, u, u, u, u, conv_w, conv_w, conv_b, conv_b)


def _conv_gate_bwd(u, da, conv_w, conv_b, name):
    S, F2 = u.shape
    F = F2 // 2
    tr, tn = _ffn_tiles(S, F)
    nj, ni = F // tn, S // tr

    def body(gm, gp, gn, vm, vp, vn, wg, wv, bg, bv, da_ref, dc_ref, dwg_ref, dwv_ref, dbg_ref, dbv_ref):
        i = pl.program_id(1)

        @pl.when(i == 0)
        def _():
            for r in (dwg_ref, dwv_ref, dbg_ref, dbv_ref):
                r[...] = jnp.zeros_like(r)

        g_rows = _halo_rows(gm, gp, gn, i, ni)
        v_rows = _halo_rows(vm, vp, vn, i, ni)
        cg = _conv3(wg, bg, *g_rows)
        cv = _conv3(wv, bv, *v_rows)
        sg = jax.nn.sigmoid(cg)
        d = da_ref[...].astype(F32)
        dcv = d * (cg * sg)
        dcg = d * cv * (sg * (1.0 + cg * (1.0 - sg)))
        dc_ref[0] = dcg.astype(dc_ref.dtype)
        dc_ref[1] = dcv.astype(dc_ref.dtype)
        for t in range(3):
            dwg_ref[t:t + 1, :] += jnp.sum(dcg * g_rows[t], axis=0, keepdims=True)
            dwv_ref[t:t + 1, :] += jnp.sum(dcv * v_rows[t], axis=0, keepdims=True)
        dbg_ref[...] += jnp.sum(dcg, axis=0, keepdims=True)
        dbv_ref[...] += jnp.sum(dcv, axis=0, keepdims=True)

    wspec = lambda off: pl.BlockSpec((3, tn), lambda j, i: (0, j + off))
    bspec = lambda off: pl.BlockSpec((1, tn), lambda j, i: (0, j + off))
    tile = pl.BlockSpec((tr, tn), lambda j, i: (i, j))
    outs = pl.pallas_call(
        body, name=name, grid=(nj, ni),
        in_specs=[*_halo_specs(tr, tn, S, lambda j: j), *_halo_specs(tr, tn, S, lambda j: j + nj),
                  wspec(0), wspec(nj), bspec(0), bspec(nj), tile],
        out_specs=[pl.BlockSpec((2, tr, tn), lambda j, i: (0, i, j)), wspec(0), wspec(0), bspec(0), bspec(0)],
        out_shape=[jax.ShapeDtypeStruct((2, S, F), BF16),
                   jax.ShapeDtypeStruct((3, F), F32), jax.ShapeDtypeStruct((3, F), F32),
                   jax.ShapeDtypeStruct((1, F), F32), jax.ShapeDtypeStruct((1, F), F32)],
        compiler_params=_cp(("parallel", "arbitrary")),
    )(u, u, u, u, u, u, conv_w, conv_w, conv_b, conv_b, da)
    dc, dwg, dwv, dbg, dbv = outs
    return dc, jnp.concatenate([dwg, dwv], axis=1), jnp.concatenate([dbg, dbv], axis=1)


def _conv_transpose(dc, w, name):
    _, S, F = dc.shape
    tr, tn = _ffn_tiles(S, F)
    nj, ni = F // tn, S // tr
    hb = tr // HALO
    last = S // HALO - 1

    def body(m, p, n, w_ref, o_ref):
        up, mid, dn = _halo_rows(m, p, n, pl.program_id(2), ni)
        o_ref[...] = (w_ref[0:1, :] * dn + w_ref[1:2, :] * mid + w_ref[2:3, :] * up).astype(o_ref.dtype)

    tile = pl.BlockSpec((None, tr, tn), lambda h, j, i: (h, i, j))
    return pl.pallas_call(
        body, name=name, grid=(2, nj, ni),
        in_specs=[tile,
                  pl.BlockSpec((None, HALO, tn), lambda h, j, i: (h, jnp.maximum(i * hb - 1, 0), j)),
                  pl.BlockSpec((None, HALO, tn), lambda h, j, i: (h, jnp.minimum((i + 1) * hb, last), j)),
                  pl.BlockSpec((3, tn), lambda h, j, i: (0, j + h * nj))],
        out_specs=tile, out_shape=jax.ShapeDtypeStruct((2, S, F), BF16),
        compiler_params=_cp(("parallel", "parallel", "parallel")),
    )(dc, dc, dc, w)


def _ffn_fwd(x, norm_g, w_up, conv_w, conv_b, w_down, tag):
    hn = _rmsnorm_fwd(x, norm_g, BF16, f"ffn_norm_{tag}")
    u = _mm(hn, w_up, out_dtype=BF16, name=f"ffn_up_{tag}")
    a = _conv_gate_fwd(u, conv_w, conv_b, f"ffn_gate_{tag}")
    x_new = _mm(a, w_down, name=f"ffn_down_{tag}", epi=_add_epi, epi_in=(x,))
    return x_new, (x, hn, u, a)


def _ffn_bwd(dres, saved, norm_g, w_up, conv_w, conv_b, w_down, tag):
    x, hn, u, a = saved
    dres, dres_b = dres
    da = _mm(dres_b, w_down, tb=True, out_dtype=BF16, name=f"ffn_da_{tag}")
    dw_down = _mm(a, dres_b, ta=True, name=f"ffn_dwdown_{tag}", pieces='row')
    dc, dconv_w, dconv_b = _conv_gate_bwd(u, da, conv_w, conv_b, f"ffn_gate_bwd_{tag}")
    du = _conv_transpose(dc, conv_w, f"ffn_convt_{tag}")
    dhn = _mm(du, w_up, tb=True, name=f"ffn_dhn_{tag}")
    dw_up = _mm(hn, du, ta=True, name=f"ffn_dwup_{tag}", pieces='col')
    dres, dg = _rmsnorm_bwd(x, norm_g, dhn, dres, f"ffn_norm_bwd_{tag}")
    return dres, (dg, dw_up, dconv_w, dconv_b, dw_down)


ATTN_KEYS = 3 * ATTN_BLOCK


def _attn_window(i, S):
    ks = pl.multiple_of(jnp.clip((i - 1) * ATTN_BLOCK, 0, S - ATTN_KEYS), ATTN_BLOCK)
    qpos = i * ATTN_BLOCK + lax.broadcasted_iota(jnp.int32, (ATTN_BLOCK, ATTN_KEYS), 0)
    kpos = ks + lax.broadcasted_iota(jnp.int32, (ATTN_BLOCK, ATTN_KEYS), 1)
    arel = jnp.abs(kpos - qpos)
    return ks, arel.astype(F32), arel <= ATTN_BLOCK


def _attn_probs(q, k, slope, sink, arel, valid):
    s = lax.dot_general(q, k, (((1,), (1,)), ((), ())), preferred_element_type=F32) * (ATTN_DH ** -0.5)
    s = jnp.where(valid, s - slope * arel, -jnp.inf)
    m = jnp.maximum(jnp.max(s, axis=-1, keepdims=True), sink)
    p = jnp.exp(s - m)
    es = jnp.exp(sink - m)
    inv = 1.0 / (jnp.sum(p, axis=-1, keepdims=True) + es)
    return p * inv, es * inv


def _attn_specs(S, D):
    H = D // ATTN_DH
    KVW = (H // ATTN_GROUP) * ATTN_DH
    q_spec = pl.BlockSpec((ATTN_BLOCK, D), lambda i: (i, 0))
    k_spec = pl.BlockSpec((S, KVW), lambda i: (0, D // KVW))
    v_spec = pl.BlockSpec((S, KVW), lambda i: (0, D // KVW + 1))
    return H, KVW, q_spec, k_spec, v_spec


def _attn_fwd(qkv, sink, name):
    S = qkv.shape[0]
    D = qkv.shape[1] * ATTN_GROUP // (ATTN_GROUP + 2)
    H, KVW, q_spec, k_spec, v_spec = _attn_specs(S, D)

    def body(q_ref, k_ref, v_ref, sink_ref, o_ref):
        ks, arel, valid = _attn_window(pl.program_id(0), S)
        for kvh in range(H // ATTN_GROUP):
            cols = slice(kvh * ATTN_DH, (kvh + 1) * ATTN_DH)
            k = k_ref[pl.ds(ks, ATTN_KEYS), cols]
            v = v_ref[pl.ds(ks, ATTN_KEYS), cols]
            for g in range(ATTN_GROUP):
                h = kvh * ATTN_GROUP + g
                hc = slice(h * ATTN_DH, (h + 1) * ATTN_DH)
                p, _ = _attn_probs(q_ref[:, hc], k, 2.0 ** (-8.0 * (h + 1) / H), sink_ref[h], arel, valid)
                o_ref[:, hc] = jnp.dot(p.astype(BF16), v, preferred_element_type=F32).astype(o_ref.dtype)

    return pl.pallas_call(
        body, name=name, grid=(S // ATTN_BLOCK,),
        in_specs=[q_spec, k_spec, v_spec, pl.BlockSpec(memory_space=pltpu.SMEM)],
        out_specs=q_spec, out_shape=jax.ShapeDtypeStruct((S, D), BF16),
        compiler_params=_cp(("parallel",)),
    )(qkv, qkv, qkv, sink)


def _attn_bwd(qkv, sink, do, name):
    S = qkv.shape[0]
    D = qkv.shape[1] * ATTN_GROUP // (ATTN_GROUP + 2)
    H, KVW, q_spec, k_spec, v_spec = _attn_specs(S, D)
    scale = ATTN_DH ** -0.5

    def body(q_ref, k_ref, v_ref, sink_ref, do_ref, dq_ref, dk_ref, dv_ref, ds_ref):
        @pl.when(pl.program_id(0) == 0)
        def _():
            dk_ref[...] = jnp.zeros_like(dk_ref)
            dv_ref[...] = jnp.zeros_like(dv_ref)
            ds_ref[...] = jnp.zeros_like(ds_ref)

        ks, arel, valid = _attn_window(pl.program_id(0), S)
        rows = pl.ds(ks, ATTN_KEYS)
        for kvh in range(H // ATTN_GROUP):
            cols = slice(kvh * ATTN_DH, (kvh + 1) * ATTN_DH)
            k = k_ref[rows, cols]
            v = v_ref[rows, cols]
            dk = jnp.zeros((ATTN_KEYS, ATTN_DH), F32)
            dv = jnp.zeros((ATTN_KEYS, ATTN_DH), F32)
            for g in range(ATTN_GROUP):
                h = kvh * ATTN_GROUP + g
                hc = slice(h * ATTN_DH, (h + 1) * ATTN_DH)
                q = q_ref[:, hc]
                d_o = do_ref[:, hc]
                p, p_sink = _attn_probs(q, k, 2.0 ** (-8.0 * (h + 1) / H), sink_ref[h], arel, valid)
                dp = lax.dot_general(d_o, v, (((1,), (1,)), ((), ())), preferred_element_type=F32)
                delta = jnp.sum(p * dp, axis=-1, keepdims=True)
                dsc = (p * (dp - delta)).astype(BF16)
                ds_ref[:, h:h + 1] += -p_sink * delta
                dq_ref[:, hc] = (jnp.dot(dsc, k, preferred_element_type=F32) * scale).astype(dq_ref.dtype)
                dk += lax.dot_general(dsc, q, (((0,), (0,)), ((), ())), preferred_element_type=F32)
                dv += lax.dot_general(p.astype(BF16), d_o, (((0,), (0,)), ((), ())), preferred_element_type=F32)
            dk_ref[rows, cols] += dk * scale
            dv_ref[rows, cols] += dv

    kv_out = pl.BlockSpec((S, KVW), lambda i: (0, 0))
    return pl.pallas_call(
        body, name=name, grid=(S // ATTN_BLOCK,),
        in_specs=[q_spec, k_spec, v_spec, pl.BlockSpec(memory_space=pltpu.SMEM), q_spec],
        out_specs=[q_spec, kv_out, kv_out, pl.BlockSpec((ATTN_BLOCK, H), lambda i: (0, 0))],
        out_shape=[jax.ShapeDtypeStruct((S, D), BF16), jax.ShapeDtypeStruct((S, KVW), F32),
                   jax.ShapeDtypeStruct((S, KVW), F32), jax.ShapeDtypeStruct((ATTN_BLOCK, H), F32)],
        compiler_params=_cp(("arbitrary",)),
    )(qkv, qkv, qkv, sink, do)


def _attn_layer_fwd(x, norm_g, w_qkv, w_o, sink, tag):
    hn = _rmsnorm_fwd(x, norm_g, BF16, f"attn_norm_{tag}")
    qkv = _mm(hn, w_qkv, out_dtype=BF16, name=f"attn_qkv_{tag}")
    o = _attn_fwd(qkv, sink, f"attn_core_{tag}")
    x_new = _mm(o, w_o, name=f"attn_out_{tag}", epi=_add_epi, epi_in=(x,))
    return x_new, (x, hn, qkv, o)


def _attn_layer_bwd(dres, saved, norm_g, w_qkv, w_o, sink, tag):
    x, hn, qkv, o = saved
    dres, dres_b = dres
    do = _mm(dres_b, w_o, tb=True, out_dtype=BF16, name=f"attn_do_{tag}")
    dw_o = _mm(o, dres_b, ta=True, name=f"attn_dwo_{tag}", pieces='row')
    dq, dk, dv, dsink = _attn_bwd(qkv, sink, do, f"attn_core_bwd_{tag}")
    dqkv = jnp.concatenate([dq, dk.astype(BF16), dv.astype(BF16)], axis=1)
    dhn = _mm(dqkv, w_qkv, tb=True, name=f"attn_dhn_{tag}")
    dw_qkv = _mm(hn, dqkv, ta=True, name=f"attn_dwqkv_{tag}", pieces='col')
    dres, dg = _rmsnorm_bwd(x, norm_g, dhn, dres, f"attn_norm_bwd_{tag}")
    return dres, (dg, dw_qkv, dw_o, jnp.sum(dsink, axis=0))


MLA_W = 2 * LANES
MLA_DPAD = 2 * MLA_LORA + LANES
MLA_SCALE = (MLA_NOPE + MLA_ROPE) ** -0.5
MLA_TILES = (1024, 512, 256, 128)
LOG2E = math.log2(math.e)
LN2 = math.log(2.0)


def _rope_tables(S):
    half = MLA_ROPE // 2
    pos = jnp.arange(S, dtype=F32)
    inv = ROPE_THETA ** (-jnp.arange(half, dtype=F32) / half)
    ang = pos[:, None] * inv[None, :]
    cos, sin = jnp.cos(ang), jnp.sin(ang)
    z = jnp.zeros((S, LANES - 2 * half), F32)
    zh = jnp.zeros((S, half), F32)
    return (jnp.concatenate([cos, cos, z], axis=1), jnp.concatenate([-sin, zh, z], axis=1),
            jnp.concatenate([zh, sin, z], axis=1))


def _rope(t, ca, sb, sc):
    return t * ca + pltpu.roll(t, 96, 1) * sb + pltpu.roll(t, 32, 1) * sc


def _rope_t(d, ca, sb, sc):
    return d * ca + pltpu.roll(d * sb, 32, 1) + pltpu.roll(d * sc, 96, 1)


def _rms(xf, g):
    return (xf * lax.rsqrt(jnp.mean(xf * xf, axis=-1, keepdims=True) + RMS_EPS)) * g


def _mla_prep(d, qn, kvn, tabs, name):
    S = d.shape[0]
    tr = _pick(S, (256, 128, 8))
    L = MLA_LORA

    def body(d_ref, qn_ref, kvn_ref, ca, sb, sc, cq_ref, ckv_ref, kr_ref):
        cq_ref[...] = _rms(d_ref[:, :L], qn_ref[...]).astype(BF16)
        ckv_ref[...] = _rms(d_ref[:, L:2 * L], kvn_ref[...]).astype(BF16)
        kr_ref[...] = _rope(d_ref[:, 2 * L:], ca[...], sb[...], sc[...]).astype(BF16)

    row = lambda w: pl.BlockSpec((tr, w), lambda i: (i, 0))
    vec = pl.BlockSpec((1, L), lambda i: (0, 0))
    return pl.pallas_call(
        body, name=name, grid=(S // tr,),
        in_specs=[row(MLA_DPAD), vec, vec, row(LANES), row(LANES), row(LANES)],
        out_specs=[row(L), row(L), row(LANES)],
        out_shape=[jax.ShapeDtypeStruct((S, L), BF16), jax.ShapeDtypeStruct((S, L), BF16),
                   jax.ShapeDtypeStruct((S, LANES), BF16)],
        compiler_params=_cp(("parallel",)),
    )(d, qn, kvn, *tabs)


def _mla_prep_bwd(d, qn, kvn, tabs, dcq, dckv, dkr_h, name):
    S = d.shape[0]
    H = dkr_h.shape[0]
    tr = _pick(S, (256, 128, 8))
    L = MLA_LORA

    def body(d_ref, qn_ref, kvn_ref, ca, sb, sc, dcq_ref, dckv_ref, dkr_ref, dd_ref, dqn_ref, dkvn_ref):
        @pl.when(pl.program_id(0) == 0)
        def _():
            dqn_ref[...] = jnp.zeros_like(dqn_ref)
            dkvn_ref[...] = jnp.zeros_like(dkvn_ref)

        dx, dg = _rms_bwd_math(d_ref[:, :L], qn_ref[...], dcq_ref[...])
        dd_ref[:, :L] = dx.astype(BF16)
        dqn_ref[...] += dg
        dx, dg = _rms_bwd_math(d_ref[:, L:2 * L], kvn_ref[...], dckv_ref[...])
        dd_ref[:, L:2 * L] = dx.astype(BF16)
        dkvn_ref[...] += dg
        dkr = dkr_ref[0]
        for h in range(1, H):
            dkr = dkr + dkr_ref[h]
        dd_ref[:, 2 * L:] = _rope_t(dkr, ca[...], sb[...], sc[...]).astype(BF16)

    row = lambda w: pl.BlockSpec((tr, w), lambda i: (i, 0))
    vec = pl.BlockSpec((1, L), lambda i: (0, 0))
    return pl.pallas_call(
        body, name=name, grid=(S // tr,),
        in_specs=[row(MLA_DPAD), vec, vec, row(LANES), row(LANES), row(LANES), row(L), row(L),
                  pl.BlockSpec((H, tr, LANES), lambda i: (0, i, 0))],
        out_specs=[row(MLA_DPAD), vec, vec],
        out_shape=[jax.ShapeDtypeStruct((S, MLA_DPAD), BF16), jax.ShapeDtypeStruct((1, L), F32),
                   jax.ShapeDtypeStruct((1, L), F32)],
        compiler_params=_cp(("arbitrary",)),
    )(d, qn, kvn, *tabs, dcq, dckv, dkr_h)


def _heads_proj(a, w, out_dtype, name):
    S, K = a.shape
    H, _, n = w.shape
    tm = _pick(S, (1024, 512, 256, 128))
    return _gmm(a, w, grid=(S // tm, H, 1),
                a_spec=pl.BlockSpec((tm, K), lambda m, h, k: (m, 0)),
                b_spec=pl.BlockSpec((None, K, n), lambda m, h, k: (h, 0, 0)),
                o_spec=pl.BlockSpec((None, tm, n), lambda m, h, k: (h, m, 0)),
                out_shape=jax.ShapeDtypeStruct((H, S, n), out_dtype),
                dims=(((1,), (0,)), ((), ())), kax=2, acc_shape=(tm, n), name=name)


def _heads_proj_dx(dy, w, name):
    H, S, n = dy.shape
    K = w.shape[1]
    tm = _pick(S, (1024, 512, 256, 128))
    return _gmm(dy, w, grid=(S // tm, 1, H),
                a_spec=pl.BlockSpec((None, tm, n), lambda m, j, h: (h, m, 0)),
                b_spec=pl.BlockSpec((None, K, n), lambda m, j, h: (h, 0, 0)),
                o_spec=pl.BlockSpec((tm, K), lambda m, j, h: (m, 0)),
                out_shape=jax.ShapeDtypeStruct((S, K), F32),
                dims=(((1,), (1,)), ((), ())), kax=2, acc_shape=(tm, K), name=name)


def _heads_proj_dw(a, dy, name):
    S, K = a.shape
    H, _, n = dy.shape
    tk = _pick(S, (512, 256, 128))
    return _gmm(a, dy, grid=(H, 1, S // tk),
                a_spec=pl.BlockSpec((tk, K), lambda h, j, k: (k, 0)),
                b_spec=pl.BlockSpec((None, tk, n), lambda h, j, k: (h, k, 0)),
                o_spec=pl.BlockSpec((None, K, n), lambda h, j, k: (h, 0, 0)),
                out_shape=jax.ShapeDtypeStruct((H, K, n), F32),
                dims=(((0,), (0,)), ((), ())), kax=2, acc_shape=(K, n), name=name)


def _mla_rope_q(q_ext, tabs, bwd, name):
    H, S, _ = q_ext.shape
    tr = _pick(S, (512, 256, 128, 8))
    mult = 1.0 if bwd else MLA_SCALE * LOG2E

    def body(q_ref, ca, sb, sc, o_ref):
        o_ref[:, :LANES] = (q_ref[:, :LANES].astype(F32) * mult).astype(BF16)
        fn = _rope_t if bwd else _rope
        o_ref[:, LANES:] = (fn(q_ref[:, LANES:].astype(F32), ca[...], sb[...], sc[...]) * mult).astype(BF16)

    blk = pl.BlockSpec((None, tr, MLA_W), lambda i, h: (h, i, 0))
    tab = pl.BlockSpec((tr, LANES), lambda i, h: (i, 0))
    return pl.pallas_call(
        body, name=name, grid=(S // tr, H), in_specs=[blk, tab, tab, tab], out_specs=blk,
        out_shape=jax.ShapeDtypeStruct((H, S, MLA_W), BF16), compiler_params=_cp(("parallel", "parallel")),
    )(q_ext, *tabs)


def _col_to_row(col):
    n = col.shape[0]
    eye = lax.broadcasted_iota(jnp.int32, (n, n), 0) == lax.broadcasted_iota(jnp.int32, (n, n), 1)
    return jnp.sum(jnp.where(eye, col, 0.0), axis=0, keepdims=True)


def _mla_flash_fwd(q, kv, kr, name, tq=None, tk=None, unroll=1):
    H, S, _ = q.shape
    tq = tq or _pick(S, MLA_TILES)
    tk = tk or _pick(S, MLA_TILES)

    def body(q_ref, kv_ref, kr_ref, o_ref, lse_ref, kbuf, vbuf):
        @pl.when(pl.program_id(1) == 0)
        def _():
            kbuf[:, :LANES] = kv_ref[:, :LANES]
            kbuf[:, LANES:] = kr_ref[...]
            vbuf[:, :LANES] = kv_ref[:, LANES:]
            vbuf[:, LANES:] = jnp.ones((S, LANES), BF16)

        qv = q_ref[...]

        def step(c, carry):
            m, acc = carry
            rows = pl.ds(pl.multiple_of(c * tk, tk), tk)
            s = lax.dot_general(qv, kbuf[rows, :], (((1,), (1,)), ((), ())), preferred_element_type=F32)
            m_new = jnp.maximum(m, jnp.max(s, axis=-1, keepdims=True))
            p = jnp.exp2(s - m_new).astype(BF16)
            acc = jnp.exp2(m - m_new) * acc + jnp.dot(p, vbuf[rows, :], preferred_element_type=F32)
            return m_new, acc

        init = (jnp.full((tq, 1), -jnp.inf, F32), jnp.zeros((tq, MLA_W), F32))
        m, acc = lax.fori_loop(0, S // tk, step, init, unroll=unroll)
        l = acc[:, LANES:LANES + 1]
        o_ref[...] = (acc[:, :LANES] / l).astype(o_ref.dtype)
        lse_ref[...] = _col_to_row(m + jnp.log2(l))

    return pl.pallas_call(
        body, name=name, grid=(H, S // tq),
        in_specs=[pl.BlockSpec((None, tq, MLA_W), lambda h, i: (h, i, 0)),
                  pl.BlockSpec((None, S, MLA_W), lambda h, i: (h, 0, 0)),
                  pl.BlockSpec((S, LANES), lambda h, i: (0, 0))],
        out_specs=[pl.BlockSpec((tq, MLA_V), lambda h, i: (i, h)),
                   pl.BlockSpec((None, 1, tq), lambda h, i: (h, 0, i))],
        out_shape=[jax.ShapeDtypeStruct((S, H * MLA_V), BF16), jax.ShapeDtypeStruct((H, 1, S), F32)],
        scratch_shapes=[pltpu.VMEM((S, MLA_W), BF16), pltpu.VMEM((S, MLA_W), BF16)],
        compiler_params=_cp(("parallel", "arbitrary")),
    )(q, kv, kr)


def _mla_delta(o, do, H, name):
    S = o.shape[0]
    tq = _pick(S, (512, 256, 128))

    def body(o_ref, do_ref, d_ref):
        prod = o_ref[...].astype(F32) * do_ref[...].astype(F32)
        d_ref[...] = _col_to_row(jnp.sum(prod, axis=-1, keepdims=True))

    blk = pl.BlockSpec((tq, MLA_V), lambda i, h: (i, h))
    return pl.pallas_call(
        body, name=name, grid=(S // tq, H), in_specs=[blk, blk],
        out_specs=pl.BlockSpec((None, 1, tq), lambda i, h: (h, 0, i)),
        out_shape=jax.ShapeDtypeStruct((H, 1, S), F32), compiler_params=_cp(("parallel", "parallel")),
    )(o, do)


def _mla_flash_bwd(q, kv, kr, do, lse, delta, name, tq=None, tkv=None, unroll=1):
    H, S, _ = q.shape
    tq = tq or _pick(S, MLA_TILES)
    tkv = tkv or _pick(S, MLA_TILES)

    def body(q_ref, kv_ref, kr_ref, do_ref, lse_ref, dl_ref, dq_ref, dkv_ref, dkr_ref):
        @pl.when(pl.program_id(1) == 0)
        def _():
            dq_ref[...] = jnp.zeros_like(dq_ref)

        v = kv_ref[:, LANES:]
        k = jnp.concatenate([kv_ref[:, :LANES], kr_ref[...]], axis=1)

        def step(c, carry):
            dk, dv = carry
            start = pl.multiple_of(c * tq, tq)
            rows = pl.ds(start, tq)
            qv = q_ref[rows, :]
            d_o = do_ref[rows, :]
            s_t = lax.dot_general(k, qv, (((1,), (1,)), ((), ())), preferred_element_type=F32)
            p_t = jnp.exp2(s_t - lse_ref[:, rows])
            dv = dv + jnp.dot(p_t.astype(BF16), d_o, preferred_element_type=F32)
            dp_t = lax.dot_general(v, d_o, (((1,), (1,)), ((), ())), preferred_element_type=F32)
            ds_t = (p_t * (dp_t - dl_ref[:, rows])).astype(BF16)
            dk = dk + jnp.dot(ds_t, qv, preferred_element_type=F32)
            dq_ref[rows, :] += lax.dot_general(ds_t, k, (((0,), (0,)), ((), ())),
                                               preferred_element_type=F32) * MLA_SCALE
            return dk, dv

        dk, dv = lax.fori_loop(0, S // tq, step, (jnp.zeros((tkv, MLA_W), F32), jnp.zeros((tkv, MLA_V), F32)),
                               unroll=unroll)
        dkv_ref[:, :LANES] = (dk[:, :LANES] * LN2).astype(BF16)
        dkv_ref[:, LANES:] = dv.astype(BF16)
        dkr_ref[...] = dk[:, LANES:] * LN2

    stat = pl.BlockSpec((None, 1, S), lambda h, j: (h, 0, 0))
    return pl.pallas_call(
        body, name=name, grid=(H, S // tkv),
        in_specs=[pl.BlockSpec((None, S, MLA_W), lambda h, j: (h, 0, 0)),
                  pl.BlockSpec((None, tkv, MLA_W), lambda h, j: (h, j, 0)),
                  pl.BlockSpec((tkv, LANES), lambda h, j: (j, 0)),
                  pl.BlockSpec((S, MLA_V), lambda h, j: (0, h)), stat, stat],
        out_specs=[pl.BlockSpec((None, S, MLA_W), lambda h, j: (h, 0, 0)),
                   pl.BlockSpec((None, tkv, MLA_W), lambda h, j: (h, j, 0)),
                   pl.BlockSpec((None, tkv, LANES), lambda h, j: (h, j, 0))],
        out_shape=[jax.ShapeDtypeStruct((H, S, MLA_W), F32), jax.ShapeDtypeStruct((H, S, MLA_W), BF16),
                   jax.ShapeDtypeStruct((H, S, LANES), F32)],
        compiler_params=_cp(("parallel", "arbitrary")),
    )(q, kv, kr, do, lse, delta)


def _mla_weights(w_dqkv, w_uq, w_ukv):
    H = MLA_HEADS
    wd = jnp.pad(w_dqkv, ((0, 0), (0, MLA_DPAD - w_dqkv.shape[1])))
    wq = w_uq.reshape(MLA_LORA, H, MLA_NOPE + MLA_ROPE)
    wq = jnp.pad(wq, ((0, 0), (0, 0), (0, MLA_W - wq.shape[2]))).transpose(1, 0, 2)
    wkv = w_ukv.reshape(MLA_LORA, H, MLA_NOPE + MLA_V).transpose(1, 0, 2)
    return wd, wq, wkv


def _mla_layer_fwd(x, norm_g, wd, wq, wkv, w_o, qn, kvn, tag):
    S = x.shape[0]
    tabs = _rope_tables(S)
    hn = _rmsnorm_fwd(x, norm_g, BF16, f"mla_norm_{tag}")
    d = _mm(hn, wd, name=f"mla_down_{tag}")
    cq, ckv, kr = _mla_prep(d, qn, kvn, tabs, f"mla_prep_{tag}")
    q = _mla_rope_q(_heads_proj(cq, wq, F32, f"mla_uq_{tag}"), tabs, False, f"mla_ropeq_{tag}")
    kv = _heads_proj(ckv, wkv, BF16, f"mla_ukv_{tag}")
    o, lse = _mla_flash_fwd(q, kv, kr, f"mla_flash_{tag}")
    x_new = _mm(o, w_o, name=f"mla_out_{tag}", epi=_add_epi, epi_in=(x,))
    return x_new, (x, hn, d, cq, ckv, kr, q, kv, o, lse)


def _mla_layer_bwd(dres, saved, norm_g, wd, wq, wkv, w_o, qn, kvn, tag):
    x, hn, d, cq, ckv, kr, q, kv, o, lse = saved
    S = x.shape[0]
    H = MLA_HEADS
    tabs = _rope_tables(S)
    dres, dres_b = dres
    do = _mm(dres_b, w_o, tb=True, out_dtype=BF16, name=f"mla_do_{tag}")
    dw_o = _mm(o, dres_b, ta=True, name=f"mla_dwo_{tag}", pieces='row')
    delta = _mla_delta(o, do, H, f"mla_delta_{tag}")
    dq, dkv, dkr_h = _mla_flash_bwd(q, kv, kr, do, lse, delta, f"mla_flash_bwd_{tag}")
    dq_ext = _mla_rope_q(dq, tabs, True, f"mla_ropeq_bwd_{tag}")
    dwq = _heads_proj_dw(cq, dq_ext, f"mla_dwuq_{tag}")
    dcq = _heads_proj_dx(dq_ext, wq, f"mla_dcq_{tag}")
    dwkv = _heads_proj_dw(ckv, dkv, f"mla_dwukv_{tag}")
    dckv = _heads_proj_dx(dkv, wkv, f"mla_dckv_{tag}")
    dd, dqn, dkvn = _mla_prep_bwd(d, qn, kvn, tabs, dcq, dckv, dkr_h, f"mla_prep_bwd_{tag}")
    dhn = _mm(dd, wd, tb=True, name=f"mla_dhn_{tag}")
    dwd = _mm(hn, dd, ta=True, name=f"mla_dwd_{tag}")
    dres, dg = _rmsnorm_bwd(x, norm_g, dhn, dres, f"mla_norm_bwd_{tag}")
    dw_dqkv = dwd[:, :2 * MLA_LORA + MLA_ROPE]
    dw_uq = dwq.transpose(1, 0, 2)[:, :, :MLA_NOPE + MLA_ROPE].reshape(MLA_LORA, -1)
    dw_ukv = dwkv.transpose(1, 0, 2).reshape(MLA_LORA, -1)
    return dres, (dg, dw_dqkv, dqn, dkvn, dw_uq, dw_ukv, dw_o)


S5_CB = LANES
S5_SB = (S5_CB // SSM_GROUP_CH) * SSM_STATE
S5_ROWS = 1024


def _s5_disc(a_re, a_im, ls, b_re, b_im):
    step = jnp.exp(ls)
    mag = jnp.exp(step * a_re)
    lb_re = mag * jnp.cos(step * a_im)
    lb_im = mag * jnp.sin(step * a_im)
    n_re, n_im = lb_re - 1.0, lb_im
    den = a_re * a_re + a_im * a_im
    coef_re = (n_re * a_re + n_im * a_im) / den
    coef_im = (n_im * a_re - n_re * a_im) / den
    return lb_re, lb_im, coef_re * b_re - coef_im * b_im, coef_re * b_im + coef_im * b_re


def _s5_disc_fwd(a_re, a_im, ls, b_re, b_im, name):
    GN = a_re.shape[-1]

    def body(ar, ai, l, br, bi, o_lr, o_li, o_br, o_bi):
        for o, v in zip((o_lr, o_li, o_br, o_bi), _s5_disc(ar[...], ai[...], l[...], br[...], bi[...])):
            o[...] = v

    vec = pl.BlockSpec((None, 1, GN), lambda d: (d, 0, 0))
    mat = pl.BlockSpec((None, SSM_GROUP_CH, GN), lambda d: (d, 0, 0))
    sv = jax.ShapeDtypeStruct(a_re.shape, F32)
    sm = jax.ShapeDtypeStruct(b_re.shape, F32)
    return pl.pallas_call(body, name=name, grid=(2,), in_specs=[vec, vec, vec, mat, mat],
                          out_specs=[vec, vec, mat, mat], out_shape=[sv, sv, sm, sm],
                          compiler_params=_cp(("parallel",)))(a_re, a_im, ls, b_re, b_im)


def _s5_disc_bwd(a_re, a_im, ls, b_re, b_im, d_lr, d_li, d_br, d_bi, name):
    GN = a_re.shape[-1]

    def body(ar, ai, l, br, bi, g_lr, g_li, g_br, g_bi, o_ar, o_ai, o_l, o_br, o_bi):
        _, vjp = jax.vjp(_s5_disc, ar[...], ai[...], l[...], br[...], bi[...])
        for o, v in zip((o_ar, o_ai, o_l, o_br, o_bi), vjp((g_lr[...], g_li[...], g_br[...], g_bi[...]))):
            o[...] = v

    vec = pl.BlockSpec((None, 1, GN), lambda d: (d, 0, 0))
    mat = pl.BlockSpec((None, SSM_GROUP_CH, GN), lambda d: (d, 0, 0))
    sv = jax.ShapeDtypeStruct(a_re.shape, F32)
    sm = jax.ShapeDtypeStruct(b_re.shape, F32)
    return pl.pallas_call(body, name=name, grid=(2,), in_specs=[vec, vec, vec, mat, mat, vec, vec, mat, mat],
                          out_specs=[vec, vec, vec, mat, mat], out_shape=[sv, sv, sv, sm, sm],
                          compiler_params=_cp(("parallel",)))(a_re, a_im, ls, b_re, b_im, d_lr, d_li, d_br, d_bi)


def _cmul(ar, ai, br, bi):
    return ar * br - ai * bi, ar * bi + ai * br


def _segment_carries(lr, li, er, ei, n_steps, reverse):
    pr, pi = lr, li
    for _ in range(int(math.log2(n_steps))):
        pr, pi = _cmul(pr, pi, pr, pi)
    row = lax.broadcasted_iota(jnp.int32, er.shape, 0)
    edge = (SSM_SEGMENTS - 1) if reverse else 0
    shift = (SSM_SEGMENTS - 1) if reverse else 1
    cr = jnp.zeros_like(er)
    ci = jnp.zeros_like(ei)
    for _ in range(SSM_SEGMENTS - 1):
        tr_, ti_ = _cmul(pr, pi, cr, ci)
        cr = jnp.where(row == edge, 0.0, pltpu.roll(tr_ + er, shift, 0))
        ci = jnp.where(row == edge, 0.0, pltpu.roll(ti_ + ei, shift, 0))
    return cr, ci


def _s5_geometry(S, D):
    assert S % SSM_SEGMENTS == 0 and D % S5_CB == 0
    n_steps = S // SSM_SEGMENTS
    assert n_steps & (n_steps - 1) == 0, "segment length must be a power of two"
    rows = min(S5_ROWS, S)
    return n_steps, rows, S // rows, D // S5_CB


def _s5_scan(u, b_re, b_im, c_re, c_im, lam_re, lam_im, ends, descending, name):
    S, D = u.shape
    n_steps, rows, nch, ncb = _s5_geometry(S, D)
    full = ends is not None
    GN = ncb * S5_SB

    def body(*refs):
        if full:
            (u_ref, br_ref, bi_ref, cr_ref, ci_ref, lr_ref, li_ref, er_ref, ei_ref,
             xr_ref, xi_ref, y_ref, st_r, st_i, buf_r, buf_i) = refs
        else:
            u_ref, br_ref, bi_ref, lr_ref, li_ref, er_ref, ei_ref, st_r, st_i, buf_r, buf_i = refs
        lr = jnp.broadcast_to(lr_ref[...], (SSM_SEGMENTS, S5_SB))
        li = jnp.broadcast_to(li_ref[...], (SSM_SEGMENTS, S5_SB))

        @pl.when(pl.program_id(1) == 0)
        def _():
            if full:
                st_r[...], st_i[...] = _segment_carries(lr, li, er_ref[...], ei_ref[...], n_steps, descending)
            else:
                st_r[...] = jnp.zeros_like(st_r)
                st_i[...] = jnp.zeros_like(st_i)

        ub = u_ref[...].astype(BF16)
        buf_r[...] = jnp.dot(ub, br_ref[...], preferred_element_type=F32)
        buf_i[...] = jnp.dot(ub, bi_ref[...], preferred_element_type=F32)

        n_it = rows // SSM_SEGMENTS

        def step(i, carry):
            sr, si = carry
            i = n_it - 1 - i if descending else i
            r = pl.ds(pl.multiple_of(i * SSM_SEGMENTS, SSM_SEGMENTS), SSM_SEGMENTS)
            if full:
                xr_ref[r, :] = sr
                xi_ref[r, :] = si
            nr = lr * sr - li * si + buf_r[r, :]
            ni = lr * si + li * sr + buf_i[r, :]
            if full:
                buf_r[r, :] = nr
                buf_i[r, :] = ni
            return nr, ni

        sr, si = lax.fori_loop(0, n_it, step, (st_r[...], st_i[...]))
        st_r[...] = sr
        st_i[...] = si
        if full:
            y_ref[...] = (jnp.dot(buf_r[...].astype(BF16), cr_ref[...], preferred_element_type=F32)
                          - jnp.dot(buf_i[...].astype(BF16), ci_ref[...], preferred_element_type=F32))
        else:
            er_ref[...] = sr
            ei_ref[...] = si

    chunk = (lambda c: nch - 1 - c) if descending else (lambda c: c)
    u_spec = pl.BlockSpec((rows, S5_CB), lambda b, c: (chunk(c), b))
    bmat = pl.BlockSpec((None, S5_CB, S5_SB), lambda b, c: (b, 0, 0))
    cmat = pl.BlockSpec((None, S5_SB, S5_CB), lambda b, c: (b, 0, 0))
    lvec = pl.BlockSpec((1, S5_SB), lambda b, c: (0, b))
    evec = pl.BlockSpec((SSM_SEGMENTS, S5_SB), lambda b, c: (0, b))
    xblk = pl.BlockSpec((rows, S5_SB), lambda b, c: (chunk(c), b))
    scratch = [pltpu.VMEM((SSM_SEGMENTS, S5_SB), F32)] * 2 + [pltpu.VMEM((rows, S5_SB), F32)] * 2
    e_shape = jax.ShapeDtypeStruct((SSM_SEGMENTS, GN), F32)
    if full:
        x_shape = jax.ShapeDtypeStruct((S, GN), F32)
        return pl.pallas_call(
            body, name=name, grid=(ncb, nch),
            in_specs=[u_spec, bmat, bmat, cmat, cmat, lvec, lvec, evec, evec],
            out_specs=[xblk, xblk, u_spec], out_shape=[x_shape, x_shape, jax.ShapeDtypeStruct((S, D), F32)],
            scratch_shapes=scratch, compiler_params=_cp(("parallel", "arbitrary")),
        )(u, b_re, b_im, c_re, c_im, lam_re, lam_im, *ends)
    return pl.pallas_call(
        body, name=name, grid=(ncb, nch), in_specs=[u_spec, bmat, bmat, lvec, lvec],
        out_specs=[evec, evec], out_shape=[e_shape, e_shape],
        scratch_shapes=scratch, compiler_params=_cp(("parallel", "arbitrary")),
    )(u, b_re, b_im, lam_re, lam_im)


def _s5_scan_bwd(dy, u, xp, b_re, b_im, c_re, c_im, lam_re, lam_im, starts, descending, name):
    S, D = dy.shape
    n_steps, rows, nch, ncb = _s5_geometry(S, D)
    full = starts is not None
    GN = ncb * S5_SB
    nt = (((1,), (1,)), ((), ()))
    tn = (((0,), (0,)), ((), ()))

    def body(*refs):
        if full:
            (dy_ref, u_ref, xr_ref, xi_ref, br_ref, bi_ref, cr_ref, ci_ref, lr_ref, li_ref, gr_ref, gi_ref,
             du_ref, dbr_ref, dbi_ref, dcr_ref, dci_ref, dlr_ref, dli_ref, st_r, st_i, buf_r, buf_i) = refs
        else:
            dy_ref, cr_ref, ci_ref, lr_ref, li_ref, gr_ref, gi_ref, st_r, st_i, buf_r, buf_i = refs
        lr = jnp.broadcast_to(lr_ref[...], (SSM_SEGMENTS, S5_SB))
        li = jnp.broadcast_to(li_ref[...], (SSM_SEGMENTS, S5_SB))

        @pl.when(pl.program_id(1) == 0)
        def _():
            if full:
                st_r[...], st_i[...] = _segment_carries(lr, -li, gr_ref[...], gi_ref[...], n_steps, descending)
                for r in (dbr_ref, dbi_ref, dcr_ref, dci_ref, dlr_ref, dli_ref):
                    r[...] = jnp.zeros_like(r)
            else:
                st_r[...] = jnp.zeros_like(st_r)
                st_i[...] = jnp.zeros_like(st_i)

        dyb = dy_ref[...].astype(BF16)
        buf_r[...] = lax.dot_general(dyb, cr_ref[...], nt, preferred_element_type=F32)
        buf_i[...] = -lax.dot_general(dyb, ci_ref[...], nt, preferred_element_type=F32)
        n_it = rows // SSM_SEGMENTS

        def step(j, carry):
            gr, gi = carry
            j = n_it - 1 - j if descending else j
            r = pl.ds(pl.multiple_of(j * SSM_SEGMENTS, SSM_SEGMENTS), SSM_SEGMENTS)
            nr = lr * gr + li * gi + buf_r[r, :]
            ni = lr * gi - li * gr + buf_i[r, :]
            if full:
                buf_r[r, :] = nr
                buf_i[r, :] = ni
            return nr, ni

        gr, gi = lax.fori_loop(0, n_it, step, (st_r[...], st_i[...]))
        st_r[...] = gr
        st_i[...] = gi
        if not full:
            gr_ref[...] = gr
            gi_ref[...] = gi
            return
        g_r, g_i = buf_r[...], buf_i[...]
        xr, xi = xr_ref[...], xi_ref[...]
        dlr_ref[...] += jnp.sum(g_r * xr + g_i * xi, axis=0, keepdims=True)
        dli_ref[...] += jnp.sum(g_i * xr - g_r * xi, axis=0, keepdims=True)
        ub = u_ref[...].astype(BF16)
        gb_r, gb_i = g_r.astype(BF16), g_i.astype(BF16)
        du_ref[...] = (lax.dot_general(gb_r, br_ref[...], nt, preferred_element_type=F32)
                       + lax.dot_general(gb_i, bi_ref[...], nt, preferred_element_type=F32))
        dbr_ref[...] += lax.dot_general(ub, gb_r, tn, preferred_element_type=F32)
        dbi_ref[...] += lax.dot_general(ub, gb_i, tn, preferred_element_type=F32)
        lr_, li_ = lr_ref[...], li_ref[...]
        x_r = lr_ * xr - li_ * xi + jnp.dot(ub, br_ref[...], preferred_element_type=F32)
        x_i = lr_ * xi + li_ * xr + jnp.dot(ub, bi_ref[...], preferred_element_type=F32)
        dcr_ref[...] += lax.dot_general(x_r.astype(BF16), dyb, tn, preferred_element_type=F32)
        dci_ref[...] -= lax.dot_general(x_i.astype(BF16), dyb, tn, preferred_element_type=F32)

    rev = (lambda c: nch - 1 - c) if descending else (lambda c: c)
    u_spec = pl.BlockSpec((rows, S5_CB), lambda b, c: (rev(c), b))
    bmat = pl.BlockSpec((None, S5_CB, S5_SB), lambda b, c: (b, 0, 0))
    cmat = pl.BlockSpec((None, S5_SB, S5_CB), lambda b, c: (b, 0, 0))
    lvec = pl.BlockSpec((1, S5_SB), lambda b, c: (0, b))
    evec = pl.BlockSpec((SSM_SEGMENTS, S5_SB), lambda b, c: (0, b))
    xblk = pl.BlockSpec((rows, S5_SB), lambda b, c: (rev(c), b))
    scratch = [pltpu.VMEM((SSM_SEGMENTS, S5_SB), F32)] * 2 + [pltpu.VMEM((rows, S5_SB), F32)] * 2
    e_shape = jax.ShapeDtypeStruct((SSM_SEGMENTS, GN), F32)
    if full:
        return pl.pallas_call(
            body, name=name, grid=(ncb, nch),
            in_specs=[u_spec, u_spec, xblk, xblk, bmat, bmat, cmat, cmat, lvec, lvec, evec, evec],
            out_specs=[u_spec, bmat, bmat, cmat, cmat, lvec, lvec],
            out_shape=[jax.ShapeDtypeStruct((S, D), F32), jax.ShapeDtypeStruct(b_re.shape, F32),
                       jax.ShapeDtypeStruct(b_re.shape, F32), jax.ShapeDtypeStruct(c_re.shape, F32),
                       jax.ShapeDtypeStruct(c_re.shape, F32), jax.ShapeDtypeStruct((1, GN), F32),
                       jax.ShapeDtypeStruct((1, GN), F32)],
            scratch_shapes=scratch, compiler_params=_cp(("parallel", "arbitrary")),
        )(dy, u, *xp, b_re, b_im, c_re, c_im, lam_re, lam_im, *starts)
    return pl.pallas_call(
        body, name=name, grid=(ncb, nch), in_specs=[u_spec, cmat, cmat, lvec, lvec],
        out_specs=[evec, evec], out_shape=[e_shape, e_shape],
        scratch_shapes=scratch, compiler_params=_cp(("parallel", "arbitrary")),
    )(dy, c_re, c_im, lam_re, lam_im)


def _s5_perm(t):
    S, D = t.shape
    return t.reshape(SSM_SEGMENTS, S // SSM_SEGMENTS, D).transpose(1, 0, 2).reshape(S, D)


def _s5_unperm(t):
    S, D = t.shape
    return t.reshape(S // SSM_SEGMENTS, SSM_SEGMENTS, D).transpose(1, 0, 2).reshape(S, D)


def _s5_blockdiag_b(bb, ncb):
    gpb = S5_CB // SSM_GROUP_CH
    t = bb.reshape(SSM_GROUP_CH, ncb, gpb, SSM_STATE)
    return jnp.einsum('cbgn,gh->bgchn', t, jnp.eye(gpb, dtype=bb.dtype)).reshape(ncb, S5_CB, S5_SB)


def _s5_blockdiag_b_t(dblk):
    ncb = dblk.shape[0]
    gpb = S5_CB // SSM_GROUP_CH
    t = dblk.reshape(ncb, gpb, SSM_GROUP_CH, gpb, SSM_STATE)
    return jnp.einsum('bgchn,gh->cbgn', t, jnp.eye(gpb, dtype=dblk.dtype)).reshape(SSM_GROUP_CH, -1)


def _s5_blockdiag_c(c, ncb):
    gpb = S5_CB // SSM_GROUP_CH
    t = c.reshape(ncb, gpb, SSM_GROUP_CH, SSM_STATE)
    return jnp.einsum('bgcn,gh->bgnhc', t, jnp.eye(gpb, dtype=c.dtype)).reshape(ncb, S5_SB, S5_CB)


def _s5_blockdiag_c_t(dblk):
    ncb = dblk.shape[0]
    gpb = S5_CB // SSM_GROUP_CH
    t = dblk.reshape(ncb, gpb, SSM_STATE, gpb, SSM_GROUP_CH)
    return jnp.einsum('bgnhc,gh->bgcn', t, jnp.eye(gpb, dtype=dblk.dtype)).reshape(-1, SSM_GROUP_CH, SSM_STATE)


_GELU_C = math.sqrt(2.0 / math.pi)


def _gelu(y):
    return y * (0.5 * (1.0 + jnp.tanh(_GELU_C * (y + 0.044715 * (y * y * y)))))


def _gelu_grad(y):
    t = jnp.tanh(_GELU_C * (y + 0.044715 * (y * y * y)))
    return 0.5 * (1.0 + t) + 0.5 * y * (1.0 - t * t) * (_GELU_C * (1.0 + 3.0 * 0.044715 * y * y))


def _rowwise(fn, ins, outs, name, acc=()):
    S, D = next(a.shape for a in ins if a.shape[0] != 1)
    tr = _pick(S, (256, 128, 8))
    row = pl.BlockSpec((tr, D), lambda i: (i, 0))
    vec = pl.BlockSpec((1, D), lambda i: (0, 0))
    n_in = len(ins)

    def body(*refs):
        res = fn(*[r[...] for r in refs[:n_in]])
        for k, (o, v) in enumerate(zip(refs[n_in:], res)):
            if k in acc:
                @pl.when(pl.program_id(0) == 0)
                def _():
                    o[...] = jnp.zeros_like(o)
                o[...] += jnp.sum(v, axis=0, keepdims=True)
            else:
                o[...] = v.astype(o.dtype)

    return pl.pallas_call(
        body, name=name, grid=(S // tr,), in_specs=[vec if a.shape[0] == 1 else row for a in ins],
        out_specs=[vec if k in acc else row for k in range(len(outs))],
        out_shape=[jax.ShapeDtypeStruct((1, D) if k in acc else (S, D), dt) for k, dt in enumerate(outs)],
        compiler_params=_cp(("arbitrary",) if acc else ("parallel",)),
    )(*ins)


def _s5_params(p):
    G, N = p["a_re"].shape[1:]
    vec = lambda a: a.reshape(2, 1, G * N)
    ls = jnp.broadcast_to(p["log_step"][:, :, None], (2, G, N))
    bt = lambda b: b.transpose(0, 3, 1, 2).reshape(2, SSM_GROUP_CH, G * N)
    return vec(p["a_re"]), vec(p["a_im"]), vec(ls), bt(p["b_re"]), bt(p["b_im"])


def _s5_layer_fwd(x, norm_g, p, w_glu, tag):
    S, D = x.shape
    ncb = D // S5_CB
    xp = _s5_perm(x)
    hn = _rmsnorm_fwd(xp, norm_g, F32, f"s5_norm_{tag}")
    raw = _s5_params(p)
    lam_r, lam_i, bb_r, bb_i = _s5_disc_fwd(*raw, f"s5_disc_{tag}")
    dirs = []
    ys = []
    for dirn in range(2):
        mats = (_s5_blockdiag_b(bb_r[dirn], ncb).astype(BF16), _s5_blockdiag_b(bb_i[dirn], ncb).astype(BF16),
                _s5_blockdiag_c(p["c_re"][dirn], ncb).astype(BF16), _s5_blockdiag_c(p["c_im"][dirn], ncb).astype(BF16))
        lam = (lam_r[dirn], lam_i[dirn])
        ends = _s5_scan(hn, mats[0], mats[1], None, None, *lam, None, dirn == 1, f"s5_ends_{tag}_{dirn}")
        xr, xi, y = _s5_scan(hn, *mats, *lam, ends, dirn == 1, f"s5_scan_{tag}_{dirn}")
        dirs.append(((xr, xi), mats, lam))
        ys.append(y)
    ytot, z = _rowwise(lambda u, d, a, b: ((lambda y: (y, _gelu(y)))(d * u + a + b)),
                       [hn, p["d"], ys[0], ys[1]], [F32, BF16], f"s5_gelu_{tag}")
    t = _mm(z, w_glu, name=f"s5_glu_{tag}", epi=lambda r, b: r + b, epi_in=(p["b_glu"],))
    (x_new,) = _rowwise(lambda xx, zz, tt: (xx + zz.astype(F32) * jax.nn.sigmoid(tt),),
                        [xp, z, t], [F32], f"s5_out_{tag}")
    return _s5_unperm(x_new), (xp, hn, raw, dirs, ytot, z, t)


def _s5_layer_bwd(dres, saved, norm_g, p, w_glu, tag):
    x, hn, raw, dirs, ytot, z, t = saved
    S, D = x.shape
    G, N = p["a_re"].shape[1:]
    dres = _s5_perm(dres[0])

    def glu_bwd(do, zz, tt):
        sg = jax.nn.sigmoid(tt)
        dt = do * zz.astype(F32) * (sg * (1.0 - sg))
        return dt, do * sg, dt

    dt, dzd, db_glu = _rowwise(glu_bwd, [dres, z, t], [BF16, F32, F32], f"s5_out_bwd_{tag}", acc=(2,))
    dz = _mm(dt, w_glu, tb=True, name=f"s5_dz_{tag}", epi=_add_epi, epi_in=(dzd,))
    dw_glu = _mm(z, dt, ta=True, name=f"s5_dwglu_{tag}", pieces='row')

    def gelu_bwd(dzz, y, u, d):
        dy = dzz * _gelu_grad(y)
        return dy, dy * d, dy * u

    dy, du, dd = _rowwise(gelu_bwd, [dz, ytot, hn, p["d"]], [F32, F32, F32], f"s5_gelu_bwd_{tag}", acc=(2,))
    d_lr, d_li, d_bbr, d_bbi, d_cr, d_ci = [], [], [], [], [], []
    du = [du]
    for dirn in range(2):
        xp, mats, lam = dirs[dirn]
        starts = _s5_scan_bwd(dy, None, None, None, None, mats[2], mats[3], *lam, None, dirn == 0,
                              f"s5_starts_{tag}_{dirn}")
        dup, dbr, dbi, dcr, dci, dlr, dli = _s5_scan_bwd(dy, hn, xp, *mats, *lam, starts, dirn == 0,
                                                         f"s5_scan_bwd_{tag}_{dirn}")
        du.append(dup)
        d_lr.append(dlr)
        d_li.append(dli)
        d_bbr.append(_s5_blockdiag_b_t(dbr))
        d_bbi.append(_s5_blockdiag_b_t(dbi))
        d_cr.append(_s5_blockdiag_c_t(dcr))
        d_ci.append(_s5_blockdiag_c_t(dci))
    da_re, da_im, dls, db_re, db_im = _s5_disc_bwd(*raw, jnp.stack(d_lr), jnp.stack(d_li), jnp.stack(d_bbr),
                                                   jnp.stack(d_bbi), f"s5_disc_bwd_{tag}")
    dres, dg = _rmsnorm_bwd(x, norm_g, du, dres, f"s5_norm_bwd_{tag}")
    dres = tuple(_s5_unperm(t_) for t_ in dres)
    unb = lambda b: b.reshape(2, SSM_GROUP_CH, G, N).transpose(0, 2, 3, 1)
    grads = dict(a_re=da_re.reshape(2, G, N), a_im=da_im.reshape(2, G, N), log_step=dls.reshape(2, G, N).sum(-1),
                 b_re=unb(db_re), b_im=unb(db_im), c_re=jnp.stack(d_cr), c_im=jnp.stack(d_ci),
                 d=dd, w_glu=dw_glu, b_glu=db_glu, norm=dg)
    return dres, grads


def _adamw(w, g, m, v, name):
    R, C = w.shape
    tr = _pick(R, (512, 256, 128, 64, 32, 16, 8))
    tn = _pick(C, (512, 256, 128))

    def body(w_ref, g_ref, m_ref, v_ref, d_ref, nm_ref, nv_ref):
        gg = g_ref[...]
        m2 = ADAM_B1 * m_ref[...] + (1.0 - ADAM_B1) * gg
        v2 = ADAM_B2 * v_ref[...] + (1.0 - ADAM_B2) * (gg * gg)
        m_hat = m2 / (1.0 - ADAM_B1 ** ADAM_STEP)
        v_hat = v2 / (1.0 - ADAM_B2 ** ADAM_STEP)
        d_ref[...] = -ADAM_LR * (m_hat / (jnp.sqrt(v_hat) + ADAM_EPS) + ADAM_WD * w_ref[...])
        nm_ref[...] = m2
        nv_ref[...] = v2

    blk = pl.BlockSpec((tr, tn), lambda i, j: (i, j))
    shp = jax.ShapeDtypeStruct((R, C), F32)
    return pl.pallas_call(body, name=name, grid=(R // tr, C // tn), in_specs=[blk] * 4, out_specs=[blk] * 3,
                          out_shape=[shp] * 3, compiler_params=_cp(("parallel", "parallel")))(w, g, m, v)


def _add_selected(p, sel, others, name, also_bf16=False):
    K, _, M, C = p.shape
    tr = _pick(M, [t for t in (512, 256, 128, 64, 32, 16) if t * C * 4 <= 2 ** 21])
    n_o = len(others)

    def body(sel_ref, p_ref, *refs):
        acc = p_ref[...]
        for r in refs[:n_o]:
            acc = acc + r[...].astype(F32)
        refs[n_o][...] = acc
        if also_bf16:
            refs[n_o + 1][...] = acc.astype(BF16)

    blk = pl.BlockSpec((None, tr, C), lambda k, i, s: (k, i, 0))
    grid_spec = pltpu.PrefetchScalarGridSpec(
        num_scalar_prefetch=1, grid=(K, M // tr),
        in_specs=[pl.BlockSpec((None, None, tr, C), lambda k, i, s: (k, s[0], i, 0))] + [blk] * n_o,
        out_specs=[blk, blk] if also_bf16 else blk)
    shp = jax.ShapeDtypeStruct((K, M, C), F32)
    return pl.pallas_call(body, name=name, grid_spec=grid_spec,
                          out_shape=[shp, jax.ShapeDtypeStruct((K, M, C), BF16)] if also_bf16 else shp,
                          compiler_params=_cp(("parallel", "parallel")))(sel, p, *others)


def _sum_slots(a, own, me, name):
    n, R, C = a.shape
    tr = _pick(R, (512, 256, 128, 64, 32, 16, 8))

    def body(me_ref, a_ref, own_ref, o_ref):
        term = lambda k: jnp.where(me_ref[0] == k, own_ref[...], a_ref[k])
        acc = term(0)
        for k in range(1, n):
            acc = acc + term(k)
        o_ref[...] = acc

    grid_spec = pltpu.PrefetchScalarGridSpec(
        num_scalar_prefetch=1, grid=(R // tr,),
        in_specs=[pl.BlockSpec((n, tr, C), lambda i, s: (0, i, 0)), pl.BlockSpec((tr, C), lambda i, s: (i, 0))],
        out_specs=pl.BlockSpec((tr, C), lambda i, s: (i, 0)))
    return pl.pallas_call(body, name=name, grid_spec=grid_spec, out_shape=jax.ShapeDtypeStruct((R, C), F32),
                          compiler_params=_cp(("parallel",)))(me, a, own)


def _position():
    return lax.axis_index("x"), lax.axis_index("y"), lax.axis_index("c")


def _other_chips(x, y):
    return [(1 - x, y), (x, 1 - y), (1 - x, 1 - y)]


def _run_copies(sends, recvs, local, send_sems, recv_sems, local_sems):
    started = []
    for k, (src, dst, dev) in enumerate(sends):
        cp = pltpu.make_async_remote_copy(src_ref=src, dst_ref=dst, send_sem=send_sems.at[k],
                                          recv_sem=recv_sems.at[k], device_id=dev, device_id_type=MESH)
        cp.start()
        started.append(cp)
    locals_ = []
    for k, (src, dst) in enumerate(local):
        cp = pltpu.make_async_copy(src, dst, local_sems.at[k])
        cp.start()
        locals_.append(cp)
    for k, (dst, dev) in enumerate(recvs):
        pltpu.make_async_remote_copy(src_ref=dst, dst_ref=dst, send_sem=send_sems.at[k], recv_sem=recv_sems.at[k],
                                     device_id=dev, device_id_type=MESH).wait_recv()
    for cp in started:
        cp.wait_send()
    for cp in locals_:
        cp.wait()


def _comm_call(plan, ins, out_shapes, n_remote, n_local, name):
    n_in = len(ins)
    n_out = len(out_shapes)

    def body(*refs):
        in_refs, out_refs = refs[:n_in], refs[n_in:n_in + n_out]
        send_sems, recv_sems, local_sems = refs[n_in + n_out:]
        sends, recvs, local = plan(in_refs, out_refs)
        assert len(sends) == len(recvs) == n_remote and len(local) == n_local
        _run_copies(sends, recvs, local, send_sems, recv_sems, local_sems)

    hbm = pl.BlockSpec(memory_space=pltpu.HBM)
    return pl.pallas_call(
        body, name=name, in_specs=[hbm] * n_in, out_specs=[hbm] * n_out, out_shape=out_shapes,
        scratch_shapes=[pltpu.SemaphoreType.DMA((n_remote,)), pltpu.SemaphoreType.DMA((n_remote,)),
                        pltpu.SemaphoreType.DMA((max(n_local, 1),))],
    )(*ins)


def _gather_chips(arrs, name):
    n = len(arrs)

    def plan(ins, outs):
        x, y, c = _position()
        me = 2 * x + y
        sends, recvs = [], []
        for px, py in _other_chips(x, y):
            for i in range(n):
                sends.append((ins[i], outs[i].at[me], (px, py, c)))
                recvs.append((outs[i].at[2 * px + py], (px, py, c)))
        return sends, recvs, [(ins[i], outs[i].at[me]) for i in range(n)]

    shapes = [jax.ShapeDtypeStruct((4,) + a.shape, a.dtype) for a in arrs]
    return _comm_call(plan, arrs, shapes, 3 * n, n, name)


def _gather_chips_two_level(arrs, name):
    n = len(arrs)
    hr = [a.shape[1] // 2 for a in arrs]

    def body(*refs):
        ins, outs = refs[:n], refs[n:2 * n]
        send_sems, recv_sems = refs[2 * n:]
        x, y, c = _position()
        me = 2 * x + y
        sib = (x, y, 1 - c)
        chips = _other_chips(x, y)

        def remote(k, src, dst, dev):
            return pltpu.make_async_remote_copy(src_ref=src, dst_ref=dst, send_sem=send_sems.at[k],
                                                recv_sem=recv_sems.at[k], device_id=dev, device_id_type=MESH)

        first = []
        for j, (px, py) in enumerate(chips):
            for i in range(n):
                cp = remote(j * n + i, ins[i].at[:, pl.ds(c * hr[i], hr[i])], outs[i].at[me, c], (px, py, c))
                cp.start()
                first.append(cp)
        passed = []
        for j, (px, py) in enumerate(chips):
            for i in range(n):
                landed = outs[i].at[2 * px + py, c]
                remote(j * n + i, landed, landed, (px, py, c)).wait_recv()
                cp = remote(3 * n + j * n + i, landed, landed, sib)
                cp.start()
                passed.append(cp)
        for j, (px, py) in enumerate(chips):
            for i in range(n):
                slot = outs[i].at[2 * px + py, 1 - c]
                remote(3 * n + j * n + i, slot, slot, sib).wait_recv()
        for cp in first + passed:
            cp.wait_send()

    hbm = pl.BlockSpec(memory_space=pltpu.HBM)
    shapes = [jax.ShapeDtypeStruct((N_CHIPS, 2, a.shape[0], a.shape[1] // 2, a.shape[2]), a.dtype) for a in arrs]
    return pl.pallas_call(
        body, name=name, in_specs=[hbm] * n, out_specs=[hbm] * n, out_shape=shapes,
        scratch_shapes=[pltpu.SemaphoreType.DMA((6 * n,)), pltpu.SemaphoreType.DMA((6 * n,))],
    )(*arrs)


def _sibling_halves(pieces, name):
    n = len(pieces)

    def plan(ins, outs):
        x, y, c = _position()
        sib = (x, y, 1 - c)
        return ([(ins[i].at[:, 1 - c], outs[i], sib) for i in range(n)],
                [(outs[i], sib) for i in range(n)], [])

    shapes = [jax.ShapeDtypeStruct((p.shape[0],) + p.shape[2:], p.dtype) for p in pieces]
    return _comm_call(plan, pieces, shapes, n, 0, name)


def _scatter_chips(sums, name):
    n = len(sums)

    def plan(ins, outs):
        x, y, c = _position()
        sends, recvs = [], []
        for j, (px, py) in enumerate(_other_chips(x, y)):
            for i in range(n):
                sends.append((ins[i].at[2 * px + py], outs[i].at[j], (px, py, c)))
                recvs.append((outs[i].at[j], (px, py, c)))
        return sends, recvs, []

    shapes = [jax.ShapeDtypeStruct((3,) + s.shape[1:], s.dtype) for s in sums]
    return _comm_call(plan, sums, shapes, 3 * n, 0, name)


def _sibling_swap(halves, name):
    n = len(halves)

    def plan(ins, outs):
        x, y, c = _position()
        sib = (x, y, 1 - c)
        return [(ins[i], outs[i], sib) for i in range(n)], [(outs[i], sib) for i in range(n)], []

    shapes = [jax.ShapeDtypeStruct(h.shape, h.dtype) for h in halves]
    return _comm_call(plan, halves, shapes, n, 0, name)


def _gather_all(v, name):
    rels = [(dx, dy, dc) for dx in (0, 1) for dy in (0, 1) for dc in (0, 1)][1:]

    def plan(ins, outs):
        x, y, c = _position()
        me = 4 * x + 2 * y + c
        flip = lambda a, d: 1 - a if d else a
        sends, recvs = [], []
        for dx, dy, dc in rels:
            px, py, pc = flip(x, dx), flip(y, dy), flip(c, dc)
            sends.append((ins[0], outs[0].at[me], (px, py, pc)))
            recvs.append((outs[0].at[4 * px + 2 * py + pc], (px, py, pc)))
        return sends, recvs, []

    return _comm_call(plan, [v], [jax.ShapeDtypeStruct((8,) + v.shape, v.dtype)], len(rels), 0, name)[0]


WEIGHTS = ['mix_norm', 'ffn_norm', 'final_norm', 'attn_w_qkv', 'attn_w_o', 'attn_sink', 'ssm_a_re', 'ssm_a_im',
           'ssm_log_step', 'ssm_b_re', 'ssm_b_im', 'ssm_c_re', 'ssm_c_im', 'ssm_d', 'ssm_w_glu', 'ssm_b_glu',
           'mla_w_dqkv', 'mla_q_norm', 'mla_kv_norm', 'mla_w_uq', 'mla_w_ukv', 'mla_w_o', 'ffn_w_up',
           'ffn_conv_w', 'ffn_conv_b', 'ffn_w_down']
BIG = dict(attn_w_qkv='col', attn_w_o='row', ssm_w_glu='row', mla_w_dqkv='row', mla_w_uq='col',
           mla_w_ukv='col', mla_w_o='row', ffn_w_up='col', ffn_w_down='row')
SMALL_SHARDED = ('mla_q_norm', 'mla_kv_norm', 'ffn_conv_w')
N_CHIPS = 4


def _assemble(g, own, chip, kind):
    L = g.shape[2]
    own_halves = own.reshape(L, 2, g.shape[3], g.shape[4]).transpose(1, 0, 2, 3)
    slot = lax.broadcasted_iota(jnp.int32, (N_CHIPS, 1, 1, 1, 1), 0)
    g = jnp.where(slot == chip, own_halves[None], g)
    if kind == 'row':
        return g.transpose(2, 0, 1, 3, 4).reshape(L, -1, g.shape[4])
    return g.transpose(2, 1, 3, 0, 4).reshape(L, 2 * g.shape[3], -1)


def _to_pieces(w, kind):
    L, R, C = w.shape
    if kind == 'col':
        t = w.reshape(L, 2, R // 2, N_CHIPS, C // N_CHIPS).transpose(3, 1, 0, 2, 4)
    else:
        t = w.reshape(L, N_CHIPS, R // N_CHIPS, 2, C // 2).transpose(1, 3, 0, 2, 4)
    return t.reshape(N_CHIPS, 2, L * t.shape[3], t.shape[4])


def _from_halves(h, kind, shard_shape):
    L = shard_shape[0]
    t = h.reshape(2, L, -1, h.shape[2])
    t = t.transpose(1, 0, 2, 3) if kind == 'col' else t.transpose(1, 2, 0, 3)
    return t.reshape(shard_shape)


def _pack(arrs):
    flat = jnp.concatenate([a.reshape(-1) for a in arrs])
    pad = (-flat.shape[0]) % (8 * LANES)
    return jnp.pad(flat, (0, pad)).reshape(-1, LANES)


def _unpack(packed, like):
    flat = packed.reshape(-1)
    out, off = [], 0
    for a in like:
        out.append(flat[off:off + a.size].reshape(a.shape))
        off += a.size
    return out


def kernel(x, mix_norm, ffn_norm, final_norm, attn_w_qkv, attn_w_o, attn_sink, ssm_a_re, ssm_a_im, ssm_log_step, ssm_b_re, ssm_b_im, ssm_c_re, ssm_c_im, ssm_d, ssm_w_glu, ssm_b_glu, mla_w_dqkv, mla_q_norm, mla_kv_norm, mla_w_uq, mla_w_ukv, mla_w_o, ffn_w_up, ffn_conv_w, ffn_conv_b, ffn_w_down, loss_target, m_mix_norm, m_ffn_norm, m_final_norm, m_attn_w_qkv, m_attn_w_o, m_attn_sink, m_ssm_a_re, m_ssm_a_im, m_ssm_log_step, m_ssm_b_re, m_ssm_b_im, m_ssm_c_re, m_ssm_c_im, m_ssm_d, m_ssm_w_glu, m_ssm_b_glu, m_mla_w_dqkv, m_mla_q_norm, m_mla_kv_norm, m_mla_w_uq, m_mla_w_ukv, m_mla_w_o, m_ffn_w_up, m_ffn_conv_w, m_ffn_conv_b, m_ffn_w_down, v_mix_norm, v_ffn_norm, v_final_norm, v_attn_w_qkv, v_attn_w_o, v_attn_sink, v_ssm_a_re, v_ssm_a_im, v_ssm_log_step, v_ssm_b_re, v_ssm_b_im, v_ssm_c_re, v_ssm_c_im, v_ssm_d, v_ssm_w_glu, v_ssm_b_glu, v_mla_w_dqkv, v_mla_q_norm, v_mla_kv_norm, v_mla_w_uq, v_mla_w_ukv, v_mla_w_o, v_ffn_w_up, v_ffn_conv_w, v_ffn_conv_b, v_ffn_w_down):
    args = locals()
    w = {n: args[n] for n in WEIGHTS}
    mom = {n: args["m_" + n] for n in WEIGHTS}
    var = {n: args["v_" + n] for n in WEIGHTS}
    depth = mix_norm.shape[0]
    xs = x[0]
    target = loss_target[0]
    chip = 2 * lax.axis_index("x") + lax.axis_index("y")
    core = lax.axis_index("c")

    big_names = list(BIG)
    w_bf = [w[n].astype(BF16) for n in big_names]
    gathered = _gather_chips_two_level(w_bf, "gather_weights")
    full = {n: _assemble(g, own, chip, BIG[n]) for n, g, own in zip(big_names, gathered, w_bf)}
    for n, g in zip(SMALL_SHARDED, _gather_chips([w[n] for n in SMALL_SHARDED], "gather_small_weights")):
        full[n] = jnp.moveaxis(g, 0, -2).reshape(g.shape[1:-1] + (-1,))

    def ffn_args(i):
        return (ffn_norm[i:i + 1], full["ffn_w_up"][i], full["ffn_conv_w"][i], ffn_conv_b[i:i + 1], full["ffn_w_down"][i])

    def s5_params(j):
        return dict(a_re=ssm_a_re[j], a_im=ssm_a_im[j], log_step=ssm_log_step[j], b_re=ssm_b_re[j], b_im=ssm_b_im[j],
                    c_re=ssm_c_re[j], c_im=ssm_c_im[j], d=ssm_d[j:j + 1], b_glu=ssm_b_glu[j:j + 1])

    def mla_args(j):
        wd, wq, wkv = _mla_weights(full["mla_w_dqkv"][j], full["mla_w_uq"][j], full["mla_w_ukv"][j])
        return (wd, wq, wkv, full["mla_w_o"][j], full["mla_q_norm"][j:j + 1], full["mla_kv_norm"][j:j + 1])

    h = xs
    saved = []
    for i in range(depth):
        kind, j = i % 3, i // 3
        g = mix_norm[i:i + 1]
        if kind == 0:
            h, sm = _attn_layer_fwd(h, g, full["attn_w_qkv"][j], full["attn_w_o"][j], attn_sink[j], f"{i}")
        elif kind == 1:
            h, sm = _s5_layer_fwd(h, g, s5_params(j), full["ssm_w_glu"][j], f"{i}")
        else:
            h, sm = _mla_layer_fwd(h, g, *mla_args(j), f"{i}")
        h, sf = _ffn_fwd(h, *ffn_args(i), f"{i}")
        saved.append((sm, sf))
    loss_part, dres, d_final = _loss_head(h, final_norm.reshape(1, -1), target, "loss_head")

    gl = {n: [None] * w[n].shape[0] for n in WEIGHTS if n != 'final_norm'}
    for i in reversed(range(depth)):
        kind, j = i % 3, i // 3
        sm, sf = saved[i]
        dres, (dg, dwu, dcw, dcb, dwd) = _ffn_bwd(dres, sf, *ffn_args(i), f"{i}")
        gl['ffn_norm'][i], gl['ffn_w_up'][i], gl['ffn_conv_w'][i] = dg[0], dwu, dcw
        gl['ffn_conv_b'][i], gl['ffn_w_down'][i] = dcb[0], dwd
        g = mix_norm[i:i + 1]
        if kind == 0:
            dres, (dg, dwq, dwo, dsk) = _attn_layer_bwd(dres, sm, g, full["attn_w_qkv"][j], full["attn_w_o"][j],
                                                        attn_sink[j], f"{i}")
            gl['attn_w_qkv'][j], gl['attn_w_o'][j], gl['attn_sink'][j] = dwq, dwo, dsk
        elif kind == 1:
            dres, gs = _s5_layer_bwd(dres, sm, g, s5_params(j), full["ssm_w_glu"][j], f"{i}")
            dg = gs['norm']
            for k in ('a_re', 'a_im', 'log_step', 'b_re', 'b_im', 'c_re', 'c_im'):
                gl['ssm_' + k][j] = gs[k]
            gl['ssm_d'][j], gl['ssm_b_glu'][j], gl['ssm_w_glu'][j] = gs['d'][0], gs['b_glu'][0], gs['w_glu']
        else:
            dres, (dg, dwd_, dqn, dkvn, dwuq, dwukv, dwo) = _mla_layer_bwd(dres, sm, g, *mla_args(j), f"{i}")
            gl['mla_w_dqkv'][j], gl['mla_q_norm'][j], gl['mla_kv_norm'][j] = dwd_, dqn[0], dkvn[0]
            gl['mla_w_uq'][j], gl['mla_w_ukv'][j], gl['mla_w_o'][j] = dwuq, dwukv, dwo
        gl['mix_norm'][i] = dg[0]
    local = {n: jnp.stack(v) for n, v in gl.items() if n not in BIG}
    local['final_norm'] = d_final[0]

    items = [(n, l) for n in big_names for l in range(w[n].shape[0])]
    pieces = [gl[n][l] if gl[n][l].ndim == 4 else _to_pieces(gl[n][l][None], BIG[n]) for n, l in items]
    tags = [f"{n}_{l}" for n, l in items]
    from_sib = _sibling_halves(pieces, "reduce_sibling")
    sel_core = jnp.reshape(core, (1,)).astype(jnp.int32)
    pair = [_add_selected(p, sel_core, [r], f"reduce_add_pair_{t}", also_bf16=True)
            for t, p, r in zip(tags, pieces, from_sib)]
    from_chips = _scatter_chips([pb for _, pb in pair], "reduce_chips")
    sel_chip = jnp.reshape(chip, (1,)).astype(jnp.int32)
    mine = [_add_selected(p.reshape((1,) + p.shape), sel_chip, [r[k:k + 1] for k in range(3)], f"reduce_add_chips_{t}")[0]
            for t, (p, _), r in zip(tags, pair, from_chips)]
    theirs = _sibling_swap(mine, "reduce_join")
    per_layer = {n: [] for n in big_names}
    for (n, l), a, b in zip(items, mine, theirs):
        halves = jnp.where(core == 0, jnp.stack([a, b]), jnp.stack([b, a]))
        per_layer[n].append(_from_halves(halves, BIG[n], (1,) + w[n].shape[1:]))
    grads = {n: jnp.concatenate(v, axis=0) for n, v in per_layer.items()}

    small_names = [n for n in WEIGHTS if n not in BIG]
    packed = _pack([local[n] for n in small_names])
    device = jnp.reshape(2 * chip + core, (1,)).astype(jnp.int32)
    summed = _sum_slots(_gather_all(packed, "reduce_small"), packed, device, "reduce_small_sum")
    for n, gsum in zip(small_names, _unpack(summed, [local[n] for n in small_names])):
        if n in SMALL_SHARDED:
            width = w[n].shape[-1]
            gsum = lax.dynamic_slice_in_dim(gsum, chip * width, width, axis=gsum.ndim - 1)
        grads[n] = gsum

    delta, new_m, new_v = {}, {}, {}
    for n in big_names:
        two_d = lambda a: a.reshape(-1, a.shape[-1])
        d_, m_, v_ = _adamw(two_d(w[n]), two_d(grads[n]), two_d(mom[n]), two_d(var[n]), f"adamw_{n}")
        delta[n], new_m[n], new_v[n] = (t.reshape(w[n].shape) for t in (d_, m_, v_))
    outs = _adamw(*[_pack([src[n] for n in small_names]) for src in (w, grads, mom, var)], "adamw_small")
    for dst, packed in zip((delta, new_m, new_v), outs):
        for n, t in zip(small_names, _unpack(packed, [w[n] for n in small_names])):
            dst[n] = t

    loss = lax.psum(loss_part[0, 0], ("x", "y", "c"))
    return (loss, dres[0][None], *[grads[n] for n in WEIGHTS], *[delta[n] for n in WEIGHTS],
            *[new_m[n] for n in WEIGHTS], *[new_v[n] for n in WEIGHTS])
```

```python
import functools
import math

import numpy as np
import jax
import jax.numpy as jnp
from jax import lax
from jax.experimental import pallas as pl
from jax.experimental.pallas import tpu as pltpu

F32 = jnp.float32
BF16 = jnp.bfloat16
MESH = pl.DeviceIdType.MESH

RMS_EPS = 1e-6
ATTN_DH = 64
ATTN_GROUP = 8
ATTN_BLOCK = 128
SSM_GROUP_CH = 16
SSM_STATE = 64
SSM_SEGMENTS = 8
MLA_HEADS = 16
MLA_LORA = 512
MLA_NOPE = 128
MLA_ROPE = 64
MLA_V = 128
ROPE_THETA = 10000.0
LANES = 128
VMEM_LIMIT = 56 * 1024 * 1024

ADAM_LR = 0.001
ADAM_B1 = 0.9
ADAM_B2 = 0.999
ADAM_EPS = 1e-08
ADAM_WD = 0.01
ADAM_STEP = 10


def _cp(sem=None):
    kw = dict(vmem_limit_bytes=VMEM_LIMIT)
    if sem is not None:
        kw["dimension_semantics"] = sem
    return pltpu.CompilerParams(**kw)


MM_FULL_K = 2048
MM_FULL_N = 1536


def _pick(n, cands):
    for c in cands:
        if n % c == 0:
            return c
    return n


class _Exchange:
    def __init__(self, ins, out_shapes, n_remote, plan):
        self.ins, self.out_shapes, self.n_remote, self.plan = list(ins), list(out_shapes), n_remote, plan

    def _copy(self, k, src, dst, dev, send_sems, recv_sems):
        return pltpu.make_async_remote_copy(src_ref=src, dst_ref=dst, send_sem=send_sems.at[k],
                                            recv_sem=recv_sems.at[k], device_id=dev, device_id_type=MESH)

    def start(self, in_refs, out_refs, send_sems, recv_sems):
        sends, recvs = self.plan(in_refs, out_refs)
        assert len(sends) == len(recvs) == self.n_remote
        for k, (src, dst, dev) in enumerate(sends):
            self._copy(k, src, dst, dev, send_sems, recv_sems).start()

    def wait(self, in_refs, out_refs, send_sems, recv_sems):
        sends, recvs = self.plan(in_refs, out_refs)
        for k, (dst, dev) in enumerate(recvs):
            self._copy(k, dst, dst, dev, send_sems, recv_sems).wait_recv()
        for k, (src, dst, dev) in enumerate(sends):
            self._copy(k, src, dst, dev, send_sems, recv_sems).wait_send()

    def sem_shapes(self):
        return [pltpu.SemaphoreType.DMA((self.n_remote,)), pltpu.SemaphoreType.DMA((self.n_remote,))]


def _gmm(a, b, *, grid, a_spec, b_spec, o_spec, out_shape, dims, kax, acc_shape, name,
         epi=None, epi_in=(), epi_specs=(), hosted=None):
    nk = grid[kax]
    n_epi = len(epi_in)
    n_hin = len(hosted.ins) if hosted else 0
    n_hout = len(hosted.out_shapes) if hosted else 0

    def finish(r, e_refs, o_ref):
        if epi is not None:
            r = epi(r, *[e[...] for e in e_refs])
        o_ref[...] = r.astype(o_ref.dtype)

    def product(a_ref, b_ref):
        return lax.dot_general(a_ref[...].astype(BF16), b_ref[...].astype(BF16), dims, preferred_element_type=F32)

    def body(a_ref, b_ref, *rest):
        e_refs = rest[:n_epi]
        h_in = rest[n_epi:n_epi + n_hin]
        o_ref = rest[n_epi + n_hin]
        h_out = rest[n_epi + n_hin + 1:n_epi + n_hin + 1 + n_hout]
        scratch = rest[n_epi + n_hin + 1 + n_hout:]
        if hosted:
            sems = scratch[-2:]
            ids = [pl.program_id(i) for i in range(len(grid))]
            first = functools.reduce(jnp.logical_and, [i == 0 for i in ids])
            last = functools.reduce(jnp.logical_and, [i == n - 1 for i, n in zip(ids, grid)])
            pl.when(first)(lambda: hosted.start(h_in, h_out, *sems))
        if nk == 1:
            finish(product(a_ref, b_ref), e_refs, o_ref)
        else:
            acc_ref = scratch[0]
            k = pl.program_id(kax)

            @pl.when(k == 0)
            def _():
                acc_ref[...] = jnp.zeros_like(acc_ref)

            acc_ref[...] += product(a_ref, b_ref)

            @pl.when(k == nk - 1)
            def _():
                finish(acc_ref[...], e_refs, o_ref)
        if hosted:
            pl.when(last)(lambda: hosted.wait(h_in, h_out, *sems))

    sem = tuple("arbitrary" if (i == kax or hosted) else "parallel" for i in range(len(grid)))
    hbm = pl.BlockSpec(memory_space=pltpu.HBM)
    res = pl.pallas_call(
        body, name=name, grid=grid,
        in_specs=[a_spec, b_spec, *epi_specs] + [hbm] * n_hin,
        out_specs=[o_spec] + [hbm] * n_hout if hosted else o_spec,
        out_shape=[out_shape] + hosted.out_shapes if hosted else out_shape,
        scratch_shapes=([] if nk == 1 else [pltpu.VMEM(acc_shape, F32)]) + (hosted.sem_shapes() if hosted else []),
        compiler_params=_cp(sem),
    )(a, b, *epi_in, *(hosted.ins if hosted else ()))
    return (res[0], list(res[1:])) if hosted else res


def _mm(a, b, *, ta=False, tb=False, out_dtype=F32, name, epi=None, epi_in=(), pieces=None, hosted=None):
    a_parts = a.shape[0] if a.ndim == 3 else 1
    b_parts = b.shape[0] if b.ndim == 3 else 1
    assert not (ta and a_parts > 1) and not (tb and b_parts > 1)
    (K, M) = a.shape if ta else (a.shape[-1] * a_parts, a.shape[-2])
    (N, K2) = b.shape if tb else (b.shape[-1] * b_parts, b.shape[-2])
    assert K == K2, (a.shape, b.shape, ta, tb)
    m_unit = {None: M, 'col': M // 2, 'row': M // N_CHIPS}[pieces]
    n_unit = {None: N // b_parts, 'col': N // N_CHIPS, 'row': N // 2}[pieces]
    k_unit = K // a_parts
    tm = _pick(m_unit, (1024, 1408, 512, 256, 128))
    tn = n_unit if n_unit <= MM_FULL_N else _pick(n_unit, (1024, 1408, 512, 256, 128))
    tk = k_unit if k_unit <= MM_FULL_K else _pick(k_unit, (2048, 1408, 1024, 512, 256, 128))
    grid = (M // tm, N // tn, K // tk)
    if ta:
        a_spec = pl.BlockSpec((tk, tm), lambda i, j, k: (k, i))
    elif a_parts > 1:
        nkp = k_unit // tk
        a_spec = pl.BlockSpec((None, tm, tk), lambda i, j, k: (k // nkp, i, k % nkp))
    else:
        a_spec = pl.BlockSpec((tm, tk), lambda i, j, k: (i, k))
    if tb:
        b_spec = pl.BlockSpec((tn, tk), lambda i, j, k: (j, k))
    elif b_parts > 1:
        njp = (N // b_parts) // tn
        b_spec = pl.BlockSpec((None, tk, tn), lambda i, j, k: (j // njp, k, j % njp))
    else:
        b_spec = pl.BlockSpec((tk, tn), lambda i, j, k: (k, j))
    dims = (((0 if ta else 1,), (1 if tb else 0,)), ((), ()))
    epi_specs = [pl.BlockSpec((1, tn), lambda i, j, k: (0, j)) if e.shape[0] == 1
                 else pl.BlockSpec((tm, tn), lambda i, j, k: (i, j)) for e in epi_in]
    if pieces is None:
        o_spec = pl.BlockSpec((tm, tn), lambda i, j, k: (i, j))
        out_shape = jax.ShapeDtypeStruct((M, N), out_dtype)
    else:
        mi, nj = m_unit // tm, n_unit // tn
        if pieces == 'col':
            o_spec = pl.BlockSpec((None, None, tm, tn), lambda i, j, k: (j // nj, i // mi, i % mi, j % nj))
        else:
            o_spec = pl.BlockSpec((None, None, tm, tn), lambda i, j, k: (i // mi, j // nj, i % mi, j % nj))
        out_shape = jax.ShapeDtypeStruct((N_CHIPS, 2, m_unit, n_unit), out_dtype)
    return _gmm(a, b, grid=grid, a_spec=a_spec, b_spec=b_spec, o_spec=o_spec, out_shape=out_shape, dims=dims, kax=2,
                acc_shape=(tm, tn), name=name, epi=epi, epi_in=epi_in, epi_specs=epi_specs, hosted=hosted)


def _add_epi(r, res):
    return res + r


def _rmsnorm_fwd(x, g, out_dtype, name):
    S, D = x.shape
    tr = _pick(S, (256, 128, 8))

    def body(x_ref, g_ref, o_ref):
        xf = x_ref[...]
        r = lax.rsqrt(jnp.mean(xf * xf, axis=-1, keepdims=True) + RMS_EPS)
        o_ref[...] = ((xf * r) * g_ref[...]).astype(o_ref.dtype)

    return pl.pallas_call(
        body, name=name, grid=(S // tr,),
        in_specs=[pl.BlockSpec((tr, D), lambda i: (i, 0)), pl.BlockSpec((1, D), lambda i: (0, 0))],
        out_specs=pl.BlockSpec((tr, D), lambda i: (i, 0)),
        out_shape=jax.ShapeDtypeStruct((S, D), out_dtype), compiler_params=_cp(("parallel",)),
    )(x, g)


def _rms_bwd_math(xf, g, dy):
    r = lax.rsqrt(jnp.mean(xf * xf, axis=-1, keepdims=True) + RMS_EPS)
    xh = xf * r
    dxh = dy * g
    dx = r * (dxh - xh * jnp.mean(dxh * xh, axis=-1, keepdims=True))
    return dx, jnp.sum(dy * xh, axis=0, keepdims=True)


def _rmsnorm_bwd(x, g, dy, dres, name):
    S, D = x.shape
    tr = _pick(S, (256, 128, 8))
    dys = list(dy) if isinstance(dy, (list, tuple)) else [dy]
    n_dy = len(dys)

    def body(x_ref, g_ref, *refs):
        dy_refs = refs[:n_dy]
        dres_ref, dx_ref, dxb_ref, dg_ref = refs[n_dy:]

        @pl.when(pl.program_id(0) == 0)
        def _():
            dg_ref[...] = jnp.zeros_like(dg_ref)

        dy_sum = dy_refs[0][...].astype(F32)
        for r in dy_refs[1:]:
            dy_sum = dy_sum + r[...].astype(F32)
        dx, dg = _rms_bwd_math(x_ref[...], g_ref[...], dy_sum)
        tot = dres_ref[...] + dx
        dx_ref[...] = tot
        dxb_ref[...] = tot.astype(BF16)
        dg_ref[...] += dg

    row = pl.BlockSpec((tr, D), lambda i: (i, 0))
    vec = pl.BlockSpec((1, D), lambda i: (0, 0))
    dx, dxb, dg = pl.pallas_call(
        body, name=name, grid=(S // tr,), in_specs=[row, vec] + [row] * (n_dy + 1), out_specs=[row, row, vec],
        out_shape=[jax.ShapeDtypeStruct((S, D), F32), jax.ShapeDtypeStruct((S, D), BF16),
                   jax.ShapeDtypeStruct((1, D), F32)],
        compiler_params=_cp(("arbitrary",)),
    )(x, g, *dys, dres)
    return (dx, dxb), dg


def _loss_head(x, g, target, name):
    S, D = x.shape
    tr = _pick(S, (256, 128, 8))

    def body(x_ref, g_ref, t_ref, loss_ref, dx_ref, dxb_ref, dg_ref):
        @pl.when(pl.program_id(0) == 0)
        def _():
            dg_ref[...] = jnp.zeros_like(dg_ref)
            loss_ref[...] = jnp.zeros_like(loss_ref)

        xf = x_ref[...]
        gg = g_ref[...]
        r = lax.rsqrt(jnp.mean(xf * xf, axis=-1, keepdims=True) + RMS_EPS)
        e = (xf * r) * gg - t_ref[...]
        loss_ref[...] += 0.5 * jnp.sum(jnp.mean(e * e, axis=-1, keepdims=True), axis=0, keepdims=True)
        dx, dg = _rms_bwd_math(xf, gg, e * (1.0 / D))
        dx_ref[...] = dx
        dxb_ref[...] = dx.astype(BF16)
        dg_ref[...] += dg

    row = pl.BlockSpec((tr, D), lambda i: (i, 0))
    vec = pl.BlockSpec((1, D), lambda i: (0, 0))
    one = pl.BlockSpec((1, 1), lambda i: (0, 0))
    loss, dx, dxb, dg = pl.pallas_call(
        body, name=name, grid=(S // tr,), in_specs=[row, vec, row], out_specs=[one, row, row, vec],
        out_shape=[jax.ShapeDtypeStruct((1, 1), F32), jax.ShapeDtypeStruct((S, D), F32),
                   jax.ShapeDtypeStruct((S, D), BF16), jax.ShapeDtypeStruct((1, D), F32)],
        compiler_params=_cp(("arbitrary",)),
    )(x, g, target)
    return loss, (dx, dxb), dg


HALO = 16


def _shift_rows(main, prev_row, next_row):
    tr = main.shape[0]
    row = lax.broadcasted_iota(jnp.int32, main.shape, 0)
    up = jnp.where(row == 0, prev_row, pltpu.roll(main, 1, 0))
    dn = jnp.where(row == tr - 1, next_row, pltpu.roll(main, tr - 1, 0))
    return up, dn


def _halo_specs(tr, tn, S, col_of):
    hb = tr // HALO
    last = S // HALO - 1
    return [pl.BlockSpec((tr, tn), lambda j, i: (i, col_of(j))),
            pl.BlockSpec((HALO, tn), lambda j, i: (jnp.maximum(i * hb - 1, 0), col_of(j))),
            pl.BlockSpec((HALO, tn), lambda j, i: (jnp.minimum((i + 1) * hb, last), col_of(j)))]


def _halo_rows(main_ref, prev_ref, next_ref, i, n_i):
    main = main_ref[...].astype(F32)
    prev_row = prev_ref[HALO - 1:HALO, :].astype(F32) * (i > 0).astype(F32)
    next_row = next_ref[0:1, :].astype(F32) * (i < n_i - 1).astype(F32)
    up, dn = _shift_rows(main, prev_row, next_row)
    return up, main, dn


def _conv3(w_ref, b_ref, up, mid, dn):
    return b_ref[...] + w_ref[0:1, :] * up + w_ref[1:2, :] * mid + w_ref[2:3, :] * dn


def _ffn_tiles(S, F):
    return _pick(S, (512, 256, 128, 16)), _pick(F, (512, 256, 128))


def _conv_gate_fwd(u, conv_w, conv_b, name):
    S, F2 = u.shape
    F = F2 // 2
    tr, tn = _ffn_tiles(S, F)
    nj, ni = F // tn, S // tr

    def body(gm, gp, gn, vm, vp, vn, wg, wv, bg, bv, o_ref):
        i = pl.program_id(1)
        cg = _conv3(wg, bg, *_halo_rows(gm, gp, gn, i, ni))
        cv = _conv3(wv, bv, *_halo_rows(vm, vp, vn, i, ni))
        o_ref[...] = (cg * jax.nn.sigmoid(cg) * cv).astype(o_ref.dtype)

    wspec = lambda off: pl.BlockSpec((3, tn), lambda j, i: (0, j + off))
    bspec = lambda off: pl.BlockSpec((1, tn), lambda j, i: (0, j + off))
    return pl.pallas_call(
        body, name=name, grid=(nj, ni),
        in_specs=[*_halo_specs(tr, tn, S, lambda j: j), *_halo_specs(tr, tn, S, lambda j: j + nj),
                  wspec(0), wspec(nj), bspec(0), bspec(nj)],
        out_specs=pl.BlockSpec((tr, tn), lambda j, i: (i, j)),
        out_shape=jax.ShapeDtypeStruct((S, F), BF16), compiler_params=_cp(("parallel", "parallel")),
    )(u, u, u, u, u, u, conv_w, conv_w, conv_b, conv_b)


def _conv_gate_bwd(u, da, conv_w, conv_b, name):
    S, F2 = u.shape
    F = F2 // 2
    tr, tn = _ffn_tiles(S, F)
    nj, ni = F // tn, S // tr

    def body(gm, gp, gn, vm, vp, vn, wg, wv, bg, bv, da_ref, dc_ref, dwg_ref, dwv_ref, dbg_ref, dbv_ref):
        i = pl.program_id(1)

        @pl.when(i == 0)
        def _():
            for r in (dwg_ref, dwv_ref, dbg_ref, dbv_ref):
                r[...] = jnp.zeros_like(r)

        g_rows = _halo_rows(gm, gp, gn, i, ni)
        v_rows = _halo_rows(vm, vp, vn, i, ni)
        cg = _conv3(wg, bg, *g_rows)
        cv = _conv3(wv, bv, *v_rows)
        sg = jax.nn.sigmoid(cg)
        d = da_ref[...].astype(F32)
        dcv = d * (cg * sg)
        dcg = d * cv * (sg * (1.0 + cg * (1.0 - sg)))
        dc_ref[0] = dcg.astype(dc_ref.dtype)
        dc_ref[1] = dcv.astype(dc_ref.dtype)
        for t in range(3):
            dwg_ref[t:t + 1, :] += jnp.sum(dcg * g_rows[t], axis=0, keepdims=True)
            dwv_ref[t:t + 1, :] += jnp.sum(dcv * v_rows[t], axis=0, keepdims=True)
        dbg_ref[...] += jnp.sum(dcg, axis=0, keepdims=True)
        dbv_ref[...] += jnp.sum(dcv, axis=0, keepdims=True)

    wspec = lambda off: pl.BlockSpec((3, tn), lambda j, i: (0, j + off))
    bspec = lambda off: pl.BlockSpec((1, tn), lambda j, i: (0, j + off))
    tile = pl.BlockSpec((tr, tn), lambda j, i: (i, j))
    outs = pl.pallas_call(
        body, name=name, grid=(nj, ni),
        in_specs=[*_halo_specs(tr, tn, S, lambda j: j), *_halo_specs(tr, tn, S, lambda j: j + nj),
                  wspec(0), wspec(nj), bspec(0), bspec(nj), tile],
        out_specs=[pl.BlockSpec((2, tr, tn), lambda j, i: (0, i, j)), wspec(0), wspec(0), bspec(0), bspec(0)],
        out_shape=[jax.ShapeDtypeStruct((2, S, F), BF16),
                   jax.ShapeDtypeStruct((3, F), F32), jax.ShapeDtypeStruct((3, F), F32),
                   jax.ShapeDtypeStruct((1, F), F32), jax.ShapeDtypeStruct((1, F), F32)],
        compiler_params=_cp(("parallel", "arbitrary")),
    )(u, u, u, u, u, u, conv_w, conv_w, conv_b, conv_b, da)
    dc, dwg, dwv, dbg, dbv = outs
    return dc, jnp.concatenate([dwg, dwv], axis=1), jnp.concatenate([dbg, dbv], axis=1)


def _conv_transpose(dc, w, name):
    _, S, F = dc.shape
    tr, tn = _ffn_tiles(S, F)
    nj, ni = F // tn, S // tr
    hb = tr // HALO
    last = S // HALO - 1

    def body(m, p, n, w_ref, o_ref):
        up, mid, dn = _halo_rows(m, p, n, pl.program_id(2), ni)
        o_ref[...] = (w_ref[0:1, :] * dn + w_ref[1:2, :] * mid + w_ref[2:3, :] * up).astype(o_ref.dtype)

    tile = pl.BlockSpec((None, tr, tn), lambda h, j, i: (h, i, j))
    return pl.pallas_call(
        body, name=name, grid=(2, nj, ni),
        in_specs=[tile,
                  pl.BlockSpec((None, HALO, tn), lambda h, j, i: (h, jnp.maximum(i * hb - 1, 0), j)),
                  pl.BlockSpec((None, HALO, tn), lambda h, j, i: (h, jnp.minimum((i + 1) * hb, last), j)),
                  pl.BlockSpec((3, tn), lambda h, j, i: (0, j + h * nj))],
        out_specs=tile, out_shape=jax.ShapeDtypeStruct((2, S, F), BF16),
        compiler_params=_cp(("parallel", "parallel", "parallel")),
    )(dc, dc, dc, w)


def _ffn_fwd(x, norm_g, w_up, conv_w, conv_b, w_down, tag, next_blocks=None):
    hn = _rmsnorm_fwd(x, norm_g, BF16, f"ffn_norm_{tag}")
    if next_blocks is None:
        u = _mm(hn, w_up, out_dtype=BF16, name=f"ffn_up_{tag}")
    else:
        u, over_ici = _mm(hn, w_up, out_dtype=BF16, name=f"ffn_up_{tag}", hosted=_gather_halves_ici(next_blocks))
    a = _conv_gate_fwd(u, conv_w, conv_b, f"ffn_gate_{tag}")
    if next_blocks is None:
        x_new, landed = _mm(a, w_down, name=f"ffn_down_{tag}", epi=_add_epi, epi_in=(x,)), None
    else:
        x_new, over_d2d = _mm(a, w_down, name=f"ffn_down_{tag}", epi=_add_epi, epi_in=(x,),
                              hosted=_gather_halves_d2d(over_ici))
        landed = (over_ici, over_d2d)
    return x_new, (x, hn, u, a), landed


def _ffn_bwd(dres, saved, norm_g, w_up, conv_w, conv_b, w_down, tag, reducer=None):
    x, hn, u, a = saved
    dres, dres_b = dres
    if reducer is None:
        da = _mm(dres_b, w_down, tb=True, out_dtype=BF16, name=f"ffn_da_{tag}")
    else:
        da, landed = _mm(dres_b, w_down, tb=True, out_dtype=BF16, name=f"ffn_da_{tag}", hosted=reducer.sibling())
    dw_down = _mm(a, dres_b, ta=True, name=f"ffn_dwdown_{tag}", pieces='row')
    dc, dconv_w, dconv_b = _conv_gate_bwd(u, da, conv_w, conv_b, f"ffn_gate_bwd_{tag}")
    du = _conv_transpose(dc, conv_w, f"ffn_convt_{tag}")
    if reducer is None:
        dhn = _mm(du, w_up, tb=True, name=f"ffn_dhn_{tag}")
        dw_up = _mm(hn, du, ta=True, name=f"ffn_dwup_{tag}", pieces='col')
        reduced = None
    else:
        dhn, landed = _mm(du, w_up, tb=True, name=f"ffn_dhn_{tag}", hosted=reducer.chips(landed))
        dw_up, landed = _mm(hn, du, ta=True, name=f"ffn_dwup_{tag}", pieces='col', hosted=reducer.swap(landed))
        reduced = reducer.finish(landed)
    dres, dg = _rmsnorm_bwd(x, norm_g, dhn, dres, f"ffn_norm_bwd_{tag}")
    return dres, (dg, dw_up, dconv_w, dconv_b, dw_down), reduced


ATTN_KEYS = 3 * ATTN_BLOCK


def _attn_window(i, S):
    ks = pl.multiple_of(jnp.clip((i - 1) * ATTN_BLOCK, 0, S - ATTN_KEYS), ATTN_BLOCK)
    qpos = i * ATTN_BLOCK + lax.broadcasted_iota(jnp.int32, (ATTN_BLOCK, ATTN_KEYS), 0)
    kpos = ks + lax.broadcasted_iota(jnp.int32, (ATTN_BLOCK, ATTN_KEYS), 1)
    arel = jnp.abs(kpos - qpos)
    return ks, arel.astype(F32), arel <= ATTN_BLOCK


def _attn_probs(q, k, slope, sink, arel, valid):
    s = lax.dot_general(q, k, (((1,), (1,)), ((), ())), preferred_element_type=F32) * (ATTN_DH ** -0.5)
    s = jnp.where(valid, s - slope * arel, -jnp.inf)
    m = jnp.maximum(jnp.max(s, axis=-1, keepdims=True), sink)
    p = jnp.exp(s - m)
    es = jnp.exp(sink - m)
    inv = 1.0 / (jnp.sum(p, axis=-1, keepdims=True) + es)
    return p * inv, es * inv


def _attn_specs(S, D):
    H = D // ATTN_DH
    KVW = (H // ATTN_GROUP) * ATTN_DH
    q_spec = pl.BlockSpec((ATTN_BLOCK, D), lambda i: (i, 0))
    k_spec = pl.BlockSpec((S, KVW), lambda i: (0, D // KVW))
    v_spec = pl.BlockSpec((S, KVW), lambda i: (0, D // KVW + 1))
    return H, KVW, q_spec, k_spec, v_spec


def _attn_fwd(qkv, sink, name):
    S = qkv.shape[0]
    D = qkv.shape[1] * ATTN_GROUP // (ATTN_GROUP + 2)
    H, KVW, q_spec, k_spec, v_spec = _attn_specs(S, D)

    def body(q_ref, k_ref, v_ref, sink_ref, o_ref):
        ks, arel, valid = _attn_window(pl.program_id(0), S)
        for kvh in range(H // ATTN_GROUP):
            cols = slice(kvh * ATTN_DH, (kvh + 1) * ATTN_DH)
            k = k_ref[pl.ds(ks, ATTN_KEYS), cols]
            v = v_ref[pl.ds(ks, ATTN_KEYS), cols]
            for g in range(ATTN_GROUP):
                h = kvh * ATTN_GROUP + g
                hc = slice(h * ATTN_DH, (h + 1) * ATTN_DH)
                p, _ = _attn_probs(q_ref[:, hc], k, 2.0 ** (-8.0 * (h + 1) / H), sink_ref[h], arel, valid)
                o_ref[:, hc] = jnp.dot(p.astype(BF16), v, preferred_element_type=F32).astype(o_ref.dtype)

    return pl.pallas_call(
        body, name=name, grid=(S // ATTN_BLOCK,),
        in_specs=[q_spec, k_spec, v_spec, pl.BlockSpec(memory_space=pltpu.SMEM)],
        out_specs=q_spec, out_shape=jax.ShapeDtypeStruct((S, D), BF16),
        compiler_params=_cp(("parallel",)),
    )(qkv, qkv, qkv, sink)


def _attn_bwd(qkv, sink, do, name):
    S = qkv.shape[0]
    D = qkv.shape[1] * ATTN_GROUP // (ATTN_GROUP + 2)
    H, KVW, q_spec, k_spec, v_spec = _attn_specs(S, D)
    scale = ATTN_DH ** -0.5

    def body(q_ref, k_ref, v_ref, sink_ref, do_ref, dq_ref, dk_ref, dv_ref, ds_ref):
        @pl.when(pl.program_id(0) == 0)
        def _():
            dk_ref[...] = jnp.zeros_like(dk_ref)
            dv_ref[...] = jnp.zeros_like(dv_ref)
            ds_ref[...] = jnp.zeros_like(ds_ref)

        ks, arel, valid = _attn_window(pl.program_id(0), S)
        rows = pl.ds(ks, ATTN_KEYS)
        for kvh in range(H // ATTN_GROUP):
            cols = slice(kvh * ATTN_DH, (kvh + 1) * ATTN_DH)
            k = k_ref[rows, cols]
            v = v_ref[rows, cols]
            dk = jnp.zeros((ATTN_KEYS, ATTN_DH), F32)
            dv = jnp.zeros((ATTN_KEYS, ATTN_DH), F32)
            for g in range(ATTN_GROUP):
                h = kvh * ATTN_GROUP + g
                hc = slice(h * ATTN_DH, (h + 1) * ATTN_DH)
                q = q_ref[:, hc]
                d_o = do_ref[:, hc]
                p, p_sink = _attn_probs(q, k, 2.0 ** (-8.0 * (h + 1) / H), sink_ref[h], arel, valid)
                dp = lax.dot_general(d_o, v, (((1,), (1,)), ((), ())), preferred_element_type=F32)
                delta = jnp.sum(p * dp, axis=-1, keepdims=True)
                dsc = (p * (dp - delta)).astype(BF16)
                ds_ref[:, h:h + 1] += -p_sink * delta
                dq_ref[:, hc] = (jnp.dot(dsc, k, preferred_element_type=F32) * scale).astype(dq_ref.dtype)
                dk += lax.dot_general(dsc, q, (((0,), (0,)), ((), ())), preferred_element_type=F32)
                dv += lax.dot_general(p.astype(BF16), d_o, (((0,), (0,)), ((), ())), preferred_element_type=F32)
            dk_ref[rows, cols] += dk * scale
            dv_ref[rows, cols] += dv

    kv_out = pl.BlockSpec((S, KVW), lambda i: (0, 0))
    return pl.pallas_call(
        body, name=name, grid=(S // ATTN_BLOCK,),
        in_specs=[q_spec, k_spec, v_spec, pl.BlockSpec(memory_space=pltpu.SMEM), q_spec],
        out_specs=[q_spec, kv_out, kv_out, pl.BlockSpec((ATTN_BLOCK, H), lambda i: (0, 0))],
        out_shape=[jax.ShapeDtypeStruct((S, D), BF16), jax.ShapeDtypeStruct((S, KVW), F32),
                   jax.ShapeDtypeStruct((S, KVW), F32), jax.ShapeDtypeStruct((ATTN_BLOCK, H), F32)],
        compiler_params=_cp(("arbitrary",)),
    )(qkv, qkv, qkv, sink, do)


def _attn_layer_fwd(x, norm_g, w_qkv, w_o, sink, tag):
    hn = _rmsnorm_fwd(x, norm_g, BF16, f"attn_norm_{tag}")
    qkv = _mm(hn, w_qkv, out_dtype=BF16, name=f"attn_qkv_{tag}")
    o = _attn_fwd(qkv, sink, f"attn_core_{tag}")
    x_new = _mm(o, w_o, name=f"attn_out_{tag}", epi=_add_epi, epi_in=(x,))
    return x_new, (x, hn, qkv, o)


def _attn_layer_bwd(dres, saved, norm_g, w_qkv, w_o, sink, tag):
    x, hn, qkv, o = saved
    dres, dres_b = dres
    do = _mm(dres_b, w_o, tb=True, out_dtype=BF16, name=f"attn_do_{tag}")
    dw_o = _mm(o, dres_b, ta=True, name=f"attn_dwo_{tag}", pieces='row')
    dq, dk, dv, dsink = _attn_bwd(qkv, sink, do, f"attn_core_bwd_{tag}")
    dqkv = jnp.concatenate([dq, dk.astype(BF16), dv.astype(BF16)], axis=1)
    dhn = _mm(dqkv, w_qkv, tb=True, name=f"attn_dhn_{tag}")
    dw_qkv = _mm(hn, dqkv, ta=True, name=f"attn_dwqkv_{tag}", pieces='col')
    dres, dg = _rmsnorm_bwd(x, norm_g, dhn, dres, f"attn_norm_bwd_{tag}")
    return dres, (dg, dw_qkv, dw_o, jnp.sum(dsink, axis=0))


MLA_W = 2 * LANES
MLA_DPAD = 2 * MLA_LORA + LANES
MLA_SCALE = (MLA_NOPE + MLA_ROPE) ** -0.5
MLA_TILES = (1024, 512, 256, 128)
LOG2E = math.log2(math.e)
LN2 = math.log(2.0)


def _rope_tables(S):
    half = MLA_ROPE // 2
    pos = jnp.arange(S, dtype=F32)
    inv = ROPE_THETA ** (-jnp.arange(half, dtype=F32) / half)
    ang = pos[:, None] * inv[None, :]
    cos, sin = jnp.cos(ang), jnp.sin(ang)
    z = jnp.zeros((S, LANES - 2 * half), F32)
    zh = jnp.zeros((S, half), F32)
    return (jnp.concatenate([cos, cos, z], axis=1), jnp.concatenate([-sin, zh, z], axis=1),
            jnp.concatenate([zh, sin, z], axis=1))


def _rope(t, ca, sb, sc):
    return t * ca + pltpu.roll(t, 96, 1) * sb + pltpu.roll(t, 32, 1) * sc


def _rope_t(d, ca, sb, sc):
    return d * ca + pltpu.roll(d * sb, 32, 1) + pltpu.roll(d * sc, 96, 1)


def _rms(xf, g):
    return (xf * lax.rsqrt(jnp.mean(xf * xf, axis=-1, keepdims=True) + RMS_EPS)) * g


def _mla_prep(d, qn, kvn, tabs, name):
    S = d.shape[0]
    tr = _pick(S, (256, 128, 8))
    L = MLA_LORA

    def body(d_ref, qn_ref, kvn_ref, ca, sb, sc, cq_ref, ckv_ref, kr_ref):
        cq_ref[...] = _rms(d_ref[:, :L], qn_ref[...]).astype(BF16)
        ckv_ref[...] = _rms(d_ref[:, L:2 * L], kvn_ref[...]).astype(BF16)
        kr_ref[...] = _rope(d_ref[:, 2 * L:], ca[...], sb[...], sc[...]).astype(BF16)

    row = lambda w: pl.BlockSpec((tr, w), lambda i: (i, 0))
    vec = pl.BlockSpec((1, L), lambda i: (0, 0))
    return pl.pallas_call(
        body, name=name, grid=(S // tr,),
        in_specs=[row(MLA_DPAD), vec, vec, row(LANES), row(LANES), row(LANES)],
        out_specs=[row(L), row(L), row(LANES)],
        out_shape=[jax.ShapeDtypeStruct((S, L), BF16), jax.ShapeDtypeStruct((S, L), BF16),
                   jax.ShapeDtypeStruct((S, LANES), BF16)],
        compiler_params=_cp(("parallel",)),
    )(d, qn, kvn, *tabs)


def _mla_prep_bwd(d, qn, kvn, tabs, dcq, dckv, dkr_h, name):
    S = d.shape[0]
    H = dkr_h.shape[0]
    tr = _pick(S, (256, 128, 8))
    L = MLA_LORA

    def body(d_ref, qn_ref, kvn_ref, ca, sb, sc, dcq_ref, dckv_ref, dkr_ref, dd_ref, dqn_ref, dkvn_ref):
        @pl.when(pl.program_id(0) == 0)
        def _():
            dqn_ref[...] = jnp.zeros_like(dqn_ref)
            dkvn_ref[...] = jnp.zeros_like(dkvn_ref)

        dx, dg = _rms_bwd_math(d_ref[:, :L], qn_ref[...], dcq_ref[...])
        dd_ref[:, :L] = dx.astype(BF16)
        dqn_ref[...] += dg
        dx, dg = _rms_bwd_math(d_ref[:, L:2 * L], kvn_ref[...], dckv_ref[...])
        dd_ref[:, L:2 * L] = dx.astype(BF16)
        dkvn_ref[...] += dg
        dkr = dkr_ref[0]
        for h in range(1, H):
            dkr = dkr + dkr_ref[h]
        dd_ref[:, 2 * L:] = _rope_t(dkr, ca[...], sb[...], sc[...]).astype(BF16)

    row = lambda w: pl.BlockSpec((tr, w), lambda i: (i, 0))
    vec = pl.BlockSpec((1, L), lambda i: (0, 0))
    return pl.pallas_call(
        body, name=name, grid=(S // tr,),
        in_specs=[row(MLA_DPAD), vec, vec, row(LANES), row(LANES), row(LANES), row(L), row(L),
                  pl.BlockSpec((H, tr, LANES), lambda i: (0, i, 0))],
        out_specs=[row(MLA_DPAD), vec, vec],
        out_shape=[jax.ShapeDtypeStruct((S, MLA_DPAD), BF16), jax.ShapeDtypeStruct((1, L), F32),
                   jax.ShapeDtypeStruct((1, L), F32)],
        compiler_params=_cp(("arbitrary",)),
    )(d, qn, kvn, *tabs, dcq, dckv, dkr_h)


def _heads_proj(a, w, out_dtype, name):
    S, K = a.shape
    H, _, n = w.shape
    tm = _pick(S, (1024, 512, 256, 128))
    return _gmm(a, w, grid=(S // tm, H, 1),
                a_spec=pl.BlockSpec((tm, K), lambda m, h, k: (m, 0)),
                b_spec=pl.BlockSpec((None, K, n), lambda m, h, k: (h, 0, 0)),
                o_spec=pl.BlockSpec((None, tm, n), lambda m, h, k: (h, m, 0)),
                out_shape=jax.ShapeDtypeStruct((H, S, n), out_dtype),
                dims=(((1,), (0,)), ((), ())), kax=2, acc_shape=(tm, n), name=name)


def _heads_proj_dx(dy, w, name):
    H, S, n = dy.shape
    K = w.shape[1]
    tm = _pick(S, (1024, 512, 256, 128))
    return _gmm(dy, w, grid=(S // tm, 1, H),
                a_spec=pl.BlockSpec((None, tm, n), lambda m, j, h: (h, m, 0)),
                b_spec=pl.BlockSpec((None, K, n), lambda m, j, h: (h, 0, 0)),
                o_spec=pl.BlockSpec((tm, K), lambda m, j, h: (m, 0)),
                out_shape=jax.ShapeDtypeStruct((S, K), F32),
                dims=(((1,), (1,)), ((), ())), kax=2, acc_shape=(tm, K), name=name)


def _heads_proj_dw(a, dy, name):
    S, K = a.shape
    H, _, n = dy.shape
    tk = _pick(S, (512, 256, 128))
    return _gmm(a, dy, grid=(H, 1, S // tk),
                a_spec=pl.BlockSpec((tk, K), lambda h, j, k: (k, 0)),
                b_spec=pl.BlockSpec((None, tk, n), lambda h, j, k: (h, k, 0)),
                o_spec=pl.BlockSpec((None, K, n), lambda h, j, k: (h, 0, 0)),
                out_shape=jax.ShapeDtypeStruct((H, K, n), F32),
                dims=(((0,), (0,)), ((), ())), kax=2, acc_shape=(K, n), name=name)


def _mla_rope_q(q_ext, tabs, bwd, name):
    H, S, _ = q_ext.shape
    tr = _pick(S, (512, 256, 128, 8))
    mult = 1.0 if bwd else MLA_SCALE * LOG2E

    def body(q_ref, ca, sb, sc, o_ref):
        o_ref[:, :LANES] = (q_ref[:, :LANES].astype(F32) * mult).astype(BF16)
        fn = _rope_t if bwd else _rope
        o_ref[:, LANES:] = (fn(q_ref[:, LANES:].astype(F32), ca[...], sb[...], sc[...]) * mult).astype(BF16)

    blk = pl.BlockSpec((None, tr, MLA_W), lambda i, h: (h, i, 0))
    tab = pl.BlockSpec((tr, LANES), lambda i, h: (i, 0))
    return pl.pallas_call(
        body, name=name, grid=(S // tr, H), in_specs=[blk, tab, tab, tab], out_specs=blk,
        out_shape=jax.ShapeDtypeStruct((H, S, MLA_W), BF16), compiler_params=_cp(("parallel", "parallel")),
    )(q_ext, *tabs)


def _col_to_row(col):
    n = col.shape[0]
    eye = lax.broadcasted_iota(jnp.int32, (n, n), 0) == lax.broadcasted_iota(jnp.int32, (n, n), 1)
    return jnp.sum(jnp.where(eye, col, 0.0), axis=0, keepdims=True)


def _mla_flash_fwd(q, kv, kr, name, tq=None, tk=None, unroll=1):
    H, S, _ = q.shape
    tq = tq or _pick(S, MLA_TILES)
    tk = tk or _pick(S, MLA_TILES)

    def body(q_ref, kv_ref, kr_ref, o_ref, lse_ref, kbuf, vbuf):
        @pl.when(pl.program_id(1) == 0)
        def _():
            kbuf[:, :LANES] = kv_ref[:, :LANES]
            kbuf[:, LANES:] = kr_ref[...]
            vbuf[:, :LANES] = kv_ref[:, LANES:]
            vbuf[:, LANES:] = jnp.ones((S, LANES), BF16)

        qv = q_ref[...]

        def step(c, carry):
            m, acc = carry
            rows = pl.ds(pl.multiple_of(c * tk, tk), tk)
            s = lax.dot_general(qv, kbuf[rows, :], (((1,), (1,)), ((), ())), preferred_element_type=F32)
            m_new = jnp.maximum(m, jnp.max(s, axis=-1, keepdims=True))
            p = jnp.exp2(s - m_new).astype(BF16)
            acc = jnp.exp2(m - m_new) * acc + jnp.dot(p, vbuf[rows, :], preferred_element_type=F32)
            return m_new, acc

        init = (jnp.full((tq, 1), -jnp.inf, F32), jnp.zeros((tq, MLA_W), F32))
        m, acc = lax.fori_loop(0, S // tk, step, init, unroll=unroll)
        l = acc[:, LANES:LANES + 1]
        o_ref[...] = (acc[:, :LANES] / l).astype(o_ref.dtype)
        lse_ref[...] = _col_to_row(m + jnp.log2(l))

    return pl.pallas_call(
        body, name=name, grid=(H, S // tq),
        in_specs=[pl.BlockSpec((None, tq, MLA_W), lambda h, i: (h, i, 0)),
                  pl.BlockSpec((None, S, MLA_W), lambda h, i: (h, 0, 0)),
                  pl.BlockSpec((S, LANES), lambda h, i: (0, 0))],
        out_specs=[pl.BlockSpec((tq, MLA_V), lambda h, i: (i, h)),
                   pl.BlockSpec((None, 1, tq), lambda h, i: (h, 0, i))],
        out_shape=[jax.ShapeDtypeStruct((S, H * MLA_V), BF16), jax.ShapeDtypeStruct((H, 1, S), F32)],
        scratch_shapes=[pltpu.VMEM((S, MLA_W), BF16), pltpu.VMEM((S, MLA_W), BF16)],
        compiler_params=_cp(("parallel", "arbitrary")),
    )(q, kv, kr)


def _mla_delta(o, do, H, name):
    S = o.shape[0]
    tq = _pick(S, (512, 256, 128))

    def body(o_ref, do_ref, d_ref):
        prod = o_ref[...].astype(F32) * do_ref[...].astype(F32)
        d_ref[...] = _col_to_row(jnp.sum(prod, axis=-1, keepdims=True))

    blk = pl.BlockSpec((tq, MLA_V), lambda i, h: (i, h))
    return pl.pallas_call(
        body, name=name, grid=(S // tq, H), in_specs=[blk, blk],
        out_specs=pl.BlockSpec((None, 1, tq), lambda i, h: (h, 0, i)),
        out_shape=jax.ShapeDtypeStruct((H, 1, S), F32), compiler_params=_cp(("parallel", "parallel")),
    )(o, do)


def _mla_flash_bwd(q, kv, kr, do, lse, delta, name, tq=None, tkv=None, unroll=1):
    H, S, _ = q.shape
    tq = tq or _pick(S, MLA_TILES)
    tkv = tkv or _pick(S, MLA_TILES)

    def body(q_ref, kv_ref, kr_ref, do_ref, lse_ref, dl_ref, dq_ref, dkv_ref, dkr_ref):
        @pl.when(pl.program_id(1) == 0)
        def _():
            dq_ref[...] = jnp.zeros_like(dq_ref)

        v = kv_ref[:, LANES:]
        k = jnp.concatenate([kv_ref[:, :LANES], kr_ref[...]], axis=1)

        def step(c, carry):
            dk, dv = carry
            start = pl.multiple_of(c * tq, tq)
            rows = pl.ds(start, tq)
            qv = q_ref[rows, :]
            d_o = do_ref[rows, :]
            s_t = lax.dot_general(k, qv, (((1,), (1,)), ((), ())), preferred_element_type=F32)
            p_t = jnp.exp2(s_t - lse_ref[:, rows])
            dv = dv + jnp.dot(p_t.astype(BF16), d_o, preferred_element_type=F32)
            dp_t = lax.dot_general(v, d_o, (((1,), (1,)), ((), ())), preferred_element_type=F32)
            ds_t = (p_t * (dp_t - dl_ref[:, rows])).astype(BF16)
            dk = dk + jnp.dot(ds_t, qv, preferred_element_type=F32)
            dq_ref[rows, :] += lax.dot_general(ds_t, k, (((0,), (0,)), ((), ())),
                                               preferred_element_type=F32) * MLA_SCALE
            return dk, dv

        dk, dv = lax.fori_loop(0, S // tq, step, (jnp.zeros((tkv, MLA_W), F32), jnp.zeros((tkv, MLA_V), F32)),
                               unroll=unroll)
        dkv_ref[:, :LANES] = (dk[:, :LANES] * LN2).astype(BF16)
        dkv_ref[:, LANES:] = dv.astype(BF16)
        dkr_ref[...] = dk[:, LANES:] * LN2

    stat = pl.BlockSpec((None, 1, S), lambda h, j: (h, 0, 0))
    return pl.pallas_call(
        body, name=name, grid=(H, S // tkv),
        in_specs=[pl.BlockSpec((None, S, MLA_W), lambda h, j: (h, 0, 0)),
                  pl.BlockSpec((None, tkv, MLA_W), lambda h, j: (h, j, 0)),
                  pl.BlockSpec((tkv, LANES), lambda h, j: (j, 0)),
                  pl.BlockSpec((S, MLA_V), lambda h, j: (0, h)), stat, stat],
        out_specs=[pl.BlockSpec((None, S, MLA_W), lambda h, j: (h, 0, 0)),
                   pl.BlockSpec((None, tkv, MLA_W), lambda h, j: (h, j, 0)),
                   pl.BlockSpec((None, tkv, LANES), lambda h, j: (h, j, 0))],
        out_shape=[jax.ShapeDtypeStruct((H, S, MLA_W), F32), jax.ShapeDtypeStruct((H, S, MLA_W), BF16),
                   jax.ShapeDtypeStruct((H, S, LANES), F32)],
        compiler_params=_cp(("parallel", "arbitrary")),
    )(q, kv, kr, do, lse, delta)


def _mla_weights(w_dqkv, w_uq, w_ukv):
    H = MLA_HEADS
    wd = jnp.pad(w_dqkv, ((0, 0), (0, MLA_DPAD - w_dqkv.shape[1])))
    wq = w_uq.reshape(MLA_LORA, H, MLA_NOPE + MLA_ROPE)
    wq = jnp.pad(wq, ((0, 0), (0, 0), (0, MLA_W - wq.shape[2]))).transpose(1, 0, 2)
    wkv = w_ukv.reshape(MLA_LORA, H, MLA_NOPE + MLA_V).transpose(1, 0, 2)
    return wd, wq, wkv


def _mla_layer_fwd(x, norm_g, wd, wq, wkv, w_o, qn, kvn, tag):
    S = x.shape[0]
    tabs = _rope_tables(S)
    hn = _rmsnorm_fwd(x, norm_g, BF16, f"mla_norm_{tag}")
    d = _mm(hn, wd, name=f"mla_down_{tag}")
    cq, ckv, kr = _mla_prep(d, qn, kvn, tabs, f"mla_prep_{tag}")
    q = _mla_rope_q(_heads_proj(cq, wq, F32, f"mla_uq_{tag}"), tabs, False, f"mla_ropeq_{tag}")
    kv = _heads_proj(ckv, wkv, BF16, f"mla_ukv_{tag}")
    o, lse = _mla_flash_fwd(q, kv, kr, f"mla_flash_{tag}")
    x_new = _mm(o, w_o, name=f"mla_out_{tag}", epi=_add_epi, epi_in=(x,))
    return x_new, (x, hn, d, cq, ckv, kr, q, kv, o, lse)


def _mla_layer_bwd(dres, saved, norm_g, wd, wq, wkv, w_o, qn, kvn, tag):
    x, hn, d, cq, ckv, kr, q, kv, o, lse = saved
    S = x.shape[0]
    H = MLA_HEADS
    tabs = _rope_tables(S)
    dres, dres_b = dres
    do = _mm(dres_b, w_o, tb=True, out_dtype=BF16, name=f"mla_do_{tag}")
    dw_o = _mm(o, dres_b, ta=True, name=f"mla_dwo_{tag}", pieces='row')
    delta = _mla_delta(o, do, H, f"mla_delta_{tag}")
    dq, dkv, dkr_h = _mla_flash_bwd(q, kv, kr, do, lse, delta, f"mla_flash_bwd_{tag}")
    dq_ext = _mla_rope_q(dq, tabs, True, f"mla_ropeq_bwd_{tag}")
    dwq = _heads_proj_dw(cq, dq_ext, f"mla_dwuq_{tag}")
    dcq = _heads_proj_dx(dq_ext, wq, f"mla_dcq_{tag}")
    dwkv = _heads_proj_dw(ckv, dkv, f"mla_dwukv_{tag}")
    dckv = _heads_proj_dx(dkv, wkv, f"mla_dckv_{tag}")
    dd, dqn, dkvn = _mla_prep_bwd(d, qn, kvn, tabs, dcq, dckv, dkr_h, f"mla_prep_bwd_{tag}")
    dhn = _mm(dd, wd, tb=True, name=f"mla_dhn_{tag}")
    dwd = _mm(hn, dd, ta=True, name=f"mla_dwd_{tag}")
    dres, dg = _rmsnorm_bwd(x, norm_g, dhn, dres, f"mla_norm_bwd_{tag}")
    dw_dqkv = dwd[:, :2 * MLA_LORA + MLA_ROPE]
    dw_uq = dwq.transpose(1, 0, 2)[:, :, :MLA_NOPE + MLA_ROPE].reshape(MLA_LORA, -1)
    dw_ukv = dwkv.transpose(1, 0, 2).reshape(MLA_LORA, -1)
    return dres, (dg, dw_dqkv, dqn, dkvn, dw_uq, dw_ukv, dw_o)


S5_CB = LANES
S5_SB = (S5_CB // SSM_GROUP_CH) * SSM_STATE
S5_ROWS = 1024


def _s5_disc(a_re, a_im, ls, b_re, b_im):
    step = jnp.exp(ls)
    mag = jnp.exp(step * a_re)
    lb_re = mag * jnp.cos(step * a_im)
    lb_im = mag * jnp.sin(step * a_im)
    n_re, n_im = lb_re - 1.0, lb_im
    den = a_re * a_re + a_im * a_im
    coef_re = (n_re * a_re + n_im * a_im) / den
    coef_im = (n_im * a_re - n_re * a_im) / den
    return lb_re, lb_im, coef_re * b_re - coef_im * b_im, coef_re * b_im + coef_im * b_re


def _s5_disc_fwd(a_re, a_im, ls, b_re, b_im, name):
    GN = a_re.shape[-1]

    def body(ar, ai, l, br, bi, o_lr, o_li, o_br, o_bi):
        for o, v in zip((o_lr, o_li, o_br, o_bi), _s5_disc(ar[...], ai[...], l[...], br[...], bi[...])):
            o[...] = v

    vec = pl.BlockSpec((None, 1, GN), lambda d: (d, 0, 0))
    mat = pl.BlockSpec((None, SSM_GROUP_CH, GN), lambda d: (d, 0, 0))
    sv = jax.ShapeDtypeStruct(a_re.shape, F32)
    sm = jax.ShapeDtypeStruct(b_re.shape, F32)
    return pl.pallas_call(body, name=name, grid=(2,), in_specs=[vec, vec, vec, mat, mat],
                          out_specs=[vec, vec, mat, mat], out_shape=[sv, sv, sm, sm],
                          compiler_params=_cp(("parallel",)))(a_re, a_im, ls, b_re, b_im)


def _s5_disc_bwd(a_re, a_im, ls, b_re, b_im, d_lr, d_li, d_br, d_bi, name):
    GN = a_re.shape[-1]

    def body(ar, ai, l, br, bi, g_lr, g_li, g_br, g_bi, o_ar, o_ai, o_l, o_br, o_bi):
        _, vjp = jax.vjp(_s5_disc, ar[...], ai[...], l[...], br[...], bi[...])
        for o, v in zip((o_ar, o_ai, o_l, o_br, o_bi), vjp((g_lr[...], g_li[...], g_br[...], g_bi[...]))):
            o[...] = v

    vec = pl.BlockSpec((None, 1, GN), lambda d: (d, 0, 0))
    mat = pl.BlockSpec((None, SSM_GROUP_CH, GN), lambda d: (d, 0, 0))
    sv = jax.ShapeDtypeStruct(a_re.shape, F32)
    sm = jax.ShapeDtypeStruct(b_re.shape, F32)
    return pl.pallas_call(body, name=name, grid=(2,), in_specs=[vec, vec, vec, mat, mat, vec, vec, mat, mat],
                          out_specs=[vec, vec, vec, mat, mat], out_shape=[sv, sv, sv, sm, sm],
                          compiler_params=_cp(("parallel",)))(a_re, a_im, ls, b_re, b_im, d_lr, d_li, d_br, d_bi)


def _cmul(ar, ai, br, bi):
    return ar * br - ai * bi, ar * bi + ai * br


def _segment_carries(lr, li, er, ei, n_steps, reverse):
    pr, pi = lr, li
    for _ in range(int(math.log2(n_steps))):
        pr, pi = _cmul(pr, pi, pr, pi)
    row = lax.broadcasted_iota(jnp.int32, er.shape, 0)
    edge = (SSM_SEGMENTS - 1) if reverse else 0
    shift = (SSM_SEGMENTS - 1) if reverse else 1
    cr = jnp.zeros_like(er)
    ci = jnp.zeros_like(ei)
    for _ in range(SSM_SEGMENTS - 1):
        tr_, ti_ = _cmul(pr, pi, cr, ci)
        cr = jnp.where(row == edge, 0.0, pltpu.roll(tr_ + er, shift, 0))
        ci = jnp.where(row == edge, 0.0, pltpu.roll(ti_ + ei, shift, 0))
    return cr, ci


def _s5_geometry(S, D):
    assert S % SSM_SEGMENTS == 0 and D % S5_CB == 0
    n_steps = S // SSM_SEGMENTS
    assert n_steps & (n_steps - 1) == 0, "segment length must be a power of two"
    rows = min(S5_ROWS, S)
    return n_steps, rows, S // rows, D // S5_CB


def _s5_scan(u, b_re, b_im, c_re, c_im, lam_re, lam_im, ends, descending, name):
    S, D = u.shape
    n_steps, rows, nch, ncb = _s5_geometry(S, D)
    full = ends is not None
    GN = ncb * S5_SB

    def body(*refs):
        if full:
            (u_ref, br_ref, bi_ref, cr_ref, ci_ref, lr_ref, li_ref, er_ref, ei_ref,
             xr_ref, xi_ref, y_ref, st_r, st_i, buf_r, buf_i) = refs
        else:
            u_ref, br_ref, bi_ref, lr_ref, li_ref, er_ref, ei_ref, st_r, st_i, buf_r, buf_i = refs
        lr = jnp.broadcast_to(lr_ref[...], (SSM_SEGMENTS, S5_SB))
        li = jnp.broadcast_to(li_ref[...], (SSM_SEGMENTS, S5_SB))

        @pl.when(pl.program_id(1) == 0)
        def _():
            if full:
                st_r[...], st_i[...] = _segment_carries(lr, li, er_ref[...], ei_ref[...], n_steps, descending)
            else:
                st_r[...] = jnp.zeros_like(st_r)
                st_i[...] = jnp.zeros_like(st_i)

        ub = u_ref[...].astype(BF16)
        buf_r[...] = jnp.dot(ub, br_ref[...], preferred_element_type=F32)
        buf_i[...] = jnp.dot(ub, bi_ref[...], preferred_element_type=F32)

        n_it = rows // SSM_SEGMENTS

        def step(i, carry):
            sr, si = carry
            i = n_it - 1 - i if descending else i
            r = pl.ds(pl.multiple_of(i * SSM_SEGMENTS, SSM_SEGMENTS), SSM_SEGMENTS)
            if full:
                xr_ref[r, :] = sr
                xi_ref[r, :] = si
            nr = lr * sr - li * si + buf_r[r, :]
            ni = lr * si + li * sr + buf_i[r, :]
            if full:
                buf_r[r, :] = nr
                buf_i[r, :] = ni
            return nr, ni

        sr, si = lax.fori_loop(0, n_it, step, (st_r[...], st_i[...]))
        st_r[...] = sr
        st_i[...] = si
        if full:
            y_ref[...] = (jnp.dot(buf_r[...].astype(BF16), cr_ref[...], preferred_element_type=F32)
                          - jnp.dot(buf_i[...].astype(BF16), ci_ref[...], preferred_element_type=F32))
        else:
            er_ref[...] = sr
            ei_ref[...] = si

    chunk = (lambda c: nch - 1 - c) if descending else (lambda c: c)
    u_spec = pl.BlockSpec((rows, S5_CB), lambda b, c: (chunk(c), b))
    bmat = pl.BlockSpec((None, S5_CB, S5_SB), lambda b, c: (b, 0, 0))
    cmat = pl.BlockSpec((None, S5_SB, S5_CB), lambda b, c: (b, 0, 0))
    lvec = pl.BlockSpec((1, S5_SB), lambda b, c: (0, b))
    evec = pl.BlockSpec((SSM_SEGMENTS, S5_SB), lambda b, c: (0, b))
    xblk = pl.BlockSpec((rows, S5_SB), lambda b, c: (chunk(c), b))
    scratch = [pltpu.VMEM((SSM_SEGMENTS, S5_SB), F32)] * 2 + [pltpu.VMEM((rows, S5_SB), F32)] * 2
    e_shape = jax.ShapeDtypeStruct((SSM_SEGMENTS, GN), F32)
    if full:
        x_shape = jax.ShapeDtypeStruct((S, GN), F32)
        return pl.pallas_call(
            body, name=name, grid=(ncb, nch),
            in_specs=[u_spec, bmat, bmat, cmat, cmat, lvec, lvec, evec, evec],
            out_specs=[xblk, xblk, u_spec], out_shape=[x_shape, x_shape, jax.ShapeDtypeStruct((S, D), F32)],
            scratch_shapes=scratch, compiler_params=_cp(("parallel", "arbitrary")),
        )(u, b_re, b_im, c_re, c_im, lam_re, lam_im, *ends)
    return pl.pallas_call(
        body, name=name, grid=(ncb, nch), in_specs=[u_spec, bmat, bmat, lvec, lvec],
        out_specs=[evec, evec], out_shape=[e_shape, e_shape],
        scratch_shapes=scratch, compiler_params=_cp(("parallel", "arbitrary")),
    )(u, b_re, b_im, lam_re, lam_im)


def _s5_scan_bwd(dy, u, xp, b_re, b_im, c_re, c_im, lam_re, lam_im, starts, descending, name):
    S, D = dy.shape
    n_steps, rows, nch, ncb = _s5_geometry(S, D)
    full = starts is not None
    GN = ncb * S5_SB
    nt = (((1,), (1,)), ((), ()))
    tn = (((0,), (0,)), ((), ()))

    def body(*refs):
        if full:
            (dy_ref, u_ref, xr_ref, xi_ref, br_ref, bi_ref, cr_ref, ci_ref, lr_ref, li_ref, gr_ref, gi_ref,
             du_ref, dbr_ref, dbi_ref, dcr_ref, dci_ref, dlr_ref, dli_ref, st_r, st_i, buf_r, buf_i) = refs
        else:
            dy_ref, cr_ref, ci_ref, lr_ref, li_ref, gr_ref, gi_ref, st_r, st_i, buf_r, buf_i = refs
        lr = jnp.broadcast_to(lr_ref[...], (SSM_SEGMENTS, S5_SB))
        li = jnp.broadcast_to(li_ref[...], (SSM_SEGMENTS, S5_SB))

        @pl.when(pl.program_id(1) == 0)
        def _():
            if full:
                st_r[...], st_i[...] = _segment_carries(lr, -li, gr_ref[...], gi_ref[...], n_steps, descending)
                for r in (dbr_ref, dbi_ref, dcr_ref, dci_ref, dlr_ref, dli_ref):
                    r[...] = jnp.zeros_like(r)
            else:
                st_r[...] = jnp.zeros_like(st_r)
                st_i[...] = jnp.zeros_like(st_i)

        dyb = dy_ref[...].astype(BF16)
        buf_r[...] = lax.dot_general(dyb, cr_ref[...], nt, preferred_element_type=F32)
        buf_i[...] = -lax.dot_general(dyb, ci_ref[...], nt, preferred_element_type=F32)
        n_it = rows // SSM_SEGMENTS

        def step(j, carry):
            gr, gi = carry
            j = n_it - 1 - j if descending else j
            r = pl.ds(pl.multiple_of(j * SSM_SEGMENTS, SSM_SEGMENTS), SSM_SEGMENTS)
            nr = lr * gr + li * gi + buf_r[r, :]
            ni = lr * gi - li * gr + buf_i[r, :]
            if full:
                buf_r[r, :] = nr
                buf_i[r, :] = ni
            return nr, ni

        gr, gi = lax.fori_loop(0, n_it, step, (st_r[...], st_i[...]))
        st_r[...] = gr
        st_i[...] = gi
        if not full:
            gr_ref[...] = gr
            gi_ref[...] = gi
            return
        g_r, g_i = buf_r[...], buf_i[...]
        xr, xi = xr_ref[...], xi_ref[...]
        dlr_ref[...] += jnp.sum(g_r * xr + g_i * xi, axis=0, keepdims=True)
        dli_ref[...] += jnp.sum(g_i * xr - g_r * xi, axis=0, keepdims=True)
        ub = u_ref[...].astype(BF16)
        gb_r, gb_i = g_r.astype(BF16), g_i.astype(BF16)
        du_ref[...] = (lax.dot_general(gb_r, br_ref[...], nt, preferred_element_type=F32)
                       + lax.dot_general(gb_i, bi_ref[...], nt, preferred_element_type=F32))
        dbr_ref[...] += lax.dot_general(ub, gb_r, tn, preferred_element_type=F32)
        dbi_ref[...] += lax.dot_general(ub, gb_i, tn, preferred_element_type=F32)
        lr_, li_ = lr_ref[...], li_ref[...]
        x_r = lr_ * xr - li_ * xi + jnp.dot(ub, br_ref[...], preferred_element_type=F32)
        x_i = lr_ * xi + li_ * xr + jnp.dot(ub, bi_ref[...], preferred_element_type=F32)
        dcr_ref[...] += lax.dot_general(x_r.astype(BF16), dyb, tn, preferred_element_type=F32)
        dci_ref[...] -= lax.dot_general(x_i.astype(BF16), dyb, tn, preferred_element_type=F32)

    rev = (lambda c: nch - 1 - c) if descending else (lambda c: c)
    u_spec = pl.BlockSpec((rows, S5_CB), lambda b, c: (rev(c), b))
    bmat = pl.BlockSpec((None, S5_CB, S5_SB), lambda b, c: (b, 0, 0))
    cmat = pl.BlockSpec((None, S5_SB, S5_CB), lambda b, c: (b, 0, 0))
    lvec = pl.BlockSpec((1, S5_SB), lambda b, c: (0, b))
    evec = pl.BlockSpec((SSM_SEGMENTS, S5_SB), lambda b, c: (0, b))
    xblk = pl.BlockSpec((rows, S5_SB), lambda b, c: (rev(c), b))
    scratch = [pltpu.VMEM((SSM_SEGMENTS, S5_SB), F32)] * 2 + [pltpu.VMEM((rows, S5_SB), F32)] * 2
    e_shape = jax.ShapeDtypeStruct((SSM_SEGMENTS, GN), F32)
    if full:
        return pl.pallas_call(
            body, name=name, grid=(ncb, nch),
            in_specs=[u_spec, u_spec, xblk, xblk, bmat, bmat, cmat, cmat, lvec, lvec, evec, evec],
            out_specs=[u_spec, bmat, bmat, cmat, cmat, lvec, lvec],
            out_shape=[jax.ShapeDtypeStruct((S, D), F32), jax.ShapeDtypeStruct(b_re.shape, F32),
                       jax.ShapeDtypeStruct(b_re.shape, F32), jax.ShapeDtypeStruct(c_re.shape, F32),
                       jax.ShapeDtypeStruct(c_re.shape, F32), jax.ShapeDtypeStruct((1, GN), F32),
                       jax.ShapeDtypeStruct((1, GN), F32)],
            scratch_shapes=scratch, compiler_params=_cp(("parallel", "arbitrary")),
        )(dy, u, *xp, b_re, b_im, c_re, c_im, lam_re, lam_im, *starts)
    return pl.pallas_call(
        body, name=name, grid=(ncb, nch), in_specs=[u_spec, cmat, cmat, lvec, lvec],
        out_specs=[evec, evec], out_shape=[e_shape, e_shape],
        scratch_shapes=scratch, compiler_params=_cp(("parallel", "arbitrary")),
    )(dy, c_re, c_im, lam_re, lam_im)


def _s5_perm(t):
    S, D = t.shape
    return t.reshape(SSM_SEGMENTS, S // SSM_SEGMENTS, D).transpose(1, 0, 2).reshape(S, D)


def _s5_unperm(t):
    S, D = t.shape
    return t.reshape(S // SSM_SEGMENTS, SSM_SEGMENTS, D).transpose(1, 0, 2).reshape(S, D)


def _s5_blockdiag_b(bb, ncb):
    gpb = S5_CB // SSM_GROUP_CH
    t = bb.reshape(SSM_GROUP_CH, ncb, gpb, SSM_STATE)
    return jnp.einsum('cbgn,gh->bgchn', t, jnp.eye(gpb, dtype=bb.dtype)).reshape(ncb, S5_CB, S5_SB)


def _s5_blockdiag_b_t(dblk):
    ncb = dblk.shape[0]
    gpb = S5_CB // SSM_GROUP_CH
    t = dblk.reshape(ncb, gpb, SSM_GROUP_CH, gpb, SSM_STATE)
    return jnp.einsum('bgchn,gh->cbgn', t, jnp.eye(gpb, dtype=dblk.dtype)).reshape(SSM_GROUP_CH, -1)


def _s5_blockdiag_c(c, ncb):
    gpb = S5_CB // SSM_GROUP_CH
    t = c.reshape(ncb, gpb, SSM_GROUP_CH, SSM_STATE)
    return jnp.einsum('bgcn,gh->bgnhc', t, jnp.eye(gpb, dtype=c.dtype)).reshape(ncb, S5_SB, S5_CB)


def _s5_blockdiag_c_t(dblk):
    ncb = dblk.shape[0]
    gpb = S5_CB // SSM_GROUP_CH
    t = dblk.reshape(ncb, gpb, SSM_STATE, gpb, SSM_GROUP_CH)
    return jnp.einsum('bgnhc,gh->bgcn', t, jnp.eye(gpb, dtype=dblk.dtype)).reshape(-1, SSM_GROUP_CH, SSM_STATE)


_GELU_C = math.sqrt(2.0 / math.pi)


def _gelu(y):
    return y * (0.5 * (1.0 + jnp.tanh(_GELU_C * (y + 0.044715 * (y * y * y)))))


def _gelu_grad(y):
    t = jnp.tanh(_GELU_C * (y + 0.044715 * (y * y * y)))
    return 0.5 * (1.0 + t) + 0.5 * y * (1.0 - t * t) * (_GELU_C * (1.0 + 3.0 * 0.044715 * y * y))


def _rowwise(fn, ins, outs, name, acc=()):
    S, D = next(a.shape for a in ins if a.shape[0] != 1)
    tr = _pick(S, (256, 128, 8))
    row = pl.BlockSpec((tr, D), lambda i: (i, 0))
    vec = pl.BlockSpec((1, D), lambda i: (0, 0))
    n_in = len(ins)

    def body(*refs):
        res = fn(*[r[...] for r in refs[:n_in]])
        for k, (o, v) in enumerate(zip(refs[n_in:], res)):
            if k in acc:
                @pl.when(pl.program_id(0) == 0)
                def _():
                    o[...] = jnp.zeros_like(o)
                o[...] += jnp.sum(v, axis=0, keepdims=True)
            else:
                o[...] = v.astype(o.dtype)

    return pl.pallas_call(
        body, name=name, grid=(S // tr,), in_specs=[vec if a.shape[0] == 1 else row for a in ins],
        out_specs=[vec if k in acc else row for k in range(len(outs))],
        out_shape=[jax.ShapeDtypeStruct((1, D) if k in acc else (S, D), dt) for k, dt in enumerate(outs)],
        compiler_params=_cp(("arbitrary",) if acc else ("parallel",)),
    )(*ins)


def _s5_params(p):
    G, N = p["a_re"].shape[1:]
    vec = lambda a: a.reshape(2, 1, G * N)
    ls = jnp.broadcast_to(p["log_step"][:, :, None], (2, G, N))
    bt = lambda b: b.transpose(0, 3, 1, 2).reshape(2, SSM_GROUP_CH, G * N)
    return vec(p["a_re"]), vec(p["a_im"]), vec(ls), bt(p["b_re"]), bt(p["b_im"])


def _s5_layer_fwd(x, norm_g, p, w_glu, tag):
    S, D = x.shape
    ncb = D // S5_CB
    xp = _s5_perm(x)
    hn = _rmsnorm_fwd(xp, norm_g, F32, f"s5_norm_{tag}")
    raw = _s5_params(p)
    lam_r, lam_i, bb_r, bb_i = _s5_disc_fwd(*raw, f"s5_disc_{tag}")
    dirs = []
    ys = []
    for dirn in range(2):
        mats = (_s5_blockdiag_b(bb_r[dirn], ncb).astype(BF16), _s5_blockdiag_b(bb_i[dirn], ncb).astype(BF16),
                _s5_blockdiag_c(p["c_re"][dirn], ncb).astype(BF16), _s5_blockdiag_c(p["c_im"][dirn], ncb).astype(BF16))
        lam = (lam_r[dirn], lam_i[dirn])
        ends = _s5_scan(hn, mats[0], mats[1], None, None, *lam, None, dirn == 1, f"s5_ends_{tag}_{dirn}")
        xr, xi, y = _s5_scan(hn, *mats, *lam, ends, dirn == 1, f"s5_scan_{tag}_{dirn}")
        dirs.append(((xr, xi), mats, lam))
        ys.append(y)
    ytot, z = _rowwise(lambda u, d, a, b: ((lambda y: (y, _gelu(y)))(d * u + a + b)),
                       [hn, p["d"], ys[0], ys[1]], [F32, BF16], f"s5_gelu_{tag}")
    t = _mm(z, w_glu, name=f"s5_glu_{tag}", epi=lambda r, b: r + b, epi_in=(p["b_glu"],))
    (x_new,) = _rowwise(lambda xx, zz, tt: (xx + zz.astype(F32) * jax.nn.sigmoid(tt),),
                        [xp, z, t], [F32], f"s5_out_{tag}")
    return _s5_unperm(x_new), (xp, hn, raw, dirs, ytot, z, t)


def _s5_layer_bwd(dres, saved, norm_g, p, w_glu, tag):
    x, hn, raw, dirs, ytot, z, t = saved
    S, D = x.shape
    G, N = p["a_re"].shape[1:]
    dres = _s5_perm(dres[0])

    def glu_bwd(do, zz, tt):
        sg = jax.nn.sigmoid(tt)
        dt = do * zz.astype(F32) * (sg * (1.0 - sg))
        return dt, do * sg, dt

    dt, dzd, db_glu = _rowwise(glu_bwd, [dres, z, t], [BF16, F32, F32], f"s5_out_bwd_{tag}", acc=(2,))
    dz = _mm(dt, w_glu, tb=True, name=f"s5_dz_{tag}", epi=_add_epi, epi_in=(dzd,))
    dw_glu = _mm(z, dt, ta=True, name=f"s5_dwglu_{tag}", pieces='row')

    def gelu_bwd(dzz, y, u, d):
        dy = dzz * _gelu_grad(y)
        return dy, dy * d, dy * u

    dy, du, dd = _rowwise(gelu_bwd, [dz, ytot, hn, p["d"]], [F32, F32, F32], f"s5_gelu_bwd_{tag}", acc=(2,))
    d_lr, d_li, d_bbr, d_bbi, d_cr, d_ci = [], [], [], [], [], []
    du = [du]
    for dirn in range(2):
        xp, mats, lam = dirs[dirn]
        starts = _s5_scan_bwd(dy, None, None, None, None, mats[2], mats[3], *lam, None, dirn == 0,
                              f"s5_starts_{tag}_{dirn}")
        dup, dbr, dbi, dcr, dci, dlr, dli = _s5_scan_bwd(dy, hn, xp, *mats, *lam, starts, dirn == 0,
                                                         f"s5_scan_bwd_{tag}_{dirn}")
        du.append(dup)
        d_lr.append(dlr)
        d_li.append(dli)
        d_bbr.append(_s5_blockdiag_b_t(dbr))
        d_bbi.append(_s5_blockdiag_b_t(dbi))
        d_cr.append(_s5_blockdiag_c_t(dcr))
        d_ci.append(_s5_blockdiag_c_t(dci))
    da_re, da_im, dls, db_re, db_im = _s5_disc_bwd(*raw, jnp.stack(d_lr), jnp.stack(d_li), jnp.stack(d_bbr),
                                                   jnp.stack(d_bbi), f"s5_disc_bwd_{tag}")
    dres, dg = _rmsnorm_bwd(x, norm_g, du, dres, f"s5_norm_bwd_{tag}")
    dres = tuple(_s5_unperm(t_) for t_ in dres)
    unb = lambda b: b.reshape(2, SSM_GROUP_CH, G, N).transpose(0, 2, 3, 1)
    grads = dict(a_re=da_re.reshape(2, G, N), a_im=da_im.reshape(2, G, N), log_step=dls.reshape(2, G, N).sum(-1),
                 b_re=unb(db_re), b_im=unb(db_im), c_re=jnp.stack(d_cr), c_im=jnp.stack(d_ci),
                 d=dd, w_glu=dw_glu, b_glu=db_glu, norm=dg)
    return dres, grads


def _adamw(w, g, m, v, name):
    R, C = w.shape
    tr = _pick(R, (512, 256, 128, 64, 32, 16, 8))
    tn = _pick(C, (512, 256, 128))

    def body(w_ref, g_ref, m_ref, v_ref, d_ref, nm_ref, nv_ref):
        gg = g_ref[...]
        m2 = ADAM_B1 * m_ref[...] + (1.0 - ADAM_B1) * gg
        v2 = ADAM_B2 * v_ref[...] + (1.0 - ADAM_B2) * (gg * gg)
        m_hat = m2 / (1.0 - ADAM_B1 ** ADAM_STEP)
        v_hat = v2 / (1.0 - ADAM_B2 ** ADAM_STEP)
        d_ref[...] = -ADAM_LR * (m_hat / (jnp.sqrt(v_hat) + ADAM_EPS) + ADAM_WD * w_ref[...])
        nm_ref[...] = m2
        nv_ref[...] = v2

    blk = pl.BlockSpec((tr, tn), lambda i, j: (i, j))
    shp = jax.ShapeDtypeStruct((R, C), F32)
    return pl.pallas_call(body, name=name, grid=(R // tr, C // tn), in_specs=[blk] * 4, out_specs=[blk] * 3,
                          out_shape=[shp] * 3, compiler_params=_cp(("parallel", "parallel")))(w, g, m, v)


def _add_selected(p, sel, others, name, also_bf16=False):
    K, _, M, C = p.shape
    tr = _pick(M, [t for t in (512, 256, 128, 64, 32, 16) if t * C * 4 <= 2 ** 21])
    n_o = len(others)

    def body(sel_ref, p_ref, *refs):
        acc = p_ref[...]
        for r in refs[:n_o]:
            acc = acc + r[...].astype(F32)
        refs[n_o][...] = acc
        if also_bf16:
            refs[n_o + 1][...] = acc.astype(BF16)

    blk = pl.BlockSpec((None, tr, C), lambda k, i, s: (k, i, 0))
    grid_spec = pltpu.PrefetchScalarGridSpec(
        num_scalar_prefetch=1, grid=(K, M // tr),
        in_specs=[pl.BlockSpec((None, None, tr, C), lambda k, i, s: (k, s[0], i, 0))] + [blk] * n_o,
        out_specs=[blk, blk] if also_bf16 else blk)
    shp = jax.ShapeDtypeStruct((K, M, C), F32)
    return pl.pallas_call(body, name=name, grid_spec=grid_spec,
                          out_shape=[shp, jax.ShapeDtypeStruct((K, M, C), BF16)] if also_bf16 else shp,
                          compiler_params=_cp(("parallel", "parallel")))(sel, p, *others)


def _sum_slots(a, own, me, name):
    n, R, C = a.shape
    tr = _pick(R, (512, 256, 128, 64, 32, 16, 8))

    def body(me_ref, a_ref, own_ref, o_ref):
        term = lambda k: jnp.where(me_ref[0] == k, own_ref[...], a_ref[k])
        acc = term(0)
        for k in range(1, n):
            acc = acc + term(k)
        o_ref[...] = acc

    grid_spec = pltpu.PrefetchScalarGridSpec(
        num_scalar_prefetch=1, grid=(R // tr,),
        in_specs=[pl.BlockSpec((n, tr, C), lambda i, s: (0, i, 0)), pl.BlockSpec((tr, C), lambda i, s: (i, 0))],
        out_specs=pl.BlockSpec((tr, C), lambda i, s: (i, 0)))
    return pl.pallas_call(body, name=name, grid_spec=grid_spec, out_shape=jax.ShapeDtypeStruct((R, C), F32),
                          compiler_params=_cp(("parallel",)))(me, a, own)


def _position():
    return lax.axis_index("x"), lax.axis_index("y"), lax.axis_index("c")


def _other_chips(x, y):
    return [(1 - x, y), (x, 1 - y), (1 - x, 1 - y)]


def _exchange_call(x, name):
    n_in, n_out = len(x.ins), len(x.out_shapes)

    def body(*refs):
        in_refs, out_refs, sems = refs[:n_in], refs[n_in:n_in + n_out], refs[n_in + n_out:]
        x.start(in_refs, out_refs, *sems)
        x.wait(in_refs, out_refs, *sems)

    hbm = pl.BlockSpec(memory_space=pltpu.HBM)
    return pl.pallas_call(body, name=name, in_specs=[hbm] * n_in, out_specs=[hbm] * n_out, out_shape=x.out_shapes,
                          scratch_shapes=x.sem_shapes())(*x.ins)


def _gather_chips(arrs):
    n = len(arrs)

    def plan(ins, outs):
        x, y, c = _position()
        me = 2 * x + y
        sends, recvs = [], []
        for px, py in _other_chips(x, y):
            for i in range(n):
                sends.append((ins[i], outs[i].at[me], (px, py, c)))
                recvs.append((outs[i].at[2 * px + py], (px, py, c)))
        return sends, recvs

    return _Exchange(arrs, [jax.ShapeDtypeStruct((4,) + a.shape, a.dtype) for a in arrs], 3 * n, plan)


def _gather_halves_ici(arrs):
    n = len(arrs)
    hr = [a.shape[1] // 2 for a in arrs]

    def plan(ins, outs):
        x, y, c = _position()
        me = 2 * x + y
        sends, recvs = [], []
        for px, py in _other_chips(x, y):
            for i in range(n):
                sends.append((ins[i].at[:, pl.ds(c * hr[i], hr[i])], outs[i].at[me, c], (px, py, c)))
                recvs.append((outs[i].at[2 * px + py, c], (px, py, c)))
        return sends, recvs

    shapes = [jax.ShapeDtypeStruct((N_CHIPS, 2, a.shape[0], a.shape[1] // 2, a.shape[2]), a.dtype) for a in arrs]
    return _Exchange(arrs, shapes, 3 * n, plan)


def _gather_halves_d2d(landed):
    n = len(landed)

    def plan(ins, outs):
        x, y, c = _position()
        sib = (x, y, 1 - c)
        sends, recvs = [], []
        for px, py in _other_chips(x, y):
            for i in range(n):
                sends.append((ins[i].at[2 * px + py, c], outs[i].at[2 * px + py, c], sib))
                recvs.append((outs[i].at[2 * px + py, 1 - c], sib))
        return sends, recvs

    return _Exchange(landed, [jax.ShapeDtypeStruct(a.shape, a.dtype) for a in landed], 3 * n, plan)


class _Reducer:
    def __init__(self, pieces, tags, sel_core, sel_chip):
        self.pieces, self.tags, self.sel_core, self.sel_chip = pieces, tags, sel_core, sel_chip

    def sibling(self):
        return _sibling_halves(self.pieces)

    def chips(self, from_sibling):
        self.pair = [_add_selected(p, self.sel_core, [r], f"reduce_add_pair_{t}", also_bf16=True)
                     for t, p, r in zip(self.tags, self.pieces, from_sibling)]
        return _scatter_chips([pb for _, pb in self.pair])

    def swap(self, from_chips):
        self.mine = [_add_selected(p.reshape((1,) + p.shape), self.sel_chip, [r[k:k + 1] for k in range(3)],
                                   f"reduce_add_chips_{t}")[0]
                     for t, (p, _), r in zip(self.tags, self.pair, from_chips)]
        return _sibling_swap(self.mine)

    def finish(self, theirs):
        return list(zip(self.mine, theirs))

    def run(self, tag):
        landed = _exchange_call(self.sibling(), f"reduce_sibling_{tag}")
        landed = _exchange_call(self.chips(landed), f"reduce_chips_{tag}")
        return self.finish(_exchange_call(self.swap(landed), f"reduce_join_{tag}"))


def _sibling_halves(pieces):
    n = len(pieces)

    def plan(ins, outs):
        x, y, c = _position()
        sib = (x, y, 1 - c)
        return [(ins[i].at[:, 1 - c], outs[i], sib) for i in range(n)], [(outs[i], sib) for i in range(n)]

    shapes = [jax.ShapeDtypeStruct((p.shape[0],) + p.shape[2:], p.dtype) for p in pieces]
    return _Exchange(pieces, shapes, n, plan)


def _scatter_chips(sums):
    n = len(sums)

    def plan(ins, outs):
        x, y, c = _position()
        sends, recvs = [], []
        for j, (px, py) in enumerate(_other_chips(x, y)):
            for i in range(n):
                sends.append((ins[i].at[2 * px + py], outs[i].at[j], (px, py, c)))
                recvs.append((outs[i].at[j], (px, py, c)))
        return sends, recvs

    return _Exchange(sums, [jax.ShapeDtypeStruct((3,) + s.shape[1:], s.dtype) for s in sums], 3 * n, plan)


def _sibling_swap(halves):
    n = len(halves)

    def plan(ins, outs):
        x, y, c = _position()
        sib = (x, y, 1 - c)
        return [(ins[i], outs[i], sib) for i in range(n)], [(outs[i], sib) for i in range(n)]

    return _Exchange(halves, [jax.ShapeDtypeStruct(h.shape, h.dtype) for h in halves], n, plan)


def _gather_all(v):
    rels = [(dx, dy, dc) for dx in (0, 1) for dy in (0, 1) for dc in (0, 1)][1:]

    def plan(ins, outs):
        x, y, c = _position()
        me = 4 * x + 2 * y + c
        flip = lambda a, d: 1 - a if d else a
        sends, recvs = [], []
        for dx, dy, dc in rels:
            px, py, pc = flip(x, dx), flip(y, dy), flip(c, dc)
            sends.append((ins[0], outs[0].at[me], (px, py, pc)))
            recvs.append((outs[0].at[4 * px + 2 * py + pc], (px, py, pc)))
        return sends, recvs

    return _Exchange([v], [jax.ShapeDtypeStruct((8,) + v.shape, v.dtype)], len(rels), plan)


WEIGHTS = ['mix_norm', 'ffn_norm', 'final_norm', 'attn_w_qkv', 'attn_w_o', 'attn_sink', 'ssm_a_re', 'ssm_a_im',
           'ssm_log_step', 'ssm_b_re', 'ssm_b_im', 'ssm_c_re', 'ssm_c_im', 'ssm_d', 'ssm_w_glu', 'ssm_b_glu',
           'mla_w_dqkv', 'mla_q_norm', 'mla_kv_norm', 'mla_w_uq', 'mla_w_ukv', 'mla_w_o', 'ffn_w_up',
           'ffn_conv_w', 'ffn_conv_b', 'ffn_w_down']
BIG = dict(attn_w_qkv='col', attn_w_o='row', ssm_w_glu='row', mla_w_dqkv='row', mla_w_uq='col',
           mla_w_ukv='col', mla_w_o='row', ffn_w_up='col', ffn_w_down='row')
SMALL_SHARDED = ('mla_q_norm', 'mla_kv_norm', 'ffn_conv_w')
N_CHIPS = 4


def _assemble(over_ici, over_d2d, own, chip, core, kind):
    L = over_ici.shape[2]
    half = lax.broadcasted_iota(jnp.int32, (1, 2, 1, 1, 1), 1)
    g = jnp.where(half == core, over_ici, over_d2d)
    own_halves = own.reshape(L, 2, g.shape[3], g.shape[4]).transpose(1, 0, 2, 3)
    slot = lax.broadcasted_iota(jnp.int32, (N_CHIPS, 1, 1, 1, 1), 0)
    g = jnp.where(slot == chip, own_halves[None], g)
    if kind == 'row':
        return g.transpose(2, 0, 1, 3, 4).reshape(L, -1, g.shape[4])
    return g.transpose(2, 1, 3, 0, 4).reshape(L, 2 * g.shape[3], -1)


def _to_pieces(w, kind):
    L, R, C = w.shape
    if kind == 'col':
        t = w.reshape(L, 2, R // 2, N_CHIPS, C // N_CHIPS).transpose(3, 1, 0, 2, 4)
    else:
        t = w.reshape(L, N_CHIPS, R // N_CHIPS, 2, C // 2).transpose(1, 3, 0, 2, 4)
    return t.reshape(N_CHIPS, 2, L * t.shape[3], t.shape[4])


def _from_halves(h, kind, shard_shape):
    L = shard_shape[0]
    t = h.reshape(2, L, -1, h.shape[2])
    t = t.transpose(1, 0, 2, 3) if kind == 'col' else t.transpose(1, 2, 0, 3)
    return t.reshape(shard_shape)


def _pack(arrs):
    flat = jnp.concatenate([a.reshape(-1) for a in arrs])
    pad = (-flat.shape[0]) % (8 * LANES)
    return jnp.pad(flat, (0, pad)).reshape(-1, LANES)


def _unpack(packed, like):
    flat = packed.reshape(-1)
    out, off = [], 0
    for a in like:
        out.append(flat[off:off + a.size].reshape(a.shape))
        off += a.size
    return out


def kernel(x, mix_norm, ffn_norm, final_norm, attn_w_qkv, attn_w_o, attn_sink, ssm_a_re, ssm_a_im, ssm_log_step, ssm_b_re, ssm_b_im, ssm_c_re, ssm_c_im, ssm_d, ssm_w_glu, ssm_b_glu, mla_w_dqkv, mla_q_norm, mla_kv_norm, mla_w_uq, mla_w_ukv, mla_w_o, ffn_w_up, ffn_conv_w, ffn_conv_b, ffn_w_down, loss_target, m_mix_norm, m_ffn_norm, m_final_norm, m_attn_w_qkv, m_attn_w_o, m_attn_sink, m_ssm_a_re, m_ssm_a_im, m_ssm_log_step, m_ssm_b_re, m_ssm_b_im, m_ssm_c_re, m_ssm_c_im, m_ssm_d, m_ssm_w_glu, m_ssm_b_glu, m_mla_w_dqkv, m_mla_q_norm, m_mla_kv_norm, m_mla_w_uq, m_mla_w_ukv, m_mla_w_o, m_ffn_w_up, m_ffn_conv_w, m_ffn_conv_b, m_ffn_w_down, v_mix_norm, v_ffn_norm, v_final_norm, v_attn_w_qkv, v_attn_w_o, v_attn_sink, v_ssm_a_re, v_ssm_a_im, v_ssm_log_step, v_ssm_b_re, v_ssm_b_im, v_ssm_c_re, v_ssm_c_im, v_ssm_d, v_ssm_w_glu, v_ssm_b_glu, v_mla_w_dqkv, v_mla_q_norm, v_mla_kv_norm, v_mla_w_uq, v_mla_w_ukv, v_mla_w_o, v_ffn_w_up, v_ffn_conv_w, v_ffn_conv_b, v_ffn_w_down):
    args = locals()
    w = {n: args[n] for n in WEIGHTS}
    mom = {n: args["m_" + n] for n in WEIGHTS}
    var = {n: args["v_" + n] for n in WEIGHTS}
    depth = mix_norm.shape[0]
    xs = x[0]
    target = loss_target[0]
    chip = 2 * lax.axis_index("x") + lax.axis_index("y")
    core = lax.axis_index("c")

    big_names = list(BIG)
    w_bf = {n: w[n].astype(BF16) for n in big_names}
    mixer_weights = {0: ('attn_w_qkv', 'attn_w_o'), 1: ('ssm_w_glu',),
                     2: ('mla_w_dqkv', 'mla_w_uq', 'mla_w_ukv', 'mla_w_o')}

    def layer_items(i):
        return [(n, i // 3) for n in mixer_weights[i % 3]] + [('ffn_w_up', i), ('ffn_w_down', i)]

    def blocks(i):
        return [w_bf[n][l:l + 1] for n, l in layer_items(i)]

    full = {}

    def install(i, landed):
        for (n, l), a, b, own in zip(layer_items(i), *landed, blocks(i)):
            full[n, l] = _assemble(a, b, own, chip, core, BIG[n])[0]

    over_ici = _exchange_call(_gather_halves_ici(blocks(0)), "gather_first_ici")
    install(0, (over_ici, _exchange_call(_gather_halves_d2d(over_ici), "gather_first_d2d")))
    small_sharded = _exchange_call(_gather_chips([w[n] for n in SMALL_SHARDED]), "gather_small_weights")
    for n, g in zip(SMALL_SHARDED, small_sharded):
        slot = lax.broadcasted_iota(jnp.int32, (N_CHIPS,) + (1,) * w[n].ndim, 0)
        g = jnp.where(slot == chip, w[n][None], g)
        full[n] = jnp.moveaxis(g, 0, -2).reshape(g.shape[1:-1] + (-1,))

    def ffn_args(i):
        return (ffn_norm[i:i + 1], full["ffn_w_up", i], full["ffn_conv_w"][i], ffn_conv_b[i:i + 1], full["ffn_w_down", i])

    def s5_params(j):
        return dict(a_re=ssm_a_re[j], a_im=ssm_a_im[j], log_step=ssm_log_step[j], b_re=ssm_b_re[j], b_im=ssm_b_im[j],
                    c_re=ssm_c_re[j], c_im=ssm_c_im[j], d=ssm_d[j:j + 1], b_glu=ssm_b_glu[j:j + 1])

    def mla_args(j):
        wd, wq, wkv = _mla_weights(full["mla_w_dqkv", j], full["mla_w_uq", j], full["mla_w_ukv", j])
        return (wd, wq, wkv, full["mla_w_o", j], full["mla_q_norm"][j:j + 1], full["mla_kv_norm"][j:j + 1])

    h = xs
    saved = []
    for i in range(depth):
        kind, j = i % 3, i // 3
        g = mix_norm[i:i + 1]
        if kind == 0:
            h, sm = _attn_layer_fwd(h, g, full["attn_w_qkv", j], full["attn_w_o", j], attn_sink[j], f"{i}")
        elif kind == 1:
            h, sm = _s5_layer_fwd(h, g, s5_params(j), full["ssm_w_glu", j], f"{i}")
        else:
            h, sm = _mla_layer_fwd(h, g, *mla_args(j), f"{i}")
        h, sf, landed = _ffn_fwd(h, *ffn_args(i), f"{i}", next_blocks=blocks(i + 1) if i + 1 < depth else None)
        if landed is not None:
            install(i + 1, landed)
        saved.append((sm, sf))
    loss_part, dres, d_final = _loss_head(h, final_norm.reshape(1, -1), target, "loss_head")

    gl = {n: [None] * w[n].shape[0] for n in WEIGHTS if n != 'final_norm'}
    sel_core = jnp.reshape(core, (1,)).astype(jnp.int32)
    sel_chip = jnp.reshape(chip, (1,)).astype(jnp.int32)
    reduced = {}
    pending = None
    for i in reversed(range(depth)):
        kind, j = i % 3, i // 3
        sm, sf = saved[i]
        dres, (dg, dwu, dcw, dcb, dwd), done = _ffn_bwd(dres, sf, *ffn_args(i), f"{i}", reducer=pending)
        if pending is not None:
            reduced.update(zip(layer_items(i + 1), done))
        gl['ffn_norm'][i], gl['ffn_w_up'][i], gl['ffn_conv_w'][i] = dg[0], dwu, dcw
        gl['ffn_conv_b'][i], gl['ffn_w_down'][i] = dcb[0], dwd
        g = mix_norm[i:i + 1]
        if kind == 0:
            dres, (dg, dwq, dwo, dsk) = _attn_layer_bwd(dres, sm, g, full["attn_w_qkv", j], full["attn_w_o", j],
                                                        attn_sink[j], f"{i}")
            gl['attn_w_qkv'][j], gl['attn_w_o'][j], gl['attn_sink'][j] = dwq, dwo, dsk
        elif kind == 1:
            dres, gs = _s5_layer_bwd(dres, sm, g, s5_params(j), full["ssm_w_glu", j], f"{i}")
            dg = gs['norm']
            for k in ('a_re', 'a_im', 'log_step', 'b_re', 'b_im', 'c_re', 'c_im'):
                gl['ssm_' + k][j] = gs[k]
            gl['ssm_d'][j], gl['ssm_b_glu'][j], gl['ssm_w_glu'][j] = gs['d'][0], gs['b_glu'][0], gs['w_glu']
        else:
            dres, (dg, dwd_, dqn, dkvn, dwuq, dwukv, dwo) = _mla_layer_bwd(dres, sm, g, *mla_args(j), f"{i}")
            gl['mla_w_dqkv'][j], gl['mla_q_norm'][j], gl['mla_kv_norm'][j] = dwd_, dqn[0], dkvn[0]
            gl['mla_w_uq'][j], gl['mla_w_ukv'][j], gl['mla_w_o'][j] = dwuq, dwukv, dwo
        gl['mix_norm'][i] = dg[0]
        pieces = [gl[n][l] if gl[n][l].ndim == 4 else _to_pieces(gl[n][l][None], BIG[n]) for n, l in layer_items(i)]
        pending = _Reducer(pieces, [f"{n}_{l}" for n, l in layer_items(i)], sel_core, sel_chip)
    reduced.update(zip(layer_items(0), pending.run("first_layer")))
    local = {n: jnp.stack(v) for n, v in gl.items() if n not in BIG}
    local['final_norm'] = d_final[0]

    grads = {}
    for n in big_names:
        shards = []
        for l in range(w[n].shape[0]):
            a, b = reduced[n, l]
            halves = jnp.where(core == 0, jnp.stack([a, b]), jnp.stack([b, a]))
            shards.append(_from_halves(halves, BIG[n], (1,) + w[n].shape[1:]))
        grads[n] = jnp.concatenate(shards, axis=0)

    small_names = [n for n in WEIGHTS if n not in BIG]
    packed = _pack([local[n] for n in small_names])
    device = jnp.reshape(2 * chip + core, (1,)).astype(jnp.int32)
    summed = _sum_slots(_exchange_call(_gather_all(packed), "reduce_small")[0], packed, device, "reduce_small_sum")
    for n, gsum in zip(small_names, _unpack(summed, [local[n] for n in small_names])):
        if n in SMALL_SHARDED:
            width = w[n].shape[-1]
            gsum = lax.dynamic_slice_in_dim(gsum, chip * width, width, axis=gsum.ndim - 1)
        grads[n] = gsum

    delta, new_m, new_v = {}, {}, {}
    for n in big_names:
        two_d = lambda a: a.reshape(-1, a.shape[-1])
        d_, m_, v_ = _adamw(two_d(w[n]), two_d(grads[n]), two_d(mom[n]), two_d(var[n]), f"adamw_{n}")
        delta[n], new_m[n], new_v[n] = (t.reshape(w[n].shape) for t in (d_, m_, v_))
    outs = _adamw(*[_pack([src[n] for n in small_names]) for src in (w, grads, mom, var)], "adamw_small")
    for dst, packed in zip((delta, new_m, new_v), outs):
        for n, t in zip(small_names, _unpack(packed, [w[n] for n in small_names])):
            dst[n] = t

    loss = lax.psum(loss_part[0, 0], ("x", "y", "c"))
    return (loss, dres[0][None], *[grads[n] for n in WEIGHTS], *[delta[n] for n in WEIGHTS],
            *[new_m[n] for n in WEIGHTS], *[new_v[n] for n in WEIGHTS])
```

```python
import functools
import math

import numpy as np
import jax
import jax.numpy as jnp
from jax import lax
from jax.experimental import pallas as pl
from jax.experimental.pallas import tpu as pltpu

F32 = jnp.float32
BF16 = jnp.bfloat16
MESH = pl.DeviceIdType.MESH

RMS_EPS = 1e-6
ATTN_DH = 64
ATTN_GROUP = 8
ATTN_BLOCK = 128
SSM_GROUP_CH = 16
SSM_STATE = 64
SSM_SEGMENTS = 8
MLA_HEADS = 16
MLA_LORA = 512
MLA_NOPE = 128
MLA_ROPE = 64
MLA_V = 128
ROPE_THETA = 10000.0
LANES = 128
VMEM_LIMIT = 56 * 1024 * 1024

ADAM_LR = 0.001
ADAM_B1 = 0.9
ADAM_B2 = 0.999
ADAM_EPS = 1e-08
ADAM_WD = 0.01
ADAM_STEP = 10


def _cp(sem=None):
    kw = dict(vmem_limit_bytes=VMEM_LIMIT)
    if sem is not None:
        kw["dimension_semantics"] = sem
    return pltpu.CompilerParams(**kw)


MM_FULL_K = 2048
MM_FULL_N = 1536


def _pick(n, cands):
    for c in cands:
        if n % c == 0:
            return c
    return n


class _Exchange:
    def __init__(self, ins, out_shapes, n_remote, plan):
        self.ins, self.out_shapes, self.n_remote, self.plan = list(ins), list(out_shapes), n_remote, plan

    def _copy(self, k, src, dst, dev, send_sems, recv_sems):
        return pltpu.make_async_remote_copy(src_ref=src, dst_ref=dst, send_sem=send_sems.at[k],
                                            recv_sem=recv_sems.at[k], device_id=dev, device_id_type=MESH)

    def start(self, in_refs, out_refs, send_sems, recv_sems):
        sends, recvs = self.plan(in_refs, out_refs)
        assert len(sends) == len(recvs) == self.n_remote
        for k, (src, dst, dev) in enumerate(sends):
            self._copy(k, src, dst, dev, send_sems, recv_sems).start()

    def wait(self, in_refs, out_refs, send_sems, recv_sems):
        sends, recvs = self.plan(in_refs, out_refs)
        for k, (dst, dev) in enumerate(recvs):
            self._copy(k, dst, dst, dev, send_sems, recv_sems).wait_recv()
        for k, (src, dst, dev) in enumerate(sends):
            self._copy(k, src, dst, dev, send_sems, recv_sems).wait_send()

    def sem_shapes(self):
        return [pltpu.SemaphoreType.DMA((self.n_remote,)), pltpu.SemaphoreType.DMA((self.n_remote,))]


def _gmm(a, b, *, grid, a_spec, b_spec, o_spec, out_shape, dims, kax, acc_shape, name,
         epi=None, epi_in=(), epi_specs=(), hosted=None):
    nk = grid[kax]
    n_epi = len(epi_in)
    n_hin = len(hosted.ins) if hosted else 0
    n_hout = len(hosted.out_shapes) if hosted else 0

    def finish(r, e_refs, o_ref):
        if epi is not None:
            r = epi(r, *[e[...] for e in e_refs])
        o_ref[...] = r.astype(o_ref.dtype)

    def product(a_ref, b_ref):
        return lax.dot_general(a_ref[...].astype(BF16), b_ref[...].astype(BF16), dims, preferred_element_type=F32)

    def body(a_ref, b_ref, *rest):
        e_refs = rest[:n_epi]
        h_in = rest[n_epi:n_epi + n_hin]
        o_ref = rest[n_epi + n_hin]
        h_out = rest[n_epi + n_hin + 1:n_epi + n_hin + 1 + n_hout]
        scratch = rest[n_epi + n_hin + 1 + n_hout:]
        if hosted:
            sems = scratch[-2:]
            ids = [pl.program_id(i) for i in range(len(grid))]
            first = functools.reduce(jnp.logical_and, [i == 0 for i in ids])
            last = functools.reduce(jnp.logical_and, [i == n - 1 for i, n in zip(ids, grid)])
            pl.when(first)(lambda: hosted.start(h_in, h_out, *sems))
        if nk == 1:
            finish(product(a_ref, b_ref), e_refs, o_ref)
        else:
            acc_ref = scratch[0]
            k = pl.program_id(kax)

            @pl.when(k == 0)
            def _():
                acc_ref[...] = jnp.zeros_like(acc_ref)

            acc_ref[...] += product(a_ref, b_ref)

            @pl.when(k == nk - 1)
            def _():
                finish(acc_ref[...], e_refs, o_ref)
        if hosted:
            pl.when(last)(lambda: hosted.wait(h_in, h_out, *sems))

    sem = tuple("arbitrary" if (i == kax or hosted) else "parallel" for i in range(len(grid)))
    hbm = pl.BlockSpec(memory_space=pltpu.HBM)
    res = pl.pallas_call(
        body, name=name, grid=grid,
        in_specs=[a_spec, b_spec, *epi_specs] + [hbm] * n_hin,
        out_specs=[o_spec] + [hbm] * n_hout if hosted else o_spec,
        out_shape=[out_shape] + hosted.out_shapes if hosted else out_shape,
        scratch_shapes=([] if nk == 1 else [pltpu.VMEM(acc_shape, F32)]) + (hosted.sem_shapes() if hosted else []),
        compiler_params=_cp(sem),
    )(a, b, *epi_in, *(hosted.ins if hosted else ()))
    return (res[0], list(res[1:])) if hosted else res


def _mm(a, b, *, ta=False, tb=False, out_dtype=F32, name, epi=None, epi_in=(), pieces=None, hosted=None):
    a_parts = a.shape[0] if a.ndim == 3 else 1
    b_parts = b.shape[0] if b.ndim == 3 else 1
    assert not (ta and a_parts > 1) and not (tb and b_parts > 1)
    (K, M) = a.shape if ta else (a.shape[-1] * a_parts, a.shape[-2])
    (N, K2) = b.shape if tb else (b.shape[-1] * b_parts, b.shape[-2])
    assert K == K2, (a.shape, b.shape, ta, tb)
    m_unit = {None: M, 'col': M // 2, 'row': M // N_CHIPS}[pieces]
    n_unit = {None: N // b_parts, 'col': N // N_CHIPS, 'row': N // 2}[pieces]
    k_unit = K // a_parts
    tm = _pick(m_unit, (1024, 1408, 512, 256, 128))
    tn = n_unit if n_unit <= MM_FULL_N else _pick(n_unit, (1024, 1408, 512, 256, 128))
    tk = k_unit if k_unit <= MM_FULL_K else _pick(k_unit, (2048, 1408, 1024, 512, 256, 128))
    grid = (M // tm, N // tn, K // tk)
    if ta:
        a_spec = pl.BlockSpec((tk, tm), lambda i, j, k: (k, i))
    elif a_parts > 1:
        nkp = k_unit // tk
        a_spec = pl.BlockSpec((None, tm, tk), lambda i, j, k: (k // nkp, i, k % nkp))
    else:
        a_spec = pl.BlockSpec((tm, tk), lambda i, j, k: (i, k))
    if tb:
        b_spec = pl.BlockSpec((tn, tk), lambda i, j, k: (j, k))
    elif b_parts > 1:
        njp = (N // b_parts) // tn
        b_spec = pl.BlockSpec((None, tk, tn), lambda i, j, k: (j // njp, k, j % njp))
    else:
        b_spec = pl.BlockSpec((tk, tn), lambda i, j, k: (k, j))
    dims = (((0 if ta else 1,), (1 if tb else 0,)), ((), ()))
    epi_specs = [pl.BlockSpec((1, tn), lambda i, j, k: (0, j)) if e.shape[0] == 1
                 else pl.BlockSpec((tm, tn), lambda i, j, k: (i, j)) for e in epi_in]
    if pieces is None:
        o_spec = pl.BlockSpec((tm, tn), lambda i, j, k: (i, j))
        out_shape = jax.ShapeDtypeStruct((M, N), out_dtype)
    else:
        mi, nj = m_unit // tm, n_unit // tn
        if pieces == 'col':
            o_spec = pl.BlockSpec((None, None, tm, tn), lambda i, j, k: (j // nj, i // mi, i % mi, j % nj))
        else:
            o_spec = pl.BlockSpec((None, None, tm, tn), lambda i, j, k: (i // mi, j // nj, i % mi, j % nj))
        out_shape = jax.ShapeDtypeStruct((N_CHIPS, 2, m_unit, n_unit), out_dtype)
    return _gmm(a, b, grid=grid, a_spec=a_spec, b_spec=b_spec, o_spec=o_spec, out_shape=out_shape, dims=dims, kax=2,
                acc_shape=(tm, tn), name=name, epi=epi, epi_in=epi_in, epi_specs=epi_specs, hosted=hosted)


def _add_epi(r, res):
    return res + r


def _rmsnorm_fwd(x, g, out_dtype, name):
    S, D = x.shape
    tr = _pick(S, (256, 128, 8))

    def body(x_ref, g_ref, o_ref):
        xf = x_ref[...]
        r = lax.rsqrt(jnp.mean(xf * xf, axis=-1, keepdims=True) + RMS_EPS)
        o_ref[...] = ((xf * r) * g_ref[...]).astype(o_ref.dtype)

    return pl.pallas_call(
        body, name=name, grid=(S // tr,),
        in_specs=[pl.BlockSpec((tr, D), lambda i: (i, 0)), pl.BlockSpec((1, D), lambda i: (0, 0))],
        out_specs=pl.BlockSpec((tr, D), lambda i: (i, 0)),
        out_shape=jax.ShapeDtypeStruct((S, D), out_dtype), compiler_params=_cp(("parallel",)),
    )(x, g)


def _rms_bwd_math(xf, g, dy):
    r = lax.rsqrt(jnp.mean(xf * xf, axis=-1, keepdims=True) + RMS_EPS)
    xh = xf * r
    dxh = dy * g
    dx = r * (dxh - xh * jnp.mean(dxh * xh, axis=-1, keepdims=True))
    return dx, jnp.sum(dy * xh, axis=0, keepdims=True)


def _rmsnorm_bwd(x, g, dy, dres, name):
    S, D = x.shape
    tr = _pick(S, (256, 128, 8))
    dys = list(dy) if isinstance(dy, (list, tuple)) else [dy]
    n_dy = len(dys)

    def body(x_ref, g_ref, *refs):
        dy_refs = refs[:n_dy]
        dres_ref, dx_ref, dxb_ref, dg_ref = refs[n_dy:]

        @pl.when(pl.program_id(0) == 0)
        def _():
            dg_ref[...] = jnp.zeros_like(dg_ref)

        dy_sum = dy_refs[0][...].astype(F32)
        for r in dy_refs[1:]:
            dy_sum = dy_sum + r[...].astype(F32)
        dx, dg = _rms_bwd_math(x_ref[...], g_ref[...], dy_sum)
        tot = dres_ref[...] + dx
        dx_ref[...] = tot
        dxb_ref[...] = tot.astype(BF16)
        dg_ref[...] += dg

    row = pl.BlockSpec((tr, D), lambda i: (i, 0))
    vec = pl.BlockSpec((1, D), lambda i: (0, 0))
    dx, dxb, dg = pl.pallas_call(
        body, name=name, grid=(S // tr,), in_specs=[row, vec] + [row] * (n_dy + 1), out_specs=[row, row, vec],
        out_shape=[jax.ShapeDtypeStruct((S, D), F32), jax.ShapeDtypeStruct((S, D), BF16),
                   jax.ShapeDtypeStruct((1, D), F32)],
        compiler_params=_cp(("arbitrary",)),
    )(x, g, *dys, dres)
    return (dx, dxb), dg


def _loss_head(x, g, target, name):
    S, D = x.shape
    tr = _pick(S, (256, 128, 8))

    def body(x_ref, g_ref, t_ref, loss_ref, dx_ref, dxb_ref, dg_ref):
        @pl.when(pl.program_id(0) == 0)
        def _():
            dg_ref[...] = jnp.zeros_like(dg_ref)
            loss_ref[...] = jnp.zeros_like(loss_ref)

        xf = x_ref[...]
        gg = g_ref[...]
        r = lax.rsqrt(jnp.mean(xf * xf, axis=-1, keepdims=True) + RMS_EPS)
        e = (xf * r) * gg - t_ref[...]
        loss_ref[...] += 0.5 * jnp.sum(jnp.mean(e * e, axis=-1, keepdims=True), axis=0, keepdims=True)
        dx, dg = _rms_bwd_math(xf, gg, e * (1.0 / D))
        dx_ref[...] = dx
        dxb_ref[...] = dx.astype(BF16)
        dg_ref[...] += dg

    row = pl.BlockSpec((tr, D), lambda i: (i, 0))
    vec = pl.BlockSpec((1, D), lambda i: (0, 0))
    one = pl.BlockSpec((1, 1), lambda i: (0, 0))
    loss, dx, dxb, dg = pl.pallas_call(
        body, name=name, grid=(S // tr,), in_specs=[row, vec, row], out_specs=[one, row, row, vec],
        out_shape=[jax.ShapeDtypeStruct((1, 1), F32), jax.ShapeDtypeStruct((S, D), F32),
                   jax.ShapeDtypeStruct((S, D), BF16), jax.ShapeDtypeStruct((1, D), F32)],
        compiler_params=_cp(("arbitrary",)),
    )(x, g, target)
    return loss, (dx, dxb), dg


HALO = 16


def _shift_rows(main, prev_row, next_row):
    tr = main.shape[0]
    row = lax.broadcasted_iota(jnp.int32, main.shape, 0)
    up = jnp.where(row == 0, prev_row, pltpu.roll(main, 1, 0))
    dn = jnp.where(row == tr - 1, next_row, pltpu.roll(main, tr - 1, 0))
    return up, dn


def _halo_specs(tr, tn, S, col_of):
    hb = tr // HALO
    last = S // HALO - 1
    return [pl.BlockSpec((tr, tn), lambda j, i: (i, col_of(j))),
            pl.BlockSpec((HALO, tn), lambda j, i: (jnp.maximum(i * hb - 1, 0), col_of(j))),
            pl.BlockSpec((HALO, tn), lambda j, i: (jnp.minimum((i + 1) * hb, last), col_of(j)))]


def _halo_rows(main_ref, prev_ref, next_ref, i, n_i):
    main = main_ref[...].astype(F32)
    prev_row = prev_ref[HALO - 1:HALO, :].astype(F32) * (i > 0).astype(F32)
    next_row = next_ref[0:1, :].astype(F32) * (i < n_i - 1).astype(F32)
    up, dn = _shift_rows(main, prev_row, next_row)
    return up, main, dn


def _conv3(w_ref, b_ref, up, mid, dn):
    return b_ref[...] + w_ref[0:1, :] * up + w_ref[1:2, :] * mid + w_ref[2:3, :] * dn


def _ffn_tiles(S, F, rows=(512, 256, 128, 16)):
    return _pick(S, rows), _pick(F, (512, 256, 128))


def _sigmoid(x):
    return 0.5 * jnp.tanh(0.5 * x) + 0.5


def _conv_gate_fwd(u, conv_w, conv_b, name):
    S, F2 = u.shape
    F = F2 // 2
    tr, tn = _ffn_tiles(S, F)
    nj, ni = F // tn, S // tr

    def body(gm, gp, gn, vm, vp, vn, wg, wv, bg, bv, o_ref):
        i = pl.program_id(1)
        cg = _conv3(wg, bg, *_halo_rows(gm, gp, gn, i, ni))
        cv = _conv3(wv, bv, *_halo_rows(vm, vp, vn, i, ni))
        o_ref[...] = (cg * _sigmoid(cg) * cv).astype(o_ref.dtype)

    wspec = lambda off: pl.BlockSpec((3, tn), lambda j, i: (0, j + off))
    bspec = lambda off: pl.BlockSpec((1, tn), lambda j, i: (0, j + off))
    return pl.pallas_call(
        body, name=name, grid=(nj, ni),
        in_specs=[*_halo_specs(tr, tn, S, lambda j: j), *_halo_specs(tr, tn, S, lambda j: j + nj),
                  wspec(0), wspec(nj), bspec(0), bspec(nj)],
        out_specs=pl.BlockSpec((tr, tn), lambda j, i: (i, j)),
        out_shape=jax.ShapeDtypeStruct((S, F), BF16), compiler_params=_cp(("parallel", "parallel")),
    )(u, u, u, u, u, u, conv_w, conv_w, conv_b, conv_b)


def _conv_gate_bwd(u, da, conv_w, conv_b, name):
    S, F2 = u.shape
    F = F2 // 2
    tr, tn = _ffn_tiles(S, F)
    nj, ni = F // tn, S // tr

    def body(gm, gp, gn, vm, vp, vn, wg, wv, bg, bv, da_ref, dc_ref, dwg_ref, dwv_ref, dbg_ref, dbv_ref):
        i = pl.program_id(1)

        @pl.when(i == 0)
        def _():
            for r in (dwg_ref, dwv_ref, dbg_ref, dbv_ref):
                r[...] = jnp.zeros_like(r)

        g_rows = _halo_rows(gm, gp, gn, i, ni)
        v_rows = _halo_rows(vm, vp, vn, i, ni)
        cg = _conv3(wg, bg, *g_rows)
        cv = _conv3(wv, bv, *v_rows)
        sg = _sigmoid(cg)
        d = da_ref[...].astype(F32)
        dcv = d * (cg * sg)
        dcg = d * cv * (sg * (1.0 + cg * (1.0 - sg)))
        dc_ref[0] = dcg.astype(dc_ref.dtype)
        dc_ref[1] = dcv.astype(dc_ref.dtype)
        for t in range(3):
            dwg_ref[t:t + 1, :] += jnp.sum(dcg * g_rows[t], axis=0, keepdims=True)
            dwv_ref[t:t + 1, :] += jnp.sum(dcv * v_rows[t], axis=0, keepdims=True)
        dbg_ref[...] += jnp.sum(dcg, axis=0, keepdims=True)
        dbv_ref[...] += jnp.sum(dcv, axis=0, keepdims=True)

    wspec = lambda off: pl.BlockSpec((3, tn), lambda j, i: (0, j + off))
    bspec = lambda off: pl.BlockSpec((1, tn), lambda j, i: (0, j + off))
    tile = pl.BlockSpec((tr, tn), lambda j, i: (i, j))
    outs = pl.pallas_call(
        body, name=name, grid=(nj, ni),
        in_specs=[*_halo_specs(tr, tn, S, lambda j: j), *_halo_specs(tr, tn, S, lambda j: j + nj),
                  wspec(0), wspec(nj), bspec(0), bspec(nj), tile],
        out_specs=[pl.BlockSpec((2, tr, tn), lambda j, i: (0, i, j)), wspec(0), wspec(0), bspec(0), bspec(0)],
        out_shape=[jax.ShapeDtypeStruct((2, S, F), BF16),
                   jax.ShapeDtypeStruct((3, F), F32), jax.ShapeDtypeStruct((3, F), F32),
                   jax.ShapeDtypeStruct((1, F), F32), jax.ShapeDtypeStruct((1, F), F32)],
        compiler_params=_cp(("parallel", "arbitrary")),
    )(u, u, u, u, u, u, conv_w, conv_w, conv_b, conv_b, da)
    dc, dwg, dwv, dbg, dbv = outs
    return dc, jnp.concatenate([dwg, dwv], axis=1), jnp.concatenate([dbg, dbv], axis=1)


def _conv_transpose(dc, w, name):
    _, S, F = dc.shape
    tr, tn = _ffn_tiles(S, F, rows=(2048, 1024, 512, 256, 128, 16))
    nj, ni = F // tn, S // tr
    hb = tr // HALO
    last = S // HALO - 1

    def body(m, p, n, w_ref, o_ref):
        up, mid, dn = _halo_rows(m, p, n, pl.program_id(2), ni)
        o_ref[...] = (w_ref[0:1, :] * dn + w_ref[1:2, :] * mid + w_ref[2:3, :] * up).astype(o_ref.dtype)

    tile = pl.BlockSpec((None, tr, tn), lambda h, j, i: (h, i, j))
    return pl.pallas_call(
        body, name=name, grid=(2, nj, ni),
        in_specs=[tile,
                  pl.BlockSpec((None, HALO, tn), lambda h, j, i: (h, jnp.maximum(i * hb - 1, 0), j)),
                  pl.BlockSpec((None, HALO, tn), lambda h, j, i: (h, jnp.minimum((i + 1) * hb, last), j)),
                  pl.BlockSpec((3, tn), lambda h, j, i: (0, j + h * nj))],
        out_specs=tile, out_shape=jax.ShapeDtypeStruct((2, S, F), BF16),
        compiler_params=_cp(("parallel", "parallel", "parallel")),
    )(dc, dc, dc, w)


def _ffn_fwd(x, norm_g, w_up, conv_w, conv_b, w_down, tag, next_blocks=None):
    hn = _rmsnorm_fwd(x, norm_g, BF16, f"ffn_norm_{tag}")
    if next_blocks is None:
        u = _mm(hn, w_up, out_dtype=BF16, name=f"ffn_up_{tag}")
    else:
        u, over_ici = _mm(hn, w_up, out_dtype=BF16, name=f"ffn_up_{tag}", hosted=_gather_halves_ici(next_blocks))
    a = _conv_gate_fwd(u, conv_w, conv_b, f"ffn_gate_{tag}")
    if next_blocks is None:
        x_new, landed = _mm(a, w_down, name=f"ffn_down_{tag}", epi=_add_epi, epi_in=(x,)), None
    else:
        x_new, over_d2d = _mm(a, w_down, name=f"ffn_down_{tag}", epi=_add_epi, epi_in=(x,),
                              hosted=_gather_halves_d2d(over_ici))
        landed = (over_ici, over_d2d)
    return x_new, (x, hn, u, a), landed


def _ffn_bwd(dres, saved, norm_g, w_up, conv_w, conv_b, w_down, tag, reducer=None):
    x, hn, u, a = saved
    dres, dres_b = dres
    if reducer is None:
        da = _mm(dres_b, w_down, tb=True, out_dtype=BF16, name=f"ffn_da_{tag}")
    else:
        da, landed = _mm(dres_b, w_down, tb=True, out_dtype=BF16, name=f"ffn_da_{tag}", hosted=reducer.sibling())
    dw_down = _mm(a, dres_b, ta=True, name=f"ffn_dwdown_{tag}", pieces='row')
    dc, dconv_w, dconv_b = _conv_gate_bwd(u, da, conv_w, conv_b, f"ffn_gate_bwd_{tag}")
    du = _conv_transpose(dc, conv_w, f"ffn_convt_{tag}")
    if reducer is None:
        dhn = _mm(du, w_up, tb=True, name=f"ffn_dhn_{tag}")
        dw_up = _mm(hn, du, ta=True, name=f"ffn_dwup_{tag}", pieces='col')
        reduced = None
    else:
        dhn, landed = _mm(du, w_up, tb=True, name=f"ffn_dhn_{tag}", hosted=reducer.chips(landed))
        dw_up, landed = _mm(hn, du, ta=True, name=f"ffn_dwup_{tag}", pieces='col', hosted=reducer.swap(landed))
        reduced = reducer.finish(landed)
    dres, dg = _rmsnorm_bwd(x, norm_g, dhn, dres, f"ffn_norm_bwd_{tag}")
    return dres, (dg, dw_up, dconv_w, dconv_b, dw_down), reduced


ATTN_KEYS = 3 * ATTN_BLOCK


def _attn_window(i, S, reps):
    ks = pl.multiple_of(jnp.clip((i - 1) * ATTN_BLOCK, 0, S - ATTN_KEYS), ATTN_BLOCK)
    shape = (reps * ATTN_BLOCK, ATTN_KEYS)
    qpos = i * ATTN_BLOCK + lax.rem(lax.broadcasted_iota(jnp.int32, shape, 0), ATTN_BLOCK)
    kpos = ks + lax.broadcasted_iota(jnp.int32, shape, 1)
    arel = jnp.abs(kpos - qpos)
    return ks, arel.astype(F32), arel <= ATTN_BLOCK


ATTN_PAIRS = ATTN_GROUP // 2


def _stack_pairs(ref, kvh):
    c0 = kvh * ATTN_PAIRS * LANES
    return jnp.concatenate([ref[:, c0 + t * LANES:c0 + (t + 1) * LANES] for t in range(ATTN_PAIRS)], axis=0)


def _even_odd_operands(ref, rows, kvh):
    slab = ref[rows, (kvh // 2) * LANES:(kvh // 2 + 1) * LANES].astype(F32)
    other = pltpu.roll(slab, ATTN_DH, 1)
    low = lax.broadcasted_iota(jnp.int32, slab.shape, 1) < ATTN_DH
    lo_src, hi_src = (slab, other) if kvh % 2 == 0 else (other, slab)
    return jnp.where(low, lo_src, 0.0).astype(BF16), jnp.where(low, 0.0, hi_src).astype(BF16)


def _per_pair_rows(values):
    blk = lax.broadcasted_iota(jnp.int32, (ATTN_PAIRS * ATTN_BLOCK, 1), 0) // ATTN_BLOCK
    col = jnp.full(blk.shape, values[0], F32)
    for t in range(1, ATTN_PAIRS):
        col = jnp.where(blk == t, values[t], col)
    return col


def _attn_heads(kvh, parity, H):
    heads = [kvh * ATTN_GROUP + 2 * t + parity for t in range(ATTN_PAIRS)]
    return heads, [2.0 ** (-8.0 * (h + 1) / H) for h in heads]


def _attn_probs(q, k, slope, sink, arel, valid):
    s = lax.dot_general(q, k, (((1,), (1,)), ((), ())), preferred_element_type=F32) * (ATTN_DH ** -0.5)
    s = jnp.where(valid, s - slope * arel, -jnp.inf)
    m = jnp.maximum(jnp.max(s, axis=-1, keepdims=True), sink)
    p = jnp.exp(s - m)
    es = jnp.exp(sink - m)
    inv = 1.0 / (jnp.sum(p, axis=-1, keepdims=True) + es)
    return p * inv, es * inv


def _attn_specs(S, D):
    H = D // ATTN_DH
    KVW = (H // ATTN_GROUP) * ATTN_DH
    q_spec = pl.BlockSpec((ATTN_BLOCK, D), lambda i: (i, 0))
    k_spec = pl.BlockSpec((S, KVW), lambda i: (0, D // KVW))
    v_spec = pl.BlockSpec((S, KVW), lambda i: (0, D // KVW + 1))
    return H, KVW, q_spec, k_spec, v_spec


def _attn_fwd(qkv, sink, name):
    S = qkv.shape[0]
    D = qkv.shape[1] * ATTN_GROUP // (ATTN_GROUP + 2)
    H, KVW, q_spec, k_spec, v_spec = _attn_specs(S, D)

    def body(q_ref, k_ref, v_ref, sink_ref, o_ref):
        ks, arel, valid = _attn_window(pl.program_id(0), S, ATTN_PAIRS)
        rows = pl.ds(ks, ATTN_KEYS)
        for kvh in range(H // ATTN_GROUP):
            q = _stack_pairs(q_ref, kvh)
            out = None
            for parity, k, v in zip((0, 1), _even_odd_operands(k_ref, rows, kvh), _even_odd_operands(v_ref, rows, kvh)):
                heads, slopes = _attn_heads(kvh, parity, H)
                p, _ = _attn_probs(q, k, _per_pair_rows(slopes), _per_pair_rows([sink_ref[h] for h in heads]),
                                   arel, valid)
                part = jnp.dot(p.astype(BF16), v, preferred_element_type=F32)
                out = part if out is None else out + part
            c0 = kvh * ATTN_PAIRS * LANES
            for t in range(ATTN_PAIRS):
                o_ref[:, c0 + t * LANES:c0 + (t + 1) * LANES] = out[t * ATTN_BLOCK:(t + 1) * ATTN_BLOCK].astype(o_ref.dtype)

    return pl.pallas_call(
        body, name=name, grid=(S // ATTN_BLOCK,),
        in_specs=[q_spec, k_spec, v_spec, pl.BlockSpec(memory_space=pltpu.SMEM)],
        out_specs=q_spec, out_shape=jax.ShapeDtypeStruct((S, D), BF16),
        compiler_params=_cp(("parallel",)),
    )(qkv, qkv, qkv, sink)


def _attn_bwd(qkv, sink, do, name):
    S = qkv.shape[0]
    D = qkv.shape[1] * ATTN_GROUP // (ATTN_GROUP + 2)
    H, KVW, q_spec, k_spec, v_spec = _attn_specs(S, D)
    scale = ATTN_DH ** -0.5
    nt = (((1,), (1,)), ((), ()))
    tn = (((0,), (0,)), ((), ()))

    def body(q_ref, k_ref, v_ref, sink_ref, do_ref, dq_ref, dk_ref, dv_ref, ds_ref):
        @pl.when(pl.program_id(0) == 0)
        def _():
            dk_ref[...] = jnp.zeros_like(dk_ref)
            dv_ref[...] = jnp.zeros_like(dv_ref)
            ds_ref[...] = jnp.zeros_like(ds_ref)

        ks, arel, valid = _attn_window(pl.program_id(0), S, ATTN_PAIRS)
        rows = pl.ds(ks, ATTN_KEYS)
        low = lax.broadcasted_iota(jnp.int32, (ATTN_KEYS, LANES), 1) < ATTN_DH
        for kvh in range(H // ATTN_GROUP):
            q = _stack_pairs(q_ref, kvh)
            d_o = _stack_pairs(do_ref, kvh)
            dq = None
            dk_halves, dv_halves = [], []
            for parity, k, v in zip((0, 1), _even_odd_operands(k_ref, rows, kvh), _even_odd_operands(v_ref, rows, kvh)):
                heads, slopes = _attn_heads(kvh, parity, H)
                p, p_sink = _attn_probs(q, k, _per_pair_rows(slopes), _per_pair_rows([sink_ref[h] for h in heads]),
                                        arel, valid)
                dp = lax.dot_general(d_o, v, nt, preferred_element_type=F32)
                delta = jnp.sum(p * dp, axis=-1, keepdims=True)
                dsc = (p * (dp - delta)).astype(BF16)
                dsink = -p_sink * delta
                for t, h in enumerate(heads):
                    ds_ref[:, h:h + 1] += dsink[t * ATTN_BLOCK:(t + 1) * ATTN_BLOCK]
                part = jnp.dot(dsc, k, preferred_element_type=F32)
                dq = part if dq is None else dq + part
                dk_halves.append(lax.dot_general(dsc, q, tn, preferred_element_type=F32))
                dv_halves.append(lax.dot_general(p.astype(BF16), d_o, tn, preferred_element_type=F32))
            c0 = kvh * ATTN_PAIRS * LANES
            for t in range(ATTN_PAIRS):
                dq_ref[:, c0 + t * LANES:c0 + (t + 1) * LANES] = (
                    dq[t * ATTN_BLOCK:(t + 1) * ATTN_BLOCK] * scale).astype(dq_ref.dtype)
            slab = slice((kvh // 2) * LANES, (kvh // 2 + 1) * LANES)
            mine = low if kvh % 2 == 0 else jnp.logical_not(low)
            for ref, (even, odd), mult in ((dk_ref, dk_halves, scale), (dv_ref, dv_halves, 1.0)):
                both = jnp.where(low, even, odd)
                total = both + pltpu.roll(both, ATTN_DH, 1)
                ref[rows, slab] += jnp.where(mine, total * mult, 0.0)

    kv_out = pl.BlockSpec((S, KVW), lambda i: (0, 0))
    return pl.pallas_call(
        body, name=name, grid=(S // ATTN_BLOCK,),
        in_specs=[q_spec, k_spec, v_spec, pl.BlockSpec(memory_space=pltpu.SMEM), q_spec],
        out_specs=[q_spec, kv_out, kv_out, pl.BlockSpec((ATTN_BLOCK, H), lambda i: (0, 0))],
        out_shape=[jax.ShapeDtypeStruct((S, D), BF16), jax.ShapeDtypeStruct((S, KVW), F32),
                   jax.ShapeDtypeStruct((S, KVW), F32), jax.ShapeDtypeStruct((ATTN_BLOCK, H), F32)],
        compiler_params=_cp(("arbitrary",)),
    )(qkv, qkv, qkv, sink, do)


def _mm_hosting(exchange, a, b, **kw):
    if exchange is None:
        return _mm(a, b, **kw), None
    return _mm(a, b, hosted=exchange, **kw)


def _attn_layer_fwd(x, norm_g, w_qkv, w_o, sink, tag, next_blocks=None):
    hn = _rmsnorm_fwd(x, norm_g, BF16, f"attn_norm_{tag}")
    qkv, over_ici = _mm_hosting(None if next_blocks is None else _gather_halves_ici(next_blocks), hn, w_qkv,
                                out_dtype=BF16, name=f"attn_qkv_{tag}")
    o = _attn_fwd(qkv, sink, f"attn_core_{tag}")
    x_new, over_d2d = _mm_hosting(None if next_blocks is None else _gather_halves_d2d(over_ici), o, w_o,
                                  name=f"attn_out_{tag}", epi=_add_epi, epi_in=(x,))
    return x_new, (x, hn, qkv, o), None if next_blocks is None else (over_ici, over_d2d)


def _attn_layer_bwd(dres, saved, norm_g, w_qkv, w_o, sink, tag, reducer=None):
    x, hn, qkv, o = saved
    dres, dres_b = dres
    do, landed = _mm_hosting(reducer and reducer.sibling(), dres_b, w_o, tb=True, out_dtype=BF16, name=f"attn_do_{tag}")
    dw_o = _mm(o, dres_b, ta=True, name=f"attn_dwo_{tag}", pieces='row')
    dq, dk, dv, dsink = _attn_bwd(qkv, sink, do, f"attn_core_bwd_{tag}")
    dqkv = jnp.concatenate([dq, dk.astype(BF16), dv.astype(BF16)], axis=1)
    dhn, landed = _mm_hosting(reducer and reducer.chips(landed), dqkv, w_qkv, tb=True, name=f"attn_dhn_{tag}")
    dw_qkv, landed = _mm_hosting(reducer and reducer.swap(landed), hn, dqkv, ta=True, name=f"attn_dwqkv_{tag}",
                                 pieces='col')
    dres, dg = _rmsnorm_bwd(x, norm_g, dhn, dres, f"attn_norm_bwd_{tag}")
    return dres, (dg, dw_qkv, dw_o, jnp.sum(dsink, axis=0)), reducer and reducer.finish(landed)


MLA_W = 2 * LANES
MLA_DPAD = 2 * MLA_LORA + LANES
MLA_SCALE = (MLA_NOPE + MLA_ROPE) ** -0.5
MLA_TILES = (1024, 512, 256, 128)
LOG2E = math.log2(math.e)
LN2 = math.log(2.0)


def _rope_tables(S):
    half = MLA_ROPE // 2
    pos = jnp.arange(S, dtype=F32)
    inv = ROPE_THETA ** (-jnp.arange(half, dtype=F32) / half)
    ang = pos[:, None] * inv[None, :]
    cos, sin = jnp.cos(ang), jnp.sin(ang)
    z = jnp.zeros((S, LANES - 2 * half), F32)
    zh = jnp.zeros((S, half), F32)
    return (jnp.concatenate([cos, cos, z], axis=1), jnp.concatenate([-sin, zh, z], axis=1),
            jnp.concatenate([zh, sin, z], axis=1))


def _rope(t, ca, sb, sc):
    return t * ca + pltpu.roll(t, 96, 1) * sb + pltpu.roll(t, 32, 1) * sc


def _rope_t(d, ca, sb, sc):
    return d * ca + pltpu.roll(d * sb, 32, 1) + pltpu.roll(d * sc, 96, 1)


def _rms(xf, g):
    return (xf * lax.rsqrt(jnp.mean(xf * xf, axis=-1, keepdims=True) + RMS_EPS)) * g


def _mla_prep(d, qn, kvn, tabs, name):
    S = d.shape[0]
    tr = _pick(S, (256, 128, 8))
    L = MLA_LORA

    def body(d_ref, qn_ref, kvn_ref, ca, sb, sc, cq_ref, ckv_ref, kr_ref):
        cq_ref[...] = _rms(d_ref[:, :L], qn_ref[...]).astype(BF16)
        ckv_ref[...] = _rms(d_ref[:, L:2 * L], kvn_ref[...]).astype(BF16)
        kr_ref[...] = _rope(d_ref[:, 2 * L:], ca[...], sb[...], sc[...]).astype(BF16)

    row = lambda w: pl.BlockSpec((tr, w), lambda i: (i, 0))
    vec = pl.BlockSpec((1, L), lambda i: (0, 0))
    return pl.pallas_call(
        body, name=name, grid=(S // tr,),
        in_specs=[row(MLA_DPAD), vec, vec, row(LANES), row(LANES), row(LANES)],
        out_specs=[row(L), row(L), row(LANES)],
        out_shape=[jax.ShapeDtypeStruct((S, L), BF16), jax.ShapeDtypeStruct((S, L), BF16),
                   jax.ShapeDtypeStruct((S, LANES), BF16)],
        compiler_params=_cp(("parallel",)),
    )(d, qn, kvn, *tabs)


def _mla_prep_bwd(d, qn, kvn, tabs, dcq, dckv, dkr_h, name):
    S = d.shape[0]
    H = dkr_h.shape[0]
    tr = _pick(S, (256, 128, 8))
    L = MLA_LORA

    def body(d_ref, qn_ref, kvn_ref, ca, sb, sc, dcq_ref, dckv_ref, dkr_ref, dd_ref, dqn_ref, dkvn_ref):
        @pl.when(pl.program_id(0) == 0)
        def _():
            dqn_ref[...] = jnp.zeros_like(dqn_ref)
            dkvn_ref[...] = jnp.zeros_like(dkvn_ref)

        dx, dg = _rms_bwd_math(d_ref[:, :L], qn_ref[...], dcq_ref[...])
        dd_ref[:, :L] = dx.astype(BF16)
        dqn_ref[...] += dg
        dx, dg = _rms_bwd_math(d_ref[:, L:2 * L], kvn_ref[...], dckv_ref[...])
        dd_ref[:, L:2 * L] = dx.astype(BF16)
        dkvn_ref[...] += dg
        dkr = dkr_ref[0]
        for h in range(1, H):
            dkr = dkr + dkr_ref[h]
        dd_ref[:, 2 * L:] = _rope_t(dkr, ca[...], sb[...], sc[...]).astype(BF16)

    row = lambda w: pl.BlockSpec((tr, w), lambda i: (i, 0))
    vec = pl.BlockSpec((1, L), lambda i: (0, 0))
    return pl.pallas_call(
        body, name=name, grid=(S // tr,),
        in_specs=[row(MLA_DPAD), vec, vec, row(LANES), row(LANES), row(LANES), row(L), row(L),
                  pl.BlockSpec((H, tr, LANES), lambda i: (0, i, 0))],
        out_specs=[row(MLA_DPAD), vec, vec],
        out_shape=[jax.ShapeDtypeStruct((S, MLA_DPAD), BF16), jax.ShapeDtypeStruct((1, L), F32),
                   jax.ShapeDtypeStruct((1, L), F32)],
        compiler_params=_cp(("arbitrary",)),
    )(d, qn, kvn, *tabs, dcq, dckv, dkr_h)


def _heads_proj(a, w, out_dtype, name):
    S, K = a.shape
    H, _, n = w.shape
    tm = _pick(S, (1024, 512, 256, 128))
    return _gmm(a, w, grid=(S // tm, H, 1),
                a_spec=pl.BlockSpec((tm, K), lambda m, h, k: (m, 0)),
                b_spec=pl.BlockSpec((None, K, n), lambda m, h, k: (h, 0, 0)),
                o_spec=pl.BlockSpec((None, tm, n), lambda m, h, k: (h, m, 0)),
                out_shape=jax.ShapeDtypeStruct((H, S, n), out_dtype),
                dims=(((1,), (0,)), ((), ())), kax=2, acc_shape=(tm, n), name=name)


def _heads_proj_dx(dy, w, name):
    H, S, n = dy.shape
    K = w.shape[1]
    tm = _pick(S, (1024, 512, 256, 128))
    return _gmm(dy, w, grid=(S // tm, 1, H),
                a_spec=pl.BlockSpec((None, tm, n), lambda m, j, h: (h, m, 0)),
                b_spec=pl.BlockSpec((None, K, n), lambda m, j, h: (h, 0, 0)),
                o_spec=pl.BlockSpec((tm, K), lambda m, j, h: (m, 0)),
                out_shape=jax.ShapeDtypeStruct((S, K), F32),
                dims=(((1,), (1,)), ((), ())), kax=2, acc_shape=(tm, K), name=name)


def _heads_proj_dw(a, dy, name):
    S, K = a.shape
    H, _, n = dy.shape
    tk = _pick(S, (512, 256, 128))
    return _gmm(a, dy, grid=(H, 1, S // tk),
                a_spec=pl.BlockSpec((tk, K), lambda h, j, k: (k, 0)),
                b_spec=pl.BlockSpec((None, tk, n), lambda h, j, k: (h, k, 0)),
                o_spec=pl.BlockSpec((None, K, n), lambda h, j, k: (h, 0, 0)),
                out_shape=jax.ShapeDtypeStruct((H, K, n), F32),
                dims=(((0,), (0,)), ((), ())), kax=2, acc_shape=(K, n), name=name)


def _mla_rope_q(q_ext, tabs, bwd, name):
    H, S, _ = q_ext.shape
    tr = _pick(S, (512, 256, 128, 8))
    mult = 1.0 if bwd else MLA_SCALE * LOG2E

    def body(q_ref, ca, sb, sc, o_ref):
        o_ref[:, :LANES] = (q_ref[:, :LANES].astype(F32) * mult).astype(BF16)
        fn = _rope_t if bwd else _rope
        o_ref[:, LANES:] = (fn(q_ref[:, LANES:].astype(F32), ca[...], sb[...], sc[...]) * mult).astype(BF16)

    blk = pl.BlockSpec((None, tr, MLA_W), lambda i, h: (h, i, 0))
    tab = pl.BlockSpec((tr, LANES), lambda i, h: (i, 0))
    return pl.pallas_call(
        body, name=name, grid=(S // tr, H), in_specs=[blk, tab, tab, tab], out_specs=blk,
        out_shape=jax.ShapeDtypeStruct((H, S, MLA_W), BF16), compiler_params=_cp(("parallel", "parallel")),
    )(q_ext, *tabs)


def _col_to_row(col):
    n = col.shape[0]
    eye = lax.broadcasted_iota(jnp.int32, (n, n), 0) == lax.broadcasted_iota(jnp.int32, (n, n), 1)
    return jnp.sum(jnp.where(eye, col, 0.0), axis=0, keepdims=True)


def _mla_flash_fwd(q, kv, kr, name, tq=None, tk=None, unroll=1):
    H, S, _ = q.shape
    tq = tq or _pick(S, MLA_TILES)
    tk = tk or _pick(S, MLA_TILES)

    def body(q_ref, kv_ref, kr_ref, o_ref, lse_ref, kbuf, vbuf):
        @pl.when(pl.program_id(1) == 0)
        def _():
            kbuf[:, :LANES] = kv_ref[:, :LANES]
            kbuf[:, LANES:] = kr_ref[...]
            vbuf[:, :LANES] = kv_ref[:, LANES:]
            vbuf[:, LANES:] = jnp.ones((S, LANES), BF16)

        qv = q_ref[...]

        def step(c, carry):
            m, acc = carry
            rows = pl.ds(pl.multiple_of(c * tk, tk), tk)
            s = lax.dot_general(qv, kbuf[rows, :], (((1,), (1,)), ((), ())), preferred_element_type=F32)
            m_new = jnp.maximum(m, jnp.max(s, axis=-1, keepdims=True))
            p = jnp.exp2(s - m_new).astype(BF16)
            acc = jnp.exp2(m - m_new) * acc + jnp.dot(p, vbuf[rows, :], preferred_element_type=F32)
            return m_new, acc

        init = (jnp.full((tq, 1), -jnp.inf, F32), jnp.zeros((tq, MLA_W), F32))
        m, acc = lax.fori_loop(0, S // tk, step, init, unroll=unroll)
        l = acc[:, LANES:LANES + 1]
        o_ref[...] = (acc[:, :LANES] / l).astype(o_ref.dtype)
        lse_ref[...] = _col_to_row(m + jnp.log2(l))

    return pl.pallas_call(
        body, name=name, grid=(H, S // tq),
        in_specs=[pl.BlockSpec((None, tq, MLA_W), lambda h, i: (h, i, 0)),
                  pl.BlockSpec((None, S, MLA_W), lambda h, i: (h, 0, 0)),
                  pl.BlockSpec((S, LANES), lambda h, i: (0, 0))],
        out_specs=[pl.BlockSpec((tq, MLA_V), lambda h, i: (i, h)),
                   pl.BlockSpec((None, 1, tq), lambda h, i: (h, 0, i))],
        out_shape=[jax.ShapeDtypeStruct((S, H * MLA_V), BF16), jax.ShapeDtypeStruct((H, 1, S), F32)],
        scratch_shapes=[pltpu.VMEM((S, MLA_W), BF16), pltpu.VMEM((S, MLA_W), BF16)],
        compiler_params=_cp(("parallel", "arbitrary")),
    )(q, kv, kr)


def _mla_delta(o, do, H, name):
    S = o.shape[0]
    tq = _pick(S, (512, 256, 128))

    def body(o_ref, do_ref, d_ref):
        prod = o_ref[...].astype(F32) * do_ref[...].astype(F32)
        d_ref[...] = _col_to_row(jnp.sum(prod, axis=-1, keepdims=True))

    blk = pl.BlockSpec((tq, MLA_V), lambda i, h: (i, h))
    return pl.pallas_call(
        body, name=name, grid=(S // tq, H), in_specs=[blk, blk],
        out_specs=pl.BlockSpec((None, 1, tq), lambda i, h: (h, 0, i)),
        out_shape=jax.ShapeDtypeStruct((H, 1, S), F32), compiler_params=_cp(("parallel", "parallel")),
    )(o, do)


def _mla_flash_bwd(q, kv, kr, do, lse, delta, name, tq=None, tkv=None, unroll=1):
    H, S, _ = q.shape
    tq = tq or _pick(S, MLA_TILES)
    tkv = tkv or _pick(S, MLA_TILES)

    def body(q_ref, kv_ref, kr_ref, do_ref, lse_ref, dl_ref, dq_ref, dkv_ref, dkr_ref):
        @pl.when(pl.program_id(1) == 0)
        def _():
            dq_ref[...] = jnp.zeros_like(dq_ref)

        v = kv_ref[:, LANES:]
        k = jnp.concatenate([kv_ref[:, :LANES], kr_ref[...]], axis=1)

        def step(c, carry):
            dk, dv = carry
            start = pl.multiple_of(c * tq, tq)
            rows = pl.ds(start, tq)
            qv = q_ref[rows, :]
            d_o = do_ref[rows, :]
            s_t = lax.dot_general(k, qv, (((1,), (1,)), ((), ())), preferred_element_type=F32)
            p_t = jnp.exp2(s_t - lse_ref[:, rows])
            dv = dv + jnp.dot(p_t.astype(BF16), d_o, preferred_element_type=F32)
            dp_t = lax.dot_general(v, d_o, (((1,), (1,)), ((), ())), preferred_element_type=F32)
            ds_t = (p_t * (dp_t - dl_ref[:, rows])).astype(BF16)
            dk = dk + jnp.dot(ds_t, qv, preferred_element_type=F32)
            dq_ref[rows, :] += lax.dot_general(ds_t, k, (((0,), (0,)), ((), ())),
                                               preferred_element_type=F32) * MLA_SCALE
            return dk, dv

        dk, dv = lax.fori_loop(0, S // tq, step, (jnp.zeros((tkv, MLA_W), F32), jnp.zeros((tkv, MLA_V), F32)),
                               unroll=unroll)
        dkv_ref[:, :LANES] = (dk[:, :LANES] * LN2).astype(BF16)
        dkv_ref[:, LANES:] = dv.astype(BF16)
        dkr_ref[...] = dk[:, LANES:] * LN2

    stat = pl.BlockSpec((None, 1, S), lambda h, j: (h, 0, 0))
    return pl.pallas_call(
        body, name=name, grid=(H, S // tkv),
        in_specs=[pl.BlockSpec((None, S, MLA_W), lambda h, j: (h, 0, 0)),
                  pl.BlockSpec((None, tkv, MLA_W), lambda h, j: (h, j, 0)),
                  pl.BlockSpec((tkv, LANES), lambda h, j: (j, 0)),
                  pl.BlockSpec((S, MLA_V), lambda h, j: (0, h)), stat, stat],
        out_specs=[pl.BlockSpec((None, S, MLA_W), lambda h, j: (h, 0, 0)),
                   pl.BlockSpec((None, tkv, MLA_W), lambda h, j: (h, j, 0)),
                   pl.BlockSpec((None, tkv, LANES), lambda h, j: (h, j, 0))],
        out_shape=[jax.ShapeDtypeStruct((H, S, MLA_W), F32), jax.ShapeDtypeStruct((H, S, MLA_W), BF16),
                   jax.ShapeDtypeStruct((H, S, LANES), F32)],
        compiler_params=_cp(("parallel", "arbitrary")),
    )(q, kv, kr, do, lse, delta)


def _mla_weights(w_dqkv, w_uq, w_ukv):
    H = MLA_HEADS
    wd = jnp.pad(w_dqkv, ((0, 0), (0, MLA_DPAD - w_dqkv.shape[1])))
    wq = w_uq.reshape(MLA_LORA, H, MLA_NOPE + MLA_ROPE)
    wq = jnp.pad(wq, ((0, 0), (0, 0), (0, MLA_W - wq.shape[2]))).transpose(1, 0, 2)
    wkv = w_ukv.reshape(MLA_LORA, H, MLA_NOPE + MLA_V).transpose(1, 0, 2)
    return wd, wq, wkv


def _mla_layer_fwd(x, norm_g, wd, wq, wkv, w_o, qn, kvn, tag):
    S = x.shape[0]
    tabs = _rope_tables(S)
    hn = _rmsnorm_fwd(x, norm_g, BF16, f"mla_norm_{tag}")
    d = _mm(hn, wd, name=f"mla_down_{tag}")
    cq, ckv, kr = _mla_prep(d, qn, kvn, tabs, f"mla_prep_{tag}")
    q = _mla_rope_q(_heads_proj(cq, wq, F32, f"mla_uq_{tag}"), tabs, False, f"mla_ropeq_{tag}")
    kv = _heads_proj(ckv, wkv, BF16, f"mla_ukv_{tag}")
    o, lse = _mla_flash_fwd(q, kv, kr, f"mla_flash_{tag}")
    x_new = _mm(o, w_o, name=f"mla_out_{tag}", epi=_add_epi, epi_in=(x,))
    return x_new, (x, hn, d, cq, ckv, kr, q, kv, o, lse)


def _mla_layer_bwd(dres, saved, norm_g, wd, wq, wkv, w_o, qn, kvn, tag):
    x, hn, d, cq, ckv, kr, q, kv, o, lse = saved
    S = x.shape[0]
    H = MLA_HEADS
    tabs = _rope_tables(S)
    dres, dres_b = dres
    do = _mm(dres_b, w_o, tb=True, out_dtype=BF16, name=f"mla_do_{tag}")
    dw_o = _mm(o, dres_b, ta=True, name=f"mla_dwo_{tag}", pieces='row')
    delta = _mla_delta(o, do, H, f"mla_delta_{tag}")
    dq, dkv, dkr_h = _mla_flash_bwd(q, kv, kr, do, lse, delta, f"mla_flash_bwd_{tag}")
    dq_ext = _mla_rope_q(dq, tabs, True, f"mla_ropeq_bwd_{tag}")
    dwq = _heads_proj_dw(cq, dq_ext, f"mla_dwuq_{tag}")
    dcq = _heads_proj_dx(dq_ext, wq, f"mla_dcq_{tag}")
    dwkv = _heads_proj_dw(ckv, dkv, f"mla_dwukv_{tag}")
    dckv = _heads_proj_dx(dkv, wkv, f"mla_dckv_{tag}")
    dd, dqn, dkvn = _mla_prep_bwd(d, qn, kvn, tabs, dcq, dckv, dkr_h, f"mla_prep_bwd_{tag}")
    dhn = _mm(dd, wd, tb=True, name=f"mla_dhn_{tag}")
    dwd = _mm(hn, dd, ta=True, name=f"mla_dwd_{tag}")
    dres, dg = _rmsnorm_bwd(x, norm_g, dhn, dres, f"mla_norm_bwd_{tag}")
    dw_dqkv = dwd[:, :2 * MLA_LORA + MLA_ROPE]
    dw_uq = dwq.transpose(1, 0, 2)[:, :, :MLA_NOPE + MLA_ROPE].reshape(MLA_LORA, -1)
    dw_ukv = dwkv.transpose(1, 0, 2).reshape(MLA_LORA, -1)
    return dres, (dg, dw_dqkv, dqn, dkvn, dw_uq, dw_ukv, dw_o)


S5_CB = LANES
S5_SB = (S5_CB // SSM_GROUP_CH) * SSM_STATE
S5_ROWS = 1024


def _s5_disc(a_re, a_im, ls, b_re, b_im):
    step = jnp.exp(ls)
    mag = jnp.exp(step * a_re)
    lb_re = mag * jnp.cos(step * a_im)
    lb_im = mag * jnp.sin(step * a_im)
    n_re, n_im = lb_re - 1.0, lb_im
    den = a_re * a_re + a_im * a_im
    coef_re = (n_re * a_re + n_im * a_im) / den
    coef_im = (n_im * a_re - n_re * a_im) / den
    return lb_re, lb_im, coef_re * b_re - coef_im * b_im, coef_re * b_im + coef_im * b_re


def _s5_disc_fwd(a_re, a_im, ls, b_re, b_im, name):
    GN = a_re.shape[-1]

    def body(ar, ai, l, br, bi, o_lr, o_li, o_br, o_bi):
        for o, v in zip((o_lr, o_li, o_br, o_bi), _s5_disc(ar[...], ai[...], l[...], br[...], bi[...])):
            o[...] = v

    vec = pl.BlockSpec((None, 1, GN), lambda d: (d, 0, 0))
    mat = pl.BlockSpec((None, SSM_GROUP_CH, GN), lambda d: (d, 0, 0))
    sv = jax.ShapeDtypeStruct(a_re.shape, F32)
    sm = jax.ShapeDtypeStruct(b_re.shape, F32)
    return pl.pallas_call(body, name=name, grid=(2,), in_specs=[vec, vec, vec, mat, mat],
                          out_specs=[vec, vec, mat, mat], out_shape=[sv, sv, sm, sm],
                          compiler_params=_cp(("parallel",)))(a_re, a_im, ls, b_re, b_im)


def _s5_disc_bwd(a_re, a_im, ls, b_re, b_im, d_lr, d_li, d_br, d_bi, name):
    GN = a_re.shape[-1]

    def body(ar, ai, l, br, bi, g_lr, g_li, g_br, g_bi, o_ar, o_ai, o_l, o_br, o_bi):
        _, vjp = jax.vjp(_s5_disc, ar[...], ai[...], l[...], br[...], bi[...])
        for o, v in zip((o_ar, o_ai, o_l, o_br, o_bi), vjp((g_lr[...], g_li[...], g_br[...], g_bi[...]))):
            o[...] = v

    vec = pl.BlockSpec((None, 1, GN), lambda d: (d, 0, 0))
    mat = pl.BlockSpec((None, SSM_GROUP_CH, GN), lambda d: (d, 0, 0))
    sv = jax.ShapeDtypeStruct(a_re.shape, F32)
    sm = jax.ShapeDtypeStruct(b_re.shape, F32)
    return pl.pallas_call(body, name=name, grid=(2,), in_specs=[vec, vec, vec, mat, mat, vec, vec, mat, mat],
                          out_specs=[vec, vec, vec, mat, mat], out_shape=[sv, sv, sv, sm, sm],
                          compiler_params=_cp(("parallel",)))(a_re, a_im, ls, b_re, b_im, d_lr, d_li, d_br, d_bi)


def _cmul(ar, ai, br, bi):
    return ar * br - ai * bi, ar * bi + ai * br


def _segment_carries(lr, li, er, ei, n_steps, reverse):
    pr, pi = lr, li
    for _ in range(int(math.log2(n_steps))):
        pr, pi = _cmul(pr, pi, pr, pi)
    row = lax.broadcasted_iota(jnp.int32, er.shape, 0)
    edge = (SSM_SEGMENTS - 1) if reverse else 0
    shift = (SSM_SEGMENTS - 1) if reverse else 1
    cr = jnp.zeros_like(er)
    ci = jnp.zeros_like(ei)
    for _ in range(SSM_SEGMENTS - 1):
        tr_, ti_ = _cmul(pr, pi, cr, ci)
        cr = jnp.where(row == edge, 0.0, pltpu.roll(tr_ + er, shift, 0))
        ci = jnp.where(row == edge, 0.0, pltpu.roll(ti_ + ei, shift, 0))
    return cr, ci


def _s5_geometry(S, D):
    assert S % SSM_SEGMENTS == 0 and D % S5_CB == 0
    n_steps = S // SSM_SEGMENTS
    assert n_steps & (n_steps - 1) == 0, "segment length must be a power of two"
    rows = min(S5_ROWS, S)
    return n_steps, rows, S // rows, D // S5_CB


def _s5_scan(u, b_re, b_im, c_re, c_im, lam_re, lam_im, ends, descending, name):
    S, D = u.shape
    n_steps, rows, nch, ncb = _s5_geometry(S, D)
    full = ends is not None
    GN = ncb * S5_SB

    def body(*refs):
        if full:
            (u_ref, br_ref, bi_ref, cr_ref, ci_ref, lr_ref, li_ref, er_ref, ei_ref,
             xr_ref, xi_ref, y_ref, st_r, st_i, buf_r, buf_i) = refs
        else:
            u_ref, br_ref, bi_ref, lr_ref, li_ref, er_ref, ei_ref, st_r, st_i, buf_r, buf_i = refs
        lr = jnp.broadcast_to(lr_ref[...], (SSM_SEGMENTS, S5_SB))
        li = jnp.broadcast_to(li_ref[...], (SSM_SEGMENTS, S5_SB))

        @pl.when(pl.program_id(1) == 0)
        def _():
            if full:
                st_r[...], st_i[...] = _segment_carries(lr, li, er_ref[...], ei_ref[...], n_steps, descending)
            else:
                st_r[...] = jnp.zeros_like(st_r)
                st_i[...] = jnp.zeros_like(st_i)

        ub = u_ref[...].astype(BF16)
        buf_r[...] = jnp.dot(ub, br_ref[...], preferred_element_type=F32)
        buf_i[...] = jnp.dot(ub, bi_ref[...], preferred_element_type=F32)

        n_it = rows // SSM_SEGMENTS

        def step(i, carry):
            sr, si = carry
            i = n_it - 1 - i if descending else i
            r = pl.ds(pl.multiple_of(i * SSM_SEGMENTS, SSM_SEGMENTS), SSM_SEGMENTS)
            if full:
                xr_ref[r, :] = sr
                xi_ref[r, :] = si
            nr = lr * sr - li * si + buf_r[r, :]
            ni = lr * si + li * sr + buf_i[r, :]
            if full:
                buf_r[r, :] = nr
                buf_i[r, :] = ni
            return nr, ni

        sr, si = lax.fori_loop(0, n_it, step, (st_r[...], st_i[...]))
        st_r[...] = sr
        st_i[...] = si
        if full:
            y_ref[...] = (jnp.dot(buf_r[...].astype(BF16), cr_ref[...], preferred_element_type=F32)
                          - jnp.dot(buf_i[...].astype(BF16), ci_ref[...], preferred_element_type=F32))
        else:
            er_ref[...] = sr
            ei_ref[...] = si

    chunk = (lambda c: nch - 1 - c) if descending else (lambda c: c)
    u_spec = pl.BlockSpec((rows, S5_CB), lambda b, c: (chunk(c), b))
    bmat = pl.BlockSpec((None, S5_CB, S5_SB), lambda b, c: (b, 0, 0))
    cmat = pl.BlockSpec((None, S5_SB, S5_CB), lambda b, c: (b, 0, 0))
    lvec = pl.BlockSpec((1, S5_SB), lambda b, c: (0, b))
    evec = pl.BlockSpec((SSM_SEGMENTS, S5_SB), lambda b, c: (0, b))
    xblk = pl.BlockSpec((rows, S5_SB), lambda b, c: (chunk(c), b))
    scratch = [pltpu.VMEM((SSM_SEGMENTS, S5_SB), F32)] * 2 + [pltpu.VMEM((rows, S5_SB), F32)] * 2
    e_shape = jax.ShapeDtypeStruct((SSM_SEGMENTS, GN), F32)
    if full:
        x_shape = jax.ShapeDtypeStruct((S, GN), F32)
        return pl.pallas_call(
            body, name=name, grid=(ncb, nch),
            in_specs=[u_spec, bmat, bmat, cmat, cmat, lvec, lvec, evec, evec],
            out_specs=[xblk, xblk, u_spec], out_shape=[x_shape, x_shape, jax.ShapeDtypeStruct((S, D), F32)],
            scratch_shapes=scratch, compiler_params=_cp(("parallel", "arbitrary")),
        )(u, b_re, b_im, c_re, c_im, lam_re, lam_im, *ends)
    return pl.pallas_call(
        body, name=name, grid=(ncb, nch), in_specs=[u_spec, bmat, bmat, lvec, lvec],
        out_specs=[evec, evec], out_shape=[e_shape, e_shape],
        scratch_shapes=scratch, compiler_params=_cp(("parallel", "arbitrary")),
    )(u, b_re, b_im, lam_re, lam_im)


def _s5_scan_bwd(dy, u, xp, b_re, b_im, c_re, c_im, lam_re, lam_im, starts, descending, name):
    S, D = dy.shape
    n_steps, rows, nch, ncb = _s5_geometry(S, D)
    full = starts is not None
    GN = ncb * S5_SB
    nt = (((1,), (1,)), ((), ()))
    tn = (((0,), (0,)), ((), ()))

    def body(*refs):
        if full:
            (dy_ref, u_ref, xr_ref, xi_ref, br_ref, bi_ref, cr_ref, ci_ref, lr_ref, li_ref, gr_ref, gi_ref,
             du_ref, dbr_ref, dbi_ref, dcr_ref, dci_ref, dlr_ref, dli_ref, st_r, st_i, buf_r, buf_i) = refs
        else:
            dy_ref, cr_ref, ci_ref, lr_ref, li_ref, gr_ref, gi_ref, st_r, st_i, buf_r, buf_i = refs
        lr = jnp.broadcast_to(lr_ref[...], (SSM_SEGMENTS, S5_SB))
        li = jnp.broadcast_to(li_ref[...], (SSM_SEGMENTS, S5_SB))

        @pl.when(pl.program_id(1) == 0)
        def _():
            if full:
                st_r[...], st_i[...] = _segment_carries(lr, -li, gr_ref[...], gi_ref[...], n_steps, descending)
                for r in (dbr_ref, dbi_ref, dcr_ref, dci_ref, dlr_ref, dli_ref):
                    r[...] = jnp.zeros_like(r)
            else:
                st_r[...] = jnp.zeros_like(st_r)
                st_i[...] = jnp.zeros_like(st_i)

        dyb = dy_ref[...].astype(BF16)
        buf_r[...] = lax.dot_general(dyb, cr_ref[...], nt, preferred_element_type=F32)
        buf_i[...] = -lax.dot_general(dyb, ci_ref[...], nt, preferred_element_type=F32)
        n_it = rows // SSM_SEGMENTS

        def step(j, carry):
            gr, gi = carry
            j = n_it - 1 - j if descending else j
            r = pl.ds(pl.multiple_of(j * SSM_SEGMENTS, SSM_SEGMENTS), SSM_SEGMENTS)
            nr = lr * gr + li * gi + buf_r[r, :]
            ni = lr * gi - li * gr + buf_i[r, :]
            if full:
                buf_r[r, :] = nr
                buf_i[r, :] = ni
            return nr, ni

        gr, gi = lax.fori_loop(0, n_it, step, (st_r[...], st_i[...]))
        st_r[...] = gr
        st_i[...] = gi
        if not full:
            gr_ref[...] = gr
            gi_ref[...] = gi
            return
        g_r, g_i = buf_r[...], buf_i[...]
        xr, xi = xr_ref[...], xi_ref[...]
        dlr_ref[...] += jnp.sum(g_r * xr + g_i * xi, axis=0, keepdims=True)
        dli_ref[...] += jnp.sum(g_i * xr - g_r * xi, axis=0, keepdims=True)
        ub = u_ref[...].astype(BF16)
        gb_r, gb_i = g_r.astype(BF16), g_i.astype(BF16)
        du_ref[...] = (lax.dot_general(gb_r, br_ref[...], nt, preferred_element_type=F32)
                       + lax.dot_general(gb_i, bi_ref[...], nt, preferred_element_type=F32))
        dbr_ref[...] += lax.dot_general(ub, gb_r, tn, preferred_element_type=F32)
        dbi_ref[...] += lax.dot_general(ub, gb_i, tn, preferred_element_type=F32)
        lr_, li_ = lr_ref[...], li_ref[...]
        x_r = lr_ * xr - li_ * xi + jnp.dot(ub, br_ref[...], preferred_element_type=F32)
        x_i = lr_ * xi + li_ * xr + jnp.dot(ub, bi_ref[...], preferred_element_type=F32)
        dcr_ref[...] += lax.dot_general(x_r.astype(BF16), dyb, tn, preferred_element_type=F32)
        dci_ref[...] -= lax.dot_general(x_i.astype(BF16), dyb, tn, preferred_element_type=F32)

    rev = (lambda c: nch - 1 - c) if descending else (lambda c: c)
    u_spec = pl.BlockSpec((rows, S5_CB), lambda b, c: (rev(c), b))
    bmat = pl.BlockSpec((None, S5_CB, S5_SB), lambda b, c: (b, 0, 0))
    cmat = pl.BlockSpec((None, S5_SB, S5_CB), lambda b, c: (b, 0, 0))
    lvec = pl.BlockSpec((1, S5_SB), lambda b, c: (0, b))
    evec = pl.BlockSpec((SSM_SEGMENTS, S5_SB), lambda b, c: (0, b))
    xblk = pl.BlockSpec((rows, S5_SB), lambda b, c: (rev(c), b))
    scratch = [pltpu.VMEM((SSM_SEGMENTS, S5_SB), F32)] * 2 + [pltpu.VMEM((rows, S5_SB), F32)] * 2
    e_shape = jax.ShapeDtypeStruct((SSM_SEGMENTS, GN), F32)
    if full:
        return pl.pallas_call(
            body, name=name, grid=(ncb, nch),
            in_specs=[u_spec, u_spec, xblk, xblk, bmat, bmat, cmat, cmat, lvec, lvec, evec, evec],
            out_specs=[u_spec, bmat, bmat, cmat, cmat, lvec, lvec],
            out_shape=[jax.ShapeDtypeStruct((S, D), F32), jax.ShapeDtypeStruct(b_re.shape, F32),
                       jax.ShapeDtypeStruct(b_re.shape, F32), jax.ShapeDtypeStruct(c_re.shape, F32),
                       jax.ShapeDtypeStruct(c_re.shape, F32), jax.ShapeDtypeStruct((1, GN), F32),
                       jax.ShapeDtypeStruct((1, GN), F32)],
            scratch_shapes=scratch, compiler_params=_cp(("parallel", "arbitrary")),
        )(dy, u, *xp, b_re, b_im, c_re, c_im, lam_re, lam_im, *starts)
    return pl.pallas_call(
        body, name=name, grid=(ncb, nch), in_specs=[u_spec, cmat, cmat, lvec, lvec],
        out_specs=[evec, evec], out_shape=[e_shape, e_shape],
        scratch_shapes=scratch, compiler_params=_cp(("parallel", "arbitrary")),
    )(dy, c_re, c_im, lam_re, lam_im)


def _s5_perm(t):
    S, D = t.shape
    return t.reshape(SSM_SEGMENTS, S // SSM_SEGMENTS, D).transpose(1, 0, 2).reshape(S, D)


def _s5_unperm(t):
    S, D = t.shape
    return t.reshape(S // SSM_SEGMENTS, SSM_SEGMENTS, D).transpose(1, 0, 2).reshape(S, D)


def _s5_blockdiag_b(bb, ncb):
    gpb = S5_CB // SSM_GROUP_CH
    t = bb.reshape(SSM_GROUP_CH, ncb, gpb, SSM_STATE)
    return jnp.einsum('cbgn,gh->bgchn', t, jnp.eye(gpb, dtype=bb.dtype)).reshape(ncb, S5_CB, S5_SB)


def _s5_blockdiag_b_t(dblk):
    ncb = dblk.shape[0]
    gpb = S5_CB // SSM_GROUP_CH
    t = dblk.reshape(ncb, gpb, SSM_GROUP_CH, gpb, SSM_STATE)
    return jnp.einsum('bgchn,gh->cbgn', t, jnp.eye(gpb, dtype=dblk.dtype)).reshape(SSM_GROUP_CH, -1)


def _s5_blockdiag_c(c, ncb):
    gpb = S5_CB // SSM_GROUP_CH
    t = c.reshape(ncb, gpb, SSM_GROUP_CH, SSM_STATE)
    return jnp.einsum('bgcn,gh->bgnhc', t, jnp.eye(gpb, dtype=c.dtype)).reshape(ncb, S5_SB, S5_CB)


def _s5_blockdiag_c_t(dblk):
    ncb = dblk.shape[0]
    gpb = S5_CB // SSM_GROUP_CH
    t = dblk.reshape(ncb, gpb, SSM_STATE, gpb, SSM_GROUP_CH)
    return jnp.einsum('bgnhc,gh->bgcn', t, jnp.eye(gpb, dtype=dblk.dtype)).reshape(-1, SSM_GROUP_CH, SSM_STATE)


_GELU_C = math.sqrt(2.0 / math.pi)


def _gelu(y):
    return y * (0.5 * (1.0 + jnp.tanh(_GELU_C * (y + 0.044715 * (y * y * y)))))


def _gelu_grad(y):
    t = jnp.tanh(_GELU_C * (y + 0.044715 * (y * y * y)))
    return 0.5 * (1.0 + t) + 0.5 * y * (1.0 - t * t) * (_GELU_C * (1.0 + 3.0 * 0.044715 * y * y))


def _rowwise(fn, ins, outs, name, acc=()):
    S, D = next(a.shape for a in ins if a.shape[0] != 1)
    tr = _pick(S, (256, 128, 8))
    row = pl.BlockSpec((tr, D), lambda i: (i, 0))
    vec = pl.BlockSpec((1, D), lambda i: (0, 0))
    n_in = len(ins)

    def body(*refs):
        res = fn(*[r[...] for r in refs[:n_in]])
        for k, (o, v) in enumerate(zip(refs[n_in:], res)):
            if k in acc:
                @pl.when(pl.program_id(0) == 0)
                def _():
                    o[...] = jnp.zeros_like(o)
                o[...] += jnp.sum(v, axis=0, keepdims=True)
            else:
                o[...] = v.astype(o.dtype)

    return pl.pallas_call(
        body, name=name, grid=(S // tr,), in_specs=[vec if a.shape[0] == 1 else row for a in ins],
        out_specs=[vec if k in acc else row for k in range(len(outs))],
        out_shape=[jax.ShapeDtypeStruct((1, D) if k in acc else (S, D), dt) for k, dt in enumerate(outs)],
        compiler_params=_cp(("arbitrary",) if acc else ("parallel",)),
    )(*ins)


def _s5_params(p):
    G, N = p["a_re"].shape[1:]
    vec = lambda a: a.reshape(2, 1, G * N)
    ls = jnp.broadcast_to(p["log_step"][:, :, None], (2, G, N))
    bt = lambda b: b.transpose(0, 3, 1, 2).reshape(2, SSM_GROUP_CH, G * N)
    return vec(p["a_re"]), vec(p["a_im"]), vec(ls), bt(p["b_re"]), bt(p["b_im"])


def _s5_layer_fwd(x, norm_g, p, w_glu, tag):
    S, D = x.shape
    ncb = D // S5_CB
    xp = _s5_perm(x)
    hn = _rmsnorm_fwd(xp, norm_g, F32, f"s5_norm_{tag}")
    raw = _s5_params(p)
    lam_r, lam_i, bb_r, bb_i = _s5_disc_fwd(*raw, f"s5_disc_{tag}")
    dirs = []
    ys = []
    for dirn in range(2):
        mats = (_s5_blockdiag_b(bb_r[dirn], ncb).astype(BF16), _s5_blockdiag_b(bb_i[dirn], ncb).astype(BF16),
                _s5_blockdiag_c(p["c_re"][dirn], ncb).astype(BF16), _s5_blockdiag_c(p["c_im"][dirn], ncb).astype(BF16))
        lam = (lam_r[dirn], lam_i[dirn])
        ends = _s5_scan(hn, mats[0], mats[1], None, None, *lam, None, dirn == 1, f"s5_ends_{tag}_{dirn}")
        xr, xi, y = _s5_scan(hn, *mats, *lam, ends, dirn == 1, f"s5_scan_{tag}_{dirn}")
        dirs.append(((xr, xi), mats, lam))
        ys.append(y)
    ytot, z = _rowwise(lambda u, d, a, b: ((lambda y: (y, _gelu(y)))(d * u + a + b)),
                       [hn, p["d"], ys[0], ys[1]], [F32, BF16], f"s5_gelu_{tag}")
    t = _mm(z, w_glu, name=f"s5_glu_{tag}", epi=lambda r, b: r + b, epi_in=(p["b_glu"],))
    (x_new,) = _rowwise(lambda xx, zz, tt: (xx + zz.astype(F32) * jax.nn.sigmoid(tt),),
                        [xp, z, t], [F32], f"s5_out_{tag}")
    return _s5_unperm(x_new), (xp, hn, raw, dirs, ytot, z, t)


def _s5_layer_bwd(dres, saved, norm_g, p, w_glu, tag):
    x, hn, raw, dirs, ytot, z, t = saved
    S, D = x.shape
    G, N = p["a_re"].shape[1:]
    dres = _s5_perm(dres[0])

    def glu_bwd(do, zz, tt):
        sg = jax.nn.sigmoid(tt)
        dt = do * zz.astype(F32) * (sg * (1.0 - sg))
        return dt, do * sg, dt

    dt, dzd, db_glu = _rowwise(glu_bwd, [dres, z, t], [BF16, F32, F32], f"s5_out_bwd_{tag}", acc=(2,))
    dz = _mm(dt, w_glu, tb=True, name=f"s5_dz_{tag}", epi=_add_epi, epi_in=(dzd,))
    dw_glu = _mm(z, dt, ta=True, name=f"s5_dwglu_{tag}", pieces='row')

    def gelu_bwd(dzz, y, u, d):
        dy = dzz * _gelu_grad(y)
        return dy, dy * d, dy * u

    dy, du, dd = _rowwise(gelu_bwd, [dz, ytot, hn, p["d"]], [F32, F32, F32], f"s5_gelu_bwd_{tag}", acc=(2,))
    d_lr, d_li, d_bbr, d_bbi, d_cr, d_ci = [], [], [], [], [], []
    du = [du]
    for dirn in range(2):
        xp, mats, lam = dirs[dirn]
        starts = _s5_scan_bwd(dy, None, None, None, None, mats[2], mats[3], *lam, None, dirn == 0,
                              f"s5_starts_{tag}_{dirn}")
        dup, dbr, dbi, dcr, dci, dlr, dli = _s5_scan_bwd(dy, hn, xp, *mats, *lam, starts, dirn == 0,
                                                         f"s5_scan_bwd_{tag}_{dirn}")
        du.append(dup)
        d_lr.append(dlr)
        d_li.append(dli)
        d_bbr.append(_s5_blockdiag_b_t(dbr))
        d_bbi.append(_s5_blockdiag_b_t(dbi))
        d_cr.append(_s5_blockdiag_c_t(dcr))
        d_ci.append(_s5_blockdiag_c_t(dci))
    da_re, da_im, dls, db_re, db_im = _s5_disc_bwd(*raw, jnp.stack(d_lr), jnp.stack(d_li), jnp.stack(d_bbr),
                                                   jnp.stack(d_bbi), f"s5_disc_bwd_{tag}")
    dres, dg = _rmsnorm_bwd(x, norm_g, du, dres, f"s5_norm_bwd_{tag}")
    dres = tuple(_s5_unperm(t_) for t_ in dres)
    unb = lambda b: b.reshape(2, SSM_GROUP_CH, G, N).transpose(0, 2, 3, 1)
    grads = dict(a_re=da_re.reshape(2, G, N), a_im=da_im.reshape(2, G, N), log_step=dls.reshape(2, G, N).sum(-1),
                 b_re=unb(db_re), b_im=unb(db_im), c_re=jnp.stack(d_cr), c_im=jnp.stack(d_ci),
                 d=dd, w_glu=dw_glu, b_glu=db_glu, norm=dg)
    return dres, grads


def _adamw(w, g, m, v, name):
    R, C = w.shape
    tr = _pick(R, (512, 256, 128, 64, 32, 16, 8))
    tn = _pick(C, (512, 256, 128))

    def body(w_ref, g_ref, m_ref, v_ref, d_ref, nm_ref, nv_ref):
        gg = g_ref[...]
        m2 = ADAM_B1 * m_ref[...] + (1.0 - ADAM_B1) * gg
        v2 = ADAM_B2 * v_ref[...] + (1.0 - ADAM_B2) * (gg * gg)
        m_hat = m2 / (1.0 - ADAM_B1 ** ADAM_STEP)
        v_hat = v2 / (1.0 - ADAM_B2 ** ADAM_STEP)
        d_ref[...] = -ADAM_LR * (m_hat / (jnp.sqrt(v_hat) + ADAM_EPS) + ADAM_WD * w_ref[...])
        nm_ref[...] = m2
        nv_ref[...] = v2

    blk = pl.BlockSpec((tr, tn), lambda i, j: (i, j))
    shp = jax.ShapeDtypeStruct((R, C), F32)
    return pl.pallas_call(body, name=name, grid=(R // tr, C // tn), in_specs=[blk] * 4, out_specs=[blk] * 3,
                          out_shape=[shp] * 3, compiler_params=_cp(("parallel", "parallel")))(w, g, m, v)


def _add_selected(p, sel, others, name, also_bf16=False):
    K, _, M, C = p.shape
    tr = _pick(M, [t for t in (512, 256, 128, 64, 32, 16) if t * C * 4 <= 2 ** 21])
    n_o = len(others)

    def body(sel_ref, p_ref, *refs):
        acc = p_ref[...]
        for r in refs[:n_o]:
            acc = acc + r[...].astype(F32)
        refs[n_o][...] = acc
        if also_bf16:
            refs[n_o + 1][...] = acc.astype(BF16)

    blk = pl.BlockSpec((None, tr, C), lambda k, i, s: (k, i, 0))
    grid_spec = pltpu.PrefetchScalarGridSpec(
        num_scalar_prefetch=1, grid=(K, M // tr),
        in_specs=[pl.BlockSpec((None, None, tr, C), lambda k, i, s: (k, s[0], i, 0))] + [blk] * n_o,
        out_specs=[blk, blk] if also_bf16 else blk)
    shp = jax.ShapeDtypeStruct((K, M, C), F32)
    return pl.pallas_call(body, name=name, grid_spec=grid_spec,
                          out_shape=[shp, jax.ShapeDtypeStruct((K, M, C), BF16)] if also_bf16 else shp,
                          compiler_params=_cp(("parallel", "parallel")))(sel, p, *others)


def _sum_slots(a, own, me, name):
    n, R, C = a.shape
    tr = _pick(R, (512, 256, 128, 64, 32, 16, 8))

    def body(me_ref, a_ref, own_ref, o_ref):
        term = lambda k: jnp.where(me_ref[0] == k, own_ref[...], a_ref[k])
        acc = term(0)
        for k in range(1, n):
            acc = acc + term(k)
        o_ref[...] = acc

    grid_spec = pltpu.PrefetchScalarGridSpec(
        num_scalar_prefetch=1, grid=(R // tr,),
        in_specs=[pl.BlockSpec((n, tr, C), lambda i, s: (0, i, 0)), pl.BlockSpec((tr, C), lambda i, s: (i, 0))],
        out_specs=pl.BlockSpec((tr, C), lambda i, s: (i, 0)))
    return pl.pallas_call(body, name=name, grid_spec=grid_spec, out_shape=jax.ShapeDtypeStruct((R, C), F32),
                          compiler_params=_cp(("parallel",)))(me, a, own)


def _position():
    return lax.axis_index("x"), lax.axis_index("y"), lax.axis_index("c")


def _other_chips(x, y):
    return [(1 - x, y), (x, 1 - y), (1 - x, 1 - y)]


def _exchange_call(x, name):
    n_in, n_out = len(x.ins), len(x.out_shapes)

    def body(*refs):
        in_refs, out_refs, sems = refs[:n_in], refs[n_in:n_in + n_out], refs[n_in + n_out:]
        x.start(in_refs, out_refs, *sems)
        x.wait(in_refs, out_refs, *sems)

    hbm = pl.BlockSpec(memory_space=pltpu.HBM)
    return pl.pallas_call(body, name=name, in_specs=[hbm] * n_in, out_specs=[hbm] * n_out, out_shape=x.out_shapes,
                          scratch_shapes=x.sem_shapes())(*x.ins)


def _gather_chips(arrs):
    n = len(arrs)

    def plan(ins, outs):
        x, y, c = _position()
        me = 2 * x + y
        sends, recvs = [], []
        for px, py in _other_chips(x, y):
            for i in range(n):
                sends.append((ins[i], outs[i].at[me], (px, py, c)))
                recvs.append((outs[i].at[2 * px + py], (px, py, c)))
        return sends, recvs

    return _Exchange(arrs, [jax.ShapeDtypeStruct((4,) + a.shape, a.dtype) for a in arrs], 3 * n, plan)


def _gather_halves_ici(arrs):
    n = len(arrs)
    hr = [a.shape[1] // 2 for a in arrs]

    def plan(ins, outs):
        x, y, c = _position()
        me = 2 * x + y
        sends, recvs = [], []
        for px, py in _other_chips(x, y):
            for i in range(n):
                sends.append((ins[i].at[:, pl.ds(c * hr[i], hr[i])], outs[i].at[me, c], (px, py, c)))
                recvs.append((outs[i].at[2 * px + py, c], (px, py, c)))
        return sends, recvs

    shapes = [jax.ShapeDtypeStruct((N_CHIPS, 2, a.shape[0], a.shape[1] // 2, a.shape[2]), a.dtype) for a in arrs]
    return _Exchange(arrs, shapes, 3 * n, plan)


def _gather_halves_d2d(landed):
    n = len(landed)

    def plan(ins, outs):
        x, y, c = _position()
        sib = (x, y, 1 - c)
        sends, recvs = [], []
        for px, py in _other_chips(x, y):
            for i in range(n):
                sends.append((ins[i].at[2 * px + py, c], outs[i].at[2 * px + py, c], sib))
                recvs.append((outs[i].at[2 * px + py, 1 - c], sib))
        return sends, recvs

    return _Exchange(landed, [jax.ShapeDtypeStruct(a.shape, a.dtype) for a in landed], 3 * n, plan)


class _Reducer:
    def __init__(self, pieces, tags, sel_core, sel_chip):
        self.pieces, self.tags, self.sel_core, self.sel_chip = pieces, tags, sel_core, sel_chip

    def sibling(self):
        return _sibling_halves(self.pieces)

    def chips(self, from_sibling):
        self.pair = [_add_selected(p, self.sel_core, [r], f"reduce_add_pair_{t}", also_bf16=True)
                     for t, p, r in zip(self.tags, self.pieces, from_sibling)]
        return _scatter_chips([pb for _, pb in self.pair])

    def swap(self, from_chips):
        self.mine = [_add_selected(p.reshape((1,) + p.shape), self.sel_chip, [r[k:k + 1] for k in range(3)],
                                   f"reduce_add_chips_{t}")[0]
                     for t, (p, _), r in zip(self.tags, self.pair, from_chips)]
        return _sibling_swap(self.mine)

    def finish(self, theirs):
        return list(zip(self.mine, theirs))

    def run(self, tag):
        landed = _exchange_call(self.sibling(), f"reduce_sibling_{tag}")
        landed = _exchange_call(self.chips(landed), f"reduce_chips_{tag}")
        return self.finish(_exchange_call(self.swap(landed), f"reduce_join_{tag}"))


def _sibling_halves(pieces):
    n = len(pieces)

    def plan(ins, outs):
        x, y, c = _position()
        sib = (x, y, 1 - c)
        return [(ins[i].at[:, 1 - c], outs[i], sib) for i in range(n)], [(outs[i], sib) for i in range(n)]

    shapes = [jax.ShapeDtypeStruct((p.shape[0],) + p.shape[2:], p.dtype) for p in pieces]
    return _Exchange(pieces, shapes, n, plan)


def _scatter_chips(sums):
    n = len(sums)

    def plan(ins, outs):
        x, y, c = _position()
        sends, recvs = [], []
        for j, (px, py) in enumerate(_other_chips(x, y)):
            for i in range(n):
                sends.append((ins[i].at[2 * px + py], outs[i].at[j], (px, py, c)))
                recvs.append((outs[i].at[j], (px, py, c)))
        return sends, recvs

    return _Exchange(sums, [jax.ShapeDtypeStruct((3,) + s.shape[1:], s.dtype) for s in sums], 3 * n, plan)


def _sibling_swap(halves):
    n = len(halves)

    def plan(ins, outs):
        x, y, c = _position()
        sib = (x, y, 1 - c)
        return [(ins[i], outs[i], sib) for i in range(n)], [(outs[i], sib) for i in range(n)]

    return _Exchange(halves, [jax.ShapeDtypeStruct(h.shape, h.dtype) for h in halves], n, plan)


def _gather_all(v):
    rels = [(dx, dy, dc) for dx in (0, 1) for dy in (0, 1) for dc in (0, 1)][1:]

    def plan(ins, outs):
        x, y, c = _position()
        me = 4 * x + 2 * y + c
        flip = lambda a, d: 1 - a if d else a
        sends, recvs = [], []
        for dx, dy, dc in rels:
            px, py, pc = flip(x, dx), flip(y, dy), flip(c, dc)
            sends.append((ins[0], outs[0].at[me], (px, py, pc)))
            recvs.append((outs[0].at[4 * px + 2 * py + pc], (px, py, pc)))
        return sends, recvs

    return _Exchange([v], [jax.ShapeDtypeStruct((8,) + v.shape, v.dtype)], len(rels), plan)


WEIGHTS = ['mix_norm', 'ffn_norm', 'final_norm', 'attn_w_qkv', 'attn_w_o', 'attn_sink', 'ssm_a_re', 'ssm_a_im',
           'ssm_log_step', 'ssm_b_re', 'ssm_b_im', 'ssm_c_re', 'ssm_c_im', 'ssm_d', 'ssm_w_glu', 'ssm_b_glu',
           'mla_w_dqkv', 'mla_q_norm', 'mla_kv_norm', 'mla_w_uq', 'mla_w_ukv', 'mla_w_o', 'ffn_w_up',
           'ffn_conv_w', 'ffn_conv_b', 'ffn_w_down']
BIG = dict(attn_w_qkv='col', attn_w_o='row', ssm_w_glu='row', mla_w_dqkv='row', mla_w_uq='col',
           mla_w_ukv='col', mla_w_o='row', ffn_w_up='col', ffn_w_down='row')
SMALL_SHARDED = ('mla_q_norm', 'mla_kv_norm', 'ffn_conv_w')
N_CHIPS = 4


def _assemble(over_ici, over_d2d, own, chip, core, kind):
    L = over_ici.shape[2]
    half = lax.broadcasted_iota(jnp.int32, (1, 2, 1, 1, 1), 1)
    g = jnp.where(half == core, over_ici, over_d2d)
    own_halves = own.reshape(L, 2, g.shape[3], g.shape[4]).transpose(1, 0, 2, 3)
    slot = lax.broadcasted_iota(jnp.int32, (N_CHIPS, 1, 1, 1, 1), 0)
    g = jnp.where(slot == chip, own_halves[None], g)
    if kind == 'row':
        return g.transpose(2, 0, 1, 3, 4).reshape(L, -1, g.shape[4])
    return g.transpose(2, 1, 3, 0, 4).reshape(L, 2 * g.shape[3], -1)


def _to_pieces(w, kind):
    L, R, C = w.shape
    if kind == 'col':
        t = w.reshape(L, 2, R // 2, N_CHIPS, C // N_CHIPS).transpose(3, 1, 0, 2, 4)
    else:
        t = w.reshape(L, N_CHIPS, R // N_CHIPS, 2, C // 2).transpose(1, 3, 0, 2, 4)
    return t.reshape(N_CHIPS, 2, L * t.shape[3], t.shape[4])


def _from_halves(h, kind, shard_shape):
    L = shard_shape[0]
    t = h.reshape(2, L, -1, h.shape[2])
    t = t.transpose(1, 0, 2, 3) if kind == 'col' else t.transpose(1, 2, 0, 3)
    return t.reshape(shard_shape)


def _pack(arrs):
    flat = jnp.concatenate([a.reshape(-1) for a in arrs])
    pad = (-flat.shape[0]) % (8 * LANES)
    return jnp.pad(flat, (0, pad)).reshape(-1, LANES)


def _unpack(packed, like):
    flat = packed.reshape(-1)
    out, off = [], 0
    for a in like:
        out.append(flat[off:off + a.size].reshape(a.shape))
        off += a.size
    return out


def kernel(x, mix_norm, ffn_norm, final_norm, attn_w_qkv, attn_w_o, attn_sink, ssm_a_re, ssm_a_im, ssm_log_step, ssm_b_re, ssm_b_im, ssm_c_re, ssm_c_im, ssm_d, ssm_w_glu, ssm_b_glu, mla_w_dqkv, mla_q_norm, mla_kv_norm, mla_w_uq, mla_w_ukv, mla_w_o, ffn_w_up, ffn_conv_w, ffn_conv_b, ffn_w_down, loss_target, m_mix_norm, m_ffn_norm, m_final_norm, m_attn_w_qkv, m_attn_w_o, m_attn_sink, m_ssm_a_re, m_ssm_a_im, m_ssm_log_step, m_ssm_b_re, m_ssm_b_im, m_ssm_c_re, m_ssm_c_im, m_ssm_d, m_ssm_w_glu, m_ssm_b_glu, m_mla_w_dqkv, m_mla_q_norm, m_mla_kv_norm, m_mla_w_uq, m_mla_w_ukv, m_mla_w_o, m_ffn_w_up, m_ffn_conv_w, m_ffn_conv_b, m_ffn_w_down, v_mix_norm, v_ffn_norm, v_final_norm, v_attn_w_qkv, v_attn_w_o, v_attn_sink, v_ssm_a_re, v_ssm_a_im, v_ssm_log_step, v_ssm_b_re, v_ssm_b_im, v_ssm_c_re, v_ssm_c_im, v_ssm_d, v_ssm_w_glu, v_ssm_b_glu, v_mla_w_dqkv, v_mla_q_norm, v_mla_kv_norm, v_mla_w_uq, v_mla_w_ukv, v_mla_w_o, v_ffn_w_up, v_ffn_conv_w, v_ffn_conv_b, v_ffn_w_down):
    args = locals()
    w = {n: args[n] for n in WEIGHTS}
    mom = {n: args["m_" + n] for n in WEIGHTS}
    var = {n: args["v_" + n] for n in WEIGHTS}
    depth = mix_norm.shape[0]
    xs = x[0]
    target = loss_target[0]
    chip = 2 * lax.axis_index("x") + lax.axis_index("y")
    core = lax.axis_index("c")

    big_names = list(BIG)
    w_bf = {n: w[n].astype(BF16) for n in big_names}
    mixer_weights = {0: ('attn_w_qkv', 'attn_w_o'), 1: ('ssm_w_glu',),
                     2: ('mla_w_dqkv', 'mla_w_uq', 'mla_w_ukv', 'mla_w_o')}

    def mixer_items(i):
        return [(n, i // 3) for n in mixer_weights[i % 3]]

    def ffn_items(i):
        return [('ffn_w_up', i), ('ffn_w_down', i)]

    def layer_items(i):
        return mixer_items(i) + ffn_items(i)

    def blocks(items):
        return [w_bf[n][l:l + 1] for n, l in items]

    full = {}

    def install(items, landed):
        for (n, l), a, b, own in zip(items, *landed, blocks(items)):
            full[n, l] = _assemble(a, b, own, chip, core, BIG[n])[0]

    over_ici = _exchange_call(_gather_halves_ici(blocks(mixer_items(0))), "gather_first_ici")
    install(mixer_items(0), (over_ici, _exchange_call(_gather_halves_d2d(over_ici), "gather_first_d2d")))
    small_sharded = _exchange_call(_gather_chips([w[n] for n in SMALL_SHARDED]), "gather_small_weights")
    for n, g in zip(SMALL_SHARDED, small_sharded):
        slot = lax.broadcasted_iota(jnp.int32, (N_CHIPS,) + (1,) * w[n].ndim, 0)
        g = jnp.where(slot == chip, w[n][None], g)
        full[n] = jnp.moveaxis(g, 0, -2).reshape(g.shape[1:-1] + (-1,))

    def ffn_args(i):
        return (ffn_norm[i:i + 1], full["ffn_w_up", i], full["ffn_conv_w"][i], ffn_conv_b[i:i + 1], full["ffn_w_down", i])

    def s5_params(j):
        return dict(a_re=ssm_a_re[j], a_im=ssm_a_im[j], log_step=ssm_log_step[j], b_re=ssm_b_re[j], b_im=ssm_b_im[j],
                    c_re=ssm_c_re[j], c_im=ssm_c_im[j], d=ssm_d[j:j + 1], b_glu=ssm_b_glu[j:j + 1])

    def mla_args(j):
        wd, wq, wkv = _mla_weights(full["mla_w_dqkv", j], full["mla_w_uq", j], full["mla_w_ukv", j])
        return (wd, wq, wkv, full["mla_w_o", j], full["mla_q_norm"][j:j + 1], full["mla_kv_norm"][j:j + 1])

    h = xs
    saved = []
    for i in range(depth):
        kind, j = i % 3, i // 3
        g = mix_norm[i:i + 1]
        if kind == 0:
            h, sm, landed = _attn_layer_fwd(h, g, full["attn_w_qkv", j], full["attn_w_o", j], attn_sink[j], f"{i}",
                                            next_blocks=blocks(ffn_items(0)) if i == 0 else None)
            if landed is not None:
                install(ffn_items(0), landed)
        elif kind == 1:
            h, sm = _s5_layer_fwd(h, g, s5_params(j), full["ssm_w_glu", j], f"{i}")
        else:
            h, sm = _mla_layer_fwd(h, g, *mla_args(j), f"{i}")
        nxt = layer_items(i + 1) if i + 1 < depth else None
        h, sf, landed = _ffn_fwd(h, *ffn_args(i), f"{i}", next_blocks=nxt and blocks(nxt))
        if landed is not None:
            install(nxt, landed)
        saved.append((sm, sf))
    loss_part, dres, d_final = _loss_head(h, final_norm.reshape(1, -1), target, "loss_head")

    gl = {n: [None] * w[n].shape[0] for n in WEIGHTS if n != 'final_norm'}
    sel_core = jnp.reshape(core, (1,)).astype(jnp.int32)
    sel_chip = jnp.reshape(chip, (1,)).astype(jnp.int32)
    reduced = {}

    def reducer_for(items):
        pieces = [gl[n][l] if gl[n][l].ndim == 4 else _to_pieces(gl[n][l][None], BIG[n]) for n, l in items]
        return _Reducer(pieces, [f"{n}_{l}" for n, l in items], sel_core, sel_chip)

    pending = None
    for i in reversed(range(depth)):
        kind, j = i % 3, i // 3
        sm, sf = saved[i]
        dres, (dg, dwu, dcw, dcb, dwd), done = _ffn_bwd(dres, sf, *ffn_args(i), f"{i}", reducer=pending)
        if pending is not None:
            reduced.update(zip(layer_items(i + 1), done))
        gl['ffn_norm'][i], gl['ffn_w_up'][i], gl['ffn_conv_w'][i] = dg[0], dwu, dcw
        gl['ffn_conv_b'][i], gl['ffn_w_down'][i] = dcb[0], dwd
        g = mix_norm[i:i + 1]
        if kind == 0:
            own_ffn = reducer_for(ffn_items(0)) if i == 0 else None
            dres, (dg, dwq, dwo, dsk), done = _attn_layer_bwd(dres, sm, g, full["attn_w_qkv", j], full["attn_w_o", j],
                                                              attn_sink[j], f"{i}", reducer=own_ffn)
            if own_ffn is not None:
                reduced.update(zip(ffn_items(0), done))
            gl['attn_w_qkv'][j], gl['attn_w_o'][j], gl['attn_sink'][j] = dwq, dwo, dsk
        elif kind == 1:
            dres, gs = _s5_layer_bwd(dres, sm, g, s5_params(j), full["ssm_w_glu", j], f"{i}")
            dg = gs['norm']
            for k in ('a_re', 'a_im', 'log_step', 'b_re', 'b_im', 'c_re', 'c_im'):
                gl['ssm_' + k][j] = gs[k]
            gl['ssm_d'][j], gl['ssm_b_glu'][j], gl['ssm_w_glu'][j] = gs['d'][0], gs['b_glu'][0], gs['w_glu']
        else:
            dres, (dg, dwd_, dqn, dkvn, dwuq, dwukv, dwo) = _mla_layer_bwd(dres, sm, g, *mla_args(j), f"{i}")
            gl['mla_w_dqkv'][j], gl['mla_q_norm'][j], gl['mla_kv_norm'][j] = dwd_, dqn[0], dkvn[0]
            gl['mla_w_uq'][j], gl['mla_w_ukv'][j], gl['mla_w_o'][j] = dwuq, dwukv, dwo
        gl['mix_norm'][i] = dg[0]
        pending = reducer_for(layer_items(i) if i > 0 else mixer_items(0))
    reduced.update(zip(mixer_items(0), pending.run("first_layer")))
    local = {n: jnp.stack(v) for n, v in gl.items() if n not in BIG}
    local['final_norm'] = d_final[0]

    grads = {}
    for n in big_names:
        shards = []
        for l in range(w[n].shape[0]):
            a, b = reduced[n, l]
            halves = jnp.where(core == 0, jnp.stack([a, b]), jnp.stack([b, a]))
            shards.append(_from_halves(halves, BIG[n], (1,) + w[n].shape[1:]))
        grads[n] = jnp.concatenate(shards, axis=0)

    small_names = [n for n in WEIGHTS if n not in BIG]
    packed = _pack([local[n] for n in small_names])
    device = jnp.reshape(2 * chip + core, (1,)).astype(jnp.int32)
    summed = _sum_slots(_exchange_call(_gather_all(packed), "reduce_small")[0], packed, device, "reduce_small_sum")
    for n, gsum in zip(small_names, _unpack(summed, [local[n] for n in small_names])):
        if n in SMALL_SHARDED:
            width = w[n].shape[-1]
            gsum = lax.dynamic_slice_in_dim(gsum, chip * width, width, axis=gsum.ndim - 1)
        grads[n] = gsum

    delta, new_m, new_v = {}, {}, {}
    for n in big_names:
        two_d = lambda a: a.reshape(-1, a.shape[-1])
        d_, m_, v_ = _adamw(two_d(w[n]), two_d(grads[n]), two_d(mom[n]), two_d(var[n]), f"adamw_{n}")
        delta[n], new_m[n], new_v[n] = (t.reshape(w[n].shape) for t in (d_, m_, v_))
    outs = _adamw(*[_pack([src[n] for n in small_names]) for src in (w, grads, mom, var)], "adamw_small")
    for dst, packed in zip((delta, new_m, new_v), outs):
        for n, t in zip(small_names, _unpack(packed, [w[n] for n in small_names])):
            dst[n] = t

    loss = lax.psum(loss_part[0, 0], ("x", "y", "c"))
    return (loss, dres[0][None], *[grads[n] for n in WEIGHTS], *[delta[n] for n in WEIGHTS],
            *[new_m[n] for n in WEIGHTS], *[new_v[n] for n in WEIGHTS])
```

```python
import functools
import math

import numpy as np
import jax
import jax.numpy as jnp
from jax import lax
from jax.experimental import pallas as pl
from jax.experimental.pallas import tpu as pltpu

F32 = jnp.float32
BF16 = jnp.bfloat16
MESH = pl.DeviceIdType.MESH

RMS_EPS = 1e-6
ATTN_DH = 64
ATTN_GROUP = 8
ATTN_BLOCK = 128
SSM_GROUP_CH = 16
SSM_STATE = 64
SSM_SEGMENTS = 8
MLA_HEADS = 16
MLA_LORA = 512
MLA_NOPE = 128
MLA_ROPE = 64
MLA_V = 128
ROPE_THETA = 10000.0
LANES = 128
VMEM_LIMIT = 56 * 1024 * 1024

ADAM_LR = 0.001
ADAM_B1 = 0.9
ADAM_B2 = 0.999
ADAM_EPS = 1e-08
ADAM_WD = 0.01
ADAM_STEP = 10


def _cp(sem=None):
    kw = dict(vmem_limit_bytes=VMEM_LIMIT)
    if sem is not None:
        kw["dimension_semantics"] = sem
    return pltpu.CompilerParams(**kw)


MM_FULL_K = 2048
MM_FULL_N = 1536


def _pick(n, cands):
    for c in cands:
        if n % c == 0:
            return c
    return n


class _Exchange:
    def __init__(self, ins, out_shapes, n_remote, plan):
        self.ins, self.out_shapes, self.n_remote, self.plan = list(ins), list(out_shapes), n_remote, plan

    def _copy(self, k, src, dst, dev, send_sems, recv_sems):
        return pltpu.make_async_remote_copy(src_ref=src, dst_ref=dst, send_sem=send_sems.at[k],
                                            recv_sem=recv_sems.at[k], device_id=dev, device_id_type=MESH)

    def start(self, in_refs, out_refs, send_sems, recv_sems):
        sends, recvs = self.plan(in_refs, out_refs)
        assert len(sends) == len(recvs) == self.n_remote
        for k, (src, dst, dev) in enumerate(sends):
            self._copy(k, src, dst, dev, send_sems, recv_sems).start()

    def wait(self, in_refs, out_refs, send_sems, recv_sems):
        sends, recvs = self.plan(in_refs, out_refs)
        for k, (dst, dev) in enumerate(recvs):
            self._copy(k, dst, dst, dev, send_sems, recv_sems).wait_recv()
        for k, (src, dst, dev) in enumerate(sends):
            self._copy(k, src, dst, dev, send_sems, recv_sems).wait_send()

    def sem_shapes(self):
        return [pltpu.SemaphoreType.DMA((self.n_remote,)), pltpu.SemaphoreType.DMA((self.n_remote,))]


def _gmm(a, b, *, grid, a_spec, b_spec, o_spec, out_shape, dims, kax, acc_shape, name,
         epi=None, epi_in=(), epi_specs=(), hosted=None):
    nk = grid[kax]
    n_epi = len(epi_in)
    n_hin = len(hosted.ins) if hosted else 0
    n_hout = len(hosted.out_shapes) if hosted else 0

    def finish(r, e_refs, o_ref):
        if epi is not None:
            r = epi(r, *[e[...] for e in e_refs])
        o_ref[...] = r.astype(o_ref.dtype)

    def product(a_ref, b_ref):
        return lax.dot_general(a_ref[...].astype(BF16), b_ref[...].astype(BF16), dims, preferred_element_type=F32)

    def body(a_ref, b_ref, *rest):
        e_refs = rest[:n_epi]
        h_in = rest[n_epi:n_epi + n_hin]
        o_ref = rest[n_epi + n_hin]
        h_out = rest[n_epi + n_hin + 1:n_epi + n_hin + 1 + n_hout]
        scratch = rest[n_epi + n_hin + 1 + n_hout:]
        if hosted:
            sems = scratch[-2:]
            ids = [pl.program_id(i) for i in range(len(grid))]
            first = functools.reduce(jnp.logical_and, [i == 0 for i in ids])
            last = functools.reduce(jnp.logical_and, [i == n - 1 for i, n in zip(ids, grid)])
            pl.when(first)(lambda: hosted.start(h_in, h_out, *sems))
        if nk == 1:
            finish(product(a_ref, b_ref), e_refs, o_ref)
        else:
            acc_ref = scratch[0]
            k = pl.program_id(kax)

            @pl.when(k == 0)
            def _():
                acc_ref[...] = jnp.zeros_like(acc_ref)

            acc_ref[...] += product(a_ref, b_ref)

            @pl.when(k == nk - 1)
            def _():
                finish(acc_ref[...], e_refs, o_ref)
        if hosted:
            pl.when(last)(lambda: hosted.wait(h_in, h_out, *sems))

    sem = tuple("arbitrary" if (i == kax or hosted) else "parallel" for i in range(len(grid)))
    hbm = pl.BlockSpec(memory_space=pltpu.HBM)
    res = pl.pallas_call(
        body, name=name, grid=grid,
        in_specs=[a_spec, b_spec, *epi_specs] + [hbm] * n_hin,
        out_specs=[o_spec] + [hbm] * n_hout if hosted else o_spec,
        out_shape=[out_shape] + hosted.out_shapes if hosted else out_shape,
        scratch_shapes=([] if nk == 1 else [pltpu.VMEM(acc_shape, F32)]) + (hosted.sem_shapes() if hosted else []),
        compiler_params=_cp(sem),
    )(a, b, *epi_in, *(hosted.ins if hosted else ()))
    return (res[0], list(res[1:])) if hosted else res


def _mm(a, b, *, ta=False, tb=False, out_dtype=F32, name, epi=None, epi_in=(), pieces=None, hosted=None):
    a_parts = a.shape[0] if a.ndim == 3 else 1
    b_parts = b.shape[0] if b.ndim == 3 else 1
    assert not (ta and a_parts > 1) and not (tb and b_parts > 1)
    (K, M) = a.shape if ta else (a.shape[-1] * a_parts, a.shape[-2])
    (N, K2) = b.shape if tb else (b.shape[-1] * b_parts, b.shape[-2])
    assert K == K2, (a.shape, b.shape, ta, tb)
    m_unit = {None: M, 'col': M // 2, 'row': M // N_CHIPS}[pieces]
    n_unit = {None: N // b_parts, 'col': N // N_CHIPS, 'row': N // 2}[pieces]
    k_unit = K // a_parts
    tm = _pick(m_unit, (1024, 1408, 512, 256, 128))
    if (pieces is None and not ta and K > MM_FULL_K and M % 2048 == 0
            and not any(e.shape[0] != 1 for e in epi_in)):
        tm = 2048
    tn = n_unit if n_unit <= MM_FULL_N else _pick(n_unit, (1024, 1408, 512, 256, 128))
    tk = k_unit if k_unit <= MM_FULL_K else _pick(k_unit, (2048, 1408, 1024, 512, 256, 128))
    grid = (M // tm, N // tn, K // tk)
    if ta:
        a_spec = pl.BlockSpec((tk, tm), lambda i, j, k: (k, i))
    elif a_parts > 1:
        nkp = k_unit // tk
        a_spec = pl.BlockSpec((None, tm, tk), lambda i, j, k: (k // nkp, i, k % nkp))
    else:
        a_spec = pl.BlockSpec((tm, tk), lambda i, j, k: (i, k))
    if tb:
        b_spec = pl.BlockSpec((tn, tk), lambda i, j, k: (j, k))
    elif b_parts > 1:
        njp = (N // b_parts) // tn
        b_spec = pl.BlockSpec((None, tk, tn), lambda i, j, k: (j // njp, k, j % njp))
    else:
        b_spec = pl.BlockSpec((tk, tn), lambda i, j, k: (k, j))
    dims = (((0 if ta else 1,), (1 if tb else 0,)), ((), ()))
    epi_specs = [pl.BlockSpec((1, tn), lambda i, j, k: (0, j)) if e.shape[0] == 1
                 else pl.BlockSpec((tm, tn), lambda i, j, k: (i, j)) for e in epi_in]
    if pieces is None:
        o_spec = pl.BlockSpec((tm, tn), lambda i, j, k: (i, j))
        out_shape = jax.ShapeDtypeStruct((M, N), out_dtype)
    else:
        mi, nj = m_unit // tm, n_unit // tn
        if pieces == 'col':
            o_spec = pl.BlockSpec((None, None, tm, tn), lambda i, j, k: (j // nj, i // mi, i % mi, j % nj))
        else:
            o_spec = pl.BlockSpec((None, None, tm, tn), lambda i, j, k: (i // mi, j // nj, i % mi, j % nj))
        out_shape = jax.ShapeDtypeStruct((N_CHIPS, 2, m_unit, n_unit), out_dtype)
    return _gmm(a, b, grid=grid, a_spec=a_spec, b_spec=b_spec, o_spec=o_spec, out_shape=out_shape, dims=dims, kax=2,
                acc_shape=(tm, tn), name=name, epi=epi, epi_in=epi_in, epi_specs=epi_specs, hosted=hosted)


def _add_epi(r, res):
    return res + r


def _rmsnorm_fwd(x, g, out_dtype, name):
    S, D = x.shape
    tr = _pick(S, (256, 128, 8))

    def body(x_ref, g_ref, o_ref):
        xf = x_ref[...]
        r = lax.rsqrt(jnp.mean(xf * xf, axis=-1, keepdims=True) + RMS_EPS)
        o_ref[...] = ((xf * r) * g_ref[...]).astype(o_ref.dtype)

    return pl.pallas_call(
        body, name=name, grid=(S // tr,),
        in_specs=[pl.BlockSpec((tr, D), lambda i: (i, 0)), pl.BlockSpec((1, D), lambda i: (0, 0))],
        out_specs=pl.BlockSpec((tr, D), lambda i: (i, 0)),
        out_shape=jax.ShapeDtypeStruct((S, D), out_dtype), compiler_params=_cp(("parallel",)),
    )(x, g)


def _rms_bwd_math(xf, g, dy):
    r = lax.rsqrt(jnp.mean(xf * xf, axis=-1, keepdims=True) + RMS_EPS)
    xh = xf * r
    dxh = dy * g
    dx = r * (dxh - xh * jnp.mean(dxh * xh, axis=-1, keepdims=True))
    return dx, jnp.sum(dy * xh, axis=0, keepdims=True)


def _rmsnorm_bwd(x, g, dy, dres, name):
    S, D = x.shape
    tr = _pick(S, (256, 128, 8))
    dys = list(dy) if isinstance(dy, (list, tuple)) else [dy]
    n_dy = len(dys)

    def body(x_ref, g_ref, *refs):
        dy_refs = refs[:n_dy]
        dres_ref, dx_ref, dxb_ref, dg_ref = refs[n_dy:]

        @pl.when(pl.program_id(0) == 0)
        def _():
            dg_ref[...] = jnp.zeros_like(dg_ref)

        dy_sum = dy_refs[0][...].astype(F32)
        for r in dy_refs[1:]:
            dy_sum = dy_sum + r[...].astype(F32)
        dx, dg = _rms_bwd_math(x_ref[...], g_ref[...], dy_sum)
        tot = dres_ref[...] + dx
        dx_ref[...] = tot
        dxb_ref[...] = tot.astype(BF16)
        dg_ref[...] += dg

    row = pl.BlockSpec((tr, D), lambda i: (i, 0))
    vec = pl.BlockSpec((1, D), lambda i: (0, 0))
    dx, dxb, dg = pl.pallas_call(
        body, name=name, grid=(S // tr,), in_specs=[row, vec] + [row] * (n_dy + 1), out_specs=[row, row, vec],
        out_shape=[jax.ShapeDtypeStruct((S, D), F32), jax.ShapeDtypeStruct((S, D), BF16),
                   jax.ShapeDtypeStruct((1, D), F32)],
        compiler_params=_cp(("arbitrary",)),
    )(x, g, *dys, dres)
    return (dx, dxb), dg


def _loss_head(x, g, target, name):
    S, D = x.shape
    tr = _pick(S, (256, 128, 8))

    def body(x_ref, g_ref, t_ref, loss_ref, dx_ref, dxb_ref, dg_ref):
        @pl.when(pl.program_id(0) == 0)
        def _():
            dg_ref[...] = jnp.zeros_like(dg_ref)
            loss_ref[...] = jnp.zeros_like(loss_ref)

        xf = x_ref[...]
        gg = g_ref[...]
        r = lax.rsqrt(jnp.mean(xf * xf, axis=-1, keepdims=True) + RMS_EPS)
        e = (xf * r) * gg - t_ref[...]
        loss_ref[...] += 0.5 * jnp.sum(jnp.mean(e * e, axis=-1, keepdims=True), axis=0, keepdims=True)
        dx, dg = _rms_bwd_math(xf, gg, e * (1.0 / D))
        dx_ref[...] = dx
        dxb_ref[...] = dx.astype(BF16)
        dg_ref[...] += dg

    row = pl.BlockSpec((tr, D), lambda i: (i, 0))
    vec = pl.BlockSpec((1, D), lambda i: (0, 0))
    one = pl.BlockSpec((1, 1), lambda i: (0, 0))
    loss, dx, dxb, dg = pl.pallas_call(
        body, name=name, grid=(S // tr,), in_specs=[row, vec, row], out_specs=[one, row, row, vec],
        out_shape=[jax.ShapeDtypeStruct((1, 1), F32), jax.ShapeDtypeStruct((S, D), F32),
                   jax.ShapeDtypeStruct((S, D), BF16), jax.ShapeDtypeStruct((1, D), F32)],
        compiler_params=_cp(("arbitrary",)),
    )(x, g, target)
    return loss, (dx, dxb), dg


HALO = 16


def _shift_rows(main, prev_row, next_row):
    tr = main.shape[0]
    row = lax.broadcasted_iota(jnp.int32, main.shape, 0)
    up = jnp.where(row == 0, prev_row, pltpu.roll(main, 1, 0))
    dn = jnp.where(row == tr - 1, next_row, pltpu.roll(main, tr - 1, 0))
    return up, dn


def _halo_specs(tr, tn, S, col_of):
    hb = tr // HALO
    last = S // HALO - 1
    return [pl.BlockSpec((tr, tn), lambda j, i: (i, col_of(j))),
            pl.BlockSpec((HALO, tn), lambda j, i: (jnp.maximum(i * hb - 1, 0), col_of(j))),
            pl.BlockSpec((HALO, tn), lambda j, i: (jnp.minimum((i + 1) * hb, last), col_of(j)))]


def _halo_rows(main_ref, prev_ref, next_ref, i, n_i):
    main = main_ref[...].astype(F32)
    prev_row = prev_ref[HALO - 1:HALO, :].astype(F32) * (i > 0).astype(F32)
    next_row = next_ref[0:1, :].astype(F32) * (i < n_i - 1).astype(F32)
    up, dn = _shift_rows(main, prev_row, next_row)
    return up, main, dn


def _conv3(w_ref, b_ref, up, mid, dn):
    return b_ref[...] + w_ref[0:1, :] * up + w_ref[1:2, :] * mid + w_ref[2:3, :] * dn


def _ffn_tiles(S, F, rows=(1024, 512, 256, 128, 16)):
    return _pick(S, rows), _pick(F, (512, 256, 128))


def _sigmoid(x):
    return 0.5 * jnp.tanh(0.5 * x) + 0.5


def _conv_gate_fwd(u, conv_w, conv_b, name):
    S, F2 = u.shape
    F = F2 // 2
    tr, tn = _ffn_tiles(S, F)
    nj, ni = F // tn, S // tr

    def body(gm, gp, gn, vm, vp, vn, wg, wv, bg, bv, o_ref):
        i = pl.program_id(1)
        cg = _conv3(wg, bg, *_halo_rows(gm, gp, gn, i, ni))
        cv = _conv3(wv, bv, *_halo_rows(vm, vp, vn, i, ni))
        o_ref[...] = (cg * _sigmoid(cg) * cv).astype(o_ref.dtype)

    wspec = lambda off: pl.BlockSpec((3, tn), lambda j, i: (0, j + off))
    bspec = lambda off: pl.BlockSpec((1, tn), lambda j, i: (0, j + off))
    return pl.pallas_call(
        body, name=name, grid=(nj, ni),
        in_specs=[*_halo_specs(tr, tn, S, lambda j: j), *_halo_specs(tr, tn, S, lambda j: j + nj),
                  wspec(0), wspec(nj), bspec(0), bspec(nj)],
        out_specs=pl.BlockSpec((tr, tn), lambda j, i: (i, j)),
        out_shape=jax.ShapeDtypeStruct((S, F), BF16), compiler_params=_cp(("parallel", "parallel")),
    )(u, u, u, u, u, u, conv_w, conv_w, conv_b, conv_b)


def _conv_gate_bwd(u, da, conv_w, conv_b, name):
    S, F2 = u.shape
    F = F2 // 2
    tr, tn = _ffn_tiles(S, F)
    nj, ni = F // tn, S // tr

    def body(gm, gp, gn, vm, vp, vn, wg, wv, bg, bv, da_ref, dc_ref, dwg_ref, dwv_ref, dbg_ref, dbv_ref):
        i = pl.program_id(1)

        @pl.when(i == 0)
        def _():
            for r in (dwg_ref, dwv_ref, dbg_ref, dbv_ref):
                r[...] = jnp.zeros_like(r)

        g_rows = _halo_rows(gm, gp, gn, i, ni)
        v_rows = _halo_rows(vm, vp, vn, i, ni)
        cg = _conv3(wg, bg, *g_rows)
        cv = _conv3(wv, bv, *v_rows)
        sg = _sigmoid(cg)
        d = da_ref[...].astype(F32)
        dcv = d * (cg * sg)
        dcg = d * cv * (sg * (1.0 + cg * (1.0 - sg)))
        dc_ref[0] = dcg.astype(dc_ref.dtype)
        dc_ref[1] = dcv.astype(dc_ref.dtype)
        for t in range(3):
            dwg_ref[t:t + 1, :] += jnp.sum(dcg * g_rows[t], axis=0, keepdims=True)
            dwv_ref[t:t + 1, :] += jnp.sum(dcv * v_rows[t], axis=0, keepdims=True)
        dbg_ref[...] += jnp.sum(dcg, axis=0, keepdims=True)
        dbv_ref[...] += jnp.sum(dcv, axis=0, keepdims=True)

    wspec = lambda off: pl.BlockSpec((3, tn), lambda j, i: (0, j + off))
    bspec = lambda off: pl.BlockSpec((1, tn), lambda j, i: (0, j + off))
    tile = pl.BlockSpec((tr, tn), lambda j, i: (i, j))
    outs = pl.pallas_call(
        body, name=name, grid=(nj, ni),
        in_specs=[*_halo_specs(tr, tn, S, lambda j: j), *_halo_specs(tr, tn, S, lambda j: j + nj),
                  wspec(0), wspec(nj), bspec(0), bspec(nj), tile],
        out_specs=[pl.BlockSpec((2, tr, tn), lambda j, i: (0, i, j)), wspec(0), wspec(0), bspec(0), bspec(0)],
        out_shape=[jax.ShapeDtypeStruct((2, S, F), BF16),
                   jax.ShapeDtypeStruct((3, F), F32), jax.ShapeDtypeStruct((3, F), F32),
                   jax.ShapeDtypeStruct((1, F), F32), jax.ShapeDtypeStruct((1, F), F32)],
        compiler_params=_cp(("parallel", "arbitrary")),
    )(u, u, u, u, u, u, conv_w, conv_w, conv_b, conv_b, da)
    dc, dwg, dwv, dbg, dbv = outs
    return dc, jnp.concatenate([dwg, dwv], axis=1), jnp.concatenate([dbg, dbv], axis=1)


def _conv_transpose(dc, w, name):
    _, S, F = dc.shape
    tr, tn = _ffn_tiles(S, F, rows=(2048, 1024, 512, 256, 128, 16))
    nj, ni = F // tn, S // tr
    hb = tr // HALO
    last = S // HALO - 1

    def body(m, p, n, w_ref, o_ref):
        up, mid, dn = _halo_rows(m, p, n, pl.program_id(2), ni)
        o_ref[...] = (w_ref[0:1, :] * dn + w_ref[1:2, :] * mid + w_ref[2:3, :] * up).astype(o_ref.dtype)

    tile = pl.BlockSpec((None, tr, tn), lambda h, j, i: (h, i, j))
    return pl.pallas_call(
        body, name=name, grid=(2, nj, ni),
        in_specs=[tile,
                  pl.BlockSpec((None, HALO, tn), lambda h, j, i: (h, jnp.maximum(i * hb - 1, 0), j)),
                  pl.BlockSpec((None, HALO, tn), lambda h, j, i: (h, jnp.minimum((i + 1) * hb, last), j)),
                  pl.BlockSpec((3, tn), lambda h, j, i: (0, j + h * nj))],
        out_specs=tile, out_shape=jax.ShapeDtypeStruct((2, S, F), BF16),
        compiler_params=_cp(("parallel", "parallel", "parallel")),
    )(dc, dc, dc, w)


def _ffn_fwd(x, norm_g, w_up, conv_w, conv_b, w_down, tag, next_blocks=None):
    hn = _rmsnorm_fwd(x, norm_g, BF16, f"ffn_norm_{tag}")
    if next_blocks is None:
        u = _mm(hn, w_up, out_dtype=BF16, name=f"ffn_up_{tag}")
    else:
        u, over_ici = _mm(hn, w_up, out_dtype=BF16, name=f"ffn_up_{tag}", hosted=_gather_halves_ici(next_blocks))
    a = _conv_gate_fwd(u, conv_w, conv_b, f"ffn_gate_{tag}")
    if next_blocks is None:
        x_new, landed = _mm(a, w_down, name=f"ffn_down_{tag}", epi=_add_epi, epi_in=(x,)), None
    else:
        x_new, over_d2d = _mm(a, w_down, name=f"ffn_down_{tag}", epi=_add_epi, epi_in=(x,),
                              hosted=_gather_halves_d2d(over_ici))
        landed = (over_ici, over_d2d)
    return x_new, (x, hn, u, a), landed


def _ffn_bwd(dres, saved, norm_g, w_up, conv_w, conv_b, w_down, tag, reducer=None):
    x, hn, u, a = saved
    dres, dres_b = dres
    if reducer is None:
        da = _mm(dres_b, w_down, tb=True, out_dtype=BF16, name=f"ffn_da_{tag}")
    else:
        da, landed = _mm(dres_b, w_down, tb=True, out_dtype=BF16, name=f"ffn_da_{tag}", hosted=reducer.sibling())
    dw_down = _mm(a, dres_b, ta=True, name=f"ffn_dwdown_{tag}", pieces='row')
    dc, dconv_w, dconv_b = _conv_gate_bwd(u, da, conv_w, conv_b, f"ffn_gate_bwd_{tag}")
    du = _conv_transpose(dc, conv_w, f"ffn_convt_{tag}")
    if reducer is None:
        dhn = _mm(du, w_up, tb=True, name=f"ffn_dhn_{tag}")
        dw_up = _mm(hn, du, ta=True, name=f"ffn_dwup_{tag}", pieces='col')
        reduced = None
    else:
        dhn, landed = _mm(du, w_up, tb=True, name=f"ffn_dhn_{tag}", hosted=reducer.chips(landed))
        dw_up, landed = _mm(hn, du, ta=True, name=f"ffn_dwup_{tag}", pieces='col', hosted=reducer.swap(landed))
        reduced = reducer.finish(landed)
    dres, dg = _rmsnorm_bwd(x, norm_g, dhn, dres, f"ffn_norm_bwd_{tag}")
    return dres, (dg, dw_up, dconv_w, dconv_b, dw_down), reduced


ATTN_KEYS = 3 * ATTN_BLOCK


def _attn_window(i, S, reps):
    ks = pl.multiple_of(jnp.clip((i - 1) * ATTN_BLOCK, 0, S - ATTN_KEYS), ATTN_BLOCK)
    shape = (reps * ATTN_BLOCK, ATTN_KEYS)
    qpos = i * ATTN_BLOCK + lax.rem(lax.broadcasted_iota(jnp.int32, shape, 0), ATTN_BLOCK)
    kpos = ks + lax.broadcasted_iota(jnp.int32, shape, 1)
    arel = jnp.abs(kpos - qpos)
    return ks, arel.astype(F32), arel <= ATTN_BLOCK


ATTN_PAIRS = ATTN_GROUP // 2


def _stack_pairs(ref, kvh):
    c0 = kvh * ATTN_PAIRS * LANES
    return jnp.concatenate([ref[:, c0 + t * LANES:c0 + (t + 1) * LANES] for t in range(ATTN_PAIRS)], axis=0)


def _even_odd_operands(ref, rows, kvh):
    slab = ref[rows, (kvh // 2) * LANES:(kvh // 2 + 1) * LANES].astype(F32)
    other = pltpu.roll(slab, ATTN_DH, 1)
    low = lax.broadcasted_iota(jnp.int32, slab.shape, 1) < ATTN_DH
    lo_src, hi_src = (slab, other) if kvh % 2 == 0 else (other, slab)
    return jnp.where(low, lo_src, 0.0).astype(BF16), jnp.where(low, 0.0, hi_src).astype(BF16)


def _per_pair_rows(values):
    blk = lax.broadcasted_iota(jnp.int32, (ATTN_PAIRS * ATTN_BLOCK, 1), 0) // ATTN_BLOCK
    col = jnp.full(blk.shape, values[0], F32)
    for t in range(1, ATTN_PAIRS):
        col = jnp.where(blk == t, values[t], col)
    return col


def _attn_heads(kvh, parity, H):
    heads = [kvh * ATTN_GROUP + 2 * t + parity for t in range(ATTN_PAIRS)]
    return heads, [2.0 ** (-8.0 * (h + 1) / H) for h in heads]


def _attn_probs(q, k, slope, sink, arel, valid):
    s = lax.dot_general(q, k, (((1,), (1,)), ((), ())), preferred_element_type=F32) * (ATTN_DH ** -0.5)
    s = jnp.where(valid, s - slope * arel, -jnp.inf)
    m = jnp.maximum(jnp.max(s, axis=-1, keepdims=True), sink)
    p = jnp.exp(s - m)
    es = jnp.exp(sink - m)
    inv = 1.0 / (jnp.sum(p, axis=-1, keepdims=True) + es)
    return p * inv, es * inv


def _attn_specs(S, D):
    H = D // ATTN_DH
    KVW = (H // ATTN_GROUP) * ATTN_DH
    q_spec = pl.BlockSpec((ATTN_BLOCK, D), lambda i: (i, 0))
    k_spec = pl.BlockSpec((S, KVW), lambda i: (0, D // KVW))
    v_spec = pl.BlockSpec((S, KVW), lambda i: (0, D // KVW + 1))
    return H, KVW, q_spec, k_spec, v_spec


def _attn_fwd(qkv, sink, name):
    S = qkv.shape[0]
    D = qkv.shape[1] * ATTN_GROUP // (ATTN_GROUP + 2)
    H, KVW, q_spec, k_spec, v_spec = _attn_specs(S, D)

    def body(q_ref, k_ref, v_ref, sink_ref, o_ref):
        ks, arel, valid = _attn_window(pl.program_id(0), S, ATTN_PAIRS)
        rows = pl.ds(ks, ATTN_KEYS)
        for kvh in range(H // ATTN_GROUP):
            q = _stack_pairs(q_ref, kvh)
            out = None
            for parity, k, v in zip((0, 1), _even_odd_operands(k_ref, rows, kvh), _even_odd_operands(v_ref, rows, kvh)):
                heads, slopes = _attn_heads(kvh, parity, H)
                p, _ = _attn_probs(q, k, _per_pair_rows(slopes), _per_pair_rows([sink_ref[h] for h in heads]),
                                   arel, valid)
                part = jnp.dot(p.astype(BF16), v, preferred_element_type=F32)
                out = part if out is None else out + part
            c0 = kvh * ATTN_PAIRS * LANES
            for t in range(ATTN_PAIRS):
                o_ref[:, c0 + t * LANES:c0 + (t + 1) * LANES] = out[t * ATTN_BLOCK:(t + 1) * ATTN_BLOCK].astype(o_ref.dtype)

    return pl.pallas_call(
        body, name=name, grid=(S // ATTN_BLOCK,),
        in_specs=[q_spec, k_spec, v_spec, pl.BlockSpec(memory_space=pltpu.SMEM)],
        out_specs=q_spec, out_shape=jax.ShapeDtypeStruct((S, D), BF16),
        compiler_params=_cp(("parallel",)),
    )(qkv, qkv, qkv, sink)


def _attn_bwd(qkv, sink, do, name):
    S = qkv.shape[0]
    D = qkv.shape[1] * ATTN_GROUP // (ATTN_GROUP + 2)
    H, KVW, q_spec, k_spec, v_spec = _attn_specs(S, D)
    scale = ATTN_DH ** -0.5
    nt = (((1,), (1,)), ((), ()))
    tn = (((0,), (0,)), ((), ()))

    def body(q_ref, k_ref, v_ref, sink_ref, do_ref, dq_ref, dk_ref, dv_ref, ds_ref):
        @pl.when(pl.program_id(0) == 0)
        def _():
            dk_ref[...] = jnp.zeros_like(dk_ref)
            dv_ref[...] = jnp.zeros_like(dv_ref)
            ds_ref[...] = jnp.zeros_like(ds_ref)

        ks, arel, valid = _attn_window(pl.program_id(0), S, ATTN_PAIRS)
        rows = pl.ds(ks, ATTN_KEYS)
        low = lax.broadcasted_iota(jnp.int32, (ATTN_KEYS, LANES), 1) < ATTN_DH
        for kvh in range(H // ATTN_GROUP):
            q = _stack_pairs(q_ref, kvh)
            d_o = _stack_pairs(do_ref, kvh)
            dq = None
            dk_halves, dv_halves = [], []
            for parity, k, v in zip((0, 1), _even_odd_operands(k_ref, rows, kvh), _even_odd_operands(v_ref, rows, kvh)):
                heads, slopes = _attn_heads(kvh, parity, H)
                p, p_sink = _attn_probs(q, k, _per_pair_rows(slopes), _per_pair_rows([sink_ref[h] for h in heads]),
                                        arel, valid)
                dp = lax.dot_general(d_o, v, nt, preferred_element_type=F32)
                delta = jnp.sum(p * dp, axis=-1, keepdims=True)
                dsc = (p * (dp - delta)).astype(BF16)
                dsink = -p_sink * delta
                for t, h in enumerate(heads):
                    ds_ref[:, h:h + 1] += dsink[t * ATTN_BLOCK:(t + 1) * ATTN_BLOCK]
                part = jnp.dot(dsc, k, preferred_element_type=F32)
                dq = part if dq is None else dq + part
                dk_halves.append(lax.dot_general(dsc, q, tn, preferred_element_type=F32))
                dv_halves.append(lax.dot_general(p.astype(BF16), d_o, tn, preferred_element_type=F32))
            c0 = kvh * ATTN_PAIRS * LANES
            for t in range(ATTN_PAIRS):
                dq_ref[:, c0 + t * LANES:c0 + (t + 1) * LANES] = (
                    dq[t * ATTN_BLOCK:(t + 1) * ATTN_BLOCK] * scale).astype(dq_ref.dtype)
            slab = slice((kvh // 2) * LANES, (kvh // 2 + 1) * LANES)
            mine = low if kvh % 2 == 0 else jnp.logical_not(low)
            for ref, (even, odd), mult in ((dk_ref, dk_halves, scale), (dv_ref, dv_halves, 1.0)):
                both = jnp.where(low, even, odd)
                total = both + pltpu.roll(both, ATTN_DH, 1)
                ref[rows, slab] += jnp.where(mine, total * mult, 0.0)

    kv_out = pl.BlockSpec((S, KVW), lambda i: (0, 0))
    return pl.pallas_call(
        body, name=name, grid=(S // ATTN_BLOCK,),
        in_specs=[q_spec, k_spec, v_spec, pl.BlockSpec(memory_space=pltpu.SMEM), q_spec],
        out_specs=[q_spec, kv_out, kv_out, pl.BlockSpec((ATTN_BLOCK, H), lambda i: (0, 0))],
        out_shape=[jax.ShapeDtypeStruct((S, D), BF16), jax.ShapeDtypeStruct((S, KVW), F32),
                   jax.ShapeDtypeStruct((S, KVW), F32), jax.ShapeDtypeStruct((ATTN_BLOCK, H), F32)],
        compiler_params=_cp(("arbitrary",)),
    )(qkv, qkv, qkv, sink, do)


def _mm_hosting(exchange, a, b, **kw):
    if exchange is None:
        return _mm(a, b, **kw), None
    return _mm(a, b, hosted=exchange, **kw)


def _attn_layer_fwd(x, norm_g, w_qkv, w_o, sink, tag, next_blocks=None):
    hn = _rmsnorm_fwd(x, norm_g, BF16, f"attn_norm_{tag}")
    qkv, over_ici = _mm_hosting(None if next_blocks is None else _gather_halves_ici(next_blocks), hn, w_qkv,
                                out_dtype=BF16, name=f"attn_qkv_{tag}")
    o = _attn_fwd(qkv, sink, f"attn_core_{tag}")
    x_new, over_d2d = _mm_hosting(None if next_blocks is None else _gather_halves_d2d(over_ici), o, w_o,
                                  name=f"attn_out_{tag}", epi=_add_epi, epi_in=(x,))
    return x_new, (x, hn, qkv, o), None if next_blocks is None else (over_ici, over_d2d)


def _attn_layer_bwd(dres, saved, norm_g, w_qkv, w_o, sink, tag, reducer=None):
    x, hn, qkv, o = saved
    dres, dres_b = dres
    do, landed = _mm_hosting(reducer and reducer.sibling(), dres_b, w_o, tb=True, out_dtype=BF16, name=f"attn_do_{tag}")
    dw_o = _mm(o, dres_b, ta=True, name=f"attn_dwo_{tag}", pieces='row')
    dq, dk, dv, dsink = _attn_bwd(qkv, sink, do, f"attn_core_bwd_{tag}")
    dqkv = jnp.concatenate([dq, dk.astype(BF16), dv.astype(BF16)], axis=1)
    dhn, landed = _mm_hosting(reducer and reducer.chips(landed), dqkv, w_qkv, tb=True, name=f"attn_dhn_{tag}")
    dw_qkv, landed = _mm_hosting(reducer and reducer.swap(landed), hn, dqkv, ta=True, name=f"attn_dwqkv_{tag}",
                                 pieces='col')
    dres, dg = _rmsnorm_bwd(x, norm_g, dhn, dres, f"attn_norm_bwd_{tag}")
    return dres, (dg, dw_qkv, dw_o, jnp.sum(dsink, axis=0)), reducer and reducer.finish(landed)


MLA_W = 2 * LANES
MLA_DPAD = 2 * MLA_LORA + LANES
MLA_SCALE = (MLA_NOPE + MLA_ROPE) ** -0.5
MLA_TILES = (1024, 512, 256, 128)
MLA_ROW_GROUP = 256
LOG2E = math.log2(math.e)
LN2 = math.log(2.0)


def _rope_tables(S):
    half = MLA_ROPE // 2
    pos = jnp.arange(S, dtype=F32)
    inv = ROPE_THETA ** (-jnp.arange(half, dtype=F32) / half)
    ang = pos[:, None] * inv[None, :]
    cos, sin = jnp.cos(ang), jnp.sin(ang)
    z = jnp.zeros((S, LANES - 2 * half), F32)
    zh = jnp.zeros((S, half), F32)
    return (jnp.concatenate([cos, cos, z], axis=1), jnp.concatenate([-sin, zh, z], axis=1),
            jnp.concatenate([zh, sin, z], axis=1))


def _rope(t, ca, sb, sc):
    return t * ca + pltpu.roll(t, 96, 1) * sb + pltpu.roll(t, 32, 1) * sc


def _rope_t(d, ca, sb, sc):
    return d * ca + pltpu.roll(d * sb, 32, 1) + pltpu.roll(d * sc, 96, 1)


def _rms(xf, g):
    return (xf * lax.rsqrt(jnp.mean(xf * xf, axis=-1, keepdims=True) + RMS_EPS)) * g


def _mla_prep(d, qn, kvn, tabs, name):
    S = d.shape[0]
    tr = _pick(S, (256, 128, 8))
    L = MLA_LORA

    def body(d_ref, qn_ref, kvn_ref, ca, sb, sc, cq_ref, ckv_ref, kr_ref):
        cq_ref[...] = _rms(d_ref[:, :L], qn_ref[...]).astype(BF16)
        ckv_ref[...] = _rms(d_ref[:, L:2 * L], kvn_ref[...]).astype(BF16)
        kr_ref[...] = _rope(d_ref[:, 2 * L:], ca[...], sb[...], sc[...]).astype(BF16)

    row = lambda w: pl.BlockSpec((tr, w), lambda i: (i, 0))
    vec = pl.BlockSpec((1, L), lambda i: (0, 0))
    return pl.pallas_call(
        body, name=name, grid=(S // tr,),
        in_specs=[row(MLA_DPAD), vec, vec, row(LANES), row(LANES), row(LANES)],
        out_specs=[row(L), row(L), row(LANES)],
        out_shape=[jax.ShapeDtypeStruct((S, L), BF16), jax.ShapeDtypeStruct((S, L), BF16),
                   jax.ShapeDtypeStruct((S, LANES), BF16)],
        compiler_params=_cp(("parallel",)),
    )(d, qn, kvn, *tabs)


def _mla_prep_bwd(d, qn, kvn, tabs, dcq, dckv, dkr_h, name):
    S = d.shape[0]
    H = dkr_h.shape[0]
    tr = _pick(S, (256, 128, 8))
    L = MLA_LORA

    def body(d_ref, qn_ref, kvn_ref, ca, sb, sc, dcq_ref, dckv_ref, dkr_ref, dd_ref, dqn_ref, dkvn_ref):
        @pl.when(pl.program_id(0) == 0)
        def _():
            dqn_ref[...] = jnp.zeros_like(dqn_ref)
            dkvn_ref[...] = jnp.zeros_like(dkvn_ref)

        dx, dg = _rms_bwd_math(d_ref[:, :L], qn_ref[...], dcq_ref[...])
        dd_ref[:, :L] = dx.astype(BF16)
        dqn_ref[...] += dg
        dx, dg = _rms_bwd_math(d_ref[:, L:2 * L], kvn_ref[...], dckv_ref[...])
        dd_ref[:, L:2 * L] = dx.astype(BF16)
        dkvn_ref[...] += dg
        dkr = dkr_ref[0]
        for h in range(1, H):
            dkr = dkr + dkr_ref[h]
        dd_ref[:, 2 * L:] = _rope_t(dkr, ca[...], sb[...], sc[...]).astype(BF16)

    row = lambda w: pl.BlockSpec((tr, w), lambda i: (i, 0))
    vec = pl.BlockSpec((1, L), lambda i: (0, 0))
    return pl.pallas_call(
        body, name=name, grid=(S // tr,),
        in_specs=[row(MLA_DPAD), vec, vec, row(LANES), row(LANES), row(LANES), row(L), row(L),
                  pl.BlockSpec((H, tr, LANES), lambda i: (0, i, 0))],
        out_specs=[row(MLA_DPAD), vec, vec],
        out_shape=[jax.ShapeDtypeStruct((S, MLA_DPAD), BF16), jax.ShapeDtypeStruct((1, L), F32),
                   jax.ShapeDtypeStruct((1, L), F32)],
        compiler_params=_cp(("arbitrary",)),
    )(d, qn, kvn, *tabs, dcq, dckv, dkr_h)


def _heads_proj(a, w, out_dtype, name):
    S, K = a.shape
    H, _, n = w.shape
    tm = _pick(S, (1024, 512, 256, 128))
    return _gmm(a, w, grid=(S // tm, H, 1),
                a_spec=pl.BlockSpec((tm, K), lambda m, h, k: (m, 0)),
                b_spec=pl.BlockSpec((None, K, n), lambda m, h, k: (h, 0, 0)),
                o_spec=pl.BlockSpec((None, tm, n), lambda m, h, k: (h, m, 0)),
                out_shape=jax.ShapeDtypeStruct((H, S, n), out_dtype),
                dims=(((1,), (0,)), ((), ())), kax=2, acc_shape=(tm, n), name=name)


def _heads_proj_dx(dy, w, name):
    H, S, n = dy.shape
    K = w.shape[1]
    tm = _pick(S, (1024, 512, 256, 128))
    return _gmm(dy, w, grid=(S // tm, 1, H),
                a_spec=pl.BlockSpec((None, tm, n), lambda m, j, h: (h, m, 0)),
                b_spec=pl.BlockSpec((None, K, n), lambda m, j, h: (h, 0, 0)),
                o_spec=pl.BlockSpec((tm, K), lambda m, j, h: (m, 0)),
                out_shape=jax.ShapeDtypeStruct((S, K), F32),
                dims=(((1,), (1,)), ((), ())), kax=2, acc_shape=(tm, K), name=name)


def _heads_proj_dw(a, dy, name):
    S, K = a.shape
    H, _, n = dy.shape
    tk = _pick(S, (512, 256, 128))
    return _gmm(a, dy, grid=(H, 1, S // tk),
                a_spec=pl.BlockSpec((tk, K), lambda h, j, k: (k, 0)),
                b_spec=pl.BlockSpec((None, tk, n), lambda h, j, k: (h, k, 0)),
                o_spec=pl.BlockSpec((None, K, n), lambda h, j, k: (h, 0, 0)),
                out_shape=jax.ShapeDtypeStruct((H, K, n), F32),
                dims=(((0,), (0,)), ((), ())), kax=2, acc_shape=(K, n), name=name)


def _mla_rope_q(q_ext, tabs, bwd, name):
    H, S, _ = q_ext.shape
    tr = _pick(S, (512, 256, 128, 8))
    mult = 1.0 if bwd else MLA_SCALE * LOG2E

    def body(q_ref, ca, sb, sc, o_ref):
        o_ref[:, :LANES] = (q_ref[:, :LANES].astype(F32) * mult).astype(BF16)
        fn = _rope_t if bwd else _rope
        o_ref[:, LANES:] = (fn(q_ref[:, LANES:].astype(F32), ca[...], sb[...], sc[...]) * mult).astype(BF16)

    blk = pl.BlockSpec((None, tr, MLA_W), lambda i, h: (h, i, 0))
    tab = pl.BlockSpec((tr, LANES), lambda i, h: (i, 0))
    return pl.pallas_call(
        body, name=name, grid=(S // tr, H), in_specs=[blk, tab, tab, tab], out_specs=blk,
        out_shape=jax.ShapeDtypeStruct((H, S, MLA_W), BF16), compiler_params=_cp(("parallel", "parallel")),
    )(q_ext, *tabs)


def _col_to_row(col):
    n = col.shape[0]
    eye = lax.broadcasted_iota(jnp.int32, (n, n), 0) == lax.broadcasted_iota(jnp.int32, (n, n), 1)
    return jnp.sum(jnp.where(eye, col, 0.0), axis=0, keepdims=True)


def _mla_flash_fwd(q, kv, kr, name, tq=None, tk=None, unroll=1, splits=None):
    H, S, _ = q.shape
    tq = tq or _pick(S, (2048,) + MLA_TILES)
    tk = tk or _pick(S, (2048,) + MLA_TILES)
    splits = splits or max(1, tq // MLA_ROW_GROUP)

    def body(q_ref, kv_ref, kr_ref, o_ref, lse_ref, kbuf, vbuf):
        @pl.when(pl.program_id(1) == 0)
        def _():
            kbuf[:, :LANES] = kv_ref[:, :LANES]
            kbuf[:, LANES:] = kr_ref[...]
            vbuf[:, :LANES] = kv_ref[:, LANES:]
            vbuf[:, LANES:] = jnp.ones((S, LANES), BF16)

        sub = tq // splits
        qs = [q_ref[g * sub:(g + 1) * sub, :] for g in range(splits)]

        def step(c, carry):
            rows = pl.ds(pl.multiple_of(c * tk, tk), tk)
            k, v = kbuf[rows, :], vbuf[rows, :]
            out = []
            for qv, (m, acc) in zip(qs, carry):
                s = lax.dot_general(qv, k, (((1,), (1,)), ((), ())), preferred_element_type=F32)
                m_new = jnp.maximum(m, jnp.max(s, axis=-1, keepdims=True))
                p = jnp.exp2(s - m_new).astype(BF16)
                out.append((m_new, jnp.exp2(m - m_new) * acc + jnp.dot(p, v, preferred_element_type=F32)))
            return tuple(out)

        init = tuple((jnp.full((sub, 1), -jnp.inf, F32), jnp.zeros((sub, MLA_W), F32)) for _ in range(splits))
        for g, (m, acc) in enumerate(lax.fori_loop(0, S // tk, step, init, unroll=unroll)):
            l = acc[:, LANES:LANES + 1]
            o_ref[g * sub:(g + 1) * sub, :] = (acc[:, :LANES] / l).astype(o_ref.dtype)
            lse_ref[:, g * sub:(g + 1) * sub] = _col_to_row(m + jnp.log2(l))

    return pl.pallas_call(
        body, name=name, grid=(H, S // tq),
        in_specs=[pl.BlockSpec((None, tq, MLA_W), lambda h, i: (h, i, 0)),
                  pl.BlockSpec((None, S, MLA_W), lambda h, i: (h, 0, 0)),
                  pl.BlockSpec((S, LANES), lambda h, i: (0, 0))],
        out_specs=[pl.BlockSpec((tq, MLA_V), lambda h, i: (i, h)),
                   pl.BlockSpec((None, 1, tq), lambda h, i: (h, 0, i))],
        out_shape=[jax.ShapeDtypeStruct((S, H * MLA_V), BF16), jax.ShapeDtypeStruct((H, 1, S), F32)],
        scratch_shapes=[pltpu.VMEM((S, MLA_W), BF16), pltpu.VMEM((S, MLA_W), BF16)],
        compiler_params=_cp(("parallel", "arbitrary")),
    )(q, kv, kr)


def _mla_delta(o, do, H, name):
    S = o.shape[0]
    tq = _pick(S, (512, 256, 128))

    def body(o_ref, do_ref, d_ref):
        prod = o_ref[...].astype(F32) * do_ref[...].astype(F32)
        d_ref[...] = _col_to_row(jnp.sum(prod, axis=-1, keepdims=True))

    blk = pl.BlockSpec((tq, MLA_V), lambda i, h: (i, h))
    return pl.pallas_call(
        body, name=name, grid=(S // tq, H), in_specs=[blk, blk],
        out_specs=pl.BlockSpec((None, 1, tq), lambda i, h: (h, 0, i)),
        out_shape=jax.ShapeDtypeStruct((H, 1, S), F32), compiler_params=_cp(("parallel", "parallel")),
    )(o, do)


def _mla_flash_bwd(q, kv, kr, do, lse, delta, name, tq=None, tkv=None, unroll=1):
    H, S, _ = q.shape
    tq = tq or _pick(S, MLA_TILES)
    tkv = tkv or _pick(S, MLA_TILES)

    def body(q_ref, kv_ref, kr_ref, do_ref, lse_ref, dl_ref, dq_ref, dkv_ref, dkr_ref):
        @pl.when(pl.program_id(1) == 0)
        def _():
            dq_ref[...] = jnp.zeros_like(dq_ref)

        v = kv_ref[:, LANES:]
        k = jnp.concatenate([kv_ref[:, :LANES], kr_ref[...]], axis=1)

        def step(c, carry):
            dk, dv = carry
            start = pl.multiple_of(c * tq, tq)
            rows = pl.ds(start, tq)
            qv = q_ref[rows, :]
            d_o = do_ref[rows, :]
            s_t = lax.dot_general(k, qv, (((1,), (1,)), ((), ())), preferred_element_type=F32)
            p_t = jnp.exp2(s_t - lse_ref[:, rows])
            dv = dv + jnp.dot(p_t.astype(BF16), d_o, preferred_element_type=F32)
            dp_t = lax.dot_general(v, d_o, (((1,), (1,)), ((), ())), preferred_element_type=F32)
            ds_t = (p_t * (dp_t - dl_ref[:, rows])).astype(BF16)
            dk = dk + jnp.dot(ds_t, qv, preferred_element_type=F32)
            dq_ref[rows, :] += lax.dot_general(ds_t, k, (((0,), (0,)), ((), ())),
                                               preferred_element_type=F32) * MLA_SCALE
            return dk, dv

        dk, dv = lax.fori_loop(0, S // tq, step, (jnp.zeros((tkv, MLA_W), F32), jnp.zeros((tkv, MLA_V), F32)),
                               unroll=unroll)
        dkv_ref[:, :LANES] = (dk[:, :LANES] * LN2).astype(BF16)
        dkv_ref[:, LANES:] = dv.astype(BF16)
        dkr_ref[...] = dk[:, LANES:] * LN2

    stat = pl.BlockSpec((None, 1, S), lambda h, j: (h, 0, 0))
    return pl.pallas_call(
        body, name=name, grid=(H, S // tkv),
        in_specs=[pl.BlockSpec((None, S, MLA_W), lambda h, j: (h, 0, 0)),
                  pl.BlockSpec((None, tkv, MLA_W), lambda h, j: (h, j, 0)),
                  pl.BlockSpec((tkv, LANES), lambda h, j: (j, 0)),
                  pl.BlockSpec((S, MLA_V), lambda h, j: (0, h)), stat, stat],
        out_specs=[pl.BlockSpec((None, S, MLA_W), lambda h, j: (h, 0, 0)),
                   pl.BlockSpec((None, tkv, MLA_W), lambda h, j: (h, j, 0)),
                   pl.BlockSpec((None, tkv, LANES), lambda h, j: (h, j, 0))],
        out_shape=[jax.ShapeDtypeStruct((H, S, MLA_W), F32), jax.ShapeDtypeStruct((H, S, MLA_W), BF16),
                   jax.ShapeDtypeStruct((H, S, LANES), F32)],
        compiler_params=_cp(("parallel", "arbitrary")),
    )(q, kv, kr, do, lse, delta)


def _mla_weights(w_dqkv, w_uq, w_ukv):
    H = MLA_HEADS
    wd = jnp.pad(w_dqkv, ((0, 0), (0, MLA_DPAD - w_dqkv.shape[1])))
    wq = w_uq.reshape(MLA_LORA, H, MLA_NOPE + MLA_ROPE)
    wq = jnp.pad(wq, ((0, 0), (0, 0), (0, MLA_W - wq.shape[2]))).transpose(1, 0, 2)
    wkv = w_ukv.reshape(MLA_LORA, H, MLA_NOPE + MLA_V).transpose(1, 0, 2)
    return wd, wq, wkv


def _mla_layer_fwd(x, norm_g, wd, wq, wkv, w_o, qn, kvn, tag):
    S = x.shape[0]
    tabs = _rope_tables(S)
    hn = _rmsnorm_fwd(x, norm_g, BF16, f"mla_norm_{tag}")
    d = _mm(hn, wd, name=f"mla_down_{tag}")
    cq, ckv, kr = _mla_prep(d, qn, kvn, tabs, f"mla_prep_{tag}")
    q = _mla_rope_q(_heads_proj(cq, wq, F32, f"mla_uq_{tag}"), tabs, False, f"mla_ropeq_{tag}")
    kv = _heads_proj(ckv, wkv, BF16, f"mla_ukv_{tag}")
    o, lse = _mla_flash_fwd(q, kv, kr, f"mla_flash_{tag}")
    x_new = _mm(o, w_o, name=f"mla_out_{tag}", epi=_add_epi, epi_in=(x,))
    return x_new, (x, hn, d, cq, ckv, kr, q, kv, o, lse)


def _mla_layer_bwd(dres, saved, norm_g, wd, wq, wkv, w_o, qn, kvn, tag):
    x, hn, d, cq, ckv, kr, q, kv, o, lse = saved
    S = x.shape[0]
    H = MLA_HEADS
    tabs = _rope_tables(S)
    dres, dres_b = dres
    do = _mm(dres_b, w_o, tb=True, out_dtype=BF16, name=f"mla_do_{tag}")
    dw_o = _mm(o, dres_b, ta=True, name=f"mla_dwo_{tag}", pieces='row')
    delta = _mla_delta(o, do, H, f"mla_delta_{tag}")
    dq, dkv, dkr_h = _mla_flash_bwd(q, kv, kr, do, lse, delta, f"mla_flash_bwd_{tag}")
    dq_ext = _mla_rope_q(dq, tabs, True, f"mla_ropeq_bwd_{tag}")
    dwq = _heads_proj_dw(cq, dq_ext, f"mla_dwuq_{tag}")
    dcq = _heads_proj_dx(dq_ext, wq, f"mla_dcq_{tag}")
    dwkv = _heads_proj_dw(ckv, dkv, f"mla_dwukv_{tag}")
    dckv = _heads_proj_dx(dkv, wkv, f"mla_dckv_{tag}")
    dd, dqn, dkvn = _mla_prep_bwd(d, qn, kvn, tabs, dcq, dckv, dkr_h, f"mla_prep_bwd_{tag}")
    dhn = _mm(dd, wd, tb=True, name=f"mla_dhn_{tag}")
    dwd = _mm(hn, dd, ta=True, name=f"mla_dwd_{tag}")
    dres, dg = _rmsnorm_bwd(x, norm_g, dhn, dres, f"mla_norm_bwd_{tag}")
    dw_dqkv = dwd[:, :2 * MLA_LORA + MLA_ROPE]
    dw_uq = dwq.transpose(1, 0, 2)[:, :, :MLA_NOPE + MLA_ROPE].reshape(MLA_LORA, -1)
    dw_ukv = dwkv.transpose(1, 0, 2).reshape(MLA_LORA, -1)
    return dres, (dg, dw_dqkv, dqn, dkvn, dw_uq, dw_ukv, dw_o)


S5_CB = LANES
S5_SB = (S5_CB // SSM_GROUP_CH) * SSM_STATE
S5_ROWS = 1024


def _s5_disc(a_re, a_im, ls, b_re, b_im):
    step = jnp.exp(ls)
    mag = jnp.exp(step * a_re)
    lb_re = mag * jnp.cos(step * a_im)
    lb_im = mag * jnp.sin(step * a_im)
    n_re, n_im = lb_re - 1.0, lb_im
    den = a_re * a_re + a_im * a_im
    coef_re = (n_re * a_re + n_im * a_im) / den
    coef_im = (n_im * a_re - n_re * a_im) / den
    return lb_re, lb_im, coef_re * b_re - coef_im * b_im, coef_re * b_im + coef_im * b_re


def _s5_disc_fwd(a_re, a_im, ls, b_re, b_im, name):
    GN = a_re.shape[-1]

    def body(ar, ai, l, br, bi, o_lr, o_li, o_br, o_bi):
        for o, v in zip((o_lr, o_li, o_br, o_bi), _s5_disc(ar[...], ai[...], l[...], br[...], bi[...])):
            o[...] = v

    vec = pl.BlockSpec((None, 1, GN), lambda d: (d, 0, 0))
    mat = pl.BlockSpec((None, SSM_GROUP_CH, GN), lambda d: (d, 0, 0))
    sv = jax.ShapeDtypeStruct(a_re.shape, F32)
    sm = jax.ShapeDtypeStruct(b_re.shape, F32)
    return pl.pallas_call(body, name=name, grid=(2,), in_specs=[vec, vec, vec, mat, mat],
                          out_specs=[vec, vec, mat, mat], out_shape=[sv, sv, sm, sm],
                          compiler_params=_cp(("parallel",)))(a_re, a_im, ls, b_re, b_im)


def _s5_disc_bwd(a_re, a_im, ls, b_re, b_im, d_lr, d_li, d_br, d_bi, name):
    GN = a_re.shape[-1]

    def body(ar, ai, l, br, bi, g_lr, g_li, g_br, g_bi, o_ar, o_ai, o_l, o_br, o_bi):
        _, vjp = jax.vjp(_s5_disc, ar[...], ai[...], l[...], br[...], bi[...])
        for o, v in zip((o_ar, o_ai, o_l, o_br, o_bi), vjp((g_lr[...], g_li[...], g_br[...], g_bi[...]))):
            o[...] = v

    vec = pl.BlockSpec((None, 1, GN), lambda d: (d, 0, 0))
    mat = pl.BlockSpec((None, SSM_GROUP_CH, GN), lambda d: (d, 0, 0))
    sv = jax.ShapeDtypeStruct(a_re.shape, F32)
    sm = jax.ShapeDtypeStruct(b_re.shape, F32)
    return pl.pallas_call(body, name=name, grid=(2,), in_specs=[vec, vec, vec, mat, mat, vec, vec, mat, mat],
                          out_specs=[vec, vec, vec, mat, mat], out_shape=[sv, sv, sv, sm, sm],
                          compiler_params=_cp(("parallel",)))(a_re, a_im, ls, b_re, b_im, d_lr, d_li, d_br, d_bi)


def _cmul(ar, ai, br, bi):
    return ar * br - ai * bi, ar * bi + ai * br


def _segment_carries(lr, li, er, ei, n_steps, reverse):
    pr, pi = lr, li
    for _ in range(int(math.log2(n_steps))):
        pr, pi = _cmul(pr, pi, pr, pi)
    row = lax.broadcasted_iota(jnp.int32, er.shape, 0)
    edge = (SSM_SEGMENTS - 1) if reverse else 0
    shift = (SSM_SEGMENTS - 1) if reverse else 1
    cr = jnp.zeros_like(er)
    ci = jnp.zeros_like(ei)
    for _ in range(SSM_SEGMENTS - 1):
        tr_, ti_ = _cmul(pr, pi, cr, ci)
        cr = jnp.where(row == edge, 0.0, pltpu.roll(tr_ + er, shift, 0))
        ci = jnp.where(row == edge, 0.0, pltpu.roll(ti_ + ei, shift, 0))
    return cr, ci


def _s5_geometry(S, D):
    assert S % SSM_SEGMENTS == 0 and D % S5_CB == 0
    n_steps = S // SSM_SEGMENTS
    assert n_steps & (n_steps - 1) == 0, "segment length must be a power of two"
    rows = min(S5_ROWS, S)
    return n_steps, rows, S // rows, D // S5_CB


def _s5_scan(u, b_re, b_im, c_re, c_im, lam_re, lam_im, ends, descending, name):
    S, D = u.shape
    n_steps, rows, nch, ncb = _s5_geometry(S, D)
    full = ends is not None
    GN = ncb * S5_SB

    def body(*refs):
        if full:
            (u_ref, br_ref, bi_ref, cr_ref, ci_ref, lr_ref, li_ref, er_ref, ei_ref,
             xr_ref, xi_ref, y_ref, st_r, st_i, buf_r, buf_i) = refs
        else:
            u_ref, br_ref, bi_ref, lr_ref, li_ref, er_ref, ei_ref, st_r, st_i, buf_r, buf_i = refs
        lr = jnp.broadcast_to(lr_ref[...], (SSM_SEGMENTS, S5_SB))
        li = jnp.broadcast_to(li_ref[...], (SSM_SEGMENTS, S5_SB))

        @pl.when(pl.program_id(1) == 0)
        def _():
            if full:
                st_r[...], st_i[...] = _segment_carries(lr, li, er_ref[...], ei_ref[...], n_steps, descending)
            else:
                st_r[...] = jnp.zeros_like(st_r)
                st_i[...] = jnp.zeros_like(st_i)

        ub = u_ref[...].astype(BF16)
        buf_r[...] = jnp.dot(ub, br_ref[...], preferred_element_type=F32)
        buf_i[...] = jnp.dot(ub, bi_ref[...], preferred_element_type=F32)

        n_it = rows // SSM_SEGMENTS

        def step(i, carry):
            sr, si = carry
            i = n_it - 1 - i if descending else i
            r = pl.ds(pl.multiple_of(i * SSM_SEGMENTS, SSM_SEGMENTS), SSM_SEGMENTS)
            if full:
                xr_ref[r, :] = sr
                xi_ref[r, :] = si
            nr = lr * sr - li * si + buf_r[r, :]
            ni = lr * si + li * sr + buf_i[r, :]
            if full:
                buf_r[r, :] = nr
                buf_i[r, :] = ni
            return nr, ni

        sr, si = lax.fori_loop(0, n_it, step, (st_r[...], st_i[...]))
        st_r[...] = sr
        st_i[...] = si
        if full:
            y_ref[...] = (jnp.dot(buf_r[...].astype(BF16), cr_ref[...], preferred_element_type=F32)
                          - jnp.dot(buf_i[...].astype(BF16), ci_ref[...], preferred_element_type=F32))
        else:
            er_ref[...] = sr
            ei_ref[...] = si

    chunk = (lambda c: nch - 1 - c) if descending else (lambda c: c)
    u_spec = pl.BlockSpec((rows, S5_CB), lambda b, c: (chunk(c), b))
    bmat = pl.BlockSpec((None, S5_CB, S5_SB), lambda b, c: (b, 0, 0))
    cmat = pl.BlockSpec((None, S5_SB, S5_CB), lambda b, c: (b, 0, 0))
    lvec = pl.BlockSpec((1, S5_SB), lambda b, c: (0, b))
    evec = pl.BlockSpec((SSM_SEGMENTS, S5_SB), lambda b, c: (0, b))
    xblk = pl.BlockSpec((rows, S5_SB), lambda b, c: (chunk(c), b))
    scratch = [pltpu.VMEM((SSM_SEGMENTS, S5_SB), F32)] * 2 + [pltpu.VMEM((rows, S5_SB), F32)] * 2
    e_shape = jax.ShapeDtypeStruct((SSM_SEGMENTS, GN), F32)
    if full:
        x_shape = jax.ShapeDtypeStruct((S, GN), F32)
        return pl.pallas_call(
            body, name=name, grid=(ncb, nch),
            in_specs=[u_spec, bmat, bmat, cmat, cmat, lvec, lvec, evec, evec],
            out_specs=[xblk, xblk, u_spec], out_shape=[x_shape, x_shape, jax.ShapeDtypeStruct((S, D), F32)],
            scratch_shapes=scratch, compiler_params=_cp(("parallel", "arbitrary")),
        )(u, b_re, b_im, c_re, c_im, lam_re, lam_im, *ends)
    return pl.pallas_call(
        body, name=name, grid=(ncb, nch), in_specs=[u_spec, bmat, bmat, lvec, lvec],
        out_specs=[evec, evec], out_shape=[e_shape, e_shape],
        scratch_shapes=scratch, compiler_params=_cp(("parallel", "arbitrary")),
    )(u, b_re, b_im, lam_re, lam_im)


def _s5_scan_bwd(dy, u, xp, b_re, b_im, c_re, c_im, lam_re, lam_im, starts, descending, name):
    S, D = dy.shape
    n_steps, rows, nch, ncb = _s5_geometry(S, D)
    full = starts is not None
    GN = ncb * S5_SB
    nt = (((1,), (1,)), ((), ()))
    tn = (((0,), (0,)), ((), ()))

    def body(*refs):
        if full:
            (dy_ref, u_ref, xr_ref, xi_ref, br_ref, bi_ref, cr_ref, ci_ref, lr_ref, li_ref, gr_ref, gi_ref,
             du_ref, dbr_ref, dbi_ref, dcr_ref, dci_ref, dlr_ref, dli_ref, st_r, st_i, buf_r, buf_i) = refs
        else:
            dy_ref, cr_ref, ci_ref, lr_ref, li_ref, gr_ref, gi_ref, st_r, st_i, buf_r, buf_i = refs
        lr = jnp.broadcast_to(lr_ref[...], (SSM_SEGMENTS, S5_SB))
        li = jnp.broadcast_to(li_ref[...], (SSM_SEGMENTS, S5_SB))

        @pl.when(pl.program_id(1) == 0)
        def _():
            if full:
                st_r[...], st_i[...] = _segment_carries(lr, -li, gr_ref[...], gi_ref[...], n_steps, descending)
                for r in (dbr_ref, dbi_ref, dcr_ref, dci_ref, dlr_ref, dli_ref):
                    r[...] = jnp.zeros_like(r)
            else:
                st_r[...] = jnp.zeros_like(st_r)
                st_i[...] = jnp.zeros_like(st_i)

        dyb = dy_ref[...].astype(BF16)
        buf_r[...] = lax.dot_general(dyb, cr_ref[...], nt, preferred_element_type=F32)
        buf_i[...] = -lax.dot_general(dyb, ci_ref[...], nt, preferred_element_type=F32)
        n_it = rows // SSM_SEGMENTS

        def step(j, carry):
            gr, gi = carry
            j = n_it - 1 - j if descending else j
            r = pl.ds(pl.multiple_of(j * SSM_SEGMENTS, SSM_SEGMENTS), SSM_SEGMENTS)
            nr = lr * gr + li * gi + buf_r[r, :]
            ni = lr * gi - li * gr + buf_i[r, :]
            if full:
                buf_r[r, :] = nr
                buf_i[r, :] = ni
            return nr, ni

        gr, gi = lax.fori_loop(0, n_it, step, (st_r[...], st_i[...]))
        st_r[...] = gr
        st_i[...] = gi
        if not full:
            gr_ref[...] = gr
            gi_ref[...] = gi
            return
        g_r, g_i = buf_r[...], buf_i[...]
        xr, xi = xr_ref[...], xi_ref[...]
        dlr_ref[...] += jnp.sum(g_r * xr + g_i * xi, axis=0, keepdims=True)
        dli_ref[...] += jnp.sum(g_i * xr - g_r * xi, axis=0, keepdims=True)
        ub = u_ref[...].astype(BF16)
        gb_r, gb_i = g_r.astype(BF16), g_i.astype(BF16)
        du_ref[...] = (lax.dot_general(gb_r, br_ref[...], nt, preferred_element_type=F32)
                       + lax.dot_general(gb_i, bi_ref[...], nt, preferred_element_type=F32))
        dbr_ref[...] += lax.dot_general(ub, gb_r, tn, preferred_element_type=F32)
        dbi_ref[...] += lax.dot_general(ub, gb_i, tn, preferred_element_type=F32)
        lr_, li_ = lr_ref[...], li_ref[...]
        x_r = lr_ * xr - li_ * xi + jnp.dot(ub, br_ref[...], preferred_element_type=F32)
        x_i = lr_ * xi + li_ * xr + jnp.dot(ub, bi_ref[...], preferred_element_type=F32)
        dcr_ref[...] += lax.dot_general(x_r.astype(BF16), dyb, tn, preferred_element_type=F32)
        dci_ref[...] -= lax.dot_general(x_i.astype(BF16), dyb, tn, preferred_element_type=F32)

    rev = (lambda c: nch - 1 - c) if descending else (lambda c: c)
    u_spec = pl.BlockSpec((rows, S5_CB), lambda b, c: (rev(c), b))
    bmat = pl.BlockSpec((None, S5_CB, S5_SB), lambda b, c: (b, 0, 0))
    cmat = pl.BlockSpec((None, S5_SB, S5_CB), lambda b, c: (b, 0, 0))
    lvec = pl.BlockSpec((1, S5_SB), lambda b, c: (0, b))
    evec = pl.BlockSpec((SSM_SEGMENTS, S5_SB), lambda b, c: (0, b))
    xblk = pl.BlockSpec((rows, S5_SB), lambda b, c: (rev(c), b))
    scratch = [pltpu.VMEM((SSM_SEGMENTS, S5_SB), F32)] * 2 + [pltpu.VMEM((rows, S5_SB), F32)] * 2
    e_shape = jax.ShapeDtypeStruct((SSM_SEGMENTS, GN), F32)
    if full:
        return pl.pallas_call(
            body, name=name, grid=(ncb, nch),
            in_specs=[u_spec, u_spec, xblk, xblk, bmat, bmat, cmat, cmat, lvec, lvec, evec, evec],
            out_specs=[u_spec, bmat, bmat, cmat, cmat, lvec, lvec],
            out_shape=[jax.ShapeDtypeStruct((S, D), F32), jax.ShapeDtypeStruct(b_re.shape, F32),
                       jax.ShapeDtypeStruct(b_re.shape, F32), jax.ShapeDtypeStruct(c_re.shape, F32),
                       jax.ShapeDtypeStruct(c_re.shape, F32), jax.ShapeDtypeStruct((1, GN), F32),
                       jax.ShapeDtypeStruct((1, GN), F32)],
            scratch_shapes=scratch, compiler_params=_cp(("parallel", "arbitrary")),
        )(dy, u, *xp, b_re, b_im, c_re, c_im, lam_re, lam_im, *starts)
    return pl.pallas_call(
        body, name=name, grid=(ncb, nch), in_specs=[u_spec, cmat, cmat, lvec, lvec],
        out_specs=[evec, evec], out_shape=[e_shape, e_shape],
        scratch_shapes=scratch, compiler_params=_cp(("parallel", "arbitrary")),
    )(dy, c_re, c_im, lam_re, lam_im)


def _s5_perm(t):
    S, D = t.shape
    return t.reshape(SSM_SEGMENTS, S // SSM_SEGMENTS, D).transpose(1, 0, 2).reshape(S, D)


def _s5_unperm(t):
    S, D = t.shape
    return t.reshape(S // SSM_SEGMENTS, SSM_SEGMENTS, D).transpose(1, 0, 2).reshape(S, D)


def _s5_blockdiag_b(bb, ncb):
    gpb = S5_CB // SSM_GROUP_CH
    t = bb.reshape(SSM_GROUP_CH, ncb, gpb, SSM_STATE)
    return jnp.einsum('cbgn,gh->bgchn', t, jnp.eye(gpb, dtype=bb.dtype)).reshape(ncb, S5_CB, S5_SB)


def _s5_blockdiag_b_t(dblk):
    ncb = dblk.shape[0]
    gpb = S5_CB // SSM_GROUP_CH
    t = dblk.reshape(ncb, gpb, SSM_GROUP_CH, gpb, SSM_STATE)
    return jnp.einsum('bgchn,gh->cbgn', t, jnp.eye(gpb, dtype=dblk.dtype)).reshape(SSM_GROUP_CH, -1)


def _s5_blockdiag_c(c, ncb):
    gpb = S5_CB // SSM_GROUP_CH
    t = c.reshape(ncb, gpb, SSM_GROUP_CH, SSM_STATE)
    return jnp.einsum('bgcn,gh->bgnhc', t, jnp.eye(gpb, dtype=c.dtype)).reshape(ncb, S5_SB, S5_CB)


def _s5_blockdiag_c_t(dblk):
    ncb = dblk.shape[0]
    gpb = S5_CB // SSM_GROUP_CH
    t = dblk.reshape(ncb, gpb, SSM_STATE, gpb, SSM_GROUP_CH)
    return jnp.einsum('bgnhc,gh->bgcn', t, jnp.eye(gpb, dtype=dblk.dtype)).reshape(-1, SSM_GROUP_CH, SSM_STATE)


_GELU_C = math.sqrt(2.0 / math.pi)


def _gelu(y):
    return y * (0.5 * (1.0 + jnp.tanh(_GELU_C * (y + 0.044715 * (y * y * y)))))


def _gelu_grad(y):
    t = jnp.tanh(_GELU_C * (y + 0.044715 * (y * y * y)))
    return 0.5 * (1.0 + t) + 0.5 * y * (1.0 - t * t) * (_GELU_C * (1.0 + 3.0 * 0.044715 * y * y))


def _rowwise(fn, ins, outs, name, acc=()):
    S, D = next(a.shape for a in ins if a.shape[0] != 1)
    tr = _pick(S, (256, 128, 8))
    row = pl.BlockSpec((tr, D), lambda i: (i, 0))
    vec = pl.BlockSpec((1, D), lambda i: (0, 0))
    n_in = len(ins)

    def body(*refs):
        res = fn(*[r[...] for r in refs[:n_in]])
        for k, (o, v) in enumerate(zip(refs[n_in:], res)):
            if k in acc:
                @pl.when(pl.program_id(0) == 0)
                def _():
                    o[...] = jnp.zeros_like(o)
                o[...] += jnp.sum(v, axis=0, keepdims=True)
            else:
                o[...] = v.astype(o.dtype)

    return pl.pallas_call(
        body, name=name, grid=(S // tr,), in_specs=[vec if a.shape[0] == 1 else row for a in ins],
        out_specs=[vec if k in acc else row for k in range(len(outs))],
        out_shape=[jax.ShapeDtypeStruct((1, D) if k in acc else (S, D), dt) for k, dt in enumerate(outs)],
        compiler_params=_cp(("arbitrary",) if acc else ("parallel",)),
    )(*ins)


def _s5_params(p):
    G, N = p["a_re"].shape[1:]
    vec = lambda a: a.reshape(2, 1, G * N)
    ls = jnp.broadcast_to(p["log_step"][:, :, None], (2, G, N))
    bt = lambda b: b.transpose(0, 3, 1, 2).reshape(2, SSM_GROUP_CH, G * N)
    return vec(p["a_re"]), vec(p["a_im"]), vec(ls), bt(p["b_re"]), bt(p["b_im"])


def _s5_layer_fwd(x, norm_g, p, w_glu, tag):
    S, D = x.shape
    ncb = D // S5_CB
    xp = _s5_perm(x)
    hn = _rmsnorm_fwd(xp, norm_g, F32, f"s5_norm_{tag}")
    raw = _s5_params(p)
    lam_r, lam_i, bb_r, bb_i = _s5_disc_fwd(*raw, f"s5_disc_{tag}")
    dirs = []
    ys = []
    for dirn in range(2):
        mats = (_s5_blockdiag_b(bb_r[dirn], ncb).astype(BF16), _s5_blockdiag_b(bb_i[dirn], ncb).astype(BF16),
                _s5_blockdiag_c(p["c_re"][dirn], ncb).astype(BF16), _s5_blockdiag_c(p["c_im"][dirn], ncb).astype(BF16))
        lam = (lam_r[dirn], lam_i[dirn])
        ends = _s5_scan(hn, mats[0], mats[1], None, None, *lam, None, dirn == 1, f"s5_ends_{tag}_{dirn}")
        xr, xi, y = _s5_scan(hn, *mats, *lam, ends, dirn == 1, f"s5_scan_{tag}_{dirn}")
        dirs.append(((xr, xi), mats, lam))
        ys.append(y)
    ytot, z = _rowwise(lambda u, d, a, b: ((lambda y: (y, _gelu(y)))(d * u + a + b)),
                       [hn, p["d"], ys[0], ys[1]], [F32, BF16], f"s5_gelu_{tag}")
    t = _mm(z, w_glu, name=f"s5_glu_{tag}", epi=lambda r, b: r + b, epi_in=(p["b_glu"],))
    (x_new,) = _rowwise(lambda xx, zz, tt: (xx + zz.astype(F32) * jax.nn.sigmoid(tt),),
                        [xp, z, t], [F32], f"s5_out_{tag}")
    return _s5_unperm(x_new), (xp, hn, raw, dirs, ytot, z, t)


def _s5_layer_bwd(dres, saved, norm_g, p, w_glu, tag):
    x, hn, raw, dirs, ytot, z, t = saved
    S, D = x.shape
    G, N = p["a_re"].shape[1:]
    dres = _s5_perm(dres[0])

    def glu_bwd(do, zz, tt):
        sg = jax.nn.sigmoid(tt)
        dt = do * zz.astype(F32) * (sg * (1.0 - sg))
        return dt, do * sg, dt

    dt, dzd, db_glu = _rowwise(glu_bwd, [dres, z, t], [BF16, F32, F32], f"s5_out_bwd_{tag}", acc=(2,))
    dz = _mm(dt, w_glu, tb=True, name=f"s5_dz_{tag}", epi=_add_epi, epi_in=(dzd,))
    dw_glu = _mm(z, dt, ta=True, name=f"s5_dwglu_{tag}", pieces='row')

    def gelu_bwd(dzz, y, u, d):
        dy = dzz * _gelu_grad(y)
        return dy, dy * d, dy * u

    dy, du, dd = _rowwise(gelu_bwd, [dz, ytot, hn, p["d"]], [F32, F32, F32], f"s5_gelu_bwd_{tag}", acc=(2,))
    d_lr, d_li, d_bbr, d_bbi, d_cr, d_ci = [], [], [], [], [], []
    du = [du]
    for dirn in range(2):
        xp, mats, lam = dirs[dirn]
        starts = _s5_scan_bwd(dy, None, None, None, None, mats[2], mats[3], *lam, None, dirn == 0,
                              f"s5_starts_{tag}_{dirn}")
        dup, dbr, dbi, dcr, dci, dlr, dli = _s5_scan_bwd(dy, hn, xp, *mats, *lam, starts, dirn == 0,
                                                         f"s5_scan_bwd_{tag}_{dirn}")
        du.append(dup)
        d_lr.append(dlr)
        d_li.append(dli)
        d_bbr.append(_s5_blockdiag_b_t(dbr))
        d_bbi.append(_s5_blockdiag_b_t(dbi))
        d_cr.append(_s5_blockdiag_c_t(dcr))
        d_ci.append(_s5_blockdiag_c_t(dci))
    da_re, da_im, dls, db_re, db_im = _s5_disc_bwd(*raw, jnp.stack(d_lr), jnp.stack(d_li), jnp.stack(d_bbr),
                                                   jnp.stack(d_bbi), f"s5_disc_bwd_{tag}")
    dres, dg = _rmsnorm_bwd(x, norm_g, du, dres, f"s5_norm_bwd_{tag}")
    dres = tuple(_s5_unperm(t_) for t_ in dres)
    unb = lambda b: b.reshape(2, SSM_GROUP_CH, G, N).transpose(0, 2, 3, 1)
    grads = dict(a_re=da_re.reshape(2, G, N), a_im=da_im.reshape(2, G, N), log_step=dls.reshape(2, G, N).sum(-1),
                 b_re=unb(db_re), b_im=unb(db_im), c_re=jnp.stack(d_cr), c_im=jnp.stack(d_ci),
                 d=dd, w_glu=dw_glu, b_glu=db_glu, norm=dg)
    return dres, grads


def _adamw(w, g, m, v, name):
    R, C = w.shape
    tr = _pick(R, (512, 256, 128, 64, 32, 16, 8))
    tn = _pick(C, (512, 256, 128))

    def body(w_ref, g_ref, m_ref, v_ref, d_ref, nm_ref, nv_ref):
        gg = g_ref[...]
        m2 = ADAM_B1 * m_ref[...] + (1.0 - ADAM_B1) * gg
        v2 = ADAM_B2 * v_ref[...] + (1.0 - ADAM_B2) * (gg * gg)
        m_hat = m2 / (1.0 - ADAM_B1 ** ADAM_STEP)
        v_hat = v2 / (1.0 - ADAM_B2 ** ADAM_STEP)
        d_ref[...] = -ADAM_LR * (m_hat / (jnp.sqrt(v_hat) + ADAM_EPS) + ADAM_WD * w_ref[...])
        nm_ref[...] = m2
        nv_ref[...] = v2

    blk = pl.BlockSpec((tr, tn), lambda i, j: (i, j))
    shp = jax.ShapeDtypeStruct((R, C), F32)
    return pl.pallas_call(body, name=name, grid=(R // tr, C // tn), in_specs=[blk] * 4, out_specs=[blk] * 3,
                          out_shape=[shp] * 3, compiler_params=_cp(("parallel", "parallel")))(w, g, m, v)


def _add_selected(p, sel, others, name, also_bf16=False):
    K, _, M, C = p.shape
    tr = _pick(M, [t for t in (512, 256, 128, 64, 32, 16) if t * C * 4 <= 2 ** 21])
    n_o = len(others)

    def body(sel_ref, p_ref, *refs):
        acc = p_ref[...]
        for r in refs[:n_o]:
            acc = acc + r[...].astype(F32)
        refs[n_o][...] = acc
        if also_bf16:
            refs[n_o + 1][...] = acc.astype(BF16)

    blk = pl.BlockSpec((None, tr, C), lambda k, i, s: (k, i, 0))
    grid_spec = pltpu.PrefetchScalarGridSpec(
        num_scalar_prefetch=1, grid=(K, M // tr),
        in_specs=[pl.BlockSpec((None, None, tr, C), lambda k, i, s: (k, s[0], i, 0))] + [blk] * n_o,
        out_specs=[blk, blk] if also_bf16 else blk)
    shp = jax.ShapeDtypeStruct((K, M, C), F32)
    return pl.pallas_call(body, name=name, grid_spec=grid_spec,
                          out_shape=[shp, jax.ShapeDtypeStruct((K, M, C), BF16)] if also_bf16 else shp,
                          compiler_params=_cp(("parallel", "parallel")))(sel, p, *others)


def _sum_slots(a, own, me, name):
    n, R, C = a.shape
    tr = _pick(R, (512, 256, 128, 64, 32, 16, 8))

    def body(me_ref, a_ref, own_ref, o_ref):
        term = lambda k: jnp.where(me_ref[0] == k, own_ref[...], a_ref[k])
        acc = term(0)
        for k in range(1, n):
            acc = acc + term(k)
        o_ref[...] = acc

    grid_spec = pltpu.PrefetchScalarGridSpec(
        num_scalar_prefetch=1, grid=(R // tr,),
        in_specs=[pl.BlockSpec((n, tr, C), lambda i, s: (0, i, 0)), pl.BlockSpec((tr, C), lambda i, s: (i, 0))],
        out_specs=pl.BlockSpec((tr, C), lambda i, s: (i, 0)))
    return pl.pallas_call(body, name=name, grid_spec=grid_spec, out_shape=jax.ShapeDtypeStruct((R, C), F32),
                          compiler_params=_cp(("parallel",)))(me, a, own)


def _position():
    return lax.axis_index("x"), lax.axis_index("y"), lax.axis_index("c")


def _other_chips(x, y):
    return [(1 - x, y), (x, 1 - y), (1 - x, 1 - y)]


def _exchange_call(x, name):
    n_in, n_out = len(x.ins), len(x.out_shapes)

    def body(*refs):
        in_refs, out_refs, sems = refs[:n_in], refs[n_in:n_in + n_out], refs[n_in + n_out:]
        x.start(in_refs, out_refs, *sems)
        x.wait(in_refs, out_refs, *sems)

    hbm = pl.BlockSpec(memory_space=pltpu.HBM)
    return pl.pallas_call(body, name=name, in_specs=[hbm] * n_in, out_specs=[hbm] * n_out, out_shape=x.out_shapes,
                          scratch_shapes=x.sem_shapes())(*x.ins)


def _gather_chips(arrs):
    n = len(arrs)

    def plan(ins, outs):
        x, y, c = _position()
        me = 2 * x + y
        sends, recvs = [], []
        for px, py in _other_chips(x, y):
            for i in range(n):
                sends.append((ins[i], outs[i].at[me], (px, py, c)))
                recvs.append((outs[i].at[2 * px + py], (px, py, c)))
        return sends, recvs

    return _Exchange(arrs, [jax.ShapeDtypeStruct((4,) + a.shape, a.dtype) for a in arrs], 3 * n, plan)


def _gather_halves_ici(arrs):
    n = len(arrs)
    hr = [a.shape[1] // 2 for a in arrs]

    def plan(ins, outs):
        x, y, c = _position()
        me = 2 * x + y
        sends, recvs = [], []
        for px, py in _other_chips(x, y):
            for i in range(n):
                sends.append((ins[i].at[:, pl.ds(c * hr[i], hr[i])], outs[i].at[me, c], (px, py, c)))
                recvs.append((outs[i].at[2 * px + py, c], (px, py, c)))
        return sends, recvs

    shapes = [jax.ShapeDtypeStruct((N_CHIPS, 2, a.shape[0], a.shape[1] // 2, a.shape[2]), a.dtype) for a in arrs]
    return _Exchange(arrs, shapes, 3 * n, plan)


def _gather_halves_d2d(landed):
    n = len(landed)

    def plan(ins, outs):
        x, y, c = _position()
        sib = (x, y, 1 - c)
        sends, recvs = [], []
        for px, py in _other_chips(x, y):
            for i in range(n):
                sends.append((ins[i].at[2 * px + py, c], outs[i].at[2 * px + py, c], sib))
                recvs.append((outs[i].at[2 * px + py, 1 - c], sib))
        return sends, recvs

    return _Exchange(landed, [jax.ShapeDtypeStruct(a.shape, a.dtype) for a in landed], 3 * n, plan)


class _Reducer:
    def __init__(self, pieces, tags, sel_core, sel_chip):
        self.pieces, self.tags, self.sel_core, self.sel_chip = pieces, tags, sel_core, sel_chip

    def sibling(self):
        return _sibling_halves(self.pieces)

    def chips(self, from_sibling):
        self.pair = [_add_selected(p, self.sel_core, [r], f"reduce_add_pair_{t}", also_bf16=True)
                     for t, p, r in zip(self.tags, self.pieces, from_sibling)]
        return _scatter_chips([pb for _, pb in self.pair])

    def swap(self, from_chips):
        self.mine = [_add_selected(p.reshape((1,) + p.shape), self.sel_chip, [r[k:k + 1] for k in range(3)],
                                   f"reduce_add_chips_{t}")[0]
                     for t, (p, _), r in zip(self.tags, self.pair, from_chips)]
        return _sibling_swap(self.mine)

    def finish(self, theirs):
        return list(zip(self.mine, theirs))

    def run(self, tag):
        landed = _exchange_call(self.sibling(), f"reduce_sibling_{tag}")
        landed = _exchange_call(self.chips(landed), f"reduce_chips_{tag}")
        return self.finish(_exchange_call(self.swap(landed), f"reduce_join_{tag}"))


def _sibling_halves(pieces):
    n = len(pieces)

    def plan(ins, outs):
        x, y, c = _position()
        sib = (x, y, 1 - c)
        return [(ins[i].at[:, 1 - c], outs[i], sib) for i in range(n)], [(outs[i], sib) for i in range(n)]

    shapes = [jax.ShapeDtypeStruct((p.shape[0],) + p.shape[2:], p.dtype) for p in pieces]
    return _Exchange(pieces, shapes, n, plan)


def _scatter_chips(sums):
    n = len(sums)

    def plan(ins, outs):
        x, y, c = _position()
        sends, recvs = [], []
        for j, (px, py) in enumerate(_other_chips(x, y)):
            for i in range(n):
                sends.append((ins[i].at[2 * px + py], outs[i].at[j], (px, py, c)))
                recvs.append((outs[i].at[j], (px, py, c)))
        return sends, recvs

    return _Exchange(sums, [jax.ShapeDtypeStruct((3,) + s.shape[1:], s.dtype) for s in sums], 3 * n, plan)


def _sibling_swap(halves):
    n = len(halves)

    def plan(ins, outs):
        x, y, c = _position()
        sib = (x, y, 1 - c)
        return [(ins[i], outs[i], sib) for i in range(n)], [(outs[i], sib) for i in range(n)]

    return _Exchange(halves, [jax.ShapeDtypeStruct(h.shape, h.dtype) for h in halves], n, plan)


def _gather_all(v):
    rels = [(dx, dy, dc) for dx in (0, 1) for dy in (0, 1) for dc in (0, 1)][1:]

    def plan(ins, outs):
        x, y, c = _position()
        me = 4 * x + 2 * y + c
        flip = lambda a, d: 1 - a if d else a
        sends, recvs = [], []
        for dx, dy, dc in rels:
            px, py, pc = flip(x, dx), flip(y, dy), flip(c, dc)
            sends.append((ins[0], outs[0].at[me], (px, py, pc)))
            recvs.append((outs[0].at[4 * px + 2 * py + pc], (px, py, pc)))
        return sends, recvs

    return _Exchange([v], [jax.ShapeDtypeStruct((8,) + v.shape, v.dtype)], len(rels), plan)


WEIGHTS = ['mix_norm', 'ffn_norm', 'final_norm', 'attn_w_qkv', 'attn_w_o', 'attn_sink', 'ssm_a_re', 'ssm_a_im',
           'ssm_log_step', 'ssm_b_re', 'ssm_b_im', 'ssm_c_re', 'ssm_c_im', 'ssm_d', 'ssm_w_glu', 'ssm_b_glu',
           'mla_w_dqkv', 'mla_q_norm', 'mla_kv_norm', 'mla_w_uq', 'mla_w_ukv', 'mla_w_o', 'ffn_w_up',
           'ffn_conv_w', 'ffn_conv_b', 'ffn_w_down']
BIG = dict(attn_w_qkv='col', attn_w_o='row', ssm_w_glu='row', mla_w_dqkv='row', mla_w_uq='col',
           mla_w_ukv='col', mla_w_o='row', ffn_w_up='col', ffn_w_down='row')
SMALL_SHARDED = ('mla_q_norm', 'mla_kv_norm', 'ffn_conv_w')
N_CHIPS = 4


def _assemble(over_ici, over_d2d, own, chip, core, kind):
    L = over_ici.shape[2]
    half = lax.broadcasted_iota(jnp.int32, (1, 2, 1, 1, 1), 1)
    g = jnp.where(half == core, over_ici, over_d2d)
    own_halves = own.reshape(L, 2, g.shape[3], g.shape[4]).transpose(1, 0, 2, 3)
    slot = lax.broadcasted_iota(jnp.int32, (N_CHIPS, 1, 1, 1, 1), 0)
    g = jnp.where(slot == chip, own_halves[None], g)
    if kind == 'row':
        return g.transpose(2, 0, 1, 3, 4).reshape(L, -1, g.shape[4])
    return g.transpose(2, 1, 3, 0, 4).reshape(L, 2 * g.shape[3], -1)


def _to_pieces(w, kind):
    L, R, C = w.shape
    if kind == 'col':
        t = w.reshape(L, 2, R // 2, N_CHIPS, C // N_CHIPS).transpose(3, 1, 0, 2, 4)
    else:
        t = w.reshape(L, N_CHIPS, R // N_CHIPS, 2, C // 2).transpose(1, 3, 0, 2, 4)
    return t.reshape(N_CHIPS, 2, L * t.shape[3], t.shape[4])


def _from_halves(h, kind, shard_shape):
    L = shard_shape[0]
    t = h.reshape(2, L, -1, h.shape[2])
    t = t.transpose(1, 0, 2, 3) if kind == 'col' else t.transpose(1, 2, 0, 3)
    return t.reshape(shard_shape)


def _pack(arrs):
    flat = jnp.concatenate([a.reshape(-1) for a in arrs])
    pad = (-flat.shape[0]) % (8 * LANES)
    return jnp.pad(flat, (0, pad)).reshape(-1, LANES)


def _unpack(packed, like):
    flat = packed.reshape(-1)
    out, off = [], 0
    for a in like:
        out.append(flat[off:off + a.size].reshape(a.shape))
        off += a.size
    return out


def kernel(x, mix_norm, ffn_norm, final_norm, attn_w_qkv, attn_w_o, attn_sink, ssm_a_re, ssm_a_im, ssm_log_step, ssm_b_re, ssm_b_im, ssm_c_re, ssm_c_im, ssm_d, ssm_w_glu, ssm_b_glu, mla_w_dqkv, mla_q_norm, mla_kv_norm, mla_w_uq, mla_w_ukv, mla_w_o, ffn_w_up, ffn_conv_w, ffn_conv_b, ffn_w_down, loss_target, m_mix_norm, m_ffn_norm, m_final_norm, m_attn_w_qkv, m_attn_w_o, m_attn_sink, m_ssm_a_re, m_ssm_a_im, m_ssm_log_step, m_ssm_b_re, m_ssm_b_im, m_ssm_c_re, m_ssm_c_im, m_ssm_d, m_ssm_w_glu, m_ssm_b_glu, m_mla_w_dqkv, m_mla_q_norm, m_mla_kv_norm, m_mla_w_uq, m_mla_w_ukv, m_mla_w_o, m_ffn_w_up, m_ffn_conv_w, m_ffn_conv_b, m_ffn_w_down, v_mix_norm, v_ffn_norm, v_final_norm, v_attn_w_qkv, v_attn_w_o, v_attn_sink, v_ssm_a_re, v_ssm_a_im, v_ssm_log_step, v_ssm_b_re, v_ssm_b_im, v_ssm_c_re, v_ssm_c_im, v_ssm_d, v_ssm_w_glu, v_ssm_b_glu, v_mla_w_dqkv, v_mla_q_norm, v_mla_kv_norm, v_mla_w_uq, v_mla_w_ukv, v_mla_w_o, v_ffn_w_up, v_ffn_conv_w, v_ffn_conv_b, v_ffn_w_down):
    args = locals()
    w = {n: args[n] for n in WEIGHTS}
    mom = {n: args["m_" + n] for n in WEIGHTS}
    var = {n: args["v_" + n] for n in WEIGHTS}
    depth = mix_norm.shape[0]
    xs = x[0]
    target = loss_target[0]
    chip = 2 * lax.axis_index("x") + lax.axis_index("y")
    core = lax.axis_index("c")

    big_names = list(BIG)
    w_bf = {n: w[n].astype(BF16) for n in big_names}
    mixer_weights = {0: ('attn_w_qkv', 'attn_w_o'), 1: ('ssm_w_glu',),
                     2: ('mla_w_dqkv', 'mla_w_uq', 'mla_w_ukv', 'mla_w_o')}

    def mixer_items(i):
        return [(n, i // 3) for n in mixer_weights[i % 3]]

    def ffn_items(i):
        return [('ffn_w_up', i), ('ffn_w_down', i)]

    def layer_items(i):
        return mixer_items(i) + ffn_items(i)

    def blocks(items):
        return [w_bf[n][l:l + 1] for n, l in items]

    full = {}

    def install(items, landed):
        for (n, l), a, b, own in zip(items, *landed, blocks(items)):
            full[n, l] = _assemble(a, b, own, chip, core, BIG[n])[0]

    over_ici = _exchange_call(_gather_halves_ici(blocks(mixer_items(0))), "gather_first_ici")
    install(mixer_items(0), (over_ici, _exchange_call(_gather_halves_d2d(over_ici), "gather_first_d2d")))
    small_sharded = _exchange_call(_gather_chips([w[n] for n in SMALL_SHARDED]), "gather_small_weights")
    for n, g in zip(SMALL_SHARDED, small_sharded):
        slot = lax.broadcasted_iota(jnp.int32, (N_CHIPS,) + (1,) * w[n].ndim, 0)
        g = jnp.where(slot == chip, w[n][None], g)
        full[n] = jnp.moveaxis(g, 0, -2).reshape(g.shape[1:-1] + (-1,))

    def ffn_args(i):
        return (ffn_norm[i:i + 1], full["ffn_w_up", i], full["ffn_conv_w"][i], ffn_conv_b[i:i + 1], full["ffn_w_down", i])

    def s5_params(j):
        return dict(a_re=ssm_a_re[j], a_im=ssm_a_im[j], log_step=ssm_log_step[j], b_re=ssm_b_re[j], b_im=ssm_b_im[j],
                    c_re=ssm_c_re[j], c_im=ssm_c_im[j], d=ssm_d[j:j + 1], b_glu=ssm_b_glu[j:j + 1])

    def mla_args(j):
        wd, wq, wkv = _mla_weights(full["mla_w_dqkv", j], full["mla_w_uq", j], full["mla_w_ukv", j])
        return (wd, wq, wkv, full["mla_w_o", j], full["mla_q_norm"][j:j + 1], full["mla_kv_norm"][j:j + 1])

    h = xs
    saved = []
    for i in range(depth):
        kind, j = i % 3, i // 3
        g = mix_norm[i:i + 1]
        if kind == 0:
            h, sm, landed = _attn_layer_fwd(h, g, full["attn_w_qkv", j], full["attn_w_o", j], attn_sink[j], f"{i}",
                                            next_blocks=blocks(ffn_items(0)) if i == 0 else None)
            if landed is not None:
                install(ffn_items(0), landed)
        elif kind == 1:
            h, sm = _s5_layer_fwd(h, g, s5_params(j), full["ssm_w_glu", j], f"{i}")
        else:
            h, sm = _mla_layer_fwd(h, g, *mla_args(j), f"{i}")
        nxt = layer_items(i + 1) if i + 1 < depth else None
        h, sf, landed = _ffn_fwd(h, *ffn_args(i), f"{i}", next_blocks=nxt and blocks(nxt))
        if landed is not None:
            install(nxt, landed)
        saved.append((sm, sf))
    loss_part, dres, d_final = _loss_head(h, final_norm.reshape(1, -1), target, "loss_head")

    gl = {n: [None] * w[n].shape[0] for n in WEIGHTS if n != 'final_norm'}
    sel_core = jnp.reshape(core, (1,)).astype(jnp.int32)
    sel_chip = jnp.reshape(chip, (1,)).astype(jnp.int32)
    reduced = {}

    def reducer_for(items):
        pieces = [gl[n][l] if gl[n][l].ndim == 4 else _to_pieces(gl[n][l][None], BIG[n]) for n, l in items]
        return _Reducer(pieces, [f"{n}_{l}" for n, l in items], sel_core, sel_chip)

    pending = None
    for i in reversed(range(depth)):
        kind, j = i % 3, i // 3
        sm, sf = saved[i]
        dres, (dg, dwu, dcw, dcb, dwd), done = _ffn_bwd(dres, sf, *ffn_args(i), f"{i}", reducer=pending)
        if pending is not None:
            reduced.update(zip(layer_items(i + 1), done))
        gl['ffn_norm'][i], gl['ffn_w_up'][i], gl['ffn_conv_w'][i] = dg[0], dwu, dcw
        gl['ffn_conv_b'][i], gl['ffn_w_down'][i] = dcb[0], dwd
        g = mix_norm[i:i + 1]
        if kind == 0:
            own_ffn = reducer_for(ffn_items(0)) if i == 0 else None
            dres, (dg, dwq, dwo, dsk), done = _attn_layer_bwd(dres, sm, g, full["attn_w_qkv", j], full["attn_w_o", j],
                                                              attn_sink[j], f"{i}", reducer=own_ffn)
            if own_ffn is not None:
                reduced.update(zip(ffn_items(0), done))
            gl['attn_w_qkv'][j], gl['attn_w_o'][j], gl['attn_sink'][j] = dwq, dwo, dsk
        elif kind == 1:
            dres, gs = _s5_layer_bwd(dres, sm, g, s5_params(j), full["ssm_w_glu", j], f"{i}")
            dg = gs['norm']
            for k in ('a_re', 'a_im', 'log_step', 'b_re', 'b_im', 'c_re', 'c_im'):
                gl['ssm_' + k][j] = gs[k]
            gl['ssm_d'][j], gl['ssm_b_glu'][j], gl['ssm_w_glu'][j] = gs['d'][0], gs['b_glu'][0], gs['w_glu']
        else:
            dres, (dg, dwd_, dqn, dkvn, dwuq, dwukv, dwo) = _mla_layer_bwd(dres, sm, g, *mla_args(j), f"{i}")
            gl['mla_w_dqkv'][j], gl['mla_q_norm'][j], gl['mla_kv_norm'][j] = dwd_, dqn[0], dkvn[0]
            gl['mla_w_uq'][j], gl['mla_w_ukv'][j], gl['mla_w_o'][j] = dwuq, dwukv, dwo
        gl['mix_norm'][i] = dg[0]
        pending = reducer_for(layer_items(i) if i > 0 else mixer_items(0))
    reduced.update(zip(mixer_items(0), pending.run("first_layer")))
    local = {n: jnp.stack(v) for n, v in gl.items() if n not in BIG}
    local['final_norm'] = d_final[0]

    grads = {}
    for n in big_names:
        shards = []
        for l in range(w[n].shape[0]):
            a, b = reduced[n, l]
            halves = jnp.where(core == 0, jnp.stack([a, b]), jnp.stack([b, a]))
            shards.append(_from_halves(halves, BIG[n], (1,) + w[n].shape[1:]))
        grads[n] = jnp.concatenate(shards, axis=0)

    small_names = [n for n in WEIGHTS if n not in BIG]
    packed = _pack([local[n] for n in small_names])
    device = jnp.reshape(2 * chip + core, (1,)).astype(jnp.int32)
    summed = _sum_slots(_exchange_call(_gather_all(packed), "reduce_small")[0], packed, device, "reduce_small_sum")
    for n, gsum in zip(small_names, _unpack(summed, [local[n] for n in small_names])):
        if n in SMALL_SHARDED:
            width = w[n].shape[-1]
            gsum = lax.dynamic_slice_in_dim(gsum, chip * width, width, axis=gsum.ndim - 1)
        grads[n] = gsum

    delta, new_m, new_v = {}, {}, {}
    for n in big_names:
        two_d = lambda a: a.reshape(-1, a.shape[-1])
        d_, m_, v_ = _adamw(two_d(w[n]), two_d(grads[n]), two_d(mom[n]), two_d(var[n]), f"adamw_{n}")
        delta[n], new_m[n], new_v[n] = (t.reshape(w[n].shape) for t in (d_, m_, v_))
    outs = _adamw(*[_pack([src[n] for n in small_names]) for src in (w, grads, mom, var)], "adamw_small")
    for dst, packed in zip((delta, new_m, new_v), outs):
        for n, t in zip(small_names, _unpack(packed, [w[n] for n in small_names])):
            dst[n] = t

    loss = lax.psum(loss_part[0, 0], ("x", "y", "c"))
    return (loss, dres[0][None], *[grads[n] for n in WEIGHTS], *[delta[n] for n in WEIGHTS],
            *[new_m[n] for n in WEIGHTS], *[new_v[n] for n in WEIGHTS])
```

```python
import functools
import math

import numpy as np
import jax
import jax.numpy as jnp
from jax import lax
from jax.experimental import pallas as pl
from jax.experimental.pallas import tpu as pltpu

F32 = jnp.float32
BF16 = jnp.bfloat16
MESH = pl.DeviceIdType.MESH

RMS_EPS = 1e-6
ATTN_DH = 64
ATTN_GROUP = 8
ATTN_BLOCK = 128
SSM_GROUP_CH = 16
SSM_STATE = 64
SSM_SEGMENTS = 8
MLA_HEADS = 16
MLA_LORA = 512
MLA_NOPE = 128
MLA_ROPE = 64
MLA_V = 128
ROPE_THETA = 10000.0
LANES = 128
VMEM_LIMIT = 56 * 1024 * 1024

ADAM_LR = 0.001
ADAM_B1 = 0.9
ADAM_B2 = 0.999
ADAM_EPS = 1e-08
ADAM_WD = 0.01
ADAM_STEP = 10


def _cp(sem=None):
    kw = dict(vmem_limit_bytes=VMEM_LIMIT)
    if sem is not None:
        kw["dimension_semantics"] = sem
    return pltpu.CompilerParams(**kw)


MM_FULL_K = 2048
MM_FULL_N = 1536


def _pick(n, cands):
    for c in cands:
        if n % c == 0:
            return c
    return n


class _Exchange:
    def __init__(self, ins, out_shapes, n_remote, plan):
        self.ins, self.out_shapes, self.n_remote, self.plan = list(ins), list(out_shapes), n_remote, plan

    def _copy(self, k, src, dst, dev, send_sems, recv_sems):
        return pltpu.make_async_remote_copy(src_ref=src, dst_ref=dst, send_sem=send_sems.at[k],
                                            recv_sem=recv_sems.at[k], device_id=dev, device_id_type=MESH)

    def start(self, in_refs, out_refs, send_sems, recv_sems):
        sends, recvs = self.plan(in_refs, out_refs)
        assert len(sends) == len(recvs) == self.n_remote
        for k, (src, dst, dev) in enumerate(sends):
            self._copy(k, src, dst, dev, send_sems, recv_sems).start()

    def wait(self, in_refs, out_refs, send_sems, recv_sems):
        sends, recvs = self.plan(in_refs, out_refs)
        for k, (dst, dev) in enumerate(recvs):
            self._copy(k, dst, dst, dev, send_sems, recv_sems).wait_recv()
        for k, (src, dst, dev) in enumerate(sends):
            self._copy(k, src, dst, dev, send_sems, recv_sems).wait_send()

    def sem_shapes(self):
        return [pltpu.SemaphoreType.DMA((self.n_remote,)), pltpu.SemaphoreType.DMA((self.n_remote,))]


def _gmm(a, b, *, grid, a_spec, b_spec, o_spec, out_shape, dims, kax, acc_shape, name,
         epi=None, epi_in=(), epi_specs=(), hosted=None):
    nk = grid[kax]
    n_epi = len(epi_in)
    n_hin = len(hosted.ins) if hosted else 0
    n_hout = len(hosted.out_shapes) if hosted else 0

    def finish(r, e_refs, o_ref):
        if epi is not None:
            r = epi(r, *[e[...] for e in e_refs])
        o_ref[...] = r.astype(o_ref.dtype)

    def product(a_ref, b_ref):
        return lax.dot_general(a_ref[...].astype(BF16), b_ref[...].astype(BF16), dims, preferred_element_type=F32)

    def body(a_ref, b_ref, *rest):
        e_refs = rest[:n_epi]
        h_in = rest[n_epi:n_epi + n_hin]
        o_ref = rest[n_epi + n_hin]
        h_out = rest[n_epi + n_hin + 1:n_epi + n_hin + 1 + n_hout]
        scratch = rest[n_epi + n_hin + 1 + n_hout:]
        if hosted:
            sems = scratch[-2:]
            ids = [pl.program_id(i) for i in range(len(grid))]
            first = functools.reduce(jnp.logical_and, [i == 0 for i in ids])
            last = functools.reduce(jnp.logical_and, [i == n - 1 for i, n in zip(ids, grid)])
            pl.when(first)(lambda: hosted.start(h_in, h_out, *sems))
        if nk == 1:
            finish(product(a_ref, b_ref), e_refs, o_ref)
        else:
            acc_ref = scratch[0]
            k = pl.program_id(kax)

            @pl.when(k == 0)
            def _():
                acc_ref[...] = jnp.zeros_like(acc_ref)

            acc_ref[...] += product(a_ref, b_ref)

            @pl.when(k == nk - 1)
            def _():
                finish(acc_ref[...], e_refs, o_ref)
        if hosted:
            pl.when(last)(lambda: hosted.wait(h_in, h_out, *sems))

    sem = tuple("arbitrary" if (i == kax or hosted) else "parallel" for i in range(len(grid)))
    hbm = pl.BlockSpec(memory_space=pltpu.HBM)
    res = pl.pallas_call(
        body, name=name, grid=grid,
        in_specs=[a_spec, b_spec, *epi_specs] + [hbm] * n_hin,
        out_specs=[o_spec] + [hbm] * n_hout if hosted else o_spec,
        out_shape=[out_shape] + hosted.out_shapes if hosted else out_shape,
        scratch_shapes=([] if nk == 1 else [pltpu.VMEM(acc_shape, F32)]) + (hosted.sem_shapes() if hosted else []),
        compiler_params=_cp(sem),
    )(a, b, *epi_in, *(hosted.ins if hosted else ()))
    return (res[0], list(res[1:])) if hosted else res


def _mm(a, b, *, ta=False, tb=False, out_dtype=F32, name, epi=None, epi_in=(), pieces=None, hosted=None):
    a_parts = a.shape[0] if a.ndim == 3 else 1
    b_parts = b.shape[0] if b.ndim == 3 else 1
    assert not (ta and a_parts > 1) and not (tb and b_parts > 1)
    (K, M) = a.shape if ta else (a.shape[-1] * a_parts, a.shape[-2])
    (N, K2) = b.shape if tb else (b.shape[-1] * b_parts, b.shape[-2])
    assert K == K2, (a.shape, b.shape, ta, tb)
    m_unit = {None: M, 'col': M // 2, 'row': M // N_CHIPS}[pieces]
    n_unit = {None: N // b_parts, 'col': N // N_CHIPS, 'row': N // 2}[pieces]
    k_unit = K // a_parts
    tm = _pick(m_unit, (1024, 1408, 512, 256, 128))
    if (pieces is None and not ta and K > MM_FULL_K and M % 2048 == 0
            and not any(e.shape[0] != 1 for e in epi_in)):
        tm = 2048
    tn = n_unit if n_unit <= MM_FULL_N else _pick(n_unit, (1024, 1408, 512, 256, 128))
    tk = k_unit if k_unit <= MM_FULL_K else _pick(k_unit, (2048, 1408, 1024, 512, 256, 128))
    grid = (M // tm, N // tn, K // tk)
    if ta:
        a_spec = pl.BlockSpec((tk, tm), lambda i, j, k: (k, i))
    elif a_parts > 1:
        nkp = k_unit // tk
        a_spec = pl.BlockSpec((None, tm, tk), lambda i, j, k: (k // nkp, i, k % nkp))
    else:
        a_spec = pl.BlockSpec((tm, tk), lambda i, j, k: (i, k))
    if tb:
        b_spec = pl.BlockSpec((tn, tk), lambda i, j, k: (j, k))
    elif b_parts > 1:
        njp = (N // b_parts) // tn
        b_spec = pl.BlockSpec((None, tk, tn), lambda i, j, k: (j // njp, k, j % njp))
    else:
        b_spec = pl.BlockSpec((tk, tn), lambda i, j, k: (k, j))
    dims = (((0 if ta else 1,), (1 if tb else 0,)), ((), ()))
    epi_specs = [pl.BlockSpec((1, tn), lambda i, j, k: (0, j)) if e.shape[0] == 1
                 else pl.BlockSpec((tm, tn), lambda i, j, k: (i, j)) for e in epi_in]
    if pieces is None:
        o_spec = pl.BlockSpec((tm, tn), lambda i, j, k: (i, j))
        out_shape = jax.ShapeDtypeStruct((M, N), out_dtype)
    else:
        mi, nj = m_unit // tm, n_unit // tn
        if pieces == 'col':
            o_spec = pl.BlockSpec((None, None, tm, tn), lambda i, j, k: (j // nj, i // mi, i % mi, j % nj))
        else:
            o_spec = pl.BlockSpec((None, None, tm, tn), lambda i, j, k: (i // mi, j // nj, i % mi, j % nj))
        out_shape = jax.ShapeDtypeStruct((N_CHIPS, 2, m_unit, n_unit), out_dtype)
    return _gmm(a, b, grid=grid, a_spec=a_spec, b_spec=b_spec, o_spec=o_spec, out_shape=out_shape, dims=dims, kax=2,
                acc_shape=(tm, tn), name=name, epi=epi, epi_in=epi_in, epi_specs=epi_specs, hosted=hosted)


def _add_epi(r, res):
    return res + r


def _rmsnorm_fwd(x, g, out_dtype, name):
    S, D = x.shape
    tr = _pick(S, (256, 128, 8))

    def body(x_ref, g_ref, o_ref):
        xf = x_ref[...]
        r = lax.rsqrt(jnp.mean(xf * xf, axis=-1, keepdims=True) + RMS_EPS)
        o_ref[...] = ((xf * r) * g_ref[...]).astype(o_ref.dtype)

    return pl.pallas_call(
        body, name=name, grid=(S // tr,),
        in_specs=[pl.BlockSpec((tr, D), lambda i: (i, 0)), pl.BlockSpec((1, D), lambda i: (0, 0))],
        out_specs=pl.BlockSpec((tr, D), lambda i: (i, 0)),
        out_shape=jax.ShapeDtypeStruct((S, D), out_dtype), compiler_params=_cp(("parallel",)),
    )(x, g)


def _rms_bwd_math(xf, g, dy):
    r = lax.rsqrt(jnp.mean(xf * xf, axis=-1, keepdims=True) + RMS_EPS)
    xh = xf * r
    dxh = dy * g
    dx = r * (dxh - xh * jnp.mean(dxh * xh, axis=-1, keepdims=True))
    return dx, jnp.sum(dy * xh, axis=0, keepdims=True)


def _rmsnorm_bwd(x, g, dy, dres, name):
    S, D = x.shape
    tr = _pick(S, (256, 128, 8))
    dys = list(dy) if isinstance(dy, (list, tuple)) else [dy]
    n_dy = len(dys)

    def body(x_ref, g_ref, *refs):
        dy_refs = refs[:n_dy]
        dres_ref, dx_ref, dxb_ref, dg_ref = refs[n_dy:]

        @pl.when(pl.program_id(0) == 0)
        def _():
            dg_ref[...] = jnp.zeros_like(dg_ref)

        dy_sum = dy_refs[0][...].astype(F32)
        for r in dy_refs[1:]:
            dy_sum = dy_sum + r[...].astype(F32)
        dx, dg = _rms_bwd_math(x_ref[...], g_ref[...], dy_sum)
        tot = dres_ref[...] + dx
        dx_ref[...] = tot
        dxb_ref[...] = tot.astype(BF16)
        dg_ref[...] += dg

    row = pl.BlockSpec((tr, D), lambda i: (i, 0))
    vec = pl.BlockSpec((1, D), lambda i: (0, 0))
    dx, dxb, dg = pl.pallas_call(
        body, name=name, grid=(S // tr,), in_specs=[row, vec] + [row] * (n_dy + 1), out_specs=[row, row, vec],
        out_shape=[jax.ShapeDtypeStruct((S, D), F32), jax.ShapeDtypeStruct((S, D), BF16),
                   jax.ShapeDtypeStruct((1, D), F32)],
        compiler_params=_cp(("arbitrary",)),
    )(x, g, *dys, dres)
    return (dx, dxb), dg


def _loss_head(x, g, target, name):
    S, D = x.shape
    tr = _pick(S, (256, 128, 8))

    def body(x_ref, g_ref, t_ref, loss_ref, dx_ref, dxb_ref, dg_ref):
        @pl.when(pl.program_id(0) == 0)
        def _():
            dg_ref[...] = jnp.zeros_like(dg_ref)
            loss_ref[...] = jnp.zeros_like(loss_ref)

        xf = x_ref[...]
        gg = g_ref[...]
        r = lax.rsqrt(jnp.mean(xf * xf, axis=-1, keepdims=True) + RMS_EPS)
        e = (xf * r) * gg - t_ref[...]
        loss_ref[...] += 0.5 * jnp.sum(jnp.mean(e * e, axis=-1, keepdims=True), axis=0, keepdims=True)
        dx, dg = _rms_bwd_math(xf, gg, e * (1.0 / D))
        dx_ref[...] = dx
        dxb_ref[...] = dx.astype(BF16)
        dg_ref[...] += dg

    row = pl.BlockSpec((tr, D), lambda i: (i, 0))
    vec = pl.BlockSpec((1, D), lambda i: (0, 0))
    one = pl.BlockSpec((1, 1), lambda i: (0, 0))
    loss, dx, dxb, dg = pl.pallas_call(
        body, name=name, grid=(S // tr,), in_specs=[row, vec, row], out_specs=[one, row, row, vec],
        out_shape=[jax.ShapeDtypeStruct((1, 1), F32), jax.ShapeDtypeStruct((S, D), F32),
                   jax.ShapeDtypeStruct((S, D), BF16), jax.ShapeDtypeStruct((1, D), F32)],
        compiler_params=_cp(("arbitrary",)),
    )(x, g, target)
    return loss, (dx, dxb), dg


HALO = 16


def _shift_rows(main, prev_row, next_row):
    tr = main.shape[0]
    row = lax.broadcasted_iota(jnp.int32, main.shape, 0)
    up = jnp.where(row == 0, prev_row, pltpu.roll(main, 1, 0))
    dn = jnp.where(row == tr - 1, next_row, pltpu.roll(main, tr - 1, 0))
    return up, dn


def _halo_specs(tr, tn, S, col_of):
    hb = tr // HALO
    last = S // HALO - 1
    return [pl.BlockSpec((tr, tn), lambda j, i: (i, col_of(j))),
            pl.BlockSpec((HALO, tn), lambda j, i: (jnp.maximum(i * hb - 1, 0), col_of(j))),
            pl.BlockSpec((HALO, tn), lambda j, i: (jnp.minimum((i + 1) * hb, last), col_of(j)))]


def _halo_rows(main_ref, prev_ref, next_ref, i, n_i):
    main = main_ref[...].astype(F32)
    prev_row = prev_ref[HALO - 1:HALO, :].astype(F32) * (i > 0).astype(F32)
    next_row = next_ref[0:1, :].astype(F32) * (i < n_i - 1).astype(F32)
    up, dn = _shift_rows(main, prev_row, next_row)
    return up, main, dn


def _conv3(w_ref, b_ref, up, mid, dn):
    return b_ref[...] + w_ref[0:1, :] * up + w_ref[1:2, :] * mid + w_ref[2:3, :] * dn


def _ffn_tiles(S, F, rows=(1024, 512, 256, 128, 16)):
    return _pick(S, rows), _pick(F, (512, 256, 128))


def _sigmoid(x):
    return 0.5 * jnp.tanh(0.5 * x) + 0.5


def _conv_gate_fwd(u, conv_w, conv_b, name):
    S, F2 = u.shape
    F = F2 // 2
    tr, tn = _ffn_tiles(S, F)
    nj, ni = F // tn, S // tr

    def body(gm, gp, gn, vm, vp, vn, wg, wv, bg, bv, o_ref):
        i = pl.program_id(1)
        cg = _conv3(wg, bg, *_halo_rows(gm, gp, gn, i, ni))
        cv = _conv3(wv, bv, *_halo_rows(vm, vp, vn, i, ni))
        o_ref[...] = (cg * _sigmoid(cg) * cv).astype(o_ref.dtype)

    wspec = lambda off: pl.BlockSpec((3, tn), lambda j, i: (0, j + off))
    bspec = lambda off: pl.BlockSpec((1, tn), lambda j, i: (0, j + off))
    return pl.pallas_call(
        body, name=name, grid=(nj, ni),
        in_specs=[*_halo_specs(tr, tn, S, lambda j: j), *_halo_specs(tr, tn, S, lambda j: j + nj),
                  wspec(0), wspec(nj), bspec(0), bspec(nj)],
        out_specs=pl.BlockSpec((tr, tn), lambda j, i: (i, j)),
        out_shape=jax.ShapeDtypeStruct((S, F), BF16), compiler_params=_cp(("parallel", "parallel")),
    )(u, u, u, u, u, u, conv_w, conv_w, conv_b, conv_b)


def _conv_gate_bwd(u, da, conv_w, conv_b, name):
    S, F2 = u.shape
    F = F2 // 2
    tr, tn = _ffn_tiles(S, F)
    nj, ni = F // tn, S // tr

    def body(gm, gp, gn, vm, vp, vn, wg, wv, bg, bv, da_ref, dc_ref, dwg_ref, dwv_ref, dbg_ref, dbv_ref):
        i = pl.program_id(1)

        @pl.when(i == 0)
        def _():
            for r in (dwg_ref, dwv_ref, dbg_ref, dbv_ref):
                r[...] = jnp.zeros_like(r)

        g_rows = _halo_rows(gm, gp, gn, i, ni)
        v_rows = _halo_rows(vm, vp, vn, i, ni)
        cg = _conv3(wg, bg, *g_rows)
        cv = _conv3(wv, bv, *v_rows)
        sg = _sigmoid(cg)
        d = da_ref[...].astype(F32)
        dcv = d * (cg * sg)
        dcg = d * cv * (sg * (1.0 + cg * (1.0 - sg)))
        dc_ref[0] = dcg.astype(dc_ref.dtype)
        dc_ref[1] = dcv.astype(dc_ref.dtype)
        for t in range(3):
            dwg_ref[t:t + 1, :] += jnp.sum(dcg * g_rows[t], axis=0, keepdims=True)
            dwv_ref[t:t + 1, :] += jnp.sum(dcv * v_rows[t], axis=0, keepdims=True)
        dbg_ref[...] += jnp.sum(dcg, axis=0, keepdims=True)
        dbv_ref[...] += jnp.sum(dcv, axis=0, keepdims=True)

    wspec = lambda off: pl.BlockSpec((3, tn), lambda j, i: (0, j + off))
    bspec = lambda off: pl.BlockSpec((1, tn), lambda j, i: (0, j + off))
    tile = pl.BlockSpec((tr, tn), lambda j, i: (i, j))
    outs = pl.pallas_call(
        body, name=name, grid=(nj, ni),
        in_specs=[*_halo_specs(tr, tn, S, lambda j: j), *_halo_specs(tr, tn, S, lambda j: j + nj),
                  wspec(0), wspec(nj), bspec(0), bspec(nj), tile],
        out_specs=[pl.BlockSpec((2, tr, tn), lambda j, i: (0, i, j)), wspec(0), wspec(0), bspec(0), bspec(0)],
        out_shape=[jax.ShapeDtypeStruct((2, S, F), BF16),
                   jax.ShapeDtypeStruct((3, F), F32), jax.ShapeDtypeStruct((3, F), F32),
                   jax.ShapeDtypeStruct((1, F), F32), jax.ShapeDtypeStruct((1, F), F32)],
        compiler_params=_cp(("parallel", "arbitrary")),
    )(u, u, u, u, u, u, conv_w, conv_w, conv_b, conv_b, da)
    dc, dwg, dwv, dbg, dbv = outs
    return dc, jnp.concatenate([dwg, dwv], axis=1), jnp.concatenate([dbg, dbv], axis=1)


def _conv_transpose(dc, w, name):
    _, S, F = dc.shape
    tr, tn = _ffn_tiles(S, F, rows=(2048, 1024, 512, 256, 128, 16))
    nj, ni = F // tn, S // tr
    hb = tr // HALO
    last = S // HALO - 1

    def body(m, p, n, w_ref, o_ref):
        up, mid, dn = _halo_rows(m, p, n, pl.program_id(2), ni)
        o_ref[...] = (w_ref[0:1, :] * dn + w_ref[1:2, :] * mid + w_ref[2:3, :] * up).astype(o_ref.dtype)

    tile = pl.BlockSpec((None, tr, tn), lambda h, j, i: (h, i, j))
    return pl.pallas_call(
        body, name=name, grid=(2, nj, ni),
        in_specs=[tile,
                  pl.BlockSpec((None, HALO, tn), lambda h, j, i: (h, jnp.maximum(i * hb - 1, 0), j)),
                  pl.BlockSpec((None, HALO, tn), lambda h, j, i: (h, jnp.minimum((i + 1) * hb, last), j)),
                  pl.BlockSpec((3, tn), lambda h, j, i: (0, j + h * nj))],
        out_specs=tile, out_shape=jax.ShapeDtypeStruct((2, S, F), BF16),
        compiler_params=_cp(("parallel", "parallel", "parallel")),
    )(dc, dc, dc, w)


def _ffn_fwd(x, norm_g, w_up, conv_w, conv_b, w_down, tag, next_blocks=None):
    hn = _rmsnorm_fwd(x, norm_g, BF16, f"ffn_norm_{tag}")
    if next_blocks is None:
        u = _mm(hn, w_up, out_dtype=BF16, name=f"ffn_up_{tag}")
    else:
        u, over_ici = _mm(hn, w_up, out_dtype=BF16, name=f"ffn_up_{tag}", hosted=_gather_halves_ici(next_blocks))
    a = _conv_gate_fwd(u, conv_w, conv_b, f"ffn_gate_{tag}")
    if next_blocks is None:
        x_new, landed = _mm(a, w_down, name=f"ffn_down_{tag}", epi=_add_epi, epi_in=(x,)), None
    else:
        x_new, over_d2d = _mm(a, w_down, name=f"ffn_down_{tag}", epi=_add_epi, epi_in=(x,),
                              hosted=_gather_halves_d2d(over_ici))
        landed = (over_ici, over_d2d)
    return x_new, (x, hn, u, a), landed


def _ffn_bwd(dres, saved, norm_g, w_up, conv_w, conv_b, w_down, tag, reducer=None):
    x, hn, u, a = saved
    dres, dres_b = dres
    if reducer is None:
        da = _mm(dres_b, w_down, tb=True, out_dtype=BF16, name=f"ffn_da_{tag}")
    else:
        da, landed = _mm(dres_b, w_down, tb=True, out_dtype=BF16, name=f"ffn_da_{tag}", hosted=reducer.sibling())
    dw_down = _mm(a, dres_b, ta=True, name=f"ffn_dwdown_{tag}", pieces='row')
    dc, dconv_w, dconv_b = _conv_gate_bwd(u, da, conv_w, conv_b, f"ffn_gate_bwd_{tag}")
    du = _conv_transpose(dc, conv_w, f"ffn_convt_{tag}")
    if reducer is None:
        dhn = _mm(du, w_up, tb=True, name=f"ffn_dhn_{tag}")
        dw_up = _mm(hn, du, ta=True, name=f"ffn_dwup_{tag}", pieces='col')
        reduced = None
    else:
        dhn, landed = _mm(du, w_up, tb=True, name=f"ffn_dhn_{tag}", hosted=reducer.chips(landed))
        dw_up, landed = _mm(hn, du, ta=True, name=f"ffn_dwup_{tag}", pieces='col', hosted=reducer.swap(landed))
        reduced = reducer.finish(landed)
    dres, dg = _rmsnorm_bwd(x, norm_g, dhn, dres, f"ffn_norm_bwd_{tag}")
    return dres, (dg, dw_up, dconv_w, dconv_b, dw_down), reduced


ATTN_KEYS = 3 * ATTN_BLOCK


def _attn_window(i, S, reps):
    ks = pl.multiple_of(jnp.clip((i - 1) * ATTN_BLOCK, 0, S - ATTN_KEYS), ATTN_BLOCK)
    shape = (reps * ATTN_BLOCK, ATTN_KEYS)
    qpos = i * ATTN_BLOCK + lax.rem(lax.broadcasted_iota(jnp.int32, shape, 0), ATTN_BLOCK)
    kpos = ks + lax.broadcasted_iota(jnp.int32, shape, 1)
    arel = jnp.abs(kpos - qpos)
    return ks, arel.astype(F32), arel <= ATTN_BLOCK


ATTN_PAIRS = ATTN_GROUP // 2


def _stack_pairs(ref, kvh):
    c0 = kvh * ATTN_PAIRS * LANES
    return jnp.concatenate([ref[:, c0 + t * LANES:c0 + (t + 1) * LANES] for t in range(ATTN_PAIRS)], axis=0)


def _even_odd_operands(ref, rows, kvh):
    slab = ref[rows, (kvh // 2) * LANES:(kvh // 2 + 1) * LANES].astype(F32)
    other = pltpu.roll(slab, ATTN_DH, 1)
    low = lax.broadcasted_iota(jnp.int32, slab.shape, 1) < ATTN_DH
    lo_src, hi_src = (slab, other) if kvh % 2 == 0 else (other, slab)
    return jnp.where(low, lo_src, 0.0).astype(BF16), jnp.where(low, 0.0, hi_src).astype(BF16)


def _per_pair_rows(values):
    blk = lax.broadcasted_iota(jnp.int32, (ATTN_PAIRS * ATTN_BLOCK, 1), 0) // ATTN_BLOCK
    col = jnp.full(blk.shape, values[0], F32)
    for t in range(1, ATTN_PAIRS):
        col = jnp.where(blk == t, values[t], col)
    return col


def _attn_heads(kvh, parity, H):
    heads = [kvh * ATTN_GROUP + 2 * t + parity for t in range(ATTN_PAIRS)]
    return heads, [2.0 ** (-8.0 * (h + 1) / H) for h in heads]


def _attn_probs(q, k, slope, sink, arel, valid):
    s = lax.dot_general(q, k, (((1,), (1,)), ((), ())), preferred_element_type=F32) * (ATTN_DH ** -0.5)
    s = jnp.where(valid, s - slope * arel, -jnp.inf)
    m = jnp.maximum(jnp.max(s, axis=-1, keepdims=True), sink)
    p = jnp.exp(s - m)
    es = jnp.exp(sink - m)
    inv = 1.0 / (jnp.sum(p, axis=-1, keepdims=True) + es)
    return p * inv, es * inv


def _attn_specs(S, D):
    H = D // ATTN_DH
    KVW = (H // ATTN_GROUP) * ATTN_DH
    q_spec = pl.BlockSpec((ATTN_BLOCK, D), lambda i: (i, 0))
    k_spec = pl.BlockSpec((S, KVW), lambda i: (0, D // KVW))
    v_spec = pl.BlockSpec((S, KVW), lambda i: (0, D // KVW + 1))
    return H, KVW, q_spec, k_spec, v_spec


def _attn_fwd(qkv, sink, name):
    S = qkv.shape[0]
    D = qkv.shape[1] * ATTN_GROUP // (ATTN_GROUP + 2)
    H, KVW, q_spec, k_spec, v_spec = _attn_specs(S, D)

    def body(q_ref, k_ref, v_ref, sink_ref, o_ref):
        ks, arel, valid = _attn_window(pl.program_id(0), S, ATTN_PAIRS)
        rows = pl.ds(ks, ATTN_KEYS)
        for kvh in range(H // ATTN_GROUP):
            q = _stack_pairs(q_ref, kvh)
            out = None
            for parity, k, v in zip((0, 1), _even_odd_operands(k_ref, rows, kvh), _even_odd_operands(v_ref, rows, kvh)):
                heads, slopes = _attn_heads(kvh, parity, H)
                p, _ = _attn_probs(q, k, _per_pair_rows(slopes), _per_pair_rows([sink_ref[h] for h in heads]),
                                   arel, valid)
                part = jnp.dot(p.astype(BF16), v, preferred_element_type=F32)
                out = part if out is None else out + part
            c0 = kvh * ATTN_PAIRS * LANES
            for t in range(ATTN_PAIRS):
                o_ref[:, c0 + t * LANES:c0 + (t + 1) * LANES] = out[t * ATTN_BLOCK:(t + 1) * ATTN_BLOCK].astype(o_ref.dtype)

    return pl.pallas_call(
        body, name=name, grid=(S // ATTN_BLOCK,),
        in_specs=[q_spec, k_spec, v_spec, pl.BlockSpec(memory_space=pltpu.SMEM)],
        out_specs=q_spec, out_shape=jax.ShapeDtypeStruct((S, D), BF16),
        compiler_params=_cp(("parallel",)),
    )(qkv, qkv, qkv, sink)


def _attn_bwd(qkv, sink, do, name):
    S = qkv.shape[0]
    D = qkv.shape[1] * ATTN_GROUP // (ATTN_GROUP + 2)
    H, KVW, q_spec, k_spec, v_spec = _attn_specs(S, D)
    scale = ATTN_DH ** -0.5
    nt = (((1,), (1,)), ((), ()))
    tn = (((0,), (0,)), ((), ()))

    def body(q_ref, k_ref, v_ref, sink_ref, do_ref, dq_ref, dk_ref, dv_ref, ds_ref):
        @pl.when(pl.program_id(0) == 0)
        def _():
            dk_ref[...] = jnp.zeros_like(dk_ref)
            dv_ref[...] = jnp.zeros_like(dv_ref)
            ds_ref[...] = jnp.zeros_like(ds_ref)

        ks, arel, valid = _attn_window(pl.program_id(0), S, ATTN_PAIRS)
        rows = pl.ds(ks, ATTN_KEYS)
        low = lax.broadcasted_iota(jnp.int32, (ATTN_KEYS, LANES), 1) < ATTN_DH
        for kvh in range(H // ATTN_GROUP):
            q = _stack_pairs(q_ref, kvh)
            d_o = _stack_pairs(do_ref, kvh)
            dq = None
            dk_halves, dv_halves = [], []
            for parity, k, v in zip((0, 1), _even_odd_operands(k_ref, rows, kvh), _even_odd_operands(v_ref, rows, kvh)):
                heads, slopes = _attn_heads(kvh, parity, H)
                p, p_sink = _attn_probs(q, k, _per_pair_rows(slopes), _per_pair_rows([sink_ref[h] for h in heads]),
                                        arel, valid)
                dp = lax.dot_general(d_o, v, nt, preferred_element_type=F32)
                delta = jnp.sum(p * dp, axis=-1, keepdims=True)
                dsc = (p * (dp - delta)).astype(BF16)
                dsink = -p_sink * delta
                for t, h in enumerate(heads):
                    ds_ref[:, h:h + 1] += dsink[t * ATTN_BLOCK:(t + 1) * ATTN_BLOCK]
                part = jnp.dot(dsc, k, preferred_element_type=F32)
                dq = part if dq is None else dq + part
                dk_halves.append(lax.dot_general(dsc, q, tn, preferred_element_type=F32))
                dv_halves.append(lax.dot_general(p.astype(BF16), d_o, tn, preferred_element_type=F32))
            c0 = kvh * ATTN_PAIRS * LANES
            for t in range(ATTN_PAIRS):
                dq_ref[:, c0 + t * LANES:c0 + (t + 1) * LANES] = (
                    dq[t * ATTN_BLOCK:(t + 1) * ATTN_BLOCK] * scale).astype(dq_ref.dtype)
            slab = slice((kvh // 2) * LANES, (kvh // 2 + 1) * LANES)
            mine = low if kvh % 2 == 0 else jnp.logical_not(low)
            for ref, (even, odd), mult in ((dk_ref, dk_halves, scale), (dv_ref, dv_halves, 1.0)):
                both = jnp.where(low, even, odd)
                total = both + pltpu.roll(both, ATTN_DH, 1)
                ref[rows, slab] += jnp.where(mine, total * mult, 0.0)

    kv_out = pl.BlockSpec((S, KVW), lambda i: (0, 0))
    return pl.pallas_call(
        body, name=name, grid=(S // ATTN_BLOCK,),
        in_specs=[q_spec, k_spec, v_spec, pl.BlockSpec(memory_space=pltpu.SMEM), q_spec],
        out_specs=[q_spec, kv_out, kv_out, pl.BlockSpec((ATTN_BLOCK, H), lambda i: (0, 0))],
        out_shape=[jax.ShapeDtypeStruct((S, D), BF16), jax.ShapeDtypeStruct((S, KVW), F32),
                   jax.ShapeDtypeStruct((S, KVW), F32), jax.ShapeDtypeStruct((ATTN_BLOCK, H), F32)],
        compiler_params=_cp(("arbitrary",)),
    )(qkv, qkv, qkv, sink, do)


def _mm_hosting(exchange, a, b, **kw):
    if exchange is None:
        return _mm(a, b, **kw), None
    return _mm(a, b, hosted=exchange, **kw)


def _attn_layer_fwd(x, norm_g, w_qkv, w_o, sink, tag, next_blocks=None):
    hn = _rmsnorm_fwd(x, norm_g, BF16, f"attn_norm_{tag}")
    qkv, over_ici = _mm_hosting(None if next_blocks is None else _gather_halves_ici(next_blocks), hn, w_qkv,
                                out_dtype=BF16, name=f"attn_qkv_{tag}")
    o = _attn_fwd(qkv, sink, f"attn_core_{tag}")
    x_new, over_d2d = _mm_hosting(None if next_blocks is None else _gather_halves_d2d(over_ici), o, w_o,
                                  name=f"attn_out_{tag}", epi=_add_epi, epi_in=(x,))
    return x_new, (x, hn, qkv, o), None if next_blocks is None else (over_ici, over_d2d)


def _attn_layer_bwd(dres, saved, norm_g, w_qkv, w_o, sink, tag, reducer=None):
    x, hn, qkv, o = saved
    dres, dres_b = dres
    do, landed = _mm_hosting(reducer and reducer.sibling(), dres_b, w_o, tb=True, out_dtype=BF16, name=f"attn_do_{tag}")
    dw_o = _mm(o, dres_b, ta=True, name=f"attn_dwo_{tag}", pieces='row')
    dq, dk, dv, dsink = _attn_bwd(qkv, sink, do, f"attn_core_bwd_{tag}")
    dqkv = jnp.concatenate([dq, dk.astype(BF16), dv.astype(BF16)], axis=1)
    dhn, landed = _mm_hosting(reducer and reducer.chips(landed), dqkv, w_qkv, tb=True, name=f"attn_dhn_{tag}")
    dw_qkv, landed = _mm_hosting(reducer and reducer.swap(landed), hn, dqkv, ta=True, name=f"attn_dwqkv_{tag}",
                                 pieces='col')
    dres, dg = _rmsnorm_bwd(x, norm_g, dhn, dres, f"attn_norm_bwd_{tag}")
    return dres, (dg, dw_qkv, dw_o, jnp.sum(dsink, axis=0)), reducer and reducer.finish(landed)


MLA_W = 2 * LANES
MLA_DPAD = 2 * MLA_LORA + LANES
MLA_SCALE = (MLA_NOPE + MLA_ROPE) ** -0.5
MLA_TILES = (1024, 512, 256, 128)
MLA_ROW_GROUP = 256
LOG2E = math.log2(math.e)
LN2 = math.log(2.0)


def _rope_tables(S):
    half = MLA_ROPE // 2
    pos = jnp.arange(S, dtype=F32)
    inv = ROPE_THETA ** (-jnp.arange(half, dtype=F32) / half)
    ang = pos[:, None] * inv[None, :]
    cos, sin = jnp.cos(ang), jnp.sin(ang)
    z = jnp.zeros((S, LANES - 2 * half), F32)
    zh = jnp.zeros((S, half), F32)
    return (jnp.concatenate([cos, cos, z], axis=1), jnp.concatenate([-sin, zh, z], axis=1),
            jnp.concatenate([zh, sin, z], axis=1))


def _rope(t, ca, sb, sc):
    return t * ca + pltpu.roll(t, 96, 1) * sb + pltpu.roll(t, 32, 1) * sc


def _rope_t(d, ca, sb, sc):
    return d * ca + pltpu.roll(d * sb, 32, 1) + pltpu.roll(d * sc, 96, 1)


def _rms(xf, g):
    return (xf * lax.rsqrt(jnp.mean(xf * xf, axis=-1, keepdims=True) + RMS_EPS)) * g


def _mla_prep(d, qn, kvn, tabs, name):
    S = d.shape[0]
    tr = _pick(S, (256, 128, 8))
    L = MLA_LORA

    def body(d_ref, qn_ref, kvn_ref, ca, sb, sc, cq_ref, ckv_ref, kr_ref):
        cq_ref[...] = _rms(d_ref[:, :L], qn_ref[...]).astype(BF16)
        ckv_ref[...] = _rms(d_ref[:, L:2 * L], kvn_ref[...]).astype(BF16)
        kr_ref[...] = _rope(d_ref[:, 2 * L:], ca[...], sb[...], sc[...]).astype(BF16)

    row = lambda w: pl.BlockSpec((tr, w), lambda i: (i, 0))
    vec = pl.BlockSpec((1, L), lambda i: (0, 0))
    return pl.pallas_call(
        body, name=name, grid=(S // tr,),
        in_specs=[row(MLA_DPAD), vec, vec, row(LANES), row(LANES), row(LANES)],
        out_specs=[row(L), row(L), row(LANES)],
        out_shape=[jax.ShapeDtypeStruct((S, L), BF16), jax.ShapeDtypeStruct((S, L), BF16),
                   jax.ShapeDtypeStruct((S, LANES), BF16)],
        compiler_params=_cp(("parallel",)),
    )(d, qn, kvn, *tabs)


def _mla_prep_bwd(d, qn, kvn, tabs, dcq, dckv, dkr_h, name):
    S = d.shape[0]
    H = dkr_h.shape[0]
    tr = _pick(S, (256, 128, 8))
    L = MLA_LORA

    def body(d_ref, qn_ref, kvn_ref, ca, sb, sc, dcq_ref, dckv_ref, dkr_ref, dd_ref, dqn_ref, dkvn_ref):
        @pl.when(pl.program_id(0) == 0)
        def _():
            dqn_ref[...] = jnp.zeros_like(dqn_ref)
            dkvn_ref[...] = jnp.zeros_like(dkvn_ref)

        dx, dg = _rms_bwd_math(d_ref[:, :L], qn_ref[...], dcq_ref[...])
        dd_ref[:, :L] = dx.astype(BF16)
        dqn_ref[...] += dg
        dx, dg = _rms_bwd_math(d_ref[:, L:2 * L], kvn_ref[...], dckv_ref[...])
        dd_ref[:, L:2 * L] = dx.astype(BF16)
        dkvn_ref[...] += dg
        dkr = dkr_ref[0]
        for h in range(1, H):
            dkr = dkr + dkr_ref[h]
        dd_ref[:, 2 * L:] = _rope_t(dkr, ca[...], sb[...], sc[...]).astype(BF16)

    row = lambda w: pl.BlockSpec((tr, w), lambda i: (i, 0))
    vec = pl.BlockSpec((1, L), lambda i: (0, 0))
    return pl.pallas_call(
        body, name=name, grid=(S // tr,),
        in_specs=[row(MLA_DPAD), vec, vec, row(LANES), row(LANES), row(LANES), row(L), row(L),
                  pl.BlockSpec((H, tr, LANES), lambda i: (0, i, 0))],
        out_specs=[row(MLA_DPAD), vec, vec],
        out_shape=[jax.ShapeDtypeStruct((S, MLA_DPAD), BF16), jax.ShapeDtypeStruct((1, L), F32),
                   jax.ShapeDtypeStruct((1, L), F32)],
        compiler_params=_cp(("arbitrary",)),
    )(d, qn, kvn, *tabs, dcq, dckv, dkr_h)


def _heads_proj(a, w, out_dtype, name):
    S, K = a.shape
    H, _, n = w.shape
    tm = _pick(S, (1024, 512, 256, 128))
    return _gmm(a, w, grid=(S // tm, H, 1),
                a_spec=pl.BlockSpec((tm, K), lambda m, h, k: (m, 0)),
                b_spec=pl.BlockSpec((None, K, n), lambda m, h, k: (h, 0, 0)),
                o_spec=pl.BlockSpec((None, tm, n), lambda m, h, k: (h, m, 0)),
                out_shape=jax.ShapeDtypeStruct((H, S, n), out_dtype),
                dims=(((1,), (0,)), ((), ())), kax=2, acc_shape=(tm, n), name=name)


def _heads_proj_dx(dy, w, name):
    H, S, n = dy.shape
    K = w.shape[1]
    tm = _pick(S, (1024, 512, 256, 128))
    return _gmm(dy, w, grid=(S // tm, 1, H),
                a_spec=pl.BlockSpec((None, tm, n), lambda m, j, h: (h, m, 0)),
                b_spec=pl.BlockSpec((None, K, n), lambda m, j, h: (h, 0, 0)),
                o_spec=pl.BlockSpec((tm, K), lambda m, j, h: (m, 0)),
                out_shape=jax.ShapeDtypeStruct((S, K), F32),
                dims=(((1,), (1,)), ((), ())), kax=2, acc_shape=(tm, K), name=name)


def _heads_proj_dw(a, dy, name):
    S, K = a.shape
    H, _, n = dy.shape
    tk = _pick(S, (512, 256, 128))
    return _gmm(a, dy, grid=(H, 1, S // tk),
                a_spec=pl.BlockSpec((tk, K), lambda h, j, k: (k, 0)),
                b_spec=pl.BlockSpec((None, tk, n), lambda h, j, k: (h, k, 0)),
                o_spec=pl.BlockSpec((None, K, n), lambda h, j, k: (h, 0, 0)),
                out_shape=jax.ShapeDtypeStruct((H, K, n), F32),
                dims=(((0,), (0,)), ((), ())), kax=2, acc_shape=(K, n), name=name)


def _mla_rope_q(q_ext, tabs, bwd, name):
    H, S, _ = q_ext.shape
    tr = _pick(S, (512, 256, 128, 8))
    mult = 1.0 if bwd else MLA_SCALE * LOG2E

    def body(q_ref, ca, sb, sc, o_ref):
        o_ref[:, :LANES] = (q_ref[:, :LANES].astype(F32) * mult).astype(BF16)
        fn = _rope_t if bwd else _rope
        o_ref[:, LANES:] = (fn(q_ref[:, LANES:].astype(F32), ca[...], sb[...], sc[...]) * mult).astype(BF16)

    blk = pl.BlockSpec((None, tr, MLA_W), lambda i, h: (h, i, 0))
    tab = pl.BlockSpec((tr, LANES), lambda i, h: (i, 0))
    return pl.pallas_call(
        body, name=name, grid=(S // tr, H), in_specs=[blk, tab, tab, tab], out_specs=blk,
        out_shape=jax.ShapeDtypeStruct((H, S, MLA_W), BF16), compiler_params=_cp(("parallel", "parallel")),
    )(q_ext, *tabs)


def _col_to_row(col):
    n = col.shape[0]
    eye = lax.broadcasted_iota(jnp.int32, (n, n), 0) == lax.broadcasted_iota(jnp.int32, (n, n), 1)
    return jnp.sum(jnp.where(eye, col, 0.0), axis=0, keepdims=True)


def _mla_flash_fwd(q, kv, kr, name, tq=None, tk=None, unroll=1, splits=None):
    H, S, _ = q.shape
    tq = tq or _pick(S, (2048,) + MLA_TILES)
    tk = tk or _pick(S, (2048,) + MLA_TILES)
    splits = splits or max(1, tq // MLA_ROW_GROUP)

    def body(q_ref, kv_ref, kr_ref, o_ref, lse_ref, kbuf, vbuf):
        @pl.when(pl.program_id(1) == 0)
        def _():
            kbuf[:, :LANES] = kv_ref[:, :LANES]
            kbuf[:, LANES:] = kr_ref[...]
            vbuf[:, :LANES] = kv_ref[:, LANES:]
            vbuf[:, LANES:] = jnp.ones((S, LANES), BF16)

        sub = tq // splits
        qs = [q_ref[g * sub:(g + 1) * sub, :] for g in range(splits)]

        def step(c, carry):
            rows = pl.ds(pl.multiple_of(c * tk, tk), tk)
            k, v = kbuf[rows, :], vbuf[rows, :]
            out = []
            for qv, (m, acc) in zip(qs, carry):
                s = lax.dot_general(qv, k, (((1,), (1,)), ((), ())), preferred_element_type=F32)
                m_new = jnp.maximum(m, jnp.max(s, axis=-1, keepdims=True))
                p = jnp.exp2(s - m_new).astype(BF16)
                out.append((m_new, jnp.exp2(m - m_new) * acc + jnp.dot(p, v, preferred_element_type=F32)))
            return tuple(out)

        init = tuple((jnp.full((sub, 1), -jnp.inf, F32), jnp.zeros((sub, MLA_W), F32)) for _ in range(splits))
        for g, (m, acc) in enumerate(lax.fori_loop(0, S // tk, step, init, unroll=unroll)):
            l = acc[:, LANES:LANES + 1]
            o_ref[g * sub:(g + 1) * sub, :] = (acc[:, :LANES] / l).astype(o_ref.dtype)
            lse_ref[:, g * sub:(g + 1) * sub] = _col_to_row(m + jnp.log2(l))

    return pl.pallas_call(
        body, name=name, grid=(H, S // tq),
        in_specs=[pl.BlockSpec((None, tq, MLA_W), lambda h, i: (h, i, 0)),
                  pl.BlockSpec((None, S, MLA_W), lambda h, i: (h, 0, 0)),
                  pl.BlockSpec((S, LANES), lambda h, i: (0, 0))],
        out_specs=[pl.BlockSpec((tq, MLA_V), lambda h, i: (i, h)),
                   pl.BlockSpec((None, 1, tq), lambda h, i: (h, 0, i))],
        out_shape=[jax.ShapeDtypeStruct((S, H * MLA_V), BF16), jax.ShapeDtypeStruct((H, 1, S), F32)],
        scratch_shapes=[pltpu.VMEM((S, MLA_W), BF16), pltpu.VMEM((S, MLA_W), BF16)],
        compiler_params=_cp(("parallel", "arbitrary")),
    )(q, kv, kr)


def _mla_delta(o, do, H, name):
    S = o.shape[0]
    tq = _pick(S, (512, 256, 128))

    def body(o_ref, do_ref, d_ref):
        prod = o_ref[...].astype(F32) * do_ref[...].astype(F32)
        d_ref[...] = _col_to_row(jnp.sum(prod, axis=-1, keepdims=True))

    blk = pl.BlockSpec((tq, MLA_V), lambda i, h: (i, h))
    return pl.pallas_call(
        body, name=name, grid=(S // tq, H), in_specs=[blk, blk],
        out_specs=pl.BlockSpec((None, 1, tq), lambda i, h: (h, 0, i)),
        out_shape=jax.ShapeDtypeStruct((H, 1, S), F32), compiler_params=_cp(("parallel", "parallel")),
    )(o, do)


def _mla_flash_bwd(q, kv, kr, do, lse, delta, name, tq=None, tkv=None, unroll=1):
    H, S, _ = q.shape
    tq = tq or _pick(S, (2048,) + MLA_TILES)
    tkv = tkv or _pick(S, MLA_TILES)

    def body(q_ref, kv_ref, kr_ref, do_ref, lse_ref, dl_ref, dq_ref, dkv_ref, dkr_ref):
        @pl.when(pl.program_id(1) == 0)
        def _():
            dq_ref[...] = jnp.zeros_like(dq_ref)

        v = kv_ref[:, LANES:]
        k = jnp.concatenate([kv_ref[:, :LANES], kr_ref[...]], axis=1)

        def step(c, carry):
            dk, dv = carry
            start = pl.multiple_of(c * tq, tq)
            rows = pl.ds(start, tq)
            qv = q_ref[rows, :]
            d_o = do_ref[rows, :]
            s_t = lax.dot_general(k, qv, (((1,), (1,)), ((), ())), preferred_element_type=F32)
            p_t = jnp.exp2(s_t - lse_ref[:, rows])
            dv = dv + jnp.dot(p_t.astype(BF16), d_o, preferred_element_type=F32)
            dp_t = lax.dot_general(v, d_o, (((1,), (1,)), ((), ())), preferred_element_type=F32)
            ds_t = (p_t * (dp_t - dl_ref[:, rows])).astype(BF16)
            dk = dk + jnp.dot(ds_t, qv, preferred_element_type=F32)
            dq_ref[rows, :] += lax.dot_general(ds_t, k, (((0,), (0,)), ((), ())),
                                               preferred_element_type=F32) * MLA_SCALE
            return dk, dv

        dk, dv = lax.fori_loop(0, S // tq, step, (jnp.zeros((tkv, MLA_W), F32), jnp.zeros((tkv, MLA_V), F32)),
                               unroll=unroll)
        dkv_ref[:, :LANES] = (dk[:, :LANES] * LN2).astype(BF16)
        dkv_ref[:, LANES:] = dv.astype(BF16)
        dkr_ref[...] = dk[:, LANES:] * LN2

    stat = pl.BlockSpec((None, 1, S), lambda h, j: (h, 0, 0))
    return pl.pallas_call(
        body, name=name, grid=(H, S // tkv),
        in_specs=[pl.BlockSpec((None, S, MLA_W), lambda h, j: (h, 0, 0)),
                  pl.BlockSpec((None, tkv, MLA_W), lambda h, j: (h, j, 0)),
                  pl.BlockSpec((tkv, LANES), lambda h, j: (j, 0)),
                  pl.BlockSpec((S, MLA_V), lambda h, j: (0, h)), stat, stat],
        out_specs=[pl.BlockSpec((None, S, MLA_W), lambda h, j: (h, 0, 0)),
                   pl.BlockSpec((None, tkv, MLA_W), lambda h, j: (h, j, 0)),
                   pl.BlockSpec((None, tkv, LANES), lambda h, j: (h, j, 0))],
        out_shape=[jax.ShapeDtypeStruct((H, S, MLA_W), F32), jax.ShapeDtypeStruct((H, S, MLA_W), BF16),
                   jax.ShapeDtypeStruct((H, S, LANES), F32)],
        compiler_params=_cp(("parallel", "arbitrary")),
    )(q, kv, kr, do, lse, delta)


def _mla_weights(w_dqkv, w_uq, w_ukv):
    H = MLA_HEADS
    wd = jnp.pad(w_dqkv, ((0, 0), (0, MLA_DPAD - w_dqkv.shape[1])))
    wq = w_uq.reshape(MLA_LORA, H, MLA_NOPE + MLA_ROPE)
    wq = jnp.pad(wq, ((0, 0), (0, 0), (0, MLA_W - wq.shape[2]))).transpose(1, 0, 2)
    wkv = w_ukv.reshape(MLA_LORA, H, MLA_NOPE + MLA_V).transpose(1, 0, 2)
    return wd, wq, wkv


def _mla_layer_fwd(x, norm_g, wd, wq, wkv, w_o, qn, kvn, tag):
    S = x.shape[0]
    tabs = _rope_tables(S)
    hn = _rmsnorm_fwd(x, norm_g, BF16, f"mla_norm_{tag}")
    d = _mm(hn, wd, name=f"mla_down_{tag}")
    cq, ckv, kr = _mla_prep(d, qn, kvn, tabs, f"mla_prep_{tag}")
    q = _mla_rope_q(_heads_proj(cq, wq, F32, f"mla_uq_{tag}"), tabs, False, f"mla_ropeq_{tag}")
    kv = _heads_proj(ckv, wkv, BF16, f"mla_ukv_{tag}")
    o, lse = _mla_flash_fwd(q, kv, kr, f"mla_flash_{tag}")
    x_new = _mm(o, w_o, name=f"mla_out_{tag}", epi=_add_epi, epi_in=(x,))
    return x_new, (x, hn, d, cq, ckv, kr, q, kv, o, lse)


def _mla_layer_bwd(dres, saved, norm_g, wd, wq, wkv, w_o, qn, kvn, tag):
    x, hn, d, cq, ckv, kr, q, kv, o, lse = saved
    S = x.shape[0]
    H = MLA_HEADS
    tabs = _rope_tables(S)
    dres, dres_b = dres
    do = _mm(dres_b, w_o, tb=True, out_dtype=BF16, name=f"mla_do_{tag}")
    dw_o = _mm(o, dres_b, ta=True, name=f"mla_dwo_{tag}", pieces='row')
    delta = _mla_delta(o, do, H, f"mla_delta_{tag}")
    dq, dkv, dkr_h = _mla_flash_bwd(q, kv, kr, do, lse, delta, f"mla_flash_bwd_{tag}")
    dq_ext = _mla_rope_q(dq, tabs, True, f"mla_ropeq_bwd_{tag}")
    dwq = _heads_proj_dw(cq, dq_ext, f"mla_dwuq_{tag}")
    dcq = _heads_proj_dx(dq_ext, wq, f"mla_dcq_{tag}")
    dwkv = _heads_proj_dw(ckv, dkv, f"mla_dwukv_{tag}")
    dckv = _heads_proj_dx(dkv, wkv, f"mla_dckv_{tag}")
    dd, dqn, dkvn = _mla_prep_bwd(d, qn, kvn, tabs, dcq, dckv, dkr_h, f"mla_prep_bwd_{tag}")
    dhn = _mm(dd, wd, tb=True, name=f"mla_dhn_{tag}")
    dwd = _mm(hn, dd, ta=True, name=f"mla_dwd_{tag}")
    dres, dg = _rmsnorm_bwd(x, norm_g, dhn, dres, f"mla_norm_bwd_{tag}")
    dw_dqkv = dwd[:, :2 * MLA_LORA + MLA_ROPE]
    dw_uq = dwq.transpose(1, 0, 2)[:, :, :MLA_NOPE + MLA_ROPE].reshape(MLA_LORA, -1)
    dw_ukv = dwkv.transpose(1, 0, 2).reshape(MLA_LORA, -1)
    return dres, (dg, dw_dqkv, dqn, dkvn, dw_uq, dw_ukv, dw_o)


S5_CB = LANES
S5_SB = (S5_CB // SSM_GROUP_CH) * SSM_STATE
S5_ROWS = 1024


def _s5_disc(a_re, a_im, ls, b_re, b_im):
    step = jnp.exp(ls)
    mag = jnp.exp(step * a_re)
    lb_re = mag * jnp.cos(step * a_im)
    lb_im = mag * jnp.sin(step * a_im)
    n_re, n_im = lb_re - 1.0, lb_im
    den = a_re * a_re + a_im * a_im
    coef_re = (n_re * a_re + n_im * a_im) / den
    coef_im = (n_im * a_re - n_re * a_im) / den
    return lb_re, lb_im, coef_re * b_re - coef_im * b_im, coef_re * b_im + coef_im * b_re


def _s5_disc_fwd(a_re, a_im, ls, b_re, b_im, name):
    GN = a_re.shape[-1]

    def body(ar, ai, l, br, bi, o_lr, o_li, o_br, o_bi):
        for o, v in zip((o_lr, o_li, o_br, o_bi), _s5_disc(ar[...], ai[...], l[...], br[...], bi[...])):
            o[...] = v

    vec = pl.BlockSpec((None, 1, GN), lambda d: (d, 0, 0))
    mat = pl.BlockSpec((None, SSM_GROUP_CH, GN), lambda d: (d, 0, 0))
    sv = jax.ShapeDtypeStruct(a_re.shape, F32)
    sm = jax.ShapeDtypeStruct(b_re.shape, F32)
    return pl.pallas_call(body, name=name, grid=(2,), in_specs=[vec, vec, vec, mat, mat],
                          out_specs=[vec, vec, mat, mat], out_shape=[sv, sv, sm, sm],
                          compiler_params=_cp(("parallel",)))(a_re, a_im, ls, b_re, b_im)


def _s5_disc_bwd(a_re, a_im, ls, b_re, b_im, d_lr, d_li, d_br, d_bi, name):
    GN = a_re.shape[-1]

    def body(ar, ai, l, br, bi, g_lr, g_li, g_br, g_bi, o_ar, o_ai, o_l, o_br, o_bi):
        _, vjp = jax.vjp(_s5_disc, ar[...], ai[...], l[...], br[...], bi[...])
        for o, v in zip((o_ar, o_ai, o_l, o_br, o_bi), vjp((g_lr[...], g_li[...], g_br[...], g_bi[...]))):
            o[...] = v

    vec = pl.BlockSpec((None, 1, GN), lambda d: (d, 0, 0))
    mat = pl.BlockSpec((None, SSM_GROUP_CH, GN), lambda d: (d, 0, 0))
    sv = jax.ShapeDtypeStruct(a_re.shape, F32)
    sm = jax.ShapeDtypeStruct(b_re.shape, F32)
    return pl.pallas_call(body, name=name, grid=(2,), in_specs=[vec, vec, vec, mat, mat, vec, vec, mat, mat],
                          out_specs=[vec, vec, vec, mat, mat], out_shape=[sv, sv, sv, sm, sm],
                          compiler_params=_cp(("parallel",)))(a_re, a_im, ls, b_re, b_im, d_lr, d_li, d_br, d_bi)


def _cmul(ar, ai, br, bi):
    return ar * br - ai * bi, ar * bi + ai * br


def _segment_carries(lr, li, er, ei, n_steps, reverse):
    pr, pi = lr, li
    for _ in range(int(math.log2(n_steps))):
        pr, pi = _cmul(pr, pi, pr, pi)
    row = lax.broadcasted_iota(jnp.int32, er.shape, 0)
    edge = (SSM_SEGMENTS - 1) if reverse else 0
    shift = (SSM_SEGMENTS - 1) if reverse else 1
    cr = jnp.zeros_like(er)
    ci = jnp.zeros_like(ei)
    for _ in range(SSM_SEGMENTS - 1):
        tr_, ti_ = _cmul(pr, pi, cr, ci)
        cr = jnp.where(row == edge, 0.0, pltpu.roll(tr_ + er, shift, 0))
        ci = jnp.where(row == edge, 0.0, pltpu.roll(ti_ + ei, shift, 0))
    return cr, ci


def _s5_geometry(S, D):
    assert S % SSM_SEGMENTS == 0 and D % S5_CB == 0
    n_steps = S // SSM_SEGMENTS
    assert n_steps & (n_steps - 1) == 0, "segment length must be a power of two"
    rows = min(S5_ROWS, S)
    return n_steps, rows, S // rows, D // S5_CB


def _s5_scan(u, b_re, b_im, c_re, c_im, lam_re, lam_im, ends, descending, name):
    S, D = u.shape
    n_steps, rows, nch, ncb = _s5_geometry(S, D)
    full = ends is not None
    GN = ncb * S5_SB

    def body(*refs):
        if full:
            (u_ref, br_ref, bi_ref, cr_ref, ci_ref, lr_ref, li_ref, er_ref, ei_ref,
             xr_ref, xi_ref, y_ref, st_r, st_i, buf_r, buf_i) = refs
        else:
            u_ref, br_ref, bi_ref, lr_ref, li_ref, er_ref, ei_ref, st_r, st_i, buf_r, buf_i = refs
        lr = jnp.broadcast_to(lr_ref[...], (SSM_SEGMENTS, S5_SB))
        li = jnp.broadcast_to(li_ref[...], (SSM_SEGMENTS, S5_SB))

        @pl.when(pl.program_id(1) == 0)
        def _():
            if full:
                st_r[...], st_i[...] = _segment_carries(lr, li, er_ref[...], ei_ref[...], n_steps, descending)
            else:
                st_r[...] = jnp.zeros_like(st_r)
                st_i[...] = jnp.zeros_like(st_i)

        ub = u_ref[...].astype(BF16)
        buf_r[...] = jnp.dot(ub, br_ref[...], preferred_element_type=F32)
        buf_i[...] = jnp.dot(ub, bi_ref[...], preferred_element_type=F32)

        n_it = rows // SSM_SEGMENTS

        def step(i, carry):
            sr, si = carry
            i = n_it - 1 - i if descending else i
            r = pl.ds(pl.multiple_of(i * SSM_SEGMENTS, SSM_SEGMENTS), SSM_SEGMENTS)
            if full:
                xr_ref[r, :] = sr
                xi_ref[r, :] = si
            nr = lr * sr - li * si + buf_r[r, :]
            ni = lr * si + li * sr + buf_i[r, :]
            if full:
                buf_r[r, :] = nr
                buf_i[r, :] = ni
            return nr, ni

        sr, si = lax.fori_loop(0, n_it, step, (st_r[...], st_i[...]))
        st_r[...] = sr
        st_i[...] = si
        if full:
            y_ref[...] = (jnp.dot(buf_r[...].astype(BF16), cr_ref[...], preferred_element_type=F32)
                          - jnp.dot(buf_i[...].astype(BF16), ci_ref[...], preferred_element_type=F32))
        else:
            er_ref[...] = sr
            ei_ref[...] = si

    chunk = (lambda c: nch - 1 - c) if descending else (lambda c: c)
    u_spec = pl.BlockSpec((rows, S5_CB), lambda b, c: (chunk(c), b))
    bmat = pl.BlockSpec((None, S5_CB, S5_SB), lambda b, c: (b, 0, 0))
    cmat = pl.BlockSpec((None, S5_SB, S5_CB), lambda b, c: (b, 0, 0))
    lvec = pl.BlockSpec((1, S5_SB), lambda b, c: (0, b))
    evec = pl.BlockSpec((SSM_SEGMENTS, S5_SB), lambda b, c: (0, b))
    xblk = pl.BlockSpec((rows, S5_SB), lambda b, c: (chunk(c), b))
    scratch = [pltpu.VMEM((SSM_SEGMENTS, S5_SB), F32)] * 2 + [pltpu.VMEM((rows, S5_SB), F32)] * 2
    e_shape = jax.ShapeDtypeStruct((SSM_SEGMENTS, GN), F32)
    if full:
        x_shape = jax.ShapeDtypeStruct((S, GN), F32)
        return pl.pallas_call(
            body, name=name, grid=(ncb, nch),
            in_specs=[u_spec, bmat, bmat, cmat, cmat, lvec, lvec, evec, evec],
            out_specs=[xblk, xblk, u_spec], out_shape=[x_shape, x_shape, jax.ShapeDtypeStruct((S, D), F32)],
            scratch_shapes=scratch, compiler_params=_cp(("parallel", "arbitrary")),
        )(u, b_re, b_im, c_re, c_im, lam_re, lam_im, *ends)
    return pl.pallas_call(
        body, name=name, grid=(ncb, nch), in_specs=[u_spec, bmat, bmat, lvec, lvec],
        out_specs=[evec, evec], out_shape=[e_shape, e_shape],
        scratch_shapes=scratch, compiler_params=_cp(("parallel", "arbitrary")),
    )(u, b_re, b_im, lam_re, lam_im)


def _s5_scan_bwd(dy, u, xp, b_re, b_im, c_re, c_im, lam_re, lam_im, starts, descending, name):
    S, D = dy.shape
    n_steps, rows, nch, ncb = _s5_geometry(S, D)
    full = starts is not None
    GN = ncb * S5_SB
    nt = (((1,), (1,)), ((), ()))
    tn = (((0,), (0,)), ((), ()))

    def body(*refs):
        if full:
            (dy_ref, u_ref, xr_ref, xi_ref, br_ref, bi_ref, cr_ref, ci_ref, lr_ref, li_ref, gr_ref, gi_ref,
             du_ref, dbr_ref, dbi_ref, dcr_ref, dci_ref, dlr_ref, dli_ref, st_r, st_i, buf_r, buf_i) = refs
        else:
            dy_ref, cr_ref, ci_ref, lr_ref, li_ref, gr_ref, gi_ref, st_r, st_i, buf_r, buf_i = refs
        lr = jnp.broadcast_to(lr_ref[...], (SSM_SEGMENTS, S5_SB))
        li = jnp.broadcast_to(li_ref[...], (SSM_SEGMENTS, S5_SB))

        @pl.when(pl.program_id(1) == 0)
        def _():
            if full:
                st_r[...], st_i[...] = _segment_carries(lr, -li, gr_ref[...], gi_ref[...], n_steps, descending)
                for r in (dbr_ref, dbi_ref, dcr_ref, dci_ref, dlr_ref, dli_ref):
                    r[...] = jnp.zeros_like(r)
            else:
                st_r[...] = jnp.zeros_like(st_r)
                st_i[...] = jnp.zeros_like(st_i)

        dyb = dy_ref[...].astype(BF16)
        buf_r[...] = lax.dot_general(dyb, cr_ref[...], nt, preferred_element_type=F32)
        buf_i[...] = -lax.dot_general(dyb, ci_ref[...], nt, preferred_element_type=F32)
        n_it = rows // SSM_SEGMENTS

        def step(j, carry):
            gr, gi = carry
            j = n_it - 1 - j if descending else j
            r = pl.ds(pl.multiple_of(j * SSM_SEGMENTS, SSM_SEGMENTS), SSM_SEGMENTS)
            nr = lr * gr + li * gi + buf_r[r, :]
            ni = lr * gi - li * gr + buf_i[r, :]
            if full:
                buf_r[r, :] = nr
                buf_i[r, :] = ni
            return nr, ni

        gr, gi = lax.fori_loop(0, n_it, step, (st_r[...], st_i[...]))
        st_r[...] = gr
        st_i[...] = gi
        if not full:
            gr_ref[...] = gr
            gi_ref[...] = gi
            return
        g_r, g_i = buf_r[...], buf_i[...]
        xr, xi = xr_ref[...], xi_ref[...]
        dlr_ref[...] += jnp.sum(g_r * xr + g_i * xi, axis=0, keepdims=True)
        dli_ref[...] += jnp.sum(g_i * xr - g_r * xi, axis=0, keepdims=True)
        ub = u_ref[...].astype(BF16)
        gb_r, gb_i = g_r.astype(BF16), g_i.astype(BF16)
        du_ref[...] = (lax.dot_general(gb_r, br_ref[...], nt, preferred_element_type=F32)
                       + lax.dot_general(gb_i, bi_ref[...], nt, preferred_element_type=F32))
        dbr_ref[...] += lax.dot_general(ub, gb_r, tn, preferred_element_type=F32)
        dbi_ref[...] += lax.dot_general(ub, gb_i, tn, preferred_element_type=F32)
        lr_, li_ = lr_ref[...], li_ref[...]
        x_r = lr_ * xr - li_ * xi + jnp.dot(ub, br_ref[...], preferred_element_type=F32)
        x_i = lr_ * xi + li_ * xr + jnp.dot(ub, bi_ref[...], preferred_element_type=F32)
        dcr_ref[...] += lax.dot_general(x_r.astype(BF16), dyb, tn, preferred_element_type=F32)
        dci_ref[...] -= lax.dot_general(x_i.astype(BF16), dyb, tn, preferred_element_type=F32)

    rev = (lambda c: nch - 1 - c) if descending else (lambda c: c)
    u_spec = pl.BlockSpec((rows, S5_CB), lambda b, c: (rev(c), b))
    bmat = pl.BlockSpec((None, S5_CB, S5_SB), lambda b, c: (b, 0, 0))
    cmat = pl.BlockSpec((None, S5_SB, S5_CB), lambda b, c: (b, 0, 0))
    lvec = pl.BlockSpec((1, S5_SB), lambda b, c: (0, b))
    evec = pl.BlockSpec((SSM_SEGMENTS, S5_SB), lambda b, c: (0, b))
    xblk = pl.BlockSpec((rows, S5_SB), lambda b, c: (rev(c), b))
    scratch = [pltpu.VMEM((SSM_SEGMENTS, S5_SB), F32)] * 2 + [pltpu.VMEM((rows, S5_SB), F32)] * 2
    e_shape = jax.ShapeDtypeStruct((SSM_SEGMENTS, GN), F32)
    if full:
        return pl.pallas_call(
            body, name=name, grid=(ncb, nch),
            in_specs=[u_spec, u_spec, xblk, xblk, bmat, bmat, cmat, cmat, lvec, lvec, evec, evec],
            out_specs=[u_spec, bmat, bmat, cmat, cmat, lvec, lvec],
            out_shape=[jax.ShapeDtypeStruct((S, D), F32), jax.ShapeDtypeStruct(b_re.shape, F32),
                       jax.ShapeDtypeStruct(b_re.shape, F32), jax.ShapeDtypeStruct(c_re.shape, F32),
                       jax.ShapeDtypeStruct(c_re.shape, F32), jax.ShapeDtypeStruct((1, GN), F32),
                       jax.ShapeDtypeStruct((1, GN), F32)],
            scratch_shapes=scratch, compiler_params=_cp(("parallel", "arbitrary")),
        )(dy, u, *xp, b_re, b_im, c_re, c_im, lam_re, lam_im, *starts)
    return pl.pallas_call(
        body, name=name, grid=(ncb, nch), in_specs=[u_spec, cmat, cmat, lvec, lvec],
        out_specs=[evec, evec], out_shape=[e_shape, e_shape],
        scratch_shapes=scratch, compiler_params=_cp(("parallel", "arbitrary")),
    )(dy, c_re, c_im, lam_re, lam_im)


def _s5_perm(t):
    S, D = t.shape
    return t.reshape(SSM_SEGMENTS, S // SSM_SEGMENTS, D).transpose(1, 0, 2).reshape(S, D)


def _s5_unperm(t):
    S, D = t.shape
    return t.reshape(S // SSM_SEGMENTS, SSM_SEGMENTS, D).transpose(1, 0, 2).reshape(S, D)


def _s5_blockdiag_b(bb, ncb):
    gpb = S5_CB // SSM_GROUP_CH
    t = bb.reshape(SSM_GROUP_CH, ncb, gpb, SSM_STATE)
    return jnp.einsum('cbgn,gh->bgchn', t, jnp.eye(gpb, dtype=bb.dtype)).reshape(ncb, S5_CB, S5_SB)


def _s5_blockdiag_b_t(dblk):
    ncb = dblk.shape[0]
    gpb = S5_CB // SSM_GROUP_CH
    t = dblk.reshape(ncb, gpb, SSM_GROUP_CH, gpb, SSM_STATE)
    return jnp.einsum('bgchn,gh->cbgn', t, jnp.eye(gpb, dtype=dblk.dtype)).reshape(SSM_GROUP_CH, -1)


def _s5_blockdiag_c(c, ncb):
    gpb = S5_CB // SSM_GROUP_CH
    t = c.reshape(ncb, gpb, SSM_GROUP_CH, SSM_STATE)
    return jnp.einsum('bgcn,gh->bgnhc', t, jnp.eye(gpb, dtype=c.dtype)).reshape(ncb, S5_SB, S5_CB)


def _s5_blockdiag_c_t(dblk):
    ncb = dblk.shape[0]
    gpb = S5_CB // SSM_GROUP_CH
    t = dblk.reshape(ncb, gpb, SSM_STATE, gpb, SSM_GROUP_CH)
    return jnp.einsum('bgnhc,gh->bgcn', t, jnp.eye(gpb, dtype=dblk.dtype)).reshape(-1, SSM_GROUP_CH, SSM_STATE)


_GELU_C = math.sqrt(2.0 / math.pi)


def _gelu(y):
    return y * (0.5 * (1.0 + jnp.tanh(_GELU_C * (y + 0.044715 * (y * y * y)))))


def _gelu_grad(y):
    t = jnp.tanh(_GELU_C * (y + 0.044715 * (y * y * y)))
    return 0.5 * (1.0 + t) + 0.5 * y * (1.0 - t * t) * (_GELU_C * (1.0 + 3.0 * 0.044715 * y * y))


def _rowwise(fn, ins, outs, name, acc=()):
    S, D = next(a.shape for a in ins if a.shape[0] != 1)
    tr = _pick(S, (256, 128, 8))
    row = pl.BlockSpec((tr, D), lambda i: (i, 0))
    vec = pl.BlockSpec((1, D), lambda i: (0, 0))
    n_in = len(ins)

    def body(*refs):
        res = fn(*[r[...] for r in refs[:n_in]])
        for k, (o, v) in enumerate(zip(refs[n_in:], res)):
            if k in acc:
                @pl.when(pl.program_id(0) == 0)
                def _():
                    o[...] = jnp.zeros_like(o)
                o[...] += jnp.sum(v, axis=0, keepdims=True)
            else:
                o[...] = v.astype(o.dtype)

    return pl.pallas_call(
        body, name=name, grid=(S // tr,), in_specs=[vec if a.shape[0] == 1 else row for a in ins],
        out_specs=[vec if k in acc else row for k in range(len(outs))],
        out_shape=[jax.ShapeDtypeStruct((1, D) if k in acc else (S, D), dt) for k, dt in enumerate(outs)],
        compiler_params=_cp(("arbitrary",) if acc else ("parallel",)),
    )(*ins)


def _s5_params(p):
    G, N = p["a_re"].shape[1:]
    vec = lambda a: a.reshape(2, 1, G * N)
    ls = jnp.broadcast_to(p["log_step"][:, :, None], (2, G, N))
    bt = lambda b: b.transpose(0, 3, 1, 2).reshape(2, SSM_GROUP_CH, G * N)
    return vec(p["a_re"]), vec(p["a_im"]), vec(ls), bt(p["b_re"]), bt(p["b_im"])


def _s5_layer_fwd(x, norm_g, p, w_glu, tag):
    S, D = x.shape
    ncb = D // S5_CB
    xp = _s5_perm(x)
    hn = _rmsnorm_fwd(xp, norm_g, F32, f"s5_norm_{tag}")
    raw = _s5_params(p)
    lam_r, lam_i, bb_r, bb_i = _s5_disc_fwd(*raw, f"s5_disc_{tag}")
    dirs = []
    ys = []
    for dirn in range(2):
        mats = (_s5_blockdiag_b(bb_r[dirn], ncb).astype(BF16), _s5_blockdiag_b(bb_i[dirn], ncb).astype(BF16),
                _s5_blockdiag_c(p["c_re"][dirn], ncb).astype(BF16), _s5_blockdiag_c(p["c_im"][dirn], ncb).astype(BF16))
        lam = (lam_r[dirn], lam_i[dirn])
        ends = _s5_scan(hn, mats[0], mats[1], None, None, *lam, None, dirn == 1, f"s5_ends_{tag}_{dirn}")
        xr, xi, y = _s5_scan(hn, *mats, *lam, ends, dirn == 1, f"s5_scan_{tag}_{dirn}")
        dirs.append(((xr, xi), mats, lam))
        ys.append(y)
    ytot, z = _rowwise(lambda u, d, a, b: ((lambda y: (y, _gelu(y)))(d * u + a + b)),
                       [hn, p["d"], ys[0], ys[1]], [F32, BF16], f"s5_gelu_{tag}")
    t = _mm(z, w_glu, name=f"s5_glu_{tag}", epi=lambda r, b: r + b, epi_in=(p["b_glu"],))
    (x_new,) = _rowwise(lambda xx, zz, tt: (xx + zz.astype(F32) * jax.nn.sigmoid(tt),),
                        [xp, z, t], [F32], f"s5_out_{tag}")
    return _s5_unperm(x_new), (xp, hn, raw, dirs, ytot, z, t)


def _s5_layer_bwd(dres, saved, norm_g, p, w_glu, tag):
    x, hn, raw, dirs, ytot, z, t = saved
    S, D = x.shape
    G, N = p["a_re"].shape[1:]
    dres = _s5_perm(dres[0])

    def glu_bwd(do, zz, tt):
        sg = jax.nn.sigmoid(tt)
        dt = do * zz.astype(F32) * (sg * (1.0 - sg))
        return dt, do * sg, dt

    dt, dzd, db_glu = _rowwise(glu_bwd, [dres, z, t], [BF16, F32, F32], f"s5_out_bwd_{tag}", acc=(2,))
    dz = _mm(dt, w_glu, tb=True, name=f"s5_dz_{tag}", epi=_add_epi, epi_in=(dzd,))
    dw_glu = _mm(z, dt, ta=True, name=f"s5_dwglu_{tag}", pieces='row')

    def gelu_bwd(dzz, y, u, d):
        dy = dzz * _gelu_grad(y)
        return dy, dy * d, dy * u

    dy, du, dd = _rowwise(gelu_bwd, [dz, ytot, hn, p["d"]], [F32, F32, F32], f"s5_gelu_bwd_{tag}", acc=(2,))
    d_lr, d_li, d_bbr, d_bbi, d_cr, d_ci = [], [], [], [], [], []
    du = [du]
    for dirn in range(2):
        xp, mats, lam = dirs[dirn]
        starts = _s5_scan_bwd(dy, None, None, None, None, mats[2], mats[3], *lam, None, dirn == 0,
                              f"s5_starts_{tag}_{dirn}")
        dup, dbr, dbi, dcr, dci, dlr, dli = _s5_scan_bwd(dy, hn, xp, *mats, *lam, starts, dirn == 0,
                                                         f"s5_scan_bwd_{tag}_{dirn}")
        du.append(dup)
        d_lr.append(dlr)
        d_li.append(dli)
        d_bbr.append(_s5_blockdiag_b_t(dbr))
        d_bbi.append(_s5_blockdiag_b_t(dbi))
        d_cr.append(_s5_blockdiag_c_t(dcr))
        d_ci.append(_s5_blockdiag_c_t(dci))
    da_re, da_im, dls, db_re, db_im = _s5_disc_bwd(*raw, jnp.stack(d_lr), jnp.stack(d_li), jnp.stack(d_bbr),
                                                   jnp.stack(d_bbi), f"s5_disc_bwd_{tag}")
    dres, dg = _rmsnorm_bwd(x, norm_g, du, dres, f"s5_norm_bwd_{tag}")
    dres = tuple(_s5_unperm(t_) for t_ in dres)
    unb = lambda b: b.reshape(2, SSM_GROUP_CH, G, N).transpose(0, 2, 3, 1)
    grads = dict(a_re=da_re.reshape(2, G, N), a_im=da_im.reshape(2, G, N), log_step=dls.reshape(2, G, N).sum(-1),
                 b_re=unb(db_re), b_im=unb(db_im), c_re=jnp.stack(d_cr), c_im=jnp.stack(d_ci),
                 d=dd, w_glu=dw_glu, b_glu=db_glu, norm=dg)
    return dres, grads


def _adamw(w, g, m, v, name):
    R, C = w.shape
    tr = _pick(R, (512, 256, 128, 64, 32, 16, 8))
    tn = _pick(C, (512, 256, 128))

    def body(w_ref, g_ref, m_ref, v_ref, d_ref, nm_ref, nv_ref):
        gg = g_ref[...]
        m2 = ADAM_B1 * m_ref[...] + (1.0 - ADAM_B1) * gg
        v2 = ADAM_B2 * v_ref[...] + (1.0 - ADAM_B2) * (gg * gg)
        m_hat = m2 / (1.0 - ADAM_B1 ** ADAM_STEP)
        v_hat = v2 / (1.0 - ADAM_B2 ** ADAM_STEP)
        d_ref[...] = -ADAM_LR * (m_hat / (jnp.sqrt(v_hat) + ADAM_EPS) + ADAM_WD * w_ref[...])
        nm_ref[...] = m2
        nv_ref[...] = v2

    blk = pl.BlockSpec((tr, tn), lambda i, j: (i, j))
    shp = jax.ShapeDtypeStruct((R, C), F32)
    return pl.pallas_call(body, name=name, grid=(R // tr, C // tn), in_specs=[blk] * 4, out_specs=[blk] * 3,
                          out_shape=[shp] * 3, compiler_params=_cp(("parallel", "parallel")))(w, g, m, v)


def _add_selected(p, sel, others, name, also_bf16=False):
    K, _, M, C = p.shape
    tr = _pick(M, [t for t in (512, 256, 128, 64, 32, 16) if t * C * 4 <= 2 ** 21])
    n_o = len(others)

    def body(sel_ref, p_ref, *refs):
        acc = p_ref[...]
        for r in refs[:n_o]:
            acc = acc + r[...].astype(F32)
        refs[n_o][...] = acc
        if also_bf16:
            refs[n_o + 1][...] = acc.astype(BF16)

    blk = pl.BlockSpec((None, tr, C), lambda k, i, s: (k, i, 0))
    grid_spec = pltpu.PrefetchScalarGridSpec(
        num_scalar_prefetch=1, grid=(K, M // tr),
        in_specs=[pl.BlockSpec((None, None, tr, C), lambda k, i, s: (k, s[0], i, 0))] + [blk] * n_o,
        out_specs=[blk, blk] if also_bf16 else blk)
    shp = jax.ShapeDtypeStruct((K, M, C), F32)
    return pl.pallas_call(body, name=name, grid_spec=grid_spec,
                          out_shape=[shp, jax.ShapeDtypeStruct((K, M, C), BF16)] if also_bf16 else shp,
                          compiler_params=_cp(("parallel", "parallel")))(sel, p, *others)


def _sum_slots(a, own, me, name):
    n, R, C = a.shape
    tr = _pick(R, (512, 256, 128, 64, 32, 16, 8))

    def body(me_ref, a_ref, own_ref, o_ref):
        term = lambda k: jnp.where(me_ref[0] == k, own_ref[...], a_ref[k])
        acc = term(0)
        for k in range(1, n):
            acc = acc + term(k)
        o_ref[...] = acc

    grid_spec = pltpu.PrefetchScalarGridSpec(
        num_scalar_prefetch=1, grid=(R // tr,),
        in_specs=[pl.BlockSpec((n, tr, C), lambda i, s: (0, i, 0)), pl.BlockSpec((tr, C), lambda i, s: (i, 0))],
        out_specs=pl.BlockSpec((tr, C), lambda i, s: (i, 0)))
    return pl.pallas_call(body, name=name, grid_spec=grid_spec, out_shape=jax.ShapeDtypeStruct((R, C), F32),
                          compiler_params=_cp(("parallel",)))(me, a, own)


def _position():
    return lax.axis_index("x"), lax.axis_index("y"), lax.axis_index("c")


def _other_chips(x, y):
    return [(1 - x, y), (x, 1 - y), (1 - x, 1 - y)]


def _exchange_call(x, name):
    n_in, n_out = len(x.ins), len(x.out_shapes)

    def body(*refs):
        in_refs, out_refs, sems = refs[:n_in], refs[n_in:n_in + n_out], refs[n_in + n_out:]
        x.start(in_refs, out_refs, *sems)
        x.wait(in_refs, out_refs, *sems)

    hbm = pl.BlockSpec(memory_space=pltpu.HBM)
    return pl.pallas_call(body, name=name, in_specs=[hbm] * n_in, out_specs=[hbm] * n_out, out_shape=x.out_shapes,
                          scratch_shapes=x.sem_shapes())(*x.ins)


def _gather_chips(arrs):
    n = len(arrs)

    def plan(ins, outs):
        x, y, c = _position()
        me = 2 * x + y
        sends, recvs = [], []
        for px, py in _other_chips(x, y):
            for i in range(n):
                sends.append((ins[i], outs[i].at[me], (px, py, c)))
                recvs.append((outs[i].at[2 * px + py], (px, py, c)))
        return sends, recvs

    return _Exchange(arrs, [jax.ShapeDtypeStruct((4,) + a.shape, a.dtype) for a in arrs], 3 * n, plan)


def _gather_halves_ici(arrs):
    n = len(arrs)
    hr = [a.shape[1] // 2 for a in arrs]

    def plan(ins, outs):
        x, y, c = _position()
        me = 2 * x + y
        sends, recvs = [], []
        for px, py in _other_chips(x, y):
            for i in range(n):
                sends.append((ins[i].at[:, pl.ds(c * hr[i], hr[i])], outs[i].at[me, c], (px, py, c)))
                recvs.append((outs[i].at[2 * px + py, c], (px, py, c)))
        return sends, recvs

    shapes = [jax.ShapeDtypeStruct((N_CHIPS, 2, a.shape[0], a.shape[1] // 2, a.shape[2]), a.dtype) for a in arrs]
    return _Exchange(arrs, shapes, 3 * n, plan)


def _gather_halves_d2d(landed):
    n = len(landed)

    def plan(ins, outs):
        x, y, c = _position()
        sib = (x, y, 1 - c)
        sends, recvs = [], []
        for px, py in _other_chips(x, y):
            for i in range(n):
                sends.append((ins[i].at[2 * px + py, c], outs[i].at[2 * px + py, c], sib))
                recvs.append((outs[i].at[2 * px + py, 1 - c], sib))
        return sends, recvs

    return _Exchange(landed, [jax.ShapeDtypeStruct(a.shape, a.dtype) for a in landed], 3 * n, plan)


class _Reducer:
    def __init__(self, pieces, tags, sel_core, sel_chip):
        self.pieces, self.tags, self.sel_core, self.sel_chip = pieces, tags, sel_core, sel_chip

    def sibling(self):
        return _sibling_halves(self.pieces)

    def chips(self, from_sibling):
        self.pair = [_add_selected(p, self.sel_core, [r], f"reduce_add_pair_{t}", also_bf16=True)
                     for t, p, r in zip(self.tags, self.pieces, from_sibling)]
        return _scatter_chips([pb for _, pb in self.pair])

    def swap(self, from_chips):
        self.mine = [_add_selected(p.reshape((1,) + p.shape), self.sel_chip, [r[k:k + 1] for k in range(3)],
                                   f"reduce_add_chips_{t}")[0]
                     for t, (p, _), r in zip(self.tags, self.pair, from_chips)]
        return _sibling_swap(self.mine)

    def finish(self, theirs):
        return list(zip(self.mine, theirs))

    def run(self, tag):
        landed = _exchange_call(self.sibling(), f"reduce_sibling_{tag}")
        landed = _exchange_call(self.chips(landed), f"reduce_chips_{tag}")
        return self.finish(_exchange_call(self.swap(landed), f"reduce_join_{tag}"))


def _sibling_halves(pieces):
    n = len(pieces)

    def plan(ins, outs):
        x, y, c = _position()
        sib = (x, y, 1 - c)
        return [(ins[i].at[:, 1 - c], outs[i], sib) for i in range(n)], [(outs[i], sib) for i in range(n)]

    shapes = [jax.ShapeDtypeStruct((p.shape[0],) + p.shape[2:], p.dtype) for p in pieces]
    return _Exchange(pieces, shapes, n, plan)


def _scatter_chips(sums):
    n = len(sums)

    def plan(ins, outs):
        x, y, c = _position()
        sends, recvs = [], []
        for j, (px, py) in enumerate(_other_chips(x, y)):
            for i in range(n):
                sends.append((ins[i].at[2 * px + py], outs[i].at[j], (px, py, c)))
                recvs.append((outs[i].at[j], (px, py, c)))
        return sends, recvs

    return _Exchange(sums, [jax.ShapeDtypeStruct((3,) + s.shape[1:], s.dtype) for s in sums], 3 * n, plan)


def _sibling_swap(halves):
    n = len(halves)

    def plan(ins, outs):
        x, y, c = _position()
        sib = (x, y, 1 - c)
        return [(ins[i], outs[i], sib) for i in range(n)], [(outs[i], sib) for i in range(n)]

    return _Exchange(halves, [jax.ShapeDtypeStruct(h.shape, h.dtype) for h in halves], n, plan)


WEIGHTS = ['mix_norm', 'ffn_norm', 'final_norm', 'attn_w_qkv', 'attn_w_o', 'attn_sink', 'ssm_a_re', 'ssm_a_im',
           'ssm_log_step', 'ssm_b_re', 'ssm_b_im', 'ssm_c_re', 'ssm_c_im', 'ssm_d', 'ssm_w_glu', 'ssm_b_glu',
           'mla_w_dqkv', 'mla_q_norm', 'mla_kv_norm', 'mla_w_uq', 'mla_w_ukv', 'mla_w_o', 'ffn_w_up',
           'ffn_conv_w', 'ffn_conv_b', 'ffn_w_down']
BIG = dict(attn_w_qkv='col', attn_w_o='row', ssm_w_glu='row', mla_w_dqkv='row', mla_w_uq='col',
           mla_w_ukv='col', mla_w_o='row', ffn_w_up='col', ffn_w_down='row')
SMALL_SHARDED = ('mla_q_norm', 'mla_kv_norm', 'ffn_conv_w')
N_CHIPS = 4


def _assemble(over_ici, over_d2d, own, chip, core, kind):
    L = over_ici.shape[2]
    half = lax.broadcasted_iota(jnp.int32, (1, 2, 1, 1, 1), 1)
    g = jnp.where(half == core, over_ici, over_d2d)
    own_halves = own.reshape(L, 2, g.shape[3], g.shape[4]).transpose(1, 0, 2, 3)
    slot = lax.broadcasted_iota(jnp.int32, (N_CHIPS, 1, 1, 1, 1), 0)
    g = jnp.where(slot == chip, own_halves[None], g)
    if kind == 'row':
        return g.transpose(2, 0, 1, 3, 4).reshape(L, -1, g.shape[4])
    return g.transpose(2, 1, 3, 0, 4).reshape(L, 2 * g.shape[3], -1)


def _to_pieces(w, kind):
    L, R, C = w.shape
    if kind == 'col':
        t = w.reshape(L, 2, R // 2, N_CHIPS, C // N_CHIPS).transpose(3, 1, 0, 2, 4)
    else:
        t = w.reshape(L, N_CHIPS, R // N_CHIPS, 2, C // 2).transpose(1, 3, 0, 2, 4)
    return t.reshape(N_CHIPS, 2, L * t.shape[3], t.shape[4])


def _from_halves(h, kind, shard_shape):
    L = shard_shape[0]
    t = h.reshape(2, L, -1, h.shape[2])
    t = t.transpose(1, 0, 2, 3) if kind == 'col' else t.transpose(1, 2, 0, 3)
    return t.reshape(shard_shape)


def _pack(arrs):
    flat = jnp.concatenate([a.reshape(-1) for a in arrs])
    pad = (-flat.shape[0]) % (32 * LANES)
    return jnp.pad(flat, (0, pad)).reshape(-1, LANES)


def _unpack(packed, like):
    flat = packed.reshape(-1)
    out, off = [], 0
    for a in like:
        out.append(flat[off:off + a.size].reshape(a.shape))
        off += a.size
    return out


def kernel(x, mix_norm, ffn_norm, final_norm, attn_w_qkv, attn_w_o, attn_sink, ssm_a_re, ssm_a_im, ssm_log_step, ssm_b_re, ssm_b_im, ssm_c_re, ssm_c_im, ssm_d, ssm_w_glu, ssm_b_glu, mla_w_dqkv, mla_q_norm, mla_kv_norm, mla_w_uq, mla_w_ukv, mla_w_o, ffn_w_up, ffn_conv_w, ffn_conv_b, ffn_w_down, loss_target, m_mix_norm, m_ffn_norm, m_final_norm, m_attn_w_qkv, m_attn_w_o, m_attn_sink, m_ssm_a_re, m_ssm_a_im, m_ssm_log_step, m_ssm_b_re, m_ssm_b_im, m_ssm_c_re, m_ssm_c_im, m_ssm_d, m_ssm_w_glu, m_ssm_b_glu, m_mla_w_dqkv, m_mla_q_norm, m_mla_kv_norm, m_mla_w_uq, m_mla_w_ukv, m_mla_w_o, m_ffn_w_up, m_ffn_conv_w, m_ffn_conv_b, m_ffn_w_down, v_mix_norm, v_ffn_norm, v_final_norm, v_attn_w_qkv, v_attn_w_o, v_attn_sink, v_ssm_a_re, v_ssm_a_im, v_ssm_log_step, v_ssm_b_re, v_ssm_b_im, v_ssm_c_re, v_ssm_c_im, v_ssm_d, v_ssm_w_glu, v_ssm_b_glu, v_mla_w_dqkv, v_mla_q_norm, v_mla_kv_norm, v_mla_w_uq, v_mla_w_ukv, v_mla_w_o, v_ffn_w_up, v_ffn_conv_w, v_ffn_conv_b, v_ffn_w_down):
    args = locals()
    w = {n: args[n] for n in WEIGHTS}
    mom = {n: args["m_" + n] for n in WEIGHTS}
    var = {n: args["v_" + n] for n in WEIGHTS}
    depth = mix_norm.shape[0]
    xs = x[0]
    target = loss_target[0]
    chip = 2 * lax.axis_index("x") + lax.axis_index("y")
    core = lax.axis_index("c")

    big_names = list(BIG)
    w_bf = {n: w[n].astype(BF16) for n in big_names}
    mixer_weights = {0: ('attn_w_qkv', 'attn_w_o'), 1: ('ssm_w_glu',),
                     2: ('mla_w_dqkv', 'mla_w_uq', 'mla_w_ukv', 'mla_w_o')}

    def mixer_items(i):
        return [(n, i // 3) for n in mixer_weights[i % 3]]

    def ffn_items(i):
        return [('ffn_w_up', i), ('ffn_w_down', i)]

    def layer_items(i):
        return mixer_items(i) + ffn_items(i)

    def blocks(items):
        return [w_bf[n][l:l + 1] for n, l in items]

    full = {}

    def install(items, landed):
        for (n, l), a, b, own in zip(items, *landed, blocks(items)):
            full[n, l] = _assemble(a, b, own, chip, core, BIG[n])[0]

    over_ici = _exchange_call(_gather_halves_ici(blocks(mixer_items(0))), "gather_first_ici")
    install(mixer_items(0), (over_ici, _exchange_call(_gather_halves_d2d(over_ici), "gather_first_d2d")))
    small_sharded = _exchange_call(_gather_chips([w[n] for n in SMALL_SHARDED]), "gather_small_weights")
    for n, g in zip(SMALL_SHARDED, small_sharded):
        slot = lax.broadcasted_iota(jnp.int32, (N_CHIPS,) + (1,) * w[n].ndim, 0)
        g = jnp.where(slot == chip, w[n][None], g)
        full[n] = jnp.moveaxis(g, 0, -2).reshape(g.shape[1:-1] + (-1,))

    def ffn_args(i):
        return (ffn_norm[i:i + 1], full["ffn_w_up", i], full["ffn_conv_w"][i], ffn_conv_b[i:i + 1], full["ffn_w_down", i])

    def s5_params(j):
        return dict(a_re=ssm_a_re[j], a_im=ssm_a_im[j], log_step=ssm_log_step[j], b_re=ssm_b_re[j], b_im=ssm_b_im[j],
                    c_re=ssm_c_re[j], c_im=ssm_c_im[j], d=ssm_d[j:j + 1], b_glu=ssm_b_glu[j:j + 1])

    def mla_args(j):
        wd, wq, wkv = _mla_weights(full["mla_w_dqkv", j], full["mla_w_uq", j], full["mla_w_ukv", j])
        return (wd, wq, wkv, full["mla_w_o", j], full["mla_q_norm"][j:j + 1], full["mla_kv_norm"][j:j + 1])

    h = xs
    saved = []
    for i in range(depth):
        kind, j = i % 3, i // 3
        g = mix_norm[i:i + 1]
        if kind == 0:
            h, sm, landed = _attn_layer_fwd(h, g, full["attn_w_qkv", j], full["attn_w_o", j], attn_sink[j], f"{i}",
                                            next_blocks=blocks(ffn_items(0)) if i == 0 else None)
            if landed is not None:
                install(ffn_items(0), landed)
        elif kind == 1:
            h, sm = _s5_layer_fwd(h, g, s5_params(j), full["ssm_w_glu", j], f"{i}")
        else:
            h, sm = _mla_layer_fwd(h, g, *mla_args(j), f"{i}")
        nxt = layer_items(i + 1) if i + 1 < depth else None
        h, sf, landed = _ffn_fwd(h, *ffn_args(i), f"{i}", next_blocks=nxt and blocks(nxt))
        if landed is not None:
            install(nxt, landed)
        saved.append((sm, sf))
    loss_part, dres, d_final = _loss_head(h, final_norm.reshape(1, -1), target, "loss_head")

    gl = {n: [None] * w[n].shape[0] for n in WEIGHTS if n != 'final_norm'}
    sel_core = jnp.reshape(core, (1,)).astype(jnp.int32)
    sel_chip = jnp.reshape(chip, (1,)).astype(jnp.int32)
    reduced = {}

    def reducer_for(items):
        pieces = [gl[n][l] if gl[n][l].ndim == 4 else _to_pieces(gl[n][l][None], BIG[n]) for n, l in items]
        return _Reducer(pieces, [f"{n}_{l}" for n, l in items], sel_core, sel_chip)

    pending = None
    for i in reversed(range(depth)):
        kind, j = i % 3, i // 3
        sm, sf = saved[i]
        dres, (dg, dwu, dcw, dcb, dwd), done = _ffn_bwd(dres, sf, *ffn_args(i), f"{i}", reducer=pending)
        if pending is not None:
            reduced.update(zip(layer_items(i + 1), done))
        gl['ffn_norm'][i], gl['ffn_w_up'][i], gl['ffn_conv_w'][i] = dg[0], dwu, dcw
        gl['ffn_conv_b'][i], gl['ffn_w_down'][i] = dcb[0], dwd
        g = mix_norm[i:i + 1]
        if kind == 0:
            own_ffn = reducer_for(ffn_items(0)) if i == 0 else None
            dres, (dg, dwq, dwo, dsk), done = _attn_layer_bwd(dres, sm, g, full["attn_w_qkv", j], full["attn_w_o", j],
                                                              attn_sink[j], f"{i}", reducer=own_ffn)
            if own_ffn is not None:
                reduced.update(zip(ffn_items(0), done))
            gl['attn_w_qkv'][j], gl['attn_w_o'][j], gl['attn_sink'][j] = dwq, dwo, dsk
        elif kind == 1:
            dres, gs = _s5_layer_bwd(dres, sm, g, s5_params(j), full["ssm_w_glu", j], f"{i}")
            dg = gs['norm']
            for k in ('a_re', 'a_im', 'log_step', 'b_re', 'b_im', 'c_re', 'c_im'):
                gl['ssm_' + k][j] = gs[k]
            gl['ssm_d'][j], gl['ssm_b_glu'][j], gl['ssm_w_glu'][j] = gs['d'][0], gs['b_glu'][0], gs['w_glu']
        else:
            dres, (dg, dwd_, dqn, dkvn, dwuq, dwukv, dwo) = _mla_layer_bwd(dres, sm, g, *mla_args(j), f"{i}")
            gl['mla_w_dqkv'][j], gl['mla_q_norm'][j], gl['mla_kv_norm'][j] = dwd_, dqn[0], dkvn[0]
            gl['mla_w_uq'][j], gl['mla_w_ukv'][j], gl['mla_w_o'][j] = dwuq, dwukv, dwo
        gl['mix_norm'][i] = dg[0]
        pending = reducer_for(layer_items(i) if i > 0 else mixer_items(0))
    reduced.update(zip(mixer_items(0), pending.run("first_layer")))
    local = {n: jnp.stack(v) for n, v in gl.items() if n not in BIG}
    local['final_norm'] = d_final[0]

    grads = {}
    for n in big_names:
        shards = []
        for l in range(w[n].shape[0]):
            a, b = reduced[n, l]
            halves = jnp.where(core == 0, jnp.stack([a, b]), jnp.stack([b, a]))
            shards.append(_from_halves(halves, BIG[n], (1,) + w[n].shape[1:]))
        grads[n] = jnp.concatenate(shards, axis=0)

    small_names = [n for n in WEIGHTS if n not in BIG]
    packed = _pack([local[n] for n in small_names])
    halves = packed.reshape(1, 2, packed.shape[0] // 2, LANES)
    (from_sibling,) = _exchange_call(_sibling_halves([halves]), "reduce_small_sibling")
    pair = _add_selected(halves, sel_core, [from_sibling], "reduce_small_pair")[0]
    (from_chips,) = _exchange_call(_gather_chips([pair]), "reduce_small_chips")
    total = _sum_slots(from_chips, pair, sel_chip, "reduce_small_sum")
    (theirs,) = _exchange_call(_sibling_swap([total]), "reduce_small_join")
    summed = jnp.where(core == 0, jnp.concatenate([total, theirs]), jnp.concatenate([theirs, total]))
    for n, gsum in zip(small_names, _unpack(summed, [local[n] for n in small_names])):
        if n in SMALL_SHARDED:
            width = w[n].shape[-1]
            gsum = lax.dynamic_slice_in_dim(gsum, chip * width, width, axis=gsum.ndim - 1)
        grads[n] = gsum

    delta, new_m, new_v = {}, {}, {}
    for n in big_names:
        two_d = lambda a: a.reshape(-1, a.shape[-1])
        d_, m_, v_ = _adamw(two_d(w[n]), two_d(grads[n]), two_d(mom[n]), two_d(var[n]), f"adamw_{n}")
        delta[n], new_m[n], new_v[n] = (t.reshape(w[n].shape) for t in (d_, m_, v_))
    outs = _adamw(*[_pack([src[n] for n in small_names]) for src in (w, grads, mom, var)], "adamw_small")
    for dst, packed in zip((delta, new_m, new_v), outs):
        for n, t in zip(small_names, _unpack(packed, [w[n] for n in small_names])):
            dst[n] = t

    loss = lax.psum(loss_part[0, 0], ("x", "y", "c"))
    return (loss, dres[0][None], *[grads[n] for n in WEIGHTS], *[delta[n] for n in WEIGHTS],
            *[new_m[n] for n in WEIGHTS], *[new_v[n] for n in WEIGHTS])
```

```python
import functools
import math

import numpy as np
import jax
import jax.numpy as jnp
from jax import lax
from jax.experimental import pallas as pl
from jax.experimental.pallas import tpu as pltpu

F32 = jnp.float32
BF16 = jnp.bfloat16
MESH = pl.DeviceIdType.MESH

RMS_EPS = 1e-6
ATTN_DH = 64
ATTN_GROUP = 8
ATTN_BLOCK = 128
SSM_GROUP_CH = 16
SSM_STATE = 64
SSM_SEGMENTS = 8
MLA_HEADS = 16
MLA_LORA = 512
MLA_NOPE = 128
MLA_ROPE = 64
MLA_V = 128
ROPE_THETA = 10000.0
LANES = 128
VMEM_LIMIT = 56 * 1024 * 1024

ADAM_LR = 0.001
ADAM_B1 = 0.9
ADAM_B2 = 0.999
ADAM_EPS = 1e-08
ADAM_WD = 0.01
ADAM_STEP = 10


def _cp(sem=None):
    kw = dict(vmem_limit_bytes=VMEM_LIMIT)
    if sem is not None:
        kw["dimension_semantics"] = sem
    return pltpu.CompilerParams(**kw)


MM_FULL_K = 2048
MM_FULL_N = 1536


def _pick(n, cands):
    for c in cands:
        if n % c == 0:
            return c
    return n


class _Exchange:
    def __init__(self, ins, out_shapes, n_remote, plan):
        self.ins, self.out_shapes, self.n_remote, self.plan = list(ins), list(out_shapes), n_remote, plan

    def _copy(self, k, src, dst, dev, send_sems, recv_sems):
        return pltpu.make_async_remote_copy(src_ref=src, dst_ref=dst, send_sem=send_sems.at[k],
                                            recv_sem=recv_sems.at[k], device_id=dev, device_id_type=MESH)

    def start(self, in_refs, out_refs, send_sems, recv_sems):
        sends, recvs = self.plan(in_refs, out_refs)
        assert len(sends) == len(recvs) == self.n_remote
        for k, (src, dst, dev) in enumerate(sends):
            self._copy(k, src, dst, dev, send_sems, recv_sems).start()

    def wait(self, in_refs, out_refs, send_sems, recv_sems):
        sends, recvs = self.plan(in_refs, out_refs)
        for k, (dst, dev) in enumerate(recvs):
            self._copy(k, dst, dst, dev, send_sems, recv_sems).wait_recv()
        for k, (src, dst, dev) in enumerate(sends):
            self._copy(k, src, dst, dev, send_sems, recv_sems).wait_send()

    def sem_shapes(self):
        return [pltpu.SemaphoreType.DMA((self.n_remote,)), pltpu.SemaphoreType.DMA((self.n_remote,))]


def _gmm(a, b, *, grid, a_spec, b_spec, o_spec, out_shape, dims, kax, acc_shape, name,
         epi=None, epi_in=(), epi_specs=(), hosted=None):
    nk = grid[kax]
    n_epi = len(epi_in)
    n_hin = len(hosted.ins) if hosted else 0
    n_hout = len(hosted.out_shapes) if hosted else 0

    def finish(r, e_refs, o_ref):
        if epi is not None:
            r = epi(r, *[e[...] for e in e_refs])
        o_ref[...] = r.astype(o_ref.dtype)

    def product(a_ref, b_ref):
        return lax.dot_general(a_ref[...].astype(BF16), b_ref[...].astype(BF16), dims, preferred_element_type=F32)

    def body(a_ref, b_ref, *rest):
        e_refs = rest[:n_epi]
        h_in = rest[n_epi:n_epi + n_hin]
        o_ref = rest[n_epi + n_hin]
        h_out = rest[n_epi + n_hin + 1:n_epi + n_hin + 1 + n_hout]
        scratch = rest[n_epi + n_hin + 1 + n_hout:]
        if hosted:
            sems = scratch[-2:]
            ids = [pl.program_id(i) for i in range(len(grid))]
            first = functools.reduce(jnp.logical_and, [i == 0 for i in ids])
            last = functools.reduce(jnp.logical_and, [i == n - 1 for i, n in zip(ids, grid)])
            pl.when(first)(lambda: hosted.start(h_in, h_out, *sems))
        if nk == 1:
            finish(product(a_ref, b_ref), e_refs, o_ref)
        else:
            acc_ref = scratch[0]
            k = pl.program_id(kax)

            @pl.when(k == 0)
            def _():
                acc_ref[...] = jnp.zeros_like(acc_ref)

            acc_ref[...] += product(a_ref, b_ref)

            @pl.when(k == nk - 1)
            def _():
                finish(acc_ref[...], e_refs, o_ref)
        if hosted:
            pl.when(last)(lambda: hosted.wait(h_in, h_out, *sems))

    sem = tuple("arbitrary" if (i == kax or hosted) else "parallel" for i in range(len(grid)))
    hbm = pl.BlockSpec(memory_space=pltpu.HBM)
    res = pl.pallas_call(
        body, name=name, grid=grid,
        in_specs=[a_spec, b_spec, *epi_specs] + [hbm] * n_hin,
        out_specs=[o_spec] + [hbm] * n_hout if hosted else o_spec,
        out_shape=[out_shape] + hosted.out_shapes if hosted else out_shape,
        scratch_shapes=([] if nk == 1 else [pltpu.VMEM(acc_shape, F32)]) + (hosted.sem_shapes() if hosted else []),
        compiler_params=_cp(sem),
    )(a, b, *epi_in, *(hosted.ins if hosted else ()))
    return (res[0], list(res[1:])) if hosted else res


def _mm(a, b, *, ta=False, tb=False, out_dtype=F32, name, epi=None, epi_in=(), pieces=None, hosted=None):
    a_parts = a.shape[0] if a.ndim == 3 else 1
    b_parts = b.shape[0] if b.ndim == 3 else 1
    assert not (ta and a_parts > 1) and not (tb and b_parts > 1)
    (K, M) = a.shape if ta else (a.shape[-1] * a_parts, a.shape[-2])
    (N, K2) = b.shape if tb else (b.shape[-1] * b_parts, b.shape[-2])
    assert K == K2, (a.shape, b.shape, ta, tb)
    m_unit = {None: M, 'col': M // 2, 'row': M // N_CHIPS}[pieces]
    n_unit = {None: N // b_parts, 'col': N // N_CHIPS, 'row': N // 2}[pieces]
    k_unit = K // a_parts
    tm = _pick(m_unit, (1024, 1408, 512, 256, 128))
    if (pieces is None and not ta and K > MM_FULL_K and M % 2048 == 0
            and not any(e.shape[0] != 1 for e in epi_in)):
        tm = 2048
    tn = n_unit if n_unit <= MM_FULL_N else _pick(n_unit, (1024, 1408, 512, 256, 128))
    tk = k_unit if k_unit <= MM_FULL_K else _pick(k_unit, (2048, 1408, 1024, 512, 256, 128))
    grid = (M // tm, N // tn, K // tk)
    if ta:
        a_spec = pl.BlockSpec((tk, tm), lambda i, j, k: (k, i))
    elif a_parts > 1:
        nkp = k_unit // tk
        a_spec = pl.BlockSpec((None, tm, tk), lambda i, j, k: (k // nkp, i, k % nkp))
    else:
        a_spec = pl.BlockSpec((tm, tk), lambda i, j, k: (i, k))
    if tb:
        b_spec = pl.BlockSpec((tn, tk), lambda i, j, k: (j, k))
    elif b_parts > 1:
        njp = (N // b_parts) // tn
        b_spec = pl.BlockSpec((None, tk, tn), lambda i, j, k: (j // njp, k, j % njp))
    else:
        b_spec = pl.BlockSpec((tk, tn), lambda i, j, k: (k, j))
    dims = (((0 if ta else 1,), (1 if tb else 0,)), ((), ()))
    epi_specs = [pl.BlockSpec((1, tn), lambda i, j, k: (0, j)) if e.shape[0] == 1
                 else pl.BlockSpec((tm, tn), lambda i, j, k: (i, j)) for e in epi_in]
    if pieces is None:
        o_spec = pl.BlockSpec((tm, tn), lambda i, j, k: (i, j))
        out_shape = jax.ShapeDtypeStruct((M, N), out_dtype)
    else:
        mi, nj = m_unit // tm, n_unit // tn
        if pieces == 'col':
            o_spec = pl.BlockSpec((None, None, tm, tn), lambda i, j, k: (j // nj, i // mi, i % mi, j % nj))
        else:
            o_spec = pl.BlockSpec((None, None, tm, tn), lambda i, j, k: (i // mi, j // nj, i % mi, j % nj))
        out_shape = jax.ShapeDtypeStruct((N_CHIPS, 2, m_unit, n_unit), out_dtype)
    return _gmm(a, b, grid=grid, a_spec=a_spec, b_spec=b_spec, o_spec=o_spec, out_shape=out_shape, dims=dims, kax=2,
                acc_shape=(tm, tn), name=name, epi=epi, epi_in=epi_in, epi_specs=epi_specs, hosted=hosted)


def _add_epi(r, res):
    return res + r


def _rmsnorm_fwd(x, g, out_dtype, name):
    S, D = x.shape
    tr = _pick(S, (256, 128, 8))

    def body(x_ref, g_ref, o_ref):
        xf = x_ref[...]
        r = lax.rsqrt(jnp.mean(xf * xf, axis=-1, keepdims=True) + RMS_EPS)
        o_ref[...] = ((xf * r) * g_ref[...]).astype(o_ref.dtype)

    return pl.pallas_call(
        body, name=name, grid=(S // tr,),
        in_specs=[pl.BlockSpec((tr, D), lambda i: (i, 0)), pl.BlockSpec((1, D), lambda i: (0, 0))],
        out_specs=pl.BlockSpec((tr, D), lambda i: (i, 0)),
        out_shape=jax.ShapeDtypeStruct((S, D), out_dtype), compiler_params=_cp(("parallel",)),
    )(x, g)


def _rms_bwd_math(xf, g, dy):
    r = lax.rsqrt(jnp.mean(xf * xf, axis=-1, keepdims=True) + RMS_EPS)
    xh = xf * r
    dxh = dy * g
    dx = r * (dxh - xh * jnp.mean(dxh * xh, axis=-1, keepdims=True))
    return dx, jnp.sum(dy * xh, axis=0, keepdims=True)


def _rmsnorm_bwd(x, g, dy, dres, name):
    S, D = x.shape
    tr = _pick(S, (256, 128, 8))
    dys = list(dy) if isinstance(dy, (list, tuple)) else [dy]
    n_dy = len(dys)

    def body(x_ref, g_ref, *refs):
        dy_refs = refs[:n_dy]
        dres_ref, dx_ref, dxb_ref, dg_ref = refs[n_dy:]

        @pl.when(pl.program_id(0) == 0)
        def _():
            dg_ref[...] = jnp.zeros_like(dg_ref)

        dy_sum = dy_refs[0][...].astype(F32)
        for r in dy_refs[1:]:
            dy_sum = dy_sum + r[...].astype(F32)
        dx, dg = _rms_bwd_math(x_ref[...], g_ref[...], dy_sum)
        tot = dres_ref[...] + dx
        dx_ref[...] = tot
        dxb_ref[...] = tot.astype(BF16)
        dg_ref[...] += dg

    row = pl.BlockSpec((tr, D), lambda i: (i, 0))
    vec = pl.BlockSpec((1, D), lambda i: (0, 0))
    dx, dxb, dg = pl.pallas_call(
        body, name=name, grid=(S // tr,), in_specs=[row, vec] + [row] * (n_dy + 1), out_specs=[row, row, vec],
        out_shape=[jax.ShapeDtypeStruct((S, D), F32), jax.ShapeDtypeStruct((S, D), BF16),
                   jax.ShapeDtypeStruct((1, D), F32)],
        compiler_params=_cp(("arbitrary",)),
    )(x, g, *dys, dres)
    return (dx, dxb), dg


def _loss_head(x, g, target, name):
    S, D = x.shape
    tr = _pick(S, (256, 128, 8))

    def body(x_ref, g_ref, t_ref, loss_ref, dx_ref, dxb_ref, dg_ref):
        @pl.when(pl.program_id(0) == 0)
        def _():
            dg_ref[...] = jnp.zeros_like(dg_ref)
            loss_ref[...] = jnp.zeros_like(loss_ref)

        xf = x_ref[...]
        gg = g_ref[...]
        r = lax.rsqrt(jnp.mean(xf * xf, axis=-1, keepdims=True) + RMS_EPS)
        e = (xf * r) * gg - t_ref[...]
        loss_ref[...] += 0.5 * jnp.sum(jnp.mean(e * e, axis=-1, keepdims=True), axis=0, keepdims=True)
        dx, dg = _rms_bwd_math(xf, gg, e * (1.0 / D))
        dx_ref[...] = dx
        dxb_ref[...] = dx.astype(BF16)
        dg_ref[...] += dg

    row = pl.BlockSpec((tr, D), lambda i: (i, 0))
    vec = pl.BlockSpec((1, D), lambda i: (0, 0))
    one = pl.BlockSpec((1, 1), lambda i: (0, 0))
    loss, dx, dxb, dg = pl.pallas_call(
        body, name=name, grid=(S // tr,), in_specs=[row, vec, row], out_specs=[one, row, row, vec],
        out_shape=[jax.ShapeDtypeStruct((1, 1), F32), jax.ShapeDtypeStruct((S, D), F32),
                   jax.ShapeDtypeStruct((S, D), BF16), jax.ShapeDtypeStruct((1, D), F32)],
        compiler_params=_cp(("arbitrary",)),
    )(x, g, target)
    return loss, (dx, dxb), dg


HALO = 16


def _shift_rows(main, prev_row, next_row):
    tr = main.shape[0]
    row = lax.broadcasted_iota(jnp.int32, main.shape, 0)
    up = jnp.where(row == 0, prev_row, pltpu.roll(main, 1, 0))
    dn = jnp.where(row == tr - 1, next_row, pltpu.roll(main, tr - 1, 0))
    return up, dn


def _halo_specs(tr, tn, S, col_of):
    hb = tr // HALO
    last = S // HALO - 1
    return [pl.BlockSpec((tr, tn), lambda j, i: (i, col_of(j))),
            pl.BlockSpec((HALO, tn), lambda j, i: (jnp.maximum(i * hb - 1, 0), col_of(j))),
            pl.BlockSpec((HALO, tn), lambda j, i: (jnp.minimum((i + 1) * hb, last), col_of(j)))]


def _halo_rows(main_ref, prev_ref, next_ref, i, n_i):
    main = main_ref[...].astype(F32)
    prev_row = prev_ref[HALO - 1:HALO, :].astype(F32) * (i > 0).astype(F32)
    next_row = next_ref[0:1, :].astype(F32) * (i < n_i - 1).astype(F32)
    up, dn = _shift_rows(main, prev_row, next_row)
    return up, main, dn


def _conv3(w_ref, b_ref, up, mid, dn):
    return b_ref[...] + w_ref[0:1, :] * up + w_ref[1:2, :] * mid + w_ref[2:3, :] * dn


def _ffn_tiles(S, F, rows=(2048, 1024, 512, 256, 128, 16)):
    return _pick(S, rows), _pick(F, (512, 256, 128))


def _sigmoid(x):
    return 0.5 * jnp.tanh(0.5 * x) + 0.5


def _conv_gate_fwd(u, conv_w, conv_b, name):
    S, F2 = u.shape
    F = F2 // 2
    tr, tn = _ffn_tiles(S, F)
    nj, ni = F // tn, S // tr

    def body(gm, gp, gn, vm, vp, vn, wg, wv, bg, bv, o_ref):
        i = pl.program_id(1)
        cg = _conv3(wg, bg, *_halo_rows(gm, gp, gn, i, ni))
        cv = _conv3(wv, bv, *_halo_rows(vm, vp, vn, i, ni))
        o_ref[...] = (cg * _sigmoid(cg) * cv).astype(o_ref.dtype)

    wspec = lambda off: pl.BlockSpec((3, tn), lambda j, i: (0, j + off))
    bspec = lambda off: pl.BlockSpec((1, tn), lambda j, i: (0, j + off))
    return pl.pallas_call(
        body, name=name, grid=(nj, ni),
        in_specs=[*_halo_specs(tr, tn, S, lambda j: j), *_halo_specs(tr, tn, S, lambda j: j + nj),
                  wspec(0), wspec(nj), bspec(0), bspec(nj)],
        out_specs=pl.BlockSpec((tr, tn), lambda j, i: (i, j)),
        out_shape=jax.ShapeDtypeStruct((S, F), BF16), compiler_params=_cp(("parallel", "parallel")),
    )(u, u, u, u, u, u, conv_w, conv_w, conv_b, conv_b)


def _conv_gate_bwd(u, da, conv_w, conv_b, name):
    S, F2 = u.shape
    F = F2 // 2
    tr, tn = _ffn_tiles(S, F)
    nj, ni = F // tn, S // tr

    def body(gm, gp, gn, vm, vp, vn, wg, wv, bg, bv, da_ref, dc_ref, dwg_ref, dwv_ref, dbg_ref, dbv_ref):
        i = pl.program_id(1)

        @pl.when(i == 0)
        def _():
            for r in (dwg_ref, dwv_ref, dbg_ref, dbv_ref):
                r[...] = jnp.zeros_like(r)

        g_rows = _halo_rows(gm, gp, gn, i, ni)
        v_rows = _halo_rows(vm, vp, vn, i, ni)
        cg = _conv3(wg, bg, *g_rows)
        cv = _conv3(wv, bv, *v_rows)
        sg = _sigmoid(cg)
        d = da_ref[...].astype(F32)
        dcv = d * (cg * sg)
        dcg = d * cv * (sg * (1.0 + cg * (1.0 - sg)))
        dc_ref[0] = dcg.astype(dc_ref.dtype)
        dc_ref[1] = dcv.astype(dc_ref.dtype)
        for t in range(3):
            dwg_ref[t:t + 1, :] += jnp.sum(dcg * g_rows[t], axis=0, keepdims=True)
            dwv_ref[t:t + 1, :] += jnp.sum(dcv * v_rows[t], axis=0, keepdims=True)
        dbg_ref[...] += jnp.sum(dcg, axis=0, keepdims=True)
        dbv_ref[...] += jnp.sum(dcv, axis=0, keepdims=True)

    wspec = lambda off: pl.BlockSpec((3, tn), lambda j, i: (0, j + off))
    bspec = lambda off: pl.BlockSpec((1, tn), lambda j, i: (0, j + off))
    tile = pl.BlockSpec((tr, tn), lambda j, i: (i, j))
    outs = pl.pallas_call(
        body, name=name, grid=(nj, ni),
        in_specs=[*_halo_specs(tr, tn, S, lambda j: j), *_halo_specs(tr, tn, S, lambda j: j + nj),
                  wspec(0), wspec(nj), bspec(0), bspec(nj), tile],
        out_specs=[pl.BlockSpec((2, tr, tn), lambda j, i: (0, i, j)), wspec(0), wspec(0), bspec(0), bspec(0)],
        out_shape=[jax.ShapeDtypeStruct((2, S, F), BF16),
                   jax.ShapeDtypeStruct((3, F), F32), jax.ShapeDtypeStruct((3, F), F32),
                   jax.ShapeDtypeStruct((1, F), F32), jax.ShapeDtypeStruct((1, F), F32)],
        compiler_params=_cp(("parallel", "arbitrary")),
    )(u, u, u, u, u, u, conv_w, conv_w, conv_b, conv_b, da)
    dc, dwg, dwv, dbg, dbv = outs
    return dc, jnp.concatenate([dwg, dwv], axis=1), jnp.concatenate([dbg, dbv], axis=1)


def _conv_transpose(dc, w, name):
    _, S, F = dc.shape
    tr, tn = _ffn_tiles(S, F, rows=(2048, 1024, 512, 256, 128, 16))
    nj, ni = F // tn, S // tr
    hb = tr // HALO
    last = S // HALO - 1

    def body(m, p, n, w_ref, o_ref):
        up, mid, dn = _halo_rows(m, p, n, pl.program_id(2), ni)
        o_ref[...] = (w_ref[0:1, :] * dn + w_ref[1:2, :] * mid + w_ref[2:3, :] * up).astype(o_ref.dtype)

    tile = pl.BlockSpec((None, tr, tn), lambda h, j, i: (h, i, j))
    return pl.pallas_call(
        body, name=name, grid=(2, nj, ni),
        in_specs=[tile,
                  pl.BlockSpec((None, HALO, tn), lambda h, j, i: (h, jnp.maximum(i * hb - 1, 0), j)),
                  pl.BlockSpec((None, HALO, tn), lambda h, j, i: (h, jnp.minimum((i + 1) * hb, last), j)),
                  pl.BlockSpec((3, tn), lambda h, j, i: (0, j + h * nj))],
        out_specs=tile, out_shape=jax.ShapeDtypeStruct((2, S, F), BF16),
        compiler_params=_cp(("parallel", "parallel", "parallel")),
    )(dc, dc, dc, w)


def _ffn_fwd(x, norm_g, w_up, conv_w, conv_b, w_down, tag, next_blocks=None):
    hn = _rmsnorm_fwd(x, norm_g, BF16, f"ffn_norm_{tag}")
    if next_blocks is None:
        u = _mm(hn, w_up, out_dtype=BF16, name=f"ffn_up_{tag}")
    else:
        u, over_ici = _mm(hn, w_up, out_dtype=BF16, name=f"ffn_up_{tag}", hosted=_gather_halves_ici(next_blocks))
    a = _conv_gate_fwd(u, conv_w, conv_b, f"ffn_gate_{tag}")
    if next_blocks is None:
        x_new, landed = _mm(a, w_down, name=f"ffn_down_{tag}", epi=_add_epi, epi_in=(x,)), None
    else:
        x_new, over_d2d = _mm(a, w_down, name=f"ffn_down_{tag}", epi=_add_epi, epi_in=(x,),
                              hosted=_gather_halves_d2d(over_ici))
        landed = (over_ici, over_d2d)
    return x_new, (x, hn, u, a), landed


def _ffn_bwd(dres, saved, norm_g, w_up, conv_w, conv_b, w_down, tag, reducer=None):
    x, hn, u, a = saved
    dres, dres_b = dres
    if reducer is None:
        da = _mm(dres_b, w_down, tb=True, out_dtype=BF16, name=f"ffn_da_{tag}")
    else:
        da, landed = _mm(dres_b, w_down, tb=True, out_dtype=BF16, name=f"ffn_da_{tag}", hosted=reducer.sibling())
    dw_down = _mm(a, dres_b, ta=True, name=f"ffn_dwdown_{tag}", pieces='row')
    dc, dconv_w, dconv_b = _conv_gate_bwd(u, da, conv_w, conv_b, f"ffn_gate_bwd_{tag}")
    du = _conv_transpose(dc, conv_w, f"ffn_convt_{tag}")
    if reducer is None:
        dhn = _mm(du, w_up, tb=True, name=f"ffn_dhn_{tag}")
        dw_up = _mm(hn, du, ta=True, name=f"ffn_dwup_{tag}", pieces='col')
        reduced = None
    else:
        dhn, landed = _mm(du, w_up, tb=True, name=f"ffn_dhn_{tag}", hosted=reducer.chips(landed))
        dw_up, landed = _mm(hn, du, ta=True, name=f"ffn_dwup_{tag}", pieces='col', hosted=reducer.swap(landed))
        reduced = reducer.finish(landed)
    dres, dg = _rmsnorm_bwd(x, norm_g, dhn, dres, f"ffn_norm_bwd_{tag}")
    return dres, (dg, dw_up, dconv_w, dconv_b, dw_down), reduced


ATTN_KEYS = 3 * ATTN_BLOCK


def _attn_window(i, S, reps):
    ks = pl.multiple_of(jnp.clip((i - 1) * ATTN_BLOCK, 0, S - ATTN_KEYS), ATTN_BLOCK)
    shape = (reps * ATTN_BLOCK, ATTN_KEYS)
    qpos = i * ATTN_BLOCK + lax.rem(lax.broadcasted_iota(jnp.int32, shape, 0), ATTN_BLOCK)
    kpos = ks + lax.broadcasted_iota(jnp.int32, shape, 1)
    arel = jnp.abs(kpos - qpos)
    return ks, arel.astype(F32), arel <= ATTN_BLOCK


ATTN_PAIRS = ATTN_GROUP // 2


def _stack_pairs(ref, kvh):
    c0 = kvh * ATTN_PAIRS * LANES
    return jnp.concatenate([ref[:, c0 + t * LANES:c0 + (t + 1) * LANES] for t in range(ATTN_PAIRS)], axis=0)


def _even_odd_operands(ref, rows, kvh):
    slab = ref[rows, (kvh // 2) * LANES:(kvh // 2 + 1) * LANES].astype(F32)
    other = pltpu.roll(slab, ATTN_DH, 1)
    low = lax.broadcasted_iota(jnp.int32, slab.shape, 1) < ATTN_DH
    lo_src, hi_src = (slab, other) if kvh % 2 == 0 else (other, slab)
    return jnp.where(low, lo_src, 0.0).astype(BF16), jnp.where(low, 0.0, hi_src).astype(BF16)


def _per_pair_rows(values):
    blk = lax.broadcasted_iota(jnp.int32, (ATTN_PAIRS * ATTN_BLOCK, 1), 0) // ATTN_BLOCK
    col = jnp.full(blk.shape, values[0], F32)
    for t in range(1, ATTN_PAIRS):
        col = jnp.where(blk == t, values[t], col)
    return col


def _attn_heads(kvh, parity, H):
    heads = [kvh * ATTN_GROUP + 2 * t + parity for t in range(ATTN_PAIRS)]
    return heads, [2.0 ** (-8.0 * (h + 1) / H) for h in heads]


def _attn_probs(q, k, slope, sink, arel, valid):
    s = lax.dot_general(q, k, (((1,), (1,)), ((), ())), preferred_element_type=F32) * (ATTN_DH ** -0.5)
    s = jnp.where(valid, s - slope * arel, -jnp.inf)
    m = jnp.maximum(jnp.max(s, axis=-1, keepdims=True), sink)
    p = jnp.exp(s - m)
    es = jnp.exp(sink - m)
    inv = 1.0 / (jnp.sum(p, axis=-1, keepdims=True) + es)
    return p * inv, es * inv


def _attn_specs(S, D):
    H = D // ATTN_DH
    KVW = (H // ATTN_GROUP) * ATTN_DH
    q_spec = pl.BlockSpec((ATTN_BLOCK, D), lambda i: (i, 0))
    k_spec = pl.BlockSpec((S, KVW), lambda i: (0, D // KVW))
    v_spec = pl.BlockSpec((S, KVW), lambda i: (0, D // KVW + 1))
    return H, KVW, q_spec, k_spec, v_spec


def _attn_fwd(qkv, sink, name):
    S = qkv.shape[0]
    D = qkv.shape[1] * ATTN_GROUP // (ATTN_GROUP + 2)
    H, KVW, q_spec, k_spec, v_spec = _attn_specs(S, D)

    def body(q_ref, k_ref, v_ref, sink_ref, o_ref):
        ks, arel, valid = _attn_window(pl.program_id(0), S, ATTN_PAIRS)
        rows = pl.ds(ks, ATTN_KEYS)
        for kvh in range(H // ATTN_GROUP):
            q = _stack_pairs(q_ref, kvh)
            out = None
            for parity, k, v in zip((0, 1), _even_odd_operands(k_ref, rows, kvh), _even_odd_operands(v_ref, rows, kvh)):
                heads, slopes = _attn_heads(kvh, parity, H)
                p, _ = _attn_probs(q, k, _per_pair_rows(slopes), _per_pair_rows([sink_ref[h] for h in heads]),
                                   arel, valid)
                part = jnp.dot(p.astype(BF16), v, preferred_element_type=F32)
                out = part if out is None else out + part
            c0 = kvh * ATTN_PAIRS * LANES
            for t in range(ATTN_PAIRS):
                o_ref[:, c0 + t * LANES:c0 + (t + 1) * LANES] = out[t * ATTN_BLOCK:(t + 1) * ATTN_BLOCK].astype(o_ref.dtype)

    return pl.pallas_call(
        body, name=name, grid=(S // ATTN_BLOCK,),
        in_specs=[q_spec, k_spec, v_spec, pl.BlockSpec(memory_space=pltpu.SMEM)],
        out_specs=q_spec, out_shape=jax.ShapeDtypeStruct((S, D), BF16),
        compiler_params=_cp(("parallel",)),
    )(qkv, qkv, qkv, sink)


def _attn_bwd(qkv, sink, do, name):
    S = qkv.shape[0]
    D = qkv.shape[1] * ATTN_GROUP // (ATTN_GROUP + 2)
    H, KVW, q_spec, k_spec, v_spec = _attn_specs(S, D)
    scale = ATTN_DH ** -0.5
    nt = (((1,), (1,)), ((), ()))
    tn = (((0,), (0,)), ((), ()))

    def body(q_ref, k_ref, v_ref, sink_ref, do_ref, dq_ref, dk_ref, dv_ref, ds_ref):
        @pl.when(pl.program_id(0) == 0)
        def _():
            dk_ref[...] = jnp.zeros_like(dk_ref)
            dv_ref[...] = jnp.zeros_like(dv_ref)
            ds_ref[...] = jnp.zeros_like(ds_ref)

        ks, arel, valid = _attn_window(pl.program_id(0), S, ATTN_PAIRS)
        rows = pl.ds(ks, ATTN_KEYS)
        low = lax.broadcasted_iota(jnp.int32, (ATTN_KEYS, LANES), 1) < ATTN_DH
        for kvh in range(H // ATTN_GROUP):
            q = _stack_pairs(q_ref, kvh)
            d_o = _stack_pairs(do_ref, kvh)
            dq = None
            dk_halves, dv_halves = [], []
            for parity, k, v in zip((0, 1), _even_odd_operands(k_ref, rows, kvh), _even_odd_operands(v_ref, rows, kvh)):
                heads, slopes = _attn_heads(kvh, parity, H)
                p, p_sink = _attn_probs(q, k, _per_pair_rows(slopes), _per_pair_rows([sink_ref[h] for h in heads]),
                                        arel, valid)
                dp = lax.dot_general(d_o, v, nt, preferred_element_type=F32)
                delta = jnp.sum(p * dp, axis=-1, keepdims=True)
                dsc = (p * (dp - delta)).astype(BF16)
                dsink = -p_sink * delta
                for t, h in enumerate(heads):
                    ds_ref[:, h:h + 1] += dsink[t * ATTN_BLOCK:(t + 1) * ATTN_BLOCK]
                part = jnp.dot(dsc, k, preferred_element_type=F32)
                dq = part if dq is None else dq + part
                dk_halves.append(lax.dot_general(dsc, q, tn, preferred_element_type=F32))
                dv_halves.append(lax.dot_general(p.astype(BF16), d_o, tn, preferred_element_type=F32))
            c0 = kvh * ATTN_PAIRS * LANES
            for t in range(ATTN_PAIRS):
                dq_ref[:, c0 + t * LANES:c0 + (t + 1) * LANES] = (
                    dq[t * ATTN_BLOCK:(t + 1) * ATTN_BLOCK] * scale).astype(dq_ref.dtype)
            slab = slice((kvh // 2) * LANES, (kvh // 2 + 1) * LANES)
            mine = low if kvh % 2 == 0 else jnp.logical_not(low)
            for ref, (even, odd), mult in ((dk_ref, dk_halves, scale), (dv_ref, dv_halves, 1.0)):
                both = jnp.where(low, even, odd)
                total = both + pltpu.roll(both, ATTN_DH, 1)
                ref[rows, slab] += jnp.where(mine, total * mult, 0.0)

    kv_out = pl.BlockSpec((S, KVW), lambda i: (0, 0))
    return pl.pallas_call(
        body, name=name, grid=(S // ATTN_BLOCK,),
        in_specs=[q_spec, k_spec, v_spec, pl.BlockSpec(memory_space=pltpu.SMEM), q_spec],
        out_specs=[q_spec, kv_out, kv_out, pl.BlockSpec((ATTN_BLOCK, H), lambda i: (0, 0))],
        out_shape=[jax.ShapeDtypeStruct((S, D), BF16), jax.ShapeDtypeStruct((S, KVW), F32),
                   jax.ShapeDtypeStruct((S, KVW), F32), jax.ShapeDtypeStruct((ATTN_BLOCK, H), F32)],
        compiler_params=_cp(("arbitrary",)),
    )(qkv, qkv, qkv, sink, do)


def _mm_hosting(exchange, a, b, **kw):
    if exchange is None:
        return _mm(a, b, **kw), None
    return _mm(a, b, hosted=exchange, **kw)


def _attn_layer_fwd(x, norm_g, w_qkv, w_o, sink, tag, next_blocks=None):
    hn = _rmsnorm_fwd(x, norm_g, BF16, f"attn_norm_{tag}")
    qkv, over_ici = _mm_hosting(None if next_blocks is None else _gather_halves_ici(next_blocks), hn, w_qkv,
                                out_dtype=BF16, name=f"attn_qkv_{tag}")
    o = _attn_fwd(qkv, sink, f"attn_core_{tag}")
    x_new, over_d2d = _mm_hosting(None if next_blocks is None else _gather_halves_d2d(over_ici), o, w_o,
                                  name=f"attn_out_{tag}", epi=_add_epi, epi_in=(x,))
    return x_new, (x, hn, qkv, o), None if next_blocks is None else (over_ici, over_d2d)


def _attn_layer_bwd(dres, saved, norm_g, w_qkv, w_o, sink, tag, reducer=None):
    x, hn, qkv, o = saved
    dres, dres_b = dres
    do, landed = _mm_hosting(reducer and reducer.sibling(), dres_b, w_o, tb=True, out_dtype=BF16, name=f"attn_do_{tag}")
    dw_o = _mm(o, dres_b, ta=True, name=f"attn_dwo_{tag}", pieces='row')
    dq, dk, dv, dsink = _attn_bwd(qkv, sink, do, f"attn_core_bwd_{tag}")
    dqkv = jnp.concatenate([dq, dk.astype(BF16), dv.astype(BF16)], axis=1)
    dhn, landed = _mm_hosting(reducer and reducer.chips(landed), dqkv, w_qkv, tb=True, name=f"attn_dhn_{tag}")
    dw_qkv, landed = _mm_hosting(reducer and reducer.swap(landed), hn, dqkv, ta=True, name=f"attn_dwqkv_{tag}",
                                 pieces='col')
    dres, dg = _rmsnorm_bwd(x, norm_g, dhn, dres, f"attn_norm_bwd_{tag}")
    return dres, (dg, dw_qkv, dw_o, jnp.sum(dsink, axis=0)), reducer and reducer.finish(landed)


MLA_W = 2 * LANES
MLA_DPAD = 2 * MLA_LORA + LANES
MLA_SCALE = (MLA_NOPE + MLA_ROPE) ** -0.5
MLA_TILES = (1024, 512, 256, 128)
MLA_ROW_GROUP = 256
LOG2E = math.log2(math.e)
LN2 = math.log(2.0)


def _rope_tables(S):
    half = MLA_ROPE // 2
    pos = jnp.arange(S, dtype=F32)
    inv = ROPE_THETA ** (-jnp.arange(half, dtype=F32) / half)
    ang = pos[:, None] * inv[None, :]
    cos, sin = jnp.cos(ang), jnp.sin(ang)
    z = jnp.zeros((S, LANES - 2 * half), F32)
    zh = jnp.zeros((S, half), F32)
    return (jnp.concatenate([cos, cos, z], axis=1), jnp.concatenate([-sin, zh, z], axis=1),
            jnp.concatenate([zh, sin, z], axis=1))


def _rope(t, ca, sb, sc):
    return t * ca + pltpu.roll(t, 96, 1) * sb + pltpu.roll(t, 32, 1) * sc


def _rope_t(d, ca, sb, sc):
    return d * ca + pltpu.roll(d * sb, 32, 1) + pltpu.roll(d * sc, 96, 1)


def _rms(xf, g):
    return (xf * lax.rsqrt(jnp.mean(xf * xf, axis=-1, keepdims=True) + RMS_EPS)) * g


def _mla_prep(d, qn, kvn, tabs, name):
    S = d.shape[0]
    tr = _pick(S, (256, 128, 8))
    L = MLA_LORA

    def body(d_ref, qn_ref, kvn_ref, ca, sb, sc, cq_ref, ckv_ref, kr_ref):
        cq_ref[...] = _rms(d_ref[:, :L], qn_ref[...]).astype(BF16)
        ckv_ref[...] = _rms(d_ref[:, L:2 * L], kvn_ref[...]).astype(BF16)
        kr_ref[...] = _rope(d_ref[:, 2 * L:], ca[...], sb[...], sc[...]).astype(BF16)

    row = lambda w: pl.BlockSpec((tr, w), lambda i: (i, 0))
    vec = pl.BlockSpec((1, L), lambda i: (0, 0))
    return pl.pallas_call(
        body, name=name, grid=(S // tr,),
        in_specs=[row(MLA_DPAD), vec, vec, row(LANES), row(LANES), row(LANES)],
        out_specs=[row(L), row(L), row(LANES)],
        out_shape=[jax.ShapeDtypeStruct((S, L), BF16), jax.ShapeDtypeStruct((S, L), BF16),
                   jax.ShapeDtypeStruct((S, LANES), BF16)],
        compiler_params=_cp(("parallel",)),
    )(d, qn, kvn, *tabs)


def _mla_prep_bwd(d, qn, kvn, tabs, dcq, dckv, dkr_h, name):
    S = d.shape[0]
    H = dkr_h.shape[0]
    tr = _pick(S, (256, 128, 8))
    L = MLA_LORA

    def body(d_ref, qn_ref, kvn_ref, ca, sb, sc, dcq_ref, dckv_ref, dkr_ref, dd_ref, dqn_ref, dkvn_ref):
        @pl.when(pl.program_id(0) == 0)
        def _():
            dqn_ref[...] = jnp.zeros_like(dqn_ref)
            dkvn_ref[...] = jnp.zeros_like(dkvn_ref)

        dx, dg = _rms_bwd_math(d_ref[:, :L], qn_ref[...], dcq_ref[...])
        dd_ref[:, :L] = dx.astype(BF16)
        dqn_ref[...] += dg
        dx, dg = _rms_bwd_math(d_ref[:, L:2 * L], kvn_ref[...], dckv_ref[...])
        dd_ref[:, L:2 * L] = dx.astype(BF16)
        dkvn_ref[...] += dg
        dkr = dkr_ref[0]
        for h in range(1, H):
            dkr = dkr + dkr_ref[h]
        dd_ref[:, 2 * L:] = _rope_t(dkr, ca[...], sb[...], sc[...]).astype(BF16)

    row = lambda w: pl.BlockSpec((tr, w), lambda i: (i, 0))
    vec = pl.BlockSpec((1, L), lambda i: (0, 0))
    return pl.pallas_call(
        body, name=name, grid=(S // tr,),
        in_specs=[row(MLA_DPAD), vec, vec, row(LANES), row(LANES), row(LANES), row(L), row(L),
                  pl.BlockSpec((H, tr, LANES), lambda i: (0, i, 0))],
        out_specs=[row(MLA_DPAD), vec, vec],
        out_shape=[jax.ShapeDtypeStruct((S, MLA_DPAD), BF16), jax.ShapeDtypeStruct((1, L), F32),
                   jax.ShapeDtypeStruct((1, L), F32)],
        compiler_params=_cp(("arbitrary",)),
    )(d, qn, kvn, *tabs, dcq, dckv, dkr_h)


def _heads_proj(a, w, out_dtype, name):
    S, K = a.shape
    H, _, n = w.shape
    tm = _pick(S, (1024, 512, 256, 128))
    return _gmm(a, w, grid=(S // tm, H, 1),
                a_spec=pl.BlockSpec((tm, K), lambda m, h, k: (m, 0)),
                b_spec=pl.BlockSpec((None, K, n), lambda m, h, k: (h, 0, 0)),
                o_spec=pl.BlockSpec((None, tm, n), lambda m, h, k: (h, m, 0)),
                out_shape=jax.ShapeDtypeStruct((H, S, n), out_dtype),
                dims=(((1,), (0,)), ((), ())), kax=2, acc_shape=(tm, n), name=name)


def _heads_proj_dx(dy, w, name):
    H, S, n = dy.shape
    K = w.shape[1]
    tm = _pick(S, (1024, 512, 256, 128))
    return _gmm(dy, w, grid=(S // tm, 1, H),
                a_spec=pl.BlockSpec((None, tm, n), lambda m, j, h: (h, m, 0)),
                b_spec=pl.BlockSpec((None, K, n), lambda m, j, h: (h, 0, 0)),
                o_spec=pl.BlockSpec((tm, K), lambda m, j, h: (m, 0)),
                out_shape=jax.ShapeDtypeStruct((S, K), F32),
                dims=(((1,), (1,)), ((), ())), kax=2, acc_shape=(tm, K), name=name)


def _heads_proj_dw(a, dy, name):
    S, K = a.shape
    H, _, n = dy.shape
    tk = _pick(S, (512, 256, 128))
    return _gmm(a, dy, grid=(H, 1, S // tk),
                a_spec=pl.BlockSpec((tk, K), lambda h, j, k: (k, 0)),
                b_spec=pl.BlockSpec((None, tk, n), lambda h, j, k: (h, k, 0)),
                o_spec=pl.BlockSpec((None, K, n), lambda h, j, k: (h, 0, 0)),
                out_shape=jax.ShapeDtypeStruct((H, K, n), F32),
                dims=(((0,), (0,)), ((), ())), kax=2, acc_shape=(K, n), name=name)


def _mla_rope_q(q_ext, tabs, bwd, name):
    H, S, _ = q_ext.shape
    tr = _pick(S, (512, 256, 128, 8))
    mult = 1.0 if bwd else MLA_SCALE * LOG2E

    def body(q_ref, ca, sb, sc, o_ref):
        o_ref[:, :LANES] = (q_ref[:, :LANES].astype(F32) * mult).astype(BF16)
        fn = _rope_t if bwd else _rope
        o_ref[:, LANES:] = (fn(q_ref[:, LANES:].astype(F32), ca[...], sb[...], sc[...]) * mult).astype(BF16)

    blk = pl.BlockSpec((None, tr, MLA_W), lambda i, h: (h, i, 0))
    tab = pl.BlockSpec((tr, LANES), lambda i, h: (i, 0))
    return pl.pallas_call(
        body, name=name, grid=(S // tr, H), in_specs=[blk, tab, tab, tab], out_specs=blk,
        out_shape=jax.ShapeDtypeStruct((H, S, MLA_W), BF16), compiler_params=_cp(("parallel", "parallel")),
    )(q_ext, *tabs)


def _col_to_row(col):
    n = col.shape[0]
    eye = lax.broadcasted_iota(jnp.int32, (n, n), 0) == lax.broadcasted_iota(jnp.int32, (n, n), 1)
    return jnp.sum(jnp.where(eye, col, 0.0), axis=0, keepdims=True)


def _mla_flash_fwd(q, kv, kr, name, tq=None, tk=None, unroll=1, splits=None):
    H, S, _ = q.shape
    tq = tq or _pick(S, (2048,) + MLA_TILES)
    tk = tk or _pick(S, (2048,) + MLA_TILES)
    splits = splits or max(1, tq // MLA_ROW_GROUP)

    def body(q_ref, kv_ref, kr_ref, o_ref, lse_ref, kbuf, vbuf):
        @pl.when(pl.program_id(1) == 0)
        def _():
            kbuf[:, :LANES] = kv_ref[:, :LANES]
            kbuf[:, LANES:] = kr_ref[...]
            vbuf[:, :LANES] = kv_ref[:, LANES:]
            vbuf[:, LANES:] = jnp.ones((S, LANES), BF16)

        sub = tq // splits
        qs = [q_ref[g * sub:(g + 1) * sub, :] for g in range(splits)]

        def step(c, carry):
            rows = pl.ds(pl.multiple_of(c * tk, tk), tk)
            k, v = kbuf[rows, :], vbuf[rows, :]
            out = []
            for qv, (m, acc) in zip(qs, carry):
                s = lax.dot_general(qv, k, (((1,), (1,)), ((), ())), preferred_element_type=F32)
                m_new = jnp.maximum(m, jnp.max(s, axis=-1, keepdims=True))
                p = jnp.exp2(s - m_new).astype(BF16)
                out.append((m_new, jnp.exp2(m - m_new) * acc + jnp.dot(p, v, preferred_element_type=F32)))
            return tuple(out)

        init = tuple((jnp.full((sub, 1), -jnp.inf, F32), jnp.zeros((sub, MLA_W), F32)) for _ in range(splits))
        for g, (m, acc) in enumerate(lax.fori_loop(0, S // tk, step, init, unroll=unroll)):
            l = acc[:, LANES:LANES + 1]
            o_ref[g * sub:(g + 1) * sub, :] = (acc[:, :LANES] / l).astype(o_ref.dtype)
            lse_ref[:, g * sub:(g + 1) * sub] = _col_to_row(m + jnp.log2(l))

    return pl.pallas_call(
        body, name=name, grid=(H, S // tq),
        in_specs=[pl.BlockSpec((None, tq, MLA_W), lambda h, i: (h, i, 0)),
                  pl.BlockSpec((None, S, MLA_W), lambda h, i: (h, 0, 0)),
                  pl.BlockSpec((S, LANES), lambda h, i: (0, 0))],
        out_specs=[pl.BlockSpec((tq, MLA_V), lambda h, i: (i, h)),
                   pl.BlockSpec((None, 1, tq), lambda h, i: (h, 0, i))],
        out_shape=[jax.ShapeDtypeStruct((S, H * MLA_V), BF16), jax.ShapeDtypeStruct((H, 1, S), F32)],
        scratch_shapes=[pltpu.VMEM((S, MLA_W), BF16), pltpu.VMEM((S, MLA_W), BF16)],
        compiler_params=_cp(("parallel", "arbitrary")),
    )(q, kv, kr)


def _mla_delta(o, do, H, name):
    S = o.shape[0]
    tq = _pick(S, (512, 256, 128))

    def body(o_ref, do_ref, d_ref):
        prod = o_ref[...].astype(F32) * do_ref[...].astype(F32)
        d_ref[...] = _col_to_row(jnp.sum(prod, axis=-1, keepdims=True))

    blk = pl.BlockSpec((tq, MLA_V), lambda i, h: (i, h))
    return pl.pallas_call(
        body, name=name, grid=(S // tq, H), in_specs=[blk, blk],
        out_specs=pl.BlockSpec((None, 1, tq), lambda i, h: (h, 0, i)),
        out_shape=jax.ShapeDtypeStruct((H, 1, S), F32), compiler_params=_cp(("parallel", "parallel")),
    )(o, do)


def _mla_flash_bwd(q, kv, kr, do, lse, delta, name, tq=None, tkv=None, unroll=1):
    H, S, _ = q.shape
    tq = tq or _pick(S, (2048,) + MLA_TILES)
    tkv = tkv or _pick(S, MLA_TILES)

    def body(q_ref, kv_ref, kr_ref, do_ref, lse_ref, dl_ref, dq_ref, dkv_ref, dkr_ref):
        @pl.when(pl.program_id(1) == 0)
        def _():
            dq_ref[...] = jnp.zeros_like(dq_ref)

        v = kv_ref[:, LANES:]
        k = jnp.concatenate([kv_ref[:, :LANES], kr_ref[...]], axis=1)

        def step(c, carry):
            dk, dv = carry
            start = pl.multiple_of(c * tq, tq)
            rows = pl.ds(start, tq)
            qv = q_ref[rows, :]
            d_o = do_ref[rows, :]
            s_t = lax.dot_general(k, qv, (((1,), (1,)), ((), ())), preferred_element_type=F32)
            p_t = jnp.exp2(s_t - lse_ref[:, rows])
            dv = dv + jnp.dot(p_t.astype(BF16), d_o, preferred_element_type=F32)
            dp_t = lax.dot_general(v, d_o, (((1,), (1,)), ((), ())), preferred_element_type=F32)
            ds_t = (p_t * (dp_t - dl_ref[:, rows])).astype(BF16)
            dk = dk + jnp.dot(ds_t, qv, preferred_element_type=F32)
            dq_ref[rows, :] += lax.dot_general(ds_t, k, (((0,), (0,)), ((), ())),
                                               preferred_element_type=F32) * MLA_SCALE
            return dk, dv

        dk, dv = lax.fori_loop(0, S // tq, step, (jnp.zeros((tkv, MLA_W), F32), jnp.zeros((tkv, MLA_V), F32)),
                               unroll=unroll)
        dkv_ref[:, :LANES] = (dk[:, :LANES] * LN2).astype(BF16)
        dkv_ref[:, LANES:] = dv.astype(BF16)
        dkr_ref[...] = dk[:, LANES:] * LN2

    stat = pl.BlockSpec((None, 1, S), lambda h, j: (h, 0, 0))
    return pl.pallas_call(
        body, name=name, grid=(H, S // tkv),
        in_specs=[pl.BlockSpec((None, S, MLA_W), lambda h, j: (h, 0, 0)),
                  pl.BlockSpec((None, tkv, MLA_W), lambda h, j: (h, j, 0)),
                  pl.BlockSpec((tkv, LANES), lambda h, j: (j, 0)),
                  pl.BlockSpec((S, MLA_V), lambda h, j: (0, h)), stat, stat],
        out_specs=[pl.BlockSpec((None, S, MLA_W), lambda h, j: (h, 0, 0)),
                   pl.BlockSpec((None, tkv, MLA_W), lambda h, j: (h, j, 0)),
                   pl.BlockSpec((None, tkv, LANES), lambda h, j: (h, j, 0))],
        out_shape=[jax.ShapeDtypeStruct((H, S, MLA_W), F32), jax.ShapeDtypeStruct((H, S, MLA_W), BF16),
                   jax.ShapeDtypeStruct((H, S, LANES), F32)],
        compiler_params=_cp(("parallel", "arbitrary")),
    )(q, kv, kr, do, lse, delta)


def _mla_weights(w_dqkv, w_uq, w_ukv):
    H = MLA_HEADS
    wd = jnp.pad(w_dqkv, ((0, 0), (0, MLA_DPAD - w_dqkv.shape[1])))
    wq = w_uq.reshape(MLA_LORA, H, MLA_NOPE + MLA_ROPE)
    wq = jnp.pad(wq, ((0, 0), (0, 0), (0, MLA_W - wq.shape[2]))).transpose(1, 0, 2)
    wkv = w_ukv.reshape(MLA_LORA, H, MLA_NOPE + MLA_V).transpose(1, 0, 2)
    return wd, wq, wkv


def _mla_layer_fwd(x, norm_g, wd, wq, wkv, w_o, qn, kvn, tag):
    S = x.shape[0]
    tabs = _rope_tables(S)
    hn = _rmsnorm_fwd(x, norm_g, BF16, f"mla_norm_{tag}")
    d = _mm(hn, wd, name=f"mla_down_{tag}")
    cq, ckv, kr = _mla_prep(d, qn, kvn, tabs, f"mla_prep_{tag}")
    q = _mla_rope_q(_heads_proj(cq, wq, F32, f"mla_uq_{tag}"), tabs, False, f"mla_ropeq_{tag}")
    kv = _heads_proj(ckv, wkv, BF16, f"mla_ukv_{tag}")
    o, lse = _mla_flash_fwd(q, kv, kr, f"mla_flash_{tag}")
    x_new = _mm(o, w_o, name=f"mla_out_{tag}", epi=_add_epi, epi_in=(x,))
    return x_new, (x, hn, d, cq, ckv, kr, q, kv, o, lse)


def _mla_layer_bwd(dres, saved, norm_g, wd, wq, wkv, w_o, qn, kvn, tag):
    x, hn, d, cq, ckv, kr, q, kv, o, lse = saved
    S = x.shape[0]
    H = MLA_HEADS
    tabs = _rope_tables(S)
    dres, dres_b = dres
    do = _mm(dres_b, w_o, tb=True, out_dtype=BF16, name=f"mla_do_{tag}")
    dw_o = _mm(o, dres_b, ta=True, name=f"mla_dwo_{tag}", pieces='row')
    delta = _mla_delta(o, do, H, f"mla_delta_{tag}")
    dq, dkv, dkr_h = _mla_flash_bwd(q, kv, kr, do, lse, delta, f"mla_flash_bwd_{tag}")
    dq_ext = _mla_rope_q(dq, tabs, True, f"mla_ropeq_bwd_{tag}")
    dwq = _heads_proj_dw(cq, dq_ext, f"mla_dwuq_{tag}")
    dcq = _heads_proj_dx(dq_ext, wq, f"mla_dcq_{tag}")
    dwkv = _heads_proj_dw(ckv, dkv, f"mla_dwukv_{tag}")
    dckv = _heads_proj_dx(dkv, wkv, f"mla_dckv_{tag}")
    dd, dqn, dkvn = _mla_prep_bwd(d, qn, kvn, tabs, dcq, dckv, dkr_h, f"mla_prep_bwd_{tag}")
    dhn = _mm(dd, wd, tb=True, name=f"mla_dhn_{tag}")
    dwd = _mm(hn, dd, ta=True, name=f"mla_dwd_{tag}")
    dres, dg = _rmsnorm_bwd(x, norm_g, dhn, dres, f"mla_norm_bwd_{tag}")
    dw_dqkv = dwd[:, :2 * MLA_LORA + MLA_ROPE]
    dw_uq = dwq.transpose(1, 0, 2)[:, :, :MLA_NOPE + MLA_ROPE].reshape(MLA_LORA, -1)
    dw_ukv = dwkv.transpose(1, 0, 2).reshape(MLA_LORA, -1)
    return dres, (dg, dw_dqkv, dqn, dkvn, dw_uq, dw_ukv, dw_o)


S5_CB = LANES
S5_SB = (S5_CB // SSM_GROUP_CH) * SSM_STATE
S5_ROWS = 2048


def _s5_disc(a_re, a_im, ls, b_re, b_im):
    step = jnp.exp(ls)
    mag = jnp.exp(step * a_re)
    lb_re = mag * jnp.cos(step * a_im)
    lb_im = mag * jnp.sin(step * a_im)
    n_re, n_im = lb_re - 1.0, lb_im
    den = a_re * a_re + a_im * a_im
    coef_re = (n_re * a_re + n_im * a_im) / den
    coef_im = (n_im * a_re - n_re * a_im) / den
    return lb_re, lb_im, coef_re * b_re - coef_im * b_im, coef_re * b_im + coef_im * b_re


def _s5_disc_fwd(a_re, a_im, ls, b_re, b_im, name):
    GN = a_re.shape[-1]

    def body(ar, ai, l, br, bi, o_lr, o_li, o_br, o_bi):
        for o, v in zip((o_lr, o_li, o_br, o_bi), _s5_disc(ar[...], ai[...], l[...], br[...], bi[...])):
            o[...] = v

    vec = pl.BlockSpec((None, 1, GN), lambda d: (d, 0, 0))
    mat = pl.BlockSpec((None, SSM_GROUP_CH, GN), lambda d: (d, 0, 0))
    sv = jax.ShapeDtypeStruct(a_re.shape, F32)
    sm = jax.ShapeDtypeStruct(b_re.shape, F32)
    return pl.pallas_call(body, name=name, grid=(2,), in_specs=[vec, vec, vec, mat, mat],
                          out_specs=[vec, vec, mat, mat], out_shape=[sv, sv, sm, sm],
                          compiler_params=_cp(("parallel",)))(a_re, a_im, ls, b_re, b_im)


def _s5_disc_bwd(a_re, a_im, ls, b_re, b_im, d_lr, d_li, d_br, d_bi, name):
    GN = a_re.shape[-1]

    def body(ar, ai, l, br, bi, g_lr, g_li, g_br, g_bi, o_ar, o_ai, o_l, o_br, o_bi):
        _, vjp = jax.vjp(_s5_disc, ar[...], ai[...], l[...], br[...], bi[...])
        for o, v in zip((o_ar, o_ai, o_l, o_br, o_bi), vjp((g_lr[...], g_li[...], g_br[...], g_bi[...]))):
            o[...] = v

    vec = pl.BlockSpec((None, 1, GN), lambda d: (d, 0, 0))
    mat = pl.BlockSpec((None, SSM_GROUP_CH, GN), lambda d: (d, 0, 0))
    sv = jax.ShapeDtypeStruct(a_re.shape, F32)
    sm = jax.ShapeDtypeStruct(b_re.shape, F32)
    return pl.pallas_call(body, name=name, grid=(2,), in_specs=[vec, vec, vec, mat, mat, vec, vec, mat, mat],
                          out_specs=[vec, vec, vec, mat, mat], out_shape=[sv, sv, sv, sm, sm],
                          compiler_params=_cp(("parallel",)))(a_re, a_im, ls, b_re, b_im, d_lr, d_li, d_br, d_bi)


def _cmul(ar, ai, br, bi):
    return ar * br - ai * bi, ar * bi + ai * br


def _segment_carries(lr, li, er, ei, n_steps, reverse):
    pr, pi = lr, li
    for _ in range(int(math.log2(n_steps))):
        pr, pi = _cmul(pr, pi, pr, pi)
    row = lax.broadcasted_iota(jnp.int32, er.shape, 0)
    edge = (SSM_SEGMENTS - 1) if reverse else 0
    shift = (SSM_SEGMENTS - 1) if reverse else 1
    cr = jnp.zeros_like(er)
    ci = jnp.zeros_like(ei)
    for _ in range(SSM_SEGMENTS - 1):
        tr_, ti_ = _cmul(pr, pi, cr, ci)
        cr = jnp.where(row == edge, 0.0, pltpu.roll(tr_ + er, shift, 0))
        ci = jnp.where(row == edge, 0.0, pltpu.roll(ti_ + ei, shift, 0))
    return cr, ci


def _s5_geometry(S, D):
    assert S % SSM_SEGMENTS == 0 and D % S5_CB == 0
    n_steps = S // SSM_SEGMENTS
    assert n_steps & (n_steps - 1) == 0, "segment length must be a power of two"
    rows = min(S5_ROWS, S)
    return n_steps, rows, S // rows, D // S5_CB


def _s5_scan(u, b_re, b_im, c_re, c_im, lam_re, lam_im, ends, descending, name):
    S, D = u.shape
    n_steps, rows, nch, ncb = _s5_geometry(S, D)
    full = ends is not None
    GN = ncb * S5_SB

    def body(*refs):
        if full:
            (u_ref, br_ref, bi_ref, cr_ref, ci_ref, lr_ref, li_ref, er_ref, ei_ref,
             xr_ref, xi_ref, y_ref, st_r, st_i, buf_r, buf_i) = refs
        else:
            u_ref, br_ref, bi_ref, lr_ref, li_ref, er_ref, ei_ref, st_r, st_i, buf_r, buf_i = refs
        lr = jnp.broadcast_to(lr_ref[...], (SSM_SEGMENTS, S5_SB))
        li = jnp.broadcast_to(li_ref[...], (SSM_SEGMENTS, S5_SB))

        @pl.when(pl.program_id(1) == 0)
        def _():
            if full:
                st_r[...], st_i[...] = _segment_carries(lr, li, er_ref[...], ei_ref[...], n_steps, descending)
            else:
                st_r[...] = jnp.zeros_like(st_r)
                st_i[...] = jnp.zeros_like(st_i)

        ub = u_ref[...].astype(BF16)
        buf_r[...] = jnp.dot(ub, br_ref[...], preferred_element_type=F32)
        buf_i[...] = jnp.dot(ub, bi_ref[...], preferred_element_type=F32)

        n_it = rows // SSM_SEGMENTS

        def step(i, carry):
            sr, si = carry
            i = n_it - 1 - i if descending else i
            r = pl.ds(pl.multiple_of(i * SSM_SEGMENTS, SSM_SEGMENTS), SSM_SEGMENTS)
            if full:
                xr_ref[r, :] = sr
                xi_ref[r, :] = si
            nr = lr * sr - li * si + buf_r[r, :]
            ni = lr * si + li * sr + buf_i[r, :]
            if full:
                buf_r[r, :] = nr
                buf_i[r, :] = ni
            return nr, ni

        sr, si = lax.fori_loop(0, n_it, step, (st_r[...], st_i[...]))
        st_r[...] = sr
        st_i[...] = si
        if full:
            y_ref[...] = (jnp.dot(buf_r[...].astype(BF16), cr_ref[...], preferred_element_type=F32)
                          - jnp.dot(buf_i[...].astype(BF16), ci_ref[...], preferred_element_type=F32))
        else:
            er_ref[...] = sr
            ei_ref[...] = si

    chunk = (lambda c: nch - 1 - c) if descending else (lambda c: c)
    u_spec = pl.BlockSpec((rows, S5_CB), lambda b, c: (chunk(c), b))
    bmat = pl.BlockSpec((None, S5_CB, S5_SB), lambda b, c: (b, 0, 0))
    cmat = pl.BlockSpec((None, S5_SB, S5_CB), lambda b, c: (b, 0, 0))
    lvec = pl.BlockSpec((1, S5_SB), lambda b, c: (0, b))
    evec = pl.BlockSpec((SSM_SEGMENTS, S5_SB), lambda b, c: (0, b))
    xblk = pl.BlockSpec((rows, S5_SB), lambda b, c: (chunk(c), b))
    scratch = [pltpu.VMEM((SSM_SEGMENTS, S5_SB), F32)] * 2 + [pltpu.VMEM((rows, S5_SB), F32)] * 2
    e_shape = jax.ShapeDtypeStruct((SSM_SEGMENTS, GN), F32)
    if full:
        x_shape = jax.ShapeDtypeStruct((S, GN), F32)
        return pl.pallas_call(
            body, name=name, grid=(ncb, nch),
            in_specs=[u_spec, bmat, bmat, cmat, cmat, lvec, lvec, evec, evec],
            out_specs=[xblk, xblk, u_spec], out_shape=[x_shape, x_shape, jax.ShapeDtypeStruct((S, D), F32)],
            scratch_shapes=scratch, compiler_params=_cp(("parallel", "arbitrary")),
        )(u, b_re, b_im, c_re, c_im, lam_re, lam_im, *ends)
    return pl.pallas_call(
        body, name=name, grid=(ncb, nch), in_specs=[u_spec, bmat, bmat, lvec, lvec],
        out_specs=[evec, evec], out_shape=[e_shape, e_shape],
        scratch_shapes=scratch, compiler_params=_cp(("parallel", "arbitrary")),
    )(u, b_re, b_im, lam_re, lam_im)


def _s5_scan_bwd(dy, u, xp, b_re, b_im, c_re, c_im, lam_re, lam_im, starts, descending, name):
    S, D = dy.shape
    n_steps, rows, nch, ncb = _s5_geometry(S, D)
    full = starts is not None
    GN = ncb * S5_SB
    nt = (((1,), (1,)), ((), ()))
    tn = (((0,), (0,)), ((), ()))

    def body(*refs):
        if full:
            (dy_ref, u_ref, xr_ref, xi_ref, br_ref, bi_ref, cr_ref, ci_ref, lr_ref, li_ref, gr_ref, gi_ref,
             du_ref, dbr_ref, dbi_ref, dcr_ref, dci_ref, dlr_ref, dli_ref, st_r, st_i, buf_r, buf_i) = refs
        else:
            dy_ref, cr_ref, ci_ref, lr_ref, li_ref, gr_ref, gi_ref, st_r, st_i, buf_r, buf_i = refs
        lr = jnp.broadcast_to(lr_ref[...], (SSM_SEGMENTS, S5_SB))
        li = jnp.broadcast_to(li_ref[...], (SSM_SEGMENTS, S5_SB))

        @pl.when(pl.program_id(1) == 0)
        def _():
            if full:
                st_r[...], st_i[...] = _segment_carries(lr, -li, gr_ref[...], gi_ref[...], n_steps, descending)
                for r in (dbr_ref, dbi_ref, dcr_ref, dci_ref, dlr_ref, dli_ref):
                    r[...] = jnp.zeros_like(r)
            else:
                st_r[...] = jnp.zeros_like(st_r)
                st_i[...] = jnp.zeros_like(st_i)

        dyb = dy_ref[...].astype(BF16)
        buf_r[...] = lax.dot_general(dyb, cr_ref[...], nt, preferred_element_type=F32)
        buf_i[...] = -lax.dot_general(dyb, ci_ref[...], nt, preferred_element_type=F32)
        n_it = rows // SSM_SEGMENTS

        def step(j, carry):
            gr, gi = carry
            j = n_it - 1 - j if descending else j
            r = pl.ds(pl.multiple_of(j * SSM_SEGMENTS, SSM_SEGMENTS), SSM_SEGMENTS)
            nr = lr * gr + li * gi + buf_r[r, :]
            ni = lr * gi - li * gr + buf_i[r, :]
            if full:
                buf_r[r, :] = nr
                buf_i[r, :] = ni
            return nr, ni

        gr, gi = lax.fori_loop(0, n_it, step, (st_r[...], st_i[...]))
        st_r[...] = gr
        st_i[...] = gi
        if not full:
            gr_ref[...] = gr
            gi_ref[...] = gi
            return
        g_r, g_i = buf_r[...], buf_i[...]
        xr, xi = xr_ref[...], xi_ref[...]
        dlr_ref[...] += jnp.sum(g_r * xr + g_i * xi, axis=0, keepdims=True)
        dli_ref[...] += jnp.sum(g_i * xr - g_r * xi, axis=0, keepdims=True)
        ub = u_ref[...].astype(BF16)
        gb_r, gb_i = g_r.astype(BF16), g_i.astype(BF16)
        du_ref[...] = (lax.dot_general(gb_r, br_ref[...], nt, preferred_element_type=F32)
                       + lax.dot_general(gb_i, bi_ref[...], nt, preferred_element_type=F32))
        dbr_ref[...] += lax.dot_general(ub, gb_r, tn, preferred_element_type=F32)
        dbi_ref[...] += lax.dot_general(ub, gb_i, tn, preferred_element_type=F32)
        lr_, li_ = lr_ref[...], li_ref[...]
        x_r = lr_ * xr - li_ * xi + jnp.dot(ub, br_ref[...], preferred_element_type=F32)
        x_i = lr_ * xi + li_ * xr + jnp.dot(ub, bi_ref[...], preferred_element_type=F32)
        dcr_ref[...] += lax.dot_general(x_r.astype(BF16), dyb, tn, preferred_element_type=F32)
        dci_ref[...] -= lax.dot_general(x_i.astype(BF16), dyb, tn, preferred_element_type=F32)

    rev = (lambda c: nch - 1 - c) if descending else (lambda c: c)
    u_spec = pl.BlockSpec((rows, S5_CB), lambda b, c: (rev(c), b))
    bmat = pl.BlockSpec((None, S5_CB, S5_SB), lambda b, c: (b, 0, 0))
    cmat = pl.BlockSpec((None, S5_SB, S5_CB), lambda b, c: (b, 0, 0))
    lvec = pl.BlockSpec((1, S5_SB), lambda b, c: (0, b))
    evec = pl.BlockSpec((SSM_SEGMENTS, S5_SB), lambda b, c: (0, b))
    xblk = pl.BlockSpec((rows, S5_SB), lambda b, c: (rev(c), b))
    scratch = [pltpu.VMEM((SSM_SEGMENTS, S5_SB), F32)] * 2 + [pltpu.VMEM((rows, S5_SB), F32)] * 2
    e_shape = jax.ShapeDtypeStruct((SSM_SEGMENTS, GN), F32)
    if full:
        return pl.pallas_call(
            body, name=name, grid=(ncb, nch),
            in_specs=[u_spec, u_spec, xblk, xblk, bmat, bmat, cmat, cmat, lvec, lvec, evec, evec],
            out_specs=[u_spec, bmat, bmat, cmat, cmat, lvec, lvec],
            out_shape=[jax.ShapeDtypeStruct((S, D), F32), jax.ShapeDtypeStruct(b_re.shape, F32),
                       jax.ShapeDtypeStruct(b_re.shape, F32), jax.ShapeDtypeStruct(c_re.shape, F32),
                       jax.ShapeDtypeStruct(c_re.shape, F32), jax.ShapeDtypeStruct((1, GN), F32),
                       jax.ShapeDtypeStruct((1, GN), F32)],
            scratch_shapes=scratch, compiler_params=_cp(("parallel", "arbitrary")),
        )(dy, u, *xp, b_re, b_im, c_re, c_im, lam_re, lam_im, *starts)
    return pl.pallas_call(
        body, name=name, grid=(ncb, nch), in_specs=[u_spec, cmat, cmat, lvec, lvec],
        out_specs=[evec, evec], out_shape=[e_shape, e_shape],
        scratch_shapes=scratch, compiler_params=_cp(("parallel", "arbitrary")),
    )(dy, c_re, c_im, lam_re, lam_im)


def _s5_perm(t):
    S, D = t.shape
    return t.reshape(SSM_SEGMENTS, S // SSM_SEGMENTS, D).transpose(1, 0, 2).reshape(S, D)


def _s5_unperm(t):
    S, D = t.shape
    return t.reshape(S // SSM_SEGMENTS, SSM_SEGMENTS, D).transpose(1, 0, 2).reshape(S, D)


def _s5_blockdiag_b(bb, ncb):
    gpb = S5_CB // SSM_GROUP_CH
    t = bb.reshape(SSM_GROUP_CH, ncb, gpb, SSM_STATE)
    return jnp.einsum('cbgn,gh->bgchn', t, jnp.eye(gpb, dtype=bb.dtype)).reshape(ncb, S5_CB, S5_SB)


def _s5_blockdiag_b_t(dblk):
    ncb = dblk.shape[0]
    gpb = S5_CB // SSM_GROUP_CH
    t = dblk.reshape(ncb, gpb, SSM_GROUP_CH, gpb, SSM_STATE)
    return jnp.einsum('bgchn,gh->cbgn', t, jnp.eye(gpb, dtype=dblk.dtype)).reshape(SSM_GROUP_CH, -1)


def _s5_blockdiag_c(c, ncb):
    gpb = S5_CB // SSM_GROUP_CH
    t = c.reshape(ncb, gpb, SSM_GROUP_CH, SSM_STATE)
    return jnp.einsum('bgcn,gh->bgnhc', t, jnp.eye(gpb, dtype=c.dtype)).reshape(ncb, S5_SB, S5_CB)


def _s5_blockdiag_c_t(dblk):
    ncb = dblk.shape[0]
    gpb = S5_CB // SSM_GROUP_CH
    t = dblk.reshape(ncb, gpb, SSM_STATE, gpb, SSM_GROUP_CH)
    return jnp.einsum('bgnhc,gh->bgcn', t, jnp.eye(gpb, dtype=dblk.dtype)).reshape(-1, SSM_GROUP_CH, SSM_STATE)


_GELU_C = math.sqrt(2.0 / math.pi)


def _gelu(y):
    return y * (0.5 * (1.0 + jnp.tanh(_GELU_C * (y + 0.044715 * (y * y * y)))))


def _gelu_grad(y):
    t = jnp.tanh(_GELU_C * (y + 0.044715 * (y * y * y)))
    return 0.5 * (1.0 + t) + 0.5 * y * (1.0 - t * t) * (_GELU_C * (1.0 + 3.0 * 0.044715 * y * y))


def _rowwise(fn, ins, outs, name, acc=()):
    S, D = next(a.shape for a in ins if a.shape[0] != 1)
    tr = _pick(S, (256, 128, 8))
    row = pl.BlockSpec((tr, D), lambda i: (i, 0))
    vec = pl.BlockSpec((1, D), lambda i: (0, 0))
    n_in = len(ins)

    def body(*refs):
        res = fn(*[r[...] for r in refs[:n_in]])
        for k, (o, v) in enumerate(zip(refs[n_in:], res)):
            if k in acc:
                @pl.when(pl.program_id(0) == 0)
                def _():
                    o[...] = jnp.zeros_like(o)
                o[...] += jnp.sum(v, axis=0, keepdims=True)
            else:
                o[...] = v.astype(o.dtype)

    return pl.pallas_call(
        body, name=name, grid=(S // tr,), in_specs=[vec if a.shape[0] == 1 else row for a in ins],
        out_specs=[vec if k in acc else row for k in range(len(outs))],
        out_shape=[jax.ShapeDtypeStruct((1, D) if k in acc else (S, D), dt) for k, dt in enumerate(outs)],
        compiler_params=_cp(("arbitrary",) if acc else ("parallel",)),
    )(*ins)


def _s5_params(p):
    G, N = p["a_re"].shape[1:]
    vec = lambda a: a.reshape(2, 1, G * N)
    ls = jnp.broadcast_to(p["log_step"][:, :, None], (2, G, N))
    bt = lambda b: b.transpose(0, 3, 1, 2).reshape(2, SSM_GROUP_CH, G * N)
    return vec(p["a_re"]), vec(p["a_im"]), vec(ls), bt(p["b_re"]), bt(p["b_im"])


def _s5_layer_fwd(x, norm_g, p, w_glu, tag):
    S, D = x.shape
    ncb = D // S5_CB
    xp = _s5_perm(x)
    hn = _rmsnorm_fwd(xp, norm_g, F32, f"s5_norm_{tag}")
    raw = _s5_params(p)
    lam_r, lam_i, bb_r, bb_i = _s5_disc_fwd(*raw, f"s5_disc_{tag}")
    dirs = []
    ys = []
    for dirn in range(2):
        mats = (_s5_blockdiag_b(bb_r[dirn], ncb).astype(BF16), _s5_blockdiag_b(bb_i[dirn], ncb).astype(BF16),
                _s5_blockdiag_c(p["c_re"][dirn], ncb).astype(BF16), _s5_blockdiag_c(p["c_im"][dirn], ncb).astype(BF16))
        lam = (lam_r[dirn], lam_i[dirn])
        ends = _s5_scan(hn, mats[0], mats[1], None, None, *lam, None, dirn == 1, f"s5_ends_{tag}_{dirn}")
        xr, xi, y = _s5_scan(hn, *mats, *lam, ends, dirn == 1, f"s5_scan_{tag}_{dirn}")
        dirs.append(((xr, xi), mats, lam))
        ys.append(y)
    ytot, z = _rowwise(lambda u, d, a, b: ((lambda y: (y, _gelu(y)))(d * u + a + b)),
                       [hn, p["d"], ys[0], ys[1]], [F32, BF16], f"s5_gelu_{tag}")
    t = _mm(z, w_glu, name=f"s5_glu_{tag}", epi=lambda r, b: r + b, epi_in=(p["b_glu"],))
    (x_new,) = _rowwise(lambda xx, zz, tt: (xx + zz.astype(F32) * jax.nn.sigmoid(tt),),
                        [xp, z, t], [F32], f"s5_out_{tag}")
    return _s5_unperm(x_new), (xp, hn, raw, dirs, ytot, z, t)


def _s5_layer_bwd(dres, saved, norm_g, p, w_glu, tag):
    x, hn, raw, dirs, ytot, z, t = saved
    S, D = x.shape
    G, N = p["a_re"].shape[1:]
    dres = _s5_perm(dres[0])

    def glu_bwd(do, zz, tt):
        sg = jax.nn.sigmoid(tt)
        dt = do * zz.astype(F32) * (sg * (1.0 - sg))
        return dt, do * sg, dt

    dt, dzd, db_glu = _rowwise(glu_bwd, [dres, z, t], [BF16, F32, F32], f"s5_out_bwd_{tag}", acc=(2,))
    dz = _mm(dt, w_glu, tb=True, name=f"s5_dz_{tag}", epi=_add_epi, epi_in=(dzd,))
    dw_glu = _mm(z, dt, ta=True, name=f"s5_dwglu_{tag}", pieces='row')

    def gelu_bwd(dzz, y, u, d):
        dy = dzz * _gelu_grad(y)
        return dy, dy * d, dy * u

    dy, du, dd = _rowwise(gelu_bwd, [dz, ytot, hn, p["d"]], [F32, F32, F32], f"s5_gelu_bwd_{tag}", acc=(2,))
    d_lr, d_li, d_bbr, d_bbi, d_cr, d_ci = [], [], [], [], [], []
    du = [du]
    for dirn in range(2):
        xp, mats, lam = dirs[dirn]
        starts = _s5_scan_bwd(dy, None, None, None, None, mats[2], mats[3], *lam, None, dirn == 0,
                              f"s5_starts_{tag}_{dirn}")
        dup, dbr, dbi, dcr, dci, dlr, dli = _s5_scan_bwd(dy, hn, xp, *mats, *lam, starts, dirn == 0,
                                                         f"s5_scan_bwd_{tag}_{dirn}")
        du.append(dup)
        d_lr.append(dlr)
        d_li.append(dli)
        d_bbr.append(_s5_blockdiag_b_t(dbr))
        d_bbi.append(_s5_blockdiag_b_t(dbi))
        d_cr.append(_s5_blockdiag_c_t(dcr))
        d_ci.append(_s5_blockdiag_c_t(dci))
    da_re, da_im, dls, db_re, db_im = _s5_disc_bwd(*raw, jnp.stack(d_lr), jnp.stack(d_li), jnp.stack(d_bbr),
                                                   jnp.stack(d_bbi), f"s5_disc_bwd_{tag}")
    dres, dg = _rmsnorm_bwd(x, norm_g, du, dres, f"s5_norm_bwd_{tag}")
    dres = tuple(_s5_unperm(t_) for t_ in dres)
    unb = lambda b: b.reshape(2, SSM_GROUP_CH, G, N).transpose(0, 2, 3, 1)
    grads = dict(a_re=da_re.reshape(2, G, N), a_im=da_im.reshape(2, G, N), log_step=dls.reshape(2, G, N).sum(-1),
                 b_re=unb(db_re), b_im=unb(db_im), c_re=jnp.stack(d_cr), c_im=jnp.stack(d_ci),
                 d=dd, w_glu=dw_glu, b_glu=db_glu, norm=dg)
    return dres, grads


def _adamw(w, g, m, v, name):
    R, C = w.shape
    tr = _pick(R, (512, 256, 128, 64, 32, 16, 8))
    tn = _pick(C, (512, 256, 128))

    def body(w_ref, g_ref, m_ref, v_ref, d_ref, nm_ref, nv_ref):
        gg = g_ref[...]
        m2 = ADAM_B1 * m_ref[...] + (1.0 - ADAM_B1) * gg
        v2 = ADAM_B2 * v_ref[...] + (1.0 - ADAM_B2) * (gg * gg)
        m_hat = m2 / (1.0 - ADAM_B1 ** ADAM_STEP)
        v_hat = v2 / (1.0 - ADAM_B2 ** ADAM_STEP)
        d_ref[...] = -ADAM_LR * (m_hat / (jnp.sqrt(v_hat) + ADAM_EPS) + ADAM_WD * w_ref[...])
        nm_ref[...] = m2
        nv_ref[...] = v2

    blk = pl.BlockSpec((tr, tn), lambda i, j: (i, j))
    shp = jax.ShapeDtypeStruct((R, C), F32)
    return pl.pallas_call(body, name=name, grid=(R // tr, C // tn), in_specs=[blk] * 4, out_specs=[blk] * 3,
                          out_shape=[shp] * 3, compiler_params=_cp(("parallel", "parallel")))(w, g, m, v)


def _add_selected(p, sel, others, name, also_bf16=False):
    K, _, M, C = p.shape
    tr = _pick(M, [t for t in (512, 256, 128, 64, 32, 16) if t * C * 4 <= 2 ** 21])
    n_o = len(others)

    def body(sel_ref, p_ref, *refs):
        acc = p_ref[...]
        for r in refs[:n_o]:
            acc = acc + r[...].astype(F32)
        refs[n_o][...] = acc
        if also_bf16:
            refs[n_o + 1][...] = acc.astype(BF16)

    blk = pl.BlockSpec((None, tr, C), lambda k, i, s: (k, i, 0))
    grid_spec = pltpu.PrefetchScalarGridSpec(
        num_scalar_prefetch=1, grid=(K, M // tr),
        in_specs=[pl.BlockSpec((None, None, tr, C), lambda k, i, s: (k, s[0], i, 0))] + [blk] * n_o,
        out_specs=[blk, blk] if also_bf16 else blk)
    shp = jax.ShapeDtypeStruct((K, M, C), F32)
    return pl.pallas_call(body, name=name, grid_spec=grid_spec,
                          out_shape=[shp, jax.ShapeDtypeStruct((K, M, C), BF16)] if also_bf16 else shp,
                          compiler_params=_cp(("parallel", "parallel")))(sel, p, *others)


def _sum_slots(a, own, me, name):
    n, R, C = a.shape
    tr = _pick(R, (512, 256, 128, 64, 32, 16, 8))

    def body(me_ref, a_ref, own_ref, o_ref):
        term = lambda k: jnp.where(me_ref[0] == k, own_ref[...], a_ref[k])
        acc = term(0)
        for k in range(1, n):
            acc = acc + term(k)
        o_ref[...] = acc

    grid_spec = pltpu.PrefetchScalarGridSpec(
        num_scalar_prefetch=1, grid=(R // tr,),
        in_specs=[pl.BlockSpec((n, tr, C), lambda i, s: (0, i, 0)), pl.BlockSpec((tr, C), lambda i, s: (i, 0))],
        out_specs=pl.BlockSpec((tr, C), lambda i, s: (i, 0)))
    return pl.pallas_call(body, name=name, grid_spec=grid_spec, out_shape=jax.ShapeDtypeStruct((R, C), F32),
                          compiler_params=_cp(("parallel",)))(me, a, own)


def _position():
    return lax.axis_index("x"), lax.axis_index("y"), lax.axis_index("c")


def _other_chips(x, y):
    return [(1 - x, y), (x, 1 - y), (1 - x, 1 - y)]


def _exchange_call(x, name):
    n_in, n_out = len(x.ins), len(x.out_shapes)

    def body(*refs):
        in_refs, out_refs, sems = refs[:n_in], refs[n_in:n_in + n_out], refs[n_in + n_out:]
        x.start(in_refs, out_refs, *sems)
        x.wait(in_refs, out_refs, *sems)

    hbm = pl.BlockSpec(memory_space=pltpu.HBM)
    return pl.pallas_call(body, name=name, in_specs=[hbm] * n_in, out_specs=[hbm] * n_out, out_shape=x.out_shapes,
                          scratch_shapes=x.sem_shapes())(*x.ins)


def _gather_chips(arrs):
    n = len(arrs)

    def plan(ins, outs):
        x, y, c = _position()
        me = 2 * x + y
        sends, recvs = [], []
        for px, py in _other_chips(x, y):
            for i in range(n):
                sends.append((ins[i], outs[i].at[me], (px, py, c)))
                recvs.append((outs[i].at[2 * px + py], (px, py, c)))
        return sends, recvs

    return _Exchange(arrs, [jax.ShapeDtypeStruct((4,) + a.shape, a.dtype) for a in arrs], 3 * n, plan)


def _gather_halves_ici(arrs):
    n = len(arrs)
    hr = [a.shape[1] // 2 for a in arrs]

    def plan(ins, outs):
        x, y, c = _position()
        me = 2 * x + y
        sends, recvs = [], []
        for px, py in _other_chips(x, y):
            for i in range(n):
                sends.append((ins[i].at[:, pl.ds(c * hr[i], hr[i])], outs[i].at[me, c], (px, py, c)))
                recvs.append((outs[i].at[2 * px + py, c], (px, py, c)))
        return sends, recvs

    shapes = [jax.ShapeDtypeStruct((N_CHIPS, 2, a.shape[0], a.shape[1] // 2, a.shape[2]), a.dtype) for a in arrs]
    return _Exchange(arrs, shapes, 3 * n, plan)


def _gather_halves_d2d(landed):
    n = len(landed)

    def plan(ins, outs):
        x, y, c = _position()
        sib = (x, y, 1 - c)
        sends, recvs = [], []
        for px, py in _other_chips(x, y):
            for i in range(n):
                sends.append((ins[i].at[2 * px + py, c], outs[i].at[2 * px + py, c], sib))
                recvs.append((outs[i].at[2 * px + py, 1 - c], sib))
        return sends, recvs

    return _Exchange(landed, [jax.ShapeDtypeStruct(a.shape, a.dtype) for a in landed], 3 * n, plan)


class _Reducer:
    def __init__(self, pieces, tags, sel_core, sel_chip):
        self.pieces, self.tags, self.sel_core, self.sel_chip = pieces, tags, sel_core, sel_chip

    def sibling(self):
        return _sibling_halves(self.pieces)

    def chips(self, from_sibling):
        self.pair = [_add_selected(p, self.sel_core, [r], f"reduce_add_pair_{t}", also_bf16=True)
                     for t, p, r in zip(self.tags, self.pieces, from_sibling)]
        return _scatter_chips([pb for _, pb in self.pair])

    def swap(self, from_chips):
        self.mine = [_add_selected(p.reshape((1,) + p.shape), self.sel_chip, [r[k:k + 1] for k in range(3)],
                                   f"reduce_add_chips_{t}")[0]
                     for t, (p, _), r in zip(self.tags, self.pair, from_chips)]
        return _sibling_swap(self.mine)

    def finish(self, theirs):
        return list(zip(self.mine, theirs))

    def run(self, tag):
        landed = _exchange_call(self.sibling(), f"reduce_sibling_{tag}")
        landed = _exchange_call(self.chips(landed), f"reduce_chips_{tag}")
        return self.finish(_exchange_call(self.swap(landed), f"reduce_join_{tag}"))


def _sibling_halves(pieces):
    n = len(pieces)

    def plan(ins, outs):
        x, y, c = _position()
        sib = (x, y, 1 - c)
        return [(ins[i].at[:, 1 - c], outs[i], sib) for i in range(n)], [(outs[i], sib) for i in range(n)]

    shapes = [jax.ShapeDtypeStruct((p.shape[0],) + p.shape[2:], p.dtype) for p in pieces]
    return _Exchange(pieces, shapes, n, plan)


def _scatter_chips(sums):
    n = len(sums)

    def plan(ins, outs):
        x, y, c = _position()
        sends, recvs = [], []
        for j, (px, py) in enumerate(_other_chips(x, y)):
            for i in range(n):
                sends.append((ins[i].at[2 * px + py], outs[i].at[j], (px, py, c)))
                recvs.append((outs[i].at[j], (px, py, c)))
        return sends, recvs

    return _Exchange(sums, [jax.ShapeDtypeStruct((3,) + s.shape[1:], s.dtype) for s in sums], 3 * n, plan)


def _sibling_swap(halves):
    n = len(halves)

    def plan(ins, outs):
        x, y, c = _position()
        sib = (x, y, 1 - c)
        return [(ins[i], outs[i], sib) for i in range(n)], [(outs[i], sib) for i in range(n)]

    return _Exchange(halves, [jax.ShapeDtypeStruct(h.shape, h.dtype) for h in halves], n, plan)


WEIGHTS = ['mix_norm', 'ffn_norm', 'final_norm', 'attn_w_qkv', 'attn_w_o', 'attn_sink', 'ssm_a_re', 'ssm_a_im',
           'ssm_log_step', 'ssm_b_re', 'ssm_b_im', 'ssm_c_re', 'ssm_c_im', 'ssm_d', 'ssm_w_glu', 'ssm_b_glu',
           'mla_w_dqkv', 'mla_q_norm', 'mla_kv_norm', 'mla_w_uq', 'mla_w_ukv', 'mla_w_o', 'ffn_w_up',
           'ffn_conv_w', 'ffn_conv_b', 'ffn_w_down']
BIG = dict(attn_w_qkv='col', attn_w_o='row', ssm_w_glu='row', mla_w_dqkv='row', mla_w_uq='col',
           mla_w_ukv='col', mla_w_o='row', ffn_w_up='col', ffn_w_down='row')
SMALL_SHARDED = ('mla_q_norm', 'mla_kv_norm', 'ffn_conv_w')
N_CHIPS = 4


def _assemble(over_ici, over_d2d, own, chip, core, kind):
    L = over_ici.shape[2]
    half = lax.broadcasted_iota(jnp.int32, (1, 2, 1, 1, 1), 1)
    g = jnp.where(half == core, over_ici, over_d2d)
    own_halves = own.reshape(L, 2, g.shape[3], g.shape[4]).transpose(1, 0, 2, 3)
    slot = lax.broadcasted_iota(jnp.int32, (N_CHIPS, 1, 1, 1, 1), 0)
    g = jnp.where(slot == chip, own_halves[None], g)
    if kind == 'row':
        return g.transpose(2, 0, 1, 3, 4).reshape(L, -1, g.shape[4])
    return g.transpose(2, 1, 3, 0, 4).reshape(L, 2 * g.shape[3], -1)


def _to_pieces(w, kind):
    L, R, C = w.shape
    if kind == 'col':
        t = w.reshape(L, 2, R // 2, N_CHIPS, C // N_CHIPS).transpose(3, 1, 0, 2, 4)
    else:
        t = w.reshape(L, N_CHIPS, R // N_CHIPS, 2, C // 2).transpose(1, 3, 0, 2, 4)
    return t.reshape(N_CHIPS, 2, L * t.shape[3], t.shape[4])


def _from_halves(h, kind, shard_shape):
    L = shard_shape[0]
    t = h.reshape(2, L, -1, h.shape[2])
    t = t.transpose(1, 0, 2, 3) if kind == 'col' else t.transpose(1, 2, 0, 3)
    return t.reshape(shard_shape)


def _pack(arrs):
    flat = jnp.concatenate([a.reshape(-1) for a in arrs])
    pad = (-flat.shape[0]) % (32 * LANES)
    return jnp.pad(flat, (0, pad)).reshape(-1, LANES)


def _unpack(packed, like):
    flat = packed.reshape(-1)
    out, off = [], 0
    for a in like:
        out.append(flat[off:off + a.size].reshape(a.shape))
        off += a.size
    return out


def kernel(x, mix_norm, ffn_norm, final_norm, attn_w_qkv, attn_w_o, attn_sink, ssm_a_re, ssm_a_im, ssm_log_step, ssm_b_re, ssm_b_im, ssm_c_re, ssm_c_im, ssm_d, ssm_w_glu, ssm_b_glu, mla_w_dqkv, mla_q_norm, mla_kv_norm, mla_w_uq, mla_w_ukv, mla_w_o, ffn_w_up, ffn_conv_w, ffn_conv_b, ffn_w_down, loss_target, m_mix_norm, m_ffn_norm, m_final_norm, m_attn_w_qkv, m_attn_w_o, m_attn_sink, m_ssm_a_re, m_ssm_a_im, m_ssm_log_step, m_ssm_b_re, m_ssm_b_im, m_ssm_c_re, m_ssm_c_im, m_ssm_d, m_ssm_w_glu, m_ssm_b_glu, m_mla_w_dqkv, m_mla_q_norm, m_mla_kv_norm, m_mla_w_uq, m_mla_w_ukv, m_mla_w_o, m_ffn_w_up, m_ffn_conv_w, m_ffn_conv_b, m_ffn_w_down, v_mix_norm, v_ffn_norm, v_final_norm, v_attn_w_qkv, v_attn_w_o, v_attn_sink, v_ssm_a_re, v_ssm_a_im, v_ssm_log_step, v_ssm_b_re, v_ssm_b_im, v_ssm_c_re, v_ssm_c_im, v_ssm_d, v_ssm_w_glu, v_ssm_b_glu, v_mla_w_dqkv, v_mla_q_norm, v_mla_kv_norm, v_mla_w_uq, v_mla_w_ukv, v_mla_w_o, v_ffn_w_up, v_ffn_conv_w, v_ffn_conv_b, v_ffn_w_down):
    args = locals()
    w = {n: args[n] for n in WEIGHTS}
    mom = {n: args["m_" + n] for n in WEIGHTS}
    var = {n: args["v_" + n] for n in WEIGHTS}
    depth = mix_norm.shape[0]
    xs = x[0]
    target = loss_target[0]
    chip = 2 * lax.axis_index("x") + lax.axis_index("y")
    core = lax.axis_index("c")

    big_names = list(BIG)
    w_bf = {n: w[n].astype(BF16) for n in big_names}
    mixer_weights = {0: ('attn_w_qkv', 'attn_w_o'), 1: ('ssm_w_glu',),
                     2: ('mla_w_dqkv', 'mla_w_uq', 'mla_w_ukv', 'mla_w_o')}

    def mixer_items(i):
        return [(n, i // 3) for n in mixer_weights[i % 3]]

    def ffn_items(i):
        return [('ffn_w_up', i), ('ffn_w_down', i)]

    def layer_items(i):
        return mixer_items(i) + ffn_items(i)

    def blocks(items):
        return [w_bf[n][l:l + 1] for n, l in items]

    full = {}

    def install(items, landed):
        for (n, l), a, b, own in zip(items, *landed, blocks(items)):
            full[n, l] = _assemble(a, b, own, chip, core, BIG[n])[0]

    over_ici = _exchange_call(_gather_halves_ici(blocks(mixer_items(0))), "gather_first_ici")
    install(mixer_items(0), (over_ici, _exchange_call(_gather_halves_d2d(over_ici), "gather_first_d2d")))
    small_sharded = _exchange_call(_gather_chips([w[n] for n in SMALL_SHARDED]), "gather_small_weights")
    for n, g in zip(SMALL_SHARDED, small_sharded):
        slot = lax.broadcasted_iota(jnp.int32, (N_CHIPS,) + (1,) * w[n].ndim, 0)
        g = jnp.where(slot == chip, w[n][None], g)
        full[n] = jnp.moveaxis(g, 0, -2).reshape(g.shape[1:-1] + (-1,))

    def ffn_args(i):
        return (ffn_norm[i:i + 1], full["ffn_w_up", i], full["ffn_conv_w"][i], ffn_conv_b[i:i + 1], full["ffn_w_down", i])

    def s5_params(j):
        return dict(a_re=ssm_a_re[j], a_im=ssm_a_im[j], log_step=ssm_log_step[j], b_re=ssm_b_re[j], b_im=ssm_b_im[j],
                    c_re=ssm_c_re[j], c_im=ssm_c_im[j], d=ssm_d[j:j + 1], b_glu=ssm_b_glu[j:j + 1])

    def mla_args(j):
        wd, wq, wkv = _mla_weights(full["mla_w_dqkv", j], full["mla_w_uq", j], full["mla_w_ukv", j])
        return (wd, wq, wkv, full["mla_w_o", j], full["mla_q_norm"][j:j + 1], full["mla_kv_norm"][j:j + 1])

    h = xs
    saved = []
    for i in range(depth):
        kind, j = i % 3, i // 3
        g = mix_norm[i:i + 1]
        if kind == 0:
            h, sm, landed = _attn_layer_fwd(h, g, full["attn_w_qkv", j], full["attn_w_o", j], attn_sink[j], f"{i}",
                                            next_blocks=blocks(ffn_items(0)) if i == 0 else None)
            if landed is not None:
                install(ffn_items(0), landed)
        elif kind == 1:
            h, sm = _s5_layer_fwd(h, g, s5_params(j), full["ssm_w_glu", j], f"{i}")
        else:
            h, sm = _mla_layer_fwd(h, g, *mla_args(j), f"{i}")
        nxt = layer_items(i + 1) if i + 1 < depth else None
        h, sf, landed = _ffn_fwd(h, *ffn_args(i), f"{i}", next_blocks=nxt and blocks(nxt))
        if landed is not None:
            install(nxt, landed)
        saved.append((sm, sf))
    loss_part, dres, d_final = _loss_head(h, final_norm.reshape(1, -1), target, "loss_head")

    gl = {n: [None] * w[n].shape[0] for n in WEIGHTS if n != 'final_norm'}
    sel_core = jnp.reshape(core, (1,)).astype(jnp.int32)
    sel_chip = jnp.reshape(chip, (1,)).astype(jnp.int32)
    reduced = {}

    def reducer_for(items):
        pieces = [gl[n][l] if gl[n][l].ndim == 4 else _to_pieces(gl[n][l][None], BIG[n]) for n, l in items]
        return _Reducer(pieces, [f"{n}_{l}" for n, l in items], sel_core, sel_chip)

    pending = None
    for i in reversed(range(depth)):
        kind, j = i % 3, i // 3
        sm, sf = saved[i]
        dres, (dg, dwu, dcw, dcb, dwd), done = _ffn_bwd(dres, sf, *ffn_args(i), f"{i}", reducer=pending)
        if pending is not None:
            reduced.update(zip(layer_items(i + 1), done))
        gl['ffn_norm'][i], gl['ffn_w_up'][i], gl['ffn_conv_w'][i] = dg[0], dwu, dcw
        gl['ffn_conv_b'][i], gl['ffn_w_down'][i] = dcb[0], dwd
        g = mix_norm[i:i + 1]
        if kind == 0:
            own_ffn = reducer_for(ffn_items(0)) if i == 0 else None
            dres, (dg, dwq, dwo, dsk), done = _attn_layer_bwd(dres, sm, g, full["attn_w_qkv", j], full["attn_w_o", j],
                                                              attn_sink[j], f"{i}", reducer=own_ffn)
            if own_ffn is not None:
                reduced.update(zip(ffn_items(0), done))
            gl['attn_w_qkv'][j], gl['attn_w_o'][j], gl['attn_sink'][j] = dwq, dwo, dsk
        elif kind == 1:
            dres, gs = _s5_layer_bwd(dres, sm, g, s5_params(j), full["ssm_w_glu", j], f"{i}")
            dg = gs['norm']
            for k in ('a_re', 'a_im', 'log_step', 'b_re', 'b_im', 'c_re', 'c_im'):
                gl['ssm_' + k][j] = gs[k]
            gl['ssm_d'][j], gl['ssm_b_glu'][j], gl['ssm_w_glu'][j] = gs['d'][0], gs['b_glu'][0], gs['w_glu']
        else:
            dres, (dg, dwd_, dqn, dkvn, dwuq, dwukv, dwo) = _mla_layer_bwd(dres, sm, g, *mla_args(j), f"{i}")
            gl['mla_w_dqkv'][j], gl['mla_q_norm'][j], gl['mla_kv_norm'][j] = dwd_, dqn[0], dkvn[0]
            gl['mla_w_uq'][j], gl['mla_w_ukv'][j], gl['mla_w_o'][j] = dwuq, dwukv, dwo
        gl['mix_norm'][i] = dg[0]
        pending = reducer_for(layer_items(i) if i > 0 else mixer_items(0))
    reduced.update(zip(mixer_items(0), pending.run("first_layer")))
    local = {n: jnp.stack(v) for n, v in gl.items() if n not in BIG}
    local['final_norm'] = d_final[0]

    grads = {}
    for n in big_names:
        shards = []
        for l in range(w[n].shape[0]):
            a, b = reduced[n, l]
            halves = jnp.where(core == 0, jnp.stack([a, b]), jnp.stack([b, a]))
            shards.append(_from_halves(halves, BIG[n], (1,) + w[n].shape[1:]))
        grads[n] = jnp.concatenate(shards, axis=0)

    small_names = [n for n in WEIGHTS if n not in BIG]
    packed = _pack([local[n] for n in small_names])
    halves = packed.reshape(1, 2, packed.shape[0] // 2, LANES)
    (from_sibling,) = _exchange_call(_sibling_halves([halves]), "reduce_small_sibling")
    pair = _add_selected(halves, sel_core, [from_sibling], "reduce_small_pair")[0]
    (from_chips,) = _exchange_call(_gather_chips([pair]), "reduce_small_chips")
    total = _sum_slots(from_chips, pair, sel_chip, "reduce_small_sum")
    (theirs,) = _exchange_call(_sibling_swap([total]), "reduce_small_join")
    summed = jnp.where(core == 0, jnp.concatenate([total, theirs]), jnp.concatenate([theirs, total]))
    for n, gsum in zip(small_names, _unpack(summed, [local[n] for n in small_names])):
        if n in SMALL_SHARDED:
            width = w[n].shape[-1]
            gsum = lax.dynamic_slice_in_dim(gsum, chip * width, width, axis=gsum.ndim - 1)
        grads[n] = gsum

    delta, new_m, new_v = {}, {}, {}
    for n in big_names:
        two_d = lambda a: a.reshape(-1, a.shape[-1])
        d_, m_, v_ = _adamw(two_d(w[n]), two_d(grads[n]), two_d(mom[n]), two_d(var[n]), f"adamw_{n}")
        delta[n], new_m[n], new_v[n] = (t.reshape(w[n].shape) for t in (d_, m_, v_))
    outs = _adamw(*[_pack([src[n] for n in small_names]) for src in (w, grads, mom, var)], "adamw_small")
    for dst, packed in zip((delta, new_m, new_v), outs):
        for n, t in zip(small_names, _unpack(packed, [w[n] for n in small_names])):
            dst[n] = t

    loss = lax.psum(loss_part[0, 0], ("x", "y", "c"))
    return (loss, dres[0][None], *[grads[n] for n in WEIGHTS], *[delta[n] for n in WEIGHTS],
            *[new_m[n] for n in WEIGHTS], *[new_v[n] for n in WEIGHTS])
```
